```python
import math
import jax, jax.numpy as jnp
from jax import lax
import numpy as np

D_MODEL = 1024
BATCH = 32
SEQ = 256
DEPTH = 2
DEC_BATCH = 8
DEC_SEQ = 4096
PAST_LEN = 256

GRID_W = 64
HEAD_DIM = 64
A_HEADS = 4
A_KV_HEADS = 2
B_HEADS = 4
B_KV_HEADS = 2
C_HEADS = 4
D_HEADS = 4
BRANCH_W = 256
N_BRANCH = 4
WINDOW = 128
BLOCK = 128
ROPE_BASE = 10000.0
HGRN_CHUNK = 32
DELTA_CHUNK = 64
CONV_K = 5
D_FF = 2816
N_EXPERTS = 8
TOP_K = 2
D_FF_EXPERT = 3584
N_DENSE = (DEPTH + 1) // 2
N_MOE = DEPTH // 2
EPS = 1e-6
NEG_INF = -1e30
SPLITS = (A_HEADS * HEAD_DIM, A_KV_HEADS * HEAD_DIM, A_KV_HEADS * HEAD_DIM,
          B_HEADS * HEAD_DIM, B_KV_HEADS * HEAD_DIM, B_KV_HEADS * HEAD_DIM,
          BRANCH_W, BRANCH_W, BRANCH_W, BRANCH_W, BRANCH_W,
          3 * BRANCH_W, 2 * D_HEADS, 2 * D_HEADS, BRANCH_W,
          N_BRANCH * D_MODEL)
W_IN_COLS = sum(SPLITS)
F32 = jnp.float32

kernel_name = 'hybrid_diffusion_prefix_trunk_step'


def rms_norm(x, g):
    x32 = x.astype(F32)
    y = x32 * lax.rsqrt(jnp.mean(x32 * x32, axis=-1, keepdims=True) + EPS)
    return (y * g.astype(F32)).astype(x.dtype)


def l2_normalize(x):
    x32 = x.astype(F32)
    return x32 * lax.rsqrt(jnp.sum(x32 * x32, axis=-1, keepdims=True) + EPS)


def flip(t):
    return jnp.flip(t, axis=1)


def split_columns(z):
    idx = [int(i) for i in np.cumsum(SPLITS)[:-1]]
    return jnp.split(z, idx, axis=-1)


def ada_modulation(cond, w, b):
    m = jax.nn.silu(cond) @ w + b
    return jnp.split(m[:, None, :], 6, axis=-1)


def rope_tables(L, dtype):
    rows = L // GRID_W
    row = jnp.repeat(jnp.arange(rows, dtype=F32), GRID_W)
    col = jnp.tile(jnp.arange(GRID_W, dtype=F32), rows)
    n_freq = HEAD_DIM // 4
    inv = ROPE_BASE ** (-jnp.arange(n_freq, dtype=F32) / n_freq)
    ang = jnp.stack([row, col], 0)[:, :, None] * inv
    return jnp.cos(ang).astype(dtype), jnp.sin(ang).astype(dtype)


def apply_axial_rope(x, tables):
    cos, sin = tables
    half = HEAD_DIM // 2
    nf = HEAD_DIM // 4
    outs = []
    for a in range(2):
        xa = x[..., a * half:(a + 1) * half]
        x1, x2 = xa[..., :nf], xa[..., nf:]
        ca = cos[a][None, :, None, :]
        sa = sin[a][None, :, None, :]
        outs += [x1 * ca - x2 * sa, x2 * ca + x1 * sa]
    return jnp.concatenate(outs, axis=-1)


def dense_gqa_blocked(q, k, v, sink):
    Bn, Lq, Hq, d = q.shape
    Hkv = k.shape[2]
    G = Hq // Hkv
    nb = Lq // BLOCK
    scale = d ** -0.5
    qb = q.reshape(Bn, nb, BLOCK, Hkv, G, d).transpose(1, 0, 2, 3, 4, 5)

    def one(qblk):
        s = jnp.einsum('bqhgd,bkhd->bhgqk', qblk, k).astype(F32) * scale
        if sink is not None:
            sl = jnp.broadcast_to(sink.astype(F32).reshape(Hkv, G)[None, :, :, None, None], s.shape[:-1] + (1,))
            p = jax.nn.softmax(jnp.concatenate([s, sl], axis=-1), axis=-1)[..., :-1]
        else:
            p = jax.nn.softmax(s, axis=-1)
        return jnp.einsum('bhgqk,bkhd->bqhgd', p.astype(v.dtype), v)

    o = lax.map(one, qb)
    return o.transpose(1, 0, 2, 3, 4, 5).reshape(Bn, Lq, Hq * d)


def windowed_gqa_with_ctx(q, k, v, kc, vc, sink):
    Bn, L, Hq, d = q.shape
    Hkv = k.shape[2]
    G = Hq // Hkv
    nb = L // BLOCK
    scale = d ** -0.5
    qb = q.reshape(Bn, nb, BLOCK, Hkv, G, d)

    def band(t):
        tb = t.reshape(Bn, nb, BLOCK, Hkv, d)
        z = jnp.zeros_like(tb[:, :1])
        prev = jnp.concatenate([z, tb[:, :-1]], axis=1)
        nxt = jnp.concatenate([tb[:, 1:], z], axis=1)
        return jnp.concatenate([prev, tb, nxt], axis=2)

    kband, vband = band(k), band(v)
    s_loc = jnp.einsum('bnqhgd,bnkhd->bnhgqk', qb, kband).astype(F32) * scale
    s_ctx = jnp.einsum('bnqhgd,bkhd->bnhgqk', qb, kc).astype(F32) * scale
    qpos = jnp.arange(nb)[:, None, None] * BLOCK + jnp.arange(BLOCK)[None, :, None]
    kpos = jnp.arange(nb)[:, None, None] * BLOCK - BLOCK + jnp.arange(3 * BLOCK)[None, None, :]
    mask = (jnp.abs(kpos - qpos) <= WINDOW) & (kpos >= 0) & (kpos < L)
    s_loc = jnp.where(mask[None, :, None, None], s_loc, NEG_INF)
    sl = jnp.broadcast_to(sink.astype(F32).reshape(Hkv, G)[None, None, :, :, None, None], s_loc.shape[:-1] + (1,))
    p = jax.nn.softmax(jnp.concatenate([s_loc, s_ctx, sl], axis=-1), axis=-1)
    nk = 3 * BLOCK
    lc = kc.shape[1]
    p_loc = p[..., :nk].astype(v.dtype)
    p_ctx = p[..., nk:nk + lc].astype(v.dtype)
    o = (jnp.einsum('bnhgqk,bnkhd->bnqhgd', p_loc, vband)
         + jnp.einsum('bnhgqk,bkhd->bnqhgd', p_ctx, vc))
    return o.reshape(Bn, L, Hq * d)


def gla_chunked(q, k, v, logf, s0):
    Bn, L, H, _ = q.shape
    C = HGRN_CHUNK
    n = L // C
    ch = lambda t: t.astype(F32).reshape(Bn, n, C, H, t.shape[-1]).transpose(1, 0, 3, 2, 4)
    q, k, v, logf = ch(q), ch(k), ch(v), ch(logf)
    G = jnp.cumsum(logf, axis=3)
    G_last = G[:, :, :, -1:, :]
    q_in = q * jnp.exp(G)
    k_in = k * jnp.exp(-G)
    k_out = k * jnp.exp(G_last - G)
    causal = jnp.tril(jnp.ones((C, C), bool))
    A = jnp.where(causal, jnp.einsum('nbhtd,nbhsd->nbhts', q_in, k_in), 0.0)
    o_intra = jnp.einsum('nbhts,nbhsv->nbhtv', A, v)

    def step(S, xs):
        qi, ko, vi, oi, gl = xs
        o = oi + jnp.einsum('bhtd,bhdv->bhtv', qi, S)
        S = jnp.exp(gl)[:, :, 0, :, None] * S + jnp.einsum('bhsd,bhsv->bhdv', ko, vi)
        return S, o

    S, o = lax.scan(step, s0.astype(F32), (q_in, k_out, v, o_intra, G_last))
    return o.transpose(1, 0, 3, 2, 4).reshape(Bn, L, H, -1), S


def delta_chunked(q, k, v, log_a, beta, s0):
    Bn, L, H, _ = q.shape
    C = DELTA_CHUNK
    n = L // C
    ch = lambda t: t.astype(F32).reshape(Bn, n, C, H, t.shape[-1]).transpose(1, 0, 3, 2, 4)
    chs = lambda t: t.astype(F32).reshape(Bn, n, C, H).transpose(1, 0, 3, 2)
    q, k, v = ch(q), ch(k), ch(v)
    log_a, beta = chs(log_a), chs(beta)
    g = jnp.cumsum(log_a, axis=-1)
    kb = k * beta[..., None]
    vb = v * beta[..., None]
    incl = jnp.tril(jnp.ones((C, C), bool))
    strict = jnp.tril(jnp.ones((C, C), bool), -1)
    decay = jnp.exp(jnp.where(incl, g[..., :, None] - g[..., None, :], -jnp.inf))
    Lm = jnp.where(strict, jnp.einsum('nbhid,nbhjd->nbhij', kb, k) * decay, 0.0)
    eye = jnp.eye(C, dtype=F32)
    T = lax.linalg.triangular_solve(eye + Lm, jnp.broadcast_to(eye, Lm.shape),
                                    left_side=True, lower=True, unit_diagonal=True)
    u = T @ vb
    w = T @ (kb * jnp.exp(g)[..., None])
    attn = jnp.einsum('nbhid,nbhjd->nbhij', q, k) * decay
    q_g = q * jnp.exp(g)[..., None]
    g_last = g[..., -1]
    k_end = k * jnp.exp(g_last[..., None] - g)[..., None]

    def step(S, xs):
        u_c, w_c, qg_c, at_c, ke_c, gl_c = xs
        v_new = u_c - w_c @ S
        o = qg_c @ S + at_c @ v_new
        S = jnp.exp(gl_c)[..., None, None] * S + jnp.einsum('bhcd,bhcv->bhdv', ke_c, v_new)
        return S, o

    S, o = lax.scan(step, s0.astype(F32), (u, w, q_g, attn, k_end, g_last))
    return o.transpose(1, 0, 3, 2, 4).reshape(Bn, L, H, -1), S


def hgrn2_mixer(q, zf_f, zf_b, i, g, lb, onorm_g, s0):
    Bn, L, _ = q.shape
    hd = lambda t: t.reshape(Bn, L, C_HEADS, HEAD_DIM)
    lb = lb.reshape(C_HEADS, HEAD_DIM)

    def gates(zf):
        zf = hd(zf).astype(F32)
        f = lb + (1.0 - lb) * jax.nn.sigmoid(zf)
        return jnp.log(f), (1.0 - lb) * jax.nn.sigmoid(-zf)

    q, i = hd(q), hd(i)
    logf_f, k_f = gates(zf_f)
    logf_b, k_b = gates(zf_b)
    o_f, s_f = gla_chunked(q, k_f, i, logf_f, s0[:, 0])
    o_b, s_b = gla_chunked(flip(q), flip(k_b), flip(i), flip(logf_b), s0[:, 1])
    o = rms_norm(o_f + flip(o_b), onorm_g).reshape(Bn, L, BRANCH_W)
    return (o * jax.nn.silu(g.astype(F32))).astype(g.dtype), jnp.stack([s_f, s_b], axis=1)


def centred_conv(x, w):
    pad = (CONV_K - 1) // 2
    return lax.conv_general_dilated(x, w[:, None, :].astype(x.dtype), window_strides=(1,),
                                    padding=[(pad, pad)], dimension_numbers=('NWC', 'WIO', 'NWC'),
                                    feature_group_count=x.shape[-1])


def gated_delta_mixer(qkv, za, zb, g, conv_w, a_log, dt_bias, onorm_g, s0):
    Bn, L, _ = qkv.shape
    qkv = jax.nn.silu(centred_conv(qkv, conv_w))
    q, k, v = [t.reshape(Bn, L, D_HEADS, HEAD_DIM) for t in jnp.split(qkv, 3, axis=-1)]
    q = l2_normalize(q) * HEAD_DIM ** -0.5
    k = l2_normalize(k)
    za = za.astype(F32).reshape(Bn, L, 2, D_HEADS)
    zb = zb.astype(F32).reshape(Bn, L, 2, D_HEADS)
    log_alpha = -jnp.exp(a_log.astype(F32)) * jax.nn.softplus(za + dt_bias.astype(F32))
    beta = jax.nn.sigmoid(zb)
    o_f, s_f = delta_chunked(q, k, v, log_alpha[:, :, 0], beta[:, :, 0], s0[:, 0])
    o_b, s_b = delta_chunked(flip(q), flip(k), flip(v), flip(log_alpha[:, :, 1]), flip(beta[:, :, 1]), s0[:, 1])
    o = rms_norm(o_f + flip(o_b), onorm_g).reshape(Bn, L, BRANCH_W)
    return (o * jax.nn.silu(g.astype(F32))).astype(g.dtype), jnp.stack([s_f, s_b], axis=1)


def token_mixer(h, lw, ctx):
    Bn, L, _ = h.shape
    (aq, ak, av, bq, bk, bv, cq, cf_f, cf_b, ci, cg, dqkv, da, db, dg, gl) = split_columns(h @ lw['w_in'])
    heads = lambda t: t.reshape(Bn, L, -1, HEAD_DIM)
    aq, ak, av, bv = heads(aq), heads(ak), heads(av), heads(bv)
    bq = rms_norm(heads(bq), lw['b_qnorm_g'])
    bk = rms_norm(heads(bk), lw['b_knorm_g'])
    if ctx is None:
        s_c0 = jnp.zeros((Bn, 2, C_HEADS, HEAD_DIM, HEAD_DIM), F32)
        s_d0 = jnp.zeros((Bn, 2, D_HEADS, HEAD_DIM, HEAD_DIM), F32)
        o_a = dense_gqa_blocked(aq, ak, av, lw['a_sink'])
        o_b = dense_gqa_blocked(bq, bk, bv, None)
    else:
        ka, va, kbc, vbc, s_c0, s_d0 = ctx
        tabs = rope_tables(L, h.dtype)
        o_a = windowed_gqa_with_ctx(apply_axial_rope(aq, tabs), apply_axial_rope(ak, tabs), av,
                                    ka.astype(h.dtype), va.astype(h.dtype), lw['a_sink'])
        o_b = dense_gqa_blocked(apply_axial_rope(bq, tabs),
                                jnp.concatenate([apply_axial_rope(bk, tabs), kbc.astype(h.dtype)], axis=1),
                                jnp.concatenate([bv, vbc.astype(h.dtype)], axis=1), None)
    o_c, s_c = hgrn2_mixer(cq, cf_f, cf_b, ci, cg, lw['lb'], lw['c_onorm_g'], s_c0)
    o_d, s_d = gated_delta_mixer(dqkv, da, db, dg, lw['d_conv'], lw['d_a_log'], lw['d_dt_bias'],
                                 lw['d_onorm_g'], s_d0)
    gates = jax.nn.sigmoid(gl.astype(F32)).astype(h.dtype).reshape(Bn, L, N_BRANCH, D_MODEL)
    branches = (o_a.astype(h.dtype), o_b.astype(h.dtype), o_c.astype(h.dtype), o_d.astype(h.dtype))
    merged = sum(gates[:, :, j] * (branches[j] @ lw['w_branch'][j]) for j in range(N_BRANCH))
    return merged @ lw['w_out'], (ak, av, bk, bv, s_c, s_d)


def swiglu(h, w1, w3, w2):
    return (jax.nn.silu(h @ w1) * (h @ w3)) @ w2


def moe_swiglu(h, router_w, router_b, w1, w3, w2):
    logits = (h @ router_w).astype(F32) + router_b.astype(F32)
    top_l, top_i = lax.top_k(logits, TOP_K)
    top_w = jax.nn.softmax(top_l, axis=-1)
    y = jnp.zeros_like(h)
    for e in range(N_EXPERTS):
        we = jnp.sum(jnp.where(top_i == e, top_w, 0.0), axis=-1, keepdims=True).astype(h.dtype)
        y = y + we * swiglu(h, w1[e], w3[e], w2[e])
    return y


def setup_inputs(seed: int = 0) -> dict:
    key = jax.random.key(seed)
    ks = iter(list(jax.random.split(key, 48)))
    nrm = lambda shape, s=1.0: jax.random.normal(next(ks), shape, F32) * s
    gain = lambda shape: 1.0 + nrm(shape, 0.02)
    D = D_MODEL
    dt = jnp.exp(jax.random.uniform(next(ks), (DEPTH, 2, D_HEADS), F32, math.log(1e-3), math.log(1e-1)))
    a_log = jnp.log(jax.random.uniform(next(ks), (DEPTH, 2, D_HEADS), F32, 1.0, 16.0))
    return {
        'x_prompt': nrm((BATCH, SEQ, D)),
        'x_sample': nrm((DEC_BATCH, DEC_SEQ, D)),
        'cache_attn_a_k': nrm((DEC_BATCH, DEPTH, PAST_LEN, A_KV_HEADS, HEAD_DIM)),
        'cache_attn_a_v': nrm((DEC_BATCH, DEPTH, PAST_LEN, A_KV_HEADS, HEAD_DIM)),
        'cache_attn_b_k': nrm((DEC_BATCH, DEPTH, PAST_LEN, B_KV_HEADS, HEAD_DIM)),
        'cache_attn_b_v': nrm((DEC_BATCH, DEPTH, PAST_LEN, B_KV_HEADS, HEAD_DIM)),
        'state_hgrn': nrm((DEC_BATCH, DEPTH, 2, C_HEADS, HEAD_DIM, HEAD_DIM), 0.3),
        'state_delta': nrm((DEC_BATCH, DEPTH, 2, D_HEADS, HEAD_DIM, HEAD_DIM), 0.3),
        'c': nrm((DEC_BATCH, D)),
        'c_ctx': nrm((D,)),
        'norm1_g': gain((DEPTH, D)),
        'norm2_g': gain((DEPTH, D)),
        'w_ada': nrm((DEPTH, D, 6 * D), 0.5 * D ** -0.5),
        'b_ada': nrm((DEPTH, 6 * D), 0.02),
        'w_in': nrm((DEPTH, D, W_IN_COLS), D ** -0.5),
        'a_sink': nrm((DEPTH, A_HEADS)),
        'b_qnorm_g': gain((DEPTH, HEAD_DIM)),
        'b_knorm_g': gain((DEPTH, HEAD_DIM)),
        'c_lb': nrm((DEPTH, BRANCH_W), 0.5),
        'c_onorm_g': gain((DEPTH, HEAD_DIM)),
        'd_conv': nrm((DEPTH, CONV_K, 3 * BRANCH_W), CONV_K ** -0.5),
        'd_a_log': a_log,
        'd_dt_bias': dt + jnp.log(-jnp.expm1(-dt)),
        'd_onorm_g': gain((DEPTH, HEAD_DIM)),
        'w_branch': nrm((DEPTH, N_BRANCH, BRANCH_W, D), BRANCH_W ** -0.5),
        'w_out': nrm((DEPTH, D, D), D ** -0.5),
        'ffn_w1': nrm((N_DENSE, D, D_FF), D ** -0.5),
        'ffn_w3': nrm((N_DENSE, D, D_FF), D ** -0.5),
        'ffn_w2': nrm((N_DENSE, D_FF, D), D_FF ** -0.5),
        'router_w': nrm((N_MOE, D, N_EXPERTS), D ** -0.5),
        'router_b': nrm((N_MOE, N_EXPERTS), 0.01),
        'moe_w1': nrm((N_MOE, N_EXPERTS, D, D_FF_EXPERT), D ** -0.5),
        'moe_w3': nrm((N_MOE, N_EXPERTS, D, D_FF_EXPERT), D ** -0.5),
        'moe_w2': nrm((N_MOE, N_EXPERTS, D_FF_EXPERT, D), D_FF_EXPERT ** -0.5),
        'final_norm_g': gain((D,)),
    }


def reference(x_prompt, x_sample, cache_attn_a_k, cache_attn_a_v, cache_attn_b_k, cache_attn_b_v,
              state_hgrn, state_delta, c, c_ctx, norm1_g, norm2_g, w_ada, b_ada, w_in, a_sink,
              b_qnorm_g, b_knorm_g, c_lb, c_onorm_g, d_conv, d_a_log, d_dt_bias, d_onorm_g,
              w_branch, w_out, ffn_w1, ffn_w3, ffn_w2, router_w, router_b, moe_w1, moe_w3, moe_w2,
              final_norm_g):
    cum = jnp.cumsum(jax.nn.softmax(c_lb.astype(F32), axis=0), axis=0)
    lower_bounds = cum - cum[:1]

    def layer(l, x, cond, ctx):
        lw = {'w_in': w_in[l], 'a_sink': a_sink[l], 'b_qnorm_g': b_qnorm_g[l], 'b_knorm_g': b_knorm_g[l],
              'lb': lower_bounds[l], 'c_onorm_g': c_onorm_g[l], 'd_conv': d_conv[l], 'd_a_log': d_a_log[l],
              'd_dt_bias': d_dt_bias[l], 'd_onorm_g': d_onorm_g[l], 'w_branch': w_branch[l], 'w_out': w_out[l]}
        shift1, scale1, gate1, shift2, scale2, gate2 = ada_modulation(cond, w_ada[l], b_ada[l])
        h = rms_norm(x, norm1_g[l]) * (1 + scale1) + shift1
        mix, ctx_out = token_mixer(h, lw, ctx)
        x = x + gate1 * mix
        h = rms_norm(x, norm2_g[l]) * (1 + scale2) + shift2
        j = l // 2
        if l % 2 == 0:
            f = swiglu(h, ffn_w1[j], ffn_w3[j], ffn_w2[j])
        else:
            f = moe_swiglu(h, router_w[j], router_b[j], moe_w1[j], moe_w3[j], moe_w2[j])
        return x + gate2 * f, ctx_out

    xp = x_prompt
    cond_ctx = c_ctx[None, :]
    aks, avs, bks, bvs, scs, sds = [], [], [], [], [], []
    for l in range(DEPTH):
        xp, (ak, av, bk, bv, sc, sd) = layer(l, xp, cond_ctx, None)
        aks.append(ak); avs.append(av); bks.append(bk); bvs.append(bv); scs.append(sc); sds.append(sd)
    y_prompt = rms_norm(xp, final_norm_g)
    new_attn_a_k = jnp.stack(aks, axis=1)
    new_attn_a_v = jnp.stack(avs, axis=1)
    new_attn_b_k = jnp.stack(bks, axis=1)
    new_attn_b_v = jnp.stack(bvs, axis=1)
    new_state_hgrn = jnp.stack(scs, axis=1)
    new_state_delta = jnp.stack(sds, axis=1)

    xs = x_sample
    for l in range(DEPTH):
        ctx = (cache_attn_a_k[:, l], cache_attn_a_v[:, l], cache_attn_b_k[:, l], cache_attn_b_v[:, l],
               state_hgrn[:, l], state_delta[:, l])
        xs, _ = layer(l, xs, c, ctx)
    y_sample = rms_norm(xs, final_norm_g)

    return (y_prompt, y_sample, new_attn_a_k, new_attn_a_v, new_attn_b_k, new_attn_b_v, new_state_hgrn, new_state_delta)
```

```python
import functools
import math

import jax
import jax.numpy as jnp
import numpy as np
from jax import lax
from jax.experimental import pallas as pl
from jax.experimental.pallas import tpu as pltpu

F32 = jnp.float32
BF16 = jnp.bfloat16

D_MODEL = 1024
BATCH = 32
SEQ = 256
DEPTH = 2
DEC_BATCH = 8
DEC_SEQ = 4096
PAST_LEN = 256
GRID_W = 64
HEAD_DIM = 64
A_HEADS = 4
A_KV_HEADS = 2
B_HEADS = 4
B_KV_HEADS = 2
C_HEADS = 4
D_HEADS = 4
BRANCH_W = 256
N_BRANCH = 4
WINDOW = 128
BLOCK = 128
ROPE_BASE = 10000.0
HGRN_CHUNK = 32
DELTA_CHUNK = 64
CONV_K = 5
D_FF = 2816
N_EXPERTS = 8
TOP_K = 2
D_FF_EXPERT = 3584
EPS = 1e-6
NEG_INF = -1e30
F32_MIN = float(np.finfo(np.float32).min)

CTX_TOK = BATCH * SEQ
LAT_TOK = DEC_BATCH * DEC_SEQ
N_TOK = CTX_TOK + LAT_TOK
N_COND = 1 + DEC_BATCH

Z_MAIN = 3072
Z_DAB = Z_MAIN
Z_DG = Z_MAIN + 128
Z_COLS = Z_DG + BRANCH_W
W_IN_MIX = 3344

TM = 512
VMEM_LIMIT = 56 * 1024 * 1024


def _tile_cond(i, tm):
    ctx_tiles = CTX_TOK // tm
    per_b = DEC_SEQ // tm
    return jnp.where(i < ctx_tiles, 0, 1 + (i - ctx_tiles) // per_b)


def _rms(x, g):
    return x * lax.rsqrt(jnp.mean(x * x, axis=-1, keepdims=True) + EPS) * g


def _params(sem):
    return pltpu.CompilerParams(dimension_semantics=sem, vmem_limit_bytes=VMEM_LIMIT)


def _ada_kernel(c_ref, w_ref, b_ref, o_ref):
    c = c_ref[...]
    s = c * jax.nn.sigmoid(c)
    o_ref[...] = jnp.dot(s.astype(BF16), w_ref[...].astype(BF16), preferred_element_type=F32) + b_ref[...]


def ada_modulation(cond_pad, w, b):
    n = 6 * D_MODEL
    tn = 1536
    return pl.pallas_call(
        _ada_kernel,
        grid=(n // tn,),
        in_specs=[pl.BlockSpec((16, D_MODEL), lambda j: (0, 0)),
                  pl.BlockSpec((D_MODEL, tn), lambda j: (0, j)),
                  pl.BlockSpec((1, tn), lambda j: (0, j))],
        out_specs=pl.BlockSpec((16, tn), lambda j: (0, j)),
        out_shape=jax.ShapeDtypeStruct((16, n), F32),
        compiler_params=_params(("arbitrary",)),
        name="ada_modulation",
    )(cond_pad, w, b.reshape(1, n))


def _in_kernel(x_ref, mod_ref, g_ref, w_ref, z_ref):
    h = _rms(x_ref[...], g_ref[...]) * (1.0 + mod_ref[1:2, :]) + mod_ref[0:1, :]
    z_ref[...] = jnp.dot(h.astype(BF16), w_ref[...], preferred_element_type=F32)


def input_projection(x, mod, g, w):
    nt = N_TOK // TM
    return pl.pallas_call(
        _in_kernel,
        grid=(nt,),
        in_specs=[pl.BlockSpec((TM, D_MODEL), lambda i: (i, 0)),
                  pl.BlockSpec((None, 6, D_MODEL), lambda i: (_tile_cond(i, TM), 0, 0)),
                  pl.BlockSpec((1, D_MODEL), lambda i: (0, 0)),
                  pl.BlockSpec((D_MODEL, Z_COLS), lambda i: (0, 0))],
        out_specs=pl.BlockSpec((TM, Z_COLS), lambda i: (i, 0)),
        out_shape=jax.ShapeDtypeStruct((N_TOK, Z_COLS), F32),
        compiler_params=_params(("arbitrary",)),
        name="input_projection",
    )(x, mod, g.reshape(1, D_MODEL), w)


def _merge_kernel(x_ref, mod_ref, g_ref, o_ref, wgl_ref, wbr_ref, wout_ref, xo_ref):
    x = x_ref[...]
    h = (_rms(x, g_ref[...]) * (1.0 + mod_ref[1:2, :]) + mod_ref[0:1, :]).astype(BF16)
    merged = None
    for j in range(N_BRANCH):
        gate = jax.nn.sigmoid(jnp.dot(h, wgl_ref[j], preferred_element_type=F32))
        br = jnp.dot(o_ref[:, j * BRANCH_W:(j + 1) * BRANCH_W].astype(BF16), wbr_ref[j],
                     preferred_element_type=F32)
        merged = gate * br if merged is None else merged + gate * br
    mix = jnp.dot(merged.astype(BF16), wout_ref[...], preferred_element_type=F32)
    xo_ref[...] = x + mod_ref[2:3, :] * mix


def merge_projection(x, mod, g, o, wgl, wbr, wout):
    nt = N_TOK // TM
    return pl.pallas_call(
        _merge_kernel,
        grid=(nt,),
        in_specs=[pl.BlockSpec((TM, D_MODEL), lambda i: (i, 0)),
                  pl.BlockSpec((None, 6, D_MODEL), lambda i: (_tile_cond(i, TM), 0, 0)),
                  pl.BlockSpec((1, D_MODEL), lambda i: (0, 0)),
                  pl.BlockSpec((TM, N_BRANCH * BRANCH_W), lambda i: (i, 0)),
                  pl.BlockSpec((N_BRANCH, D_MODEL, D_MODEL), lambda i: (0, 0, 0)),
                  pl.BlockSpec((N_BRANCH, BRANCH_W, D_MODEL), lambda i: (0, 0, 0)),
                  pl.BlockSpec((D_MODEL, D_MODEL), lambda i: (0, 0))],
        out_specs=pl.BlockSpec((TM, D_MODEL), lambda i: (i, 0)),
        out_shape=jax.ShapeDtypeStruct((N_TOK, D_MODEL), F32),
        compiler_params=_params(("arbitrary",)),
        name="merge_projection",
    )(x, mod, g.reshape(1, D_MODEL), o, wgl, wbr, wout)


def _ffn_kernel(x_ref, mod_ref, g_ref, w1_ref, w3_ref, w2_ref, xo_ref):
    x = x_ref[...]
    h = (_rms(x, g_ref[...]) * (1.0 + mod_ref[4:5, :]) + mod_ref[3:4, :]).astype(BF16)
    a = jnp.dot(h, w1_ref[...], preferred_element_type=F32)
    b = jnp.dot(h, w3_ref[...], preferred_element_type=F32)
    hid = (a * jax.nn.sigmoid(a) * b).astype(BF16)
    f = jnp.dot(hid, w2_ref[...], preferred_element_type=F32)
    xo_ref[...] = x + mod_ref[5:6, :] * f


def dense_ffn(x, mod, g, w1, w3, w2):
    nt = N_TOK // TM
    const = lambda i: (0, 0)
    return pl.pallas_call(
        _ffn_kernel,
        grid=(nt,),
        in_specs=[pl.BlockSpec((TM, D_MODEL), lambda i: (i, 0)),
                  pl.BlockSpec((None, 6, D_MODEL), lambda i: (_tile_cond(i, TM), 0, 0)),
                  pl.BlockSpec((1, D_MODEL), const),
                  pl.BlockSpec((D_MODEL, D_FF), const, pipeline_mode=pl.Buffered(1)),
                  pl.BlockSpec((D_MODEL, D_FF), const, pipeline_mode=pl.Buffered(1)),
                  pl.BlockSpec((D_FF, D_MODEL), const, pipeline_mode=pl.Buffered(1))],
        out_specs=pl.BlockSpec((TM, D_MODEL), lambda i: (i, 0)),
        out_shape=jax.ShapeDtypeStruct((N_TOK, D_MODEL), F32),
        compiler_params=_params(("arbitrary",)),
        name="dense_ffn",
    )(x, mod, g.reshape(1, D_MODEL), w1, w3, w2)


MOE_TM = 1024
MOE_TF = 896
ROUTER_LANES = 128


def _moe_kernel(x_ref, mod_ref, g_ref, rw_ref, rb_ref, w1_ref, w3_ref, w2_ref, xo_ref,
                h_scr, we_scr, acc_scr):
    e = pl.program_id(1)
    f = pl.program_id(2)

    @pl.when((e == 0) & (f == 0))
    def _():
        h = _rms(x_ref[...], g_ref[...]) * (1.0 + mod_ref[4:5, :]) + mod_ref[3:4, :]
        h_scr[...] = h.astype(BF16)
        logits = jnp.dot(h, rw_ref[...], preferred_element_type=F32,
                         precision=lax.Precision.HIGHEST) + rb_ref[...]
        lane = lax.broadcasted_iota(jnp.int32, logits.shape, 1)
        m1 = jnp.max(logits, axis=-1, keepdims=True)
        i1 = jnp.min(jnp.where(logits == m1, lane, ROUTER_LANES), axis=-1, keepdims=True)
        rest = jnp.where(lane == i1, F32_MIN, logits)
        m2 = jnp.max(rest, axis=-1, keepdims=True)
        i2 = jnp.min(jnp.where(rest == m2, lane, ROUTER_LANES), axis=-1, keepdims=True)
        e2 = jnp.exp(m2 - m1)
        p1 = 1.0 / (1.0 + e2)
        p2 = e2 / (1.0 + e2)
        we_scr[...] = jnp.where(lane == i1, p1, 0.0) + jnp.where(lane == i2, p2, 0.0)
        acc_scr[...] = jnp.zeros_like(acc_scr)

    h = h_scr[...]
    a = jnp.dot(h, w1_ref[...], preferred_element_type=F32)
    b = jnp.dot(h, w3_ref[...], preferred_element_type=F32)
    lane = lax.broadcasted_iota(jnp.int32, (MOE_TM, ROUTER_LANES), 1)
    we = jnp.sum(jnp.where(lane == e, we_scr[...], 0.0), axis=-1, keepdims=True)
    hid = (a * jax.nn.sigmoid(a) * b * we).astype(BF16)
    acc_scr[...] += jnp.dot(hid, w2_ref[...], preferred_element_type=F32)

    @pl.when((e == N_EXPERTS - 1) & (f == pl.num_programs(2) - 1))
    def _():
        xo_ref[...] = x_ref[...] + mod_ref[5:6, :] * acc_scr[...]


def moe_ffn(x, mod, g, rw, rb, w1, w3, w2):
    nt = N_TOK // MOE_TM
    nf = D_FF_EXPERT // MOE_TF
    return pl.pallas_call(
        _moe_kernel,
        grid=(nt, N_EXPERTS, nf),
        in_specs=[pl.BlockSpec((MOE_TM, D_MODEL), lambda i, e, f: (i, 0)),
                  pl.BlockSpec((None, 6, D_MODEL), lambda i, e, f: (_tile_cond(i, MOE_TM), 0, 0)),
                  pl.BlockSpec((1, D_MODEL), lambda i, e, f: (0, 0)),
                  pl.BlockSpec((D_MODEL, ROUTER_LANES), lambda i, e, f: (0, 0)),
                  pl.BlockSpec((1, ROUTER_LANES), lambda i, e, f: (0, 0)),
                  pl.BlockSpec((None, D_MODEL, MOE_TF), lambda i, e, f: (e, 0, f)),
                  pl.BlockSpec((None, D_MODEL, MOE_TF), lambda i, e, f: (e, 0, f)),
                  pl.BlockSpec((None, MOE_TF, D_MODEL), lambda i, e, f: (e, f, 0))],
        out_specs=pl.BlockSpec((MOE_TM, D_MODEL), lambda i, e, f: (i, 0)),
        out_shape=jax.ShapeDtypeStruct((N_TOK, D_MODEL), F32),
        scratch_shapes=[pltpu.VMEM((MOE_TM, D_MODEL), BF16),
                        pltpu.VMEM((MOE_TM, ROUTER_LANES), F32),
                        pltpu.VMEM((MOE_TM, D_MODEL), F32)],
        compiler_params=_params(("arbitrary", "arbitrary", "arbitrary")),
        name="moe_ffn",
    )(x, mod, g.reshape(1, D_MODEL), rw, rb, w1, w3, w2)


def _final_kernel(x_ref, g_ref, o_ref):
    o_ref[...] = _rms(x_ref[...], g_ref[...])


def final_norm(x, g):
    tm = 1024
    return pl.pallas_call(
        _final_kernel,
        grid=(N_TOK // tm,),
        in_specs=[pl.BlockSpec((tm, D_MODEL), lambda i: (i, 0)),
                  pl.BlockSpec((1, D_MODEL), lambda i: (0, 0))],
        out_specs=pl.BlockSpec((tm, D_MODEL), lambda i: (i, 0)),
        out_shape=jax.ShapeDtypeStruct((N_TOK, D_MODEL), F32),
        compiler_params=_params(("arbitrary",)),
        name="final_norm",
    )(x, g.reshape(1, D_MODEL))


def _rms_heads(x, g):
    return x * lax.rsqrt(jnp.mean(x * x, axis=-1, keepdims=True) + EPS) * g


def _l2n(x):
    return x * lax.rsqrt(jnp.sum(x * x, axis=-1, keepdims=True) + EPS)


def _flip(t):
    return jnp.flip(t, axis=1)


def _rope_tables(L):
    rows = L // GRID_W
    row = jnp.repeat(jnp.arange(rows, dtype=F32), GRID_W)
    col = jnp.tile(jnp.arange(GRID_W, dtype=F32), rows)
    n_freq = HEAD_DIM // 4
    inv = ROPE_BASE ** (-jnp.arange(n_freq, dtype=F32) / n_freq)
    ang = jnp.stack([row, col], 0)[:, :, None] * inv
    return jnp.cos(ang), jnp.sin(ang)


def _rope(x, tables):
    cos, sin = tables
    half = HEAD_DIM // 2
    nf = HEAD_DIM // 4
    outs = []
    for a in range(2):
        xa = x[..., a * half:(a + 1) * half]
        x1, x2 = xa[..., :nf], xa[..., nf:]
        ca = cos[a][None, :, None, :]
        sa = sin[a][None, :, None, :]
        outs += [x1 * ca - x2 * sa, x2 * ca + x1 * sa]
    return jnp.concatenate(outs, axis=-1)


def _dense_gqa(q, k, v, sink):
    Bn, Lq, Hq, d = q.shape
    Hkv = k.shape[2]
    G = Hq // Hkv
    nb = Lq // BLOCK
    scale = d ** -0.5
    qb = q.reshape(Bn, nb, BLOCK, Hkv, G, d).transpose(1, 0, 2, 3, 4, 5)

    def one(qblk):
        s = jnp.einsum('bqhgd,bkhd->bhgqk', qblk, k) * scale
        if sink is not None:
            sl = jnp.broadcast_to(sink.reshape(Hkv, G)[None, :, :, None, None], s.shape[:-1] + (1,))
            p = jax.nn.softmax(jnp.concatenate([s, sl], axis=-1), axis=-1)[..., :-1]
        else:
            p = jax.nn.softmax(s, axis=-1)
        return jnp.einsum('bhgqk,bkhd->bqhgd', p, v)

    o = lax.map(one, qb)
    return o.transpose(1, 0, 2, 3, 4, 5).reshape(Bn, Lq, Hq * d)


def _windowed_gqa(q, k, v, kc, vc, sink):
    Bn, L, Hq, d = q.shape
    Hkv = k.shape[2]
    G = Hq // Hkv
    nb = L // BLOCK
    scale = d ** -0.5
    qb = q.reshape(Bn, nb, BLOCK, Hkv, G, d)

    def band(t):
        tb = t.reshape(Bn, nb, BLOCK, Hkv, d)
        z = jnp.zeros_like(tb[:, :1])
        prev = jnp.concatenate([z, tb[:, :-1]], axis=1)
        nxt = jnp.concatenate([tb[:, 1:], z], axis=1)
        return jnp.concatenate([prev, tb, nxt], axis=2)

    kband, vband = band(k), band(v)
    s_loc = jnp.einsum('bnqhgd,bnkhd->bnhgqk', qb, kband) * scale
    s_ctx = jnp.einsum('bnqhgd,bkhd->bnhgqk', qb, kc) * scale
    qpos = jnp.arange(nb)[:, None, None] * BLOCK + jnp.arange(BLOCK)[None, :, None]
    kpos = jnp.arange(nb)[:, None, None] * BLOCK - BLOCK + jnp.arange(3 * BLOCK)[None, None, :]
    mask = (jnp.abs(kpos - qpos) <= WINDOW) & (kpos >= 0) & (kpos < L)
    s_loc = jnp.where(mask[None, :, None, None], s_loc, NEG_INF)
    sl = jnp.broadcast_to(sink.reshape(Hkv, G)[None, None, :, :, None, None], s_loc.shape[:-1] + (1,))
    p = jax.nn.softmax(jnp.concatenate([s_loc, s_ctx, sl], axis=-1), axis=-1)
    nk = 3 * BLOCK
    lc = kc.shape[1]
    o = (jnp.einsum('bnhgqk,bnkhd->bnqhgd', p[..., :nk], vband)
         + jnp.einsum('bnhgqk,bkhd->bnqhgd', p[..., nk:nk + lc], vc))
    return o.reshape(Bn, L, Hq * d)


def _gla_chunked(q, k, v, logf, s0):
    Bn, L, H, _ = q.shape
    C = HGRN_CHUNK
    n = L // C
    ch = lambda t: t.reshape(Bn, n, C, H, t.shape[-1]).transpose(1, 0, 3, 2, 4)
    q, k, v, logf = ch(q), ch(k), ch(v), ch(logf)
    G = jnp.cumsum(logf, axis=3)
    G_last = G[:, :, :, -1:, :]
    q_in = q * jnp.exp(G)
    k_in = k * jnp.exp(-G)
    k_out = k * jnp.exp(G_last - G)
    causal = jnp.tril(jnp.ones((C, C), bool))
    A = jnp.where(causal, jnp.einsum('nbhtd,nbhsd->nbhts', q_in, k_in), 0.0)
    o_intra = jnp.einsum('nbhts,nbhsv->nbhtv', A, v)

    def step(S, xs):
        qi, ko, vi, oi, gl = xs
        o = oi + jnp.einsum('bhtd,bhdv->bhtv', qi, S)
        S = jnp.exp(gl)[:, :, 0, :, None] * S + jnp.einsum('bhsd,bhsv->bhdv', ko, vi)
        return S, o

    S, o = lax.scan(step, s0, (q_in, k_out, v, o_intra, G_last))
    return o.transpose(1, 0, 3, 2, 4).reshape(Bn, L, H, -1), S


def _delta_chunked(q, k, v, log_a, beta, s0):
    Bn, L, H, _ = q.shape
    C = DELTA_CHUNK
    n = L // C
    ch = lambda t: t.reshape(Bn, n, C, H, t.shape[-1]).transpose(1, 0, 3, 2, 4)
    chs = lambda t: t.reshape(Bn, n, C, H).transpose(1, 0, 3, 2)
    q, k, v = ch(q), ch(k), ch(v)
    log_a, beta = chs(log_a), chs(beta)
    g = jnp.cumsum(log_a, axis=-1)
    kb = k * beta[..., None]
    vb = v * beta[..., None]
    incl = jnp.tril(jnp.ones((C, C), bool))
    strict = jnp.tril(jnp.ones((C, C), bool), -1)
    decay = jnp.exp(jnp.where(incl, g[..., :, None] - g[..., None, :], -jnp.inf))
    Lm = jnp.where(strict, jnp.einsum('nbhid,nbhjd->nbhij', kb, k) * decay, 0.0)
    eye = jnp.eye(C, dtype=F32)
    T = lax.linalg.triangular_solve(eye + Lm, jnp.broadcast_to(eye, Lm.shape),
                                    left_side=True, lower=True, unit_diagonal=True)
    u = T @ vb
    w = T @ (kb * jnp.exp(g)[..., None])
    attn = jnp.einsum('nbhid,nbhjd->nbhij', q, k) * decay
    q_g = q * jnp.exp(g)[..., None]
    g_last = g[..., -1]
    k_end = k * jnp.exp(g_last[..., None] - g)[..., None]

    def step(S, xs):
        u_c, w_c, qg_c, at_c, ke_c, gl_c = xs
        v_new = u_c - w_c @ S
        o = qg_c @ S + at_c @ v_new
        S = jnp.exp(gl_c)[..., None, None] * S + jnp.einsum('bhcd,bhcv->bhdv', ke_c, v_new)
        return S, o

    S, o = lax.scan(step, s0, (u, w, q_g, attn, k_end, g_last))
    return o.transpose(1, 0, 3, 2, 4).reshape(Bn, L, H, -1), S


def _hgrn2(q, zf_f, zf_b, i, g, lb, onorm_g, s0):
    Bn, L, _ = q.shape
    hd = lambda t: t.reshape(Bn, L, C_HEADS, HEAD_DIM)
    lb = lb.reshape(C_HEADS, HEAD_DIM)

    def gates(zf):
        zf = hd(zf)
        f = lb + (1.0 - lb) * jax.nn.sigmoid(zf)
        return jnp.log(f), (1.0 - lb) * jax.nn.sigmoid(-zf)

    q, i = hd(q), hd(i)
    logf_f, k_f = gates(zf_f)
    logf_b, k_b = gates(zf_b)
    o_f, s_f = _gla_chunked(q, k_f, i, logf_f, s0[:, 0])
    o_b, s_b = _gla_chunked(_flip(q), _flip(k_b), _flip(i), _flip(logf_b), s0[:, 1])
    o = _rms_heads(o_f + _flip(o_b), onorm_g).reshape(Bn, L, BRANCH_W)
    return o * jax.nn.silu(g), jnp.stack([s_f, s_b], axis=1)


def _conv(x, w):
    pad = (CONV_K - 1) // 2
    return lax.conv_general_dilated(x, w[:, None, :], window_strides=(1,),
                                    padding=[(pad, pad)], dimension_numbers=('NWC', 'WIO', 'NWC'),
                                    feature_group_count=x.shape[-1])


def _gated_delta(qkv, za, zb, g, conv_w, a_log, dt_bias, onorm_g, s0):
    Bn, L, _ = qkv.shape
    qkv = jax.nn.silu(_conv(qkv, conv_w))
    q, k, v = [t.reshape(Bn, L, D_HEADS, HEAD_DIM) for t in jnp.split(qkv, 3, axis=-1)]
    q = _l2n(q) * HEAD_DIM ** -0.5
    k = _l2n(k)
    za = za.reshape(Bn, L, 2, D_HEADS)
    zb = zb.reshape(Bn, L, 2, D_HEADS)
    log_alpha = -jnp.exp(a_log) * jax.nn.softplus(za + dt_bias)
    beta = jax.nn.sigmoid(zb)
    o_f, s_f = _delta_chunked(q, k, v, log_alpha[:, :, 0], beta[:, :, 0], s0[:, 0])
    o_b, s_b = _delta_chunked(_flip(q), _flip(k), _flip(v), _flip(log_alpha[:, :, 1]),
                              _flip(beta[:, :, 1]), s0[:, 1])
    o = _rms_heads(o_f + _flip(o_b), onorm_g).reshape(Bn, L, BRANCH_W)
    return o * jax.nn.silu(g), jnp.stack([s_f, s_b], axis=1)


def _mixers(z, lw, ctx):
    Bn, L, _ = z.shape
    aq, ak, av = z[..., 0:256], z[..., 256:384], z[..., 384:512]
    bq, bk, bv = z[..., 512:768], z[..., 768:896], z[..., 896:1024]
    cq, cf_f, cf_b = z[..., 1024:1280], z[..., 1280:1536], z[..., 1536:1792]
    ci, cg = z[..., 1792:2048], z[..., 2048:2304]
    dqkv = z[..., 2304:3072]
    da, db = z[..., Z_DAB:Z_DAB + 8], z[..., Z_DAB + 8:Z_DAB + 16]
    dg = z[..., Z_DG:Z_DG + 256]
    heads = lambda t: t.reshape(Bn, L, -1, HEAD_DIM)
    aq, ak, av, bv = heads(aq), heads(ak), heads(av), heads(bv)
    bq = _rms_heads(heads(bq), lw['b_qnorm_g'])
    bk = _rms_heads(heads(bk), lw['b_knorm_g'])
    if ctx is None:
        s_c0 = jnp.zeros((Bn, 2, C_HEADS, HEAD_DIM, HEAD_DIM), F32)
        s_d0 = jnp.zeros((Bn, 2, D_HEADS, HEAD_DIM, HEAD_DIM), F32)
        o_a = _dense_gqa(aq, ak, av, lw['a_sink'])
        o_b = _dense_gqa(bq, bk, bv, None)
    else:
        ka, va, kbc, vbc, s_c0, s_d0 = ctx
        tabs = _rope_tables(L)
        o_a = _windowed_gqa(_rope(aq, tabs), _rope(ak, tabs), av, ka, va, lw['a_sink'])
        o_b = _dense_gqa(_rope(bq, tabs), jnp.concatenate([_rope(bk, tabs), kbc], axis=1),
                         jnp.concatenate([bv, vbc], axis=1), None)
    o_c, s_c = _hgrn2(cq, cf_f, cf_b, ci, cg, lw['lb'], lw['c_onorm_g'], s_c0)
    o_d, s_d = _gated_delta(dqkv, da, db, dg, lw['d_conv'], lw['d_a_log'], lw['d_dt_bias'],
                            lw['d_onorm_g'], s_d0)
    o = jnp.concatenate([o_a, o_b, o_c, o_d], axis=-1)
    return o, (ak, av, bk, bv, s_c, s_d)


def kernel(x_prompt, x_sample, cache_attn_a_k, cache_attn_a_v, cache_attn_b_k, cache_attn_b_v,
           state_hgrn, state_delta, c, c_ctx, norm1_g, norm2_g, w_ada, b_ada, w_in, a_sink,
           b_qnorm_g, b_knorm_g, c_lb, c_onorm_g, d_conv, d_a_log, d_dt_bias, d_onorm_g,
           w_branch, w_out, ffn_w1, ffn_w3, ffn_w2, router_w, router_b, moe_w1, moe_w3, moe_w2,
           final_norm_g):
    cum = jnp.cumsum(jax.nn.softmax(c_lb, axis=0), axis=0)
    lower_bounds = cum - cum[:1]

    x = jnp.concatenate([x_prompt.reshape(CTX_TOK, D_MODEL), x_sample.reshape(LAT_TOK, D_MODEL)], axis=0)
    cond = jnp.concatenate([c_ctx[None, :], c, jnp.zeros((16 - N_COND, D_MODEL), F32)], axis=0)

    caches = []
    for l in range(DEPTH):
        lw = {'a_sink': a_sink[l], 'b_qnorm_g': b_qnorm_g[l], 'b_knorm_g': b_knorm_g[l],
              'lb': lower_bounds[l], 'c_onorm_g': c_onorm_g[l], 'd_conv': d_conv[l], 'd_a_log': d_a_log[l],
              'd_dt_bias': d_dt_bias[l], 'd_onorm_g': d_onorm_g[l]}
        mod = ada_modulation(cond, w_ada[l], b_ada[l])[:N_COND].reshape(N_COND, 6, D_MODEL)
        w_mix = jnp.concatenate([w_in[l][:, :Z_DAB + 16], jnp.zeros((D_MODEL, 128 - 16), F32),
                                 w_in[l][:, Z_DAB + 16:W_IN_MIX]], axis=1).astype(BF16)
        w_gl = w_in[l][:, W_IN_MIX:].reshape(D_MODEL, N_BRANCH, D_MODEL).transpose(1, 0, 2).astype(BF16)
        z = input_projection(x, mod, norm1_g[l], w_mix)
        z_ctx = z[:CTX_TOK].reshape(BATCH, SEQ, Z_COLS)
        z_lat = z[CTX_TOK:].reshape(DEC_BATCH, DEC_SEQ, Z_COLS)
        o_ctx, cache_l = _mixers(z_ctx, lw, None)
        caches.append(cache_l)
        ctx = (cache_attn_a_k[:, l], cache_attn_a_v[:, l], cache_attn_b_k[:, l], cache_attn_b_v[:, l],
               state_hgrn[:, l], state_delta[:, l])
        o_lat, _ = _mixers(z_lat, lw, ctx)
        o = jnp.concatenate([o_ctx.reshape(CTX_TOK, -1), o_lat.reshape(LAT_TOK, -1)], axis=0)
        x = merge_projection(x, mod, norm1_g[l], o, w_gl, w_branch[l].astype(BF16), w_out[l].astype(BF16))
        j = l // 2
        if l % 2 == 0:
            x = dense_ffn(x, mod, norm2_g[l], ffn_w1[j].astype(BF16), ffn_w3[j].astype(BF16),
                          ffn_w2[j].astype(BF16))
        else:
            rw = jnp.concatenate([router_w[j], jnp.zeros((D_MODEL, ROUTER_LANES - N_EXPERTS), F32)], axis=1)
            rb = jnp.concatenate([router_b[j], jnp.full((ROUTER_LANES - N_EXPERTS,), F32_MIN, F32)])[None, :]
            x = moe_ffn(x, mod, norm2_g[l], rw, rb, moe_w1[j].astype(BF16), moe_w3[j].astype(BF16),
                        moe_w2[j].astype(BF16))

    y = final_norm(x, final_norm_g)
    y_prompt = y[:CTX_TOK].reshape(BATCH, SEQ, D_MODEL)
    y_sample = y[CTX_TOK:].reshape(DEC_BATCH, DEC_SEQ, D_MODEL)
    stack = lambda idx: jnp.stack([caches[l][idx] for l in range(DEPTH)], axis=1)
    return (y_prompt, y_sample, stack(0), stack(1), stack(2), stack(3), stack(4), stack(5))
```

```python
import functools
import math

import jax
import jax.numpy as jnp
import numpy as np
from jax import lax
from jax.experimental import pallas as pl
from jax.experimental.pallas import tpu as pltpu

F32 = jnp.float32
BF16 = jnp.bfloat16

D_MODEL = 1024
BATCH = 32
SEQ = 256
DEPTH = 2
DEC_BATCH = 8
DEC_SEQ = 4096
PAST_LEN = 256
GRID_W = 64
HEAD_DIM = 64
A_HEADS = 4
A_KV_HEADS = 2
B_HEADS = 4
B_KV_HEADS = 2
C_HEADS = 4
D_HEADS = 4
BRANCH_W = 256
N_BRANCH = 4
WINDOW = 128
BLOCK = 128
ROPE_BASE = 10000.0
HGRN_CHUNK = 32
DELTA_CHUNK = 64
CONV_K = 5
D_FF = 2816
N_EXPERTS = 8
TOP_K = 2
D_FF_EXPERT = 3584
EPS = 1e-6
NEG_INF = -1e30
F32_MIN = float(np.finfo(np.float32).min)

CTX_TOK = BATCH * SEQ
LAT_TOK = DEC_BATCH * DEC_SEQ
N_TOK = CTX_TOK + LAT_TOK
N_COND = 1 + DEC_BATCH

Z_MAIN = 3072
Z_DAB = Z_MAIN
Z_DG = Z_MAIN + 128
Z_COLS = Z_DG + BRANCH_W
W_IN_MIX = 3344

TM = 512
VMEM_LIMIT = 56 * 1024 * 1024


def _tile_cond(i, tm):
    ctx_tiles = CTX_TOK // tm
    per_b = DEC_SEQ // tm
    return jnp.where(i < ctx_tiles, 0, 1 + (i - ctx_tiles) // per_b)


def _rms(x, g):
    return x * lax.rsqrt(jnp.mean(x * x, axis=-1, keepdims=True) + EPS) * g


def _params(sem):
    return pltpu.CompilerParams(dimension_semantics=sem, vmem_limit_bytes=VMEM_LIMIT)


def _ada_kernel(c_ref, w_ref, b_ref, o_ref):
    c = c_ref[...]
    s = c * jax.nn.sigmoid(c)
    o_ref[...] = jnp.dot(s.astype(BF16), w_ref[...].astype(BF16), preferred_element_type=F32) + b_ref[...]


def ada_modulation(cond_pad, w, b):
    n = 6 * D_MODEL
    tn = 1536
    return pl.pallas_call(
        _ada_kernel,
        grid=(n // tn,),
        in_specs=[pl.BlockSpec((16, D_MODEL), lambda j: (0, 0)),
                  pl.BlockSpec((D_MODEL, tn), lambda j: (0, j)),
                  pl.BlockSpec((1, tn), lambda j: (0, j))],
        out_specs=pl.BlockSpec((16, tn), lambda j: (0, j)),
        out_shape=jax.ShapeDtypeStruct((16, n), F32),
        compiler_params=_params(("arbitrary",)),
        name="ada_modulation",
    )(cond_pad, w, b.reshape(1, n))


def _in_kernel(x_ref, mod_ref, g_ref, w_ref, z_ref):
    h = _rms(x_ref[...], g_ref[...]) * (1.0 + mod_ref[1:2, :]) + mod_ref[0:1, :]
    z_ref[...] = jnp.dot(h.astype(BF16), w_ref[...], preferred_element_type=F32)


def input_projection(x, mod, g, w):
    nt = N_TOK // TM
    return pl.pallas_call(
        _in_kernel,
        grid=(nt,),
        in_specs=[pl.BlockSpec((TM, D_MODEL), lambda i: (i, 0)),
                  pl.BlockSpec((None, 6, D_MODEL), lambda i: (_tile_cond(i, TM), 0, 0)),
                  pl.BlockSpec((1, D_MODEL), lambda i: (0, 0)),
                  pl.BlockSpec((D_MODEL, Z_COLS), lambda i: (0, 0))],
        out_specs=pl.BlockSpec((TM, Z_COLS), lambda i: (i, 0)),
        out_shape=jax.ShapeDtypeStruct((N_TOK, Z_COLS), F32),
        compiler_params=_params(("arbitrary",)),
        name="input_projection",
    )(x, mod, g.reshape(1, D_MODEL), w)


def _merge_kernel(x_ref, mod_ref, g_ref, o_ref, wgl_ref, wbr_ref, wout_ref, xo_ref):
    x = x_ref[...]
    h = (_rms(x, g_ref[...]) * (1.0 + mod_ref[1:2, :]) + mod_ref[0:1, :]).astype(BF16)
    merged = None
    for j in range(N_BRANCH):
        gate = jax.nn.sigmoid(jnp.dot(h, wgl_ref[j], preferred_element_type=F32))
        br = jnp.dot(o_ref[:, j * BRANCH_W:(j + 1) * BRANCH_W].astype(BF16), wbr_ref[j],
                     preferred_element_type=F32)
        merged = gate * br if merged is None else merged + gate * br
    mix = jnp.dot(merged.astype(BF16), wout_ref[...], preferred_element_type=F32)
    xo_ref[...] = x + mod_ref[2:3, :] * mix


def merge_projection(x, mod, g, o, wgl, wbr, wout):
    nt = N_TOK // TM
    return pl.pallas_call(
        _merge_kernel,
        grid=(nt,),
        in_specs=[pl.BlockSpec((TM, D_MODEL), lambda i: (i, 0)),
                  pl.BlockSpec((None, 6, D_MODEL), lambda i: (_tile_cond(i, TM), 0, 0)),
                  pl.BlockSpec((1, D_MODEL), lambda i: (0, 0)),
                  pl.BlockSpec((TM, N_BRANCH * BRANCH_W), lambda i: (i, 0)),
                  pl.BlockSpec((N_BRANCH, D_MODEL, D_MODEL), lambda i: (0, 0, 0)),
                  pl.BlockSpec((N_BRANCH, BRANCH_W, D_MODEL), lambda i: (0, 0, 0)),
                  pl.BlockSpec((D_MODEL, D_MODEL), lambda i: (0, 0))],
        out_specs=pl.BlockSpec((TM, D_MODEL), lambda i: (i, 0)),
        out_shape=jax.ShapeDtypeStruct((N_TOK, D_MODEL), F32),
        compiler_params=_params(("arbitrary",)),
        name="merge_projection",
    )(x, mod, g.reshape(1, D_MODEL), o, wgl, wbr, wout)


def _ffn_kernel(x_ref, mod_ref, g_ref, w1_ref, w3_ref, w2_ref, xo_ref):
    x = x_ref[...]
    h = (_rms(x, g_ref[...]) * (1.0 + mod_ref[4:5, :]) + mod_ref[3:4, :]).astype(BF16)
    a = jnp.dot(h, w1_ref[...], preferred_element_type=F32)
    b = jnp.dot(h, w3_ref[...], preferred_element_type=F32)
    hid = (a * jax.nn.sigmoid(a) * b).astype(BF16)
    f = jnp.dot(hid, w2_ref[...], preferred_element_type=F32)
    xo_ref[...] = x + mod_ref[5:6, :] * f


def dense_ffn(x, mod, g, w1, w3, w2):
    nt = N_TOK // TM
    const = lambda i: (0, 0)
    return pl.pallas_call(
        _ffn_kernel,
        grid=(nt,),
        in_specs=[pl.BlockSpec((TM, D_MODEL), lambda i: (i, 0)),
                  pl.BlockSpec((None, 6, D_MODEL), lambda i: (_tile_cond(i, TM), 0, 0)),
                  pl.BlockSpec((1, D_MODEL), const),
                  pl.BlockSpec((D_MODEL, D_FF), const, pipeline_mode=pl.Buffered(1)),
                  pl.BlockSpec((D_MODEL, D_FF), const, pipeline_mode=pl.Buffered(1)),
                  pl.BlockSpec((D_FF, D_MODEL), const, pipeline_mode=pl.Buffered(1))],
        out_specs=pl.BlockSpec((TM, D_MODEL), lambda i: (i, 0)),
        out_shape=jax.ShapeDtypeStruct((N_TOK, D_MODEL), F32),
        compiler_params=_params(("arbitrary",)),
        name="dense_ffn",
    )(x, mod, g.reshape(1, D_MODEL), w1, w3, w2)


MOE_TM = 1024
MOE_TF = 896
ROUTER_LANES = 128


def _moe_kernel(x_ref, mod_ref, g_ref, rw_ref, rb_ref, w1_ref, w3_ref, w2_ref, xo_ref,
                h_scr, we_scr, acc_scr):
    e = pl.program_id(1)
    f = pl.program_id(2)

    @pl.when((e == 0) & (f == 0))
    def _():
        h = _rms(x_ref[...], g_ref[...]) * (1.0 + mod_ref[4:5, :]) + mod_ref[3:4, :]
        h_scr[...] = h.astype(BF16)
        logits = jnp.dot(h, rw_ref[...], preferred_element_type=F32,
                         precision=lax.Precision.HIGHEST) + rb_ref[...]
        lane = lax.broadcasted_iota(jnp.int32, logits.shape, 1)
        m1 = jnp.max(logits, axis=-1, keepdims=True)
        i1 = jnp.min(jnp.where(logits == m1, lane, ROUTER_LANES), axis=-1, keepdims=True)
        rest = jnp.where(lane == i1, F32_MIN, logits)
        m2 = jnp.max(rest, axis=-1, keepdims=True)
        i2 = jnp.min(jnp.where(rest == m2, lane, ROUTER_LANES), axis=-1, keepdims=True)
        e2 = jnp.exp(m2 - m1)
        p1 = 1.0 / (1.0 + e2)
        p2 = e2 / (1.0 + e2)
        we_scr[...] = jnp.where(lane == i1, p1, 0.0) + jnp.where(lane == i2, p2, 0.0)
        acc_scr[...] = jnp.zeros_like(acc_scr)

    h = h_scr[...]
    a = jnp.dot(h, w1_ref[...], preferred_element_type=F32)
    b = jnp.dot(h, w3_ref[...], preferred_element_type=F32)
    lane = lax.broadcasted_iota(jnp.int32, (MOE_TM, ROUTER_LANES), 1)
    we = jnp.sum(jnp.where(lane == e, we_scr[...], 0.0), axis=-1, keepdims=True)
    hid = (a * jax.nn.sigmoid(a) * b * we).astype(BF16)
    acc_scr[...] += jnp.dot(hid, w2_ref[...], preferred_element_type=F32)

    @pl.when((e == N_EXPERTS - 1) & (f == pl.num_programs(2) - 1))
    def _():
        xo_ref[...] = x_ref[...] + mod_ref[5:6, :] * acc_scr[...]


def moe_ffn(x, mod, g, rw, rb, w1, w3, w2):
    nt = N_TOK // MOE_TM
    nf = D_FF_EXPERT // MOE_TF
    return pl.pallas_call(
        _moe_kernel,
        grid=(nt, N_EXPERTS, nf),
        in_specs=[pl.BlockSpec((MOE_TM, D_MODEL), lambda i, e, f: (i, 0)),
                  pl.BlockSpec((None, 6, D_MODEL), lambda i, e, f: (_tile_cond(i, MOE_TM), 0, 0)),
                  pl.BlockSpec((1, D_MODEL), lambda i, e, f: (0, 0)),
                  pl.BlockSpec((D_MODEL, ROUTER_LANES), lambda i, e, f: (0, 0)),
                  pl.BlockSpec((1, ROUTER_LANES), lambda i, e, f: (0, 0)),
                  pl.BlockSpec((None, D_MODEL, MOE_TF), lambda i, e, f: (e, 0, f)),
                  pl.BlockSpec((None, D_MODEL, MOE_TF), lambda i, e, f: (e, 0, f)),
                  pl.BlockSpec((None, MOE_TF, D_MODEL), lambda i, e, f: (e, f, 0))],
        out_specs=pl.BlockSpec((MOE_TM, D_MODEL), lambda i, e, f: (i, 0)),
        out_shape=jax.ShapeDtypeStruct((N_TOK, D_MODEL), F32),
        scratch_shapes=[pltpu.VMEM((MOE_TM, D_MODEL), BF16),
                        pltpu.VMEM((MOE_TM, ROUTER_LANES), F32),
                        pltpu.VMEM((MOE_TM, D_MODEL), F32)],
        compiler_params=_params(("arbitrary", "arbitrary", "arbitrary")),
        name="moe_ffn",
    )(x, mod, g.reshape(1, D_MODEL), rw, rb, w1, w3, w2)


def _final_kernel(x_ref, g_ref, o_ref):
    o_ref[...] = _rms(x_ref[...], g_ref[...])


def final_norm(x, g):
    tm = 1024
    return pl.pallas_call(
        _final_kernel,
        grid=(N_TOK // tm,),
        in_specs=[pl.BlockSpec((tm, D_MODEL), lambda i: (i, 0)),
                  pl.BlockSpec((1, D_MODEL), lambda i: (0, 0))],
        out_specs=pl.BlockSpec((tm, D_MODEL), lambda i: (i, 0)),
        out_shape=jax.ShapeDtypeStruct((N_TOK, D_MODEL), F32),
        compiler_params=_params(("arbitrary",)),
        name="final_norm",
    )(x, g.reshape(1, D_MODEL))


ATT_SCALE = HEAD_DIM ** -0.5
_NT = (((1,), (1,)), ((), ()))


def _head_rms(x, g_row):
    outs = []
    for h in range(x.shape[1] // HEAD_DIM):
        xh = x[:, h * HEAD_DIM:(h + 1) * HEAD_DIM]
        outs.append(xh * lax.rsqrt(jnp.mean(xh * xh, axis=-1, keepdims=True) + EPS) * g_row)
    return jnp.concatenate(outs, axis=-1)


def _rope_apply(x, c, s):
    w = x.shape[-1]
    lane = lax.broadcasted_iota(jnp.int32, x.shape, 1)
    first_half = ((lane // (HEAD_DIM // 4)) % 2) == 0
    partner = jnp.where(first_half, pltpu.roll(x, w - HEAD_DIM // 4, 1), pltpu.roll(x, HEAD_DIM // 4, 1))
    return x * c + partner * s


def rope_lane_tables(L):
    rows = L // GRID_W
    row = jnp.repeat(jnp.arange(rows, dtype=F32), GRID_W)
    col = jnp.tile(jnp.arange(GRID_W, dtype=F32), rows)
    n_freq = HEAD_DIM // 4
    inv = ROPE_BASE ** (-jnp.arange(n_freq, dtype=F32) / n_freq)
    ang = jnp.stack([row, col], 0)[:, :, None] * inv
    cos, sin = jnp.cos(ang), jnp.sin(ang)
    c = jnp.concatenate([cos[0], cos[0], cos[1], cos[1]], axis=-1)
    s = jnp.concatenate([-sin[0], sin[0], -sin[1], sin[1]], axis=-1)
    return jnp.tile(c, (1, 4)), jnp.tile(s, (1, 4))


def _attend(q, k, v, extra=None, sink=None, mask=None):
    s = lax.dot_general(q.astype(BF16), k, _NT, preferred_element_type=F32) * ATT_SCALE
    if mask is not None:
        s = jnp.where(mask, s, NEG_INF)
    m = jnp.max(s, axis=-1, keepdims=True)
    if extra is not None:
        s2 = lax.dot_general(q.astype(BF16), extra[0], _NT, preferred_element_type=F32) * ATT_SCALE
        m = jnp.maximum(m, jnp.max(s2, axis=-1, keepdims=True))
    if sink is not None:
        m = jnp.maximum(m, sink)
    p = jnp.exp(s - m)
    den = jnp.sum(p, axis=-1, keepdims=True)
    o = jnp.dot(p.astype(BF16), v, preferred_element_type=F32)
    if extra is not None:
        p2 = jnp.exp(s2 - m)
        den = den + jnp.sum(p2, axis=-1, keepdims=True)
        o = o + jnp.dot(p2.astype(BF16), extra[1], preferred_element_type=F32)
    if sink is not None:
        den = den + jnp.exp(sink - m)
    return o / den


def _ctx_attn_kernel(sink_ref, z_ref, gq_ref, gk_ref, o_ref, bk_ref):
    z = z_ref[...]
    bq = _head_rms(z[:, 512:768], gq_ref[...])
    bk = _head_rms(z[:, 768:896], gk_ref[...])
    bk_ref[...] = bk
    groups = ((z[:, 0:256], z[:, 256:384], z[:, 384:512], True),
              (bq, bk, z[:, 896:1024], False))
    outs = []
    for q_all, k_all, v_all, use_sink in groups:
        k_all = k_all.astype(BF16)
        v_all = v_all.astype(BF16)
        for hq in range(A_HEADS):
            kv = hq // (A_HEADS // A_KV_HEADS)
            sl = slice(kv * HEAD_DIM, (kv + 1) * HEAD_DIM)
            outs.append(_attend(q_all[:, hq * HEAD_DIM:(hq + 1) * HEAD_DIM], k_all[:, sl], v_all[:, sl],
                                sink=sink_ref[hq] if use_sink else None))
    o_ref[...] = jnp.concatenate(outs, axis=-1).astype(o_ref.dtype)


def ctx_attention(z, sink, gq, gk):
    return pl.pallas_call(
        _ctx_attn_kernel,
        grid=(BATCH,),
        in_specs=[pl.BlockSpec(memory_space=pltpu.SMEM),
                  pl.BlockSpec((SEQ, 1024), lambda b: (b, 0)),
                  pl.BlockSpec((1, HEAD_DIM), lambda b: (0, 0)),
                  pl.BlockSpec((1, HEAD_DIM), lambda b: (0, 0))],
        out_specs=[pl.BlockSpec((SEQ, 512), lambda b: (b, 0)),
                   pl.BlockSpec((SEQ, 128), lambda b: (b, 0))],
        out_shape=[jax.ShapeDtypeStruct((CTX_TOK, 512), BF16),
                   jax.ShapeDtypeStruct((CTX_TOK, 128), F32)],
        compiler_params=_params(("arbitrary",)),
        name="ctx_attention",
    )(sink, z, gq.reshape(1, HEAD_DIM), gk.reshape(1, HEAD_DIM))


LB_TQ = 256


def _lat_b_kernel(zq_ref, zkv_ref, ck_ref, cv_ref, cq_ref, sq_ref, ckk_ref, skk_ref, gq_ref, gk_ref,
                  o_ref, k_scr, v_scr):
    L = zkv_ref.shape[0]

    @pl.when(pl.program_id(1) == 0)
    def _():
        kv = zkv_ref[...]
        bk = _rope_apply(_head_rms(kv[:, :128], gk_ref[...]), ckk_ref[...], skk_ref[...])
        k_scr[0:L, :] = bk.astype(BF16)
        k_scr[L:L + PAST_LEN, :] = ck_ref[...].astype(BF16)
        v_scr[0:L, :] = kv[:, 128:].astype(BF16)
        v_scr[L:L + PAST_LEN, :] = cv_ref[...].astype(BF16)

    q = _rope_apply(_head_rms(zq_ref[...], gq_ref[...]), cq_ref[...], sq_ref[...])
    outs = []
    for hq in range(B_HEADS):
        kv = hq // (B_HEADS // B_KV_HEADS)
        sl = slice(kv * HEAD_DIM, (kv + 1) * HEAD_DIM)
        outs.append(_attend(q[:, hq * HEAD_DIM:(hq + 1) * HEAD_DIM], k_scr[:, sl], v_scr[:, sl]))
    o_ref[...] = jnp.concatenate(outs, axis=-1).astype(o_ref.dtype)


def latent_attention_b(z, row0, nb, L, cache_k, cache_v, rope_c, rope_s, gq, gk):
    nq = L // LB_TQ
    return pl.pallas_call(
        _lat_b_kernel,
        grid=(nb, nq),
        in_specs=[pl.BlockSpec((LB_TQ, 256), lambda b, i: (row0 // LB_TQ + b * nq + i, 2)),
                  pl.BlockSpec((L, 256), lambda b, i: (row0 // L + b, 3)),
                  pl.BlockSpec((None, PAST_LEN, 128), lambda b, i: (b, 0, 0)),
                  pl.BlockSpec((None, PAST_LEN, 128), lambda b, i: (b, 0, 0)),
                  pl.BlockSpec((LB_TQ, 256), lambda b, i: (i, 0)),
                  pl.BlockSpec((LB_TQ, 256), lambda b, i: (i, 0)),
                  pl.BlockSpec((L, 128), lambda b, i: (0, 0)),
                  pl.BlockSpec((L, 128), lambda b, i: (0, 0)),
                  pl.BlockSpec((1, HEAD_DIM), lambda b, i: (0, 0)),
                  pl.BlockSpec((1, HEAD_DIM), lambda b, i: (0, 0))],
        out_specs=pl.BlockSpec((LB_TQ, 256), lambda b, i: (b * nq + i, 0)),
        out_shape=jax.ShapeDtypeStruct((nb * L, 256), BF16),
        scratch_shapes=[pltpu.VMEM((L + PAST_LEN, 128), BF16),
                        pltpu.VMEM((L + PAST_LEN, 128), BF16)],
        compiler_params=_params(("arbitrary", "arbitrary")),
        name="latent_attention_b",
    )(z, z, cache_k, cache_v, rope_c, rope_s, rope_c, rope_s, gq.reshape(1, HEAD_DIM), gk.reshape(1, HEAD_DIM))


def _lat_a_kernel(sink_ref, zq_ref, zkv_ref, ck_ref, cv_ref, cq_ref, sq_ref, ckk_ref, skk_ref,
                  o_ref, k_scr, v_scr, ck_scr, cv_scr):
    L = zkv_ref.shape[0]
    i = pl.program_id(1)

    @pl.when(i == 0)
    def _():
        kv = zkv_ref[...]
        zeros = jnp.zeros((BLOCK, 128), BF16)
        k_scr[0:BLOCK, :] = zeros
        v_scr[0:BLOCK, :] = zeros
        k_scr[BLOCK:BLOCK + L, :] = _rope_apply(kv[:, :128], ckk_ref[...], skk_ref[...]).astype(BF16)
        v_scr[BLOCK:BLOCK + L, :] = kv[:, 128:].astype(BF16)
        k_scr[BLOCK + L:2 * BLOCK + L, :] = zeros
        v_scr[BLOCK + L:2 * BLOCK + L, :] = zeros
        ck_scr[...] = ck_ref[...].astype(BF16)
        cv_scr[...] = cv_ref[...].astype(BF16)

    q = _rope_apply(zq_ref[...], cq_ref[...], sq_ref[...])
    start = pl.multiple_of(i * BLOCK, BLOCK)
    kband = k_scr[pl.ds(start, 3 * BLOCK), :]
    vband = v_scr[pl.ds(start, 3 * BLOCK), :]
    r = lax.broadcasted_iota(jnp.int32, (BLOCK, 3 * BLOCK), 0)
    cidx = lax.broadcasted_iota(jnp.int32, (BLOCK, 3 * BLOCK), 1)
    kpos = i * BLOCK - BLOCK + cidx
    mask = (jnp.abs(cidx - BLOCK - r) <= WINDOW) & (kpos >= 0) & (kpos < L)
    outs = []
    for hq in range(A_HEADS):
        kv = hq // (A_HEADS // A_KV_HEADS)
        sl = slice(kv * HEAD_DIM, (kv + 1) * HEAD_DIM)
        outs.append(_attend(q[:, hq * HEAD_DIM:(hq + 1) * HEAD_DIM], kband[:, sl], vband[:, sl],
                            extra=(ck_scr[:, sl], cv_scr[:, sl]), sink=sink_ref[hq], mask=mask))
    o_ref[...] = jnp.concatenate(outs, axis=-1).astype(o_ref.dtype)


def latent_attention_a(z, row0, nb, L, sink, cache_k, cache_v, rope_c, rope_s):
    nq = L // BLOCK
    return pl.pallas_call(
        _lat_a_kernel,
        grid=(nb, nq),
        in_specs=[pl.BlockSpec(memory_space=pltpu.SMEM),
                  pl.BlockSpec((BLOCK, 256), lambda b, i: (row0 // BLOCK + b * nq + i, 0)),
                  pl.BlockSpec((L, 256), lambda b, i: (row0 // L + b, 1)),
                  pl.BlockSpec((None, PAST_LEN, 128), lambda b, i: (b, 0, 0)),
                  pl.BlockSpec((None, PAST_LEN, 128), lambda b, i: (b, 0, 0)),
                  pl.BlockSpec((BLOCK, 256), lambda b, i: (i, 0)),
                  pl.BlockSpec((BLOCK, 256), lambda b, i: (i, 0)),
                  pl.BlockSpec((L, 128), lambda b, i: (0, 0)),
                  pl.BlockSpec((L, 128), lambda b, i: (0, 0))],
        out_specs=pl.BlockSpec((BLOCK, 256), lambda b, i: (b * nq + i, 0)),
        out_shape=jax.ShapeDtypeStruct((nb * L, 256), BF16),
        scratch_shapes=[pltpu.VMEM((L + 2 * BLOCK, 128), BF16),
                        pltpu.VMEM((L + 2 * BLOCK, 128), BF16),
                        pltpu.VMEM((PAST_LEN, 128), BF16),
                        pltpu.VMEM((PAST_LEN, 128), BF16)],
        compiler_params=_params(("arbitrary", "arbitrary")),
        name="latent_attention_a",
    )(sink, z, z, cache_k, cache_v, rope_c, rope_s, rope_c, rope_s)


_TN = (((0,), (0,)), ((), ()))


def _hgrn_kernel(zq_ref, zf_ref, zi_ref, zg_ref, lb_ref, gn_ref, s0_ref, o_ref, sT_ref,
                 of_scr, ob_scr, logf_scr, k_scr, S_scr, *, tt):
    d = pl.program_id(1)
    j = pl.program_id(2)
    n_t = pl.num_programs(2)
    C = HGRN_CHUNK
    n_c = tt // C

    @pl.when(j == 0)
    def _():
        S_scr[...] = s0_ref[...]

    lb = lb_ref[...]
    zf = zf_ref[...]
    logf_scr[...] = jnp.log(lb + (1.0 - lb) * jax.nn.sigmoid(zf))
    k_scr[...] = (1.0 - lb) * jax.nn.sigmoid(-zf)

    row = lax.broadcasted_iota(jnp.int32, (C, C), 0)
    col = lax.broadcasted_iota(jnp.int32, (C, C), 1)

    def run(reverse, tile):
        tri = (row <= col) if reverse else (row >= col)
        cum = tri.astype(F32)

        def body(ci, carry):
            c = (n_c - 1 - ci) if reverse else ci
            r0 = pl.multiple_of(c * C, C)
            G = jnp.dot(cum, logf_scr[pl.ds(r0, C), :], precision=lax.Precision.HIGHEST,
                        preferred_element_type=F32)
            G_end = G[0:1, :] if reverse else G[C - 1:C, :]
            kc = k_scr[pl.ds(r0, C), :]
            vc = zi_ref[pl.ds(r0, C), :].astype(BF16)
            q_in = (zq_ref[pl.ds(r0, C), :] * jnp.exp(G)).astype(BF16)
            k_in = (kc * jnp.exp(-G)).astype(BF16)
            k_out = (kc * jnp.exp(G_end - G)).astype(BF16)
            decay = jnp.exp(G_end)
            outs = []
            for h in range(C_HEADS):
                sl = slice(h * HEAD_DIM, (h + 1) * HEAD_DIM)
                a = lax.dot_general(q_in[:, sl], k_in[:, sl], _NT, preferred_element_type=F32)
                a = jnp.where(tri, a, 0.0).astype(BF16)
                s_t = S_scr[h]
                o_h = (jnp.dot(a, vc[:, sl], preferred_element_type=F32)
                       + lax.dot_general(q_in[:, sl], s_t.astype(BF16), _NT, preferred_element_type=F32))
                S_scr[h] = s_t * decay[:, sl] + lax.dot_general(vc[:, sl], k_out[:, sl], _TN,
                                                                preferred_element_type=F32)
                outs.append(o_h)
            o_c = jnp.concatenate(outs, axis=-1)
            if reverse:
                ob_scr[pl.ds(r0, C), :] = o_c
            else:
                of_scr[pl.ds(pl.multiple_of(tile * tt, tt) + r0, C), :] = o_c
            return carry

        lax.fori_loop(0, n_c, body, 0)

    @pl.when(d == 0)
    def _():
        run(False, j)

    @pl.when(d == 1)
    def _():
        tile = n_t - 1 - j
        run(True, tile)
        o = of_scr[pl.ds(pl.multiple_of(tile * tt, tt), tt), :] + ob_scr[...]
        g = zg_ref[...]
        o_ref[...] = (_head_rms(o, gn_ref[...]) * (g * jax.nn.sigmoid(g))).astype(o_ref.dtype)

    @pl.when(j == n_t - 1)
    def _():
        sT_ref[...] = S_scr[...]


def hgrn_mixer(z, row0, nb, L, lb, gn, s0_t):
    tt = min(L, 512)
    n_t = L // tt
    rb = row0 // tt

    def tile(d, j):
        return jnp.where(d == 0, j, n_t - 1 - j)

    def late(d, j):
        return jnp.where(d == 0, n_t - 1, n_t - 1 - j)

    st_spec = pl.BlockSpec((None, None, C_HEADS, HEAD_DIM, HEAD_DIM), lambda b, d, j: (b, d, 0, 0, 0))
    return pl.pallas_call(
        functools.partial(_hgrn_kernel, tt=tt),
        grid=(nb, 2, n_t),
        in_specs=[pl.BlockSpec((tt, 256), lambda b, d, j: (rb + b * n_t + tile(d, j), 4)),
                  pl.BlockSpec((tt, 256), lambda b, d, j: (rb + b * n_t + tile(d, j), 5 + d)),
                  pl.BlockSpec((tt, 256), lambda b, d, j: (rb + b * n_t + tile(d, j), 7)),
                  pl.BlockSpec((tt, 256), lambda b, d, j: (rb + b * n_t + late(d, j), 8)),
                  pl.BlockSpec((1, 256), lambda b, d, j: (0, 0)),
                  pl.BlockSpec((1, HEAD_DIM), lambda b, d, j: (0, 0)),
                  st_spec],
        out_specs=[pl.BlockSpec((tt, 256), lambda b, d, j: (b * n_t + late(d, j), 0)),
                   st_spec],
        out_shape=[jax.ShapeDtypeStruct((nb * L, 256), BF16),
                   jax.ShapeDtypeStruct((nb, 2, C_HEADS, HEAD_DIM, HEAD_DIM), F32)],
        scratch_shapes=[pltpu.VMEM((L, 256), F32),
                        pltpu.VMEM((tt, 256), F32),
                        pltpu.VMEM((tt, 256), F32),
                        pltpu.VMEM((tt, 256), F32),
                        pltpu.VMEM((C_HEADS, HEAD_DIM, HEAD_DIM), F32)],
        compiler_params=_params(("arbitrary", "arbitrary", "arbitrary")),
        name="hgrn_mixer",
    )(z, z, z, z, lb.reshape(1, 256), gn.reshape(1, HEAD_DIM), s0_t)


def _rms_heads(x, g):
    return x * lax.rsqrt(jnp.mean(x * x, axis=-1, keepdims=True) + EPS) * g


def _l2n(x):
    return x * lax.rsqrt(jnp.sum(x * x, axis=-1, keepdims=True) + EPS)


def _flip(t):
    return jnp.flip(t, axis=1)


def _rope_tables(L):
    rows = L // GRID_W
    row = jnp.repeat(jnp.arange(rows, dtype=F32), GRID_W)
    col = jnp.tile(jnp.arange(GRID_W, dtype=F32), rows)
    n_freq = HEAD_DIM // 4
    inv = ROPE_BASE ** (-jnp.arange(n_freq, dtype=F32) / n_freq)
    ang = jnp.stack([row, col], 0)[:, :, None] * inv
    return jnp.cos(ang), jnp.sin(ang)


def _rope(x, tables):
    cos, sin = tables
    half = HEAD_DIM // 2
    nf = HEAD_DIM // 4
    outs = []
    for a in range(2):
        xa = x[..., a * half:(a + 1) * half]
        x1, x2 = xa[..., :nf], xa[..., nf:]
        ca = cos[a][None, :, None, :]
        sa = sin[a][None, :, None, :]
        outs += [x1 * ca - x2 * sa, x2 * ca + x1 * sa]
    return jnp.concatenate(outs, axis=-1)


def _dense_gqa(q, k, v, sink):
    Bn, Lq, Hq, d = q.shape
    Hkv = k.shape[2]
    G = Hq // Hkv
    nb = Lq // BLOCK
    scale = d ** -0.5
    qb = q.reshape(Bn, nb, BLOCK, Hkv, G, d).transpose(1, 0, 2, 3, 4, 5)

    def one(qblk):
        s = jnp.einsum('bqhgd,bkhd->bhgqk', qblk, k) * scale
        if sink is not None:
            sl = jnp.broadcast_to(sink.reshape(Hkv, G)[None, :, :, None, None], s.shape[:-1] + (1,))
            p = jax.nn.softmax(jnp.concatenate([s, sl], axis=-1), axis=-1)[..., :-1]
        else:
            p = jax.nn.softmax(s, axis=-1)
        return jnp.einsum('bhgqk,bkhd->bqhgd', p, v)

    o = lax.map(one, qb)
    return o.transpose(1, 0, 2, 3, 4, 5).reshape(Bn, Lq, Hq * d)


def _windowed_gqa(q, k, v, kc, vc, sink):
    Bn, L, Hq, d = q.shape
    Hkv = k.shape[2]
    G = Hq // Hkv
    nb = L // BLOCK
    scale = d ** -0.5
    qb = q.reshape(Bn, nb, BLOCK, Hkv, G, d)

    def band(t):
        tb = t.reshape(Bn, nb, BLOCK, Hkv, d)
        z = jnp.zeros_like(tb[:, :1])
        prev = jnp.concatenate([z, tb[:, :-1]], axis=1)
        nxt = jnp.concatenate([tb[:, 1:], z], axis=1)
        return jnp.concatenate([prev, tb, nxt], axis=2)

    kband, vband = band(k), band(v)
    s_loc = jnp.einsum('bnqhgd,bnkhd->bnhgqk', qb, kband) * scale
    s_ctx = jnp.einsum('bnqhgd,bkhd->bnhgqk', qb, kc) * scale
    qpos = jnp.arange(nb)[:, None, None] * BLOCK + jnp.arange(BLOCK)[None, :, None]
    kpos = jnp.arange(nb)[:, None, None] * BLOCK - BLOCK + jnp.arange(3 * BLOCK)[None, None, :]
    mask = (jnp.abs(kpos - qpos) <= WINDOW) & (kpos >= 0) & (kpos < L)
    s_loc = jnp.where(mask[None, :, None, None], s_loc, NEG_INF)
    sl = jnp.broadcast_to(sink.reshape(Hkv, G)[None, None, :, :, None, None], s_loc.shape[:-1] + (1,))
    p = jax.nn.softmax(jnp.concatenate([s_loc, s_ctx, sl], axis=-1), axis=-1)
    nk = 3 * BLOCK
    lc = kc.shape[1]
    o = (jnp.einsum('bnhgqk,bnkhd->bnqhgd', p[..., :nk], vband)
         + jnp.einsum('bnhgqk,bkhd->bnqhgd', p[..., nk:nk + lc], vc))
    return o.reshape(Bn, L, Hq * d)


def _gla_chunked(q, k, v, logf, s0):
    Bn, L, H, _ = q.shape
    C = HGRN_CHUNK
    n = L // C
    ch = lambda t: t.reshape(Bn, n, C, H, t.shape[-1]).transpose(1, 0, 3, 2, 4)
    q, k, v, logf = ch(q), ch(k), ch(v), ch(logf)
    G = jnp.cumsum(logf, axis=3)
    G_last = G[:, :, :, -1:, :]
    q_in = q * jnp.exp(G)
    k_in = k * jnp.exp(-G)
    k_out = k * jnp.exp(G_last - G)
    causal = jnp.tril(jnp.ones((C, C), bool))
    A = jnp.where(causal, jnp.einsum('nbhtd,nbhsd->nbhts', q_in, k_in), 0.0)
    o_intra = jnp.einsum('nbhts,nbhsv->nbhtv', A, v)

    def step(S, xs):
        qi, ko, vi, oi, gl = xs
        o = oi + jnp.einsum('bhtd,bhdv->bhtv', qi, S)
        S = jnp.exp(gl)[:, :, 0, :, None] * S + jnp.einsum('bhsd,bhsv->bhdv', ko, vi)
        return S, o

    S, o = lax.scan(step, s0, (q_in, k_out, v, o_intra, G_last))
    return o.transpose(1, 0, 3, 2, 4).reshape(Bn, L, H, -1), S


def _delta_chunked(q, k, v, log_a, beta, s0):
    Bn, L, H, _ = q.shape
    C = DELTA_CHUNK
    n = L // C
    ch = lambda t: t.reshape(Bn, n, C, H, t.shape[-1]).transpose(1, 0, 3, 2, 4)
    chs = lambda t: t.reshape(Bn, n, C, H).transpose(1, 0, 3, 2)
    q, k, v = ch(q), ch(k), ch(v)
    log_a, beta = chs(log_a), chs(beta)
    g = jnp.cumsum(log_a, axis=-1)
    kb = k * beta[..., None]
    vb = v * beta[..., None]
    incl = jnp.tril(jnp.ones((C, C), bool))
    strict = jnp.tril(jnp.ones((C, C), bool), -1)
    decay = jnp.exp(jnp.where(incl, g[..., :, None] - g[..., None, :], -jnp.inf))
    Lm = jnp.where(strict, jnp.einsum('nbhid,nbhjd->nbhij', kb, k) * decay, 0.0)
    eye = jnp.eye(C, dtype=F32)
    T = lax.linalg.triangular_solve(eye + Lm, jnp.broadcast_to(eye, Lm.shape),
                                    left_side=True, lower=True, unit_diagonal=True)
    u = T @ vb
    w = T @ (kb * jnp.exp(g)[..., None])
    attn = jnp.einsum('nbhid,nbhjd->nbhij', q, k) * decay
    q_g = q * jnp.exp(g)[..., None]
    g_last = g[..., -1]
    k_end = k * jnp.exp(g_last[..., None] - g)[..., None]

    def step(S, xs):
        u_c, w_c, qg_c, at_c, ke_c, gl_c = xs
        v_new = u_c - w_c @ S
        o = qg_c @ S + at_c @ v_new
        S = jnp.exp(gl_c)[..., None, None] * S + jnp.einsum('bhcd,bhcv->bhdv', ke_c, v_new)
        return S, o

    S, o = lax.scan(step, s0, (u, w, q_g, attn, k_end, g_last))
    return o.transpose(1, 0, 3, 2, 4).reshape(Bn, L, H, -1), S


def _hgrn2(q, zf_f, zf_b, i, g, lb, onorm_g, s0):
    Bn, L, _ = q.shape
    hd = lambda t: t.reshape(Bn, L, C_HEADS, HEAD_DIM)
    lb = lb.reshape(C_HEADS, HEAD_DIM)

    def gates(zf):
        zf = hd(zf)
        f = lb + (1.0 - lb) * jax.nn.sigmoid(zf)
        return jnp.log(f), (1.0 - lb) * jax.nn.sigmoid(-zf)

    q, i = hd(q), hd(i)
    logf_f, k_f = gates(zf_f)
    logf_b, k_b = gates(zf_b)
    o_f, s_f = _gla_chunked(q, k_f, i, logf_f, s0[:, 0])
    o_b, s_b = _gla_chunked(_flip(q), _flip(k_b), _flip(i), _flip(logf_b), s0[:, 1])
    o = _rms_heads(o_f + _flip(o_b), onorm_g).reshape(Bn, L, BRANCH_W)
    return o * jax.nn.silu(g), jnp.stack([s_f, s_b], axis=1)


def _conv(x, w):
    pad = (CONV_K - 1) // 2
    return lax.conv_general_dilated(x, w[:, None, :], window_strides=(1,),
                                    padding=[(pad, pad)], dimension_numbers=('NWC', 'WIO', 'NWC'),
                                    feature_group_count=x.shape[-1])


def _gated_delta(qkv, za, zb, g, conv_w, a_log, dt_bias, onorm_g, s0):
    Bn, L, _ = qkv.shape
    qkv = jax.nn.silu(_conv(qkv, conv_w))
    q, k, v = [t.reshape(Bn, L, D_HEADS, HEAD_DIM) for t in jnp.split(qkv, 3, axis=-1)]
    q = _l2n(q) * HEAD_DIM ** -0.5
    k = _l2n(k)
    za = za.reshape(Bn, L, 2, D_HEADS)
    zb = zb.reshape(Bn, L, 2, D_HEADS)
    log_alpha = -jnp.exp(a_log) * jax.nn.softplus(za + dt_bias)
    beta = jax.nn.sigmoid(zb)
    o_f, s_f = _delta_chunked(q, k, v, log_alpha[:, :, 0], beta[:, :, 0], s0[:, 0])
    o_b, s_b = _delta_chunked(_flip(q), _flip(k), _flip(v), _flip(log_alpha[:, :, 1]),
                              _flip(beta[:, :, 1]), s0[:, 1])
    o = _rms_heads(o_f + _flip(o_b), onorm_g).reshape(Bn, L, BRANCH_W)
    return o * jax.nn.silu(g), jnp.stack([s_f, s_b], axis=1)


def _mixers(z, lw, ctx):
    Bn, L, _ = z.shape
    aq, ak, av = z[..., 0:256], z[..., 256:384], z[..., 384:512]
    bq, bk, bv = z[..., 512:768], z[..., 768:896], z[..., 896:1024]
    cq, cf_f, cf_b = z[..., 1024:1280], z[..., 1280:1536], z[..., 1536:1792]
    ci, cg = z[..., 1792:2048], z[..., 2048:2304]
    dqkv = z[..., 2304:3072]
    da, db = z[..., Z_DAB:Z_DAB + 8], z[..., Z_DAB + 8:Z_DAB + 16]
    dg = z[..., Z_DG:Z_DG + 256]
    heads = lambda t: t.reshape(Bn, L, -1, HEAD_DIM)
    aq, ak, av, bv = heads(aq), heads(ak), heads(av), heads(bv)
    bq = _rms_heads(heads(bq), lw['b_qnorm_g'])
    bk = _rms_heads(heads(bk), lw['b_knorm_g'])
    if ctx is None:
        s_c0 = jnp.zeros((Bn, 2, C_HEADS, HEAD_DIM, HEAD_DIM), F32)
        s_d0 = jnp.zeros((Bn, 2, D_HEADS, HEAD_DIM, HEAD_DIM), F32)
        o_a = _dense_gqa(aq, ak, av, lw['a_sink'])
        o_b = _dense_gqa(bq, bk, bv, None)
    else:
        ka, va, kbc, vbc, s_c0, s_d0 = ctx
        tabs = _rope_tables(L)
        o_a = _windowed_gqa(_rope(aq, tabs), _rope(ak, tabs), av, ka, va, lw['a_sink'])
        o_b = _dense_gqa(_rope(bq, tabs), jnp.concatenate([_rope(bk, tabs), kbc], axis=1),
                         jnp.concatenate([bv, vbc], axis=1), None)
    o_c, s_c = _hgrn2(cq, cf_f, cf_b, ci, cg, lw['lb'], lw['c_onorm_g'], s_c0)
    o_d, s_d = _gated_delta(dqkv, da, db, dg, lw['d_conv'], lw['d_a_log'], lw['d_dt_bias'],
                            lw['d_onorm_g'], s_d0)
    o = jnp.concatenate([o_a, o_b, o_c, o_d], axis=-1)
    return o, (ak, av, bk, bv, s_c, s_d)


def kernel(x_prompt, x_sample, cache_attn_a_k, cache_attn_a_v, cache_attn_b_k, cache_attn_b_v,
           state_hgrn, state_delta, c, c_ctx, norm1_g, norm2_g, w_ada, b_ada, w_in, a_sink,
           b_qnorm_g, b_knorm_g, c_lb, c_onorm_g, d_conv, d_a_log, d_dt_bias, d_onorm_g,
           w_branch, w_out, ffn_w1, ffn_w3, ffn_w2, router_w, router_b, moe_w1, moe_w3, moe_w2,
           final_norm_g):
    cum = jnp.cumsum(jax.nn.softmax(c_lb, axis=0), axis=0)
    lower_bounds = cum - cum[:1]

    x = jnp.concatenate([x_prompt.reshape(CTX_TOK, D_MODEL), x_sample.reshape(LAT_TOK, D_MODEL)], axis=0)
    cond = jnp.concatenate([c_ctx[None, :], c, jnp.zeros((16 - N_COND, D_MODEL), F32)], axis=0)

    rope_c, rope_s = rope_lane_tables(DEC_SEQ)
    caches = []
    for l in range(DEPTH):
        lw = {'a_sink': a_sink[l], 'b_qnorm_g': b_qnorm_g[l], 'b_knorm_g': b_knorm_g[l],
              'lb': lower_bounds[l], 'c_onorm_g': c_onorm_g[l], 'd_conv': d_conv[l], 'd_a_log': d_a_log[l],
              'd_dt_bias': d_dt_bias[l], 'd_onorm_g': d_onorm_g[l]}
        mod = ada_modulation(cond, w_ada[l], b_ada[l])[:N_COND].reshape(N_COND, 6, D_MODEL)
        w_mix = jnp.concatenate([w_in[l][:, :Z_DAB + 16], jnp.zeros((D_MODEL, 128 - 16), F32),
                                 w_in[l][:, Z_DAB + 16:W_IN_MIX]], axis=1).astype(BF16)
        w_gl = w_in[l][:, W_IN_MIX:].reshape(D_MODEL, N_BRANCH, D_MODEL).transpose(1, 0, 2).astype(BF16)
        z = input_projection(x, mod, norm1_g[l], w_mix)
        kv2 = lambda t: t.reshape(DEC_BATCH, PAST_LEN, 128)
        o_ab_ctx, bk_ctx = ctx_attention(z, a_sink[l], b_qnorm_g[l], b_knorm_g[l])
        o_a_lat = latent_attention_a(z, CTX_TOK, DEC_BATCH, DEC_SEQ, a_sink[l], kv2(cache_attn_a_k[:, l]),
                                     kv2(cache_attn_a_v[:, l]), rope_c, rope_s)
        o_b_lat = latent_attention_b(z, CTX_TOK, DEC_BATCH, DEC_SEQ, kv2(cache_attn_b_k[:, l]),
                                     kv2(cache_attn_b_v[:, l]), rope_c, rope_s, b_qnorm_g[l], b_knorm_g[l])
        o_c_ctx, sc_t = hgrn_mixer(z, 0, BATCH, SEQ, lower_bounds[l], c_onorm_g[l],
                                   jnp.zeros((BATCH, 2, C_HEADS, HEAD_DIM, HEAD_DIM), F32))
        o_c_lat, _ = hgrn_mixer(z, CTX_TOK, DEC_BATCH, DEC_SEQ, lower_bounds[l], c_onorm_g[l],
                                jnp.swapaxes(state_hgrn[:, l], -1, -2))

        def delta(zz, s0):
            return _gated_delta(zz[..., 2304:3072], zz[..., Z_DAB:Z_DAB + 8], zz[..., Z_DAB + 8:Z_DAB + 16],
                                zz[..., Z_DG:Z_DG + 256], lw['d_conv'], lw['d_a_log'], lw['d_dt_bias'],
                                lw['d_onorm_g'], s0)

        o_d_ctx, sd = delta(z[:CTX_TOK].reshape(BATCH, SEQ, Z_COLS),
                            jnp.zeros((BATCH, 2, D_HEADS, HEAD_DIM, HEAD_DIM), F32))
        o_d_lat, _ = delta(z[CTX_TOK:].reshape(DEC_BATCH, DEC_SEQ, Z_COLS), state_delta[:, l])
        kvh = lambda t: t.reshape(BATCH, SEQ, 2, HEAD_DIM)
        caches.append((kvh(z[:CTX_TOK, 256:384]), kvh(z[:CTX_TOK, 384:512]), kvh(bk_ctx), kvh(z[:CTX_TOK, 896:1024]),
                       jnp.swapaxes(sc_t, -1, -2), sd))
        o = jnp.concatenate([
            jnp.concatenate([o_ab_ctx, o_c_ctx, o_d_ctx.reshape(CTX_TOK, 256).astype(BF16)], axis=1),
            jnp.concatenate([o_a_lat, o_b_lat, o_c_lat, o_d_lat.reshape(LAT_TOK, 256).astype(BF16)], axis=1)], axis=0)
        x = merge_projection(x, mod, norm1_g[l], o, w_gl, w_branch[l].astype(BF16), w_out[l].astype(BF16))
        j = l // 2
        if l % 2 == 0:
            x = dense_ffn(x, mod, norm2_g[l], ffn_w1[j].astype(BF16), ffn_w3[j].astype(BF16),
                          ffn_w2[j].astype(BF16))
        else:
            rw = jnp.concatenate([router_w[j], jnp.zeros((D_MODEL, ROUTER_LANES - N_EXPERTS), F32)], axis=1)
            rb = jnp.concatenate([router_b[j], jnp.full((ROUTER_LANES - N_EXPERTS,), F32_MIN, F32)])[None, :]
            x = moe_ffn(x, mod, norm2_g[l], rw, rb, moe_w1[j].astype(BF16), moe_w3[j].astype(BF16),
                        moe_w2[j].astype(BF16))

    y = final_norm(x, final_norm_g)
    y_prompt = y[:CTX_TOK].reshape(BATCH, SEQ, D_MODEL)
    y_sample = y[CTX_TOK:].reshape(DEC_BATCH, DEC_SEQ, D_MODEL)
    stack = lambda idx: jnp.stack([caches[l][idx] for l in range(DEPTH)], axis=1)
    return (y_prompt, y_sample, stack(0), stack(1), stack(2), stack(3), stack(4), stack(5))
```

```python
import functools
import math

import jax
import jax.numpy as jnp
import numpy as np
from jax import lax
from jax.experimental import pallas as pl
from jax.experimental.pallas import tpu as pltpu

F32 = jnp.float32
BF16 = jnp.bfloat16

D_MODEL = 1024
BATCH = 32
SEQ = 256
DEPTH = 2
DEC_BATCH = 8
DEC_SEQ = 4096
PAST_LEN = 256
GRID_W = 64
HEAD_DIM = 64
A_HEADS = 4
A_KV_HEADS = 2
B_HEADS = 4
B_KV_HEADS = 2
C_HEADS = 4
D_HEADS = 4
BRANCH_W = 256
N_BRANCH = 4
WINDOW = 128
BLOCK = 128
ROPE_BASE = 10000.0
HGRN_CHUNK = 32
DELTA_CHUNK = 64
CONV_K = 5
D_FF = 2816
N_EXPERTS = 8
TOP_K = 2
D_FF_EXPERT = 3584
EPS = 1e-6
NEG_INF = -1e30
F32_MIN = float(np.finfo(np.float32).min)

CTX_TOK = BATCH * SEQ
LAT_TOK = DEC_BATCH * DEC_SEQ
N_TOK = CTX_TOK + LAT_TOK
N_COND = 1 + DEC_BATCH

Z_MAIN = 3072
Z_DG = Z_MAIN
Z_DAB = Z_DG + BRANCH_W
Z_COLS = Z_DAB + 128
W_IN_MIX = 3344

TM = 512
VMEM_LIMIT = 56 * 1024 * 1024


def _tile_cond(i, tm):
    ctx_tiles = CTX_TOK // tm
    per_b = DEC_SEQ // tm
    return jnp.where(i < ctx_tiles, 0, 1 + (i - ctx_tiles) // per_b)


def _rms(x, g):
    return x * lax.rsqrt(jnp.mean(x * x, axis=-1, keepdims=True) + EPS) * g


def _params(sem):
    return pltpu.CompilerParams(dimension_semantics=sem, vmem_limit_bytes=VMEM_LIMIT)


def _ada_kernel(c_ref, w_ref, b_ref, o_ref):
    c = c_ref[...]
    s = c * jax.nn.sigmoid(c)
    o_ref[...] = jnp.dot(s.astype(BF16), w_ref[...].astype(BF16), preferred_element_type=F32) + b_ref[...]


def ada_modulation(cond_pad, w, b):
    n = 6 * D_MODEL
    tn = 1536
    return pl.pallas_call(
        _ada_kernel,
        grid=(n // tn,),
        in_specs=[pl.BlockSpec((16, D_MODEL), lambda j: (0, 0)),
                  pl.BlockSpec((D_MODEL, tn), lambda j: (0, j)),
                  pl.BlockSpec((1, tn), lambda j: (0, j))],
        out_specs=pl.BlockSpec((16, tn), lambda j: (0, j)),
        out_shape=jax.ShapeDtypeStruct((16, n), F32),
        compiler_params=_params(("arbitrary",)),
        name="ada_modulation",
    )(cond_pad, w, b.reshape(1, n))


def _in_kernel(x_ref, mod_ref, g_ref, w_ref, z_ref):
    h = _rms(x_ref[...], g_ref[...]) * (1.0 + mod_ref[1:2, :]) + mod_ref[0:1, :]
    z_ref[...] = jnp.dot(h.astype(BF16), w_ref[...], preferred_element_type=F32)


def input_projection(x, mod, g, w):
    nt = N_TOK // TM
    return pl.pallas_call(
        _in_kernel,
        grid=(nt,),
        in_specs=[pl.BlockSpec((TM, D_MODEL), lambda i: (i, 0)),
                  pl.BlockSpec((None, 6, D_MODEL), lambda i: (_tile_cond(i, TM), 0, 0)),
                  pl.BlockSpec((1, D_MODEL), lambda i: (0, 0)),
                  pl.BlockSpec((D_MODEL, Z_COLS), lambda i: (0, 0))],
        out_specs=pl.BlockSpec((TM, Z_COLS), lambda i: (i, 0)),
        out_shape=jax.ShapeDtypeStruct((N_TOK, Z_COLS), F32),
        compiler_params=_params(("arbitrary",)),
        name="input_projection",
    )(x, mod, g.reshape(1, D_MODEL), w)


def _merge_kernel(x_ref, mod_ref, g_ref, o_ref, wgl_ref, wbr_ref, wout_ref, xo_ref):
    x = x_ref[...]
    h = (_rms(x, g_ref[...]) * (1.0 + mod_ref[1:2, :]) + mod_ref[0:1, :]).astype(BF16)
    merged = None
    for j in range(N_BRANCH):
        gate = jax.nn.sigmoid(jnp.dot(h, wgl_ref[j], preferred_element_type=F32))
        br = jnp.dot(o_ref[:, j * BRANCH_W:(j + 1) * BRANCH_W].astype(BF16), wbr_ref[j],
                     preferred_element_type=F32)
        merged = gate * br if merged is None else merged + gate * br
    mix = jnp.dot(merged.astype(BF16), wout_ref[...], preferred_element_type=F32)
    xo_ref[...] = x + mod_ref[2:3, :] * mix


def merge_projection(x, mod, g, o, wgl, wbr, wout):
    nt = N_TOK // TM
    return pl.pallas_call(
        _merge_kernel,
        grid=(nt,),
        in_specs=[pl.BlockSpec((TM, D_MODEL), lambda i: (i, 0)),
                  pl.BlockSpec((None, 6, D_MODEL), lambda i: (_tile_cond(i, TM), 0, 0)),
                  pl.BlockSpec((1, D_MODEL), lambda i: (0, 0)),
                  pl.BlockSpec((TM, N_BRANCH * BRANCH_W), lambda i: (i, 0)),
                  pl.BlockSpec((N_BRANCH, D_MODEL, D_MODEL), lambda i: (0, 0, 0)),
                  pl.BlockSpec((N_BRANCH, BRANCH_W, D_MODEL), lambda i: (0, 0, 0)),
                  pl.BlockSpec((D_MODEL, D_MODEL), lambda i: (0, 0))],
        out_specs=pl.BlockSpec((TM, D_MODEL), lambda i: (i, 0)),
        out_shape=jax.ShapeDtypeStruct((N_TOK, D_MODEL), F32),
        compiler_params=_params(("arbitrary",)),
        name="merge_projection",
    )(x, mod, g.reshape(1, D_MODEL), o, wgl, wbr, wout)


def _ffn_kernel(x_ref, mod_ref, g_ref, w1_ref, w3_ref, w2_ref, xo_ref):
    x = x_ref[...]
    h = (_rms(x, g_ref[...]) * (1.0 + mod_ref[4:5, :]) + mod_ref[3:4, :]).astype(BF16)
    a = jnp.dot(h, w1_ref[...], preferred_element_type=F32)
    b = jnp.dot(h, w3_ref[...], preferred_element_type=F32)
    hid = (a * jax.nn.sigmoid(a) * b).astype(BF16)
    f = jnp.dot(hid, w2_ref[...], preferred_element_type=F32)
    xo_ref[...] = x + mod_ref[5:6, :] * f


def dense_ffn(x, mod, g, w1, w3, w2):
    nt = N_TOK // TM
    const = lambda i: (0, 0)
    return pl.pallas_call(
        _ffn_kernel,
        grid=(nt,),
        in_specs=[pl.BlockSpec((TM, D_MODEL), lambda i: (i, 0)),
                  pl.BlockSpec((None, 6, D_MODEL), lambda i: (_tile_cond(i, TM), 0, 0)),
                  pl.BlockSpec((1, D_MODEL), const),
                  pl.BlockSpec((D_MODEL, D_FF), const, pipeline_mode=pl.Buffered(1)),
                  pl.BlockSpec((D_MODEL, D_FF), const, pipeline_mode=pl.Buffered(1)),
                  pl.BlockSpec((D_FF, D_MODEL), const, pipeline_mode=pl.Buffered(1))],
        out_specs=pl.BlockSpec((TM, D_MODEL), lambda i: (i, 0)),
        out_shape=jax.ShapeDtypeStruct((N_TOK, D_MODEL), F32),
        compiler_params=_params(("arbitrary",)),
        name="dense_ffn",
    )(x, mod, g.reshape(1, D_MODEL), w1, w3, w2)


MOE_TM = 1024
MOE_TF = 896
ROUTER_LANES = 128


def _moe_kernel(x_ref, mod_ref, g_ref, rw_ref, rb_ref, w1_ref, w3_ref, w2_ref, xo_ref,
                h_scr, we_scr, acc_scr):
    e = pl.program_id(1)
    f = pl.program_id(2)

    @pl.when((e == 0) & (f == 0))
    def _():
        h = _rms(x_ref[...], g_ref[...]) * (1.0 + mod_ref[4:5, :]) + mod_ref[3:4, :]
        h_scr[...] = h.astype(BF16)
        logits = jnp.dot(h, rw_ref[...], preferred_element_type=F32,
                         precision=lax.Precision.HIGHEST) + rb_ref[...]
        lane = lax.broadcasted_iota(jnp.int32, logits.shape, 1)
        m1 = jnp.max(logits, axis=-1, keepdims=True)
        i1 = jnp.min(jnp.where(logits == m1, lane, ROUTER_LANES), axis=-1, keepdims=True)
        rest = jnp.where(lane == i1, F32_MIN, logits)
        m2 = jnp.max(rest, axis=-1, keepdims=True)
        i2 = jnp.min(jnp.where(rest == m2, lane, ROUTER_LANES), axis=-1, keepdims=True)
        e2 = jnp.exp(m2 - m1)
        p1 = 1.0 / (1.0 + e2)
        p2 = e2 / (1.0 + e2)
        we_scr[...] = jnp.where(lane == i1, p1, 0.0) + jnp.where(lane == i2, p2, 0.0)
        acc_scr[...] = jnp.zeros_like(acc_scr)

    h = h_scr[...]
    a = jnp.dot(h, w1_ref[...], preferred_element_type=F32)
    b = jnp.dot(h, w3_ref[...], preferred_element_type=F32)
    lane = lax.broadcasted_iota(jnp.int32, (MOE_TM, ROUTER_LANES), 1)
    we = jnp.sum(jnp.where(lane == e, we_scr[...], 0.0), axis=-1, keepdims=True)
    hid = (a * jax.nn.sigmoid(a) * b * we).astype(BF16)
    acc_scr[...] += jnp.dot(hid, w2_ref[...], preferred_element_type=F32)

    @pl.when((e == N_EXPERTS - 1) & (f == pl.num_programs(2) - 1))
    def _():
        xo_ref[...] = x_ref[...] + mod_ref[5:6, :] * acc_scr[...]


def moe_ffn(x, mod, g, rw, rb, w1, w3, w2):
    nt = N_TOK // MOE_TM
    nf = D_FF_EXPERT // MOE_TF
    return pl.pallas_call(
        _moe_kernel,
        grid=(nt, N_EXPERTS, nf),
        in_specs=[pl.BlockSpec((MOE_TM, D_MODEL), lambda i, e, f: (i, 0)),
                  pl.BlockSpec((None, 6, D_MODEL), lambda i, e, f: (_tile_cond(i, MOE_TM), 0, 0)),
                  pl.BlockSpec((1, D_MODEL), lambda i, e, f: (0, 0)),
                  pl.BlockSpec((D_MODEL, ROUTER_LANES), lambda i, e, f: (0, 0)),
                  pl.BlockSpec((1, ROUTER_LANES), lambda i, e, f: (0, 0)),
                  pl.BlockSpec((None, D_MODEL, MOE_TF), lambda i, e, f: (e, 0, f)),
                  pl.BlockSpec((None, D_MODEL, MOE_TF), lambda i, e, f: (e, 0, f)),
                  pl.BlockSpec((None, MOE_TF, D_MODEL), lambda i, e, f: (e, f, 0))],
        out_specs=pl.BlockSpec((MOE_TM, D_MODEL), lambda i, e, f: (i, 0)),
        out_shape=jax.ShapeDtypeStruct((N_TOK, D_MODEL), F32),
        scratch_shapes=[pltpu.VMEM((MOE_TM, D_MODEL), BF16),
                        pltpu.VMEM((MOE_TM, ROUTER_LANES), F32),
                        pltpu.VMEM((MOE_TM, D_MODEL), F32)],
        compiler_params=_params(("arbitrary", "arbitrary", "arbitrary")),
        name="moe_ffn",
    )(x, mod, g.reshape(1, D_MODEL), rw, rb, w1, w3, w2)


def _final_kernel(x_ref, g_ref, o_ref):
    o_ref[...] = _rms(x_ref[...], g_ref[...])


def final_norm(x, g):
    tm = 1024
    return pl.pallas_call(
        _final_kernel,
        grid=(N_TOK // tm,),
        in_specs=[pl.BlockSpec((tm, D_MODEL), lambda i: (i, 0)),
                  pl.BlockSpec((1, D_MODEL), lambda i: (0, 0))],
        out_specs=pl.BlockSpec((tm, D_MODEL), lambda i: (i, 0)),
        out_shape=jax.ShapeDtypeStruct((N_TOK, D_MODEL), F32),
        compiler_params=_params(("arbitrary",)),
        name="final_norm",
    )(x, g.reshape(1, D_MODEL))


ATT_SCALE = HEAD_DIM ** -0.5
_NT = (((1,), (1,)), ((), ()))


def _head_rms(x, g_row):
    outs = []
    for h in range(x.shape[1] // HEAD_DIM):
        xh = x[:, h * HEAD_DIM:(h + 1) * HEAD_DIM]
        outs.append(xh * lax.rsqrt(jnp.mean(xh * xh, axis=-1, keepdims=True) + EPS) * g_row)
    return jnp.concatenate(outs, axis=-1)


def _rope_apply(x, c, s):
    w = x.shape[-1]
    lane = lax.broadcasted_iota(jnp.int32, x.shape, 1)
    first_half = ((lane // (HEAD_DIM // 4)) % 2) == 0
    partner = jnp.where(first_half, pltpu.roll(x, w - HEAD_DIM // 4, 1), pltpu.roll(x, HEAD_DIM // 4, 1))
    return x * c + partner * s


def rope_lane_tables(L):
    rows = L // GRID_W
    row = jnp.repeat(jnp.arange(rows, dtype=F32), GRID_W)
    col = jnp.tile(jnp.arange(GRID_W, dtype=F32), rows)
    n_freq = HEAD_DIM // 4
    inv = ROPE_BASE ** (-jnp.arange(n_freq, dtype=F32) / n_freq)
    ang = jnp.stack([row, col], 0)[:, :, None] * inv
    cos, sin = jnp.cos(ang), jnp.sin(ang)
    c = jnp.concatenate([cos[0], cos[0], cos[1], cos[1]], axis=-1)
    s = jnp.concatenate([-sin[0], sin[0], -sin[1], sin[1]], axis=-1)
    return jnp.tile(c, (1, 4)), jnp.tile(s, (1, 4))


def _attend(q, k, v, extra=None, sink=None, mask=None):
    s = lax.dot_general(q.astype(BF16), k, _NT, preferred_element_type=F32) * ATT_SCALE
    if mask is not None:
        s = jnp.where(mask, s, NEG_INF)
    m = jnp.max(s, axis=-1, keepdims=True)
    if extra is not None:
        s2 = lax.dot_general(q.astype(BF16), extra[0], _NT, preferred_element_type=F32) * ATT_SCALE
        m = jnp.maximum(m, jnp.max(s2, axis=-1, keepdims=True))
    if sink is not None:
        m = jnp.maximum(m, sink)
    p = jnp.exp(s - m)
    den = jnp.sum(p, axis=-1, keepdims=True)
    o = jnp.dot(p.astype(BF16), v, preferred_element_type=F32)
    if extra is not None:
        p2 = jnp.exp(s2 - m)
        den = den + jnp.sum(p2, axis=-1, keepdims=True)
        o = o + jnp.dot(p2.astype(BF16), extra[1], preferred_element_type=F32)
    if sink is not None:
        den = den + jnp.exp(sink - m)
    return o / den


def _ctx_attn_kernel(sink_ref, z_ref, gq_ref, gk_ref, o_ref, bk_ref):
    z = z_ref[...]
    bq = _head_rms(z[:, 512:768], gq_ref[...])
    bk = _head_rms(z[:, 768:896], gk_ref[...])
    bk_ref[...] = bk
    groups = ((z[:, 0:256], z[:, 256:384], z[:, 384:512], True),
              (bq, bk, z[:, 896:1024], False))
    outs = []
    for q_all, k_all, v_all, use_sink in groups:
        k_all = k_all.astype(BF16)
        v_all = v_all.astype(BF16)
        for hq in range(A_HEADS):
            kv = hq // (A_HEADS // A_KV_HEADS)
            sl = slice(kv * HEAD_DIM, (kv + 1) * HEAD_DIM)
            outs.append(_attend(q_all[:, hq * HEAD_DIM:(hq + 1) * HEAD_DIM], k_all[:, sl], v_all[:, sl],
                                sink=sink_ref[hq] if use_sink else None))
    o_ref[...] = jnp.concatenate(outs, axis=-1).astype(o_ref.dtype)


def ctx_attention(z, sink, gq, gk):
    return pl.pallas_call(
        _ctx_attn_kernel,
        grid=(BATCH,),
        in_specs=[pl.BlockSpec(memory_space=pltpu.SMEM),
                  pl.BlockSpec((SEQ, 1024), lambda b: (b, 0)),
                  pl.BlockSpec((1, HEAD_DIM), lambda b: (0, 0)),
                  pl.BlockSpec((1, HEAD_DIM), lambda b: (0, 0))],
        out_specs=[pl.BlockSpec((SEQ, 512), lambda b: (b, 0)),
                   pl.BlockSpec((SEQ, 128), lambda b: (b, 0))],
        out_shape=[jax.ShapeDtypeStruct((CTX_TOK, 512), BF16),
                   jax.ShapeDtypeStruct((CTX_TOK, 128), F32)],
        compiler_params=_params(("arbitrary",)),
        name="ctx_attention",
    )(sink, z, gq.reshape(1, HEAD_DIM), gk.reshape(1, HEAD_DIM))


LB_TQ = 256


def _lat_b_kernel(zq_ref, zkv_ref, ck_ref, cv_ref, cq_ref, sq_ref, ckk_ref, skk_ref, gq_ref, gk_ref,
                  o_ref, k_scr, v_scr):
    L = zkv_ref.shape[0]

    @pl.when(pl.program_id(1) == 0)
    def _():
        kv = zkv_ref[...]
        bk = _rope_apply(_head_rms(kv[:, :128], gk_ref[...]), ckk_ref[...], skk_ref[...])
        k_scr[0:L, :] = bk.astype(BF16)
        k_scr[L:L + PAST_LEN, :] = ck_ref[...].astype(BF16)
        v_scr[0:L, :] = kv[:, 128:].astype(BF16)
        v_scr[L:L + PAST_LEN, :] = cv_ref[...].astype(BF16)

    q = _rope_apply(_head_rms(zq_ref[...], gq_ref[...]), cq_ref[...], sq_ref[...])
    outs = []
    for hq in range(B_HEADS):
        kv = hq // (B_HEADS // B_KV_HEADS)
        sl = slice(kv * HEAD_DIM, (kv + 1) * HEAD_DIM)
        outs.append(_attend(q[:, hq * HEAD_DIM:(hq + 1) * HEAD_DIM], k_scr[:, sl], v_scr[:, sl]))
    o_ref[...] = jnp.concatenate(outs, axis=-1).astype(o_ref.dtype)


def latent_attention_b(z, row0, nb, L, cache_k, cache_v, rope_c, rope_s, gq, gk):
    nq = L // LB_TQ
    return pl.pallas_call(
        _lat_b_kernel,
        grid=(nb, nq),
        in_specs=[pl.BlockSpec((LB_TQ, 256), lambda b, i: (row0 // LB_TQ + b * nq + i, 2)),
                  pl.BlockSpec((L, 256), lambda b, i: (row0 // L + b, 3)),
                  pl.BlockSpec((None, PAST_LEN, 128), lambda b, i: (b, 0, 0)),
                  pl.BlockSpec((None, PAST_LEN, 128), lambda b, i: (b, 0, 0)),
                  pl.BlockSpec((LB_TQ, 256), lambda b, i: (i, 0)),
                  pl.BlockSpec((LB_TQ, 256), lambda b, i: (i, 0)),
                  pl.BlockSpec((L, 128), lambda b, i: (0, 0)),
                  pl.BlockSpec((L, 128), lambda b, i: (0, 0)),
                  pl.BlockSpec((1, HEAD_DIM), lambda b, i: (0, 0)),
                  pl.BlockSpec((1, HEAD_DIM), lambda b, i: (0, 0))],
        out_specs=pl.BlockSpec((LB_TQ, 256), lambda b, i: (b * nq + i, 0)),
        out_shape=jax.ShapeDtypeStruct((nb * L, 256), BF16),
        scratch_shapes=[pltpu.VMEM((L + PAST_LEN, 128), BF16),
                        pltpu.VMEM((L + PAST_LEN, 128), BF16)],
        compiler_params=_params(("arbitrary", "arbitrary")),
        name="latent_attention_b",
    )(z, z, cache_k, cache_v, rope_c, rope_s, rope_c, rope_s, gq.reshape(1, HEAD_DIM), gk.reshape(1, HEAD_DIM))


def _lat_a_kernel(sink_ref, zq_ref, zkv_ref, ck_ref, cv_ref, cq_ref, sq_ref, ckk_ref, skk_ref,
                  o_ref, k_scr, v_scr, ck_scr, cv_scr):
    L = zkv_ref.shape[0]
    i = pl.program_id(1)

    @pl.when(i == 0)
    def _():
        kv = zkv_ref[...]
        zeros = jnp.zeros((BLOCK, 128), BF16)
        k_scr[0:BLOCK, :] = zeros
        v_scr[0:BLOCK, :] = zeros
        k_scr[BLOCK:BLOCK + L, :] = _rope_apply(kv[:, :128], ckk_ref[...], skk_ref[...]).astype(BF16)
        v_scr[BLOCK:BLOCK + L, :] = kv[:, 128:].astype(BF16)
        k_scr[BLOCK + L:2 * BLOCK + L, :] = zeros
        v_scr[BLOCK + L:2 * BLOCK + L, :] = zeros
        ck_scr[...] = ck_ref[...].astype(BF16)
        cv_scr[...] = cv_ref[...].astype(BF16)

    q = _rope_apply(zq_ref[...], cq_ref[...], sq_ref[...])
    start = pl.multiple_of(i * BLOCK, BLOCK)
    kband = k_scr[pl.ds(start, 3 * BLOCK), :]
    vband = v_scr[pl.ds(start, 3 * BLOCK), :]
    r = lax.broadcasted_iota(jnp.int32, (BLOCK, 3 * BLOCK), 0)
    cidx = lax.broadcasted_iota(jnp.int32, (BLOCK, 3 * BLOCK), 1)
    kpos = i * BLOCK - BLOCK + cidx
    mask = (jnp.abs(cidx - BLOCK - r) <= WINDOW) & (kpos >= 0) & (kpos < L)
    outs = []
    for hq in range(A_HEADS):
        kv = hq // (A_HEADS // A_KV_HEADS)
        sl = slice(kv * HEAD_DIM, (kv + 1) * HEAD_DIM)
        outs.append(_attend(q[:, hq * HEAD_DIM:(hq + 1) * HEAD_DIM], kband[:, sl], vband[:, sl],
                            extra=(ck_scr[:, sl], cv_scr[:, sl]), sink=sink_ref[hq], mask=mask))
    o_ref[...] = jnp.concatenate(outs, axis=-1).astype(o_ref.dtype)


def latent_attention_a(z, row0, nb, L, sink, cache_k, cache_v, rope_c, rope_s):
    nq = L // BLOCK
    return pl.pallas_call(
        _lat_a_kernel,
        grid=(nb, nq),
        in_specs=[pl.BlockSpec(memory_space=pltpu.SMEM),
                  pl.BlockSpec((BLOCK, 256), lambda b, i: (row0 // BLOCK + b * nq + i, 0)),
                  pl.BlockSpec((L, 256), lambda b, i: (row0 // L + b, 1)),
                  pl.BlockSpec((None, PAST_LEN, 128), lambda b, i: (b, 0, 0)),
                  pl.BlockSpec((None, PAST_LEN, 128), lambda b, i: (b, 0, 0)),
                  pl.BlockSpec((BLOCK, 256), lambda b, i: (i, 0)),
                  pl.BlockSpec((BLOCK, 256), lambda b, i: (i, 0)),
                  pl.BlockSpec((L, 128), lambda b, i: (0, 0)),
                  pl.BlockSpec((L, 128), lambda b, i: (0, 0))],
        out_specs=pl.BlockSpec((BLOCK, 256), lambda b, i: (b * nq + i, 0)),
        out_shape=jax.ShapeDtypeStruct((nb * L, 256), BF16),
        scratch_shapes=[pltpu.VMEM((L + 2 * BLOCK, 128), BF16),
                        pltpu.VMEM((L + 2 * BLOCK, 128), BF16),
                        pltpu.VMEM((PAST_LEN, 128), BF16),
                        pltpu.VMEM((PAST_LEN, 128), BF16)],
        compiler_params=_params(("arbitrary", "arbitrary")),
        name="latent_attention_a",
    )(sink, z, z, cache_k, cache_v, rope_c, rope_s, rope_c, rope_s)


_TN = (((0,), (0,)), ((), ()))


def _hgrn_kernel(zq_ref, zf_ref, zi_ref, zg_ref, lb_ref, gn_ref, s0_ref, o_ref, sT_ref,
                 of_scr, ob_scr, logf_scr, k_scr, S_scr, *, tt):
    d = pl.program_id(1)
    j = pl.program_id(2)
    n_t = pl.num_programs(2)
    C = HGRN_CHUNK
    n_c = tt // C

    @pl.when(j == 0)
    def _():
        S_scr[...] = s0_ref[...]

    lb = lb_ref[...]
    zf = zf_ref[...]
    logf_scr[...] = jnp.log(lb + (1.0 - lb) * jax.nn.sigmoid(zf))
    k_scr[...] = (1.0 - lb) * jax.nn.sigmoid(-zf)

    row = lax.broadcasted_iota(jnp.int32, (C, C), 0)
    col = lax.broadcasted_iota(jnp.int32, (C, C), 1)

    def run(reverse, tile):
        tri = (row <= col) if reverse else (row >= col)
        cum = tri.astype(F32)

        def body(ci, carry):
            c = (n_c - 1 - ci) if reverse else ci
            r0 = pl.multiple_of(c * C, C)
            G = jnp.dot(cum, logf_scr[pl.ds(r0, C), :], precision=lax.Precision.HIGHEST,
                        preferred_element_type=F32)
            G_end = G[0:1, :] if reverse else G[C - 1:C, :]
            kc = k_scr[pl.ds(r0, C), :]
            vc = zi_ref[pl.ds(r0, C), :].astype(BF16)
            q_in = (zq_ref[pl.ds(r0, C), :] * jnp.exp(G)).astype(BF16)
            k_in = (kc * jnp.exp(-G)).astype(BF16)
            k_out = (kc * jnp.exp(G_end - G)).astype(BF16)
            decay = jnp.exp(G_end)
            outs = []
            for h in range(C_HEADS):
                sl = slice(h * HEAD_DIM, (h + 1) * HEAD_DIM)
                a = lax.dot_general(q_in[:, sl], k_in[:, sl], _NT, preferred_element_type=F32)
                a = jnp.where(tri, a, 0.0).astype(BF16)
                s_t = S_scr[h]
                o_h = (jnp.dot(a, vc[:, sl], preferred_element_type=F32)
                       + lax.dot_general(q_in[:, sl], s_t.astype(BF16), _NT, preferred_element_type=F32))
                S_scr[h] = s_t * decay[:, sl] + lax.dot_general(vc[:, sl], k_out[:, sl], _TN,
                                                                preferred_element_type=F32)
                outs.append(o_h)
            o_c = jnp.concatenate(outs, axis=-1)
            if reverse:
                ob_scr[pl.ds(r0, C), :] = o_c
            else:
                of_scr[pl.ds(pl.multiple_of(tile * tt, tt) + r0, C), :] = o_c
            return carry

        lax.fori_loop(0, n_c, body, 0)

    @pl.when(d == 0)
    def _():
        run(False, j)

    @pl.when(d == 1)
    def _():
        tile = n_t - 1 - j
        run(True, tile)
        o = of_scr[pl.ds(pl.multiple_of(tile * tt, tt), tt), :] + ob_scr[...]
        g = zg_ref[...]
        o_ref[...] = (_head_rms(o, gn_ref[...]) * (g * jax.nn.sigmoid(g))).astype(o_ref.dtype)

    @pl.when(j == n_t - 1)
    def _():
        sT_ref[...] = S_scr[...]


def hgrn_mixer(z, row0, nb, L, lb, gn, s0_t):
    tt = min(L, 512)
    n_t = L // tt
    rb = row0 // tt

    def tile(d, j):
        return jnp.where(d == 0, j, n_t - 1 - j)

    def late(d, j):
        return jnp.where(d == 0, n_t - 1, n_t - 1 - j)

    st_spec = pl.BlockSpec((None, None, C_HEADS, HEAD_DIM, HEAD_DIM), lambda b, d, j: (b, d, 0, 0, 0))
    return pl.pallas_call(
        functools.partial(_hgrn_kernel, tt=tt),
        grid=(nb, 2, n_t),
        in_specs=[pl.BlockSpec((tt, 256), lambda b, d, j: (rb + b * n_t + tile(d, j), 4)),
                  pl.BlockSpec((tt, 256), lambda b, d, j: (rb + b * n_t + tile(d, j), 5 + d)),
                  pl.BlockSpec((tt, 256), lambda b, d, j: (rb + b * n_t + tile(d, j), 7)),
                  pl.BlockSpec((tt, 256), lambda b, d, j: (rb + b * n_t + late(d, j), 8)),
                  pl.BlockSpec((1, 256), lambda b, d, j: (0, 0)),
                  pl.BlockSpec((1, HEAD_DIM), lambda b, d, j: (0, 0)),
                  st_spec],
        out_specs=[pl.BlockSpec((tt, 256), lambda b, d, j: (b * n_t + late(d, j), 0)),
                   st_spec],
        out_shape=[jax.ShapeDtypeStruct((nb * L, 256), BF16),
                   jax.ShapeDtypeStruct((nb, 2, C_HEADS, HEAD_DIM, HEAD_DIM), F32)],
        scratch_shapes=[pltpu.VMEM((L, 256), F32),
                        pltpu.VMEM((tt, 256), F32),
                        pltpu.VMEM((tt, 256), F32),
                        pltpu.VMEM((tt, 256), F32),
                        pltpu.VMEM((C_HEADS, HEAD_DIM, HEAD_DIM), F32)],
        compiler_params=_params(("arbitrary", "arbitrary", "arbitrary")),
        name="hgrn_mixer",
    )(z, z, z, z, lb.reshape(1, 256), gn.reshape(1, HEAD_DIM), s0_t)


DL_C = DELTA_CHUNK
DL_PREP_TT = 256
DL_HALO = 8
DL_CHUNK_TT = 512
N_QKV_HEADS = 3 * D_HEADS


def _delta_prep_kernel(x_ref, xp_ref, xn_ref, zab_ref, cw_ref, na_ref, dtb_ref, qkv_ref, gate_ref, xs_scr):
    tt = x_ref.shape[0]
    row = pl.program_id(0) * tt
    lat = row - CTX_TOK
    first = jnp.where(row < CTX_TOK, True, lat % DEC_SEQ == 0)
    last = jnp.where(row < CTX_TOK, True, (lat + tt) % DEC_SEQ == 0)
    xs_scr[DL_HALO:DL_HALO + tt, :] = x_ref[...]
    xs_scr[0:DL_HALO, :] = jnp.where(first, 0.0, xp_ref[...])
    xs_scr[DL_HALO + tt:2 * DL_HALO + tt, :] = jnp.where(last, 0.0, xn_ref[...])
    pad = (CONV_K - 1) // 2
    y = None
    for t in range(CONV_K):
        term = xs_scr[pl.ds(DL_HALO - pad + t, tt), :] * cw_ref[t:t + 1, :]
        y = term if y is None else y + term
    y = y * jax.nn.sigmoid(y)
    for idx in range(N_QKV_HEADS):
        xh = y[:, idx * HEAD_DIM:(idx + 1) * HEAD_DIM]
        if idx < 2 * D_HEADS:
            xh = xh * lax.rsqrt(jnp.sum(xh * xh, axis=-1, keepdims=True) + EPS)
        if idx < D_HEADS:
            xh = xh * ATT_SCALE
        qkv_ref[idx] = xh
    zab = zab_ref[...]
    lane = lax.broadcasted_iota(jnp.int32, zab.shape, 1)
    t_ = zab + dtb_ref[...]
    softplus = jnp.maximum(t_, 0.0) + jnp.log(1.0 + jnp.exp(-jnp.abs(t_)))
    gate_ref[...] = jnp.where(lane < 2 * D_HEADS, na_ref[...] * softplus, jax.nn.sigmoid(zab))


def delta_prep(z, conv_w, a_log, dt_bias):
    tt = DL_PREP_TT
    hb = tt // DL_HALO
    n_hb = N_TOK // DL_HALO
    pad8 = lambda v: jnp.concatenate([v.reshape(1, 2 * D_HEADS), jnp.zeros((1, 128 - 2 * D_HEADS), F32)], axis=1)
    return pl.pallas_call(
        _delta_prep_kernel,
        grid=(N_TOK // tt,),
        in_specs=[pl.BlockSpec((tt, 768), lambda i: (i, 3)),
                  pl.BlockSpec((DL_HALO, 768), lambda i: (jnp.maximum(i * hb - 1, 0), 3)),
                  pl.BlockSpec((DL_HALO, 768), lambda i: (jnp.minimum((i + 1) * hb, n_hb - 1), 3)),
                  pl.BlockSpec((tt, 128), lambda i: (i, Z_DAB // 128)),
                  pl.BlockSpec((CONV_K, 768), lambda i: (0, 0)),
                  pl.BlockSpec((1, 128), lambda i: (0, 0)),
                  pl.BlockSpec((1, 128), lambda i: (0, 0))],
        out_specs=[pl.BlockSpec((N_QKV_HEADS, tt, HEAD_DIM), lambda i: (0, i, 0)),
                   pl.BlockSpec((tt, 128), lambda i: (i, 0))],
        out_shape=[jax.ShapeDtypeStruct((N_QKV_HEADS, N_TOK, HEAD_DIM), F32),
                   jax.ShapeDtypeStruct((N_TOK, 128), F32)],
        scratch_shapes=[pltpu.VMEM((tt + 2 * DL_HALO, 768), F32)],
        compiler_params=_params(("arbitrary",)),
        name="delta_prep",
    )(z, z, z, z, conv_w, pad8(-jnp.exp(a_log)), pad8(dt_bias))


def _split_bf16(a):
    hi = a.astype(BF16)
    return hi, (a - hi.astype(F32)).astype(BF16)


def _dot_hl(a, b_parts):
    a_hi, a_lo = _split_bf16(a)
    b_hi, b_lo = b_parts
    m = a.shape[0]
    r = jnp.dot(jnp.concatenate([a_hi, a_lo], axis=0), b_hi, preferred_element_type=F32)
    return r[:m] + r[m:] + jnp.dot(a_hi, b_lo, preferred_element_type=F32)


def _delta_chunk_kernel(qkv_ref, gate_ref, u2_ref, wq_ref, ak_ref):
    C = DL_C
    n_c = gate_ref.shape[0] // C
    row = lax.broadcasted_iota(jnp.int32, (C, C), 0)
    col = lax.broadcasted_iota(jnp.int32, (C, C), 1)
    eye = (row == col).astype(F32)

    def body(c, carry):
        r0 = pl.multiple_of(c * C, C)
        ga = gate_ref[pl.ds(r0, C), :]
        for d in range(2):
            incl = (row >= col) if d == 0 else (row <= col)
            strict = (row > col) if d == 0 else (row < col)
            g_all = jnp.dot(incl.astype(F32), ga, precision=lax.Precision.HIGHEST, preferred_element_type=F32)
            g_all_t = g_all.T
            for h in range(D_HEADS):
                ci = d * D_HEADS + h
                q = qkv_ref[h, pl.ds(r0, C), :]
                k = qkv_ref[D_HEADS + h, pl.ds(r0, C), :]
                v = qkv_ref[2 * D_HEADS + h, pl.ds(r0, C), :]
                g_col = g_all[:, ci:ci + 1]
                g_row = g_all_t[ci:ci + 1, :]
                beta = ga[:, 2 * D_HEADS + ci:2 * D_HEADS + ci + 1]
                g_end = g_col[C - 1:C, :] if d == 0 else g_col[0:1, :]
                kb = k * beta
                vb = v * beta
                eg = jnp.exp(g_col)
                decay = jnp.where(incl, jnp.exp(jnp.where(incl, g_col - g_row, 0.0)), 0.0)
                kq = jnp.concatenate([kb, q], axis=0).astype(BF16)
                r = lax.dot_general(kq, k.astype(BF16), _NT, preferred_element_type=F32)
                x = -jnp.where(strict, r[:C] * decay, 0.0)
                attn = r[C:] * decay
                t = eye + x
                p = x
                for _ in range(5):
                    p = _dot_hl(p, _split_bf16(p))
                    t = t + _dot_hl(t, _split_bf16(p))
                rhs = jnp.concatenate([vb, kb * eg], axis=1).astype(BF16)
                uw = jnp.dot(t.astype(BF16), rhs, preferred_element_type=F32)
                ke_t = (k * jnp.exp(g_end - g_col)).T
                u2_ref[d, h, pl.ds(r0, C), :] = jnp.concatenate(
                    [uw[:, :C], jnp.broadcast_to(jnp.exp(g_end), (C, C))], axis=1)
                wq_ref[d, h, c] = jnp.concatenate([uw[:, C:], q * eg], axis=0).astype(BF16)
                ak_ref[d, h, c] = jnp.concatenate([attn, ke_t], axis=0).astype(BF16)
        return carry

    lax.fori_loop(0, n_c, body, 0)


def delta_chunks(qkv, gates):
    tt = DL_CHUNK_TT
    n_c = tt // DL_C
    return pl.pallas_call(
        _delta_chunk_kernel,
        grid=(N_TOK // tt,),
        in_specs=[pl.BlockSpec((N_QKV_HEADS, tt, HEAD_DIM), lambda i: (0, i, 0)),
                  pl.BlockSpec((tt, 128), lambda i: (i, 0))],
        out_specs=[pl.BlockSpec((2, D_HEADS, tt, 128), lambda i: (0, 0, i, 0)),
                   pl.BlockSpec((2, D_HEADS, n_c, 2 * DL_C, HEAD_DIM), lambda i: (0, 0, i, 0, 0)),
                   pl.BlockSpec((2, D_HEADS, n_c, 2 * DL_C, HEAD_DIM), lambda i: (0, 0, i, 0, 0))],
        out_shape=[jax.ShapeDtypeStruct((2, D_HEADS, N_TOK, 128), F32),
                   jax.ShapeDtypeStruct((2, D_HEADS, N_TOK // DL_C, 2 * DL_C, HEAD_DIM), BF16),
                   jax.ShapeDtypeStruct((2, D_HEADS, N_TOK // DL_C, 2 * DL_C, HEAD_DIM), BF16)],
        compiler_params=_params(("arbitrary",)),
        name="delta_chunks",
    )(qkv, gates)


def _delta_scan_kernel(u2f_ref, wqf_ref, akf_ref, u2b_ref, wqb_ref, akb_ref, s0_ref,
                       of_ref, ob_ref, s_ref, s_scr):
    j = pl.program_id(1)
    C = DL_C
    n_c = wqf_ref.shape[1]

    @pl.when(j == 0)
    def _():
        s_scr[...] = s0_ref[...]

    def body(ci, carry):
        for d, (u2_ref, wq_ref, ak_ref, o_ref) in enumerate(((u2f_ref, wqf_ref, akf_ref, of_ref),
                                                            (u2b_ref, wqb_ref, akb_ref, ob_ref))):
            c = ci if d == 0 else n_c - 1 - ci
            r0 = pl.multiple_of(c * C, C)
            for h in range(D_HEADS):
                s = s_scr[d, h]
                u2 = u2_ref[h, pl.ds(r0, C), :]
                r1 = jnp.dot(wq_ref[h, c], s.astype(BF16), preferred_element_type=F32)
                v_new = u2[:, :C] - r1[:C]
                r2 = jnp.dot(ak_ref[h, c], v_new.astype(BF16), preferred_element_type=F32)
                o_ref[h, pl.ds(r0, C), :] = r1[C:] + r2[:C]
                s_scr[d, h] = s * u2[0:1, C:] + r2[C:]
        return carry

    lax.fori_loop(0, n_c, body, 0)

    @pl.when(j == pl.num_programs(1) - 1)
    def _():
        s_ref[...] = s_scr[...]


def delta_scan(u2, wq, ak, row0, nb, L, s0):
    tt = min(L, 512)
    n_t = L // tt
    n_c = tt // DL_C
    rb = row0 // tt
    fwd = lambda b, j: rb + b * n_t + j
    bwd = lambda b, j: rb + b * n_t + (n_t - 1 - j)
    u_spec = lambda d, f: pl.BlockSpec((None, D_HEADS, tt, 128), lambda b, j: (d, 0, f(b, j), 0))
    c_spec = lambda d, f: pl.BlockSpec((None, D_HEADS, n_c, 2 * DL_C, HEAD_DIM), lambda b, j: (d, 0, f(b, j), 0, 0))
    st_spec = pl.BlockSpec((None, 2, D_HEADS, HEAD_DIM, HEAD_DIM), lambda b, j: (b, 0, 0, 0, 0))
    return pl.pallas_call(
        _delta_scan_kernel,
        grid=(nb, n_t),
        in_specs=[u_spec(0, fwd), c_spec(0, fwd), c_spec(0, fwd),
                  u_spec(1, bwd), c_spec(1, bwd), c_spec(1, bwd), st_spec],
        out_specs=[pl.BlockSpec((D_HEADS, tt, HEAD_DIM), lambda b, j: (0, b * n_t + j, 0)),
                   pl.BlockSpec((D_HEADS, tt, HEAD_DIM), lambda b, j: (0, b * n_t + (n_t - 1 - j), 0)),
                   st_spec],
        out_shape=[jax.ShapeDtypeStruct((D_HEADS, nb * L, HEAD_DIM), F32),
                   jax.ShapeDtypeStruct((D_HEADS, nb * L, HEAD_DIM), F32),
                   jax.ShapeDtypeStruct((nb, 2, D_HEADS, HEAD_DIM, HEAD_DIM), F32)],
        scratch_shapes=[pltpu.VMEM((2, D_HEADS, HEAD_DIM, HEAD_DIM), F32)],
        compiler_params=_params(("arbitrary", "arbitrary")),
        name="delta_scan",
    )(u2, wq, ak, u2, wq, ak, s0)


def _delta_out_kernel(of_ref, ob_ref, zg_ref, gn_ref, o_ref):
    outs = []
    for h in range(D_HEADS):
        o = of_ref[h] + ob_ref[h]
        outs.append(o * lax.rsqrt(jnp.mean(o * o, axis=-1, keepdims=True) + EPS) * gn_ref[...])
    g = zg_ref[...]
    o_ref[...] = (jnp.concatenate(outs, axis=-1) * (g * jax.nn.sigmoid(g))).astype(o_ref.dtype)


def delta_output(o_f, o_b, z, row0, gn):
    n = o_f.shape[1]
    tt = 256
    return pl.pallas_call(
        _delta_out_kernel,
        grid=(n // tt,),
        in_specs=[pl.BlockSpec((D_HEADS, tt, HEAD_DIM), lambda i: (0, i, 0)),
                  pl.BlockSpec((D_HEADS, tt, HEAD_DIM), lambda i: (0, i, 0)),
                  pl.BlockSpec((tt, 256), lambda i: (row0 // tt + i, Z_DG // 256)),
                  pl.BlockSpec((1, HEAD_DIM), lambda i: (0, 0))],
        out_specs=pl.BlockSpec((tt, 256), lambda i: (i, 0)),
        out_shape=jax.ShapeDtypeStruct((n, 256), BF16),
        compiler_params=_params(("arbitrary",)),
        name="delta_output",
    )(o_f, o_b, z, gn.reshape(1, HEAD_DIM))


def _rms_heads(x, g):
    return x * lax.rsqrt(jnp.mean(x * x, axis=-1, keepdims=True) + EPS) * g


def _l2n(x):
    return x * lax.rsqrt(jnp.sum(x * x, axis=-1, keepdims=True) + EPS)


def _flip(t):
    return jnp.flip(t, axis=1)


def _rope_tables(L):
    rows = L // GRID_W
    row = jnp.repeat(jnp.arange(rows, dtype=F32), GRID_W)
    col = jnp.tile(jnp.arange(GRID_W, dtype=F32), rows)
    n_freq = HEAD_DIM // 4
    inv = ROPE_BASE ** (-jnp.arange(n_freq, dtype=F32) / n_freq)
    ang = jnp.stack([row, col], 0)[:, :, None] * inv
    return jnp.cos(ang), jnp.sin(ang)


def _rope(x, tables):
    cos, sin = tables
    half = HEAD_DIM // 2
    nf = HEAD_DIM // 4
    outs = []
    for a in range(2):
        xa = x[..., a * half:(a + 1) * half]
        x1, x2 = xa[..., :nf], xa[..., nf:]
        ca = cos[a][None, :, None, :]
        sa = sin[a][None, :, None, :]
        outs += [x1 * ca - x2 * sa, x2 * ca + x1 * sa]
    return jnp.concatenate(outs, axis=-1)


def _dense_gqa(q, k, v, sink):
    Bn, Lq, Hq, d = q.shape
    Hkv = k.shape[2]
    G = Hq // Hkv
    nb = Lq // BLOCK
    scale = d ** -0.5
    qb = q.reshape(Bn, nb, BLOCK, Hkv, G, d).transpose(1, 0, 2, 3, 4, 5)

    def one(qblk):
        s = jnp.einsum('bqhgd,bkhd->bhgqk', qblk, k) * scale
        if sink is not None:
            sl = jnp.broadcast_to(sink.reshape(Hkv, G)[None, :, :, None, None], s.shape[:-1] + (1,))
            p = jax.nn.softmax(jnp.concatenate([s, sl], axis=-1), axis=-1)[..., :-1]
        else:
            p = jax.nn.softmax(s, axis=-1)
        return jnp.einsum('bhgqk,bkhd->bqhgd', p, v)

    o = lax.map(one, qb)
    return o.transpose(1, 0, 2, 3, 4, 5).reshape(Bn, Lq, Hq * d)


def _windowed_gqa(q, k, v, kc, vc, sink):
    Bn, L, Hq, d = q.shape
    Hkv = k.shape[2]
    G = Hq // Hkv
    nb = L // BLOCK
    scale = d ** -0.5
    qb = q.reshape(Bn, nb, BLOCK, Hkv, G, d)

    def band(t):
        tb = t.reshape(Bn, nb, BLOCK, Hkv, d)
        z = jnp.zeros_like(tb[:, :1])
        prev = jnp.concatenate([z, tb[:, :-1]], axis=1)
        nxt = jnp.concatenate([tb[:, 1:], z], axis=1)
        return jnp.concatenate([prev, tb, nxt], axis=2)

    kband, vband = band(k), band(v)
    s_loc = jnp.einsum('bnqhgd,bnkhd->bnhgqk', qb, kband) * scale
    s_ctx = jnp.einsum('bnqhgd,bkhd->bnhgqk', qb, kc) * scale
    qpos = jnp.arange(nb)[:, None, None] * BLOCK + jnp.arange(BLOCK)[None, :, None]
    kpos = jnp.arange(nb)[:, None, None] * BLOCK - BLOCK + jnp.arange(3 * BLOCK)[None, None, :]
    mask = (jnp.abs(kpos - qpos) <= WINDOW) & (kpos >= 0) & (kpos < L)
    s_loc = jnp.where(mask[None, :, None, None], s_loc, NEG_INF)
    sl = jnp.broadcast_to(sink.reshape(Hkv, G)[None, None, :, :, None, None], s_loc.shape[:-1] + (1,))
    p = jax.nn.softmax(jnp.concatenate([s_loc, s_ctx, sl], axis=-1), axis=-1)
    nk = 3 * BLOCK
    lc = kc.shape[1]
    o = (jnp.einsum('bnhgqk,bnkhd->bnqhgd', p[..., :nk], vband)
         + jnp.einsum('bnhgqk,bkhd->bnqhgd', p[..., nk:nk + lc], vc))
    return o.reshape(Bn, L, Hq * d)


def _gla_chunked(q, k, v, logf, s0):
    Bn, L, H, _ = q.shape
    C = HGRN_CHUNK
    n = L // C
    ch = lambda t: t.reshape(Bn, n, C, H, t.shape[-1]).transpose(1, 0, 3, 2, 4)
    q, k, v, logf = ch(q), ch(k), ch(v), ch(logf)
    G = jnp.cumsum(logf, axis=3)
    G_last = G[:, :, :, -1:, :]
    q_in = q * jnp.exp(G)
    k_in = k * jnp.exp(-G)
    k_out = k * jnp.exp(G_last - G)
    causal = jnp.tril(jnp.ones((C, C), bool))
    A = jnp.where(causal, jnp.einsum('nbhtd,nbhsd->nbhts', q_in, k_in), 0.0)
    o_intra = jnp.einsum('nbhts,nbhsv->nbhtv', A, v)

    def step(S, xs):
        qi, ko, vi, oi, gl = xs
        o = oi + jnp.einsum('bhtd,bhdv->bhtv', qi, S)
        S = jnp.exp(gl)[:, :, 0, :, None] * S + jnp.einsum('bhsd,bhsv->bhdv', ko, vi)
        return S, o

    S, o = lax.scan(step, s0, (q_in, k_out, v, o_intra, G_last))
    return o.transpose(1, 0, 3, 2, 4).reshape(Bn, L, H, -1), S


def _delta_chunked(q, k, v, log_a, beta, s0):
    Bn, L, H, _ = q.shape
    C = DELTA_CHUNK
    n = L // C
    ch = lambda t: t.reshape(Bn, n, C, H, t.shape[-1]).transpose(1, 0, 3, 2, 4)
    chs = lambda t: t.reshape(Bn, n, C, H).transpose(1, 0, 3, 2)
    q, k, v = ch(q), ch(k), ch(v)
    log_a, beta = chs(log_a), chs(beta)
    g = jnp.cumsum(log_a, axis=-1)
    kb = k * beta[..., None]
    vb = v * beta[..., None]
    incl = jnp.tril(jnp.ones((C, C), bool))
    strict = jnp.tril(jnp.ones((C, C), bool), -1)
    decay = jnp.exp(jnp.where(incl, g[..., :, None] - g[..., None, :], -jnp.inf))
    Lm = jnp.where(strict, jnp.einsum('nbhid,nbhjd->nbhij', kb, k) * decay, 0.0)
    eye = jnp.eye(C, dtype=F32)
    T = lax.linalg.triangular_solve(eye + Lm, jnp.broadcast_to(eye, Lm.shape),
                                    left_side=True, lower=True, unit_diagonal=True)
    u = T @ vb
    w = T @ (kb * jnp.exp(g)[..., None])
    attn = jnp.einsum('nbhid,nbhjd->nbhij', q, k) * decay
    q_g = q * jnp.exp(g)[..., None]
    g_last = g[..., -1]
    k_end = k * jnp.exp(g_last[..., None] - g)[..., None]

    def step(S, xs):
        u_c, w_c, qg_c, at_c, ke_c, gl_c = xs
        v_new = u_c - w_c @ S
        o = qg_c @ S + at_c @ v_new
        S = jnp.exp(gl_c)[..., None, None] * S + jnp.einsum('bhcd,bhcv->bhdv', ke_c, v_new)
        return S, o

    S, o = lax.scan(step, s0, (u, w, q_g, attn, k_end, g_last))
    return o.transpose(1, 0, 3, 2, 4).reshape(Bn, L, H, -1), S


def _hgrn2(q, zf_f, zf_b, i, g, lb, onorm_g, s0):
    Bn, L, _ = q.shape
    hd = lambda t: t.reshape(Bn, L, C_HEADS, HEAD_DIM)
    lb = lb.reshape(C_HEADS, HEAD_DIM)

    def gates(zf):
        zf = hd(zf)
        f = lb + (1.0 - lb) * jax.nn.sigmoid(zf)
        return jnp.log(f), (1.0 - lb) * jax.nn.sigmoid(-zf)

    q, i = hd(q), hd(i)
    logf_f, k_f = gates(zf_f)
    logf_b, k_b = gates(zf_b)
    o_f, s_f = _gla_chunked(q, k_f, i, logf_f, s0[:, 0])
    o_b, s_b = _gla_chunked(_flip(q), _flip(k_b), _flip(i), _flip(logf_b), s0[:, 1])
    o = _rms_heads(o_f + _flip(o_b), onorm_g).reshape(Bn, L, BRANCH_W)
    return o * jax.nn.silu(g), jnp.stack([s_f, s_b], axis=1)


def _conv(x, w):
    pad = (CONV_K - 1) // 2
    return lax.conv_general_dilated(x, w[:, None, :], window_strides=(1,),
                                    padding=[(pad, pad)], dimension_numbers=('NWC', 'WIO', 'NWC'),
                                    feature_group_count=x.shape[-1])


def _gated_delta(qkv, za, zb, g, conv_w, a_log, dt_bias, onorm_g, s0):
    Bn, L, _ = qkv.shape
    qkv = jax.nn.silu(_conv(qkv, conv_w))
    q, k, v = [t.reshape(Bn, L, D_HEADS, HEAD_DIM) for t in jnp.split(qkv, 3, axis=-1)]
    q = _l2n(q) * HEAD_DIM ** -0.5
    k = _l2n(k)
    za = za.reshape(Bn, L, 2, D_HEADS)
    zb = zb.reshape(Bn, L, 2, D_HEADS)
    log_alpha = -jnp.exp(a_log) * jax.nn.softplus(za + dt_bias)
    beta = jax.nn.sigmoid(zb)
    o_f, s_f = _delta_chunked(q, k, v, log_alpha[:, :, 0], beta[:, :, 0], s0[:, 0])
    o_b, s_b = _delta_chunked(_flip(q), _flip(k), _flip(v), _flip(log_alpha[:, :, 1]),
                              _flip(beta[:, :, 1]), s0[:, 1])
    o = _rms_heads(o_f + _flip(o_b), onorm_g).reshape(Bn, L, BRANCH_W)
    return o * jax.nn.silu(g), jnp.stack([s_f, s_b], axis=1)


def _mixers(z, lw, ctx):
    Bn, L, _ = z.shape
    aq, ak, av = z[..., 0:256], z[..., 256:384], z[..., 384:512]
    bq, bk, bv = z[..., 512:768], z[..., 768:896], z[..., 896:1024]
    cq, cf_f, cf_b = z[..., 1024:1280], z[..., 1280:1536], z[..., 1536:1792]
    ci, cg = z[..., 1792:2048], z[..., 2048:2304]
    dqkv = z[..., 2304:3072]
    da, db = z[..., Z_DAB:Z_DAB + 8], z[..., Z_DAB + 8:Z_DAB + 16]
    dg = z[..., Z_DG:Z_DG + 256]
    heads = lambda t: t.reshape(Bn, L, -1, HEAD_DIM)
    aq, ak, av, bv = heads(aq), heads(ak), heads(av), heads(bv)
    bq = _rms_heads(heads(bq), lw['b_qnorm_g'])
    bk = _rms_heads(heads(bk), lw['b_knorm_g'])
    if ctx is None:
        s_c0 = jnp.zeros((Bn, 2, C_HEADS, HEAD_DIM, HEAD_DIM), F32)
        s_d0 = jnp.zeros((Bn, 2, D_HEADS, HEAD_DIM, HEAD_DIM), F32)
        o_a = _dense_gqa(aq, ak, av, lw['a_sink'])
        o_b = _dense_gqa(bq, bk, bv, None)
    else:
        ka, va, kbc, vbc, s_c0, s_d0 = ctx
        tabs = _rope_tables(L)
        o_a = _windowed_gqa(_rope(aq, tabs), _rope(ak, tabs), av, ka, va, lw['a_sink'])
        o_b = _dense_gqa(_rope(bq, tabs), jnp.concatenate([_rope(bk, tabs), kbc], axis=1),
                         jnp.concatenate([bv, vbc], axis=1), None)
    o_c, s_c = _hgrn2(cq, cf_f, cf_b, ci, cg, lw['lb'], lw['c_onorm_g'], s_c0)
    o_d, s_d = _gated_delta(dqkv, da, db, dg, lw['d_conv'], lw['d_a_log'], lw['d_dt_bias'],
                            lw['d_onorm_g'], s_d0)
    o = jnp.concatenate([o_a, o_b, o_c, o_d], axis=-1)
    return o, (ak, av, bk, bv, s_c, s_d)


def kernel(x_prompt, x_sample, cache_attn_a_k, cache_attn_a_v, cache_attn_b_k, cache_attn_b_v,
           state_hgrn, state_delta, c, c_ctx, norm1_g, norm2_g, w_ada, b_ada, w_in, a_sink,
           b_qnorm_g, b_knorm_g, c_lb, c_onorm_g, d_conv, d_a_log, d_dt_bias, d_onorm_g,
           w_branch, w_out, ffn_w1, ffn_w3, ffn_w2, router_w, router_b, moe_w1, moe_w3, moe_w2,
           final_norm_g):
    cum = jnp.cumsum(jax.nn.softmax(c_lb, axis=0), axis=0)
    lower_bounds = cum - cum[:1]

    x = jnp.concatenate([x_prompt.reshape(CTX_TOK, D_MODEL), x_sample.reshape(LAT_TOK, D_MODEL)], axis=0)
    cond = jnp.concatenate([c_ctx[None, :], c, jnp.zeros((16 - N_COND, D_MODEL), F32)], axis=0)

    rope_c, rope_s = rope_lane_tables(DEC_SEQ)
    caches = []
    for l in range(DEPTH):
        lw = {'a_sink': a_sink[l], 'b_qnorm_g': b_qnorm_g[l], 'b_knorm_g': b_knorm_g[l],
              'lb': lower_bounds[l], 'c_onorm_g': c_onorm_g[l], 'd_conv': d_conv[l], 'd_a_log': d_a_log[l],
              'd_dt_bias': d_dt_bias[l], 'd_onorm_g': d_onorm_g[l]}
        mod = ada_modulation(cond, w_ada[l], b_ada[l])[:N_COND].reshape(N_COND, 6, D_MODEL)
        w_mix = jnp.concatenate([w_in[l][:, :Z_MAIN], w_in[l][:, Z_MAIN + 16:W_IN_MIX],
                                 w_in[l][:, Z_MAIN:Z_MAIN + 16], jnp.zeros((D_MODEL, 128 - 16), F32)],
                                axis=1).astype(BF16)
        w_gl = w_in[l][:, W_IN_MIX:].reshape(D_MODEL, N_BRANCH, D_MODEL).transpose(1, 0, 2).astype(BF16)
        z = input_projection(x, mod, norm1_g[l], w_mix)
        kv2 = lambda t: t.reshape(DEC_BATCH, PAST_LEN, 128)
        o_ab_ctx, bk_ctx = ctx_attention(z, a_sink[l], b_qnorm_g[l], b_knorm_g[l])
        o_a_lat = latent_attention_a(z, CTX_TOK, DEC_BATCH, DEC_SEQ, a_sink[l], kv2(cache_attn_a_k[:, l]),
                                     kv2(cache_attn_a_v[:, l]), rope_c, rope_s)
        o_b_lat = latent_attention_b(z, CTX_TOK, DEC_BATCH, DEC_SEQ, kv2(cache_attn_b_k[:, l]),
                                     kv2(cache_attn_b_v[:, l]), rope_c, rope_s, b_qnorm_g[l], b_knorm_g[l])
        o_c_ctx, sc_t = hgrn_mixer(z, 0, BATCH, SEQ, lower_bounds[l], c_onorm_g[l],
                                   jnp.zeros((BATCH, 2, C_HEADS, HEAD_DIM, HEAD_DIM), F32))
        o_c_lat, _ = hgrn_mixer(z, CTX_TOK, DEC_BATCH, DEC_SEQ, lower_bounds[l], c_onorm_g[l],
                                jnp.swapaxes(state_hgrn[:, l], -1, -2))

        qkv, gates = delta_prep(z, d_conv[l], d_a_log[l], d_dt_bias[l])
        u2, wq, ak = delta_chunks(qkv, gates)
        of_ctx, ob_ctx, sd = delta_scan(u2, wq, ak, 0, BATCH, SEQ,
                                        jnp.zeros((BATCH, 2, D_HEADS, HEAD_DIM, HEAD_DIM), F32))
        of_lat, ob_lat, _ = delta_scan(u2, wq, ak, CTX_TOK, DEC_BATCH, DEC_SEQ, state_delta[:, l])
        o_d_ctx = delta_output(of_ctx, ob_ctx, z, 0, d_onorm_g[l])
        o_d_lat = delta_output(of_lat, ob_lat, z, CTX_TOK, d_onorm_g[l])
        kvh = lambda t: t.reshape(BATCH, SEQ, 2, HEAD_DIM)
        caches.append((kvh(z[:CTX_TOK, 256:384]), kvh(z[:CTX_TOK, 384:512]), kvh(bk_ctx), kvh(z[:CTX_TOK, 896:1024]),
                       jnp.swapaxes(sc_t, -1, -2), sd))
        o = jnp.concatenate([jnp.concatenate([o_ab_ctx, o_c_ctx, o_d_ctx], axis=1),
                             jnp.concatenate([o_a_lat, o_b_lat, o_c_lat, o_d_lat], axis=1)], axis=0)
        x = merge_projection(x, mod, norm1_g[l], o, w_gl, w_branch[l].astype(BF16), w_out[l].astype(BF16))
        j = l // 2
        if l % 2 == 0:
            x = dense_ffn(x, mod, norm2_g[l], ffn_w1[j].astype(BF16), ffn_w3[j].astype(BF16),
                          ffn_w2[j].astype(BF16))
        else:
            rw = jnp.concatenate([router_w[j], jnp.zeros((D_MODEL, ROUTER_LANES - N_EXPERTS), F32)], axis=1)
            rb = jnp.concatenate([router_b[j], jnp.full((ROUTER_LANES - N_EXPERTS,), F32_MIN, F32)])[None, :]
            x = moe_ffn(x, mod, norm2_g[l], rw, rb, moe_w1[j].astype(BF16), moe_w3[j].astype(BF16),
                        moe_w2[j].astype(BF16))

    y = final_norm(x, final_norm_g)
    y_prompt = y[:CTX_TOK].reshape(BATCH, SEQ, D_MODEL)
    y_sample = y[CTX_TOK:].reshape(DEC_BATCH, DEC_SEQ, D_MODEL)
    stack = lambda idx: jnp.stack([caches[l][idx] for l in range(DEPTH)], axis=1)
    return (y_prompt, y_sample, stack(0), stack(1), stack(2), stack(3), stack(4), stack(5))
```

```python
import functools
import math

import jax
import jax.numpy as jnp
import numpy as np
from jax import lax
from jax.experimental import pallas as pl
from jax.experimental.pallas import tpu as pltpu

F32 = jnp.float32
BF16 = jnp.bfloat16

D_MODEL = 1024
BATCH = 32
SEQ = 256
DEPTH = 2
DEC_BATCH = 8
DEC_SEQ = 4096
PAST_LEN = 256
GRID_W = 64
HEAD_DIM = 64
A_HEADS = 4
A_KV_HEADS = 2
B_HEADS = 4
B_KV_HEADS = 2
C_HEADS = 4
D_HEADS = 4
BRANCH_W = 256
N_BRANCH = 4
WINDOW = 128
BLOCK = 128
ROPE_BASE = 10000.0
HGRN_CHUNK = 32
DELTA_CHUNK = 64
CONV_K = 5
D_FF = 2816
N_EXPERTS = 8
TOP_K = 2
D_FF_EXPERT = 3584
EPS = 1e-6
NEG_INF = -1e30
F32_MIN = float(np.finfo(np.float32).min)

CTX_TOK = BATCH * SEQ
LAT_TOK = DEC_BATCH * DEC_SEQ
N_TOK = CTX_TOK + LAT_TOK
N_COND = 1 + DEC_BATCH

Z_MAIN = 3072
Z_DG = Z_MAIN
Z_DAB = Z_DG + BRANCH_W
Z_COLS = Z_DAB + 128
W_IN_MIX = 3344

TM = 512
VMEM_LIMIT = 56 * 1024 * 1024


def _tile_cond(i, tm):
    ctx_tiles = CTX_TOK // tm
    per_b = DEC_SEQ // tm
    return jnp.where(i < ctx_tiles, 0, 1 + (i - ctx_tiles) // per_b)


def _rms(x, g):
    return x * lax.rsqrt(jnp.mean(x * x, axis=-1, keepdims=True) + EPS) * g


def _params(sem):
    return pltpu.CompilerParams(dimension_semantics=sem, vmem_limit_bytes=VMEM_LIMIT)


def _ada_kernel(c_ref, w_ref, b_ref, o_ref):
    c = c_ref[...]
    s = c * jax.nn.sigmoid(c)
    o_ref[...] = jnp.dot(s.astype(BF16), w_ref[...].astype(BF16), preferred_element_type=F32) + b_ref[...]


def ada_modulation(cond_pad, w, b):
    n = 6 * D_MODEL
    tn = 1536
    return pl.pallas_call(
        _ada_kernel,
        grid=(n // tn,),
        in_specs=[pl.BlockSpec((16, D_MODEL), lambda j: (0, 0)),
                  pl.BlockSpec((D_MODEL, tn), lambda j: (0, j)),
                  pl.BlockSpec((1, tn), lambda j: (0, j))],
        out_specs=pl.BlockSpec((16, tn), lambda j: (0, j)),
        out_shape=jax.ShapeDtypeStruct((16, n), F32),
        compiler_params=_params(("arbitrary",)),
        name="ada_modulation",
    )(cond_pad, w, b.reshape(1, n))


def _in_kernel(x_ref, mod_ref, g_ref, w_ref, z_ref):
    h = _rms(x_ref[...], g_ref[...]) * (1.0 + mod_ref[1:2, :]) + mod_ref[0:1, :]
    z_ref[...] = jnp.dot(h.astype(BF16), w_ref[...], preferred_element_type=F32)


def input_projection(x, mod, g, w):
    nt = N_TOK // TM
    return pl.pallas_call(
        _in_kernel,
        grid=(nt,),
        in_specs=[pl.BlockSpec((TM, D_MODEL), lambda i: (i, 0)),
                  pl.BlockSpec((None, 6, D_MODEL), lambda i: (_tile_cond(i, TM), 0, 0)),
                  pl.BlockSpec((1, D_MODEL), lambda i: (0, 0)),
                  pl.BlockSpec((D_MODEL, Z_COLS), lambda i: (0, 0))],
        out_specs=pl.BlockSpec((TM, Z_COLS), lambda i: (i, 0)),
        out_shape=jax.ShapeDtypeStruct((N_TOK, Z_COLS), F32),
        compiler_params=_params(("arbitrary",)),
        name="input_projection",
    )(x, mod, g.reshape(1, D_MODEL), w)


def _merge_kernel(x_ref, mod_ref, g_ref, o_ref, wgl_ref, wbr_ref, wout_ref, xo_ref):
    x = x_ref[...]
    h = (_rms(x, g_ref[...]) * (1.0 + mod_ref[1:2, :]) + mod_ref[0:1, :]).astype(BF16)
    merged = None
    for j in range(N_BRANCH):
        gate = jax.nn.sigmoid(jnp.dot(h, wgl_ref[j], preferred_element_type=F32))
        br = jnp.dot(o_ref[:, j * BRANCH_W:(j + 1) * BRANCH_W].astype(BF16), wbr_ref[j],
                     preferred_element_type=F32)
        merged = gate * br if merged is None else merged + gate * br
    mix = jnp.dot(merged.astype(BF16), wout_ref[...], preferred_element_type=F32)
    xo_ref[...] = x + mod_ref[2:3, :] * mix


def merge_projection(x, mod, g, o, wgl, wbr, wout):
    nt = N_TOK // TM
    return pl.pallas_call(
        _merge_kernel,
        grid=(nt,),
        in_specs=[pl.BlockSpec((TM, D_MODEL), lambda i: (i, 0)),
                  pl.BlockSpec((None, 6, D_MODEL), lambda i: (_tile_cond(i, TM), 0, 0)),
                  pl.BlockSpec((1, D_MODEL), lambda i: (0, 0)),
                  pl.BlockSpec((TM, N_BRANCH * BRANCH_W), lambda i: (i, 0)),
                  pl.BlockSpec((N_BRANCH, D_MODEL, D_MODEL), lambda i: (0, 0, 0)),
                  pl.BlockSpec((N_BRANCH, BRANCH_W, D_MODEL), lambda i: (0, 0, 0)),
                  pl.BlockSpec((D_MODEL, D_MODEL), lambda i: (0, 0))],
        out_specs=pl.BlockSpec((TM, D_MODEL), lambda i: (i, 0)),
        out_shape=jax.ShapeDtypeStruct((N_TOK, D_MODEL), F32),
        compiler_params=_params(("arbitrary",)),
        name="merge_projection",
    )(x, mod, g.reshape(1, D_MODEL), o, wgl, wbr, wout)


def _ffn_kernel(x_ref, mod_ref, g_ref, w1_ref, w3_ref, w2_ref, xo_ref):
    x = x_ref[...]
    h = (_rms(x, g_ref[...]) * (1.0 + mod_ref[4:5, :]) + mod_ref[3:4, :]).astype(BF16)
    a = jnp.dot(h, w1_ref[...], preferred_element_type=F32)
    b = jnp.dot(h, w3_ref[...], preferred_element_type=F32)
    hid = (a * jax.nn.sigmoid(a) * b).astype(BF16)
    f = jnp.dot(hid, w2_ref[...], preferred_element_type=F32)
    xo_ref[...] = x + mod_ref[5:6, :] * f


def dense_ffn(x, mod, g, w1, w3, w2):
    nt = N_TOK // TM
    const = lambda i: (0, 0)
    return pl.pallas_call(
        _ffn_kernel,
        grid=(nt,),
        in_specs=[pl.BlockSpec((TM, D_MODEL), lambda i: (i, 0)),
                  pl.BlockSpec((None, 6, D_MODEL), lambda i: (_tile_cond(i, TM), 0, 0)),
                  pl.BlockSpec((1, D_MODEL), const),
                  pl.BlockSpec((D_MODEL, D_FF), const, pipeline_mode=pl.Buffered(1)),
                  pl.BlockSpec((D_MODEL, D_FF), const, pipeline_mode=pl.Buffered(1)),
                  pl.BlockSpec((D_FF, D_MODEL), const, pipeline_mode=pl.Buffered(1))],
        out_specs=pl.BlockSpec((TM, D_MODEL), lambda i: (i, 0)),
        out_shape=jax.ShapeDtypeStruct((N_TOK, D_MODEL), F32),
        compiler_params=_params(("arbitrary",)),
        name="dense_ffn",
    )(x, mod, g.reshape(1, D_MODEL), w1, w3, w2)


MOE_TM = 1024
MOE_TF = 896
ROUTER_LANES = 128


def _moe_kernel(x_ref, mod_ref, g_ref, rw_ref, rb_ref, w1_ref, w3_ref, w2_ref, xo_ref,
                h_scr, we_scr, acc_scr):
    e = pl.program_id(1)
    f = pl.program_id(2)

    @pl.when((e == 0) & (f == 0))
    def _():
        h = _rms(x_ref[...], g_ref[...]) * (1.0 + mod_ref[4:5, :]) + mod_ref[3:4, :]
        h_scr[...] = h.astype(BF16)
        logits = jnp.dot(h, rw_ref[...], preferred_element_type=F32,
                         precision=lax.Precision.HIGHEST) + rb_ref[...]
        lane = lax.broadcasted_iota(jnp.int32, logits.shape, 1)
        m1 = jnp.max(logits, axis=-1, keepdims=True)
        i1 = jnp.min(jnp.where(logits == m1, lane, ROUTER_LANES), axis=-1, keepdims=True)
        rest = jnp.where(lane == i1, F32_MIN, logits)
        m2 = jnp.max(rest, axis=-1, keepdims=True)
        i2 = jnp.min(jnp.where(rest == m2, lane, ROUTER_LANES), axis=-1, keepdims=True)
        e2 = jnp.exp(m2 - m1)
        p1 = 1.0 / (1.0 + e2)
        p2 = e2 / (1.0 + e2)
        we_scr[...] = jnp.where(lane == i1, p1, 0.0) + jnp.where(lane == i2, p2, 0.0)
        acc_scr[...] = jnp.zeros_like(acc_scr)

    h = h_scr[...]
    a = jnp.dot(h, w1_ref[...], preferred_element_type=F32)
    b = jnp.dot(h, w3_ref[...], preferred_element_type=F32)
    lane = lax.broadcasted_iota(jnp.int32, (MOE_TM, ROUTER_LANES), 1)
    we = jnp.sum(jnp.where(lane == e, we_scr[...], 0.0), axis=-1, keepdims=True)
    hid = (a * jax.nn.sigmoid(a) * b * we).astype(BF16)
    acc_scr[...] += jnp.dot(hid, w2_ref[...], preferred_element_type=F32)

    @pl.when((e == N_EXPERTS - 1) & (f == pl.num_programs(2) - 1))
    def _():
        xo_ref[...] = x_ref[...] + mod_ref[5:6, :] * acc_scr[...]


def moe_ffn(x, mod, g, rw, rb, w1, w3, w2):
    nt = N_TOK // MOE_TM
    nf = D_FF_EXPERT // MOE_TF
    return pl.pallas_call(
        _moe_kernel,
        grid=(nt, N_EXPERTS, nf),
        in_specs=[pl.BlockSpec((MOE_TM, D_MODEL), lambda i, e, f: (i, 0)),
                  pl.BlockSpec((None, 6, D_MODEL), lambda i, e, f: (_tile_cond(i, MOE_TM), 0, 0)),
                  pl.BlockSpec((1, D_MODEL), lambda i, e, f: (0, 0)),
                  pl.BlockSpec((D_MODEL, ROUTER_LANES), lambda i, e, f: (0, 0)),
                  pl.BlockSpec((1, ROUTER_LANES), lambda i, e, f: (0, 0)),
                  pl.BlockSpec((None, D_MODEL, MOE_TF), lambda i, e, f: (e, 0, f)),
                  pl.BlockSpec((None, D_MODEL, MOE_TF), lambda i, e, f: (e, 0, f)),
                  pl.BlockSpec((None, MOE_TF, D_MODEL), lambda i, e, f: (e, f, 0))],
        out_specs=pl.BlockSpec((MOE_TM, D_MODEL), lambda i, e, f: (i, 0)),
        out_shape=jax.ShapeDtypeStruct((N_TOK, D_MODEL), F32),
        scratch_shapes=[pltpu.VMEM((MOE_TM, D_MODEL), BF16),
                        pltpu.VMEM((MOE_TM, ROUTER_LANES), F32),
                        pltpu.VMEM((MOE_TM, D_MODEL), F32)],
        compiler_params=_params(("arbitrary", "arbitrary", "arbitrary")),
        name="moe_ffn",
    )(x, mod, g.reshape(1, D_MODEL), rw, rb, w1, w3, w2)


def _final_kernel(x_ref, g_ref, o_ref):
    o_ref[...] = _rms(x_ref[...], g_ref[...])


def final_norm(x, g):
    tm = 1024
    return pl.pallas_call(
        _final_kernel,
        grid=(N_TOK // tm,),
        in_specs=[pl.BlockSpec((tm, D_MODEL), lambda i: (i, 0)),
                  pl.BlockSpec((1, D_MODEL), lambda i: (0, 0))],
        out_specs=pl.BlockSpec((tm, D_MODEL), lambda i: (i, 0)),
        out_shape=jax.ShapeDtypeStruct((N_TOK, D_MODEL), F32),
        compiler_params=_params(("arbitrary",)),
        name="final_norm",
    )(x, g.reshape(1, D_MODEL))


ATT_SCALE = HEAD_DIM ** -0.5
_NT = (((1,), (1,)), ((), ()))


def _head_rms(x, g_row):
    outs = []
    for h in range(x.shape[1] // HEAD_DIM):
        xh = x[:, h * HEAD_DIM:(h + 1) * HEAD_DIM]
        outs.append(xh * lax.rsqrt(jnp.mean(xh * xh, axis=-1, keepdims=True) + EPS) * g_row)
    return jnp.concatenate(outs, axis=-1)


def _rope_apply(x, c, s):
    w = x.shape[-1]
    lane = lax.broadcasted_iota(jnp.int32, x.shape, 1)
    first_half = ((lane // (HEAD_DIM // 4)) % 2) == 0
    partner = jnp.where(first_half, pltpu.roll(x, w - HEAD_DIM // 4, 1), pltpu.roll(x, HEAD_DIM // 4, 1))
    return x * c + partner * s


def rope_lane_tables(L):
    rows = L // GRID_W
    row = jnp.repeat(jnp.arange(rows, dtype=F32), GRID_W)
    col = jnp.tile(jnp.arange(GRID_W, dtype=F32), rows)
    n_freq = HEAD_DIM // 4
    inv = ROPE_BASE ** (-jnp.arange(n_freq, dtype=F32) / n_freq)
    ang = jnp.stack([row, col], 0)[:, :, None] * inv
    cos, sin = jnp.cos(ang), jnp.sin(ang)
    c = jnp.concatenate([cos[0], cos[0], cos[1], cos[1]], axis=-1)
    s = jnp.concatenate([-sin[0], sin[0], -sin[1], sin[1]], axis=-1)
    return jnp.tile(c, (1, 4)), jnp.tile(s, (1, 4))


def _attend_heads(jobs):
    for job in jobs:
        q = job['q'].astype(BF16)
        job['s'] = lax.dot_general(q, job['k'], _NT, preferred_element_type=F32) * ATT_SCALE
        if job.get('extra') is not None:
            job['s2'] = lax.dot_general(q, job['extra'][0], _NT, preferred_element_type=F32) * ATT_SCALE
    outs = []
    for job in jobs:
        s, sink = job['s'], job.get('sink')
        if job.get('mask') is not None:
            s = jnp.where(job['mask'], s, NEG_INF)
        m = jnp.max(s, axis=-1, keepdims=True)
        if 's2' in job:
            m = jnp.maximum(m, jnp.max(job['s2'], axis=-1, keepdims=True))
        if sink is not None:
            m = jnp.maximum(m, sink)
        p = jnp.exp(s - m)
        den = jnp.sum(p, axis=-1, keepdims=True)
        o = jnp.dot(p.astype(BF16), job['v'], preferred_element_type=F32)
        if 's2' in job:
            p2 = jnp.exp(job['s2'] - m)
            den = den + jnp.sum(p2, axis=-1, keepdims=True)
            o = o + jnp.dot(p2.astype(BF16), job['extra'][1], preferred_element_type=F32)
        if sink is not None:
            den = den + jnp.exp(sink - m)
        outs.append(o / den)
    return outs


def _ctx_attn_kernel(sink_ref, z_ref, gq_ref, gk_ref, o_ref, bk_ref):
    z = z_ref[...]
    bq = _head_rms(z[:, 512:768], gq_ref[...])
    bk = _head_rms(z[:, 768:896], gk_ref[...])
    bk_ref[...] = bk
    groups = ((z[:, 0:256], z[:, 256:384], z[:, 384:512], True),
              (bq, bk, z[:, 896:1024], False))
    outs = []
    for q_all, k_all, v_all, use_sink in groups:
        k_all = k_all.astype(BF16)
        v_all = v_all.astype(BF16)
        for hq in range(A_HEADS):
            kv = hq // (A_HEADS // A_KV_HEADS)
            sl = slice(kv * HEAD_DIM, (kv + 1) * HEAD_DIM)
            outs.append(dict(q=q_all[:, hq * HEAD_DIM:(hq + 1) * HEAD_DIM], k=k_all[:, sl], v=v_all[:, sl],
                             sink=sink_ref[hq] if use_sink else None))
    o_ref[...] = jnp.concatenate(_attend_heads(outs), axis=-1).astype(o_ref.dtype)


def ctx_attention(z, sink, gq, gk):
    return pl.pallas_call(
        _ctx_attn_kernel,
        grid=(BATCH,),
        in_specs=[pl.BlockSpec(memory_space=pltpu.SMEM),
                  pl.BlockSpec((SEQ, 1024), lambda b: (b, 0)),
                  pl.BlockSpec((1, HEAD_DIM), lambda b: (0, 0)),
                  pl.BlockSpec((1, HEAD_DIM), lambda b: (0, 0))],
        out_specs=[pl.BlockSpec((SEQ, 512), lambda b: (b, 0)),
                   pl.BlockSpec((SEQ, 128), lambda b: (b, 0))],
        out_shape=[jax.ShapeDtypeStruct((CTX_TOK, 512), BF16),
                   jax.ShapeDtypeStruct((CTX_TOK, 128), F32)],
        compiler_params=_params(("arbitrary",)),
        name="ctx_attention",
    )(sink, z, gq.reshape(1, HEAD_DIM), gk.reshape(1, HEAD_DIM))


LB_TQ = 256


def _lat_b_kernel(zq_ref, zkv_ref, ck_ref, cv_ref, cq_ref, sq_ref, ckk_ref, skk_ref, gq_ref, gk_ref,
                  o_ref, k_scr, v_scr):
    L = zkv_ref.shape[0]

    @pl.when(pl.program_id(1) == 0)
    def _():
        kv = zkv_ref[...]
        bk = _rope_apply(_head_rms(kv[:, :128], gk_ref[...]), ckk_ref[...], skk_ref[...])
        k_scr[0:L, :] = bk.astype(BF16)
        k_scr[L:L + PAST_LEN, :] = ck_ref[...].astype(BF16)
        v_scr[0:L, :] = kv[:, 128:].astype(BF16)
        v_scr[L:L + PAST_LEN, :] = cv_ref[...].astype(BF16)

    q = _rope_apply(_head_rms(zq_ref[...], gq_ref[...]), cq_ref[...], sq_ref[...])
    outs = []
    for hq in range(B_HEADS):
        kv = hq // (B_HEADS // B_KV_HEADS)
        sl = slice(kv * HEAD_DIM, (kv + 1) * HEAD_DIM)
        outs.append(dict(q=q[:, hq * HEAD_DIM:(hq + 1) * HEAD_DIM], k=k_scr[:, sl], v=v_scr[:, sl]))
    o_ref[...] = jnp.concatenate(_attend_heads(outs), axis=-1).astype(o_ref.dtype)


def latent_attention_b(z, row0, nb, L, cache_k, cache_v, rope_c, rope_s, gq, gk):
    nq = L // LB_TQ
    return pl.pallas_call(
        _lat_b_kernel,
        grid=(nb, nq),
        in_specs=[pl.BlockSpec((LB_TQ, 256), lambda b, i: (row0 // LB_TQ + b * nq + i, 2)),
                  pl.BlockSpec((L, 256), lambda b, i: (row0 // L + b, 3)),
                  pl.BlockSpec((None, PAST_LEN, 128), lambda b, i: (b, 0, 0)),
                  pl.BlockSpec((None, PAST_LEN, 128), lambda b, i: (b, 0, 0)),
                  pl.BlockSpec((LB_TQ, 256), lambda b, i: (i, 0)),
                  pl.BlockSpec((LB_TQ, 256), lambda b, i: (i, 0)),
                  pl.BlockSpec((L, 128), lambda b, i: (0, 0)),
                  pl.BlockSpec((L, 128), lambda b, i: (0, 0)),
                  pl.BlockSpec((1, HEAD_DIM), lambda b, i: (0, 0)),
                  pl.BlockSpec((1, HEAD_DIM), lambda b, i: (0, 0))],
        out_specs=pl.BlockSpec((LB_TQ, 256), lambda b, i: (b * nq + i, 0)),
        out_shape=jax.ShapeDtypeStruct((nb * L, 256), BF16),
        scratch_shapes=[pltpu.VMEM((L + PAST_LEN, 128), BF16),
                        pltpu.VMEM((L + PAST_LEN, 128), BF16)],
        compiler_params=_params(("arbitrary", "arbitrary")),
        name="latent_attention_b",
    )(z, z, cache_k, cache_v, rope_c, rope_s, rope_c, rope_s, gq.reshape(1, HEAD_DIM), gk.reshape(1, HEAD_DIM))


def _lat_a_kernel(sink_ref, zq_ref, zkv_ref, ck_ref, cv_ref, cq_ref, sq_ref, ckk_ref, skk_ref,
                  o_ref, k_scr, v_scr, ck_scr, cv_scr):
    L = zkv_ref.shape[0]
    i = pl.program_id(1)

    @pl.when(i == 0)
    def _():
        kv = zkv_ref[...]
        zeros = jnp.zeros((BLOCK, 128), BF16)
        k_scr[0:BLOCK, :] = zeros
        v_scr[0:BLOCK, :] = zeros
        k_scr[BLOCK:BLOCK + L, :] = _rope_apply(kv[:, :128], ckk_ref[...], skk_ref[...]).astype(BF16)
        v_scr[BLOCK:BLOCK + L, :] = kv[:, 128:].astype(BF16)
        k_scr[BLOCK + L:2 * BLOCK + L, :] = zeros
        v_scr[BLOCK + L:2 * BLOCK + L, :] = zeros
        ck_scr[...] = ck_ref[...].astype(BF16)
        cv_scr[...] = cv_ref[...].astype(BF16)

    q = _rope_apply(zq_ref[...], cq_ref[...], sq_ref[...])
    start = pl.multiple_of(i * BLOCK, BLOCK)
    kband = k_scr[pl.ds(start, 3 * BLOCK), :]
    vband = v_scr[pl.ds(start, 3 * BLOCK), :]
    r = lax.broadcasted_iota(jnp.int32, (BLOCK, 3 * BLOCK), 0)
    cidx = lax.broadcasted_iota(jnp.int32, (BLOCK, 3 * BLOCK), 1)
    kpos = i * BLOCK - BLOCK + cidx
    mask = (jnp.abs(cidx - BLOCK - r) <= WINDOW) & (kpos >= 0) & (kpos < L)
    outs = []
    for hq in range(A_HEADS):
        kv = hq // (A_HEADS // A_KV_HEADS)
        sl = slice(kv * HEAD_DIM, (kv + 1) * HEAD_DIM)
        outs.append(dict(q=q[:, hq * HEAD_DIM:(hq + 1) * HEAD_DIM], k=kband[:, sl], v=vband[:, sl],
                         extra=(ck_scr[:, sl], cv_scr[:, sl]), sink=sink_ref[hq], mask=mask))
    o_ref[...] = jnp.concatenate(_attend_heads(outs), axis=-1).astype(o_ref.dtype)


def latent_attention_a(z, row0, nb, L, sink, cache_k, cache_v, rope_c, rope_s):
    nq = L // BLOCK
    return pl.pallas_call(
        _lat_a_kernel,
        grid=(nb, nq),
        in_specs=[pl.BlockSpec(memory_space=pltpu.SMEM),
                  pl.BlockSpec((BLOCK, 256), lambda b, i: (row0 // BLOCK + b * nq + i, 0)),
                  pl.BlockSpec((L, 256), lambda b, i: (row0 // L + b, 1)),
                  pl.BlockSpec((None, PAST_LEN, 128), lambda b, i: (b, 0, 0)),
                  pl.BlockSpec((None, PAST_LEN, 128), lambda b, i: (b, 0, 0)),
                  pl.BlockSpec((BLOCK, 256), lambda b, i: (i, 0)),
                  pl.BlockSpec((BLOCK, 256), lambda b, i: (i, 0)),
                  pl.BlockSpec((L, 128), lambda b, i: (0, 0)),
                  pl.BlockSpec((L, 128), lambda b, i: (0, 0))],
        out_specs=pl.BlockSpec((BLOCK, 256), lambda b, i: (b * nq + i, 0)),
        out_shape=jax.ShapeDtypeStruct((nb * L, 256), BF16),
        scratch_shapes=[pltpu.VMEM((L + 2 * BLOCK, 128), BF16),
                        pltpu.VMEM((L + 2 * BLOCK, 128), BF16),
                        pltpu.VMEM((PAST_LEN, 128), BF16),
                        pltpu.VMEM((PAST_LEN, 128), BF16)],
        compiler_params=_params(("arbitrary", "arbitrary")),
        name="latent_attention_a",
    )(sink, z, z, cache_k, cache_v, rope_c, rope_s, rope_c, rope_s)


_TN = (((0,), (0,)), ((), ()))
HG_GROUP = 4


def _hgrn_kernel(zq_ref, zf_ref, zi_ref, zg_ref, lb_ref, gn_ref, s0_ref, o_ref, sT_ref,
                 of_scr, ob_scr, logf_scr, k_scr, S_scr, *, tt):
    d = pl.program_id(1)
    j = pl.program_id(2)
    n_t = pl.num_programs(2)
    C = HGRN_CHUNK
    n_c = tt // C

    @pl.when(j == 0)
    def _():
        S_scr[...] = s0_ref[...]

    lb = lb_ref[...]
    zf = zf_ref[...]
    logf_scr[...] = jnp.log(lb + (1.0 - lb) * jax.nn.sigmoid(zf))
    k_scr[...] = (1.0 - lb) * jax.nn.sigmoid(-zf)

    row = lax.broadcasted_iota(jnp.int32, (C, C), 0)
    col = lax.broadcasted_iota(jnp.int32, (C, C), 1)

    def run(reverse, tile):
        tri = (row <= col) if reverse else (row >= col)
        cum = tri.astype(F32)

        def body(gi, carry):
            heads = [slice(h * HEAD_DIM, (h + 1) * HEAD_DIM) for h in range(C_HEADS)]
            chunks = []
            for g in range(HG_GROUP):
                ci = gi * HG_GROUP + g
                c = (n_c - 1 - ci) if reverse else ci
                r0 = pl.multiple_of(c * C, C)
                G = jnp.dot(cum, logf_scr[pl.ds(r0, C), :], precision=lax.Precision.HIGHEST,
                            preferred_element_type=F32)
                G_end = G[0:1, :] if reverse else G[C - 1:C, :]
                kc = k_scr[pl.ds(r0, C), :]
                vc = zi_ref[pl.ds(r0, C), :].astype(BF16)
                q_in = (zq_ref[pl.ds(r0, C), :] * jnp.exp(G)).astype(BF16)
                k_in = (kc * jnp.exp(-G)).astype(BF16)
                k_out = (kc * jnp.exp(G_end - G)).astype(BF16)
                chunks.append(dict(
                    r0=r0, vc=vc, q_in=q_in, decay=jnp.exp(G_end),
                    a=[lax.dot_general(q_in[:, sl], k_in[:, sl], _NT, preferred_element_type=F32) for sl in heads],
                    kv=[lax.dot_general(vc[:, sl], k_out[:, sl], _TN, preferred_element_type=F32) for sl in heads]))
            s_cur = [S_scr[h] for h in range(C_HEADS)]
            for ch in chunks:
                ch['qs'] = [lax.dot_general(ch['q_in'][:, sl], s_cur[h].astype(BF16), _NT,
                                            preferred_element_type=F32) for h, sl in enumerate(heads)]
                s_cur = [s_cur[h] * ch['decay'][:, sl] + ch['kv'][h] for h, sl in enumerate(heads)]
            for h in range(C_HEADS):
                S_scr[h] = s_cur[h]
            for ch in chunks:
                o_c = jnp.concatenate(
                    [jnp.dot(jnp.where(tri, ch['a'][h], 0.0).astype(BF16), ch['vc'][:, sl],
                             preferred_element_type=F32) + ch['qs'][h] for h, sl in enumerate(heads)], axis=-1)
                if reverse:
                    ob_scr[pl.ds(ch['r0'], C), :] = o_c
                else:
                    of_scr[pl.ds(pl.multiple_of(tile * tt, tt) + ch['r0'], C), :] = o_c
            return carry

        lax.fori_loop(0, n_c // HG_GROUP, body, 0)

    @pl.when(d == 0)
    def _():
        run(False, j)

    @pl.when(d == 1)
    def _():
        tile = n_t - 1 - j
        run(True, tile)
        o = of_scr[pl.ds(pl.multiple_of(tile * tt, tt), tt), :] + ob_scr[...]
        g = zg_ref[...]
        o_ref[...] = (_head_rms(o, gn_ref[...]) * (g * jax.nn.sigmoid(g))).astype(o_ref.dtype)

    @pl.when(j == n_t - 1)
    def _():
        sT_ref[...] = S_scr[...]


def hgrn_mixer(z, row0, nb, L, lb, gn, s0_t):
    tt = min(L, 512)
    n_t = L // tt
    rb = row0 // tt

    def tile(d, j):
        return jnp.where(d == 0, j, n_t - 1 - j)

    def late(d, j):
        return jnp.where(d == 0, n_t - 1, n_t - 1 - j)

    st_spec = pl.BlockSpec((None, None, C_HEADS, HEAD_DIM, HEAD_DIM), lambda b, d, j: (b, d, 0, 0, 0))
    return pl.pallas_call(
        functools.partial(_hgrn_kernel, tt=tt),
        grid=(nb, 2, n_t),
        in_specs=[pl.BlockSpec((tt, 256), lambda b, d, j: (rb + b * n_t + tile(d, j), 4)),
                  pl.BlockSpec((tt, 256), lambda b, d, j: (rb + b * n_t + tile(d, j), 5 + d)),
                  pl.BlockSpec((tt, 256), lambda b, d, j: (rb + b * n_t + tile(d, j), 7)),
                  pl.BlockSpec((tt, 256), lambda b, d, j: (rb + b * n_t + late(d, j), 8)),
                  pl.BlockSpec((1, 256), lambda b, d, j: (0, 0)),
                  pl.BlockSpec((1, HEAD_DIM), lambda b, d, j: (0, 0)),
                  st_spec],
        out_specs=[pl.BlockSpec((tt, 256), lambda b, d, j: (b * n_t + late(d, j), 0)),
                   st_spec],
        out_shape=[jax.ShapeDtypeStruct((nb * L, 256), BF16),
                   jax.ShapeDtypeStruct((nb, 2, C_HEADS, HEAD_DIM, HEAD_DIM), F32)],
        scratch_shapes=[pltpu.VMEM((L, 256), F32),
                        pltpu.VMEM((tt, 256), F32),
                        pltpu.VMEM((tt, 256), F32),
                        pltpu.VMEM((tt, 256), F32),
                        pltpu.VMEM((C_HEADS, HEAD_DIM, HEAD_DIM), F32)],
        compiler_params=_params(("arbitrary", "arbitrary", "arbitrary")),
        name="hgrn_mixer",
    )(z, z, z, z, lb.reshape(1, 256), gn.reshape(1, HEAD_DIM), s0_t)


DL_C = DELTA_CHUNK
DL_PREP_TT = 256
DL_HALO = 8
DL_CHUNK_TT = 512
N_QKV_HEADS = 3 * D_HEADS


def _delta_prep_kernel(x_ref, xp_ref, xn_ref, zab_ref, cw_ref, na_ref, dtb_ref, qkv_ref, gate_ref, xs_scr):
    tt = x_ref.shape[0]
    row = pl.program_id(0) * tt
    lat = row - CTX_TOK
    first = jnp.where(row < CTX_TOK, True, lat % DEC_SEQ == 0)
    last = jnp.where(row < CTX_TOK, True, (lat + tt) % DEC_SEQ == 0)
    xs_scr[DL_HALO:DL_HALO + tt, :] = x_ref[...]
    xs_scr[0:DL_HALO, :] = jnp.where(first, 0.0, xp_ref[...])
    xs_scr[DL_HALO + tt:2 * DL_HALO + tt, :] = jnp.where(last, 0.0, xn_ref[...])
    pad = (CONV_K - 1) // 2
    y = None
    for t in range(CONV_K):
        term = xs_scr[pl.ds(DL_HALO - pad + t, tt), :] * cw_ref[t:t + 1, :]
        y = term if y is None else y + term
    y = y * jax.nn.sigmoid(y)
    for idx in range(N_QKV_HEADS):
        xh = y[:, idx * HEAD_DIM:(idx + 1) * HEAD_DIM]
        if idx < 2 * D_HEADS:
            xh = xh * lax.rsqrt(jnp.sum(xh * xh, axis=-1, keepdims=True) + EPS)
        if idx < D_HEADS:
            xh = xh * ATT_SCALE
        qkv_ref[idx] = xh
    zab = zab_ref[...]
    lane = lax.broadcasted_iota(jnp.int32, zab.shape, 1)
    t_ = zab + dtb_ref[...]
    softplus = jnp.maximum(t_, 0.0) + jnp.log(1.0 + jnp.exp(-jnp.abs(t_)))
    gate_ref[...] = jnp.where(lane < 2 * D_HEADS, na_ref[...] * softplus, jax.nn.sigmoid(zab))


def delta_prep(z, conv_w, a_log, dt_bias):
    tt = DL_PREP_TT
    hb = tt // DL_HALO
    n_hb = N_TOK // DL_HALO
    pad8 = lambda v: jnp.concatenate([v.reshape(1, 2 * D_HEADS), jnp.zeros((1, 128 - 2 * D_HEADS), F32)], axis=1)
    return pl.pallas_call(
        _delta_prep_kernel,
        grid=(N_TOK // tt,),
        in_specs=[pl.BlockSpec((tt, 768), lambda i: (i, 3)),
                  pl.BlockSpec((DL_HALO, 768), lambda i: (jnp.maximum(i * hb - 1, 0), 3)),
                  pl.BlockSpec((DL_HALO, 768), lambda i: (jnp.minimum((i + 1) * hb, n_hb - 1), 3)),
                  pl.BlockSpec((tt, 128), lambda i: (i, Z_DAB // 128)),
                  pl.BlockSpec((CONV_K, 768), lambda i: (0, 0)),
                  pl.BlockSpec((1, 128), lambda i: (0, 0)),
                  pl.BlockSpec((1, 128), lambda i: (0, 0))],
        out_specs=[pl.BlockSpec((N_QKV_HEADS, tt, HEAD_DIM), lambda i: (0, i, 0)),
                   pl.BlockSpec((tt, 128), lambda i: (i, 0))],
        out_shape=[jax.ShapeDtypeStruct((N_QKV_HEADS, N_TOK, HEAD_DIM), F32),
                   jax.ShapeDtypeStruct((N_TOK, 128), F32)],
        scratch_shapes=[pltpu.VMEM((tt + 2 * DL_HALO, 768), F32)],
        compiler_params=_params(("arbitrary",)),
        name="delta_prep",
    )(z, z, z, z, conv_w, pad8(-jnp.exp(a_log)), pad8(dt_bias))


def _split_bf16(a):
    hi = a.astype(BF16)
    return hi, (a - hi.astype(F32)).astype(BF16)


def _dot_hl(a, b_parts):
    a_hi, a_lo = _split_bf16(a)
    b_hi, b_lo = b_parts
    m = a.shape[0]
    r = jnp.dot(jnp.concatenate([a_hi, a_lo], axis=0), b_hi, preferred_element_type=F32)
    return r[:m] + r[m:] + jnp.dot(a_hi, b_lo, preferred_element_type=F32)


def _delta_chunk_kernel(qkv_ref, gate_ref, u2_ref, wq_ref, ak_ref):
    C = DL_C
    n_c = gate_ref.shape[0] // C
    row = lax.broadcasted_iota(jnp.int32, (C, C), 0)
    col = lax.broadcasted_iota(jnp.int32, (C, C), 1)
    eye = (row == col).astype(F32)

    def body(c, carry):
        r0 = pl.multiple_of(c * C, C)
        ga = gate_ref[pl.ds(r0, C), :]
        chains = []
        for d in range(2):
            incl = (row >= col) if d == 0 else (row <= col)
            strict = (row > col) if d == 0 else (row < col)
            g_all = jnp.dot(incl.astype(F32), ga, precision=lax.Precision.HIGHEST, preferred_element_type=F32)
            g_all_t = g_all.T
            for h in range(D_HEADS):
                ci = d * D_HEADS + h
                q = qkv_ref[h, pl.ds(r0, C), :]
                k = qkv_ref[D_HEADS + h, pl.ds(r0, C), :]
                v = qkv_ref[2 * D_HEADS + h, pl.ds(r0, C), :]
                g_col = g_all[:, ci:ci + 1]
                g_row = g_all_t[ci:ci + 1, :]
                beta = ga[:, 2 * D_HEADS + ci:2 * D_HEADS + ci + 1]
                g_end = g_col[C - 1:C, :] if d == 0 else g_col[0:1, :]
                kb = k * beta
                eg = jnp.exp(g_col)
                decay = jnp.where(incl, jnp.exp(jnp.where(incl, g_col - g_row, 0.0)), 0.0)
                kq = jnp.concatenate([kb, q], axis=0).astype(BF16)
                chains.append(dict(
                    strict=strict, decay=decay, qg=q * eg, g_end=g_end,
                    r=lax.dot_general(kq, k.astype(BF16), _NT, preferred_element_type=F32),
                    rhs=jnp.concatenate([v * beta, kb * eg], axis=1).astype(BF16),
                    ke_t=(k * jnp.exp(g_end - g_col)).T))
        for ch in chains:
            ch['p'] = -jnp.where(ch['strict'], ch['r'][:C] * ch['decay'], 0.0)
            ch['t'] = eye + ch['p']
        for _ in range(5):
            for ch in chains:
                ch['p'] = _dot_hl(ch['p'], _split_bf16(ch['p']))
            for ch in chains:
                ch['t'] = ch['t'] + _dot_hl(ch['t'], _split_bf16(ch['p']))
        for ch in chains:
            ch['uw'] = jnp.dot(ch['t'].astype(BF16), ch['rhs'], preferred_element_type=F32)
        u2 = [jnp.concatenate([ch['uw'][:, :C], jnp.broadcast_to(jnp.exp(ch['g_end']), (C, C))], axis=1)
              for ch in chains]
        wq = [jnp.concatenate([ch['uw'][:, C:], ch['qg']], axis=0).astype(BF16) for ch in chains]
        ak = [jnp.concatenate([ch['r'][C:] * ch['decay'], ch['ke_t']], axis=0).astype(BF16) for ch in chains]
        pack = lambda xs: jnp.stack(xs).reshape((2, D_HEADS) + xs[0].shape)
        u2_ref[:, :, pl.ds(r0, C), :] = pack(u2)
        wq_ref[:, :, c] = pack(wq)
        ak_ref[:, :, c] = pack(ak)
        return carry

    lax.fori_loop(0, n_c, body, 0)


def delta_chunks(qkv, gates):
    tt = DL_CHUNK_TT
    n_c = tt // DL_C
    return pl.pallas_call(
        _delta_chunk_kernel,
        grid=(N_TOK // tt,),
        in_specs=[pl.BlockSpec((N_QKV_HEADS, tt, HEAD_DIM), lambda i: (0, i, 0)),
                  pl.BlockSpec((tt, 128), lambda i: (i, 0))],
        out_specs=[pl.BlockSpec((2, D_HEADS, tt, 128), lambda i: (0, 0, i, 0)),
                   pl.BlockSpec((2, D_HEADS, n_c, 2 * DL_C, HEAD_DIM), lambda i: (0, 0, i, 0, 0)),
                   pl.BlockSpec((2, D_HEADS, n_c, 2 * DL_C, HEAD_DIM), lambda i: (0, 0, i, 0, 0))],
        out_shape=[jax.ShapeDtypeStruct((2, D_HEADS, N_TOK, 128), F32),
                   jax.ShapeDtypeStruct((2, D_HEADS, N_TOK // DL_C, 2 * DL_C, HEAD_DIM), BF16),
                   jax.ShapeDtypeStruct((2, D_HEADS, N_TOK // DL_C, 2 * DL_C, HEAD_DIM), BF16)],
        compiler_params=_params(("arbitrary",)),
        name="delta_chunks",
    )(qkv, gates)


def _delta_scan_kernel(u2f_ref, wqf_ref, akf_ref, u2b_ref, wqb_ref, akb_ref, s0_ref,
                       of_ref, ob_ref, s_ref, s_scr):
    j = pl.program_id(1)
    C = DL_C
    n_c = wqf_ref.shape[1]

    @pl.when(j == 0)
    def _():
        s_scr[...] = s0_ref[...]

    def body(ci, carry):
        chains = []
        for d, (u2_ref, wq_ref, ak_ref, o_ref) in enumerate(((u2f_ref, wqf_ref, akf_ref, of_ref),
                                                            (u2b_ref, wqb_ref, akb_ref, ob_ref))):
            c = ci if d == 0 else n_c - 1 - ci
            r0 = pl.multiple_of(c * C, C)
            for h in range(D_HEADS):
                s = s_scr[d, h]
                chains.append(dict(d=d, h=h, c=c, r0=r0, s=s, ak_ref=ak_ref, o_ref=o_ref,
                                   u2=u2_ref[h, pl.ds(r0, C), :],
                                   r1=jnp.dot(wq_ref[h, c], s.astype(BF16),
                                              preferred_element_type=F32)))
        for ch in chains:
            v_new = ch['u2'][:, :C] - ch['r1'][:C]
            ch['r2'] = jnp.dot(ch['ak_ref'][ch['h'], ch['c']], v_new.astype(BF16),
                               preferred_element_type=F32)
        for ch in chains:
            ch['o_ref'][ch['h'], pl.ds(ch['r0'], C), :] = ch['r1'][C:] + ch['r2'][:C]
            s_scr[ch['d'], ch['h']] = ch['s'] * ch['u2'][0:1, C:] + ch['r2'][C:]
        return carry

    lax.fori_loop(0, n_c, body, 0)

    @pl.when(j == pl.num_programs(1) - 1)
    def _():
        s_ref[...] = s_scr[...]


def delta_scan(u2, wq, ak, row0, nb, L, s0):
    tt = min(L, 512)
    n_t = L // tt
    n_c = tt // DL_C
    rb = row0 // tt
    fwd = lambda b, j: rb + b * n_t + j
    bwd = lambda b, j: rb + b * n_t + (n_t - 1 - j)
    u_spec = lambda d, f: pl.BlockSpec((None, D_HEADS, tt, 128), lambda b, j: (d, 0, f(b, j), 0))
    c_spec = lambda d, f: pl.BlockSpec((None, D_HEADS, n_c, 2 * DL_C, HEAD_DIM), lambda b, j: (d, 0, f(b, j), 0, 0))
    st_spec = pl.BlockSpec((None, 2, D_HEADS, HEAD_DIM, HEAD_DIM), lambda b, j: (b, 0, 0, 0, 0))
    return pl.pallas_call(
        _delta_scan_kernel,
        grid=(nb, n_t),
        in_specs=[u_spec(0, fwd), c_spec(0, fwd), c_spec(0, fwd),
                  u_spec(1, bwd), c_spec(1, bwd), c_spec(1, bwd), st_spec],
        out_specs=[pl.BlockSpec((D_HEADS, tt, HEAD_DIM), lambda b, j: (0, b * n_t + j, 0)),
                   pl.BlockSpec((D_HEADS, tt, HEAD_DIM), lambda b, j: (0, b * n_t + (n_t - 1 - j), 0)),
                   st_spec],
        out_shape=[jax.ShapeDtypeStruct((D_HEADS, nb * L, HEAD_DIM), F32),
                   jax.ShapeDtypeStruct((D_HEADS, nb * L, HEAD_DIM), F32),
                   jax.ShapeDtypeStruct((nb, 2, D_HEADS, HEAD_DIM, HEAD_DIM), F32)],
        scratch_shapes=[pltpu.VMEM((2, D_HEADS, HEAD_DIM, HEAD_DIM), F32)],
        compiler_params=_params(("arbitrary", "arbitrary")),
        name="delta_scan",
    )(u2, wq, ak, u2, wq, ak, s0)


def _delta_out_kernel(of_ref, ob_ref, zg_ref, gn_ref, o_ref):
    outs = []
    for h in range(D_HEADS):
        o = of_ref[h] + ob_ref[h]
        outs.append(o * lax.rsqrt(jnp.mean(o * o, axis=-1, keepdims=True) + EPS) * gn_ref[...])
    g = zg_ref[...]
    o_ref[...] = (jnp.concatenate(outs, axis=-1) * (g * jax.nn.sigmoid(g))).astype(o_ref.dtype)


def delta_output(o_f, o_b, z, row0, gn):
    n = o_f.shape[1]
    tt = 256
    return pl.pallas_call(
        _delta_out_kernel,
        grid=(n // tt,),
        in_specs=[pl.BlockSpec((D_HEADS, tt, HEAD_DIM), lambda i: (0, i, 0)),
                  pl.BlockSpec((D_HEADS, tt, HEAD_DIM), lambda i: (0, i, 0)),
                  pl.BlockSpec((tt, 256), lambda i: (row0 // tt + i, Z_DG // 256)),
                  pl.BlockSpec((1, HEAD_DIM), lambda i: (0, 0))],
        out_specs=pl.BlockSpec((tt, 256), lambda i: (i, 0)),
        out_shape=jax.ShapeDtypeStruct((n, 256), BF16),
        compiler_params=_params(("arbitrary",)),
        name="delta_output",
    )(o_f, o_b, z, gn.reshape(1, HEAD_DIM))


def _rms_heads(x, g):
    return x * lax.rsqrt(jnp.mean(x * x, axis=-1, keepdims=True) + EPS) * g


def _l2n(x):
    return x * lax.rsqrt(jnp.sum(x * x, axis=-1, keepdims=True) + EPS)


def _flip(t):
    return jnp.flip(t, axis=1)


def _rope_tables(L):
    rows = L // GRID_W
    row = jnp.repeat(jnp.arange(rows, dtype=F32), GRID_W)
    col = jnp.tile(jnp.arange(GRID_W, dtype=F32), rows)
    n_freq = HEAD_DIM // 4
    inv = ROPE_BASE ** (-jnp.arange(n_freq, dtype=F32) / n_freq)
    ang = jnp.stack([row, col], 0)[:, :, None] * inv
    return jnp.cos(ang), jnp.sin(ang)


def _rope(x, tables):
    cos, sin = tables
    half = HEAD_DIM // 2
    nf = HEAD_DIM // 4
    outs = []
    for a in range(2):
        xa = x[..., a * half:(a + 1) * half]
        x1, x2 = xa[..., :nf], xa[..., nf:]
        ca = cos[a][None, :, None, :]
        sa = sin[a][None, :, None, :]
        outs += [x1 * ca - x2 * sa, x2 * ca + x1 * sa]
    return jnp.concatenate(outs, axis=-1)


def _dense_gqa(q, k, v, sink):
    Bn, Lq, Hq, d = q.shape
    Hkv = k.shape[2]
    G = Hq // Hkv
    nb = Lq // BLOCK
    scale = d ** -0.5
    qb = q.reshape(Bn, nb, BLOCK, Hkv, G, d).transpose(1, 0, 2, 3, 4, 5)

    def one(qblk):
        s = jnp.einsum('bqhgd,bkhd->bhgqk', qblk, k) * scale
        if sink is not None:
            sl = jnp.broadcast_to(sink.reshape(Hkv, G)[None, :, :, None, None], s.shape[:-1] + (1,))
            p = jax.nn.softmax(jnp.concatenate([s, sl], axis=-1), axis=-1)[..., :-1]
        else:
            p = jax.nn.softmax(s, axis=-1)
        return jnp.einsum('bhgqk,bkhd->bqhgd', p, v)

    o = lax.map(one, qb)
    return o.transpose(1, 0, 2, 3, 4, 5).reshape(Bn, Lq, Hq * d)


def _windowed_gqa(q, k, v, kc, vc, sink):
    Bn, L, Hq, d = q.shape
    Hkv = k.shape[2]
    G = Hq // Hkv
    nb = L // BLOCK
    scale = d ** -0.5
    qb = q.reshape(Bn, nb, BLOCK, Hkv, G, d)

    def band(t):
        tb = t.reshape(Bn, nb, BLOCK, Hkv, d)
        z = jnp.zeros_like(tb[:, :1])
        prev = jnp.concatenate([z, tb[:, :-1]], axis=1)
        nxt = jnp.concatenate([tb[:, 1:], z], axis=1)
        return jnp.concatenate([prev, tb, nxt], axis=2)

    kband, vband = band(k), band(v)
    s_loc = jnp.einsum('bnqhgd,bnkhd->bnhgqk', qb, kband) * scale
    s_ctx = jnp.einsum('bnqhgd,bkhd->bnhgqk', qb, kc) * scale
    qpos = jnp.arange(nb)[:, None, None] * BLOCK + jnp.arange(BLOCK)[None, :, None]
    kpos = jnp.arange(nb)[:, None, None] * BLOCK - BLOCK + jnp.arange(3 * BLOCK)[None, None, :]
    mask = (jnp.abs(kpos - qpos) <= WINDOW) & (kpos >= 0) & (kpos < L)
    s_loc = jnp.where(mask[None, :, None, None], s_loc, NEG_INF)
    sl = jnp.broadcast_to(sink.reshape(Hkv, G)[None, None, :, :, None, None], s_loc.shape[:-1] + (1,))
    p = jax.nn.softmax(jnp.concatenate([s_loc, s_ctx, sl], axis=-1), axis=-1)
    nk = 3 * BLOCK
    lc = kc.shape[1]
    o = (jnp.einsum('bnhgqk,bnkhd->bnqhgd', p[..., :nk], vband)
         + jnp.einsum('bnhgqk,bkhd->bnqhgd', p[..., nk:nk + lc], vc))
    return o.reshape(Bn, L, Hq * d)


def _gla_chunked(q, k, v, logf, s0):
    Bn, L, H, _ = q.shape
    C = HGRN_CHUNK
    n = L // C
    ch = lambda t: t.reshape(Bn, n, C, H, t.shape[-1]).transpose(1, 0, 3, 2, 4)
    q, k, v, logf = ch(q), ch(k), ch(v), ch(logf)
    G = jnp.cumsum(logf, axis=3)
    G_last = G[:, :, :, -1:, :]
    q_in = q * jnp.exp(G)
    k_in = k * jnp.exp(-G)
    k_out = k * jnp.exp(G_last - G)
    causal = jnp.tril(jnp.ones((C, C), bool))
    A = jnp.where(causal, jnp.einsum('nbhtd,nbhsd->nbhts', q_in, k_in), 0.0)
    o_intra = jnp.einsum('nbhts,nbhsv->nbhtv', A, v)

    def step(S, xs):
        qi, ko, vi, oi, gl = xs
        o = oi + jnp.einsum('bhtd,bhdv->bhtv', qi, S)
        S = jnp.exp(gl)[:, :, 0, :, None] * S + jnp.einsum('bhsd,bhsv->bhdv', ko, vi)
        return S, o

    S, o = lax.scan(step, s0, (q_in, k_out, v, o_intra, G_last))
    return o.transpose(1, 0, 3, 2, 4).reshape(Bn, L, H, -1), S


def _delta_chunked(q, k, v, log_a, beta, s0):
    Bn, L, H, _ = q.shape
    C = DELTA_CHUNK
    n = L // C
    ch = lambda t: t.reshape(Bn, n, C, H, t.shape[-1]).transpose(1, 0, 3, 2, 4)
    chs = lambda t: t.reshape(Bn, n, C, H).transpose(1, 0, 3, 2)
    q, k, v = ch(q), ch(k), ch(v)
    log_a, beta = chs(log_a), chs(beta)
    g = jnp.cumsum(log_a, axis=-1)
    kb = k * beta[..., None]
    vb = v * beta[..., None]
    incl = jnp.tril(jnp.ones((C, C), bool))
    strict = jnp.tril(jnp.ones((C, C), bool), -1)
    decay = jnp.exp(jnp.where(incl, g[..., :, None] - g[..., None, :], -jnp.inf))
    Lm = jnp.where(strict, jnp.einsum('nbhid,nbhjd->nbhij', kb, k) * decay, 0.0)
    eye = jnp.eye(C, dtype=F32)
    T = lax.linalg.triangular_solve(eye + Lm, jnp.broadcast_to(eye, Lm.shape),
                                    left_side=True, lower=True, unit_diagonal=True)
    u = T @ vb
    w = T @ (kb * jnp.exp(g)[..., None])
    attn = jnp.einsum('nbhid,nbhjd->nbhij', q, k) * decay
    q_g = q * jnp.exp(g)[..., None]
    g_last = g[..., -1]
    k_end = k * jnp.exp(g_last[..., None] - g)[..., None]

    def step(S, xs):
        u_c, w_c, qg_c, at_c, ke_c, gl_c = xs
        v_new = u_c - w_c @ S
        o = qg_c @ S + at_c @ v_new
        S = jnp.exp(gl_c)[..., None, None] * S + jnp.einsum('bhcd,bhcv->bhdv', ke_c, v_new)
        return S, o

    S, o = lax.scan(step, s0, (u, w, q_g, attn, k_end, g_last))
    return o.transpose(1, 0, 3, 2, 4).reshape(Bn, L, H, -1), S


def _hgrn2(q, zf_f, zf_b, i, g, lb, onorm_g, s0):
    Bn, L, _ = q.shape
    hd = lambda t: t.reshape(Bn, L, C_HEADS, HEAD_DIM)
    lb = lb.reshape(C_HEADS, HEAD_DIM)

    def gates(zf):
        zf = hd(zf)
        f = lb + (1.0 - lb) * jax.nn.sigmoid(zf)
        return jnp.log(f), (1.0 - lb) * jax.nn.sigmoid(-zf)

    q, i = hd(q), hd(i)
    logf_f, k_f = gates(zf_f)
    logf_b, k_b = gates(zf_b)
    o_f, s_f = _gla_chunked(q, k_f, i, logf_f, s0[:, 0])
    o_b, s_b = _gla_chunked(_flip(q), _flip(k_b), _flip(i), _flip(logf_b), s0[:, 1])
    o = _rms_heads(o_f + _flip(o_b), onorm_g).reshape(Bn, L, BRANCH_W)
    return o * jax.nn.silu(g), jnp.stack([s_f, s_b], axis=1)


def _conv(x, w):
    pad = (CONV_K - 1) // 2
    return lax.conv_general_dilated(x, w[:, None, :], window_strides=(1,),
                                    padding=[(pad, pad)], dimension_numbers=('NWC', 'WIO', 'NWC'),
                                    feature_group_count=x.shape[-1])


def _gated_delta(qkv, za, zb, g, conv_w, a_log, dt_bias, onorm_g, s0):
    Bn, L, _ = qkv.shape
    qkv = jax.nn.silu(_conv(qkv, conv_w))
    q, k, v = [t.reshape(Bn, L, D_HEADS, HEAD_DIM) for t in jnp.split(qkv, 3, axis=-1)]
    q = _l2n(q) * HEAD_DIM ** -0.5
    k = _l2n(k)
    za = za.reshape(Bn, L, 2, D_HEADS)
    zb = zb.reshape(Bn, L, 2, D_HEADS)
    log_alpha = -jnp.exp(a_log) * jax.nn.softplus(za + dt_bias)
    beta = jax.nn.sigmoid(zb)
    o_f, s_f = _delta_chunked(q, k, v, log_alpha[:, :, 0], beta[:, :, 0], s0[:, 0])
    o_b, s_b = _delta_chunked(_flip(q), _flip(k), _flip(v), _flip(log_alpha[:, :, 1]),
                              _flip(beta[:, :, 1]), s0[:, 1])
    o = _rms_heads(o_f + _flip(o_b), onorm_g).reshape(Bn, L, BRANCH_W)
    return o * jax.nn.silu(g), jnp.stack([s_f, s_b], axis=1)


def _mixers(z, lw, ctx):
    Bn, L, _ = z.shape
    aq, ak, av = z[..., 0:256], z[..., 256:384], z[..., 384:512]
    bq, bk, bv = z[..., 512:768], z[..., 768:896], z[..., 896:1024]
    cq, cf_f, cf_b = z[..., 1024:1280], z[..., 1280:1536], z[..., 1536:1792]
    ci, cg = z[..., 1792:2048], z[..., 2048:2304]
    dqkv = z[..., 2304:3072]
    da, db = z[..., Z_DAB:Z_DAB + 8], z[..., Z_DAB + 8:Z_DAB + 16]
    dg = z[..., Z_DG:Z_DG + 256]
    heads = lambda t: t.reshape(Bn, L, -1, HEAD_DIM)
    aq, ak, av, bv = heads(aq), heads(ak), heads(av), heads(bv)
    bq = _rms_heads(heads(bq), lw['b_qnorm_g'])
    bk = _rms_heads(heads(bk), lw['b_knorm_g'])
    if ctx is None:
        s_c0 = jnp.zeros((Bn, 2, C_HEADS, HEAD_DIM, HEAD_DIM), F32)
        s_d0 = jnp.zeros((Bn, 2, D_HEADS, HEAD_DIM, HEAD_DIM), F32)
        o_a = _dense_gqa(aq, ak, av, lw['a_sink'])
        o_b = _dense_gqa(bq, bk, bv, None)
    else:
        ka, va, kbc, vbc, s_c0, s_d0 = ctx
        tabs = _rope_tables(L)
        o_a = _windowed_gqa(_rope(aq, tabs), _rope(ak, tabs), av, ka, va, lw['a_sink'])
        o_b = _dense_gqa(_rope(bq, tabs), jnp.concatenate([_rope(bk, tabs), kbc], axis=1),
                         jnp.concatenate([bv, vbc], axis=1), None)
    o_c, s_c = _hgrn2(cq, cf_f, cf_b, ci, cg, lw['lb'], lw['c_onorm_g'], s_c0)
    o_d, s_d = _gated_delta(dqkv, da, db, dg, lw['d_conv'], lw['d_a_log'], lw['d_dt_bias'],
                            lw['d_onorm_g'], s_d0)
    o = jnp.concatenate([o_a, o_b, o_c, o_d], axis=-1)
    return o, (ak, av, bk, bv, s_c, s_d)


def kernel(x_prompt, x_sample, cache_attn_a_k, cache_attn_a_v, cache_attn_b_k, cache_attn_b_v,
           state_hgrn, state_delta, c, c_ctx, norm1_g, norm2_g, w_ada, b_ada, w_in, a_sink,
           b_qnorm_g, b_knorm_g, c_lb, c_onorm_g, d_conv, d_a_log, d_dt_bias, d_onorm_g,
           w_branch, w_out, ffn_w1, ffn_w3, ffn_w2, router_w, router_b, moe_w1, moe_w3, moe_w2,
           final_norm_g):
    cum = jnp.cumsum(jax.nn.softmax(c_lb, axis=0), axis=0)
    lower_bounds = cum - cum[:1]

    x = jnp.concatenate([x_prompt.reshape(CTX_TOK, D_MODEL), x_sample.reshape(LAT_TOK, D_MODEL)], axis=0)
    cond = jnp.concatenate([c_ctx[None, :], c, jnp.zeros((16 - N_COND, D_MODEL), F32)], axis=0)

    rope_c, rope_s = rope_lane_tables(DEC_SEQ)
    caches = []
    for l in range(DEPTH):
        lw = {'a_sink': a_sink[l], 'b_qnorm_g': b_qnorm_g[l], 'b_knorm_g': b_knorm_g[l],
              'lb': lower_bounds[l], 'c_onorm_g': c_onorm_g[l], 'd_conv': d_conv[l], 'd_a_log': d_a_log[l],
              'd_dt_bias': d_dt_bias[l], 'd_onorm_g': d_onorm_g[l]}
        mod = ada_modulation(cond, w_ada[l], b_ada[l])[:N_COND].reshape(N_COND, 6, D_MODEL)
        w_mix = jnp.concatenate([w_in[l][:, :Z_MAIN], w_in[l][:, Z_MAIN + 16:W_IN_MIX],
                                 w_in[l][:, Z_MAIN:Z_MAIN + 16], jnp.zeros((D_MODEL, 128 - 16), F32)],
                                axis=1).astype(BF16)
        w_gl = w_in[l][:, W_IN_MIX:].reshape(D_MODEL, N_BRANCH, D_MODEL).transpose(1, 0, 2).astype(BF16)
        z = input_projection(x, mod, norm1_g[l], w_mix)
        kv2 = lambda t: t.reshape(DEC_BATCH, PAST_LEN, 128)
        o_ab_ctx, bk_ctx = ctx_attention(z, a_sink[l], b_qnorm_g[l], b_knorm_g[l])
        o_a_lat = latent_attention_a(z, CTX_TOK, DEC_BATCH, DEC_SEQ, a_sink[l], kv2(cache_attn_a_k[:, l]),
                                     kv2(cache_attn_a_v[:, l]), rope_c, rope_s)
        o_b_lat = latent_attention_b(z, CTX_TOK, DEC_BATCH, DEC_SEQ, kv2(cache_attn_b_k[:, l]),
                                     kv2(cache_attn_b_v[:, l]), rope_c, rope_s, b_qnorm_g[l], b_knorm_g[l])
        o_c_ctx, sc_t = hgrn_mixer(z, 0, BATCH, SEQ, lower_bounds[l], c_onorm_g[l],
                                   jnp.zeros((BATCH, 2, C_HEADS, HEAD_DIM, HEAD_DIM), F32))
        o_c_lat, _ = hgrn_mixer(z, CTX_TOK, DEC_BATCH, DEC_SEQ, lower_bounds[l], c_onorm_g[l],
                                jnp.swapaxes(state_hgrn[:, l], -1, -2))

        qkv, gates = delta_prep(z, d_conv[l], d_a_log[l], d_dt_bias[l])
        u2, wq, ak = delta_chunks(qkv, gates)
        of_ctx, ob_ctx, sd = delta_scan(u2, wq, ak, 0, BATCH, SEQ,
                                        jnp.zeros((BATCH, 2, D_HEADS, HEAD_DIM, HEAD_DIM), F32))
        of_lat, ob_lat, _ = delta_scan(u2, wq, ak, CTX_TOK, DEC_BATCH, DEC_SEQ, state_delta[:, l])
        o_d_ctx = delta_output(of_ctx, ob_ctx, z, 0, d_onorm_g[l])
        o_d_lat = delta_output(of_lat, ob_lat, z, CTX_TOK, d_onorm_g[l])
        kvh = lambda t: t.reshape(BATCH, SEQ, 2, HEAD_DIM)
        caches.append((kvh(z[:CTX_TOK, 256:384]), kvh(z[:CTX_TOK, 384:512]), kvh(bk_ctx), kvh(z[:CTX_TOK, 896:1024]),
                       jnp.swapaxes(sc_t, -1, -2), sd))
        o = jnp.concatenate([jnp.concatenate([o_ab_ctx, o_c_ctx, o_d_ctx], axis=1),
                             jnp.concatenate([o_a_lat, o_b_lat, o_c_lat, o_d_lat], axis=1)], axis=0)
        x = merge_projection(x, mod, norm1_g[l], o, w_gl, w_branch[l].astype(BF16), w_out[l].astype(BF16))
        j = l // 2
        if l % 2 == 0:
            x = dense_ffn(x, mod, norm2_g[l], ffn_w1[j].astype(BF16), ffn_w3[j].astype(BF16),
                          ffn_w2[j].astype(BF16))
        else:
            rw = jnp.concatenate([router_w[j], jnp.zeros((D_MODEL, ROUTER_LANES - N_EXPERTS), F32)], axis=1)
            rb = jnp.concatenate([router_b[j], jnp.full((ROUTER_LANES - N_EXPERTS,), F32_MIN, F32)])[None, :]
            x = moe_ffn(x, mod, norm2_g[l], rw, rb, moe_w1[j].astype(BF16), moe_w3[j].astype(BF16),
                        moe_w2[j].astype(BF16))

    y = final_norm(x, final_norm_g)
    y_prompt = y[:CTX_TOK].reshape(BATCH, SEQ, D_MODEL)
    y_sample = y[CTX_TOK:].reshape(DEC_BATCH, DEC_SEQ, D_MODEL)
    stack = lambda idx: jnp.stack([caches[l][idx] for l in range(DEPTH)], axis=1)
    return (y_prompt, y_sample, stack(0), stack(1), stack(2), stack(3), stack(4), stack(5))
```

```python
import functools
import math

import jax
import jax.numpy as jnp
import numpy as np
from jax import lax
from jax.experimental import pallas as pl
from jax.experimental.pallas import tpu as pltpu

F32 = jnp.float32
BF16 = jnp.bfloat16

D_MODEL = 1024
BATCH = 32
SEQ = 256
DEPTH = 2
DEC_BATCH = 8
DEC_SEQ = 4096
PAST_LEN = 256
GRID_W = 64
HEAD_DIM = 64
A_HEADS = 4
A_KV_HEADS = 2
B_HEADS = 4
B_KV_HEADS = 2
C_HEADS = 4
D_HEADS = 4
BRANCH_W = 256
N_BRANCH = 4
WINDOW = 128
BLOCK = 128
ROPE_BASE = 10000.0
HGRN_CHUNK = 32
DELTA_CHUNK = 64
CONV_K = 5
D_FF = 2816
N_EXPERTS = 8
TOP_K = 2
D_FF_EXPERT = 3584
EPS = 1e-6
NEG_INF = -1e30
F32_MIN = float(np.finfo(np.float32).min)

CTX_TOK = BATCH * SEQ
LAT_TOK = DEC_BATCH * DEC_SEQ
N_TOK = CTX_TOK + LAT_TOK
N_COND = 1 + DEC_BATCH

Z_MAIN = 3072
Z_DG = Z_MAIN
Z_DAB = Z_DG + BRANCH_W
Z_COLS = Z_DAB + 128
W_IN_MIX = 3344

TM = 512
VMEM_LIMIT = 56 * 1024 * 1024


def _tile_cond(i, tm):
    ctx_tiles = CTX_TOK // tm
    per_b = DEC_SEQ // tm
    return jnp.where(i < ctx_tiles, 0, 1 + (i - ctx_tiles) // per_b)


def _rms(x, g):
    return x * lax.rsqrt(jnp.mean(x * x, axis=-1, keepdims=True) + EPS) * g


def _params(sem):
    return pltpu.CompilerParams(dimension_semantics=sem, vmem_limit_bytes=VMEM_LIMIT)


def _ada_kernel(c_ref, w_ref, b_ref, o_ref):
    c = c_ref[...]
    s = c * jax.nn.sigmoid(c)
    o_ref[...] = jnp.dot(s.astype(BF16), w_ref[...].astype(BF16), preferred_element_type=F32) + b_ref[...]


def ada_modulation(cond_pad, w, b):
    n = 6 * D_MODEL
    tn = 1536
    return pl.pallas_call(
        _ada_kernel,
        grid=(n // tn,),
        in_specs=[pl.BlockSpec((16, D_MODEL), lambda j: (0, 0)),
                  pl.BlockSpec((D_MODEL, tn), lambda j: (0, j)),
                  pl.BlockSpec((1, tn), lambda j: (0, j))],
        out_specs=pl.BlockSpec((16, tn), lambda j: (0, j)),
        out_shape=jax.ShapeDtypeStruct((16, n), F32),
        compiler_params=_params(("arbitrary",)),
        name="ada_modulation",
    )(cond_pad, w, b.reshape(1, n))


def _in_kernel(x_ref, mod_ref, g_ref, w_ref, z_ref):
    h = _rms(x_ref[...], g_ref[...]) * (1.0 + mod_ref[1:2, :]) + mod_ref[0:1, :]
    z_ref[...] = jnp.dot(h.astype(BF16), w_ref[...], preferred_element_type=F32)


def input_projection(x, mod, g, w):
    nt = N_TOK // TM
    return pl.pallas_call(
        _in_kernel,
        grid=(nt,),
        in_specs=[pl.BlockSpec((TM, D_MODEL), lambda i: (i, 0)),
                  pl.BlockSpec((None, 6, D_MODEL), lambda i: (_tile_cond(i, TM), 0, 0)),
                  pl.BlockSpec((1, D_MODEL), lambda i: (0, 0)),
                  pl.BlockSpec((D_MODEL, Z_COLS), lambda i: (0, 0))],
        out_specs=pl.BlockSpec((TM, Z_COLS), lambda i: (i, 0)),
        out_shape=jax.ShapeDtypeStruct((N_TOK, Z_COLS), F32),
        compiler_params=_params(("arbitrary",)),
        name="input_projection",
    )(x, mod, g.reshape(1, D_MODEL), w)


def _merge_kernel(x_ref, mod_ref, g_ref, o_ref, wgl_ref, wbr_ref, wout_ref, xo_ref):
    x = x_ref[...]
    h = (_rms(x, g_ref[...]) * (1.0 + mod_ref[1:2, :]) + mod_ref[0:1, :]).astype(BF16)
    merged = None
    for j in range(N_BRANCH):
        gate = jax.nn.sigmoid(jnp.dot(h, wgl_ref[j], preferred_element_type=F32))
        br = jnp.dot(o_ref[:, j * BRANCH_W:(j + 1) * BRANCH_W].astype(BF16), wbr_ref[j],
                     preferred_element_type=F32)
        merged = gate * br if merged is None else merged + gate * br
    mix = jnp.dot(merged.astype(BF16), wout_ref[...], preferred_element_type=F32)
    xo_ref[...] = x + mod_ref[2:3, :] * mix


def merge_projection(x, mod, g, o, wgl, wbr, wout):
    nt = N_TOK // TM
    return pl.pallas_call(
        _merge_kernel,
        grid=(nt,),
        in_specs=[pl.BlockSpec((TM, D_MODEL), lambda i: (i, 0)),
                  pl.BlockSpec((None, 6, D_MODEL), lambda i: (_tile_cond(i, TM), 0, 0)),
                  pl.BlockSpec((1, D_MODEL), lambda i: (0, 0)),
                  pl.BlockSpec((TM, N_BRANCH * BRANCH_W), lambda i: (i, 0)),
                  pl.BlockSpec((N_BRANCH, D_MODEL, D_MODEL), lambda i: (0, 0, 0)),
                  pl.BlockSpec((N_BRANCH, BRANCH_W, D_MODEL), lambda i: (0, 0, 0)),
                  pl.BlockSpec((D_MODEL, D_MODEL), lambda i: (0, 0))],
        out_specs=pl.BlockSpec((TM, D_MODEL), lambda i: (i, 0)),
        out_shape=jax.ShapeDtypeStruct((N_TOK, D_MODEL), F32),
        compiler_params=_params(("arbitrary",)),
        name="merge_projection",
    )(x, mod, g.reshape(1, D_MODEL), o, wgl, wbr, wout)


def _ffn_kernel(x_ref, mod_ref, g_ref, w1_ref, w3_ref, w2_ref, xo_ref):
    x = x_ref[...]
    h = (_rms(x, g_ref[...]) * (1.0 + mod_ref[4:5, :]) + mod_ref[3:4, :]).astype(BF16)
    a = jnp.dot(h, w1_ref[...], preferred_element_type=F32)
    b = jnp.dot(h, w3_ref[...], preferred_element_type=F32)
    hid = (a * jax.nn.sigmoid(a) * b).astype(BF16)
    f = jnp.dot(hid, w2_ref[...], preferred_element_type=F32)
    xo_ref[...] = x + mod_ref[5:6, :] * f


def dense_ffn(x, mod, g, w1, w3, w2):
    nt = N_TOK // TM
    const = lambda i: (0, 0)
    return pl.pallas_call(
        _ffn_kernel,
        grid=(nt,),
        in_specs=[pl.BlockSpec((TM, D_MODEL), lambda i: (i, 0)),
                  pl.BlockSpec((None, 6, D_MODEL), lambda i: (_tile_cond(i, TM), 0, 0)),
                  pl.BlockSpec((1, D_MODEL), const),
                  pl.BlockSpec((D_MODEL, D_FF), const, pipeline_mode=pl.Buffered(1)),
                  pl.BlockSpec((D_MODEL, D_FF), const, pipeline_mode=pl.Buffered(1)),
                  pl.BlockSpec((D_FF, D_MODEL), const, pipeline_mode=pl.Buffered(1))],
        out_specs=pl.BlockSpec((TM, D_MODEL), lambda i: (i, 0)),
        out_shape=jax.ShapeDtypeStruct((N_TOK, D_MODEL), F32),
        compiler_params=_params(("arbitrary",)),
        name="dense_ffn",
    )(x, mod, g.reshape(1, D_MODEL), w1, w3, w2)


MOE_TM = 1024
MOE_TF = 896
ROUTER_LANES = 128


def _moe_kernel(x_ref, mod_ref, g_ref, rw_ref, rb_ref, w1_ref, w3_ref, w2_ref, xo_ref,
                h_scr, we_scr, acc_scr):
    e = pl.program_id(1)
    f = pl.program_id(2)

    @pl.when((e == 0) & (f == 0))
    def _():
        h = _rms(x_ref[...], g_ref[...]) * (1.0 + mod_ref[4:5, :]) + mod_ref[3:4, :]
        h_scr[...] = h.astype(BF16)
        logits = jnp.dot(h, rw_ref[...], preferred_element_type=F32,
                         precision=lax.Precision.HIGHEST) + rb_ref[...]
        lane = lax.broadcasted_iota(jnp.int32, logits.shape, 1)
        m1 = jnp.max(logits, axis=-1, keepdims=True)
        i1 = jnp.min(jnp.where(logits == m1, lane, ROUTER_LANES), axis=-1, keepdims=True)
        rest = jnp.where(lane == i1, F32_MIN, logits)
        m2 = jnp.max(rest, axis=-1, keepdims=True)
        i2 = jnp.min(jnp.where(rest == m2, lane, ROUTER_LANES), axis=-1, keepdims=True)
        e2 = jnp.exp(m2 - m1)
        p1 = 1.0 / (1.0 + e2)
        p2 = e2 / (1.0 + e2)
        we_scr[...] = jnp.where(lane == i1, p1, 0.0) + jnp.where(lane == i2, p2, 0.0)
        acc_scr[...] = jnp.zeros_like(acc_scr)

    h = h_scr[...]
    a = jnp.dot(h, w1_ref[...], preferred_element_type=F32)
    b = jnp.dot(h, w3_ref[...], preferred_element_type=F32)
    lane = lax.broadcasted_iota(jnp.int32, (MOE_TM, ROUTER_LANES), 1)
    we = jnp.sum(jnp.where(lane == e, we_scr[...], 0.0), axis=-1, keepdims=True)
    hid = (a * jax.nn.sigmoid(a) * b * we).astype(BF16)
    acc_scr[...] += jnp.dot(hid, w2_ref[...], preferred_element_type=F32)

    @pl.when((e == N_EXPERTS - 1) & (f == pl.num_programs(2) - 1))
    def _():
        xo_ref[...] = x_ref[...] + mod_ref[5:6, :] * acc_scr[...]


def moe_ffn(x, mod, g, rw, rb, w1, w3, w2):
    nt = N_TOK // MOE_TM
    nf = D_FF_EXPERT // MOE_TF
    return pl.pallas_call(
        _moe_kernel,
        grid=(nt, N_EXPERTS, nf),
        in_specs=[pl.BlockSpec((MOE_TM, D_MODEL), lambda i, e, f: (i, 0)),
                  pl.BlockSpec((None, 6, D_MODEL), lambda i, e, f: (_tile_cond(i, MOE_TM), 0, 0)),
                  pl.BlockSpec((1, D_MODEL), lambda i, e, f: (0, 0)),
                  pl.BlockSpec((D_MODEL, ROUTER_LANES), lambda i, e, f: (0, 0)),
                  pl.BlockSpec((1, ROUTER_LANES), lambda i, e, f: (0, 0)),
                  pl.BlockSpec((None, D_MODEL, MOE_TF), lambda i, e, f: (e, 0, f)),
                  pl.BlockSpec((None, D_MODEL, MOE_TF), lambda i, e, f: (e, 0, f)),
                  pl.BlockSpec((None, MOE_TF, D_MODEL), lambda i, e, f: (e, f, 0))],
        out_specs=pl.BlockSpec((MOE_TM, D_MODEL), lambda i, e, f: (i, 0)),
        out_shape=jax.ShapeDtypeStruct((N_TOK, D_MODEL), F32),
        scratch_shapes=[pltpu.VMEM((MOE_TM, D_MODEL), BF16),
                        pltpu.VMEM((MOE_TM, ROUTER_LANES), F32),
                        pltpu.VMEM((MOE_TM, D_MODEL), F32)],
        compiler_params=_params(("arbitrary", "arbitrary", "arbitrary")),
        name="moe_ffn",
    )(x, mod, g.reshape(1, D_MODEL), rw, rb, w1, w3, w2)


def _final_kernel(x_ref, g_ref, o_ref):
    o_ref[...] = _rms(x_ref[...], g_ref[...])


def final_norm(x, g):
    tm = 1024
    return pl.pallas_call(
        _final_kernel,
        grid=(N_TOK // tm,),
        in_specs=[pl.BlockSpec((tm, D_MODEL), lambda i: (i, 0)),
                  pl.BlockSpec((1, D_MODEL), lambda i: (0, 0))],
        out_specs=pl.BlockSpec((tm, D_MODEL), lambda i: (i, 0)),
        out_shape=jax.ShapeDtypeStruct((N_TOK, D_MODEL), F32),
        compiler_params=_params(("arbitrary",)),
        name="final_norm",
    )(x, g.reshape(1, D_MODEL))


MOE_T = 1024
MOE_R = 128
MOE_F = 1792


def _router_kernel(x_ref, mod_ref, g_ref, rw_ref, rb_ref, h_ref, pos_ref, wgt_ref):
    t = x_ref.shape[0]
    h = _rms(x_ref[...], g_ref[...]) * (1.0 + mod_ref[4:5, :]) + mod_ref[3:4, :]
    h_ref[...] = h.astype(BF16)
    logits = jnp.dot(h, rw_ref[...], preferred_element_type=F32, precision=lax.Precision.HIGHEST) + rb_ref[...]
    lt = logits.T[:N_EXPERTS, :]
    eidx = lax.broadcasted_iota(jnp.int32, lt.shape, 0)
    m1 = jnp.max(lt, axis=0, keepdims=True)
    i1 = jnp.min(jnp.where(lt == m1, eidx, N_EXPERTS), axis=0, keepdims=True)
    rest = jnp.where(eidx == i1, F32_MIN, lt)
    m2 = jnp.max(rest, axis=0, keepdims=True)
    i2 = jnp.min(jnp.where(rest == m2, eidx, N_EXPERTS), axis=0, keepdims=True)
    e2 = jnp.exp(m2 - m1)
    p1 = 1.0 / (1.0 + e2)
    p2 = e2 / (1.0 + e2)
    wgt_ref[...] = jnp.where(eidx == i1, p1, 0.0) + jnp.where(eidx == i2, p2, 0.0)
    routed = jnp.where(eidx == i1, 1.0, jnp.where(eidx == i2, 1.0, 0.0))
    s_id = lax.broadcasted_iota(jnp.int32, (t, t), 0)
    t_id = lax.broadcasted_iota(jnp.int32, (t, t), 1)
    before = jnp.where(s_id < t_id, 1.0, 0.0).astype(BF16)
    rank = jnp.dot(routed.astype(BF16), before, preferred_element_type=F32)
    pos_ref[...] = jnp.where(routed > 0.0, rank.astype(jnp.int32), -1)


def moe_router(x, mod, g, rw, rb):
    nt = N_TOK // MOE_T
    return pl.pallas_call(
        _router_kernel,
        grid=(nt,),
        in_specs=[pl.BlockSpec((MOE_T, D_MODEL), lambda i: (i, 0)),
                  pl.BlockSpec((None, 6, D_MODEL), lambda i: (_tile_cond(i, MOE_T), 0, 0)),
                  pl.BlockSpec((1, D_MODEL), lambda i: (0, 0)),
                  pl.BlockSpec((D_MODEL, 128), lambda i: (0, 0)),
                  pl.BlockSpec((1, 128), lambda i: (0, 0))],
        out_specs=[pl.BlockSpec((MOE_T, D_MODEL), lambda i: (i, 0)),
                   pl.BlockSpec((N_EXPERTS, MOE_T), lambda i: (0, i)),
                   pl.BlockSpec((N_EXPERTS, MOE_T), lambda i: (0, i))],
        out_shape=[jax.ShapeDtypeStruct((N_TOK, D_MODEL), BF16),
                   jax.ShapeDtypeStruct((N_EXPERTS, N_TOK), jnp.int32),
                   jax.ShapeDtypeStruct((N_EXPERTS, N_TOK), F32)],
        compiler_params=_params(("arbitrary",)),
        name="moe_router",
    )(x, mod, g.reshape(1, D_MODEL), rw, rb)


def _moe_sparse_kernel(h_ref, pos_ref, wgt_ref, w1_ref, w3_ref, w2_ref, y_ref, xg_scr, acc_scr, wr_scr):
    e = pl.program_id(1)
    f = pl.program_id(2)
    t = h_ref.shape[0]
    pos_e = pos_ref[pl.ds(e, 1), :]
    n_blocks = (jnp.max(pos_e) + MOE_R) // MOE_R
    row_id = lax.broadcasted_iota(jnp.int32, (MOE_R, t), 0)

    def block_rows(r):
        return pl.ds(pl.multiple_of(r * MOE_R, MOE_R), MOE_R)

    def selects(r):
        return pos_e == row_id + r * MOE_R

    @pl.when((e == 0) & (f == 0))
    def _():
        y_ref[...] = jnp.zeros_like(y_ref)

    @pl.when(f == 0)
    def _():
        wgt_e = wgt_ref[pl.ds(e, 1), :]

        def gather(r, carry):
            sel = selects(r)
            xg = jnp.dot(jnp.where(sel, 1.0, 0.0).astype(BF16), h_ref[...], preferred_element_type=F32)
            xg_scr[block_rows(r), :] = xg.astype(BF16)
            w_rows = jnp.sum(jnp.where(sel, wgt_e, 0.0), axis=1, keepdims=True)
            wr_scr[block_rows(r), :] = jnp.broadcast_to(w_rows, (MOE_R, 128))
            acc_scr[block_rows(r), :] = jnp.zeros((MOE_R, D_MODEL), F32)
            return carry

        lax.fori_loop(0, n_blocks, gather, 0)

    def expert(r, carry):
        xg = xg_scr[block_rows(r), :]
        a = jnp.dot(xg, w1_ref[...], preferred_element_type=F32)
        b = jnp.dot(xg, w3_ref[...], preferred_element_type=F32)
        hid = (a * jax.nn.sigmoid(a) * b * wr_scr[block_rows(r), 0:1]).astype(BF16)
        acc_scr[block_rows(r), :] += jnp.dot(hid, w2_ref[...], preferred_element_type=F32)
        return carry

    lax.fori_loop(0, n_blocks, expert, 0)

    @pl.when(f == pl.num_programs(2) - 1)
    def _():
        def scatter(r, carry):
            onehot = jnp.where(selects(r), 1.0, 0.0).astype(BF16)
            hi, lo = _split_bf16(acc_scr[block_rows(r), :])
            y_ref[...] += (lax.dot_general(onehot, hi, _TN, preferred_element_type=F32)
                           + lax.dot_general(onehot, lo, _TN, preferred_element_type=F32))
            return carry

        lax.fori_loop(0, n_blocks, scatter, 0)


def moe_experts(h2, pos, wgt, w1, w3, w2):
    nt = N_TOK // MOE_T
    nf = D_FF_EXPERT // MOE_F
    return pl.pallas_call(
        _moe_sparse_kernel,
        grid=(nt, N_EXPERTS, nf),
        in_specs=[pl.BlockSpec((MOE_T, D_MODEL), lambda i, e, f: (i, 0)),
                  pl.BlockSpec((N_EXPERTS, MOE_T), lambda i, e, f: (0, i)),
                  pl.BlockSpec((N_EXPERTS, MOE_T), lambda i, e, f: (0, i)),
                  pl.BlockSpec((None, D_MODEL, MOE_F), lambda i, e, f: (e, 0, f)),
                  pl.BlockSpec((None, D_MODEL, MOE_F), lambda i, e, f: (e, 0, f)),
                  pl.BlockSpec((None, MOE_F, D_MODEL), lambda i, e, f: (e, f, 0))],
        out_specs=pl.BlockSpec((MOE_T, D_MODEL), lambda i, e, f: (i, 0)),
        out_shape=jax.ShapeDtypeStruct((N_TOK, D_MODEL), F32),
        scratch_shapes=[pltpu.VMEM((MOE_T, D_MODEL), BF16),
                        pltpu.VMEM((MOE_T, D_MODEL), F32),
                        pltpu.VMEM((MOE_T, 128), F32)],
        compiler_params=_params(("arbitrary", "arbitrary", "arbitrary")),
        name="moe_experts",
    )(h2, pos, wgt, w1, w3, w2)


def _residual_norm_kernel(x_ref, y_ref, mod_ref, g_ref, o_ref):
    o_ref[...] = _rms(x_ref[...] + mod_ref[5:6, :] * y_ref[...], g_ref[...])


def residual_final_norm(x, y, mod, g):
    tm = 1024
    return pl.pallas_call(
        _residual_norm_kernel,
        grid=(N_TOK // tm,),
        in_specs=[pl.BlockSpec((tm, D_MODEL), lambda i: (i, 0)),
                  pl.BlockSpec((tm, D_MODEL), lambda i: (i, 0)),
                  pl.BlockSpec((None, 6, D_MODEL), lambda i: (_tile_cond(i, tm), 0, 0)),
                  pl.BlockSpec((1, D_MODEL), lambda i: (0, 0))],
        out_specs=pl.BlockSpec((tm, D_MODEL), lambda i: (i, 0)),
        out_shape=jax.ShapeDtypeStruct((N_TOK, D_MODEL), F32),
        compiler_params=_params(("arbitrary",)),
        name="residual_final_norm",
    )(x, y, mod, g.reshape(1, D_MODEL))


ATT_SCALE = HEAD_DIM ** -0.5
_NT = (((1,), (1,)), ((), ()))


def _head_rms(x, g_row):
    outs = []
    for h in range(x.shape[1] // HEAD_DIM):
        xh = x[:, h * HEAD_DIM:(h + 1) * HEAD_DIM]
        outs.append(xh * lax.rsqrt(jnp.mean(xh * xh, axis=-1, keepdims=True) + EPS) * g_row)
    return jnp.concatenate(outs, axis=-1)


def _rope_apply(x, c, s):
    w = x.shape[-1]
    lane = lax.broadcasted_iota(jnp.int32, x.shape, 1)
    first_half = ((lane // (HEAD_DIM // 4)) % 2) == 0
    partner = jnp.where(first_half, pltpu.roll(x, w - HEAD_DIM // 4, 1), pltpu.roll(x, HEAD_DIM // 4, 1))
    return x * c + partner * s


def rope_lane_tables(L):
    rows = L // GRID_W
    row = jnp.repeat(jnp.arange(rows, dtype=F32), GRID_W)
    col = jnp.tile(jnp.arange(GRID_W, dtype=F32), rows)
    n_freq = HEAD_DIM // 4
    inv = ROPE_BASE ** (-jnp.arange(n_freq, dtype=F32) / n_freq)
    ang = jnp.stack([row, col], 0)[:, :, None] * inv
    cos, sin = jnp.cos(ang), jnp.sin(ang)
    c = jnp.concatenate([cos[0], cos[0], cos[1], cos[1]], axis=-1)
    s = jnp.concatenate([-sin[0], sin[0], -sin[1], sin[1]], axis=-1)
    return jnp.tile(c, (1, 4)), jnp.tile(s, (1, 4))


def _attend_heads(jobs):
    for job in jobs:
        q = job['q'].astype(BF16)
        job['s'] = lax.dot_general(q, job['k'], _NT, preferred_element_type=F32) * ATT_SCALE
        if job.get('extra') is not None:
            job['s2'] = lax.dot_general(q, job['extra'][0], _NT, preferred_element_type=F32) * ATT_SCALE
    outs = []
    for job in jobs:
        s, sink = job['s'], job.get('sink')
        if job.get('mask') is not None:
            s = jnp.where(job['mask'], s, NEG_INF)
        m = jnp.max(s, axis=-1, keepdims=True)
        if 's2' in job:
            m = jnp.maximum(m, jnp.max(job['s2'], axis=-1, keepdims=True))
        if sink is not None:
            m = jnp.maximum(m, sink)
        p = jnp.exp(s - m)
        den = jnp.sum(p, axis=-1, keepdims=True)
        o = jnp.dot(p.astype(BF16), job['v'], preferred_element_type=F32)
        if 's2' in job:
            p2 = jnp.exp(job['s2'] - m)
            den = den + jnp.sum(p2, axis=-1, keepdims=True)
            o = o + jnp.dot(p2.astype(BF16), job['extra'][1], preferred_element_type=F32)
        if sink is not None:
            den = den + jnp.exp(sink - m)
        outs.append(o / den)
    return outs


def _ctx_attn_kernel(sink_ref, z_ref, gq_ref, gk_ref, o_ref, bk_ref):
    z = z_ref[...]
    bq = _head_rms(z[:, 512:768], gq_ref[...])
    bk = _head_rms(z[:, 768:896], gk_ref[...])
    bk_ref[...] = bk
    groups = ((z[:, 0:256], z[:, 256:384], z[:, 384:512], True),
              (bq, bk, z[:, 896:1024], False))
    outs = []
    for q_all, k_all, v_all, use_sink in groups:
        k_all = k_all.astype(BF16)
        v_all = v_all.astype(BF16)
        for hq in range(A_HEADS):
            kv = hq // (A_HEADS // A_KV_HEADS)
            sl = slice(kv * HEAD_DIM, (kv + 1) * HEAD_DIM)
            outs.append(dict(q=q_all[:, hq * HEAD_DIM:(hq + 1) * HEAD_DIM], k=k_all[:, sl], v=v_all[:, sl],
                             sink=sink_ref[hq] if use_sink else None))
    o_ref[...] = jnp.concatenate(_attend_heads(outs), axis=-1).astype(o_ref.dtype)


def ctx_attention(z, sink, gq, gk):
    return pl.pallas_call(
        _ctx_attn_kernel,
        grid=(BATCH,),
        in_specs=[pl.BlockSpec(memory_space=pltpu.SMEM),
                  pl.BlockSpec((SEQ, 1024), lambda b: (b, 0)),
                  pl.BlockSpec((1, HEAD_DIM), lambda b: (0, 0)),
                  pl.BlockSpec((1, HEAD_DIM), lambda b: (0, 0))],
        out_specs=[pl.BlockSpec((SEQ, 512), lambda b: (b, 0)),
                   pl.BlockSpec((SEQ, 128), lambda b: (b, 0))],
        out_shape=[jax.ShapeDtypeStruct((CTX_TOK, 512), BF16),
                   jax.ShapeDtypeStruct((CTX_TOK, 128), F32)],
        compiler_params=_params(("arbitrary",)),
        name="ctx_attention",
    )(sink, z, gq.reshape(1, HEAD_DIM), gk.reshape(1, HEAD_DIM))


LB_TQ = 256


def _lat_b_kernel(zq_ref, zkv_ref, ck_ref, cv_ref, cq_ref, sq_ref, ckk_ref, skk_ref, gq_ref, gk_ref,
                  o_ref, k_scr, v_scr):
    L = zkv_ref.shape[0]

    @pl.when(pl.program_id(1) == 0)
    def _():
        kv = zkv_ref[...]
        bk = _rope_apply(_head_rms(kv[:, :128], gk_ref[...]), ckk_ref[...], skk_ref[...])
        k_scr[0:L, :] = bk.astype(BF16)
        k_scr[L:L + PAST_LEN, :] = ck_ref[...].astype(BF16)
        v_scr[0:L, :] = kv[:, 128:].astype(BF16)
        v_scr[L:L + PAST_LEN, :] = cv_ref[...].astype(BF16)

    q = _rope_apply(_head_rms(zq_ref[...], gq_ref[...]), cq_ref[...], sq_ref[...])
    outs = []
    for hq in range(B_HEADS):
        kv = hq // (B_HEADS // B_KV_HEADS)
        sl = slice(kv * HEAD_DIM, (kv + 1) * HEAD_DIM)
        outs.append(dict(q=q[:, hq * HEAD_DIM:(hq + 1) * HEAD_DIM], k=k_scr[:, sl], v=v_scr[:, sl]))
    o_ref[...] = jnp.concatenate(_attend_heads(outs), axis=-1).astype(o_ref.dtype)


def latent_attention_b(z, row0, nb, L, cache_k, cache_v, rope_c, rope_s, gq, gk):
    nq = L // LB_TQ
    return pl.pallas_call(
        _lat_b_kernel,
        grid=(nb, nq),
        in_specs=[pl.BlockSpec((LB_TQ, 256), lambda b, i: (row0 // LB_TQ + b * nq + i, 2)),
                  pl.BlockSpec((L, 256), lambda b, i: (row0 // L + b, 3)),
                  pl.BlockSpec((None, PAST_LEN, 128), lambda b, i: (b, 0, 0)),
                  pl.BlockSpec((None, PAST_LEN, 128), lambda b, i: (b, 0, 0)),
                  pl.BlockSpec((LB_TQ, 256), lambda b, i: (i, 0)),
                  pl.BlockSpec((LB_TQ, 256), lambda b, i: (i, 0)),
                  pl.BlockSpec((L, 128), lambda b, i: (0, 0)),
                  pl.BlockSpec((L, 128), lambda b, i: (0, 0)),
                  pl.BlockSpec((1, HEAD_DIM), lambda b, i: (0, 0)),
                  pl.BlockSpec((1, HEAD_DIM), lambda b, i: (0, 0))],
        out_specs=pl.BlockSpec((LB_TQ, 256), lambda b, i: (b * nq + i, 0)),
        out_shape=jax.ShapeDtypeStruct((nb * L, 256), BF16),
        scratch_shapes=[pltpu.VMEM((L + PAST_LEN, 128), BF16),
                        pltpu.VMEM((L + PAST_LEN, 128), BF16)],
        compiler_params=_params(("arbitrary", "arbitrary")),
        name="latent_attention_b",
    )(z, z, cache_k, cache_v, rope_c, rope_s, rope_c, rope_s, gq.reshape(1, HEAD_DIM), gk.reshape(1, HEAD_DIM))


def _lat_a_kernel(sink_ref, zq_ref, zkv_ref, ck_ref, cv_ref, cq_ref, sq_ref, ckk_ref, skk_ref,
                  o_ref, k_scr, v_scr, ck_scr, cv_scr):
    L = zkv_ref.shape[0]
    i = pl.program_id(1)

    @pl.when(i == 0)
    def _():
        kv = zkv_ref[...]
        zeros = jnp.zeros((BLOCK, 128), BF16)
        k_scr[0:BLOCK, :] = zeros
        v_scr[0:BLOCK, :] = zeros
        k_scr[BLOCK:BLOCK + L, :] = _rope_apply(kv[:, :128], ckk_ref[...], skk_ref[...]).astype(BF16)
        v_scr[BLOCK:BLOCK + L, :] = kv[:, 128:].astype(BF16)
        k_scr[BLOCK + L:2 * BLOCK + L, :] = zeros
        v_scr[BLOCK + L:2 * BLOCK + L, :] = zeros
        ck_scr[...] = ck_ref[...].astype(BF16)
        cv_scr[...] = cv_ref[...].astype(BF16)

    q = _rope_apply(zq_ref[...], cq_ref[...], sq_ref[...])
    start = pl.multiple_of(i * BLOCK, BLOCK)
    kband = k_scr[pl.ds(start, 3 * BLOCK), :]
    vband = v_scr[pl.ds(start, 3 * BLOCK), :]
    r = lax.broadcasted_iota(jnp.int32, (BLOCK, 3 * BLOCK), 0)
    cidx = lax.broadcasted_iota(jnp.int32, (BLOCK, 3 * BLOCK), 1)
    kpos = i * BLOCK - BLOCK + cidx
    mask = (jnp.abs(cidx - BLOCK - r) <= WINDOW) & (kpos >= 0) & (kpos < L)
    outs = []
    for hq in range(A_HEADS):
        kv = hq // (A_HEADS // A_KV_HEADS)
        sl = slice(kv * HEAD_DIM, (kv + 1) * HEAD_DIM)
        outs.append(dict(q=q[:, hq * HEAD_DIM:(hq + 1) * HEAD_DIM], k=kband[:, sl], v=vband[:, sl],
                         extra=(ck_scr[:, sl], cv_scr[:, sl]), sink=sink_ref[hq], mask=mask))
    o_ref[...] = jnp.concatenate(_attend_heads(outs), axis=-1).astype(o_ref.dtype)


def latent_attention_a(z, row0, nb, L, sink, cache_k, cache_v, rope_c, rope_s):
    nq = L // BLOCK
    return pl.pallas_call(
        _lat_a_kernel,
        grid=(nb, nq),
        in_specs=[pl.BlockSpec(memory_space=pltpu.SMEM),
                  pl.BlockSpec((BLOCK, 256), lambda b, i: (row0 // BLOCK + b * nq + i, 0)),
                  pl.BlockSpec((L, 256), lambda b, i: (row0 // L + b, 1)),
                  pl.BlockSpec((None, PAST_LEN, 128), lambda b, i: (b, 0, 0)),
                  pl.BlockSpec((None, PAST_LEN, 128), lambda b, i: (b, 0, 0)),
                  pl.BlockSpec((BLOCK, 256), lambda b, i: (i, 0)),
                  pl.BlockSpec((BLOCK, 256), lambda b, i: (i, 0)),
                  pl.BlockSpec((L, 128), lambda b, i: (0, 0)),
                  pl.BlockSpec((L, 128), lambda b, i: (0, 0))],
        out_specs=pl.BlockSpec((BLOCK, 256), lambda b, i: (b * nq + i, 0)),
        out_shape=jax.ShapeDtypeStruct((nb * L, 256), BF16),
        scratch_shapes=[pltpu.VMEM((L + 2 * BLOCK, 128), BF16),
                        pltpu.VMEM((L + 2 * BLOCK, 128), BF16),
                        pltpu.VMEM((PAST_LEN, 128), BF16),
                        pltpu.VMEM((PAST_LEN, 128), BF16)],
        compiler_params=_params(("arbitrary", "arbitrary")),
        name="latent_attention_a",
    )(sink, z, z, cache_k, cache_v, rope_c, rope_s, rope_c, rope_s)


_TN = (((0,), (0,)), ((), ()))
HG_GROUP = 4


def _hgrn_kernel(zq_ref, zf_ref, zi_ref, zg_ref, lb_ref, gn_ref, s0_ref, o_ref, sT_ref,
                 of_scr, ob_scr, logf_scr, k_scr, S_scr, *, tt):
    d = pl.program_id(1)
    j = pl.program_id(2)
    n_t = pl.num_programs(2)
    C = HGRN_CHUNK
    n_c = tt // C

    @pl.when(j == 0)
    def _():
        S_scr[...] = s0_ref[...]

    lb = lb_ref[...]
    zf = zf_ref[...]
    logf_scr[...] = jnp.log(lb + (1.0 - lb) * jax.nn.sigmoid(zf))
    k_scr[...] = (1.0 - lb) * jax.nn.sigmoid(-zf)

    row = lax.broadcasted_iota(jnp.int32, (C, C), 0)
    col = lax.broadcasted_iota(jnp.int32, (C, C), 1)

    def run(reverse, tile):
        tri = (row <= col) if reverse else (row >= col)
        cum = tri.astype(F32)

        def body(gi, carry):
            heads = [slice(h * HEAD_DIM, (h + 1) * HEAD_DIM) for h in range(C_HEADS)]
            chunks = []
            for g in range(HG_GROUP):
                ci = gi * HG_GROUP + g
                c = (n_c - 1 - ci) if reverse else ci
                r0 = pl.multiple_of(c * C, C)
                G = jnp.dot(cum, logf_scr[pl.ds(r0, C), :], precision=lax.Precision.HIGHEST,
                            preferred_element_type=F32)
                G_end = G[0:1, :] if reverse else G[C - 1:C, :]
                kc = k_scr[pl.ds(r0, C), :]
                vc = zi_ref[pl.ds(r0, C), :].astype(BF16)
                q_in = (zq_ref[pl.ds(r0, C), :] * jnp.exp(G)).astype(BF16)
                k_in = (kc * jnp.exp(-G)).astype(BF16)
                k_out = (kc * jnp.exp(G_end - G)).astype(BF16)
                chunks.append(dict(
                    r0=r0, vc=vc, q_in=q_in, decay=jnp.exp(G_end),
                    a=[lax.dot_general(q_in[:, sl], k_in[:, sl], _NT, preferred_element_type=F32) for sl in heads],
                    kv=[lax.dot_general(vc[:, sl], k_out[:, sl], _TN, preferred_element_type=F32) for sl in heads]))
            s_cur = [S_scr[h] for h in range(C_HEADS)]
            for ch in chunks:
                ch['qs'] = [lax.dot_general(ch['q_in'][:, sl], s_cur[h].astype(BF16), _NT,
                                            preferred_element_type=F32) for h, sl in enumerate(heads)]
                s_cur = [s_cur[h] * ch['decay'][:, sl] + ch['kv'][h] for h, sl in enumerate(heads)]
            for h in range(C_HEADS):
                S_scr[h] = s_cur[h]
            for ch in chunks:
                o_c = jnp.concatenate(
                    [jnp.dot(jnp.where(tri, ch['a'][h], 0.0).astype(BF16), ch['vc'][:, sl],
                             preferred_element_type=F32) + ch['qs'][h] for h, sl in enumerate(heads)], axis=-1)
                if reverse:
                    ob_scr[pl.ds(ch['r0'], C), :] = o_c
                else:
                    of_scr[pl.ds(pl.multiple_of(tile * tt, tt) + ch['r0'], C), :] = o_c
            return carry

        lax.fori_loop(0, n_c // HG_GROUP, body, 0)

    @pl.when(d == 0)
    def _():
        run(False, j)

    @pl.when(d == 1)
    def _():
        tile = n_t - 1 - j
        run(True, tile)
        o = of_scr[pl.ds(pl.multiple_of(tile * tt, tt), tt), :] + ob_scr[...]
        g = zg_ref[...]
        o_ref[...] = (_head_rms(o, gn_ref[...]) * (g * jax.nn.sigmoid(g))).astype(o_ref.dtype)

    @pl.when(j == n_t - 1)
    def _():
        sT_ref[...] = S_scr[...]


def hgrn_mixer(z, row0, nb, L, lb, gn, s0_t):
    tt = min(L, 512)
    n_t = L // tt
    rb = row0 // tt

    def tile(d, j):
        return jnp.where(d == 0, j, n_t - 1 - j)

    def late(d, j):
        return jnp.where(d == 0, n_t - 1, n_t - 1 - j)

    st_spec = pl.BlockSpec((None, None, C_HEADS, HEAD_DIM, HEAD_DIM), lambda b, d, j: (b, d, 0, 0, 0))
    return pl.pallas_call(
        functools.partial(_hgrn_kernel, tt=tt),
        grid=(nb, 2, n_t),
        in_specs=[pl.BlockSpec((tt, 256), lambda b, d, j: (rb + b * n_t + tile(d, j), 4)),
                  pl.BlockSpec((tt, 256), lambda b, d, j: (rb + b * n_t + tile(d, j), 5 + d)),
                  pl.BlockSpec((tt, 256), lambda b, d, j: (rb + b * n_t + tile(d, j), 7)),
                  pl.BlockSpec((tt, 256), lambda b, d, j: (rb + b * n_t + late(d, j), 8)),
                  pl.BlockSpec((1, 256), lambda b, d, j: (0, 0)),
                  pl.BlockSpec((1, HEAD_DIM), lambda b, d, j: (0, 0)),
                  st_spec],
        out_specs=[pl.BlockSpec((tt, 256), lambda b, d, j: (b * n_t + late(d, j), 0)),
                   st_spec],
        out_shape=[jax.ShapeDtypeStruct((nb * L, 256), BF16),
                   jax.ShapeDtypeStruct((nb, 2, C_HEADS, HEAD_DIM, HEAD_DIM), F32)],
        scratch_shapes=[pltpu.VMEM((L, 256), F32),
                        pltpu.VMEM((tt, 256), F32),
                        pltpu.VMEM((tt, 256), F32),
                        pltpu.VMEM((tt, 256), F32),
                        pltpu.VMEM((C_HEADS, HEAD_DIM, HEAD_DIM), F32)],
        compiler_params=_params(("arbitrary", "arbitrary", "arbitrary")),
        name="hgrn_mixer",
    )(z, z, z, z, lb.reshape(1, 256), gn.reshape(1, HEAD_DIM), s0_t)


DL_C = DELTA_CHUNK
DL_PREP_TT = 256
DL_HALO = 8
DL_CHUNK_TT = 512
N_QKV_HEADS = 3 * D_HEADS


def _delta_prep_kernel(x_ref, xp_ref, xn_ref, zab_ref, cw_ref, na_ref, dtb_ref, qkv_ref, gate_ref, xs_scr):
    tt = x_ref.shape[0]
    row = pl.program_id(0) * tt
    lat = row - CTX_TOK
    first = jnp.where(row < CTX_TOK, True, lat % DEC_SEQ == 0)
    last = jnp.where(row < CTX_TOK, True, (lat + tt) % DEC_SEQ == 0)
    xs_scr[DL_HALO:DL_HALO + tt, :] = x_ref[...]
    xs_scr[0:DL_HALO, :] = jnp.where(first, 0.0, xp_ref[...])
    xs_scr[DL_HALO + tt:2 * DL_HALO + tt, :] = jnp.where(last, 0.0, xn_ref[...])
    pad = (CONV_K - 1) // 2
    y = None
    for t in range(CONV_K):
        term = xs_scr[pl.ds(DL_HALO - pad + t, tt), :] * cw_ref[t:t + 1, :]
        y = term if y is None else y + term
    y = y * jax.nn.sigmoid(y)
    for idx in range(N_QKV_HEADS):
        xh = y[:, idx * HEAD_DIM:(idx + 1) * HEAD_DIM]
        if idx < 2 * D_HEADS:
            xh = xh * lax.rsqrt(jnp.sum(xh * xh, axis=-1, keepdims=True) + EPS)
        if idx < D_HEADS:
            xh = xh * ATT_SCALE
        qkv_ref[idx] = xh
    zab = zab_ref[...]
    lane = lax.broadcasted_iota(jnp.int32, zab.shape, 1)
    t_ = zab + dtb_ref[...]
    softplus = jnp.maximum(t_, 0.0) + jnp.log(1.0 + jnp.exp(-jnp.abs(t_)))
    gate_ref[...] = jnp.where(lane < 2 * D_HEADS, na_ref[...] * softplus, jax.nn.sigmoid(zab))


def delta_prep(z, conv_w, a_log, dt_bias):
    tt = DL_PREP_TT
    hb = tt // DL_HALO
    n_hb = N_TOK // DL_HALO
    pad8 = lambda v: jnp.concatenate([v.reshape(1, 2 * D_HEADS), jnp.zeros((1, 128 - 2 * D_HEADS), F32)], axis=1)
    return pl.pallas_call(
        _delta_prep_kernel,
        grid=(N_TOK // tt,),
        in_specs=[pl.BlockSpec((tt, 768), lambda i: (i, 3)),
                  pl.BlockSpec((DL_HALO, 768), lambda i: (jnp.maximum(i * hb - 1, 0), 3)),
                  pl.BlockSpec((DL_HALO, 768), lambda i: (jnp.minimum((i + 1) * hb, n_hb - 1), 3)),
                  pl.BlockSpec((tt, 128), lambda i: (i, Z_DAB // 128)),
                  pl.BlockSpec((CONV_K, 768), lambda i: (0, 0)),
                  pl.BlockSpec((1, 128), lambda i: (0, 0)),
                  pl.BlockSpec((1, 128), lambda i: (0, 0))],
        out_specs=[pl.BlockSpec((N_QKV_HEADS, tt, HEAD_DIM), lambda i: (0, i, 0)),
                   pl.BlockSpec((tt, 128), lambda i: (i, 0))],
        out_shape=[jax.ShapeDtypeStruct((N_QKV_HEADS, N_TOK, HEAD_DIM), F32),
                   jax.ShapeDtypeStruct((N_TOK, 128), F32)],
        scratch_shapes=[pltpu.VMEM((tt + 2 * DL_HALO, 768), F32)],
        compiler_params=_params(("arbitrary",)),
        name="delta_prep",
    )(z, z, z, z, conv_w, pad8(-jnp.exp(a_log)), pad8(dt_bias))


def _split_bf16(a):
    hi = a.astype(BF16)
    return hi, (a - hi.astype(F32)).astype(BF16)


def _dot_hl(a, b_parts):
    a_hi, a_lo = _split_bf16(a)
    b_hi, b_lo = b_parts
    m = a.shape[0]
    r = jnp.dot(jnp.concatenate([a_hi, a_lo], axis=0), b_hi, preferred_element_type=F32)
    return r[:m] + r[m:] + jnp.dot(a_hi, b_lo, preferred_element_type=F32)


def _delta_chunk_kernel(qkv_ref, gate_ref, u2_ref, wq_ref, ak_ref):
    C = DL_C
    n_c = gate_ref.shape[0] // C
    row = lax.broadcasted_iota(jnp.int32, (C, C), 0)
    col = lax.broadcasted_iota(jnp.int32, (C, C), 1)
    eye = (row == col).astype(F32)

    def body(c, carry):
        r0 = pl.multiple_of(c * C, C)
        ga = gate_ref[pl.ds(r0, C), :]
        chains = []
        for d in range(2):
            incl = (row >= col) if d == 0 else (row <= col)
            strict = (row > col) if d == 0 else (row < col)
            g_all = jnp.dot(incl.astype(F32), ga, precision=lax.Precision.HIGHEST, preferred_element_type=F32)
            g_all_t = g_all.T
            for h in range(D_HEADS):
                ci = d * D_HEADS + h
                q = qkv_ref[h, pl.ds(r0, C), :]
                k = qkv_ref[D_HEADS + h, pl.ds(r0, C), :]
                v = qkv_ref[2 * D_HEADS + h, pl.ds(r0, C), :]
                g_col = g_all[:, ci:ci + 1]
                g_row = g_all_t[ci:ci + 1, :]
                beta = ga[:, 2 * D_HEADS + ci:2 * D_HEADS + ci + 1]
                g_end = g_col[C - 1:C, :] if d == 0 else g_col[0:1, :]
                kb = k * beta
                eg = jnp.exp(g_col)
                decay = jnp.where(incl, jnp.exp(jnp.where(incl, g_col - g_row, 0.0)), 0.0)
                kq = jnp.concatenate([kb, q], axis=0).astype(BF16)
                chains.append(dict(
                    strict=strict, decay=decay, qg=q * eg, g_end=g_end,
                    r=lax.dot_general(kq, k.astype(BF16), _NT, preferred_element_type=F32),
                    rhs=jnp.concatenate([v * beta, kb * eg], axis=1).astype(BF16),
                    ke_t=(k * jnp.exp(g_end - g_col)).T))
        for ch in chains:
            ch['p'] = -jnp.where(ch['strict'], ch['r'][:C] * ch['decay'], 0.0)
            ch['t'] = eye + ch['p']
        for _ in range(5):
            for ch in chains:
                ch['p'] = _dot_hl(ch['p'], _split_bf16(ch['p']))
            for ch in chains:
                ch['t'] = ch['t'] + _dot_hl(ch['t'], _split_bf16(ch['p']))
        for ch in chains:
            ch['uw'] = jnp.dot(ch['t'].astype(BF16), ch['rhs'], preferred_element_type=F32)
        u2 = [jnp.concatenate([ch['uw'][:, :C], jnp.broadcast_to(jnp.exp(ch['g_end']), (C, C))], axis=1)
              for ch in chains]
        wq = [jnp.concatenate([ch['uw'][:, C:], ch['qg']], axis=0).astype(BF16) for ch in chains]
        ak = [jnp.concatenate([ch['r'][C:] * ch['decay'], ch['ke_t']], axis=0).astype(BF16) for ch in chains]
        pack = lambda xs: jnp.stack(xs).reshape((2, D_HEADS) + xs[0].shape)
        u2_ref[:, :, pl.ds(r0, C), :] = pack(u2)
        wq_ref[:, :, c] = pack(wq)
        ak_ref[:, :, c] = pack(ak)
        return carry

    lax.fori_loop(0, n_c, body, 0)


def delta_chunks(qkv, gates):
    tt = DL_CHUNK_TT
    n_c = tt // DL_C
    return pl.pallas_call(
        _delta_chunk_kernel,
        grid=(N_TOK // tt,),
        in_specs=[pl.BlockSpec((N_QKV_HEADS, tt, HEAD_DIM), lambda i: (0, i, 0)),
                  pl.BlockSpec((tt, 128), lambda i: (i, 0))],
        out_specs=[pl.BlockSpec((2, D_HEADS, tt, 128), lambda i: (0, 0, i, 0)),
                   pl.BlockSpec((2, D_HEADS, n_c, 2 * DL_C, HEAD_DIM), lambda i: (0, 0, i, 0, 0)),
                   pl.BlockSpec((2, D_HEADS, n_c, 2 * DL_C, HEAD_DIM), lambda i: (0, 0, i, 0, 0))],
        out_shape=[jax.ShapeDtypeStruct((2, D_HEADS, N_TOK, 128), F32),
                   jax.ShapeDtypeStruct((2, D_HEADS, N_TOK // DL_C, 2 * DL_C, HEAD_DIM), BF16),
                   jax.ShapeDtypeStruct((2, D_HEADS, N_TOK // DL_C, 2 * DL_C, HEAD_DIM), BF16)],
        compiler_params=_params(("arbitrary",)),
        name="delta_chunks",
    )(qkv, gates)


def _delta_scan_kernel(u2f_ref, wqf_ref, akf_ref, u2b_ref, wqb_ref, akb_ref, s0_ref,
                       of_ref, ob_ref, s_ref, s_scr):
    j = pl.program_id(1)
    C = DL_C
    n_c = wqf_ref.shape[1]

    @pl.when(j == 0)
    def _():
        s_scr[...] = s0_ref[...]

    def body(ci, carry):
        chains = []
        for d, (u2_ref, wq_ref, ak_ref, o_ref) in enumerate(((u2f_ref, wqf_ref, akf_ref, of_ref),
                                                            (u2b_ref, wqb_ref, akb_ref, ob_ref))):
            c = ci if d == 0 else n_c - 1 - ci
            r0 = pl.multiple_of(c * C, C)
            for h in range(D_HEADS):
                s = s_scr[d, h]
                chains.append(dict(d=d, h=h, c=c, r0=r0, s=s, ak_ref=ak_ref, o_ref=o_ref,
                                   u2=u2_ref[h, pl.ds(r0, C), :],
                                   r1=jnp.dot(wq_ref[h, c], s.astype(BF16),
                                              preferred_element_type=F32)))
        for ch in chains:
            v_new = ch['u2'][:, :C] - ch['r1'][:C]
            ch['r2'] = jnp.dot(ch['ak_ref'][ch['h'], ch['c']], v_new.astype(BF16),
                               preferred_element_type=F32)
        for ch in chains:
            ch['o_ref'][ch['h'], pl.ds(ch['r0'], C), :] = ch['r1'][C:] + ch['r2'][:C]
            s_scr[ch['d'], ch['h']] = ch['s'] * ch['u2'][0:1, C:] + ch['r2'][C:]
        return carry

    lax.fori_loop(0, n_c, body, 0)

    @pl.when(j == pl.num_programs(1) - 1)
    def _():
        s_ref[...] = s_scr[...]


def delta_scan(u2, wq, ak, row0, nb, L, s0):
    tt = min(L, 512)
    n_t = L // tt
    n_c = tt // DL_C
    rb = row0 // tt
    fwd = lambda b, j: rb + b * n_t + j
    bwd = lambda b, j: rb + b * n_t + (n_t - 1 - j)
    u_spec = lambda d, f: pl.BlockSpec((None, D_HEADS, tt, 128), lambda b, j: (d, 0, f(b, j), 0))
    c_spec = lambda d, f: pl.BlockSpec((None, D_HEADS, n_c, 2 * DL_C, HEAD_DIM), lambda b, j: (d, 0, f(b, j), 0, 0))
    st_spec = pl.BlockSpec((None, 2, D_HEADS, HEAD_DIM, HEAD_DIM), lambda b, j: (b, 0, 0, 0, 0))
    return pl.pallas_call(
        _delta_scan_kernel,
        grid=(nb, n_t),
        in_specs=[u_spec(0, fwd), c_spec(0, fwd), c_spec(0, fwd),
                  u_spec(1, bwd), c_spec(1, bwd), c_spec(1, bwd), st_spec],
        out_specs=[pl.BlockSpec((D_HEADS, tt, HEAD_DIM), lambda b, j: (0, b * n_t + j, 0)),
                   pl.BlockSpec((D_HEADS, tt, HEAD_DIM), lambda b, j: (0, b * n_t + (n_t - 1 - j), 0)),
                   st_spec],
        out_shape=[jax.ShapeDtypeStruct((D_HEADS, nb * L, HEAD_DIM), F32),
                   jax.ShapeDtypeStruct((D_HEADS, nb * L, HEAD_DIM), F32),
                   jax.ShapeDtypeStruct((nb, 2, D_HEADS, HEAD_DIM, HEAD_DIM), F32)],
        scratch_shapes=[pltpu.VMEM((2, D_HEADS, HEAD_DIM, HEAD_DIM), F32)],
        compiler_params=_params(("arbitrary", "arbitrary")),
        name="delta_scan",
    )(u2, wq, ak, u2, wq, ak, s0)


def _delta_out_kernel(of_ref, ob_ref, zg_ref, gn_ref, o_ref):
    outs = []
    for h in range(D_HEADS):
        o = of_ref[h] + ob_ref[h]
        outs.append(o * lax.rsqrt(jnp.mean(o * o, axis=-1, keepdims=True) + EPS) * gn_ref[...])
    g = zg_ref[...]
    o_ref[...] = (jnp.concatenate(outs, axis=-1) * (g * jax.nn.sigmoid(g))).astype(o_ref.dtype)


def delta_output(o_f, o_b, z, row0, gn):
    n = o_f.shape[1]
    tt = 256
    return pl.pallas_call(
        _delta_out_kernel,
        grid=(n // tt,),
        in_specs=[pl.BlockSpec((D_HEADS, tt, HEAD_DIM), lambda i: (0, i, 0)),
                  pl.BlockSpec((D_HEADS, tt, HEAD_DIM), lambda i: (0, i, 0)),
                  pl.BlockSpec((tt, 256), lambda i: (row0 // tt + i, Z_DG // 256)),
                  pl.BlockSpec((1, HEAD_DIM), lambda i: (0, 0))],
        out_specs=pl.BlockSpec((tt, 256), lambda i: (i, 0)),
        out_shape=jax.ShapeDtypeStruct((n, 256), BF16),
        compiler_params=_params(("arbitrary",)),
        name="delta_output",
    )(o_f, o_b, z, gn.reshape(1, HEAD_DIM))


def _rms_heads(x, g):
    return x * lax.rsqrt(jnp.mean(x * x, axis=-1, keepdims=True) + EPS) * g


def _l2n(x):
    return x * lax.rsqrt(jnp.sum(x * x, axis=-1, keepdims=True) + EPS)


def _flip(t):
    return jnp.flip(t, axis=1)


def _rope_tables(L):
    rows = L // GRID_W
    row = jnp.repeat(jnp.arange(rows, dtype=F32), GRID_W)
    col = jnp.tile(jnp.arange(GRID_W, dtype=F32), rows)
    n_freq = HEAD_DIM // 4
    inv = ROPE_BASE ** (-jnp.arange(n_freq, dtype=F32) / n_freq)
    ang = jnp.stack([row, col], 0)[:, :, None] * inv
    return jnp.cos(ang), jnp.sin(ang)


def _rope(x, tables):
    cos, sin = tables
    half = HEAD_DIM // 2
    nf = HEAD_DIM // 4
    outs = []
    for a in range(2):
        xa = x[..., a * half:(a + 1) * half]
        x1, x2 = xa[..., :nf], xa[..., nf:]
        ca = cos[a][None, :, None, :]
        sa = sin[a][None, :, None, :]
        outs += [x1 * ca - x2 * sa, x2 * ca + x1 * sa]
    return jnp.concatenate(outs, axis=-1)


def _dense_gqa(q, k, v, sink):
    Bn, Lq, Hq, d = q.shape
    Hkv = k.shape[2]
    G = Hq // Hkv
    nb = Lq // BLOCK
    scale = d ** -0.5
    qb = q.reshape(Bn, nb, BLOCK, Hkv, G, d).transpose(1, 0, 2, 3, 4, 5)

    def one(qblk):
        s = jnp.einsum('bqhgd,bkhd->bhgqk', qblk, k) * scale
        if sink is not None:
            sl = jnp.broadcast_to(sink.reshape(Hkv, G)[None, :, :, None, None], s.shape[:-1] + (1,))
            p = jax.nn.softmax(jnp.concatenate([s, sl], axis=-1), axis=-1)[..., :-1]
        else:
            p = jax.nn.softmax(s, axis=-1)
        return jnp.einsum('bhgqk,bkhd->bqhgd', p, v)

    o = lax.map(one, qb)
    return o.transpose(1, 0, 2, 3, 4, 5).reshape(Bn, Lq, Hq * d)


def _windowed_gqa(q, k, v, kc, vc, sink):
    Bn, L, Hq, d = q.shape
    Hkv = k.shape[2]
    G = Hq // Hkv
    nb = L // BLOCK
    scale = d ** -0.5
    qb = q.reshape(Bn, nb, BLOCK, Hkv, G, d)

    def band(t):
        tb = t.reshape(Bn, nb, BLOCK, Hkv, d)
        z = jnp.zeros_like(tb[:, :1])
        prev = jnp.concatenate([z, tb[:, :-1]], axis=1)
        nxt = jnp.concatenate([tb[:, 1:], z], axis=1)
        return jnp.concatenate([prev, tb, nxt], axis=2)

    kband, vband = band(k), band(v)
    s_loc = jnp.einsum('bnqhgd,bnkhd->bnhgqk', qb, kband) * scale
    s_ctx = jnp.einsum('bnqhgd,bkhd->bnhgqk', qb, kc) * scale
    qpos = jnp.arange(nb)[:, None, None] * BLOCK + jnp.arange(BLOCK)[None, :, None]
    kpos = jnp.arange(nb)[:, None, None] * BLOCK - BLOCK + jnp.arange(3 * BLOCK)[None, None, :]
    mask = (jnp.abs(kpos - qpos) <= WINDOW) & (kpos >= 0) & (kpos < L)
    s_loc = jnp.where(mask[None, :, None, None], s_loc, NEG_INF)
    sl = jnp.broadcast_to(sink.reshape(Hkv, G)[None, None, :, :, None, None], s_loc.shape[:-1] + (1,))
    p = jax.nn.softmax(jnp.concatenate([s_loc, s_ctx, sl], axis=-1), axis=-1)
    nk = 3 * BLOCK
    lc = kc.shape[1]
    o = (jnp.einsum('bnhgqk,bnkhd->bnqhgd', p[..., :nk], vband)
         + jnp.einsum('bnhgqk,bkhd->bnqhgd', p[..., nk:nk + lc], vc))
    return o.reshape(Bn, L, Hq * d)


def _gla_chunked(q, k, v, logf, s0):
    Bn, L, H, _ = q.shape
    C = HGRN_CHUNK
    n = L // C
    ch = lambda t: t.reshape(Bn, n, C, H, t.shape[-1]).transpose(1, 0, 3, 2, 4)
    q, k, v, logf = ch(q), ch(k), ch(v), ch(logf)
    G = jnp.cumsum(logf, axis=3)
    G_last = G[:, :, :, -1:, :]
    q_in = q * jnp.exp(G)
    k_in = k * jnp.exp(-G)
    k_out = k * jnp.exp(G_last - G)
    causal = jnp.tril(jnp.ones((C, C), bool))
    A = jnp.where(causal, jnp.einsum('nbhtd,nbhsd->nbhts', q_in, k_in), 0.0)
    o_intra = jnp.einsum('nbhts,nbhsv->nbhtv', A, v)

    def step(S, xs):
        qi, ko, vi, oi, gl = xs
        o = oi + jnp.einsum('bhtd,bhdv->bhtv', qi, S)
        S = jnp.exp(gl)[:, :, 0, :, None] * S + jnp.einsum('bhsd,bhsv->bhdv', ko, vi)
        return S, o

    S, o = lax.scan(step, s0, (q_in, k_out, v, o_intra, G_last))
    return o.transpose(1, 0, 3, 2, 4).reshape(Bn, L, H, -1), S


def _delta_chunked(q, k, v, log_a, beta, s0):
    Bn, L, H, _ = q.shape
    C = DELTA_CHUNK
    n = L // C
    ch = lambda t: t.reshape(Bn, n, C, H, t.shape[-1]).transpose(1, 0, 3, 2, 4)
    chs = lambda t: t.reshape(Bn, n, C, H).transpose(1, 0, 3, 2)
    q, k, v = ch(q), ch(k), ch(v)
    log_a, beta = chs(log_a), chs(beta)
    g = jnp.cumsum(log_a, axis=-1)
    kb = k * beta[..., None]
    vb = v * beta[..., None]
    incl = jnp.tril(jnp.ones((C, C), bool))
    strict = jnp.tril(jnp.ones((C, C), bool), -1)
    decay = jnp.exp(jnp.where(incl, g[..., :, None] - g[..., None, :], -jnp.inf))
    Lm = jnp.where(strict, jnp.einsum('nbhid,nbhjd->nbhij', kb, k) * decay, 0.0)
    eye = jnp.eye(C, dtype=F32)
    T = lax.linalg.triangular_solve(eye + Lm, jnp.broadcast_to(eye, Lm.shape),
                                    left_side=True, lower=True, unit_diagonal=True)
    u = T @ vb
    w = T @ (kb * jnp.exp(g)[..., None])
    attn = jnp.einsum('nbhid,nbhjd->nbhij', q, k) * decay
    q_g = q * jnp.exp(g)[..., None]
    g_last = g[..., -1]
    k_end = k * jnp.exp(g_last[..., None] - g)[..., None]

    def step(S, xs):
        u_c, w_c, qg_c, at_c, ke_c, gl_c = xs
        v_new = u_c - w_c @ S
        o = qg_c @ S + at_c @ v_new
        S = jnp.exp(gl_c)[..., None, None] * S + jnp.einsum('bhcd,bhcv->bhdv', ke_c, v_new)
        return S, o

    S, o = lax.scan(step, s0, (u, w, q_g, attn, k_end, g_last))
    return o.transpose(1, 0, 3, 2, 4).reshape(Bn, L, H, -1), S


def _hgrn2(q, zf_f, zf_b, i, g, lb, onorm_g, s0):
    Bn, L, _ = q.shape
    hd = lambda t: t.reshape(Bn, L, C_HEADS, HEAD_DIM)
    lb = lb.reshape(C_HEADS, HEAD_DIM)

    def gates(zf):
        zf = hd(zf)
        f = lb + (1.0 - lb) * jax.nn.sigmoid(zf)
        return jnp.log(f), (1.0 - lb) * jax.nn.sigmoid(-zf)

    q, i = hd(q), hd(i)
    logf_f, k_f = gates(zf_f)
    logf_b, k_b = gates(zf_b)
    o_f, s_f = _gla_chunked(q, k_f, i, logf_f, s0[:, 0])
    o_b, s_b = _gla_chunked(_flip(q), _flip(k_b), _flip(i), _flip(logf_b), s0[:, 1])
    o = _rms_heads(o_f + _flip(o_b), onorm_g).reshape(Bn, L, BRANCH_W)
    return o * jax.nn.silu(g), jnp.stack([s_f, s_b], axis=1)


def _conv(x, w):
    pad = (CONV_K - 1) // 2
    return lax.conv_general_dilated(x, w[:, None, :], window_strides=(1,),
                                    padding=[(pad, pad)], dimension_numbers=('NWC', 'WIO', 'NWC'),
                                    feature_group_count=x.shape[-1])


def _gated_delta(qkv, za, zb, g, conv_w, a_log, dt_bias, onorm_g, s0):
    Bn, L, _ = qkv.shape
    qkv = jax.nn.silu(_conv(qkv, conv_w))
    q, k, v = [t.reshape(Bn, L, D_HEADS, HEAD_DIM) for t in jnp.split(qkv, 3, axis=-1)]
    q = _l2n(q) * HEAD_DIM ** -0.5
    k = _l2n(k)
    za = za.reshape(Bn, L, 2, D_HEADS)
    zb = zb.reshape(Bn, L, 2, D_HEADS)
    log_alpha = -jnp.exp(a_log) * jax.nn.softplus(za + dt_bias)
    beta = jax.nn.sigmoid(zb)
    o_f, s_f = _delta_chunked(q, k, v, log_alpha[:, :, 0], beta[:, :, 0], s0[:, 0])
    o_b, s_b = _delta_chunked(_flip(q), _flip(k), _flip(v), _flip(log_alpha[:, :, 1]),
                              _flip(beta[:, :, 1]), s0[:, 1])
    o = _rms_heads(o_f + _flip(o_b), onorm_g).reshape(Bn, L, BRANCH_W)
    return o * jax.nn.silu(g), jnp.stack([s_f, s_b], axis=1)


def _mixers(z, lw, ctx):
    Bn, L, _ = z.shape
    aq, ak, av = z[..., 0:256], z[..., 256:384], z[..., 384:512]
    bq, bk, bv = z[..., 512:768], z[..., 768:896], z[..., 896:1024]
    cq, cf_f, cf_b = z[..., 1024:1280], z[..., 1280:1536], z[..., 1536:1792]
    ci, cg = z[..., 1792:2048], z[..., 2048:2304]
    dqkv = z[..., 2304:3072]
    da, db = z[..., Z_DAB:Z_DAB + 8], z[..., Z_DAB + 8:Z_DAB + 16]
    dg = z[..., Z_DG:Z_DG + 256]
    heads = lambda t: t.reshape(Bn, L, -1, HEAD_DIM)
    aq, ak, av, bv = heads(aq), heads(ak), heads(av), heads(bv)
    bq = _rms_heads(heads(bq), lw['b_qnorm_g'])
    bk = _rms_heads(heads(bk), lw['b_knorm_g'])
    if ctx is None:
        s_c0 = jnp.zeros((Bn, 2, C_HEADS, HEAD_DIM, HEAD_DIM), F32)
        s_d0 = jnp.zeros((Bn, 2, D_HEADS, HEAD_DIM, HEAD_DIM), F32)
        o_a = _dense_gqa(aq, ak, av, lw['a_sink'])
        o_b = _dense_gqa(bq, bk, bv, None)
    else:
        ka, va, kbc, vbc, s_c0, s_d0 = ctx
        tabs = _rope_tables(L)
        o_a = _windowed_gqa(_rope(aq, tabs), _rope(ak, tabs), av, ka, va, lw['a_sink'])
        o_b = _dense_gqa(_rope(bq, tabs), jnp.concatenate([_rope(bk, tabs), kbc], axis=1),
                         jnp.concatenate([bv, vbc], axis=1), None)
    o_c, s_c = _hgrn2(cq, cf_f, cf_b, ci, cg, lw['lb'], lw['c_onorm_g'], s_c0)
    o_d, s_d = _gated_delta(dqkv, da, db, dg, lw['d_conv'], lw['d_a_log'], lw['d_dt_bias'],
                            lw['d_onorm_g'], s_d0)
    o = jnp.concatenate([o_a, o_b, o_c, o_d], axis=-1)
    return o, (ak, av, bk, bv, s_c, s_d)


def kernel(x_prompt, x_sample, cache_attn_a_k, cache_attn_a_v, cache_attn_b_k, cache_attn_b_v,
           state_hgrn, state_delta, c, c_ctx, norm1_g, norm2_g, w_ada, b_ada, w_in, a_sink,
           b_qnorm_g, b_knorm_g, c_lb, c_onorm_g, d_conv, d_a_log, d_dt_bias, d_onorm_g,
           w_branch, w_out, ffn_w1, ffn_w3, ffn_w2, router_w, router_b, moe_w1, moe_w3, moe_w2,
           final_norm_g):
    cum = jnp.cumsum(jax.nn.softmax(c_lb, axis=0), axis=0)
    lower_bounds = cum - cum[:1]

    x = jnp.concatenate([x_prompt.reshape(CTX_TOK, D_MODEL), x_sample.reshape(LAT_TOK, D_MODEL)], axis=0)
    cond = jnp.concatenate([c_ctx[None, :], c, jnp.zeros((16 - N_COND, D_MODEL), F32)], axis=0)

    rope_c, rope_s = rope_lane_tables(DEC_SEQ)
    caches = []
    for l in range(DEPTH):
        lw = {'a_sink': a_sink[l], 'b_qnorm_g': b_qnorm_g[l], 'b_knorm_g': b_knorm_g[l],
              'lb': lower_bounds[l], 'c_onorm_g': c_onorm_g[l], 'd_conv': d_conv[l], 'd_a_log': d_a_log[l],
              'd_dt_bias': d_dt_bias[l], 'd_onorm_g': d_onorm_g[l]}
        mod = ada_modulation(cond, w_ada[l], b_ada[l])[:N_COND].reshape(N_COND, 6, D_MODEL)
        w_mix = jnp.concatenate([w_in[l][:, :Z_MAIN], w_in[l][:, Z_MAIN + 16:W_IN_MIX],
                                 w_in[l][:, Z_MAIN:Z_MAIN + 16], jnp.zeros((D_MODEL, 128 - 16), F32)],
                                axis=1).astype(BF16)
        w_gl = w_in[l][:, W_IN_MIX:].reshape(D_MODEL, N_BRANCH, D_MODEL).transpose(1, 0, 2).astype(BF16)
        z = input_projection(x, mod, norm1_g[l], w_mix)
        kv2 = lambda t: t.reshape(DEC_BATCH, PAST_LEN, 128)
        o_ab_ctx, bk_ctx = ctx_attention(z, a_sink[l], b_qnorm_g[l], b_knorm_g[l])
        o_a_lat = latent_attention_a(z, CTX_TOK, DEC_BATCH, DEC_SEQ, a_sink[l], kv2(cache_attn_a_k[:, l]),
                                     kv2(cache_attn_a_v[:, l]), rope_c, rope_s)
        o_b_lat = latent_attention_b(z, CTX_TOK, DEC_BATCH, DEC_SEQ, kv2(cache_attn_b_k[:, l]),
                                     kv2(cache_attn_b_v[:, l]), rope_c, rope_s, b_qnorm_g[l], b_knorm_g[l])
        o_c_ctx, sc_t = hgrn_mixer(z, 0, BATCH, SEQ, lower_bounds[l], c_onorm_g[l],
                                   jnp.zeros((BATCH, 2, C_HEADS, HEAD_DIM, HEAD_DIM), F32))
        o_c_lat, _ = hgrn_mixer(z, CTX_TOK, DEC_BATCH, DEC_SEQ, lower_bounds[l], c_onorm_g[l],
                                jnp.swapaxes(state_hgrn[:, l], -1, -2))

        qkv, gates = delta_prep(z, d_conv[l], d_a_log[l], d_dt_bias[l])
        u2, wq, ak = delta_chunks(qkv, gates)
        of_ctx, ob_ctx, sd = delta_scan(u2, wq, ak, 0, BATCH, SEQ,
                                        jnp.zeros((BATCH, 2, D_HEADS, HEAD_DIM, HEAD_DIM), F32))
        of_lat, ob_lat, _ = delta_scan(u2, wq, ak, CTX_TOK, DEC_BATCH, DEC_SEQ, state_delta[:, l])
        o_d_ctx = delta_output(of_ctx, ob_ctx, z, 0, d_onorm_g[l])
        o_d_lat = delta_output(of_lat, ob_lat, z, CTX_TOK, d_onorm_g[l])
        kvh = lambda t: t.reshape(BATCH, SEQ, 2, HEAD_DIM)
        caches.append((kvh(z[:CTX_TOK, 256:384]), kvh(z[:CTX_TOK, 384:512]), kvh(bk_ctx), kvh(z[:CTX_TOK, 896:1024]),
                       jnp.swapaxes(sc_t, -1, -2), sd))
        o = jnp.concatenate([jnp.concatenate([o_ab_ctx, o_c_ctx, o_d_ctx], axis=1),
                             jnp.concatenate([o_a_lat, o_b_lat, o_c_lat, o_d_lat], axis=1)], axis=0)
        x = merge_projection(x, mod, norm1_g[l], o, w_gl, w_branch[l].astype(BF16), w_out[l].astype(BF16))
        j = l // 2
        if l % 2 == 0:
            x = dense_ffn(x, mod, norm2_g[l], ffn_w1[j].astype(BF16), ffn_w3[j].astype(BF16),
                          ffn_w2[j].astype(BF16))
        else:
            assert l == DEPTH - 1, "the expert layer's residual is fused with the final norm"
            rw = jnp.concatenate([router_w[j], jnp.zeros((D_MODEL, 128 - N_EXPERTS), F32)], axis=1)
            rb = jnp.concatenate([router_b[j], jnp.zeros((128 - N_EXPERTS,), F32)])[None, :]
            h2, pos, wgt = moe_router(x, mod, norm2_g[l], rw, rb)
            f = moe_experts(h2, pos, wgt, moe_w1[j].astype(BF16), moe_w3[j].astype(BF16), moe_w2[j].astype(BF16))
            y = residual_final_norm(x, f, mod, final_norm_g)

    y_prompt = y[:CTX_TOK].reshape(BATCH, SEQ, D_MODEL)
    y_sample = y[CTX_TOK:].reshape(DEC_BATCH, DEC_SEQ, D_MODEL)
    stack = lambda idx: jnp.stack([caches[l][idx] for l in range(DEPTH)], axis=1)
    return (y_prompt, y_sample, stack(0), stack(1), stack(2), stack(3), stack(4), stack(5))
```

```python
import functools

import jax
import jax.numpy as jnp
import numpy as np
from jax import lax
from jax.experimental import pallas as pl
from jax.experimental.pallas import tpu as pltpu

F32 = jnp.float32
BF16 = jnp.bfloat16

D_MODEL = 1024
BATCH = 32
SEQ = 256
DEPTH = 2
DEC_BATCH = 8
DEC_SEQ = 4096
PAST_LEN = 256
GRID_W = 64
HEAD_DIM = 64
A_HEADS = 4
A_KV_HEADS = 2
B_HEADS = 4
B_KV_HEADS = 2
C_HEADS = 4
D_HEADS = 4
BRANCH_W = 256
N_BRANCH = 4
WINDOW = 128
BLOCK = 128
ROPE_BASE = 10000.0
HGRN_CHUNK = 32
DELTA_CHUNK = 64
CONV_K = 5
D_FF = 2816
N_EXPERTS = 8
D_FF_EXPERT = 3584
EPS = 1e-6
NEG_INF = -1e30
F32_MIN = float(np.finfo(np.float32).min)

CTX_TOK = BATCH * SEQ
LAT_TOK = DEC_BATCH * DEC_SEQ
N_TOK = CTX_TOK + LAT_TOK
N_COND = 1 + DEC_BATCH

Z_MAIN = 3072
Z_DG = Z_MAIN
Z_DAB = Z_DG + BRANCH_W
Z_COLS = Z_DAB + 128
W_IN_MIX = 3344

TM = 512
VMEM_LIMIT = 56 * 1024 * 1024


def _tile_cond(i, tm):
    ctx_tiles = CTX_TOK // tm
    per_b = DEC_SEQ // tm
    return jnp.where(i < ctx_tiles, 0, 1 + (i - ctx_tiles) // per_b)


def _rms(x, g):
    return x * lax.rsqrt(jnp.mean(x * x, axis=-1, keepdims=True) + EPS) * g


def _params(sem):
    return pltpu.CompilerParams(dimension_semantics=sem, vmem_limit_bytes=VMEM_LIMIT)


def _ada_kernel(c_ref, w_ref, b_ref, o_ref):
    c = c_ref[...]
    s = c * jax.nn.sigmoid(c)
    o_ref[...] = jnp.dot(s.astype(BF16), w_ref[...].astype(BF16), preferred_element_type=F32) + b_ref[...]


def ada_modulation(cond_pad, w, b):
    n = 6 * D_MODEL
    tn = 1536
    return pl.pallas_call(
        _ada_kernel,
        grid=(n // tn,),
        in_specs=[pl.BlockSpec((16, D_MODEL), lambda j: (0, 0)),
                  pl.BlockSpec((D_MODEL, tn), lambda j: (0, j)),
                  pl.BlockSpec((1, tn), lambda j: (0, j))],
        out_specs=pl.BlockSpec((16, tn), lambda j: (0, j)),
        out_shape=jax.ShapeDtypeStruct((16, n), F32),
        compiler_params=_params(("arbitrary",)),
        name="ada_modulation",
    )(cond_pad, w, b.reshape(1, n))


def _in_kernel(x_ref, mod_ref, g_ref, w_ref, z_ref):
    h = _rms(x_ref[...], g_ref[...]) * (1.0 + mod_ref[1:2, :]) + mod_ref[0:1, :]
    z_ref[...] = jnp.dot(h.astype(BF16), w_ref[...], preferred_element_type=F32)


def input_projection(x, mod, g, w):
    nt = N_TOK // TM
    return pl.pallas_call(
        _in_kernel,
        grid=(nt,),
        in_specs=[pl.BlockSpec((TM, D_MODEL), lambda i: (i, 0)),
                  pl.BlockSpec((None, 6, D_MODEL), lambda i: (_tile_cond(i, TM), 0, 0)),
                  pl.BlockSpec((1, D_MODEL), lambda i: (0, 0)),
                  pl.BlockSpec((D_MODEL, Z_COLS), lambda i: (0, 0))],
        out_specs=pl.BlockSpec((TM, Z_COLS), lambda i: (i, 0)),
        out_shape=jax.ShapeDtypeStruct((N_TOK, Z_COLS), F32),
        compiler_params=_params(("arbitrary",)),
        name="input_projection",
    )(x, mod, g.reshape(1, D_MODEL), w)


def _merge_kernel(x_ref, mod_ref, g_ref, ab_c_ref, c_c_ref, d_c_ref, a_l_ref, b_l_ref, c_l_ref, d_l_ref,
                  wgl_ref, wbr_ref, wout_ref, xo_ref):
    x = x_ref[...]
    h = (_rms(x, g_ref[...]) * (1.0 + mod_ref[1:2, :]) + mod_ref[0:1, :]).astype(BF16)
    is_ctx = pl.program_id(0) < CTX_TOK // TM
    branches = (jnp.where(is_ctx, ab_c_ref[:, :BRANCH_W], a_l_ref[...]),
                jnp.where(is_ctx, ab_c_ref[:, BRANCH_W:], b_l_ref[...]),
                jnp.where(is_ctx, c_c_ref[...], c_l_ref[...]),
                jnp.where(is_ctx, d_c_ref[...], d_l_ref[...]))
    merged = None
    for j in range(N_BRANCH):
        gate = jax.nn.sigmoid(jnp.dot(h, wgl_ref[j], preferred_element_type=F32))
        br = jnp.dot(branches[j], wbr_ref[j], preferred_element_type=F32)
        merged = gate * br if merged is None else merged + gate * br
    mix = jnp.dot(merged.astype(BF16), wout_ref[...], preferred_element_type=F32)
    xo_ref[...] = x + mod_ref[2:3, :] * mix


def merge_projection(x, mod, g, o_ctx, o_lat, wgl, wbr, wout):
    nt = N_TOK // TM
    ctx_tiles = CTX_TOK // TM
    ctx_spec = lambda w: pl.BlockSpec((TM, w), lambda i: (jnp.minimum(i, ctx_tiles - 1), 0))
    lat_spec = pl.BlockSpec((TM, BRANCH_W), lambda i: (jnp.maximum(i - ctx_tiles, 0), 0))
    return pl.pallas_call(
        _merge_kernel,
        grid=(nt,),
        in_specs=[pl.BlockSpec((TM, D_MODEL), lambda i: (i, 0)),
                  pl.BlockSpec((None, 6, D_MODEL), lambda i: (_tile_cond(i, TM), 0, 0)),
                  pl.BlockSpec((1, D_MODEL), lambda i: (0, 0)),
                  ctx_spec(2 * BRANCH_W), ctx_spec(BRANCH_W), ctx_spec(BRANCH_W),
                  lat_spec, lat_spec, lat_spec, lat_spec,
                  pl.BlockSpec((N_BRANCH, D_MODEL, D_MODEL), lambda i: (0, 0, 0)),
                  pl.BlockSpec((N_BRANCH, BRANCH_W, D_MODEL), lambda i: (0, 0, 0)),
                  pl.BlockSpec((D_MODEL, D_MODEL), lambda i: (0, 0))],
        out_specs=pl.BlockSpec((TM, D_MODEL), lambda i: (i, 0)),
        out_shape=jax.ShapeDtypeStruct((N_TOK, D_MODEL), F32),
        compiler_params=_params(("arbitrary",)),
        name="merge_projection",
    )(x, mod, g.reshape(1, D_MODEL), *o_ctx, *o_lat, wgl, wbr, wout)


def _ffn_kernel(x_ref, mod_ref, g_ref, w1_ref, w3_ref, w2_ref, xo_ref):
    x = x_ref[...]
    h = (_rms(x, g_ref[...]) * (1.0 + mod_ref[4:5, :]) + mod_ref[3:4, :]).astype(BF16)
    a = jnp.dot(h, w1_ref[...], preferred_element_type=F32)
    b = jnp.dot(h, w3_ref[...], preferred_element_type=F32)
    hid = (a * jax.nn.sigmoid(a) * b).astype(BF16)
    f = jnp.dot(hid, w2_ref[...], preferred_element_type=F32)
    xo_ref[...] = x + mod_ref[5:6, :] * f


def dense_ffn(x, mod, g, w1, w3, w2):
    nt = N_TOK // TM
    const = lambda i: (0, 0)
    return pl.pallas_call(
        _ffn_kernel,
        grid=(nt,),
        in_specs=[pl.BlockSpec((TM, D_MODEL), lambda i: (i, 0)),
                  pl.BlockSpec((None, 6, D_MODEL), lambda i: (_tile_cond(i, TM), 0, 0)),
                  pl.BlockSpec((1, D_MODEL), const),
                  pl.BlockSpec((D_MODEL, D_FF), const, pipeline_mode=pl.Buffered(1)),
                  pl.BlockSpec((D_MODEL, D_FF), const, pipeline_mode=pl.Buffered(1)),
                  pl.BlockSpec((D_FF, D_MODEL), const, pipeline_mode=pl.Buffered(1))],
        out_specs=pl.BlockSpec((TM, D_MODEL), lambda i: (i, 0)),
        out_shape=jax.ShapeDtypeStruct((N_TOK, D_MODEL), F32),
        compiler_params=_params(("arbitrary",)),
        name="dense_ffn",
    )(x, mod, g.reshape(1, D_MODEL), w1, w3, w2)


MOE_T = 1024
MOE_R = 128
MOE_F = 1792


def _router_kernel(x_ref, mod_ref, g_ref, rw_ref, rb_ref, h_ref, pos_ref, wgt_ref):
    t = x_ref.shape[0]
    h = _rms(x_ref[...], g_ref[...]) * (1.0 + mod_ref[4:5, :]) + mod_ref[3:4, :]
    h_ref[...] = h.astype(BF16)
    logits = jnp.dot(h, rw_ref[...], preferred_element_type=F32, precision=lax.Precision.HIGHEST) + rb_ref[...]
    lt = logits.T[:N_EXPERTS, :]
    eidx = lax.broadcasted_iota(jnp.int32, lt.shape, 0)
    m1 = jnp.max(lt, axis=0, keepdims=True)
    i1 = jnp.min(jnp.where(lt == m1, eidx, N_EXPERTS), axis=0, keepdims=True)
    rest = jnp.where(eidx == i1, F32_MIN, lt)
    m2 = jnp.max(rest, axis=0, keepdims=True)
    i2 = jnp.min(jnp.where(rest == m2, eidx, N_EXPERTS), axis=0, keepdims=True)
    e2 = jnp.exp(m2 - m1)
    p1 = 1.0 / (1.0 + e2)
    p2 = e2 / (1.0 + e2)
    wgt_ref[...] = jnp.where(eidx == i1, p1, 0.0) + jnp.where(eidx == i2, p2, 0.0)
    routed = jnp.where(eidx == i1, 1.0, jnp.where(eidx == i2, 1.0, 0.0))
    s_id = lax.broadcasted_iota(jnp.int32, (t, t), 0)
    t_id = lax.broadcasted_iota(jnp.int32, (t, t), 1)
    before = jnp.where(s_id < t_id, 1.0, 0.0).astype(BF16)
    rank = jnp.dot(routed.astype(BF16), before, preferred_element_type=F32)
    pos_ref[...] = jnp.where(routed > 0.0, rank.astype(jnp.int32), -1)


def moe_router(x, mod, g, rw, rb):
    nt = N_TOK // MOE_T
    return pl.pallas_call(
        _router_kernel,
        grid=(nt,),
        in_specs=[pl.BlockSpec((MOE_T, D_MODEL), lambda i: (i, 0)),
                  pl.BlockSpec((None, 6, D_MODEL), lambda i: (_tile_cond(i, MOE_T), 0, 0)),
                  pl.BlockSpec((1, D_MODEL), lambda i: (0, 0)),
                  pl.BlockSpec((D_MODEL, 128), lambda i: (0, 0)),
                  pl.BlockSpec((1, 128), lambda i: (0, 0))],
        out_specs=[pl.BlockSpec((MOE_T, D_MODEL), lambda i: (i, 0)),
                   pl.BlockSpec((N_EXPERTS, MOE_T), lambda i: (0, i)),
                   pl.BlockSpec((N_EXPERTS, MOE_T), lambda i: (0, i))],
        out_shape=[jax.ShapeDtypeStruct((N_TOK, D_MODEL), BF16),
                   jax.ShapeDtypeStruct((N_EXPERTS, N_TOK), jnp.int32),
                   jax.ShapeDtypeStruct((N_EXPERTS, N_TOK), F32)],
        compiler_params=_params(("arbitrary",)),
        name="moe_router",
    )(x, mod, g.reshape(1, D_MODEL), rw, rb)


def _moe_sparse_kernel(h_ref, pos_ref, wgt_ref, w1_ref, w3_ref, w2_ref, y_ref, xg_scr, acc_scr, wr_scr):
    e = pl.program_id(1)
    f = pl.program_id(2)
    t = h_ref.shape[0]
    pos_e = pos_ref[pl.ds(e, 1), :]
    n_blocks = (jnp.max(pos_e) + MOE_R) // MOE_R
    row_id = lax.broadcasted_iota(jnp.int32, (MOE_R, t), 0)

    def block_rows(r):
        return pl.ds(pl.multiple_of(r * MOE_R, MOE_R), MOE_R)

    def selects(r):
        return pos_e == row_id + r * MOE_R

    @pl.when((e == 0) & (f == 0))
    def _():
        y_ref[...] = jnp.zeros_like(y_ref)

    @pl.when(f == 0)
    def _():
        wgt_e = wgt_ref[pl.ds(e, 1), :]

        def gather(r, carry):
            sel = selects(r)
            xg = jnp.dot(jnp.where(sel, 1.0, 0.0).astype(BF16), h_ref[...], preferred_element_type=F32)
            xg_scr[block_rows(r), :] = xg.astype(BF16)
            w_rows = jnp.sum(jnp.where(sel, wgt_e, 0.0), axis=1, keepdims=True)
            wr_scr[block_rows(r), :] = jnp.broadcast_to(w_rows, (MOE_R, 128))
            acc_scr[block_rows(r), :] = jnp.zeros((MOE_R, D_MODEL), F32)
            return carry

        lax.fori_loop(0, n_blocks, gather, 0)

    def expert(r, carry):
        xg = xg_scr[block_rows(r), :]
        a = jnp.dot(xg, w1_ref[...], preferred_element_type=F32)
        b = jnp.dot(xg, w3_ref[...], preferred_element_type=F32)
        hid = (a * jax.nn.sigmoid(a) * b * wr_scr[block_rows(r), 0:1]).astype(BF16)
        acc_scr[block_rows(r), :] += jnp.dot(hid, w2_ref[...], preferred_element_type=F32)
        return carry

    lax.fori_loop(0, n_blocks, expert, 0)

    @pl.when(f == pl.num_programs(2) - 1)
    def _():
        def scatter(r, carry):
            onehot = jnp.where(selects(r), 1.0, 0.0).astype(BF16)
            hi, lo = _split_bf16(acc_scr[block_rows(r), :])
            y_ref[...] += (lax.dot_general(onehot, hi, _TN, preferred_element_type=F32)
                           + lax.dot_general(onehot, lo, _TN, preferred_element_type=F32))
            return carry

        lax.fori_loop(0, n_blocks, scatter, 0)


def moe_experts(h2, pos, wgt, w1, w3, w2):
    nt = N_TOK // MOE_T
    nf = D_FF_EXPERT // MOE_F
    return pl.pallas_call(
        _moe_sparse_kernel,
        grid=(nt, N_EXPERTS, nf),
        in_specs=[pl.BlockSpec((MOE_T, D_MODEL), lambda i, e, f: (i, 0)),
                  pl.BlockSpec((N_EXPERTS, MOE_T), lambda i, e, f: (0, i)),
                  pl.BlockSpec((N_EXPERTS, MOE_T), lambda i, e, f: (0, i)),
                  pl.BlockSpec((None, D_MODEL, MOE_F), lambda i, e, f: (e, 0, f)),
                  pl.BlockSpec((None, D_MODEL, MOE_F), lambda i, e, f: (e, 0, f)),
                  pl.BlockSpec((None, MOE_F, D_MODEL), lambda i, e, f: (e, f, 0))],
        out_specs=pl.BlockSpec((MOE_T, D_MODEL), lambda i, e, f: (i, 0)),
        out_shape=jax.ShapeDtypeStruct((N_TOK, D_MODEL), F32),
        scratch_shapes=[pltpu.VMEM((MOE_T, D_MODEL), BF16),
                        pltpu.VMEM((MOE_T, D_MODEL), F32),
                        pltpu.VMEM((MOE_T, 128), F32)],
        compiler_params=_params(("arbitrary", "arbitrary", "arbitrary")),
        name="moe_experts",
    )(h2, pos, wgt, w1, w3, w2)


def _residual_norm_kernel(x_ref, y_ref, mod_ref, g_ref, o_ref):
    o_ref[...] = _rms(x_ref[...] + mod_ref[5:6, :] * y_ref[...], g_ref[...])


def residual_final_norm(x, y, mod, g, row0, n_rows):
    tm = 1024
    t0 = row0 // tm
    return pl.pallas_call(
        _residual_norm_kernel,
        grid=(n_rows // tm,),
        in_specs=[pl.BlockSpec((tm, D_MODEL), lambda i: (t0 + i, 0)),
                  pl.BlockSpec((tm, D_MODEL), lambda i: (t0 + i, 0)),
                  pl.BlockSpec((None, 6, D_MODEL), lambda i: (_tile_cond(t0 + i, tm), 0, 0)),
                  pl.BlockSpec((1, D_MODEL), lambda i: (0, 0))],
        out_specs=pl.BlockSpec((tm, D_MODEL), lambda i: (i, 0)),
        out_shape=jax.ShapeDtypeStruct((n_rows, D_MODEL), F32),
        compiler_params=_params(("arbitrary",)),
        name="residual_final_norm",
    )(x, y, mod, g.reshape(1, D_MODEL))


ATT_SCALE = HEAD_DIM ** -0.5
_NT = (((1,), (1,)), ((), ()))


def _head_rms(x, g_row):
    outs = []
    for h in range(x.shape[1] // HEAD_DIM):
        xh = x[:, h * HEAD_DIM:(h + 1) * HEAD_DIM]
        outs.append(xh * lax.rsqrt(jnp.mean(xh * xh, axis=-1, keepdims=True) + EPS) * g_row)
    return jnp.concatenate(outs, axis=-1)


def _rope_apply(x, c, s):
    w = x.shape[-1]
    lane = lax.broadcasted_iota(jnp.int32, x.shape, 1)
    first_half = ((lane // (HEAD_DIM // 4)) % 2) == 0
    partner = jnp.where(first_half, pltpu.roll(x, w - HEAD_DIM // 4, 1), pltpu.roll(x, HEAD_DIM // 4, 1))
    return x * c + partner * s


def rope_lane_tables(L):
    rows = L // GRID_W
    row = jnp.repeat(jnp.arange(rows, dtype=F32), GRID_W)
    col = jnp.tile(jnp.arange(GRID_W, dtype=F32), rows)
    n_freq = HEAD_DIM // 4
    inv = ROPE_BASE ** (-jnp.arange(n_freq, dtype=F32) / n_freq)
    ang = jnp.stack([row, col], 0)[:, :, None] * inv
    cos, sin = jnp.cos(ang), jnp.sin(ang)
    c = jnp.concatenate([cos[0], cos[0], cos[1], cos[1]], axis=-1)
    s = jnp.concatenate([-sin[0], sin[0], -sin[1], sin[1]], axis=-1)
    return jnp.tile(c, (1, 4)), jnp.tile(s, (1, 4))


def _with_ones(v):
    ones = jnp.ones((v.shape[0], HEAD_DIM), BF16)
    parts = []
    for h in range(v.shape[1] // HEAD_DIM):
        parts += [v[:, h * HEAD_DIM:(h + 1) * HEAD_DIM].astype(BF16), ones]
    return jnp.concatenate(parts, axis=-1)


def _attend_heads(jobs):
    for job in jobs:
        q = (job['q'] * ATT_SCALE).astype(BF16)
        job['s'] = lax.dot_general(q, job['k'], _NT, preferred_element_type=F32)
        if job.get('extra') is not None:
            job['s2'] = lax.dot_general(q, job['extra'][0], _NT, preferred_element_type=F32)
    outs = []
    for job in jobs:
        s, sink = job['s'], job.get('sink')
        if job.get('mask') is not None:
            s = jnp.where(job['mask'], s, NEG_INF)
        m = jnp.max(s, axis=-1, keepdims=True)
        if 's2' in job:
            m = jnp.maximum(m, jnp.max(job['s2'], axis=-1, keepdims=True))
        if sink is not None:
            m = jnp.maximum(m, sink)
        o = jnp.dot(jnp.exp(s - m).astype(BF16), job['v'], preferred_element_type=F32)
        if 's2' in job:
            o = o + jnp.dot(jnp.exp(job['s2'] - m).astype(BF16), job['extra'][1], preferred_element_type=F32)
        den = o[:, HEAD_DIM:HEAD_DIM + 1]
        if sink is not None:
            den = den + jnp.exp(sink - m)
        outs.append(o[:, :HEAD_DIM] / den)
    return outs


def _ctx_attn_kernel(sink_ref, z_ref, gq_ref, gk_ref, o_ref, bk_ref):
    z = z_ref[...]
    bq = _head_rms(z[:, 512:768], gq_ref[...])
    bk = _head_rms(z[:, 768:896], gk_ref[...])
    bk_ref[...] = bk
    groups = ((z[:, 0:256], z[:, 256:384], z[:, 384:512], True),
              (bq, bk, z[:, 896:1024], False))
    outs = []
    for q_all, k_all, v_all, use_sink in groups:
        k_all = k_all.astype(BF16)
        v_all = _with_ones(v_all)
        for hq in range(A_HEADS):
            kv = hq // (A_HEADS // A_KV_HEADS)
            outs.append(dict(q=q_all[:, hq * HEAD_DIM:(hq + 1) * HEAD_DIM],
                             k=k_all[:, kv * HEAD_DIM:(kv + 1) * HEAD_DIM],
                             v=v_all[:, kv * 2 * HEAD_DIM:(kv + 1) * 2 * HEAD_DIM],
                             sink=sink_ref[hq] if use_sink else None))
    o_ref[...] = jnp.concatenate(_attend_heads(outs), axis=-1).astype(o_ref.dtype)


def ctx_attention(z, sink, gq, gk):
    return pl.pallas_call(
        _ctx_attn_kernel,
        grid=(BATCH,),
        in_specs=[pl.BlockSpec(memory_space=pltpu.SMEM),
                  pl.BlockSpec((SEQ, 1024), lambda b: (b, 0)),
                  pl.BlockSpec((1, HEAD_DIM), lambda b: (0, 0)),
                  pl.BlockSpec((1, HEAD_DIM), lambda b: (0, 0))],
        out_specs=[pl.BlockSpec((SEQ, 512), lambda b: (b, 0)),
                   pl.BlockSpec((SEQ, 128), lambda b: (b, 0))],
        out_shape=[jax.ShapeDtypeStruct((CTX_TOK, 512), BF16),
                   jax.ShapeDtypeStruct((CTX_TOK, 128), F32)],
        compiler_params=_params(("arbitrary",)),
        name="ctx_attention",
    )(sink, z, gq.reshape(1, HEAD_DIM), gk.reshape(1, HEAD_DIM))


LB_TQ = 256


def _lat_b_kernel(zq_ref, zkv_ref, ck_ref, cv_ref, cq_ref, sq_ref, ckk_ref, skk_ref, gq_ref, gk_ref,
                  o_ref, k_scr, v_scr):
    L = zkv_ref.shape[0]

    @pl.when(pl.program_id(1) == 0)
    def _():
        kv = zkv_ref[...]
        bk = _rope_apply(_head_rms(kv[:, :128], gk_ref[...]), ckk_ref[...], skk_ref[...])
        k_scr[0:L, :] = bk.astype(BF16)
        k_scr[L:L + PAST_LEN, :] = ck_ref[...].astype(BF16)
        v_scr[0:L, :] = _with_ones(kv[:, 128:])
        v_scr[L:L + PAST_LEN, :] = _with_ones(cv_ref[...])

    q = _rope_apply(_head_rms(zq_ref[...], gq_ref[...]), cq_ref[...], sq_ref[...])
    outs = []
    for hq in range(B_HEADS):
        kv = hq // (B_HEADS // B_KV_HEADS)
        sl = slice(kv * HEAD_DIM, (kv + 1) * HEAD_DIM)
        outs.append(dict(q=q[:, hq * HEAD_DIM:(hq + 1) * HEAD_DIM], k=k_scr[:, sl],
                         v=v_scr[:, kv * 2 * HEAD_DIM:(kv + 1) * 2 * HEAD_DIM]))
    o_ref[...] = jnp.concatenate(_attend_heads(outs), axis=-1).astype(o_ref.dtype)


def latent_attention_b(z, row0, nb, L, cache_k, cache_v, rope_c, rope_s, gq, gk):
    nq = L // LB_TQ
    return pl.pallas_call(
        _lat_b_kernel,
        grid=(nb, nq),
        in_specs=[pl.BlockSpec((LB_TQ, 256), lambda b, i: (row0 // LB_TQ + b * nq + i, 2)),
                  pl.BlockSpec((L, 256), lambda b, i: (row0 // L + b, 3)),
                  pl.BlockSpec((None, PAST_LEN, 128), lambda b, i: (b, 0, 0)),
                  pl.BlockSpec((None, PAST_LEN, 128), lambda b, i: (b, 0, 0)),
                  pl.BlockSpec((LB_TQ, 256), lambda b, i: (i, 0)),
                  pl.BlockSpec((LB_TQ, 256), lambda b, i: (i, 0)),
                  pl.BlockSpec((L, 128), lambda b, i: (0, 0)),
                  pl.BlockSpec((L, 128), lambda b, i: (0, 0)),
                  pl.BlockSpec((1, HEAD_DIM), lambda b, i: (0, 0)),
                  pl.BlockSpec((1, HEAD_DIM), lambda b, i: (0, 0))],
        out_specs=pl.BlockSpec((LB_TQ, 256), lambda b, i: (b * nq + i, 0)),
        out_shape=jax.ShapeDtypeStruct((nb * L, 256), BF16),
        scratch_shapes=[pltpu.VMEM((L + PAST_LEN, 128), BF16),
                        pltpu.VMEM((L + PAST_LEN, 256), BF16)],
        compiler_params=_params(("arbitrary", "arbitrary")),
        name="latent_attention_b",
    )(z, z, cache_k, cache_v, rope_c, rope_s, rope_c, rope_s, gq.reshape(1, HEAD_DIM), gk.reshape(1, HEAD_DIM))


def _lat_a_kernel(sink_ref, zq_ref, zkv_ref, ck_ref, cv_ref, cq_ref, sq_ref, ckk_ref, skk_ref,
                  o_ref, k_scr, v_scr, ck_scr, cv_scr):
    L = zkv_ref.shape[0]
    i = pl.program_id(1)

    @pl.when(i == 0)
    def _():
        kv = zkv_ref[...]
        k_scr[0:BLOCK, :] = jnp.zeros((BLOCK, 128), BF16)
        v_scr[0:BLOCK, :] = jnp.zeros((BLOCK, 256), BF16)
        k_scr[BLOCK:BLOCK + L, :] = _rope_apply(kv[:, :128], ckk_ref[...], skk_ref[...]).astype(BF16)
        v_scr[BLOCK:BLOCK + L, :] = _with_ones(kv[:, 128:])
        k_scr[BLOCK + L:2 * BLOCK + L, :] = jnp.zeros((BLOCK, 128), BF16)
        v_scr[BLOCK + L:2 * BLOCK + L, :] = jnp.zeros((BLOCK, 256), BF16)
        ck_scr[...] = ck_ref[...].astype(BF16)
        cv_scr[...] = _with_ones(cv_ref[...])

    q = _rope_apply(zq_ref[...], cq_ref[...], sq_ref[...])
    start = pl.multiple_of(i * BLOCK, BLOCK)
    kband = k_scr[pl.ds(start, 3 * BLOCK), :]
    vband = v_scr[pl.ds(start, 3 * BLOCK), :]
    r = lax.broadcasted_iota(jnp.int32, (BLOCK, 3 * BLOCK), 0)
    cidx = lax.broadcasted_iota(jnp.int32, (BLOCK, 3 * BLOCK), 1)
    kpos = i * BLOCK - BLOCK + cidx
    mask = (jnp.abs(cidx - BLOCK - r) <= WINDOW) & (kpos >= 0) & (kpos < L)
    outs = []
    for hq in range(A_HEADS):
        kv = hq // (A_HEADS // A_KV_HEADS)
        sl = slice(kv * HEAD_DIM, (kv + 1) * HEAD_DIM)
        sv = slice(kv * 2 * HEAD_DIM, (kv + 1) * 2 * HEAD_DIM)
        outs.append(dict(q=q[:, hq * HEAD_DIM:(hq + 1) * HEAD_DIM], k=kband[:, sl], v=vband[:, sv],
                         extra=(ck_scr[:, sl], cv_scr[:, sv]), sink=sink_ref[hq], mask=mask))
    o_ref[...] = jnp.concatenate(_attend_heads(outs), axis=-1).astype(o_ref.dtype)


def latent_attention_a(z, row0, nb, L, sink, cache_k, cache_v, rope_c, rope_s):
    nq = L // BLOCK
    return pl.pallas_call(
        _lat_a_kernel,
        grid=(nb, nq),
        in_specs=[pl.BlockSpec(memory_space=pltpu.SMEM),
                  pl.BlockSpec((BLOCK, 256), lambda b, i: (row0 // BLOCK + b * nq + i, 0)),
                  pl.BlockSpec((L, 256), lambda b, i: (row0 // L + b, 1)),
                  pl.BlockSpec((None, PAST_LEN, 128), lambda b, i: (b, 0, 0)),
                  pl.BlockSpec((None, PAST_LEN, 128), lambda b, i: (b, 0, 0)),
                  pl.BlockSpec((BLOCK, 256), lambda b, i: (i, 0)),
                  pl.BlockSpec((BLOCK, 256), lambda b, i: (i, 0)),
                  pl.BlockSpec((L, 128), lambda b, i: (0, 0)),
                  pl.BlockSpec((L, 128), lambda b, i: (0, 0))],
        out_specs=pl.BlockSpec((BLOCK, 256), lambda b, i: (b * nq + i, 0)),
        out_shape=jax.ShapeDtypeStruct((nb * L, 256), BF16),
        scratch_shapes=[pltpu.VMEM((L + 2 * BLOCK, 128), BF16),
                        pltpu.VMEM((L + 2 * BLOCK, 256), BF16),
                        pltpu.VMEM((PAST_LEN, 128), BF16),
                        pltpu.VMEM((PAST_LEN, 256), BF16)],
        compiler_params=_params(("arbitrary", "arbitrary")),
        name="latent_attention_a",
    )(sink, z, z, cache_k, cache_v, rope_c, rope_s, rope_c, rope_s)


_TN = (((0,), (0,)), ((), ()))
HG_GROUP = 4


def _hgrn_kernel(zq_ref, zf_ref, zi_ref, zg_ref, lb_ref, gn_ref, s0_ref, o_ref, sT_ref,
                 of_scr, ob_scr, g_scr, k_scr, qin_scr, kin_scr, v_scr, S_scr, *, tt):
    d = pl.program_id(1)
    j = pl.program_id(2)
    n_t = pl.num_programs(2)
    C = HGRN_CHUNK
    n_c = tt // C

    @pl.when(j == 0)
    def _():
        S_scr[...] = s0_ref[...]

    lb = lb_ref[...]
    sg = jax.nn.sigmoid(zf_ref[...])
    logf = jnp.log(lb + (1.0 - lb) * sg)
    k = (1.0 - lb) * (1.0 - sg)
    k_scr[...] = k
    v_scr[...] = zi_ref[...].astype(BF16)
    in_chunk = lax.broadcasted_iota(jnp.int32, (tt, C_HEADS * HEAD_DIM), 0) % C
    row = lax.broadcasted_iota(jnp.int32, (C, C), 0)
    col = lax.broadcasted_iota(jnp.int32, (C, C), 1)
    heads = [slice(h * HEAD_DIM, (h + 1) * HEAD_DIM) for h in range(C_HEADS)]

    def run(reverse, tile):
        G = logf
        step = 1
        while step < C:
            if reverse:
                G = G + jnp.where(in_chunk < C - step, pltpu.roll(G, tt - step, 0), 0.0)
            else:
                G = G + jnp.where(in_chunk >= step, pltpu.roll(G, step, 0), 0.0)
            step *= 2
        g_scr[...] = G
        qin_scr[...] = (zq_ref[...] * jnp.exp(G)).astype(BF16)
        kin_scr[...] = (k * jnp.exp(-G)).astype(BF16)
        tri = (row <= col) if reverse else (row >= col)

        def body(gi, carry):
            chunks = []
            for g in range(HG_GROUP):
                ci = gi * HG_GROUP + g
                c = (n_c - 1 - ci) if reverse else ci
                r0 = pl.multiple_of(c * C, C)
                rows = pl.ds(r0, C)
                G_c = g_scr[rows, :]
                G_end = G_c[0:1, :] if reverse else G_c[C - 1:C, :]
                vc = v_scr[rows, :]
                q_in = qin_scr[rows, :]
                k_in = kin_scr[rows, :]
                k_out = (k_scr[rows, :] * jnp.exp(G_end - G_c)).astype(BF16)
                chunks.append(dict(
                    r0=r0, rows=rows, vc=vc, q_in=q_in, decay=jnp.exp(G_end),
                    a=[lax.dot_general(q_in[:, sl], k_in[:, sl], _NT, preferred_element_type=F32) for sl in heads],
                    kv=[lax.dot_general(vc[:, sl], k_out[:, sl], _TN, preferred_element_type=F32) for sl in heads]))
            s_cur = [S_scr[h] for h in range(C_HEADS)]
            for ch in chunks:
                ch['qs'] = [lax.dot_general(ch['q_in'][:, sl], s_cur[h].astype(BF16), _NT,
                                            preferred_element_type=F32) for h, sl in enumerate(heads)]
                s_cur = [s_cur[h] * ch['decay'][:, sl] + ch['kv'][h] for h, sl in enumerate(heads)]
            for h in range(C_HEADS):
                S_scr[h] = s_cur[h]
            for ch in chunks:
                o_c = jnp.concatenate(
                    [jnp.dot(jnp.where(tri, ch['a'][h], 0.0).astype(BF16), ch['vc'][:, sl],
                             preferred_element_type=F32) + ch['qs'][h] for h, sl in enumerate(heads)], axis=-1)
                if reverse:
                    ob_scr[ch['rows'], :] = o_c
                else:
                    of_scr[pl.ds(pl.multiple_of(tile * tt, tt) + ch['r0'], C), :] = o_c
            return carry

        lax.fori_loop(0, n_c // HG_GROUP, body, 0)

    @pl.when(d == 0)
    def _():
        run(False, j)

    @pl.when(d == 1)
    def _():
        tile = n_t - 1 - j
        run(True, tile)
        o = of_scr[pl.ds(pl.multiple_of(tile * tt, tt), tt), :] + ob_scr[...]
        g = zg_ref[...]
        o_ref[...] = (_head_rms(o, gn_ref[...]) * (g * jax.nn.sigmoid(g))).astype(o_ref.dtype)

    @pl.when(j == n_t - 1)
    def _():
        sT_ref[...] = S_scr[...]


def hgrn_mixer(z, row0, nb, L, lb, gn, s0_t):
    tt = min(L, 512)
    n_t = L // tt
    rb = row0 // tt

    def tile(d, j):
        return jnp.where(d == 0, j, n_t - 1 - j)

    def late(d, j):
        return jnp.where(d == 0, n_t - 1, n_t - 1 - j)

    st_spec = pl.BlockSpec((None, None, C_HEADS, HEAD_DIM, HEAD_DIM), lambda b, d, j: (b, d, 0, 0, 0))
    return pl.pallas_call(
        functools.partial(_hgrn_kernel, tt=tt),
        grid=(nb, 2, n_t),
        in_specs=[pl.BlockSpec((tt, 256), lambda b, d, j: (rb + b * n_t + tile(d, j), 4)),
                  pl.BlockSpec((tt, 256), lambda b, d, j: (rb + b * n_t + tile(d, j), 5 + d)),
                  pl.BlockSpec((tt, 256), lambda b, d, j: (rb + b * n_t + tile(d, j), 7)),
                  pl.BlockSpec((tt, 256), lambda b, d, j: (rb + b * n_t + late(d, j), 8)),
                  pl.BlockSpec((1, 256), lambda b, d, j: (0, 0)),
                  pl.BlockSpec((1, HEAD_DIM), lambda b, d, j: (0, 0)),
                  st_spec],
        out_specs=[pl.BlockSpec((tt, 256), lambda b, d, j: (b * n_t + late(d, j), 0)),
                   st_spec],
        out_shape=[jax.ShapeDtypeStruct((nb * L, 256), BF16),
                   jax.ShapeDtypeStruct((nb, 2, C_HEADS, HEAD_DIM, HEAD_DIM), F32)],
        scratch_shapes=[pltpu.VMEM((L, 256), F32),
                        pltpu.VMEM((tt, 256), F32),
                        pltpu.VMEM((tt, 256), F32),
                        pltpu.VMEM((tt, 256), F32),
                        pltpu.VMEM((tt, 256), BF16),
                        pltpu.VMEM((tt, 256), BF16),
                        pltpu.VMEM((tt, 256), BF16),
                        pltpu.VMEM((C_HEADS, HEAD_DIM, HEAD_DIM), F32)],
        compiler_params=_params(("arbitrary", "arbitrary", "arbitrary")),
        name="hgrn_mixer",
    )(z, z, z, z, lb.reshape(1, 256), gn.reshape(1, HEAD_DIM), s0_t)


DL_C = DELTA_CHUNK
DL_PREP_TT = 256
DL_HALO = 8
DL_CHUNK_TT = 512
N_QKV_HEADS = 3 * D_HEADS


def _delta_prep_kernel(x_ref, xp_ref, xn_ref, zab_ref, cw_ref, na_ref, dtb_ref, qkv_ref, gate_ref, xs_scr):
    tt = x_ref.shape[0]
    row = pl.program_id(0) * tt
    lat = row - CTX_TOK
    first = jnp.where(row < CTX_TOK, True, lat % DEC_SEQ == 0)
    last = jnp.where(row < CTX_TOK, True, (lat + tt) % DEC_SEQ == 0)
    xs_scr[DL_HALO:DL_HALO + tt, :] = x_ref[...]
    xs_scr[0:DL_HALO, :] = jnp.where(first, 0.0, xp_ref[...])
    xs_scr[DL_HALO + tt:2 * DL_HALO + tt, :] = jnp.where(last, 0.0, xn_ref[...])
    pad = (CONV_K - 1) // 2
    y = None
    for t in range(CONV_K):
        term = xs_scr[pl.ds(DL_HALO - pad + t, tt), :] * cw_ref[t:t + 1, :]
        y = term if y is None else y + term
    y = y * jax.nn.sigmoid(y)
    for idx in range(N_QKV_HEADS):
        xh = y[:, idx * HEAD_DIM:(idx + 1) * HEAD_DIM]
        if idx < 2 * D_HEADS:
            xh = xh * lax.rsqrt(jnp.sum(xh * xh, axis=-1, keepdims=True) + EPS)
        if idx < D_HEADS:
            xh = xh * ATT_SCALE
        qkv_ref[idx] = xh
    zab = zab_ref[...]
    lane = lax.broadcasted_iota(jnp.int32, zab.shape, 1)
    t_ = zab + dtb_ref[...]
    softplus = jnp.maximum(t_, 0.0) + jnp.log(1.0 + jnp.exp(-jnp.abs(t_)))
    gate_ref[...] = jnp.where(lane < 2 * D_HEADS, na_ref[...] * softplus, jax.nn.sigmoid(zab))


def delta_prep(z, conv_w, a_log, dt_bias):
    tt = DL_PREP_TT
    hb = tt // DL_HALO
    n_hb = N_TOK // DL_HALO
    pad8 = lambda v: jnp.concatenate([v.reshape(1, 2 * D_HEADS), jnp.zeros((1, 128 - 2 * D_HEADS), F32)], axis=1)
    return pl.pallas_call(
        _delta_prep_kernel,
        grid=(N_TOK // tt,),
        in_specs=[pl.BlockSpec((tt, 768), lambda i: (i, 3)),
                  pl.BlockSpec((DL_HALO, 768), lambda i: (jnp.maximum(i * hb - 1, 0), 3)),
                  pl.BlockSpec((DL_HALO, 768), lambda i: (jnp.minimum((i + 1) * hb, n_hb - 1), 3)),
                  pl.BlockSpec((tt, 128), lambda i: (i, Z_DAB // 128)),
                  pl.BlockSpec((CONV_K, 768), lambda i: (0, 0)),
                  pl.BlockSpec((1, 128), lambda i: (0, 0)),
                  pl.BlockSpec((1, 128), lambda i: (0, 0))],
        out_specs=[pl.BlockSpec((N_QKV_HEADS, tt, HEAD_DIM), lambda i: (0, i, 0)),
                   pl.BlockSpec((tt, 128), lambda i: (i, 0))],
        out_shape=[jax.ShapeDtypeStruct((N_QKV_HEADS, N_TOK, HEAD_DIM), F32),
                   jax.ShapeDtypeStruct((N_TOK, 128), F32)],
        scratch_shapes=[pltpu.VMEM((tt + 2 * DL_HALO, 768), F32)],
        compiler_params=_params(("arbitrary",)),
        name="delta_prep",
    )(z, z, z, z, conv_w, pad8(-jnp.exp(a_log)), pad8(dt_bias))


def _split_bf16(a):
    hi = a.astype(BF16)
    return hi, (a - hi.astype(F32)).astype(BF16)


def _dot_hl(a, b_parts):
    a_hi, a_lo = _split_bf16(a)
    b_hi, b_lo = b_parts
    m = a.shape[0]
    r = jnp.dot(jnp.concatenate([a_hi, a_lo], axis=0), b_hi, preferred_element_type=F32)
    return r[:m] + r[m:] + jnp.dot(a_hi, b_lo, preferred_element_type=F32)


def _delta_chunk_kernel(qkv_ref, gate_ref, u2_ref, wq_ref, ak_ref):
    C = DL_C
    n_c = gate_ref.shape[0] // C
    row = lax.broadcasted_iota(jnp.int32, (C, C), 0)
    col = lax.broadcasted_iota(jnp.int32, (C, C), 1)
    eye = (row == col).astype(F32)

    def body(c, carry):
        r0 = pl.multiple_of(c * C, C)
        ga = gate_ref[pl.ds(r0, C), :]
        chains = []
        for d in range(2):
            incl = (row >= col) if d == 0 else (row <= col)
            strict = (row > col) if d == 0 else (row < col)
            g_all = jnp.dot(incl.astype(F32), ga, precision=lax.Precision.HIGHEST, preferred_element_type=F32)
            g_all_t = g_all.T
            for h in range(D_HEADS):
                ci = d * D_HEADS + h
                q = qkv_ref[h, pl.ds(r0, C), :]
                k = qkv_ref[D_HEADS + h, pl.ds(r0, C), :]
                v = qkv_ref[2 * D_HEADS + h, pl.ds(r0, C), :]
                g_col = g_all[:, ci:ci + 1]
                g_row = g_all_t[ci:ci + 1, :]
                beta = ga[:, 2 * D_HEADS + ci:2 * D_HEADS + ci + 1]
                g_end = g_col[C - 1:C, :] if d == 0 else g_col[0:1, :]
                kb = k * beta
                eg = jnp.exp(g_col)
                decay = jnp.where(incl, jnp.exp(jnp.where(incl, g_col - g_row, 0.0)), 0.0)
                kq = jnp.concatenate([kb, q], axis=0).astype(BF16)
                chains.append(dict(
                    strict=strict, decay=decay, qg=q * eg, g_end=g_end,
                    r=lax.dot_general(kq, k.astype(BF16), _NT, preferred_element_type=F32),
                    rhs=jnp.concatenate([v * beta, kb * eg], axis=1).astype(BF16),
                    ke_t=(k * jnp.exp(g_end - g_col)).T))
        for ch in chains:
            ch['p'] = -jnp.where(ch['strict'], ch['r'][:C] * ch['decay'], 0.0)
            ch['t'] = eye + ch['p']
        for _ in range(5):
            for ch in chains:
                ch['p'] = _dot_hl(ch['p'], _split_bf16(ch['p']))
            for ch in chains:
                ch['t'] = ch['t'] + _dot_hl(ch['t'], _split_bf16(ch['p']))
        for ch in chains:
            ch['uw'] = jnp.dot(ch['t'].astype(BF16), ch['rhs'], preferred_element_type=F32)
        u2 = [jnp.concatenate([ch['uw'][:, :C], jnp.broadcast_to(jnp.exp(ch['g_end']), (C, C))], axis=1)
              for ch in chains]
        wq = [jnp.concatenate([ch['uw'][:, C:], ch['qg']], axis=0).astype(BF16) for ch in chains]
        ak = [jnp.concatenate([ch['r'][C:] * ch['decay'], ch['ke_t']], axis=0).astype(BF16) for ch in chains]
        pack = lambda xs: jnp.stack(xs).reshape((2, D_HEADS) + xs[0].shape)
        u2_ref[:, :, pl.ds(r0, C), :] = pack(u2)
        wq_ref[:, :, c] = pack(wq)
        ak_ref[:, :, c] = pack(ak)
        return carry

    lax.fori_loop(0, n_c, body, 0)


def delta_chunks(qkv, gates):
    tt = DL_CHUNK_TT
    n_c = tt // DL_C
    return pl.pallas_call(
        _delta_chunk_kernel,
        grid=(N_TOK // tt,),
        in_specs=[pl.BlockSpec((N_QKV_HEADS, tt, HEAD_DIM), lambda i: (0, i, 0)),
                  pl.BlockSpec((tt, 128), lambda i: (i, 0))],
        out_specs=[pl.BlockSpec((2, D_HEADS, tt, 128), lambda i: (0, 0, i, 0)),
                   pl.BlockSpec((2, D_HEADS, n_c, 2 * DL_C, HEAD_DIM), lambda i: (0, 0, i, 0, 0)),
                   pl.BlockSpec((2, D_HEADS, n_c, 2 * DL_C, HEAD_DIM), lambda i: (0, 0, i, 0, 0))],
        out_shape=[jax.ShapeDtypeStruct((2, D_HEADS, N_TOK, 128), F32),
                   jax.ShapeDtypeStruct((2, D_HEADS, N_TOK // DL_C, 2 * DL_C, HEAD_DIM), BF16),
                   jax.ShapeDtypeStruct((2, D_HEADS, N_TOK // DL_C, 2 * DL_C, HEAD_DIM), BF16)],
        compiler_params=_params(("arbitrary",)),
        name="delta_chunks",
    )(qkv, gates)


def _delta_scan_kernel(u2f_ref, wqf_ref, akf_ref, u2b_ref, wqb_ref, akb_ref, s0_ref,
                       of_ref, ob_ref, s_ref, s_scr):
    j = pl.program_id(1)
    C = DL_C
    n_c = wqf_ref.shape[1]

    @pl.when(j == 0)
    def _():
        s_scr[...] = s0_ref[...]

    def body(ci, carry):
        chains = []
        for d, (u2_ref, wq_ref, ak_ref, o_ref) in enumerate(((u2f_ref, wqf_ref, akf_ref, of_ref),
                                                            (u2b_ref, wqb_ref, akb_ref, ob_ref))):
            c = ci if d == 0 else n_c - 1 - ci
            r0 = pl.multiple_of(c * C, C)
            for h in range(D_HEADS):
                s = s_scr[d, h]
                chains.append(dict(d=d, h=h, c=c, r0=r0, s=s, ak_ref=ak_ref, o_ref=o_ref,
                                   u2=u2_ref[h, pl.ds(r0, C), :],
                                   r1=jnp.dot(wq_ref[h, c], s.astype(BF16),
                                              preferred_element_type=F32)))
        for ch in chains:
            v_new = ch['u2'][:, :C] - ch['r1'][:C]
            ch['r2'] = jnp.dot(ch['ak_ref'][ch['h'], ch['c']], v_new.astype(BF16),
                               preferred_element_type=F32)
        for ch in chains:
            ch['o_ref'][ch['h'], pl.ds(ch['r0'], C), :] = ch['r1'][C:] + ch['r2'][:C]
            s_scr[ch['d'], ch['h']] = ch['s'] * ch['u2'][0:1, C:] + ch['r2'][C:]
        return carry

    lax.fori_loop(0, n_c, body, 0)

    @pl.when(j == pl.num_programs(1) - 1)
    def _():
        s_ref[...] = s_scr[...]


def delta_scan(u2, wq, ak, row0, nb, L, s0):
    tt = min(L, 512)
    n_t = L // tt
    n_c = tt // DL_C
    rb = row0 // tt
    fwd = lambda b, j: rb + b * n_t + j
    bwd = lambda b, j: rb + b * n_t + (n_t - 1 - j)
    u_spec = lambda d, f: pl.BlockSpec((None, D_HEADS, tt, 128), lambda b, j: (d, 0, f(b, j), 0))
    c_spec = lambda d, f: pl.BlockSpec((None, D_HEADS, n_c, 2 * DL_C, HEAD_DIM), lambda b, j: (d, 0, f(b, j), 0, 0))
    st_spec = pl.BlockSpec((None, 2, D_HEADS, HEAD_DIM, HEAD_DIM), lambda b, j: (b, 0, 0, 0, 0))
    return pl.pallas_call(
        _delta_scan_kernel,
        grid=(nb, n_t),
        in_specs=[u_spec(0, fwd), c_spec(0, fwd), c_spec(0, fwd),
                  u_spec(1, bwd), c_spec(1, bwd), c_spec(1, bwd), st_spec],
        out_specs=[pl.BlockSpec((D_HEADS, tt, HEAD_DIM), lambda b, j: (0, b * n_t + j, 0)),
                   pl.BlockSpec((D_HEADS, tt, HEAD_DIM), lambda b, j: (0, b * n_t + (n_t - 1 - j), 0)),
                   st_spec],
        out_shape=[jax.ShapeDtypeStruct((D_HEADS, nb * L, HEAD_DIM), F32),
                   jax.ShapeDtypeStruct((D_HEADS, nb * L, HEAD_DIM), F32),
                   jax.ShapeDtypeStruct((nb, 2, D_HEADS, HEAD_DIM, HEAD_DIM), F32)],
        scratch_shapes=[pltpu.VMEM((2, D_HEADS, HEAD_DIM, HEAD_DIM), F32)],
        compiler_params=_params(("arbitrary", "arbitrary")),
        name="delta_scan",
    )(u2, wq, ak, u2, wq, ak, s0)


def _delta_out_kernel(of_ref, ob_ref, zg_ref, gn_ref, o_ref):
    outs = []
    for h in range(D_HEADS):
        o = of_ref[h] + ob_ref[h]
        outs.append(o * lax.rsqrt(jnp.mean(o * o, axis=-1, keepdims=True) + EPS) * gn_ref[...])
    g = zg_ref[...]
    o_ref[...] = (jnp.concatenate(outs, axis=-1) * (g * jax.nn.sigmoid(g))).astype(o_ref.dtype)


def delta_output(o_f, o_b, z, row0, gn):
    n = o_f.shape[1]
    tt = 256
    return pl.pallas_call(
        _delta_out_kernel,
        grid=(n // tt,),
        in_specs=[pl.BlockSpec((D_HEADS, tt, HEAD_DIM), lambda i: (0, i, 0)),
                  pl.BlockSpec((D_HEADS, tt, HEAD_DIM), lambda i: (0, i, 0)),
                  pl.BlockSpec((tt, 256), lambda i: (row0 // tt + i, Z_DG // 256)),
                  pl.BlockSpec((1, HEAD_DIM), lambda i: (0, 0))],
        out_specs=pl.BlockSpec((tt, 256), lambda i: (i, 0)),
        out_shape=jax.ShapeDtypeStruct((n, 256), BF16),
        compiler_params=_params(("arbitrary",)),
        name="delta_output",
    )(o_f, o_b, z, gn.reshape(1, HEAD_DIM))


def kernel(x_prompt, x_sample, cache_attn_a_k, cache_attn_a_v, cache_attn_b_k, cache_attn_b_v,
           state_hgrn, state_delta, c, c_ctx, norm1_g, norm2_g, w_ada, b_ada, w_in, a_sink,
           b_qnorm_g, b_knorm_g, c_lb, c_onorm_g, d_conv, d_a_log, d_dt_bias, d_onorm_g,
           w_branch, w_out, ffn_w1, ffn_w3, ffn_w2, router_w, router_b, moe_w1, moe_w3, moe_w2,
           final_norm_g):
    cum = jnp.cumsum(jax.nn.softmax(c_lb, axis=0), axis=0)
    lower_bounds = cum - cum[:1]

    x = jnp.concatenate([x_prompt.reshape(CTX_TOK, D_MODEL), x_sample.reshape(LAT_TOK, D_MODEL)], axis=0)
    cond = jnp.concatenate([c_ctx[None, :], c, jnp.zeros((16 - N_COND, D_MODEL), F32)], axis=0)

    rope_c, rope_s = rope_lane_tables(DEC_SEQ)
    caches = []
    for l in range(DEPTH):
        mod = ada_modulation(cond, w_ada[l], b_ada[l])[:N_COND].reshape(N_COND, 6, D_MODEL)
        w_mix = jnp.concatenate([w_in[l][:, :Z_MAIN], w_in[l][:, Z_MAIN + 16:W_IN_MIX],
                                 w_in[l][:, Z_MAIN:Z_MAIN + 16], jnp.zeros((D_MODEL, 128 - 16), F32)],
                                axis=1).astype(BF16)
        w_gl = w_in[l][:, W_IN_MIX:].reshape(D_MODEL, N_BRANCH, D_MODEL).transpose(1, 0, 2).astype(BF16)
        z = input_projection(x, mod, norm1_g[l], w_mix)
        kv2 = lambda t: t.reshape(DEC_BATCH, PAST_LEN, 128)
        o_ab_ctx, bk_ctx = ctx_attention(z, a_sink[l], b_qnorm_g[l], b_knorm_g[l])
        o_a_lat = latent_attention_a(z, CTX_TOK, DEC_BATCH, DEC_SEQ, a_sink[l], kv2(cache_attn_a_k[:, l]),
                                     kv2(cache_attn_a_v[:, l]), rope_c, rope_s)
        o_b_lat = latent_attention_b(z, CTX_TOK, DEC_BATCH, DEC_SEQ, kv2(cache_attn_b_k[:, l]),
                                     kv2(cache_attn_b_v[:, l]), rope_c, rope_s, b_qnorm_g[l], b_knorm_g[l])
        o_c_ctx, sc_t = hgrn_mixer(z, 0, BATCH, SEQ, lower_bounds[l], c_onorm_g[l],
                                   jnp.zeros((BATCH, 2, C_HEADS, HEAD_DIM, HEAD_DIM), F32))
        o_c_lat, _ = hgrn_mixer(z, CTX_TOK, DEC_BATCH, DEC_SEQ, lower_bounds[l], c_onorm_g[l],
                                jnp.swapaxes(state_hgrn[:, l], -1, -2))

        qkv, gates = delta_prep(z, d_conv[l], d_a_log[l], d_dt_bias[l])
        u2, wq, ak = delta_chunks(qkv, gates)
        of_ctx, ob_ctx, sd = delta_scan(u2, wq, ak, 0, BATCH, SEQ,
                                        jnp.zeros((BATCH, 2, D_HEADS, HEAD_DIM, HEAD_DIM), F32))
        of_lat, ob_lat, _ = delta_scan(u2, wq, ak, CTX_TOK, DEC_BATCH, DEC_SEQ, state_delta[:, l])
        o_d_ctx = delta_output(of_ctx, ob_ctx, z, 0, d_onorm_g[l])
        o_d_lat = delta_output(of_lat, ob_lat, z, CTX_TOK, d_onorm_g[l])
        kvh = lambda t: t.reshape(BATCH, SEQ, 2, HEAD_DIM)
        caches.append((kvh(z[:CTX_TOK, 256:384]), kvh(z[:CTX_TOK, 384:512]), kvh(bk_ctx), kvh(z[:CTX_TOK, 896:1024]),
                       jnp.swapaxes(sc_t, -1, -2), sd))
        x = merge_projection(x, mod, norm1_g[l], (o_ab_ctx, o_c_ctx, o_d_ctx), (o_a_lat, o_b_lat, o_c_lat, o_d_lat),
                             w_gl, w_branch[l].astype(BF16), w_out[l].astype(BF16))
        j = l // 2
        if l % 2 == 0:
            x = dense_ffn(x, mod, norm2_g[l], ffn_w1[j].astype(BF16), ffn_w3[j].astype(BF16),
                          ffn_w2[j].astype(BF16))
        else:
            assert l == DEPTH - 1, "the expert layer's residual is fused with the final norm"
            rw = jnp.concatenate([router_w[j], jnp.zeros((D_MODEL, 128 - N_EXPERTS), F32)], axis=1)
            rb = jnp.concatenate([router_b[j], jnp.zeros((128 - N_EXPERTS,), F32)])[None, :]
            h2, pos, wgt = moe_router(x, mod, norm2_g[l], rw, rb)
            f = moe_experts(h2, pos, wgt, moe_w1[j].astype(BF16), moe_w3[j].astype(BF16), moe_w2[j].astype(BF16))
            y_prompt = residual_final_norm(x, f, mod, final_norm_g, 0, CTX_TOK).reshape(BATCH, SEQ, D_MODEL)
            y_sample = residual_final_norm(x, f, mod, final_norm_g, CTX_TOK, LAT_TOK).reshape(DEC_BATCH, DEC_SEQ, D_MODEL)

    stack = lambda idx: jnp.stack([caches[l][idx] for l in range(DEPTH)], axis=1)
    return (y_prompt, y_sample, stack(0), stack(1), stack(2), stack(3), stack(4), stack(5))
```

```python
import functools

import jax
import jax.numpy as jnp
import numpy as np
from jax import lax
from jax.experimental import pallas as pl
from jax.experimental.pallas import tpu as pltpu

F32 = jnp.float32
BF16 = jnp.bfloat16

D_MODEL = 1024
BATCH = 32
SEQ = 256
DEPTH = 2
DEC_BATCH = 8
DEC_SEQ = 4096
PAST_LEN = 256
GRID_W = 64
HEAD_DIM = 64
A_HEADS = 4
A_KV_HEADS = 2
B_HEADS = 4
B_KV_HEADS = 2
C_HEADS = 4
D_HEADS = 4
BRANCH_W = 256
N_BRANCH = 4
WINDOW = 128
BLOCK = 128
ROPE_BASE = 10000.0
HGRN_CHUNK = 32
DELTA_CHUNK = 64
CONV_K = 5
D_FF = 2816
N_EXPERTS = 8
D_FF_EXPERT = 3584
EPS = 1e-6
NEG_INF = -1e30
F32_MIN = float(np.finfo(np.float32).min)

CTX_TOK = BATCH * SEQ
LAT_TOK = DEC_BATCH * DEC_SEQ
N_TOK = CTX_TOK + LAT_TOK
N_COND = 1 + DEC_BATCH

Z_MAIN = 3072
Z_DG = Z_MAIN
Z_DAB = Z_DG + BRANCH_W
Z_COLS = Z_DAB + 128
W_IN_MIX = 3344

TM = 512
VMEM_LIMIT = 56 * 1024 * 1024


def _tile_cond(i, tm):
    ctx_tiles = CTX_TOK // tm
    per_b = DEC_SEQ // tm
    return jnp.where(i < ctx_tiles, 0, 1 + (i - ctx_tiles) // per_b)


def _rms(x, g):
    return x * lax.rsqrt(jnp.mean(x * x, axis=-1, keepdims=True) + EPS) * g


def _params(sem):
    return pltpu.CompilerParams(dimension_semantics=sem, vmem_limit_bytes=VMEM_LIMIT)


def _ada_kernel(c_ref, w_ref, b_ref, o_ref):
    c = c_ref[...]
    s = c * jax.nn.sigmoid(c)
    o_ref[...] = jnp.dot(s.astype(BF16), w_ref[...].astype(BF16), preferred_element_type=F32) + b_ref[...]


def ada_modulation(cond_pad, w, b):
    n = 6 * D_MODEL
    tn = 1536
    return pl.pallas_call(
        _ada_kernel,
        grid=(n // tn,),
        in_specs=[pl.BlockSpec((16, D_MODEL), lambda j: (0, 0)),
                  pl.BlockSpec((D_MODEL, tn), lambda j: (0, j)),
                  pl.BlockSpec((1, tn), lambda j: (0, j))],
        out_specs=pl.BlockSpec((16, tn), lambda j: (0, j)),
        out_shape=jax.ShapeDtypeStruct((16, n), F32),
        compiler_params=_params(("arbitrary",)),
        name="ada_modulation",
    )(cond_pad, w, b.reshape(1, n))


def _in_kernel(x_ref, mod_ref, g_ref, w_ref, z_ref):
    h = _rms(x_ref[...], g_ref[...]) * (1.0 + mod_ref[1:2, :]) + mod_ref[0:1, :]
    z_ref[...] = jnp.dot(h.astype(BF16), w_ref[...], preferred_element_type=F32)


def input_projection(x, mod, g, w):
    nt = N_TOK // TM
    return pl.pallas_call(
        _in_kernel,
        grid=(nt,),
        in_specs=[pl.BlockSpec((TM, D_MODEL), lambda i: (i, 0)),
                  pl.BlockSpec((None, 6, D_MODEL), lambda i: (_tile_cond(i, TM), 0, 0)),
                  pl.BlockSpec((1, D_MODEL), lambda i: (0, 0)),
                  pl.BlockSpec((D_MODEL, Z_COLS), lambda i: (0, 0))],
        out_specs=pl.BlockSpec((TM, Z_COLS), lambda i: (i, 0)),
        out_shape=jax.ShapeDtypeStruct((N_TOK, Z_COLS), F32),
        compiler_params=_params(("arbitrary",)),
        name="input_projection",
    )(x, mod, g.reshape(1, D_MODEL), w)


def _merge_kernel(x_ref, mod_ref, g_ref, ab_c_ref, c_c_ref, d_c_ref, a_l_ref, b_l_ref, c_l_ref, d_l_ref,
                  wgl_ref, wbr_ref, wout_ref, xo_ref):
    x = x_ref[...]
    h = (_rms(x, g_ref[...]) * (1.0 + mod_ref[1:2, :]) + mod_ref[0:1, :]).astype(BF16)
    is_ctx = pl.program_id(0) < CTX_TOK // TM
    branches = (jnp.where(is_ctx, ab_c_ref[:, :BRANCH_W], a_l_ref[...]),
                jnp.where(is_ctx, ab_c_ref[:, BRANCH_W:], b_l_ref[...]),
                jnp.where(is_ctx, c_c_ref[...], c_l_ref[...]),
                jnp.where(is_ctx, d_c_ref[...], d_l_ref[...]))
    merged = None
    for j in range(N_BRANCH):
        gate = jax.nn.sigmoid(jnp.dot(h, wgl_ref[j], preferred_element_type=F32))
        br = jnp.dot(branches[j], wbr_ref[j], preferred_element_type=F32)
        merged = gate * br if merged is None else merged + gate * br
    mix = jnp.dot(merged.astype(BF16), wout_ref[...], preferred_element_type=F32)
    xo_ref[...] = x + mod_ref[2:3, :] * mix


def merge_projection(x, mod, g, o_ctx, o_lat, wgl, wbr, wout):
    nt = N_TOK // TM
    ctx_tiles = CTX_TOK // TM
    ctx_spec = lambda w: pl.BlockSpec((TM, w), lambda i: (jnp.minimum(i, ctx_tiles - 1), 0))
    lat_spec = pl.BlockSpec((TM, BRANCH_W), lambda i: (jnp.maximum(i - ctx_tiles, 0), 0))
    return pl.pallas_call(
        _merge_kernel,
        grid=(nt,),
        in_specs=[pl.BlockSpec((TM, D_MODEL), lambda i: (i, 0)),
                  pl.BlockSpec((None, 6, D_MODEL), lambda i: (_tile_cond(i, TM), 0, 0)),
                  pl.BlockSpec((1, D_MODEL), lambda i: (0, 0)),
                  ctx_spec(2 * BRANCH_W), ctx_spec(BRANCH_W), ctx_spec(BRANCH_W),
                  lat_spec, lat_spec, lat_spec, lat_spec,
                  pl.BlockSpec((N_BRANCH, D_MODEL, D_MODEL), lambda i: (0, 0, 0)),
                  pl.BlockSpec((N_BRANCH, BRANCH_W, D_MODEL), lambda i: (0, 0, 0)),
                  pl.BlockSpec((D_MODEL, D_MODEL), lambda i: (0, 0))],
        out_specs=pl.BlockSpec((TM, D_MODEL), lambda i: (i, 0)),
        out_shape=jax.ShapeDtypeStruct((N_TOK, D_MODEL), F32),
        compiler_params=_params(("arbitrary",)),
        name="merge_projection",
    )(x, mod, g.reshape(1, D_MODEL), *o_ctx, *o_lat, wgl, wbr, wout)


def _ffn_kernel(x_ref, mod_ref, g_ref, w1_ref, w3_ref, w2_ref, xo_ref):
    x = x_ref[...]
    h = (_rms(x, g_ref[...]) * (1.0 + mod_ref[4:5, :]) + mod_ref[3:4, :]).astype(BF16)
    a = jnp.dot(h, w1_ref[...], preferred_element_type=F32)
    b = jnp.dot(h, w3_ref[...], preferred_element_type=F32)
    hid = (a * jax.nn.sigmoid(a) * b).astype(BF16)
    f = jnp.dot(hid, w2_ref[...], preferred_element_type=F32)
    xo_ref[...] = x + mod_ref[5:6, :] * f


def dense_ffn(x, mod, g, w1, w3, w2):
    nt = N_TOK // TM
    const = lambda i: (0, 0)
    return pl.pallas_call(
        _ffn_kernel,
        grid=(nt,),
        in_specs=[pl.BlockSpec((TM, D_MODEL), lambda i: (i, 0)),
                  pl.BlockSpec((None, 6, D_MODEL), lambda i: (_tile_cond(i, TM), 0, 0)),
                  pl.BlockSpec((1, D_MODEL), const),
                  pl.BlockSpec((D_MODEL, D_FF), const, pipeline_mode=pl.Buffered(1)),
                  pl.BlockSpec((D_MODEL, D_FF), const, pipeline_mode=pl.Buffered(1)),
                  pl.BlockSpec((D_FF, D_MODEL), const, pipeline_mode=pl.Buffered(1))],
        out_specs=pl.BlockSpec((TM, D_MODEL), lambda i: (i, 0)),
        out_shape=jax.ShapeDtypeStruct((N_TOK, D_MODEL), F32),
        compiler_params=_params(("arbitrary",)),
        name="dense_ffn",
    )(x, mod, g.reshape(1, D_MODEL), w1, w3, w2)


MOE_T = 1024
MOE_R = 64
MOE_KMAX = 8
MOE_F = 1792


def _router_kernel(x_ref, mod_ref, g_ref, rw_ref, rb_ref, h_ref, pos_ref, wgt_ref):
    t = x_ref.shape[0]
    h = _rms(x_ref[...], g_ref[...]) * (1.0 + mod_ref[4:5, :]) + mod_ref[3:4, :]
    h_ref[...] = h.astype(BF16)
    logits = jnp.dot(h, rw_ref[...], preferred_element_type=F32, precision=lax.Precision.HIGHEST) + rb_ref[...]
    lt = logits.T[:N_EXPERTS, :]
    eidx = lax.broadcasted_iota(jnp.int32, lt.shape, 0)
    m1 = jnp.max(lt, axis=0, keepdims=True)
    i1 = jnp.min(jnp.where(lt == m1, eidx, N_EXPERTS), axis=0, keepdims=True)
    rest = jnp.where(eidx == i1, F32_MIN, lt)
    m2 = jnp.max(rest, axis=0, keepdims=True)
    i2 = jnp.min(jnp.where(rest == m2, eidx, N_EXPERTS), axis=0, keepdims=True)
    e2 = jnp.exp(m2 - m1)
    p1 = 1.0 / (1.0 + e2)
    p2 = e2 / (1.0 + e2)
    wgt_ref[...] = jnp.where(eidx == i1, p1, 0.0) + jnp.where(eidx == i2, p2, 0.0)
    routed = jnp.where(eidx == i1, 1.0, jnp.where(eidx == i2, 1.0, 0.0))
    s_id = lax.broadcasted_iota(jnp.int32, (t, t), 0)
    t_id = lax.broadcasted_iota(jnp.int32, (t, t), 1)
    before = jnp.where(s_id < t_id, 1.0, 0.0).astype(BF16)
    rank = jnp.dot(routed.astype(BF16), before, preferred_element_type=F32)
    pos_ref[...] = jnp.where(routed > 0.0, rank.astype(jnp.int32), -1)


def moe_router(x, mod, g, rw, rb):
    nt = N_TOK // MOE_T
    return pl.pallas_call(
        _router_kernel,
        grid=(nt,),
        in_specs=[pl.BlockSpec((MOE_T, D_MODEL), lambda i: (i, 0)),
                  pl.BlockSpec((None, 6, D_MODEL), lambda i: (_tile_cond(i, MOE_T), 0, 0)),
                  pl.BlockSpec((1, D_MODEL), lambda i: (0, 0)),
                  pl.BlockSpec((D_MODEL, 128), lambda i: (0, 0)),
                  pl.BlockSpec((1, 128), lambda i: (0, 0))],
        out_specs=[pl.BlockSpec((MOE_T, D_MODEL), lambda i: (i, 0)),
                   pl.BlockSpec((N_EXPERTS, MOE_T), lambda i: (0, i)),
                   pl.BlockSpec((N_EXPERTS, MOE_T), lambda i: (0, i))],
        out_shape=[jax.ShapeDtypeStruct((N_TOK, D_MODEL), BF16),
                   jax.ShapeDtypeStruct((N_EXPERTS, N_TOK), jnp.int32),
                   jax.ShapeDtypeStruct((N_EXPERTS, N_TOK), F32)],
        compiler_params=_params(("arbitrary",)),
        name="moe_router",
    )(x, mod, g.reshape(1, D_MODEL), rw, rb)


def _moe_sparse_kernel(h_ref, pos_ref, wgt_ref, w1_ref, w3_ref, w2_ref, y_ref, xg_scr, acc_scr, wr_scr):
    e = pl.program_id(1)
    f = pl.program_id(2)
    t = h_ref.shape[0]
    pos_e = pos_ref[pl.ds(e, 1), :]
    n_blocks = (jnp.max(pos_e) + MOE_R) // MOE_R
    row_id = lax.broadcasted_iota(jnp.int32, (MOE_R, t), 0)

    def block_rows(r):
        return pl.ds(pl.multiple_of(r * MOE_R, MOE_R), MOE_R)

    def selects(r):
        return pos_e == row_id + r * MOE_R

    @pl.when((e == 0) & (f == 0))
    def _():
        y_ref[...] = jnp.zeros_like(y_ref)

    last_f = pl.num_programs(2) - 1

    def fixed_rows(k):
        m = k * MOE_R
        sel = pos_e == lax.broadcasted_iota(jnp.int32, (m, t), 0)
        onehot = jnp.where(sel, 1.0, 0.0).astype(BF16)

        @pl.when(f == 0)
        def _():
            xg_scr[0:m, :] = jnp.dot(onehot, h_ref[...], preferred_element_type=F32).astype(BF16)
            w_rows = jnp.sum(jnp.where(sel, wgt_ref[pl.ds(e, 1), :], 0.0), axis=1, keepdims=True)
            wr_scr[0:m, :] = jnp.broadcast_to(w_rows, (m, 128))

        xg = xg_scr[0:m, :]
        a = jnp.dot(xg, w1_ref[...], preferred_element_type=F32)
        b = jnp.dot(xg, w3_ref[...], preferred_element_type=F32)
        hid = (a * jax.nn.sigmoid(a) * b * wr_scr[0:m, 0:1]).astype(BF16)
        part = jnp.dot(hid, w2_ref[...], preferred_element_type=F32)

        @pl.when(f == 0)
        def _():
            acc_scr[0:m, :] = part

        @pl.when((f > 0) & (f < last_f))
        def _():
            acc_scr[0:m, :] += part

        @pl.when(f == last_f)
        def _():
            hi, lo = _split_bf16(acc_scr[0:m, :] + part)
            y_ref[...] += (lax.dot_general(onehot, hi, _TN, preferred_element_type=F32)
                           + lax.dot_general(onehot, lo, _TN, preferred_element_type=F32))

    for k in range(1, MOE_KMAX + 1):
        pl.when(n_blocks == k)(functools.partial(fixed_rows, k))

    @pl.when(n_blocks > MOE_KMAX)
    def _():
        @pl.when(f == 0)
        def _():
            wgt_e = wgt_ref[pl.ds(e, 1), :]

            def gather(r, carry):
                sel = selects(r)
                xg = jnp.dot(jnp.where(sel, 1.0, 0.0).astype(BF16), h_ref[...], preferred_element_type=F32)
                xg_scr[block_rows(r), :] = xg.astype(BF16)
                w_rows = jnp.sum(jnp.where(sel, wgt_e, 0.0), axis=1, keepdims=True)
                wr_scr[block_rows(r), :] = jnp.broadcast_to(w_rows, (MOE_R, 128))
                acc_scr[block_rows(r), :] = jnp.zeros((MOE_R, D_MODEL), F32)
                return carry

            lax.fori_loop(0, n_blocks, gather, 0)

        def expert(r, carry):
            xg = xg_scr[block_rows(r), :]
            a = jnp.dot(xg, w1_ref[...], preferred_element_type=F32)
            b = jnp.dot(xg, w3_ref[...], preferred_element_type=F32)
            hid = (a * jax.nn.sigmoid(a) * b * wr_scr[block_rows(r), 0:1]).astype(BF16)
            acc_scr[block_rows(r), :] += jnp.dot(hid, w2_ref[...], preferred_element_type=F32)
            return carry

        lax.fori_loop(0, n_blocks, expert, 0)

        @pl.when(f == last_f)
        def _():
            def scatter(r, carry):
                onehot = jnp.where(selects(r), 1.0, 0.0).astype(BF16)
                hi, lo = _split_bf16(acc_scr[block_rows(r), :])
                y_ref[...] += (lax.dot_general(onehot, hi, _TN, preferred_element_type=F32)
                               + lax.dot_general(onehot, lo, _TN, preferred_element_type=F32))
                return carry

            lax.fori_loop(0, n_blocks, scatter, 0)


def moe_experts(h2, pos, wgt, w1, w3, w2):
    nt = N_TOK // MOE_T
    nf = D_FF_EXPERT // MOE_F
    return pl.pallas_call(
        _moe_sparse_kernel,
        grid=(nt, N_EXPERTS, nf),
        in_specs=[pl.BlockSpec((MOE_T, D_MODEL), lambda i, e, f: (i, 0)),
                  pl.BlockSpec((N_EXPERTS, MOE_T), lambda i, e, f: (0, i)),
                  pl.BlockSpec((N_EXPERTS, MOE_T), lambda i, e, f: (0, i)),
                  pl.BlockSpec((None, D_MODEL, MOE_F), lambda i, e, f: (e, 0, f)),
                  pl.BlockSpec((None, D_MODEL, MOE_F), lambda i, e, f: (e, 0, f)),
                  pl.BlockSpec((None, MOE_F, D_MODEL), lambda i, e, f: (e, f, 0))],
        out_specs=pl.BlockSpec((MOE_T, D_MODEL), lambda i, e, f: (i, 0)),
        out_shape=jax.ShapeDtypeStruct((N_TOK, D_MODEL), F32),
        scratch_shapes=[pltpu.VMEM((MOE_T, D_MODEL), BF16),
                        pltpu.VMEM((MOE_T, D_MODEL), F32),
                        pltpu.VMEM((MOE_T, 128), F32)],
        compiler_params=_params(("arbitrary", "arbitrary", "arbitrary")),
        name="moe_experts",
    )(h2, pos, wgt, w1, w3, w2)


def _residual_norm_kernel(x_ref, y_ref, mod_ref, g_ref, o_ref):
    o_ref[...] = _rms(x_ref[...] + mod_ref[5:6, :] * y_ref[...], g_ref[...])


def residual_final_norm(x, y, mod, g, row0, n_rows):
    tm = 1024
    t0 = row0 // tm
    return pl.pallas_call(
        _residual_norm_kernel,
        grid=(n_rows // tm,),
        in_specs=[pl.BlockSpec((tm, D_MODEL), lambda i: (t0 + i, 0)),
                  pl.BlockSpec((tm, D_MODEL), lambda i: (t0 + i, 0)),
                  pl.BlockSpec((None, 6, D_MODEL), lambda i: (_tile_cond(t0 + i, tm), 0, 0)),
                  pl.BlockSpec((1, D_MODEL), lambda i: (0, 0))],
        out_specs=pl.BlockSpec((tm, D_MODEL), lambda i: (i, 0)),
        out_shape=jax.ShapeDtypeStruct((n_rows, D_MODEL), F32),
        compiler_params=_params(("arbitrary",)),
        name="residual_final_norm",
    )(x, y, mod, g.reshape(1, D_MODEL))


ATT_SCALE = HEAD_DIM ** -0.5
_NT = (((1,), (1,)), ((), ()))


def _head_rms(x, g_row):
    outs = []
    for h in range(x.shape[1] // HEAD_DIM):
        xh = x[:, h * HEAD_DIM:(h + 1) * HEAD_DIM]
        outs.append(xh * lax.rsqrt(jnp.mean(xh * xh, axis=-1, keepdims=True) + EPS) * g_row)
    return jnp.concatenate(outs, axis=-1)


def _rope_apply(x, c, s):
    w = x.shape[-1]
    lane = lax.broadcasted_iota(jnp.int32, x.shape, 1)
    first_half = ((lane // (HEAD_DIM // 4)) % 2) == 0
    partner = jnp.where(first_half, pltpu.roll(x, w - HEAD_DIM // 4, 1), pltpu.roll(x, HEAD_DIM // 4, 1))
    return x * c + partner * s


def rope_lane_tables(L):
    rows = L // GRID_W
    row = jnp.repeat(jnp.arange(rows, dtype=F32), GRID_W)
    col = jnp.tile(jnp.arange(GRID_W, dtype=F32), rows)
    n_freq = HEAD_DIM // 4
    inv = ROPE_BASE ** (-jnp.arange(n_freq, dtype=F32) / n_freq)
    ang = jnp.stack([row, col], 0)[:, :, None] * inv
    cos, sin = jnp.cos(ang), jnp.sin(ang)
    c = jnp.concatenate([cos[0], cos[0], cos[1], cos[1]], axis=-1)
    s = jnp.concatenate([-sin[0], sin[0], -sin[1], sin[1]], axis=-1)
    return jnp.tile(c, (1, 4)), jnp.tile(s, (1, 4))


def _with_ones(v):
    ones = jnp.ones((v.shape[0], HEAD_DIM), BF16)
    parts = []
    for h in range(v.shape[1] // HEAD_DIM):
        parts += [v[:, h * HEAD_DIM:(h + 1) * HEAD_DIM].astype(BF16), ones]
    return jnp.concatenate(parts, axis=-1)


def _attend_heads(jobs):
    for job in jobs:
        q = (job['q'] * ATT_SCALE).astype(BF16)
        job['s'] = lax.dot_general(q, job['k'], _NT, preferred_element_type=F32)
        if job.get('extra') is not None:
            job['s2'] = lax.dot_general(q, job['extra'][0], _NT, preferred_element_type=F32)
    outs = []
    for job in jobs:
        s, sink = job['s'], job.get('sink')
        if job.get('mask') is not None:
            s = jnp.where(job['mask'], s, NEG_INF)
        m = jnp.max(s, axis=-1, keepdims=True)
        if 's2' in job:
            m = jnp.maximum(m, jnp.max(job['s2'], axis=-1, keepdims=True))
        if sink is not None:
            m = jnp.maximum(m, sink)
        o = jnp.dot(jnp.exp(s - m).astype(BF16), job['v'], preferred_element_type=F32)
        if 's2' in job:
            o = o + jnp.dot(jnp.exp(job['s2'] - m).astype(BF16), job['extra'][1], preferred_element_type=F32)
        den = o[:, HEAD_DIM:HEAD_DIM + 1]
        if sink is not None:
            den = den + jnp.exp(sink - m)
        outs.append(o[:, :HEAD_DIM] / den)
    return outs


def _ctx_attn_kernel(sink_ref, z_ref, gq_ref, gk_ref, o_ref, bk_ref):
    z = z_ref[...]
    bq = _head_rms(z[:, 512:768], gq_ref[...])
    bk = _head_rms(z[:, 768:896], gk_ref[...])
    bk_ref[...] = bk
    groups = ((z[:, 0:256], z[:, 256:384], z[:, 384:512], True),
              (bq, bk, z[:, 896:1024], False))
    outs = []
    for q_all, k_all, v_all, use_sink in groups:
        k_all = k_all.astype(BF16)
        v_all = _with_ones(v_all)
        for hq in range(A_HEADS):
            kv = hq // (A_HEADS // A_KV_HEADS)
            outs.append(dict(q=q_all[:, hq * HEAD_DIM:(hq + 1) * HEAD_DIM],
                             k=k_all[:, kv * HEAD_DIM:(kv + 1) * HEAD_DIM],
                             v=v_all[:, kv * 2 * HEAD_DIM:(kv + 1) * 2 * HEAD_DIM],
                             sink=sink_ref[hq] if use_sink else None))
    o_ref[...] = jnp.concatenate(_attend_heads(outs), axis=-1).astype(o_ref.dtype)


def ctx_attention(z, sink, gq, gk):
    return pl.pallas_call(
        _ctx_attn_kernel,
        grid=(BATCH,),
        in_specs=[pl.BlockSpec(memory_space=pltpu.SMEM),
                  pl.BlockSpec((SEQ, 1024), lambda b: (b, 0)),
                  pl.BlockSpec((1, HEAD_DIM), lambda b: (0, 0)),
                  pl.BlockSpec((1, HEAD_DIM), lambda b: (0, 0))],
        out_specs=[pl.BlockSpec((SEQ, 512), lambda b: (b, 0)),
                   pl.BlockSpec((SEQ, 128), lambda b: (b, 0))],
        out_shape=[jax.ShapeDtypeStruct((CTX_TOK, 512), BF16),
                   jax.ShapeDtypeStruct((CTX_TOK, 128), F32)],
        compiler_params=_params(("arbitrary",)),
        name="ctx_attention",
    )(sink, z, gq.reshape(1, HEAD_DIM), gk.reshape(1, HEAD_DIM))


LB_TQ = 256


def _lat_b_kernel(zq_ref, zkv_ref, ck_ref, cv_ref, cq_ref, sq_ref, ckk_ref, skk_ref, gq_ref, gk_ref,
                  o_ref, k_scr, v_scr):
    L = zkv_ref.shape[0]

    @pl.when(pl.program_id(1) == 0)
    def _():
        kv = zkv_ref[...]
        bk = _rope_apply(_head_rms(kv[:, :128], gk_ref[...]), ckk_ref[...], skk_ref[...])
        k_scr[0:L, :] = bk.astype(BF16)
        k_scr[L:L + PAST_LEN, :] = ck_ref[...].astype(BF16)
        v_scr[0:L, :] = _with_ones(kv[:, 128:])
        v_scr[L:L + PAST_LEN, :] = _with_ones(cv_ref[...])

    q = _rope_apply(_head_rms(zq_ref[...], gq_ref[...]), cq_ref[...], sq_ref[...])
    outs = []
    for hq in range(B_HEADS):
        kv = hq // (B_HEADS // B_KV_HEADS)
        sl = slice(kv * HEAD_DIM, (kv + 1) * HEAD_DIM)
        outs.append(dict(q=q[:, hq * HEAD_DIM:(hq + 1) * HEAD_DIM], k=k_scr[:, sl],
                         v=v_scr[:, kv * 2 * HEAD_DIM:(kv + 1) * 2 * HEAD_DIM]))
    o_ref[...] = jnp.concatenate(_attend_heads(outs), axis=-1).astype(o_ref.dtype)


def latent_attention_b(z, row0, nb, L, cache_k, cache_v, rope_c, rope_s, gq, gk):
    nq = L // LB_TQ
    return pl.pallas_call(
        _lat_b_kernel,
        grid=(nb, nq),
        in_specs=[pl.BlockSpec((LB_TQ, 256), lambda b, i: (row0 // LB_TQ + b * nq + i, 2)),
                  pl.BlockSpec((L, 256), lambda b, i: (row0 // L + b, 3)),
                  pl.BlockSpec((None, PAST_LEN, 128), lambda b, i: (b, 0, 0)),
                  pl.BlockSpec((None, PAST_LEN, 128), lambda b, i: (b, 0, 0)),
                  pl.BlockSpec((LB_TQ, 256), lambda b, i: (i, 0)),
                  pl.BlockSpec((LB_TQ, 256), lambda b, i: (i, 0)),
                  pl.BlockSpec((L, 128), lambda b, i: (0, 0)),
                  pl.BlockSpec((L, 128), lambda b, i: (0, 0)),
                  pl.BlockSpec((1, HEAD_DIM), lambda b, i: (0, 0)),
                  pl.BlockSpec((1, HEAD_DIM), lambda b, i: (0, 0))],
        out_specs=pl.BlockSpec((LB_TQ, 256), lambda b, i: (b * nq + i, 0)),
        out_shape=jax.ShapeDtypeStruct((nb * L, 256), BF16),
        scratch_shapes=[pltpu.VMEM((L + PAST_LEN, 128), BF16),
                        pltpu.VMEM((L + PAST_LEN, 256), BF16)],
        compiler_params=_params(("arbitrary", "arbitrary")),
        name="latent_attention_b",
    )(z, z, cache_k, cache_v, rope_c, rope_s, rope_c, rope_s, gq.reshape(1, HEAD_DIM), gk.reshape(1, HEAD_DIM))


def _lat_a_kernel(sink_ref, zq_ref, zkv_ref, ck_ref, cv_ref, cq_ref, sq_ref, ckk_ref, skk_ref,
                  o_ref, k_scr, v_scr, ck_scr, cv_scr):
    L = zkv_ref.shape[0]
    i = pl.program_id(1)

    @pl.when(i == 0)
    def _():
        kv = zkv_ref[...]
        k_scr[0:BLOCK, :] = jnp.zeros((BLOCK, 128), BF16)
        v_scr[0:BLOCK, :] = jnp.zeros((BLOCK, 256), BF16)
        k_scr[BLOCK:BLOCK + L, :] = _rope_apply(kv[:, :128], ckk_ref[...], skk_ref[...]).astype(BF16)
        v_scr[BLOCK:BLOCK + L, :] = _with_ones(kv[:, 128:])
        k_scr[BLOCK + L:2 * BLOCK + L, :] = jnp.zeros((BLOCK, 128), BF16)
        v_scr[BLOCK + L:2 * BLOCK + L, :] = jnp.zeros((BLOCK, 256), BF16)
        ck_scr[...] = ck_ref[...].astype(BF16)
        cv_scr[...] = _with_ones(cv_ref[...])

    q = _rope_apply(zq_ref[...], cq_ref[...], sq_ref[...])
    start = pl.multiple_of(i * BLOCK, BLOCK)
    kband = k_scr[pl.ds(start, 3 * BLOCK), :]
    vband = v_scr[pl.ds(start, 3 * BLOCK), :]
    r = lax.broadcasted_iota(jnp.int32, (BLOCK, 3 * BLOCK), 0)
    cidx = lax.broadcasted_iota(jnp.int32, (BLOCK, 3 * BLOCK), 1)
    kpos = i * BLOCK - BLOCK + cidx
    mask = (jnp.abs(cidx - BLOCK - r) <= WINDOW) & (kpos >= 0) & (kpos < L)
    outs = []
    for hq in range(A_HEADS):
        kv = hq // (A_HEADS // A_KV_HEADS)
        sl = slice(kv * HEAD_DIM, (kv + 1) * HEAD_DIM)
        sv = slice(kv * 2 * HEAD_DIM, (kv + 1) * 2 * HEAD_DIM)
        outs.append(dict(q=q[:, hq * HEAD_DIM:(hq + 1) * HEAD_DIM], k=kband[:, sl], v=vband[:, sv],
                         extra=(ck_scr[:, sl], cv_scr[:, sv]), sink=sink_ref[hq], mask=mask))
    o_ref[...] = jnp.concatenate(_attend_heads(outs), axis=-1).astype(o_ref.dtype)


def latent_attention_a(z, row0, nb, L, sink, cache_k, cache_v, rope_c, rope_s):
    nq = L // BLOCK
    return pl.pallas_call(
        _lat_a_kernel,
        grid=(nb, nq),
        in_specs=[pl.BlockSpec(memory_space=pltpu.SMEM),
                  pl.BlockSpec((BLOCK, 256), lambda b, i: (row0 // BLOCK + b * nq + i, 0)),
                  pl.BlockSpec((L, 256), lambda b, i: (row0 // L + b, 1)),
                  pl.BlockSpec((None, PAST_LEN, 128), lambda b, i: (b, 0, 0)),
                  pl.BlockSpec((None, PAST_LEN, 128), lambda b, i: (b, 0, 0)),
                  pl.BlockSpec((BLOCK, 256), lambda b, i: (i, 0)),
                  pl.BlockSpec((BLOCK, 256), lambda b, i: (i, 0)),
                  pl.BlockSpec((L, 128), lambda b, i: (0, 0)),
                  pl.BlockSpec((L, 128), lambda b, i: (0, 0))],
        out_specs=pl.BlockSpec((BLOCK, 256), lambda b, i: (b * nq + i, 0)),
        out_shape=jax.ShapeDtypeStruct((nb * L, 256), BF16),
        scratch_shapes=[pltpu.VMEM((L + 2 * BLOCK, 128), BF16),
                        pltpu.VMEM((L + 2 * BLOCK, 256), BF16),
                        pltpu.VMEM((PAST_LEN, 128), BF16),
                        pltpu.VMEM((PAST_LEN, 256), BF16)],
        compiler_params=_params(("arbitrary", "arbitrary")),
        name="latent_attention_a",
    )(sink, z, z, cache_k, cache_v, rope_c, rope_s, rope_c, rope_s)


_TN = (((0,), (0,)), ((), ()))
HG_GROUP = 4


def _hgrn_kernel(zq_ref, zf_ref, zi_ref, zg_ref, lb_ref, gn_ref, s0_ref, o_ref, sT_ref,
                 of_scr, ob_scr, g_scr, k_scr, qin_scr, kin_scr, v_scr, S_scr, *, tt):
    d = pl.program_id(1)
    j = pl.program_id(2)
    n_t = pl.num_programs(2)
    C = HGRN_CHUNK
    n_c = tt // C

    @pl.when(j == 0)
    def _():
        S_scr[...] = s0_ref[...]

    lb = lb_ref[...]
    sg = jax.nn.sigmoid(zf_ref[...])
    logf = jnp.log(lb + (1.0 - lb) * sg)
    k = (1.0 - lb) * (1.0 - sg)
    k_scr[...] = k
    v_scr[...] = zi_ref[...].astype(BF16)
    in_chunk = lax.broadcasted_iota(jnp.int32, (tt, C_HEADS * HEAD_DIM), 0) % C
    row = lax.broadcasted_iota(jnp.int32, (C, C), 0)
    col = lax.broadcasted_iota(jnp.int32, (C, C), 1)
    heads = [slice(h * HEAD_DIM, (h + 1) * HEAD_DIM) for h in range(C_HEADS)]

    def run(reverse, tile):
        G = logf
        step = 1
        while step < C:
            if reverse:
                G = G + jnp.where(in_chunk < C - step, pltpu.roll(G, tt - step, 0), 0.0)
            else:
                G = G + jnp.where(in_chunk >= step, pltpu.roll(G, step, 0), 0.0)
            step *= 2
        g_scr[...] = G
        qin_scr[...] = (zq_ref[...] * jnp.exp(G)).astype(BF16)
        kin_scr[...] = (k * jnp.exp(-G)).astype(BF16)
        tri = (row <= col) if reverse else (row >= col)

        def body(gi, carry):
            chunks = []
            for g in range(HG_GROUP):
                ci = gi * HG_GROUP + g
                c = (n_c - 1 - ci) if reverse else ci
                r0 = pl.multiple_of(c * C, C)
                rows = pl.ds(r0, C)
                G_c = g_scr[rows, :]
                G_end = G_c[0:1, :] if reverse else G_c[C - 1:C, :]
                vc = v_scr[rows, :]
                q_in = qin_scr[rows, :]
                k_in = kin_scr[rows, :]
                k_out = (k_scr[rows, :] * jnp.exp(G_end - G_c)).astype(BF16)
                chunks.append(dict(
                    r0=r0, rows=rows, vc=vc, q_in=q_in, decay=jnp.exp(G_end),
                    a=[lax.dot_general(q_in[:, sl], k_in[:, sl], _NT, preferred_element_type=F32) for sl in heads],
                    kv=[lax.dot_general(vc[:, sl], k_out[:, sl], _TN, preferred_element_type=F32) for sl in heads]))
            s_cur = [S_scr[h] for h in range(C_HEADS)]
            for ch in chunks:
                ch['qs'] = [lax.dot_general(ch['q_in'][:, sl], s_cur[h].astype(BF16), _NT,
                                            preferred_element_type=F32) for h, sl in enumerate(heads)]
                s_cur = [s_cur[h] * ch['decay'][:, sl] + ch['kv'][h] for h, sl in enumerate(heads)]
            for h in range(C_HEADS):
                S_scr[h] = s_cur[h]
            for ch in chunks:
                o_c = jnp.concatenate(
                    [jnp.dot(jnp.where(tri, ch['a'][h], 0.0).astype(BF16), ch['vc'][:, sl],
                             preferred_element_type=F32) + ch['qs'][h] for h, sl in enumerate(heads)], axis=-1)
                if reverse:
                    ob_scr[ch['rows'], :] = o_c
                else:
                    of_scr[pl.ds(pl.multiple_of(tile * tt, tt) + ch['r0'], C), :] = o_c
            return carry

        lax.fori_loop(0, n_c // HG_GROUP, body, 0)

    @pl.when(d == 0)
    def _():
        run(False, j)

    @pl.when(d == 1)
    def _():
        tile = n_t - 1 - j
        run(True, tile)
        o = of_scr[pl.ds(pl.multiple_of(tile * tt, tt), tt), :] + ob_scr[...]
        g = zg_ref[...]
        o_ref[...] = (_head_rms(o, gn_ref[...]) * (g * jax.nn.sigmoid(g))).astype(o_ref.dtype)

    @pl.when(j == n_t - 1)
    def _():
        sT_ref[...] = S_scr[...]


def hgrn_mixer(z, row0, nb, L, lb, gn, s0_t):
    tt = min(L, 512)
    n_t = L // tt
    rb = row0 // tt

    def tile(d, j):
        return jnp.where(d == 0, j, n_t - 1 - j)

    def late(d, j):
        return jnp.where(d == 0, n_t - 1, n_t - 1 - j)

    st_spec = pl.BlockSpec((None, None, C_HEADS, HEAD_DIM, HEAD_DIM), lambda b, d, j: (b, d, 0, 0, 0))
    return pl.pallas_call(
        functools.partial(_hgrn_kernel, tt=tt),
        grid=(nb, 2, n_t),
        in_specs=[pl.BlockSpec((tt, 256), lambda b, d, j: (rb + b * n_t + tile(d, j), 4)),
                  pl.BlockSpec((tt, 256), lambda b, d, j: (rb + b * n_t + tile(d, j), 5 + d)),
                  pl.BlockSpec((tt, 256), lambda b, d, j: (rb + b * n_t + tile(d, j), 7)),
                  pl.BlockSpec((tt, 256), lambda b, d, j: (rb + b * n_t + late(d, j), 8)),
                  pl.BlockSpec((1, 256), lambda b, d, j: (0, 0)),
                  pl.BlockSpec((1, HEAD_DIM), lambda b, d, j: (0, 0)),
                  st_spec],
        out_specs=[pl.BlockSpec((tt, 256), lambda b, d, j: (b * n_t + late(d, j), 0)),
                   st_spec],
        out_shape=[jax.ShapeDtypeStruct((nb * L, 256), BF16),
                   jax.ShapeDtypeStruct((nb, 2, C_HEADS, HEAD_DIM, HEAD_DIM), F32)],
        scratch_shapes=[pltpu.VMEM((L, 256), F32),
                        pltpu.VMEM((tt, 256), F32),
                        pltpu.VMEM((tt, 256), F32),
                        pltpu.VMEM((tt, 256), F32),
                        pltpu.VMEM((tt, 256), BF16),
                        pltpu.VMEM((tt, 256), BF16),
                        pltpu.VMEM((tt, 256), BF16),
                        pltpu.VMEM((C_HEADS, HEAD_DIM, HEAD_DIM), F32)],
        compiler_params=_params(("arbitrary", "arbitrary", "arbitrary")),
        name="hgrn_mixer",
    )(z, z, z, z, lb.reshape(1, 256), gn.reshape(1, HEAD_DIM), s0_t)


DL_C = DELTA_CHUNK
DL_PREP_TT = 256
DL_HALO = 8
DL_CHUNK_TT = 512
N_QKV_HEADS = 3 * D_HEADS
DL_HL_LEVELS = 3


def _delta_prep_kernel(x_ref, xp_ref, xn_ref, zab_ref, cw_ref, na_ref, dtb_ref, qkv_ref, gate_ref, xs_scr):
    tt = x_ref.shape[0]
    row = pl.program_id(0) * tt
    lat = row - CTX_TOK
    first = jnp.where(row < CTX_TOK, True, lat % DEC_SEQ == 0)
    last = jnp.where(row < CTX_TOK, True, (lat + tt) % DEC_SEQ == 0)
    xs_scr[DL_HALO:DL_HALO + tt, :] = x_ref[...]
    xs_scr[0:DL_HALO, :] = jnp.where(first, 0.0, xp_ref[...])
    xs_scr[DL_HALO + tt:2 * DL_HALO + tt, :] = jnp.where(last, 0.0, xn_ref[...])
    pad = (CONV_K - 1) // 2
    y = None
    for t in range(CONV_K):
        term = xs_scr[pl.ds(DL_HALO - pad + t, tt), :] * cw_ref[t:t + 1, :]
        y = term if y is None else y + term
    y = y * jax.nn.sigmoid(y)
    for idx in range(N_QKV_HEADS):
        xh = y[:, idx * HEAD_DIM:(idx + 1) * HEAD_DIM]
        if idx < 2 * D_HEADS:
            xh = xh * lax.rsqrt(jnp.sum(xh * xh, axis=-1, keepdims=True) + EPS)
        if idx < D_HEADS:
            xh = xh * ATT_SCALE
        qkv_ref[idx] = xh
    zab = zab_ref[...]
    lane = lax.broadcasted_iota(jnp.int32, zab.shape, 1)
    t_ = zab + dtb_ref[...]
    softplus = jnp.maximum(t_, 0.0) + jnp.log(1.0 + jnp.exp(-jnp.abs(t_)))
    gate_ref[...] = jnp.where(lane < 2 * D_HEADS, na_ref[...] * softplus, jax.nn.sigmoid(zab))


def delta_prep(z, conv_w, a_log, dt_bias):
    tt = DL_PREP_TT
    hb = tt // DL_HALO
    n_hb = N_TOK // DL_HALO
    pad8 = lambda v: jnp.concatenate([v.reshape(1, 2 * D_HEADS), jnp.zeros((1, 128 - 2 * D_HEADS), F32)], axis=1)
    return pl.pallas_call(
        _delta_prep_kernel,
        grid=(N_TOK // tt,),
        in_specs=[pl.BlockSpec((tt, 768), lambda i: (i, 3)),
                  pl.BlockSpec((DL_HALO, 768), lambda i: (jnp.maximum(i * hb - 1, 0), 3)),
                  pl.BlockSpec((DL_HALO, 768), lambda i: (jnp.minimum((i + 1) * hb, n_hb - 1), 3)),
                  pl.BlockSpec((tt, 128), lambda i: (i, Z_DAB // 128)),
                  pl.BlockSpec((CONV_K, 768), lambda i: (0, 0)),
                  pl.BlockSpec((1, 128), lambda i: (0, 0)),
                  pl.BlockSpec((1, 128), lambda i: (0, 0))],
        out_specs=[pl.BlockSpec((N_QKV_HEADS, tt, HEAD_DIM), lambda i: (0, i, 0)),
                   pl.BlockSpec((tt, 128), lambda i: (i, 0))],
        out_shape=[jax.ShapeDtypeStruct((N_QKV_HEADS, N_TOK, HEAD_DIM), F32),
                   jax.ShapeDtypeStruct((N_TOK, 128), F32)],
        scratch_shapes=[pltpu.VMEM((tt + 2 * DL_HALO, 768), F32)],
        compiler_params=_params(("arbitrary",)),
        name="delta_prep",
    )(z, z, z, z, conv_w, pad8(-jnp.exp(a_log)), pad8(dt_bias))


def _split_bf16(a):
    hi = a.astype(BF16)
    return hi, (a - hi.astype(F32)).astype(BF16)


def _dot_hl(a, b_parts):
    a_hi, a_lo = _split_bf16(a)
    b_hi, b_lo = b_parts
    m = a.shape[0]
    r = jnp.dot(jnp.concatenate([a_hi, a_lo], axis=0), b_hi, preferred_element_type=F32)
    return r[:m] + r[m:] + jnp.dot(a_hi, b_lo, preferred_element_type=F32)


def _delta_chunk_kernel(qkv_ref, gate_ref, u2_ref, wq_ref, ak_ref):
    C = DL_C
    n_c = gate_ref.shape[0] // C
    row = lax.broadcasted_iota(jnp.int32, (C, C), 0)
    col = lax.broadcasted_iota(jnp.int32, (C, C), 1)
    eye = (row == col).astype(F32)

    def body(c, carry):
        r0 = pl.multiple_of(c * C, C)
        ga = gate_ref[pl.ds(r0, C), :]
        chains = []
        for d in range(2):
            incl = (row >= col) if d == 0 else (row <= col)
            strict = (row > col) if d == 0 else (row < col)
            g_all = jnp.dot(incl.astype(F32), ga, precision=lax.Precision.HIGHEST, preferred_element_type=F32)
            g_all_t = g_all.T
            for h in range(D_HEADS):
                ci = d * D_HEADS + h
                q = qkv_ref[h, pl.ds(r0, C), :]
                k = qkv_ref[D_HEADS + h, pl.ds(r0, C), :]
                v = qkv_ref[2 * D_HEADS + h, pl.ds(r0, C), :]
                g_col = g_all[:, ci:ci + 1]
                g_row = g_all_t[ci:ci + 1, :]
                beta = ga[:, 2 * D_HEADS + ci:2 * D_HEADS + ci + 1]
                g_end = g_col[C - 1:C, :] if d == 0 else g_col[0:1, :]
                kb = k * beta
                eg = jnp.exp(g_col)
                decay = jnp.where(incl, jnp.exp(jnp.where(incl, g_col - g_row, 0.0)), 0.0)
                kq = jnp.concatenate([kb, q], axis=0).astype(BF16)
                chains.append(dict(
                    strict=strict, decay=decay, qg=q * eg, g_end=g_end,
                    r=lax.dot_general(kq, k.astype(BF16), _NT, preferred_element_type=F32),
                    rhs=jnp.concatenate([v * beta, kb * eg], axis=1).astype(BF16),
                    ke_t=(k * jnp.exp(g_end - g_col)).T))
        for ch in chains:
            ch['p'] = -jnp.where(ch['strict'], ch['r'][:C] * ch['decay'], 0.0)
            ch['t'] = eye + ch['p']
        for level in range(5):
            if level < DL_HL_LEVELS:
                for ch in chains:
                    ch['p'] = _dot_hl(ch['p'], _split_bf16(ch['p']))
                for ch in chains:
                    ch['t'] = ch['t'] + _dot_hl(ch['t'], _split_bf16(ch['p']))
            else:
                for ch in chains:
                    pb = ch['p'].astype(BF16)
                    ch['p'] = jnp.dot(pb, pb, preferred_element_type=F32)
                for ch in chains:
                    ch['t'] = ch['t'] + jnp.dot(ch['t'].astype(BF16), ch['p'].astype(BF16),
                                                preferred_element_type=F32)
        for ch in chains:
            ch['uw'] = jnp.dot(ch['t'].astype(BF16), ch['rhs'], preferred_element_type=F32)
        u2 = [jnp.concatenate([ch['uw'][:, :C], jnp.broadcast_to(jnp.exp(ch['g_end']), (C, C))], axis=1)
              for ch in chains]
        wq = [jnp.concatenate([ch['uw'][:, C:], ch['qg']], axis=0).astype(BF16) for ch in chains]
        ak = [jnp.concatenate([ch['r'][C:] * ch['decay'], ch['ke_t']], axis=0).astype(BF16) for ch in chains]
        pack = lambda xs: jnp.stack(xs).reshape((2, D_HEADS) + xs[0].shape)
        u2_ref[:, :, pl.ds(r0, C), :] = pack(u2)
        wq_ref[:, :, c] = pack(wq)
        ak_ref[:, :, c] = pack(ak)
        return carry

    lax.fori_loop(0, n_c, body, 0)


def delta_chunks(qkv, gates):
    tt = DL_CHUNK_TT
    n_c = tt // DL_C
    return pl.pallas_call(
        _delta_chunk_kernel,
        grid=(N_TOK // tt,),
        in_specs=[pl.BlockSpec((N_QKV_HEADS, tt, HEAD_DIM), lambda i: (0, i, 0)),
                  pl.BlockSpec((tt, 128), lambda i: (i, 0))],
        out_specs=[pl.BlockSpec((2, D_HEADS, tt, 128), lambda i: (0, 0, i, 0)),
                   pl.BlockSpec((2, D_HEADS, n_c, 2 * DL_C, HEAD_DIM), lambda i: (0, 0, i, 0, 0)),
                   pl.BlockSpec((2, D_HEADS, n_c, 2 * DL_C, HEAD_DIM), lambda i: (0, 0, i, 0, 0))],
        out_shape=[jax.ShapeDtypeStruct((2, D_HEADS, N_TOK, 128), F32),
                   jax.ShapeDtypeStruct((2, D_HEADS, N_TOK // DL_C, 2 * DL_C, HEAD_DIM), BF16),
                   jax.ShapeDtypeStruct((2, D_HEADS, N_TOK // DL_C, 2 * DL_C, HEAD_DIM), BF16)],
        compiler_params=_params(("arbitrary",)),
        name="delta_chunks",
    )(qkv, gates)


def _delta_scan_kernel(u2f_ref, wqf_ref, akf_ref, u2b_ref, wqb_ref, akb_ref, s0_ref,
                       of_ref, ob_ref, s_ref, s_scr):
    j = pl.program_id(1)
    C = DL_C
    n_c = wqf_ref.shape[1]

    @pl.when(j == 0)
    def _():
        s_scr[...] = s0_ref[...]

    def body(ci, carry):
        chains = []
        for d, (u2_ref, wq_ref, ak_ref, o_ref) in enumerate(((u2f_ref, wqf_ref, akf_ref, of_ref),
                                                            (u2b_ref, wqb_ref, akb_ref, ob_ref))):
            c = ci if d == 0 else n_c - 1 - ci
            r0 = pl.multiple_of(c * C, C)
            for h in range(D_HEADS):
                s = s_scr[d, h]
                chains.append(dict(d=d, h=h, c=c, r0=r0, s=s, ak_ref=ak_ref, o_ref=o_ref,
                                   u2=u2_ref[h, pl.ds(r0, C), :],
                                   r1=jnp.dot(wq_ref[h, c], s.astype(BF16),
                                              preferred_element_type=F32)))
        for ch in chains:
            v_new = ch['u2'][:, :C] - ch['r1'][:C]
            ch['r2'] = jnp.dot(ch['ak_ref'][ch['h'], ch['c']], v_new.astype(BF16),
                               preferred_element_type=F32)
        for ch in chains:
            ch['o_ref'][ch['h'], pl.ds(ch['r0'], C), :] = ch['r1'][C:] + ch['r2'][:C]
            s_scr[ch['d'], ch['h']] = ch['s'] * ch['u2'][0:1, C:] + ch['r2'][C:]
        return carry

    lax.fori_loop(0, n_c, body, 0)

    @pl.when(j == pl.num_programs(1) - 1)
    def _():
        s_ref[...] = s_scr[...]


def delta_scan(u2, wq, ak, row0, nb, L, s0):
    tt = min(L, 512)
    n_t = L // tt
    n_c = tt // DL_C
    rb = row0 // tt
    fwd = lambda b, j: rb + b * n_t + j
    bwd = lambda b, j: rb + b * n_t + (n_t - 1 - j)
    u_spec = lambda d, f: pl.BlockSpec((None, D_HEADS, tt, 128), lambda b, j: (d, 0, f(b, j), 0))
    c_spec = lambda d, f: pl.BlockSpec((None, D_HEADS, n_c, 2 * DL_C, HEAD_DIM), lambda b, j: (d, 0, f(b, j), 0, 0))
    st_spec = pl.BlockSpec((None, 2, D_HEADS, HEAD_DIM, HEAD_DIM), lambda b, j: (b, 0, 0, 0, 0))
    return pl.pallas_call(
        _delta_scan_kernel,
        grid=(nb, n_t),
        in_specs=[u_spec(0, fwd), c_spec(0, fwd), c_spec(0, fwd),
                  u_spec(1, bwd), c_spec(1, bwd), c_spec(1, bwd), st_spec],
        out_specs=[pl.BlockSpec((D_HEADS, tt, HEAD_DIM), lambda b, j: (0, b * n_t + j, 0)),
                   pl.BlockSpec((D_HEADS, tt, HEAD_DIM), lambda b, j: (0, b * n_t + (n_t - 1 - j), 0)),
                   st_spec],
        out_shape=[jax.ShapeDtypeStruct((D_HEADS, nb * L, HEAD_DIM), F32),
                   jax.ShapeDtypeStruct((D_HEADS, nb * L, HEAD_DIM), F32),
                   jax.ShapeDtypeStruct((nb, 2, D_HEADS, HEAD_DIM, HEAD_DIM), F32)],
        scratch_shapes=[pltpu.VMEM((2, D_HEADS, HEAD_DIM, HEAD_DIM), F32)],
        compiler_params=_params(("arbitrary", "arbitrary")),
        name="delta_scan",
    )(u2, wq, ak, u2, wq, ak, s0)


def _delta_out_kernel(of_ref, ob_ref, zg_ref, gn_ref, o_ref):
    outs = []
    for h in range(D_HEADS):
        o = of_ref[h] + ob_ref[h]
        outs.append(o * lax.rsqrt(jnp.mean(o * o, axis=-1, keepdims=True) + EPS) * gn_ref[...])
    g = zg_ref[...]
    o_ref[...] = (jnp.concatenate(outs, axis=-1) * (g * jax.nn.sigmoid(g))).astype(o_ref.dtype)


def delta_output(o_f, o_b, z, row0, gn):
    n = o_f.shape[1]
    tt = 256
    return pl.pallas_call(
        _delta_out_kernel,
        grid=(n // tt,),
        in_specs=[pl.BlockSpec((D_HEADS, tt, HEAD_DIM), lambda i: (0, i, 0)),
                  pl.BlockSpec((D_HEADS, tt, HEAD_DIM), lambda i: (0, i, 0)),
                  pl.BlockSpec((tt, 256), lambda i: (row0 // tt + i, Z_DG // 256)),
                  pl.BlockSpec((1, HEAD_DIM), lambda i: (0, 0))],
        out_specs=pl.BlockSpec((tt, 256), lambda i: (i, 0)),
        out_shape=jax.ShapeDtypeStruct((n, 256), BF16),
        compiler_params=_params(("arbitrary",)),
        name="delta_output",
    )(o_f, o_b, z, gn.reshape(1, HEAD_DIM))


def kernel(x_prompt, x_sample, cache_attn_a_k, cache_attn_a_v, cache_attn_b_k, cache_attn_b_v,
           state_hgrn, state_delta, c, c_ctx, norm1_g, norm2_g, w_ada, b_ada, w_in, a_sink,
           b_qnorm_g, b_knorm_g, c_lb, c_onorm_g, d_conv, d_a_log, d_dt_bias, d_onorm_g,
           w_branch, w_out, ffn_w1, ffn_w3, ffn_w2, router_w, router_b, moe_w1, moe_w3, moe_w2,
           final_norm_g):
    cum = jnp.cumsum(jax.nn.softmax(c_lb, axis=0), axis=0)
    lower_bounds = cum - cum[:1]

    x = jnp.concatenate([x_prompt.reshape(CTX_TOK, D_MODEL), x_sample.reshape(LAT_TOK, D_MODEL)], axis=0)
    cond = jnp.concatenate([c_ctx[None, :], c, jnp.zeros((16 - N_COND, D_MODEL), F32)], axis=0)

    rope_c, rope_s = rope_lane_tables(DEC_SEQ)
    caches = []
    for l in range(DEPTH):
        mod = ada_modulation(cond, w_ada[l], b_ada[l])[:N_COND].reshape(N_COND, 6, D_MODEL)
        w_mix = jnp.concatenate([w_in[l][:, :Z_MAIN], w_in[l][:, Z_MAIN + 16:W_IN_MIX],
                                 w_in[l][:, Z_MAIN:Z_MAIN + 16], jnp.zeros((D_MODEL, 128 - 16), F32)],
                                axis=1).astype(BF16)
        w_gl = w_in[l][:, W_IN_MIX:].reshape(D_MODEL, N_BRANCH, D_MODEL).transpose(1, 0, 2).astype(BF16)
        z = input_projection(x, mod, norm1_g[l], w_mix)
        kv2 = lambda t: t.reshape(DEC_BATCH, PAST_LEN, 128)
        o_ab_ctx, bk_ctx = ctx_attention(z, a_sink[l], b_qnorm_g[l], b_knorm_g[l])
        o_a_lat = latent_attention_a(z, CTX_TOK, DEC_BATCH, DEC_SEQ, a_sink[l], kv2(cache_attn_a_k[:, l]),
                                     kv2(cache_attn_a_v[:, l]), rope_c, rope_s)
        o_b_lat = latent_attention_b(z, CTX_TOK, DEC_BATCH, DEC_SEQ, kv2(cache_attn_b_k[:, l]),
                                     kv2(cache_attn_b_v[:, l]), rope_c, rope_s, b_qnorm_g[l], b_knorm_g[l])
        o_c_ctx, sc_t = hgrn_mixer(z, 0, BATCH, SEQ, lower_bounds[l], c_onorm_g[l],
                                   jnp.zeros((BATCH, 2, C_HEADS, HEAD_DIM, HEAD_DIM), F32))
        o_c_lat, _ = hgrn_mixer(z, CTX_TOK, DEC_BATCH, DEC_SEQ, lower_bounds[l], c_onorm_g[l],
                                jnp.swapaxes(state_hgrn[:, l], -1, -2))

        qkv, gates = delta_prep(z, d_conv[l], d_a_log[l], d_dt_bias[l])
        u2, wq, ak = delta_chunks(qkv, gates)
        of_ctx, ob_ctx, sd = delta_scan(u2, wq, ak, 0, BATCH, SEQ,
                                        jnp.zeros((BATCH, 2, D_HEADS, HEAD_DIM, HEAD_DIM), F32))
        of_lat, ob_lat, _ = delta_scan(u2, wq, ak, CTX_TOK, DEC_BATCH, DEC_SEQ, state_delta[:, l])
        o_d_ctx = delta_output(of_ctx, ob_ctx, z, 0, d_onorm_g[l])
        o_d_lat = delta_output(of_lat, ob_lat, z, CTX_TOK, d_onorm_g[l])
        kvh = lambda t: t.reshape(BATCH, SEQ, 2, HEAD_DIM)
        caches.append((kvh(z[:CTX_TOK, 256:384]), kvh(z[:CTX_TOK, 384:512]), kvh(bk_ctx), kvh(z[:CTX_TOK, 896:1024]),
                       jnp.swapaxes(sc_t, -1, -2), sd))
        x = merge_projection(x, mod, norm1_g[l], (o_ab_ctx, o_c_ctx, o_d_ctx), (o_a_lat, o_b_lat, o_c_lat, o_d_lat),
                             w_gl, w_branch[l].astype(BF16), w_out[l].astype(BF16))
        j = l // 2
        if l % 2 == 0:
            x = dense_ffn(x, mod, norm2_g[l], ffn_w1[j].astype(BF16), ffn_w3[j].astype(BF16),
                          ffn_w2[j].astype(BF16))
        else:
            assert l == DEPTH - 1, "the expert layer's residual is fused with the final norm"
            rw = jnp.concatenate([router_w[j], jnp.zeros((D_MODEL, 128 - N_EXPERTS), F32)], axis=1)
            rb = jnp.concatenate([router_b[j], jnp.zeros((128 - N_EXPERTS,), F32)])[None, :]
            h2, pos, wgt = moe_router(x, mod, norm2_g[l], rw, rb)
            f = moe_experts(h2, pos, wgt, moe_w1[j].astype(BF16), moe_w3[j].astype(BF16), moe_w2[j].astype(BF16))
            y_prompt = residual_final_norm(x, f, mod, final_norm_g, 0, CTX_TOK).reshape(BATCH, SEQ, D_MODEL)
            y_sample = residual_final_norm(x, f, mod, final_norm_g, CTX_TOK, LAT_TOK).reshape(DEC_BATCH, DEC_SEQ, D_MODEL)

    stack = lambda idx: jnp.stack([caches[l][idx] for l in range(DEPTH)], axis=1)
    return (y_prompt, y_sample, stack(0), stack(1), stack(2), stack(3), stack(4), stack(5))
```

```python
import functools

import jax
import jax.numpy as jnp
import numpy as np
from jax import lax
from jax.experimental import pallas as pl
from jax.experimental.pallas import tpu as pltpu

F32 = jnp.float32
BF16 = jnp.bfloat16

D_MODEL = 1024
BATCH = 32
SEQ = 256
DEPTH = 2
DEC_BATCH = 8
DEC_SEQ = 4096
PAST_LEN = 256
GRID_W = 64
HEAD_DIM = 64
A_HEADS = 4
A_KV_HEADS = 2
B_HEADS = 4
B_KV_HEADS = 2
C_HEADS = 4
D_HEADS = 4
BRANCH_W = 256
N_BRANCH = 4
WINDOW = 128
BLOCK = 128
ROPE_BASE = 10000.0
HGRN_CHUNK = 32
DELTA_CHUNK = 64
CONV_K = 5
D_FF = 2816
N_EXPERTS = 8
D_FF_EXPERT = 3584
EPS = 1e-6
NEG_INF = -1e30
F32_MIN = float(np.finfo(np.float32).min)

CTX_TOK = BATCH * SEQ
LAT_TOK = DEC_BATCH * DEC_SEQ
N_TOK = CTX_TOK + LAT_TOK
N_COND = 1 + DEC_BATCH

Z_MAIN = 3072
Z_DG = Z_MAIN
Z_DAB = Z_DG + BRANCH_W
Z_COLS = Z_DAB + 128
W_IN_MIX = 3344

TM = 512
VMEM_LIMIT = 56 * 1024 * 1024


def _tile_cond(i, tm):
    ctx_tiles = CTX_TOK // tm
    per_b = DEC_SEQ // tm
    return jnp.where(i < ctx_tiles, 0, 1 + (i - ctx_tiles) // per_b)


def _rms(x, g):
    return x * lax.rsqrt(jnp.mean(x * x, axis=-1, keepdims=True) + EPS) * g


def _params(sem):
    return pltpu.CompilerParams(dimension_semantics=sem, vmem_limit_bytes=VMEM_LIMIT)


def _ada_kernel(c_ref, w_ref, b_ref, o_ref):
    c = c_ref[...]
    s = c * jax.nn.sigmoid(c)
    o_ref[...] = jnp.dot(s.astype(BF16), w_ref[...].astype(BF16), preferred_element_type=F32) + b_ref[...]


def ada_modulation(cond_pad, w, b):
    n = 6 * D_MODEL
    tn = 1536
    return pl.pallas_call(
        _ada_kernel,
        grid=(n // tn,),
        in_specs=[pl.BlockSpec((16, D_MODEL), lambda j: (0, 0)),
                  pl.BlockSpec((D_MODEL, tn), lambda j: (0, j)),
                  pl.BlockSpec((1, tn), lambda j: (0, j))],
        out_specs=pl.BlockSpec((16, tn), lambda j: (0, j)),
        out_shape=jax.ShapeDtypeStruct((16, n), F32),
        compiler_params=_params(("arbitrary",)),
        name="ada_modulation",
    )(cond_pad, w, b.reshape(1, n))


def _in_kernel(x_ref, mod_ref, g_ref, w_ref, z_ref):
    h = _rms(x_ref[...], g_ref[...]) * (1.0 + mod_ref[1:2, :]) + mod_ref[0:1, :]
    z_ref[...] = jnp.dot(h.astype(BF16), w_ref[...], preferred_element_type=F32)


def input_projection(x, mod, g, w):
    nt = N_TOK // TM
    return pl.pallas_call(
        _in_kernel,
        grid=(nt,),
        in_specs=[pl.BlockSpec((TM, D_MODEL), lambda i: (i, 0)),
                  pl.BlockSpec((None, 6, D_MODEL), lambda i: (_tile_cond(i, TM), 0, 0)),
                  pl.BlockSpec((1, D_MODEL), lambda i: (0, 0)),
                  pl.BlockSpec((D_MODEL, Z_COLS), lambda i: (0, 0))],
        out_specs=pl.BlockSpec((TM, Z_COLS), lambda i: (i, 0)),
        out_shape=jax.ShapeDtypeStruct((N_TOK, Z_COLS), F32),
        compiler_params=_params(("arbitrary",)),
        name="input_projection",
    )(x, mod, g.reshape(1, D_MODEL), w)


def _merge_kernel(x_ref, mod_ref, g_ref, ab_c_ref, c_c_ref, d_c_ref, a_l_ref, b_l_ref, c_l_ref, d_l_ref,
                  wgl_ref, wbr_ref, wout_ref, xo_ref):
    x = x_ref[...]
    h = (_rms(x, g_ref[...]) * (1.0 + mod_ref[1:2, :]) + mod_ref[0:1, :]).astype(BF16)
    is_ctx = pl.program_id(0) < CTX_TOK // TM
    branches = (jnp.where(is_ctx, ab_c_ref[:, :BRANCH_W], a_l_ref[...]),
                jnp.where(is_ctx, ab_c_ref[:, BRANCH_W:], b_l_ref[...]),
                jnp.where(is_ctx, c_c_ref[...], c_l_ref[...]),
                jnp.where(is_ctx, d_c_ref[...], d_l_ref[...]))
    merged = None
    for j in range(N_BRANCH):
        gate = jax.nn.sigmoid(jnp.dot(h, wgl_ref[j], preferred_element_type=F32))
        br = jnp.dot(branches[j], wbr_ref[j], preferred_element_type=F32)
        merged = gate * br if merged is None else merged + gate * br
    mix = jnp.dot(merged.astype(BF16), wout_ref[...], preferred_element_type=F32)
    xo_ref[...] = x + mod_ref[2:3, :] * mix


def merge_projection(x, mod, g, o_ctx, o_lat, wgl, wbr, wout):
    nt = N_TOK // TM
    ctx_tiles = CTX_TOK // TM
    ctx_spec = lambda w: pl.BlockSpec((TM, w), lambda i: (jnp.minimum(i, ctx_tiles - 1), 0))
    lat_spec = pl.BlockSpec((TM, BRANCH_W), lambda i: (jnp.maximum(i - ctx_tiles, 0), 0))
    return pl.pallas_call(
        _merge_kernel,
        grid=(nt,),
        in_specs=[pl.BlockSpec((TM, D_MODEL), lambda i: (i, 0)),
                  pl.BlockSpec((None, 6, D_MODEL), lambda i: (_tile_cond(i, TM), 0, 0)),
                  pl.BlockSpec((1, D_MODEL), lambda i: (0, 0)),
                  ctx_spec(2 * BRANCH_W), ctx_spec(BRANCH_W), ctx_spec(BRANCH_W),
                  lat_spec, lat_spec, lat_spec, lat_spec,
                  pl.BlockSpec((N_BRANCH, D_MODEL, D_MODEL), lambda i: (0, 0, 0)),
                  pl.BlockSpec((N_BRANCH, BRANCH_W, D_MODEL), lambda i: (0, 0, 0)),
                  pl.BlockSpec((D_MODEL, D_MODEL), lambda i: (0, 0))],
        out_specs=pl.BlockSpec((TM, D_MODEL), lambda i: (i, 0)),
        out_shape=jax.ShapeDtypeStruct((N_TOK, D_MODEL), F32),
        compiler_params=_params(("arbitrary",)),
        name="merge_projection",
    )(x, mod, g.reshape(1, D_MODEL), *o_ctx, *o_lat, wgl, wbr, wout)


def _ffn_kernel(x_ref, mod_ref, g_ref, w1_ref, w3_ref, w2_ref, xo_ref):
    x = x_ref[...]
    h = (_rms(x, g_ref[...]) * (1.0 + mod_ref[4:5, :]) + mod_ref[3:4, :]).astype(BF16)
    a = jnp.dot(h, w1_ref[...], preferred_element_type=F32)
    b = jnp.dot(h, w3_ref[...], preferred_element_type=F32)
    hid = (a * jax.nn.sigmoid(a) * b).astype(BF16)
    f = jnp.dot(hid, w2_ref[...], preferred_element_type=F32)
    xo_ref[...] = x + mod_ref[5:6, :] * f


def dense_ffn(x, mod, g, w1, w3, w2):
    nt = N_TOK // TM
    const = lambda i: (0, 0)
    return pl.pallas_call(
        _ffn_kernel,
        grid=(nt,),
        in_specs=[pl.BlockSpec((TM, D_MODEL), lambda i: (i, 0)),
                  pl.BlockSpec((None, 6, D_MODEL), lambda i: (_tile_cond(i, TM), 0, 0)),
                  pl.BlockSpec((1, D_MODEL), const),
                  pl.BlockSpec((D_MODEL, D_FF), const, pipeline_mode=pl.Buffered(1)),
                  pl.BlockSpec((D_MODEL, D_FF), const, pipeline_mode=pl.Buffered(1)),
                  pl.BlockSpec((D_FF, D_MODEL), const, pipeline_mode=pl.Buffered(1))],
        out_specs=pl.BlockSpec((TM, D_MODEL), lambda i: (i, 0)),
        out_shape=jax.ShapeDtypeStruct((N_TOK, D_MODEL), F32),
        compiler_params=_params(("arbitrary",)),
        name="dense_ffn",
    )(x, mod, g.reshape(1, D_MODEL), w1, w3, w2)


MOE_T = 1024
MOE_R = 128
MOE_F = 1792


def _router_kernel(x_ref, mod_ref, g_ref, rw_ref, rb_ref, h_ref, pos_ref, wgt_ref):
    t = x_ref.shape[0]
    h = _rms(x_ref[...], g_ref[...]) * (1.0 + mod_ref[4:5, :]) + mod_ref[3:4, :]
    h_ref[...] = h.astype(BF16)
    logits = jnp.dot(h, rw_ref[...], preferred_element_type=F32, precision=lax.Precision.HIGHEST) + rb_ref[...]
    lt = logits.T[:N_EXPERTS, :]
    eidx = lax.broadcasted_iota(jnp.int32, lt.shape, 0)
    m1 = jnp.max(lt, axis=0, keepdims=True)
    i1 = jnp.min(jnp.where(lt == m1, eidx, N_EXPERTS), axis=0, keepdims=True)
    rest = jnp.where(eidx == i1, F32_MIN, lt)
    m2 = jnp.max(rest, axis=0, keepdims=True)
    i2 = jnp.min(jnp.where(rest == m2, eidx, N_EXPERTS), axis=0, keepdims=True)
    e2 = jnp.exp(m2 - m1)
    p1 = 1.0 / (1.0 + e2)
    p2 = e2 / (1.0 + e2)
    wgt_ref[...] = jnp.where(eidx == i1, p1, 0.0) + jnp.where(eidx == i2, p2, 0.0)
    routed = jnp.where(eidx == i1, 1.0, jnp.where(eidx == i2, 1.0, 0.0))
    s_id = lax.broadcasted_iota(jnp.int32, (t, t), 0)
    t_id = lax.broadcasted_iota(jnp.int32, (t, t), 1)
    before = jnp.where(s_id < t_id, 1.0, 0.0).astype(BF16)
    rank = jnp.dot(routed.astype(BF16), before, preferred_element_type=F32)
    pos_ref[...] = jnp.where(routed > 0.0, rank.astype(jnp.int32), -1)


def moe_router(x, mod, g, rw, rb):
    nt = N_TOK // MOE_T
    return pl.pallas_call(
        _router_kernel,
        grid=(nt,),
        in_specs=[pl.BlockSpec((MOE_T, D_MODEL), lambda i: (i, 0)),
                  pl.BlockSpec((None, 6, D_MODEL), lambda i: (_tile_cond(i, MOE_T), 0, 0)),
                  pl.BlockSpec((1, D_MODEL), lambda i: (0, 0)),
                  pl.BlockSpec((D_MODEL, 128), lambda i: (0, 0)),
                  pl.BlockSpec((1, 128), lambda i: (0, 0))],
        out_specs=[pl.BlockSpec((MOE_T, D_MODEL), lambda i: (i, 0)),
                   pl.BlockSpec((N_EXPERTS, MOE_T), lambda i: (0, i)),
                   pl.BlockSpec((N_EXPERTS, MOE_T), lambda i: (0, i))],
        out_shape=[jax.ShapeDtypeStruct((N_TOK, D_MODEL), BF16),
                   jax.ShapeDtypeStruct((N_EXPERTS, N_TOK), jnp.int32),
                   jax.ShapeDtypeStruct((N_EXPERTS, N_TOK), F32)],
        compiler_params=_params(("arbitrary",)),
        name="moe_router",
    )(x, mod, g.reshape(1, D_MODEL), rw, rb)


def _moe_sparse_kernel(h_ref, pos_ref, wgt_ref, w1_ref, w3_ref, w2_ref, y_ref, xg_scr, acc_scr, wr_scr):
    e = pl.program_id(1)
    f = pl.program_id(2)
    t = h_ref.shape[0]
    pos_e = pos_ref[pl.ds(e, 1), :]
    n_blocks = (jnp.max(pos_e) + MOE_R) // MOE_R
    row_id = lax.broadcasted_iota(jnp.int32, (MOE_R, t), 0)

    def block_rows(r):
        return pl.ds(pl.multiple_of(r * MOE_R, MOE_R), MOE_R)

    def selects(r):
        return pos_e == row_id + r * MOE_R

    @pl.when((e == 0) & (f == 0))
    def _():
        y_ref[...] = jnp.zeros_like(y_ref)

    @pl.when(f == 0)
    def _():
        wgt_e = wgt_ref[pl.ds(e, 1), :]

        def gather(r, carry):
            sel = selects(r)
            xg = jnp.dot(jnp.where(sel, 1.0, 0.0).astype(BF16), h_ref[...], preferred_element_type=F32)
            xg_scr[block_rows(r), :] = xg.astype(BF16)
            w_rows = jnp.sum(jnp.where(sel, wgt_e, 0.0), axis=1, keepdims=True)
            wr_scr[block_rows(r), :] = jnp.broadcast_to(w_rows, (MOE_R, 128))
            acc_scr[block_rows(r), :] = jnp.zeros((MOE_R, D_MODEL), F32)
            return carry

        lax.fori_loop(0, n_blocks, gather, 0)

    def expert(r, carry):
        xg = xg_scr[block_rows(r), :]
        a = jnp.dot(xg, w1_ref[...], preferred_element_type=F32)
        b = jnp.dot(xg, w3_ref[...], preferred_element_type=F32)
        hid = (a * jax.nn.sigmoid(a) * b * wr_scr[block_rows(r), 0:1]).astype(BF16)
        acc_scr[block_rows(r), :] += jnp.dot(hid, w2_ref[...], preferred_element_type=F32)
        return carry

    lax.fori_loop(0, n_blocks, expert, 0)

    @pl.when(f == pl.num_programs(2) - 1)
    def _():
        def scatter(r, carry):
            onehot = jnp.where(selects(r), 1.0, 0.0).astype(BF16)
            hi, lo = _split_bf16(acc_scr[block_rows(r), :])
            y_ref[...] += lax.dot_general(jnp.concatenate([onehot, onehot], axis=0),
                                          jnp.concatenate([hi, lo], axis=0), _TN, preferred_element_type=F32)
            return carry

        lax.fori_loop(0, n_blocks, scatter, 0)


def moe_experts(h2, pos, wgt, w1, w3, w2):
    nt = N_TOK // MOE_T
    nf = D_FF_EXPERT // MOE_F
    return pl.pallas_call(
        _moe_sparse_kernel,
        grid=(nt, N_EXPERTS, nf),
        in_specs=[pl.BlockSpec((MOE_T, D_MODEL), lambda i, e, f: (i, 0)),
                  pl.BlockSpec((N_EXPERTS, MOE_T), lambda i, e, f: (0, i)),
                  pl.BlockSpec((N_EXPERTS, MOE_T), lambda i, e, f: (0, i)),
                  pl.BlockSpec((None, D_MODEL, MOE_F), lambda i, e, f: (e, 0, f)),
                  pl.BlockSpec((None, D_MODEL, MOE_F), lambda i, e, f: (e, 0, f)),
                  pl.BlockSpec((None, MOE_F, D_MODEL), lambda i, e, f: (e, f, 0))],
        out_specs=pl.BlockSpec((MOE_T, D_MODEL), lambda i, e, f: (i, 0)),
        out_shape=jax.ShapeDtypeStruct((N_TOK, D_MODEL), F32),
        scratch_shapes=[pltpu.VMEM((MOE_T, D_MODEL), BF16),
                        pltpu.VMEM((MOE_T, D_MODEL), F32),
                        pltpu.VMEM((MOE_T, 128), F32)],
        compiler_params=_params(("arbitrary", "arbitrary", "arbitrary")),
        name="moe_experts",
    )(h2, pos, wgt, w1, w3, w2)


def _residual_norm_kernel(x_ref, y_ref, mod_ref, g_ref, o_ref):
    o_ref[...] = _rms(x_ref[...] + mod_ref[5:6, :] * y_ref[...], g_ref[...])


def residual_final_norm(x, y, mod, g, row0, n_rows):
    tm = 1024
    t0 = row0 // tm
    return pl.pallas_call(
        _residual_norm_kernel,
        grid=(n_rows // tm,),
        in_specs=[pl.BlockSpec((tm, D_MODEL), lambda i: (t0 + i, 0)),
                  pl.BlockSpec((tm, D_MODEL), lambda i: (t0 + i, 0)),
                  pl.BlockSpec((None, 6, D_MODEL), lambda i: (_tile_cond(t0 + i, tm), 0, 0)),
                  pl.BlockSpec((1, D_MODEL), lambda i: (0, 0))],
        out_specs=pl.BlockSpec((tm, D_MODEL), lambda i: (i, 0)),
        out_shape=jax.ShapeDtypeStruct((n_rows, D_MODEL), F32),
        compiler_params=_params(("arbitrary",)),
        name="residual_final_norm",
    )(x, y, mod, g.reshape(1, D_MODEL))


ATT_SCALE = HEAD_DIM ** -0.5
LOG2_E = 1.4426950408889634
_NT = (((1,), (1,)), ((), ()))


def _head_rms(x, g_row):
    outs = []
    for h in range(x.shape[1] // HEAD_DIM):
        xh = x[:, h * HEAD_DIM:(h + 1) * HEAD_DIM]
        outs.append(xh * lax.rsqrt(jnp.mean(xh * xh, axis=-1, keepdims=True) + EPS) * g_row)
    return jnp.concatenate(outs, axis=-1)


def _rope_apply(x, c, s):
    w = x.shape[-1]
    lane = lax.broadcasted_iota(jnp.int32, x.shape, 1)
    first_half = ((lane // (HEAD_DIM // 4)) % 2) == 0
    partner = jnp.where(first_half, pltpu.roll(x, w - HEAD_DIM // 4, 1), pltpu.roll(x, HEAD_DIM // 4, 1))
    return x * c + partner * s


def rope_lane_tables(L):
    rows = L // GRID_W
    row = jnp.repeat(jnp.arange(rows, dtype=F32), GRID_W)
    col = jnp.tile(jnp.arange(GRID_W, dtype=F32), rows)
    n_freq = HEAD_DIM // 4
    inv = ROPE_BASE ** (-jnp.arange(n_freq, dtype=F32) / n_freq)
    ang = jnp.stack([row, col], 0)[:, :, None] * inv
    cos, sin = jnp.cos(ang), jnp.sin(ang)
    c = jnp.concatenate([cos[0], cos[0], cos[1], cos[1]], axis=-1)
    s = jnp.concatenate([-sin[0], sin[0], -sin[1], sin[1]], axis=-1)
    return jnp.tile(c, (1, 4)), jnp.tile(s, (1, 4))


def _with_ones(v):
    ones = jnp.ones((v.shape[0], HEAD_DIM), BF16)
    parts = []
    for h in range(v.shape[1] // HEAD_DIM):
        parts += [v[:, h * HEAD_DIM:(h + 1) * HEAD_DIM].astype(BF16), ones]
    return jnp.concatenate(parts, axis=-1)


def _attend_heads(jobs):
    for job in jobs:
        q = (job['q'] * (ATT_SCALE * LOG2_E)).astype(BF16)
        job['s'] = lax.dot_general(q, job['k'], _NT, preferred_element_type=F32)
        if job.get('extra') is not None:
            job['s2'] = lax.dot_general(q, job['extra'][0], _NT, preferred_element_type=F32)
    outs = []
    for job in jobs:
        s, sink = job['s'], job.get('sink')
        if sink is not None:
            sink = sink * LOG2_E
        if job.get('mask') is not None:
            s = jnp.where(job['mask'], s, NEG_INF)
        m = jnp.max(s, axis=-1, keepdims=True)
        if 's2' in job:
            m = jnp.maximum(m, jnp.max(job['s2'], axis=-1, keepdims=True))
        if sink is not None:
            m = jnp.maximum(m, sink)
        o = jnp.dot(jnp.exp2(s - m).astype(BF16), job['v'], preferred_element_type=F32)
        if 's2' in job:
            o = o + jnp.dot(jnp.exp2(job['s2'] - m).astype(BF16), job['extra'][1], preferred_element_type=F32)
        den = o[:, HEAD_DIM:HEAD_DIM + 1]
        if sink is not None:
            den = den + jnp.exp2(sink - m)
        outs.append(o[:, :HEAD_DIM] / den)
    return outs


def _ctx_attn_kernel(sink_ref, z_ref, gq_ref, gk_ref, o_ref, bk_ref):
    z = z_ref[...]
    bq = _head_rms(z[:, 512:768], gq_ref[...])
    bk = _head_rms(z[:, 768:896], gk_ref[...])
    bk_ref[...] = bk
    groups = ((z[:, 0:256], z[:, 256:384], z[:, 384:512], True),
              (bq, bk, z[:, 896:1024], False))
    outs = []
    for q_all, k_all, v_all, use_sink in groups:
        k_all = k_all.astype(BF16)
        v_all = _with_ones(v_all)
        for hq in range(A_HEADS):
            kv = hq // (A_HEADS // A_KV_HEADS)
            outs.append(dict(q=q_all[:, hq * HEAD_DIM:(hq + 1) * HEAD_DIM],
                             k=k_all[:, kv * HEAD_DIM:(kv + 1) * HEAD_DIM],
                             v=v_all[:, kv * 2 * HEAD_DIM:(kv + 1) * 2 * HEAD_DIM],
                             sink=sink_ref[hq] if use_sink else None))
    o_ref[...] = jnp.concatenate(_attend_heads(outs), axis=-1).astype(o_ref.dtype)


def ctx_attention(z, sink, gq, gk):
    return pl.pallas_call(
        _ctx_attn_kernel,
        grid=(BATCH,),
        in_specs=[pl.BlockSpec(memory_space=pltpu.SMEM),
                  pl.BlockSpec((SEQ, 1024), lambda b: (b, 0)),
                  pl.BlockSpec((1, HEAD_DIM), lambda b: (0, 0)),
                  pl.BlockSpec((1, HEAD_DIM), lambda b: (0, 0))],
        out_specs=[pl.BlockSpec((SEQ, 512), lambda b: (b, 0)),
                   pl.BlockSpec((SEQ, 128), lambda b: (b, 0))],
        out_shape=[jax.ShapeDtypeStruct((CTX_TOK, 512), BF16),
                   jax.ShapeDtypeStruct((CTX_TOK, 128), F32)],
        compiler_params=_params(("arbitrary",)),
        name="ctx_attention",
    )(sink, z, gq.reshape(1, HEAD_DIM), gk.reshape(1, HEAD_DIM))


LB_TQ = 256


def _lat_b_kernel(zq_ref, zkv_ref, ck_ref, cv_ref, cq_ref, sq_ref, ckk_ref, skk_ref, gq_ref, gk_ref,
                  o_ref, k_scr, v_scr):
    L = zkv_ref.shape[0]

    @pl.when(pl.program_id(1) == 0)
    def _():
        kv = zkv_ref[...]
        bk = _rope_apply(_head_rms(kv[:, :128], gk_ref[...]), ckk_ref[...], skk_ref[...])
        k_scr[0:L, :] = bk.astype(BF16)
        k_scr[L:L + PAST_LEN, :] = ck_ref[...].astype(BF16)
        v_scr[0:L, :] = _with_ones(kv[:, 128:])
        v_scr[L:L + PAST_LEN, :] = _with_ones(cv_ref[...])

    q = _rope_apply(_head_rms(zq_ref[...], gq_ref[...]), cq_ref[...], sq_ref[...])
    outs = []
    for hq in range(B_HEADS):
        kv = hq // (B_HEADS // B_KV_HEADS)
        sl = slice(kv * HEAD_DIM, (kv + 1) * HEAD_DIM)
        outs.append(dict(q=q[:, hq * HEAD_DIM:(hq + 1) * HEAD_DIM], k=k_scr[:, sl],
                         v=v_scr[:, kv * 2 * HEAD_DIM:(kv + 1) * 2 * HEAD_DIM]))
    o_ref[...] = jnp.concatenate(_attend_heads(outs), axis=-1).astype(o_ref.dtype)


def latent_attention_b(z, row0, nb, L, cache_k, cache_v, rope_c, rope_s, gq, gk):
    nq = L // LB_TQ
    return pl.pallas_call(
        _lat_b_kernel,
        grid=(nb, nq),
        in_specs=[pl.BlockSpec((LB_TQ, 256), lambda b, i: (row0 // LB_TQ + b * nq + i, 2)),
                  pl.BlockSpec((L, 256), lambda b, i: (row0 // L + b, 3)),
                  pl.BlockSpec((None, PAST_LEN, 128), lambda b, i: (b, 0, 0)),
                  pl.BlockSpec((None, PAST_LEN, 128), lambda b, i: (b, 0, 0)),
                  pl.BlockSpec((LB_TQ, 256), lambda b, i: (i, 0)),
                  pl.BlockSpec((LB_TQ, 256), lambda b, i: (i, 0)),
                  pl.BlockSpec((L, 128), lambda b, i: (0, 0)),
                  pl.BlockSpec((L, 128), lambda b, i: (0, 0)),
                  pl.BlockSpec((1, HEAD_DIM), lambda b, i: (0, 0)),
                  pl.BlockSpec((1, HEAD_DIM), lambda b, i: (0, 0))],
        out_specs=pl.BlockSpec((LB_TQ, 256), lambda b, i: (b * nq + i, 0)),
        out_shape=jax.ShapeDtypeStruct((nb * L, 256), BF16),
        scratch_shapes=[pltpu.VMEM((L + PAST_LEN, 128), BF16),
                        pltpu.VMEM((L + PAST_LEN, 256), BF16)],
        compiler_params=_params(("arbitrary", "arbitrary")),
        name="latent_attention_b",
    )(z, z, cache_k, cache_v, rope_c, rope_s, rope_c, rope_s, gq.reshape(1, HEAD_DIM), gk.reshape(1, HEAD_DIM))


def _lat_a_kernel(sink_ref, zq_ref, zkv_ref, ck_ref, cv_ref, cq_ref, sq_ref, ckk_ref, skk_ref,
                  o_ref, k_scr, v_scr, ck_scr, cv_scr):
    L = zkv_ref.shape[0]
    i = pl.program_id(1)

    @pl.when(i == 0)
    def _():
        kv = zkv_ref[...]
        k_scr[0:BLOCK, :] = jnp.zeros((BLOCK, 128), BF16)
        v_scr[0:BLOCK, :] = jnp.zeros((BLOCK, 256), BF16)
        k_scr[BLOCK:BLOCK + L, :] = _rope_apply(kv[:, :128], ckk_ref[...], skk_ref[...]).astype(BF16)
        v_scr[BLOCK:BLOCK + L, :] = _with_ones(kv[:, 128:])
        k_scr[BLOCK + L:2 * BLOCK + L, :] = jnp.zeros((BLOCK, 128), BF16)
        v_scr[BLOCK + L:2 * BLOCK + L, :] = jnp.zeros((BLOCK, 256), BF16)
        ck_scr[...] = ck_ref[...].astype(BF16)
        cv_scr[...] = _with_ones(cv_ref[...])

    q = _rope_apply(zq_ref[...], cq_ref[...], sq_ref[...])
    start = pl.multiple_of(i * BLOCK, BLOCK)
    kband = k_scr[pl.ds(start, 3 * BLOCK), :]
    vband = v_scr[pl.ds(start, 3 * BLOCK), :]
    r = lax.broadcasted_iota(jnp.int32, (BLOCK, 3 * BLOCK), 0)
    cidx = lax.broadcasted_iota(jnp.int32, (BLOCK, 3 * BLOCK), 1)
    kpos = i * BLOCK - BLOCK + cidx
    mask = (jnp.abs(cidx - BLOCK - r) <= WINDOW) & (kpos >= 0) & (kpos < L)
    outs = []
    for hq in range(A_HEADS):
        kv = hq // (A_HEADS // A_KV_HEADS)
        sl = slice(kv * HEAD_DIM, (kv + 1) * HEAD_DIM)
        sv = slice(kv * 2 * HEAD_DIM, (kv + 1) * 2 * HEAD_DIM)
        outs.append(dict(q=q[:, hq * HEAD_DIM:(hq + 1) * HEAD_DIM], k=kband[:, sl], v=vband[:, sv],
                         extra=(ck_scr[:, sl], cv_scr[:, sv]), sink=sink_ref[hq], mask=mask))
    o_ref[...] = jnp.concatenate(_attend_heads(outs), axis=-1).astype(o_ref.dtype)


def latent_attention_a(z, row0, nb, L, sink, cache_k, cache_v, rope_c, rope_s):
    nq = L // BLOCK
    return pl.pallas_call(
        _lat_a_kernel,
        grid=(nb, nq),
        in_specs=[pl.BlockSpec(memory_space=pltpu.SMEM),
                  pl.BlockSpec((BLOCK, 256), lambda b, i: (row0 // BLOCK + b * nq + i, 0)),
                  pl.BlockSpec((L, 256), lambda b, i: (row0 // L + b, 1)),
                  pl.BlockSpec((None, PAST_LEN, 128), lambda b, i: (b, 0, 0)),
                  pl.BlockSpec((None, PAST_LEN, 128), lambda b, i: (b, 0, 0)),
                  pl.BlockSpec((BLOCK, 256), lambda b, i: (i, 0)),
                  pl.BlockSpec((BLOCK, 256), lambda b, i: (i, 0)),
                  pl.BlockSpec((L, 128), lambda b, i: (0, 0)),
                  pl.BlockSpec((L, 128), lambda b, i: (0, 0))],
        out_specs=pl.BlockSpec((BLOCK, 256), lambda b, i: (b * nq + i, 0)),
        out_shape=jax.ShapeDtypeStruct((nb * L, 256), BF16),
        scratch_shapes=[pltpu.VMEM((L + 2 * BLOCK, 128), BF16),
                        pltpu.VMEM((L + 2 * BLOCK, 256), BF16),
                        pltpu.VMEM((PAST_LEN, 128), BF16),
                        pltpu.VMEM((PAST_LEN, 256), BF16)],
        compiler_params=_params(("arbitrary", "arbitrary")),
        name="latent_attention_a",
    )(sink, z, z, cache_k, cache_v, rope_c, rope_s, rope_c, rope_s)


_TN = (((0,), (0,)), ((), ()))
HG_GROUP = 4


def _hgrn_kernel(zq_ref, zf_ref, zi_ref, zg_ref, lb_ref, gn_ref, s0_ref, o_ref, sT_ref,
                 of_scr, ob_scr, g_scr, k_scr, qin_scr, kin_scr, v_scr, S_scr, *, tt):
    d = pl.program_id(1)
    j = pl.program_id(2)
    n_t = pl.num_programs(2)
    C = HGRN_CHUNK
    n_c = tt // C

    @pl.when(j == 0)
    def _():
        S_scr[...] = s0_ref[...]

    lb = lb_ref[...]
    sg = jax.nn.sigmoid(zf_ref[...])
    logf = jnp.log(lb + (1.0 - lb) * sg)
    k = (1.0 - lb) * (1.0 - sg)
    k_scr[...] = k
    v_scr[...] = zi_ref[...].astype(BF16)
    in_chunk = lax.broadcasted_iota(jnp.int32, (tt, C_HEADS * HEAD_DIM), 0) % C
    row = lax.broadcasted_iota(jnp.int32, (C, C), 0)
    col = lax.broadcasted_iota(jnp.int32, (C, C), 1)
    heads = [slice(h * HEAD_DIM, (h + 1) * HEAD_DIM) for h in range(C_HEADS)]

    def run(reverse, tile):
        G = logf
        step = 1
        while step < C:
            if reverse:
                G = G + jnp.where(in_chunk < C - step, pltpu.roll(G, tt - step, 0), 0.0)
            else:
                G = G + jnp.where(in_chunk >= step, pltpu.roll(G, step, 0), 0.0)
            step *= 2
        g_scr[...] = G
        qin_scr[...] = (zq_ref[...] * jnp.exp(G)).astype(BF16)
        kin_scr[...] = (k * jnp.exp(-G)).astype(BF16)
        tri = (row <= col) if reverse else (row >= col)

        def body(gi, carry):
            chunks = []
            for g in range(HG_GROUP):
                ci = gi * HG_GROUP + g
                c = (n_c - 1 - ci) if reverse else ci
                r0 = pl.multiple_of(c * C, C)
                rows = pl.ds(r0, C)
                G_c = g_scr[rows, :]
                G_end = G_c[0:1, :] if reverse else G_c[C - 1:C, :]
                vc = v_scr[rows, :]
                q_in = qin_scr[rows, :]
                k_in = kin_scr[rows, :]
                k_out = (k_scr[rows, :] * jnp.exp(G_end - G_c)).astype(BF16)
                chunks.append(dict(
                    r0=r0, rows=rows, vc=vc, q_in=q_in, decay=jnp.exp(G_end),
                    a=[lax.dot_general(q_in[:, sl], k_in[:, sl], _NT, preferred_element_type=F32) for sl in heads],
                    kv=[lax.dot_general(vc[:, sl], k_out[:, sl], _TN, preferred_element_type=F32) for sl in heads]))
            s_cur = [S_scr[h] for h in range(C_HEADS)]
            for ch in chunks:
                ch['qs'] = [lax.dot_general(ch['q_in'][:, sl], s_cur[h].astype(BF16), _NT,
                                            preferred_element_type=F32) for h, sl in enumerate(heads)]
                s_cur = [s_cur[h] * ch['decay'][:, sl] + ch['kv'][h] for h, sl in enumerate(heads)]
            for h in range(C_HEADS):
                S_scr[h] = s_cur[h]
            for ch in chunks:
                o_c = jnp.concatenate(
                    [jnp.dot(jnp.where(tri, ch['a'][h], 0.0).astype(BF16), ch['vc'][:, sl],
                             preferred_element_type=F32) + ch['qs'][h] for h, sl in enumerate(heads)], axis=-1)
                if reverse:
                    ob_scr[ch['rows'], :] = o_c
                else:
                    of_scr[pl.ds(pl.multiple_of(tile * tt, tt) + ch['r0'], C), :] = o_c
            return carry

        lax.fori_loop(0, n_c // HG_GROUP, body, 0)

    @pl.when(d == 0)
    def _():
        run(False, j)

    @pl.when(d == 1)
    def _():
        tile = n_t - 1 - j
        run(True, tile)
        o = of_scr[pl.ds(pl.multiple_of(tile * tt, tt), tt), :] + ob_scr[...]
        g = zg_ref[...]
        o_ref[...] = (_head_rms(o, gn_ref[...]) * (g * jax.nn.sigmoid(g))).astype(o_ref.dtype)

    @pl.when(j == n_t - 1)
    def _():
        sT_ref[...] = S_scr[...]


def hgrn_mixer(z, row0, nb, L, lb, gn, s0_t):
    tt = min(L, 512)
    n_t = L // tt
    rb = row0 // tt

    def tile(d, j):
        return jnp.where(d == 0, j, n_t - 1 - j)

    def late(d, j):
        return jnp.where(d == 0, n_t - 1, n_t - 1 - j)

    st_spec = pl.BlockSpec((None, None, C_HEADS, HEAD_DIM, HEAD_DIM), lambda b, d, j: (b, d, 0, 0, 0))
    return pl.pallas_call(
        functools.partial(_hgrn_kernel, tt=tt),
        grid=(nb, 2, n_t),
        in_specs=[pl.BlockSpec((tt, 256), lambda b, d, j: (rb + b * n_t + tile(d, j), 4)),
                  pl.BlockSpec((tt, 256), lambda b, d, j: (rb + b * n_t + tile(d, j), 5 + d)),
                  pl.BlockSpec((tt, 256), lambda b, d, j: (rb + b * n_t + tile(d, j), 7)),
                  pl.BlockSpec((tt, 256), lambda b, d, j: (rb + b * n_t + late(d, j), 8)),
                  pl.BlockSpec((1, 256), lambda b, d, j: (0, 0)),
                  pl.BlockSpec((1, HEAD_DIM), lambda b, d, j: (0, 0)),
                  st_spec],
        out_specs=[pl.BlockSpec((tt, 256), lambda b, d, j: (b * n_t + late(d, j), 0)),
                   st_spec],
        out_shape=[jax.ShapeDtypeStruct((nb * L, 256), BF16),
                   jax.ShapeDtypeStruct((nb, 2, C_HEADS, HEAD_DIM, HEAD_DIM), F32)],
        scratch_shapes=[pltpu.VMEM((L, 256), F32),
                        pltpu.VMEM((tt, 256), F32),
                        pltpu.VMEM((tt, 256), F32),
                        pltpu.VMEM((tt, 256), F32),
                        pltpu.VMEM((tt, 256), BF16),
                        pltpu.VMEM((tt, 256), BF16),
                        pltpu.VMEM((tt, 256), BF16),
                        pltpu.VMEM((C_HEADS, HEAD_DIM, HEAD_DIM), F32)],
        compiler_params=_params(("arbitrary", "arbitrary", "arbitrary")),
        name="hgrn_mixer",
    )(z, z, z, z, lb.reshape(1, 256), gn.reshape(1, HEAD_DIM), s0_t)


DL_C = DELTA_CHUNK
DL_PREP_TT = 256
DL_HALO = 8
DL_CHUNK_TT = 512
N_QKV_HEADS = 3 * D_HEADS
DL_HL_LEVELS = 3


def _delta_prep_kernel(x_ref, xp_ref, xn_ref, zab_ref, cw_ref, na_ref, dtb_ref, qkv_ref, gate_ref, xs_scr):
    tt = x_ref.shape[0]
    row = pl.program_id(0) * tt
    lat = row - CTX_TOK
    first = jnp.where(row < CTX_TOK, True, lat % DEC_SEQ == 0)
    last = jnp.where(row < CTX_TOK, True, (lat + tt) % DEC_SEQ == 0)
    xs_scr[DL_HALO:DL_HALO + tt, :] = x_ref[...]
    xs_scr[0:DL_HALO, :] = jnp.where(first, 0.0, xp_ref[...])
    xs_scr[DL_HALO + tt:2 * DL_HALO + tt, :] = jnp.where(last, 0.0, xn_ref[...])
    pad = (CONV_K - 1) // 2
    y = None
    for t in range(CONV_K):
        term = xs_scr[pl.ds(DL_HALO - pad + t, tt), :] * cw_ref[t:t + 1, :]
        y = term if y is None else y + term
    y = y * jax.nn.sigmoid(y)
    for idx in range(N_QKV_HEADS):
        xh = y[:, idx * HEAD_DIM:(idx + 1) * HEAD_DIM]
        if idx < 2 * D_HEADS:
            xh = xh * lax.rsqrt(jnp.sum(xh * xh, axis=-1, keepdims=True) + EPS)
        if idx < D_HEADS:
            xh = xh * ATT_SCALE
        qkv_ref[idx] = xh
    zab = zab_ref[...]
    lane = lax.broadcasted_iota(jnp.int32, zab.shape, 1)
    t_ = zab + dtb_ref[...]
    softplus = jnp.maximum(t_, 0.0) + jnp.log(1.0 + jnp.exp(-jnp.abs(t_)))
    gate_ref[...] = jnp.where(lane < 2 * D_HEADS, na_ref[...] * softplus, jax.nn.sigmoid(zab))


def delta_prep(z, conv_w, a_log, dt_bias):
    tt = DL_PREP_TT
    hb = tt // DL_HALO
    n_hb = N_TOK // DL_HALO
    pad8 = lambda v: jnp.concatenate([v.reshape(1, 2 * D_HEADS), jnp.zeros((1, 128 - 2 * D_HEADS), F32)], axis=1)
    return pl.pallas_call(
        _delta_prep_kernel,
        grid=(N_TOK // tt,),
        in_specs=[pl.BlockSpec((tt, 768), lambda i: (i, 3)),
                  pl.BlockSpec((DL_HALO, 768), lambda i: (jnp.maximum(i * hb - 1, 0), 3)),
                  pl.BlockSpec((DL_HALO, 768), lambda i: (jnp.minimum((i + 1) * hb, n_hb - 1), 3)),
                  pl.BlockSpec((tt, 128), lambda i: (i, Z_DAB // 128)),
                  pl.BlockSpec((CONV_K, 768), lambda i: (0, 0)),
                  pl.BlockSpec((1, 128), lambda i: (0, 0)),
                  pl.BlockSpec((1, 128), lambda i: (0, 0))],
        out_specs=[pl.BlockSpec((N_QKV_HEADS, tt, HEAD_DIM), lambda i: (0, i, 0)),
                   pl.BlockSpec((tt, 128), lambda i: (i, 0))],
        out_shape=[jax.ShapeDtypeStruct((N_QKV_HEADS, N_TOK, HEAD_DIM), F32),
                   jax.ShapeDtypeStruct((N_TOK, 128), F32)],
        scratch_shapes=[pltpu.VMEM((tt + 2 * DL_HALO, 768), F32)],
        compiler_params=_params(("arbitrary",)),
        name="delta_prep",
    )(z, z, z, z, conv_w, pad8(-jnp.exp(a_log)), pad8(dt_bias))


def _split_bf16(a):
    hi = a.astype(BF16)
    return hi, (a - hi.astype(F32)).astype(BF16)


def _dot_hl(a, b_parts):
    a_hi, a_lo = _split_bf16(a)
    b_hi, b_lo = b_parts
    m = a.shape[0]
    r = jnp.dot(jnp.concatenate([a_hi, a_lo], axis=0), b_hi, preferred_element_type=F32)
    return r[:m] + r[m:] + jnp.dot(a_hi, b_lo, preferred_element_type=F32)


def _delta_chunk_kernel(qkv_ref, gate_ref, u2_ref, wq_ref, ak_ref):
    C = DL_C
    n_c = gate_ref.shape[0] // C
    row = lax.broadcasted_iota(jnp.int32, (C, C), 0)
    col = lax.broadcasted_iota(jnp.int32, (C, C), 1)
    eye = (row == col).astype(F32)

    def body(c, carry):
        r0 = pl.multiple_of(c * C, C)
        ga = gate_ref[pl.ds(r0, C), :]
        chains = []
        for d in range(2):
            incl = (row >= col) if d == 0 else (row <= col)
            strict = (row > col) if d == 0 else (row < col)
            g_all = jnp.dot(incl.astype(F32), ga, precision=lax.Precision.HIGHEST, preferred_element_type=F32)
            g_all_t = g_all.T
            for h in range(D_HEADS):
                ci = d * D_HEADS + h
                q = qkv_ref[h, pl.ds(r0, C), :]
                k = qkv_ref[D_HEADS + h, pl.ds(r0, C), :]
                v = qkv_ref[2 * D_HEADS + h, pl.ds(r0, C), :]
                g_col = g_all[:, ci:ci + 1]
                g_row = g_all_t[ci:ci + 1, :]
                beta = ga[:, 2 * D_HEADS + ci:2 * D_HEADS + ci + 1]
                g_end = g_col[C - 1:C, :] if d == 0 else g_col[0:1, :]
                kb = k * beta
                eg = jnp.exp(g_col)
                decay = jnp.where(incl, jnp.exp(jnp.where(incl, g_col - g_row, 0.0)), 0.0)
                kq = jnp.concatenate([kb, q], axis=0).astype(BF16)
                chains.append(dict(
                    strict=strict, decay=decay, qg=q * eg, g_end=g_end,
                    r=lax.dot_general(kq, k.astype(BF16), _NT, preferred_element_type=F32),
                    rhs=jnp.concatenate([v * beta, kb * eg], axis=1).astype(BF16),
                    ke_t=(k * jnp.exp(g_end - g_col)).T))
        for ch in chains:
            ch['p'] = -jnp.where(ch['strict'], ch['r'][:C] * ch['decay'], 0.0)
            ch['t'] = eye + ch['p']
        for level in range(5):
            if level < DL_HL_LEVELS:
                for ch in chains:
                    ch['p'] = _dot_hl(ch['p'], _split_bf16(ch['p']))
                for ch in chains:
                    ch['t'] = ch['t'] + _dot_hl(ch['t'], _split_bf16(ch['p']))
            else:
                for ch in chains:
                    pb = ch['p'].astype(BF16)
                    ch['p'] = jnp.dot(pb, pb, preferred_element_type=F32)
                for ch in chains:
                    ch['t'] = ch['t'] + jnp.dot(ch['t'].astype(BF16), ch['p'].astype(BF16),
                                                preferred_element_type=F32)
        for ch in chains:
            ch['uw'] = jnp.dot(ch['t'].astype(BF16), ch['rhs'], preferred_element_type=F32)
        u2 = [jnp.concatenate([ch['uw'][:, :C], jnp.broadcast_to(jnp.exp(ch['g_end']), (C, C))], axis=1)
              for ch in chains]
        wq = [jnp.concatenate([ch['uw'][:, C:], ch['qg']], axis=0).astype(BF16) for ch in chains]
        ak = [jnp.concatenate([ch['r'][C:] * ch['decay'], ch['ke_t']], axis=0).astype(BF16) for ch in chains]
        pack = lambda xs: jnp.stack(xs).reshape((2, D_HEADS) + xs[0].shape)
        u2_ref[:, :, pl.ds(r0, C), :] = pack(u2)
        wq_ref[:, :, c] = pack(wq)
        ak_ref[:, :, c] = pack(ak)
        return carry

    lax.fori_loop(0, n_c, body, 0)


def delta_chunks(qkv, gates):
    tt = DL_CHUNK_TT
    n_c = tt // DL_C
    return pl.pallas_call(
        _delta_chunk_kernel,
        grid=(N_TOK // tt,),
        in_specs=[pl.BlockSpec((N_QKV_HEADS, tt, HEAD_DIM), lambda i: (0, i, 0)),
                  pl.BlockSpec((tt, 128), lambda i: (i, 0))],
        out_specs=[pl.BlockSpec((2, D_HEADS, tt, 128), lambda i: (0, 0, i, 0)),
                   pl.BlockSpec((2, D_HEADS, n_c, 2 * DL_C, HEAD_DIM), lambda i: (0, 0, i, 0, 0)),
                   pl.BlockSpec((2, D_HEADS, n_c, 2 * DL_C, HEAD_DIM), lambda i: (0, 0, i, 0, 0))],
        out_shape=[jax.ShapeDtypeStruct((2, D_HEADS, N_TOK, 128), F32),
                   jax.ShapeDtypeStruct((2, D_HEADS, N_TOK // DL_C, 2 * DL_C, HEAD_DIM), BF16),
                   jax.ShapeDtypeStruct((2, D_HEADS, N_TOK // DL_C, 2 * DL_C, HEAD_DIM), BF16)],
        compiler_params=_params(("arbitrary",)),
        name="delta_chunks",
    )(qkv, gates)


def _delta_scan_kernel(u2f_ref, wqf_ref, akf_ref, u2b_ref, wqb_ref, akb_ref, s0_ref,
                       of_ref, ob_ref, s_ref, s_scr):
    j = pl.program_id(1)
    C = DL_C
    n_c = wqf_ref.shape[1]

    @pl.when(j == 0)
    def _():
        s_scr[...] = s0_ref[...]

    def body(ci, carry):
        chains = []
        for d, (u2_ref, wq_ref, ak_ref, o_ref) in enumerate(((u2f_ref, wqf_ref, akf_ref, of_ref),
                                                            (u2b_ref, wqb_ref, akb_ref, ob_ref))):
            c = ci if d == 0 else n_c - 1 - ci
            r0 = pl.multiple_of(c * C, C)
            for h in range(D_HEADS):
                s = s_scr[d, h]
                chains.append(dict(d=d, h=h, c=c, r0=r0, s=s, ak_ref=ak_ref, o_ref=o_ref,
                                   u2=u2_ref[h, pl.ds(r0, C), :],
                                   r1=jnp.dot(wq_ref[h, c], s.astype(BF16),
                                              preferred_element_type=F32)))
        for ch in chains:
            v_new = ch['u2'][:, :C] - ch['r1'][:C]
            ch['r2'] = jnp.dot(ch['ak_ref'][ch['h'], ch['c']], v_new.astype(BF16),
                               preferred_element_type=F32)
        for ch in chains:
            ch['o_ref'][ch['h'], pl.ds(ch['r0'], C), :] = ch['r1'][C:] + ch['r2'][:C]
            s_scr[ch['d'], ch['h']] = ch['s'] * ch['u2'][0:1, C:] + ch['r2'][C:]
        return carry

    lax.fori_loop(0, n_c, body, 0)

    @pl.when(j == pl.num_programs(1) - 1)
    def _():
        s_ref[...] = s_scr[...]


def delta_scan(u2, wq, ak, row0, nb, L, s0):
    tt = min(L, 512)
    n_t = L // tt
    n_c = tt // DL_C
    rb = row0 // tt
    fwd = lambda b, j: rb + b * n_t + j
    bwd = lambda b, j: rb + b * n_t + (n_t - 1 - j)
    u_spec = lambda d, f: pl.BlockSpec((None, D_HEADS, tt, 128), lambda b, j: (d, 0, f(b, j), 0))
    c_spec = lambda d, f: pl.BlockSpec((None, D_HEADS, n_c, 2 * DL_C, HEAD_DIM), lambda b, j: (d, 0, f(b, j), 0, 0))
    st_spec = pl.BlockSpec((None, 2, D_HEADS, HEAD_DIM, HEAD_DIM), lambda b, j: (b, 0, 0, 0, 0))
    return pl.pallas_call(
        _delta_scan_kernel,
        grid=(nb, n_t),
        in_specs=[u_spec(0, fwd), c_spec(0, fwd), c_spec(0, fwd),
                  u_spec(1, bwd), c_spec(1, bwd), c_spec(1, bwd), st_spec],
        out_specs=[pl.BlockSpec((D_HEADS, tt, HEAD_DIM), lambda b, j: (0, b * n_t + j, 0)),
                   pl.BlockSpec((D_HEADS, tt, HEAD_DIM), lambda b, j: (0, b * n_t + (n_t - 1 - j), 0)),
                   st_spec],
        out_shape=[jax.ShapeDtypeStruct((D_HEADS, nb * L, HEAD_DIM), F32),
                   jax.ShapeDtypeStruct((D_HEADS, nb * L, HEAD_DIM), F32),
                   jax.ShapeDtypeStruct((nb, 2, D_HEADS, HEAD_DIM, HEAD_DIM), F32)],
        scratch_shapes=[pltpu.VMEM((2, D_HEADS, HEAD_DIM, HEAD_DIM), F32)],
        compiler_params=_params(("arbitrary", "arbitrary")),
        name="delta_scan",
    )(u2, wq, ak, u2, wq, ak, s0)


def _delta_out_kernel(of_ref, ob_ref, zg_ref, gn_ref, o_ref):
    outs = []
    for h in range(D_HEADS):
        o = of_ref[h] + ob_ref[h]
        outs.append(o * lax.rsqrt(jnp.mean(o * o, axis=-1, keepdims=True) + EPS) * gn_ref[...])
    g = zg_ref[...]
    o_ref[...] = (jnp.concatenate(outs, axis=-1) * (g * jax.nn.sigmoid(g))).astype(o_ref.dtype)


def delta_output(o_f, o_b, z, row0, gn):
    n = o_f.shape[1]
    tt = 256
    return pl.pallas_call(
        _delta_out_kernel,
        grid=(n // tt,),
        in_specs=[pl.BlockSpec((D_HEADS, tt, HEAD_DIM), lambda i: (0, i, 0)),
                  pl.BlockSpec((D_HEADS, tt, HEAD_DIM), lambda i: (0, i, 0)),
                  pl.BlockSpec((tt, 256), lambda i: (row0 // tt + i, Z_DG // 256)),
                  pl.BlockSpec((1, HEAD_DIM), lambda i: (0, 0))],
        out_specs=pl.BlockSpec((tt, 256), lambda i: (i, 0)),
        out_shape=jax.ShapeDtypeStruct((n, 256), BF16),
        compiler_params=_params(("arbitrary",)),
        name="delta_output",
    )(o_f, o_b, z, gn.reshape(1, HEAD_DIM))


def kernel(x_prompt, x_sample, cache_attn_a_k, cache_attn_a_v, cache_attn_b_k, cache_attn_b_v,
           state_hgrn, state_delta, c, c_ctx, norm1_g, norm2_g, w_ada, b_ada, w_in, a_sink,
           b_qnorm_g, b_knorm_g, c_lb, c_onorm_g, d_conv, d_a_log, d_dt_bias, d_onorm_g,
           w_branch, w_out, ffn_w1, ffn_w3, ffn_w2, router_w, router_b, moe_w1, moe_w3, moe_w2,
           final_norm_g):
    cum = jnp.cumsum(jax.nn.softmax(c_lb, axis=0), axis=0)
    lower_bounds = cum - cum[:1]

    x = jnp.concatenate([x_prompt.reshape(CTX_TOK, D_MODEL), x_sample.reshape(LAT_TOK, D_MODEL)], axis=0)
    cond = jnp.concatenate([c_ctx[None, :], c, jnp.zeros((16 - N_COND, D_MODEL), F32)], axis=0)

    rope_c, rope_s = rope_lane_tables(DEC_SEQ)
    caches = []
    for l in range(DEPTH):
        mod = ada_modulation(cond, w_ada[l], b_ada[l])[:N_COND].reshape(N_COND, 6, D_MODEL)
        w_mix = jnp.concatenate([w_in[l][:, :Z_MAIN], w_in[l][:, Z_MAIN + 16:W_IN_MIX],
                                 w_in[l][:, Z_MAIN:Z_MAIN + 16], jnp.zeros((D_MODEL, 128 - 16), F32)],
                                axis=1).astype(BF16)
        w_gl = w_in[l][:, W_IN_MIX:].reshape(D_MODEL, N_BRANCH, D_MODEL).transpose(1, 0, 2).astype(BF16)
        z = input_projection(x, mod, norm1_g[l], w_mix)
        kv2 = lambda t: t.reshape(DEC_BATCH, PAST_LEN, 128)
        o_ab_ctx, bk_ctx = ctx_attention(z, a_sink[l], b_qnorm_g[l], b_knorm_g[l])
        o_a_lat = latent_attention_a(z, CTX_TOK, DEC_BATCH, DEC_SEQ, a_sink[l], kv2(cache_attn_a_k[:, l]),
                                     kv2(cache_attn_a_v[:, l]), rope_c, rope_s)
        o_b_lat = latent_attention_b(z, CTX_TOK, DEC_BATCH, DEC_SEQ, kv2(cache_attn_b_k[:, l]),
                                     kv2(cache_attn_b_v[:, l]), rope_c, rope_s, b_qnorm_g[l], b_knorm_g[l])
        o_c_ctx, sc_t = hgrn_mixer(z, 0, BATCH, SEQ, lower_bounds[l], c_onorm_g[l],
                                   jnp.zeros((BATCH, 2, C_HEADS, HEAD_DIM, HEAD_DIM), F32))
        o_c_lat, _ = hgrn_mixer(z, CTX_TOK, DEC_BATCH, DEC_SEQ, lower_bounds[l], c_onorm_g[l],
                                jnp.swapaxes(state_hgrn[:, l], -1, -2))

        qkv, gates = delta_prep(z, d_conv[l], d_a_log[l], d_dt_bias[l])
        u2, wq, ak = delta_chunks(qkv, gates)
        of_ctx, ob_ctx, sd = delta_scan(u2, wq, ak, 0, BATCH, SEQ,
                                        jnp.zeros((BATCH, 2, D_HEADS, HEAD_DIM, HEAD_DIM), F32))
        of_lat, ob_lat, _ = delta_scan(u2, wq, ak, CTX_TOK, DEC_BATCH, DEC_SEQ, state_delta[:, l])
        o_d_ctx = delta_output(of_ctx, ob_ctx, z, 0, d_onorm_g[l])
        o_d_lat = delta_output(of_lat, ob_lat, z, CTX_TOK, d_onorm_g[l])
        kvh = lambda t: t.reshape(BATCH, SEQ, 2, HEAD_DIM)
        caches.append((kvh(z[:CTX_TOK, 256:384]), kvh(z[:CTX_TOK, 384:512]), kvh(bk_ctx), kvh(z[:CTX_TOK, 896:1024]),
                       jnp.swapaxes(sc_t, -1, -2), sd))
        x = merge_projection(x, mod, norm1_g[l], (o_ab_ctx, o_c_ctx, o_d_ctx), (o_a_lat, o_b_lat, o_c_lat, o_d_lat),
                             w_gl, w_branch[l].astype(BF16), w_out[l].astype(BF16))
        j = l // 2
        if l % 2 == 0:
            x = dense_ffn(x, mod, norm2_g[l], ffn_w1[j].astype(BF16), ffn_w3[j].astype(BF16),
                          ffn_w2[j].astype(BF16))
        else:
            assert l == DEPTH - 1, "the expert layer's residual is fused with the final norm"
            rw = jnp.concatenate([router_w[j], jnp.zeros((D_MODEL, 128 - N_EXPERTS), F32)], axis=1)
            rb = jnp.concatenate([router_b[j], jnp.zeros((128 - N_EXPERTS,), F32)])[None, :]
            h2, pos, wgt = moe_router(x, mod, norm2_g[l], rw, rb)
            f = moe_experts(h2, pos, wgt, moe_w1[j].astype(BF16), moe_w3[j].astype(BF16), moe_w2[j].astype(BF16))
            y_prompt = residual_final_norm(x, f, mod, final_norm_g, 0, CTX_TOK).reshape(BATCH, SEQ, D_MODEL)
            y_sample = residual_final_norm(x, f, mod, final_norm_g, CTX_TOK, LAT_TOK).reshape(DEC_BATCH, DEC_SEQ, D_MODEL)

    stack = lambda idx: jnp.stack([caches[l][idx] for l in range(DEPTH)], axis=1)
    return (y_prompt, y_sample, stack(0), stack(1), stack(2), stack(3), stack(4), stack(5))
```

```python
import functools

import jax
import jax.numpy as jnp
import numpy as np
from jax import lax
from jax.experimental import pallas as pl
from jax.experimental.pallas import tpu as pltpu

F32 = jnp.float32
BF16 = jnp.bfloat16

D_MODEL = 1024
BATCH = 32
SEQ = 256
DEPTH = 2
DEC_BATCH = 8
DEC_SEQ = 4096
PAST_LEN = 256
GRID_W = 64
HEAD_DIM = 64
A_HEADS = 4
A_KV_HEADS = 2
B_HEADS = 4
B_KV_HEADS = 2
C_HEADS = 4
D_HEADS = 4
BRANCH_W = 256
N_BRANCH = 4
WINDOW = 128
BLOCK = 128
ROPE_BASE = 10000.0
HGRN_CHUNK = 32
DELTA_CHUNK = 64
CONV_K = 5
D_FF = 2816
N_EXPERTS = 8
D_FF_EXPERT = 3584
EPS = 1e-6
NEG_INF = -1e30
F32_MIN = float(np.finfo(np.float32).min)

CTX_TOK = BATCH * SEQ
LAT_TOK = DEC_BATCH * DEC_SEQ
N_TOK = CTX_TOK + LAT_TOK
N_COND = 1 + DEC_BATCH

Z_MAIN = 3072
Z_DG = Z_MAIN
Z_DAB = Z_DG + BRANCH_W
Z_COLS = Z_DAB + 128
W_IN_MIX = 3344

TM = 512
VMEM_LIMIT = 56 * 1024 * 1024


def _tile_cond(i, tm):
    ctx_tiles = CTX_TOK // tm
    per_b = DEC_SEQ // tm
    return jnp.where(i < ctx_tiles, 0, 1 + (i - ctx_tiles) // per_b)


def _rms(x, g):
    return x * lax.rsqrt(jnp.mean(x * x, axis=-1, keepdims=True) + EPS) * g


def _params(sem):
    return pltpu.CompilerParams(dimension_semantics=sem, vmem_limit_bytes=VMEM_LIMIT)


def _ada_kernel(c_ref, w_ref, b_ref, o_ref):
    c = c_ref[...]
    s = c * jax.nn.sigmoid(c)
    o_ref[...] = jnp.dot(s.astype(BF16), w_ref[...].astype(BF16), preferred_element_type=F32) + b_ref[...]


def ada_modulation(cond_pad, w, b):
    n = 6 * D_MODEL
    tn = 1536
    return pl.pallas_call(
        _ada_kernel,
        grid=(n // tn,),
        in_specs=[pl.BlockSpec((16, D_MODEL), lambda j: (0, 0)),
                  pl.BlockSpec((D_MODEL, tn), lambda j: (0, j)),
                  pl.BlockSpec((1, tn), lambda j: (0, j))],
        out_specs=pl.BlockSpec((16, tn), lambda j: (0, j)),
        out_shape=jax.ShapeDtypeStruct((16, n), F32),
        compiler_params=_params(("arbitrary",)),
        name="ada_modulation",
    )(cond_pad, w, b.reshape(1, n))


def _in_kernel(x_ref, mod_ref, g_ref, w_ref, z_ref):
    h = _rms(x_ref[...], g_ref[...]) * (1.0 + mod_ref[1:2, :]) + mod_ref[0:1, :]
    z_ref[...] = jnp.dot(h.astype(BF16), w_ref[...], preferred_element_type=F32)


def input_projection(x, mod, g, w):
    nt = N_TOK // TM
    return pl.pallas_call(
        _in_kernel,
        grid=(nt,),
        in_specs=[pl.BlockSpec((TM, D_MODEL), lambda i: (i, 0)),
                  pl.BlockSpec((None, 6, D_MODEL), lambda i: (_tile_cond(i, TM), 0, 0)),
                  pl.BlockSpec((1, D_MODEL), lambda i: (0, 0)),
                  pl.BlockSpec((D_MODEL, Z_COLS), lambda i: (0, 0))],
        out_specs=pl.BlockSpec((TM, Z_COLS), lambda i: (i, 0)),
        out_shape=jax.ShapeDtypeStruct((N_TOK, Z_COLS), F32),
        compiler_params=_params(("arbitrary",)),
        name="input_projection",
    )(x, mod, g.reshape(1, D_MODEL), w)


def _merge_kernel(x_ref, mod_ref, g_ref, ab_c_ref, c_c_ref, d_c_ref, a_l_ref, b_l_ref, c_l_ref, d_l_ref,
                  wgl_ref, wbr_ref, wout_ref, xo_ref):
    x = x_ref[...]
    h = (_rms(x, g_ref[...]) * (1.0 + mod_ref[1:2, :]) + mod_ref[0:1, :]).astype(BF16)
    is_ctx = pl.program_id(0) < CTX_TOK // TM
    branches = (jnp.where(is_ctx, ab_c_ref[:, :BRANCH_W], a_l_ref[...]),
                jnp.where(is_ctx, ab_c_ref[:, BRANCH_W:], b_l_ref[...]),
                jnp.where(is_ctx, c_c_ref[...], c_l_ref[...]),
                jnp.where(is_ctx, d_c_ref[...], d_l_ref[...]))
    merged = None
    for j in range(N_BRANCH):
        gate = jax.nn.sigmoid(jnp.dot(h, wgl_ref[j], preferred_element_type=F32))
        br = jnp.dot(branches[j], wbr_ref[j], preferred_element_type=F32)
        merged = gate * br if merged is None else merged + gate * br
    mix = jnp.dot(merged.astype(BF16), wout_ref[...], preferred_element_type=F32)
    xo_ref[...] = x + mod_ref[2:3, :] * mix


def merge_projection(x, mod, g, o_ctx, o_lat, wgl, wbr, wout):
    nt = N_TOK // TM
    ctx_tiles = CTX_TOK // TM
    ctx_spec = lambda w: pl.BlockSpec((TM, w), lambda i: (jnp.minimum(i, ctx_tiles - 1), 0))
    lat_spec = pl.BlockSpec((TM, BRANCH_W), lambda i: (jnp.maximum(i - ctx_tiles, 0), 0))
    return pl.pallas_call(
        _merge_kernel,
        grid=(nt,),
        in_specs=[pl.BlockSpec((TM, D_MODEL), lambda i: (i, 0)),
                  pl.BlockSpec((None, 6, D_MODEL), lambda i: (_tile_cond(i, TM), 0, 0)),
                  pl.BlockSpec((1, D_MODEL), lambda i: (0, 0)),
                  ctx_spec(2 * BRANCH_W), ctx_spec(BRANCH_W), ctx_spec(BRANCH_W),
                  lat_spec, lat_spec, lat_spec, lat_spec,
                  pl.BlockSpec((N_BRANCH, D_MODEL, D_MODEL), lambda i: (0, 0, 0)),
                  pl.BlockSpec((N_BRANCH, BRANCH_W, D_MODEL), lambda i: (0, 0, 0)),
                  pl.BlockSpec((D_MODEL, D_MODEL), lambda i: (0, 0))],
        out_specs=pl.BlockSpec((TM, D_MODEL), lambda i: (i, 0)),
        out_shape=jax.ShapeDtypeStruct((N_TOK, D_MODEL), F32),
        compiler_params=_params(("arbitrary",)),
        name="merge_projection",
    )(x, mod, g.reshape(1, D_MODEL), *o_ctx, *o_lat, wgl, wbr, wout)


def _ffn_kernel(x_ref, mod_ref, g_ref, w1_ref, w3_ref, w2_ref, xo_ref):
    x = x_ref[...]
    h = (_rms(x, g_ref[...]) * (1.0 + mod_ref[4:5, :]) + mod_ref[3:4, :]).astype(BF16)
    a = jnp.dot(h, w1_ref[...], preferred_element_type=F32)
    b = jnp.dot(h, w3_ref[...], preferred_element_type=F32)
    hid = (a * jax.nn.sigmoid(a) * b).astype(BF16)
    f = jnp.dot(hid, w2_ref[...], preferred_element_type=F32)
    xo_ref[...] = x + mod_ref[5:6, :] * f


def dense_ffn(x, mod, g, w1, w3, w2):
    nt = N_TOK // TM
    const = lambda i: (0, 0)
    return pl.pallas_call(
        _ffn_kernel,
        grid=(nt,),
        in_specs=[pl.BlockSpec((TM, D_MODEL), lambda i: (i, 0)),
                  pl.BlockSpec((None, 6, D_MODEL), lambda i: (_tile_cond(i, TM), 0, 0)),
                  pl.BlockSpec((1, D_MODEL), const),
                  pl.BlockSpec((D_MODEL, D_FF), const, pipeline_mode=pl.Buffered(1)),
                  pl.BlockSpec((D_MODEL, D_FF), const, pipeline_mode=pl.Buffered(1)),
                  pl.BlockSpec((D_FF, D_MODEL), const, pipeline_mode=pl.Buffered(1))],
        out_specs=pl.BlockSpec((TM, D_MODEL), lambda i: (i, 0)),
        out_shape=jax.ShapeDtypeStruct((N_TOK, D_MODEL), F32),
        compiler_params=_params(("arbitrary",)),
        name="dense_ffn",
    )(x, mod, g.reshape(1, D_MODEL), w1, w3, w2)


MOE_T = 1024
MOE_R = 128
MOE_F = 1792


def _router_kernel(x_ref, mod_ref, g_ref, rw_ref, rb_ref, h_ref, pos_ref, wgt_ref):
    t = x_ref.shape[0]
    h = _rms(x_ref[...], g_ref[...]) * (1.0 + mod_ref[4:5, :]) + mod_ref[3:4, :]
    h_ref[...] = h.astype(BF16)
    logits = jnp.dot(h, rw_ref[...], preferred_element_type=F32, precision=lax.Precision.HIGHEST) + rb_ref[...]
    lt = logits.T[:N_EXPERTS, :]
    eidx = lax.broadcasted_iota(jnp.int32, lt.shape, 0)
    m1 = jnp.max(lt, axis=0, keepdims=True)
    i1 = jnp.min(jnp.where(lt == m1, eidx, N_EXPERTS), axis=0, keepdims=True)
    rest = jnp.where(eidx == i1, F32_MIN, lt)
    m2 = jnp.max(rest, axis=0, keepdims=True)
    i2 = jnp.min(jnp.where(rest == m2, eidx, N_EXPERTS), axis=0, keepdims=True)
    e2 = jnp.exp(m2 - m1)
    p1 = 1.0 / (1.0 + e2)
    p2 = e2 / (1.0 + e2)
    wgt_ref[...] = jnp.where(eidx == i1, p1, 0.0) + jnp.where(eidx == i2, p2, 0.0)
    routed = jnp.where(eidx == i1, 1.0, jnp.where(eidx == i2, 1.0, 0.0))
    s_id = lax.broadcasted_iota(jnp.int32, (t, t), 0)
    t_id = lax.broadcasted_iota(jnp.int32, (t, t), 1)
    before = jnp.where(s_id < t_id, 1.0, 0.0).astype(BF16)
    rank = jnp.dot(routed.astype(BF16), before, preferred_element_type=F32)
    pos_ref[...] = jnp.where(routed > 0.0, rank.astype(jnp.int32), -1)


def moe_router(x, mod, g, rw, rb):
    nt = N_TOK // MOE_T
    return pl.pallas_call(
        _router_kernel,
        grid=(nt,),
        in_specs=[pl.BlockSpec((MOE_T, D_MODEL), lambda i: (i, 0)),
                  pl.BlockSpec((None, 6, D_MODEL), lambda i: (_tile_cond(i, MOE_T), 0, 0)),
                  pl.BlockSpec((1, D_MODEL), lambda i: (0, 0)),
                  pl.BlockSpec((D_MODEL, 128), lambda i: (0, 0)),
                  pl.BlockSpec((1, 128), lambda i: (0, 0))],
        out_specs=[pl.BlockSpec((MOE_T, D_MODEL), lambda i: (i, 0)),
                   pl.BlockSpec((N_EXPERTS, MOE_T), lambda i: (0, i)),
                   pl.BlockSpec((N_EXPERTS, MOE_T), lambda i: (0, i))],
        out_shape=[jax.ShapeDtypeStruct((N_TOK, D_MODEL), BF16),
                   jax.ShapeDtypeStruct((N_EXPERTS, N_TOK), jnp.int32),
                   jax.ShapeDtypeStruct((N_EXPERTS, N_TOK), F32)],
        compiler_params=_params(("arbitrary",)),
        name="moe_router",
    )(x, mod, g.reshape(1, D_MODEL), rw, rb)


def _moe_sparse_kernel(h_ref, pos_ref, wgt_ref, w1_ref, w3_ref, w2_ref, y_ref, xg_scr, acc_scr, wr_scr):
    e = pl.program_id(1)
    f = pl.program_id(2)
    t = h_ref.shape[0]
    pos_e = pos_ref[pl.ds(e, 1), :]
    n_blocks = (jnp.max(pos_e) + MOE_R) // MOE_R
    row_id = lax.broadcasted_iota(jnp.int32, (MOE_R, t), 0)

    def block_rows(r):
        return pl.ds(pl.multiple_of(r * MOE_R, MOE_R), MOE_R)

    def selects(r):
        return pos_e == row_id + r * MOE_R

    @pl.when((e == 0) & (f == 0))
    def _():
        y_ref[...] = jnp.zeros_like(y_ref)

    @pl.when(f == 0)
    def _():
        wgt_e = wgt_ref[pl.ds(e, 1), :]

        def gather(r, carry):
            sel = selects(r)
            xg = jnp.dot(jnp.where(sel, 1.0, 0.0).astype(BF16), h_ref[...], preferred_element_type=F32)
            xg_scr[block_rows(r), :] = xg.astype(BF16)
            w_rows = jnp.sum(jnp.where(sel, wgt_e, 0.0), axis=1, keepdims=True)
            wr_scr[block_rows(r), :] = jnp.broadcast_to(w_rows, (MOE_R, 128))
            acc_scr[block_rows(r), :] = jnp.zeros((MOE_R, D_MODEL), F32)
            return carry

        lax.fori_loop(0, n_blocks, gather, 0)

    def expert(r, carry):
        xg = xg_scr[block_rows(r), :]
        a = jnp.dot(xg, w1_ref[...], preferred_element_type=F32)
        b = jnp.dot(xg, w3_ref[...], preferred_element_type=F32)
        hid = (a * jax.nn.sigmoid(a) * b * wr_scr[block_rows(r), 0:1]).astype(BF16)
        acc_scr[block_rows(r), :] += jnp.dot(hid, w2_ref[...], preferred_element_type=F32)
        return carry

    lax.fori_loop(0, n_blocks, expert, 0)

    @pl.when(f == pl.num_programs(2) - 1)
    def _():
        def scatter(r, carry):
            onehot = jnp.where(selects(r), 1.0, 0.0).astype(BF16)
            hi, lo = _split_bf16(acc_scr[block_rows(r), :])
            y_ref[...] += lax.dot_general(jnp.concatenate([onehot, onehot], axis=0),
                                          jnp.concatenate([hi, lo], axis=0), _TN, preferred_element_type=F32)
            return carry

        lax.fori_loop(0, n_blocks, scatter, 0)


def moe_experts(h2, pos, wgt, w1, w3, w2):
    nt = N_TOK // MOE_T
    nf = D_FF_EXPERT // MOE_F
    return pl.pallas_call(
        _moe_sparse_kernel,
        grid=(nt, N_EXPERTS, nf),
        in_specs=[pl.BlockSpec((MOE_T, D_MODEL), lambda i, e, f: (i, 0)),
                  pl.BlockSpec((N_EXPERTS, MOE_T), lambda i, e, f: (0, i)),
                  pl.BlockSpec((N_EXPERTS, MOE_T), lambda i, e, f: (0, i)),
                  pl.BlockSpec((None, D_MODEL, MOE_F), lambda i, e, f: (e, 0, f)),
                  pl.BlockSpec((None, D_MODEL, MOE_F), lambda i, e, f: (e, 0, f)),
                  pl.BlockSpec((None, MOE_F, D_MODEL), lambda i, e, f: (e, f, 0))],
        out_specs=pl.BlockSpec((MOE_T, D_MODEL), lambda i, e, f: (i, 0)),
        out_shape=jax.ShapeDtypeStruct((N_TOK, D_MODEL), F32),
        scratch_shapes=[pltpu.VMEM((MOE_T, D_MODEL), BF16),
                        pltpu.VMEM((MOE_T, D_MODEL), F32),
                        pltpu.VMEM((MOE_T, 128), F32)],
        compiler_params=_params(("arbitrary", "arbitrary", "arbitrary")),
        name="moe_experts",
    )(h2, pos, wgt, w1, w3, w2)


def _residual_norm_kernel(x_ref, y_ref, mod_ref, g_ref, o_ref):
    o_ref[...] = _rms(x_ref[...] + mod_ref[5:6, :] * y_ref[...], g_ref[...])


def residual_final_norm(x, y, mod, g, row0, n_rows):
    tm = 1024
    t0 = row0 // tm
    return pl.pallas_call(
        _residual_norm_kernel,
        grid=(n_rows // tm,),
        in_specs=[pl.BlockSpec((tm, D_MODEL), lambda i: (t0 + i, 0)),
                  pl.BlockSpec((tm, D_MODEL), lambda i: (t0 + i, 0)),
                  pl.BlockSpec((None, 6, D_MODEL), lambda i: (_tile_cond(t0 + i, tm), 0, 0)),
                  pl.BlockSpec((1, D_MODEL), lambda i: (0, 0))],
        out_specs=pl.BlockSpec((tm, D_MODEL), lambda i: (i, 0)),
        out_shape=jax.ShapeDtypeStruct((n_rows, D_MODEL), F32),
        compiler_params=_params(("arbitrary",)),
        name="residual_final_norm",
    )(x, y, mod, g.reshape(1, D_MODEL))


ATT_SCALE = HEAD_DIM ** -0.5
LOG2_E = 1.4426950408889634
_NT = (((1,), (1,)), ((), ()))


def _head_rms(x, g_row):
    outs = []
    for h in range(x.shape[1] // HEAD_DIM):
        xh = x[:, h * HEAD_DIM:(h + 1) * HEAD_DIM]
        outs.append(xh * lax.rsqrt(jnp.mean(xh * xh, axis=-1, keepdims=True) + EPS) * g_row)
    return jnp.concatenate(outs, axis=-1)


def _rope_apply(x, c, s):
    w = x.shape[-1]
    lane = lax.broadcasted_iota(jnp.int32, x.shape, 1)
    first_half = ((lane // (HEAD_DIM // 4)) % 2) == 0
    partner = jnp.where(first_half, pltpu.roll(x, w - HEAD_DIM // 4, 1), pltpu.roll(x, HEAD_DIM // 4, 1))
    return x * c + partner * s


def rope_lane_tables(L):
    rows = L // GRID_W
    row = jnp.repeat(jnp.arange(rows, dtype=F32), GRID_W)
    col = jnp.tile(jnp.arange(GRID_W, dtype=F32), rows)
    n_freq = HEAD_DIM // 4
    inv = ROPE_BASE ** (-jnp.arange(n_freq, dtype=F32) / n_freq)
    ang = jnp.stack([row, col], 0)[:, :, None] * inv
    cos, sin = jnp.cos(ang), jnp.sin(ang)
    c = jnp.concatenate([cos[0], cos[0], cos[1], cos[1]], axis=-1)
    s = jnp.concatenate([-sin[0], sin[0], -sin[1], sin[1]], axis=-1)
    return jnp.tile(c, (1, 4)), jnp.tile(s, (1, 4))


def _with_ones(v):
    ones = jnp.ones((v.shape[0], HEAD_DIM), BF16)
    parts = []
    for h in range(v.shape[1] // HEAD_DIM):
        parts += [v[:, h * HEAD_DIM:(h + 1) * HEAD_DIM].astype(BF16), ones]
    return jnp.concatenate(parts, axis=-1)


def _attend_heads(jobs):
    for job in jobs:
        q = (job['q'] * (ATT_SCALE * LOG2_E)).astype(BF16)
        job['s'] = lax.dot_general(q, job['k'], _NT, preferred_element_type=F32)
        if job.get('extra') is not None:
            job['s2'] = lax.dot_general(q, job['extra'][0], _NT, preferred_element_type=F32)
    outs = []
    for job in jobs:
        s, sink = job['s'], job.get('sink')
        if sink is not None:
            sink = sink * LOG2_E
        if job.get('mask') is not None:
            s = jnp.where(job['mask'], s, NEG_INF)
        m = jnp.max(s, axis=-1, keepdims=True)
        if 's2' in job:
            m = jnp.maximum(m, jnp.max(job['s2'], axis=-1, keepdims=True))
        if sink is not None:
            m = jnp.maximum(m, sink)
        o = jnp.dot(jnp.exp2(s - m).astype(BF16), job['v'], preferred_element_type=F32)
        if 's2' in job:
            o = o + jnp.dot(jnp.exp2(job['s2'] - m).astype(BF16), job['extra'][1], preferred_element_type=F32)
        den = o[:, HEAD_DIM:HEAD_DIM + 1]
        if sink is not None:
            den = den + jnp.exp2(sink - m)
        outs.append(o[:, :HEAD_DIM] / den)
    return outs


def _ctx_attn_kernel(sink_ref, z_ref, gq_ref, gk_ref, o_ref, bk_ref):
    z = z_ref[...]
    bq = _head_rms(z[:, 512:768], gq_ref[...])
    bk = _head_rms(z[:, 768:896], gk_ref[...])
    bk_ref[...] = bk
    groups = ((z[:, 0:256], z[:, 256:384], z[:, 384:512], True),
              (bq, bk, z[:, 896:1024], False))
    outs = []
    for q_all, k_all, v_all, use_sink in groups:
        k_all = k_all.astype(BF16)
        v_all = _with_ones(v_all)
        for hq in range(A_HEADS):
            kv = hq // (A_HEADS // A_KV_HEADS)
            outs.append(dict(q=q_all[:, hq * HEAD_DIM:(hq + 1) * HEAD_DIM],
                             k=k_all[:, kv * HEAD_DIM:(kv + 1) * HEAD_DIM],
                             v=v_all[:, kv * 2 * HEAD_DIM:(kv + 1) * 2 * HEAD_DIM],
                             sink=sink_ref[hq] if use_sink else None))
    o_ref[...] = jnp.concatenate(_attend_heads(outs), axis=-1).astype(o_ref.dtype)


def ctx_attention(z, sink, gq, gk):
    return pl.pallas_call(
        _ctx_attn_kernel,
        grid=(BATCH,),
        in_specs=[pl.BlockSpec(memory_space=pltpu.SMEM),
                  pl.BlockSpec((SEQ, 1024), lambda b: (b, 0)),
                  pl.BlockSpec((1, HEAD_DIM), lambda b: (0, 0)),
                  pl.BlockSpec((1, HEAD_DIM), lambda b: (0, 0))],
        out_specs=[pl.BlockSpec((SEQ, 512), lambda b: (b, 0)),
                   pl.BlockSpec((SEQ, 128), lambda b: (b, 0))],
        out_shape=[jax.ShapeDtypeStruct((CTX_TOK, 512), BF16),
                   jax.ShapeDtypeStruct((CTX_TOK, 128), F32)],
        compiler_params=_params(("arbitrary",)),
        name="ctx_attention",
    )(sink, z, gq.reshape(1, HEAD_DIM), gk.reshape(1, HEAD_DIM))


LB_TQ = 256


def _lat_b_kernel(zq_ref, zkv_ref, ck_ref, cv_ref, cq_ref, sq_ref, ckk_ref, skk_ref, gq_ref, gk_ref,
                  o_ref, k_scr, v_scr):
    L = zkv_ref.shape[0]

    @pl.when(pl.program_id(1) == 0)
    def _():
        kv = zkv_ref[...]
        bk = _rope_apply(_head_rms(kv[:, :128], gk_ref[...]), ckk_ref[...], skk_ref[...])
        k_scr[0:L, :] = bk.astype(BF16)
        k_scr[L:L + PAST_LEN, :] = ck_ref[...].astype(BF16)
        v_scr[0:L, :] = _with_ones(kv[:, 128:])
        v_scr[L:L + PAST_LEN, :] = _with_ones(cv_ref[...])

    q = _rope_apply(_head_rms(zq_ref[...], gq_ref[...]), cq_ref[...], sq_ref[...])
    outs = []
    for hq in range(B_HEADS):
        kv = hq // (B_HEADS // B_KV_HEADS)
        sl = slice(kv * HEAD_DIM, (kv + 1) * HEAD_DIM)
        outs.append(dict(q=q[:, hq * HEAD_DIM:(hq + 1) * HEAD_DIM], k=k_scr[:, sl],
                         v=v_scr[:, kv * 2 * HEAD_DIM:(kv + 1) * 2 * HEAD_DIM]))
    o_ref[...] = jnp.concatenate(_attend_heads(outs), axis=-1).astype(o_ref.dtype)


def latent_attention_b(z, row0, nb, L, cache_k, cache_v, rope_c, rope_s, gq, gk):
    nq = L // LB_TQ
    return pl.pallas_call(
        _lat_b_kernel,
        grid=(nb, nq),
        in_specs=[pl.BlockSpec((LB_TQ, 256), lambda b, i: (row0 // LB_TQ + b * nq + i, 2)),
                  pl.BlockSpec((L, 256), lambda b, i: (row0 // L + b, 3)),
                  pl.BlockSpec((None, PAST_LEN, 128), lambda b, i: (b, 0, 0)),
                  pl.BlockSpec((None, PAST_LEN, 128), lambda b, i: (b, 0, 0)),
                  pl.BlockSpec((LB_TQ, 256), lambda b, i: (i, 0)),
                  pl.BlockSpec((LB_TQ, 256), lambda b, i: (i, 0)),
                  pl.BlockSpec((L, 128), lambda b, i: (0, 0)),
                  pl.BlockSpec((L, 128), lambda b, i: (0, 0)),
                  pl.BlockSpec((1, HEAD_DIM), lambda b, i: (0, 0)),
                  pl.BlockSpec((1, HEAD_DIM), lambda b, i: (0, 0))],
        out_specs=pl.BlockSpec((LB_TQ, 256), lambda b, i: (b * nq + i, 0)),
        out_shape=jax.ShapeDtypeStruct((nb * L, 256), BF16),
        scratch_shapes=[pltpu.VMEM((L + PAST_LEN, 128), BF16),
                        pltpu.VMEM((L + PAST_LEN, 256), BF16)],
        compiler_params=_params(("arbitrary", "arbitrary")),
        name="latent_attention_b",
    )(z, z, cache_k, cache_v, rope_c, rope_s, rope_c, rope_s, gq.reshape(1, HEAD_DIM), gk.reshape(1, HEAD_DIM))


def _lat_a_kernel(sink_ref, zq_ref, zkv_ref, ck_ref, cv_ref, cq_ref, sq_ref, ckk_ref, skk_ref,
                  o_ref, k_scr, v_scr, ck_scr, cv_scr):
    L = zkv_ref.shape[0]
    i = pl.program_id(1)

    @pl.when(i == 0)
    def _():
        kv = zkv_ref[...]
        k_scr[0:BLOCK, :] = jnp.zeros((BLOCK, 128), BF16)
        v_scr[0:BLOCK, :] = jnp.zeros((BLOCK, 256), BF16)
        k_scr[BLOCK:BLOCK + L, :] = _rope_apply(kv[:, :128], ckk_ref[...], skk_ref[...]).astype(BF16)
        v_scr[BLOCK:BLOCK + L, :] = _with_ones(kv[:, 128:])
        k_scr[BLOCK + L:2 * BLOCK + L, :] = jnp.zeros((BLOCK, 128), BF16)
        v_scr[BLOCK + L:2 * BLOCK + L, :] = jnp.zeros((BLOCK, 256), BF16)
        ck_scr[...] = ck_ref[...].astype(BF16)
        cv_scr[...] = _with_ones(cv_ref[...])

    q = _rope_apply(zq_ref[...], cq_ref[...], sq_ref[...])
    start = pl.multiple_of(i * BLOCK, BLOCK)
    kband = k_scr[pl.ds(start, 3 * BLOCK), :]
    vband = v_scr[pl.ds(start, 3 * BLOCK), :]
    r = lax.broadcasted_iota(jnp.int32, (BLOCK, 3 * BLOCK), 0)
    cidx = lax.broadcasted_iota(jnp.int32, (BLOCK, 3 * BLOCK), 1)
    kpos = i * BLOCK - BLOCK + cidx
    mask = (jnp.abs(cidx - BLOCK - r) <= WINDOW) & (kpos >= 0) & (kpos < L)
    outs = []
    for hq in range(A_HEADS):
        kv = hq // (A_HEADS // A_KV_HEADS)
        sl = slice(kv * HEAD_DIM, (kv + 1) * HEAD_DIM)
        sv = slice(kv * 2 * HEAD_DIM, (kv + 1) * 2 * HEAD_DIM)
        outs.append(dict(q=q[:, hq * HEAD_DIM:(hq + 1) * HEAD_DIM], k=kband[:, sl], v=vband[:, sv],
                         extra=(ck_scr[:, sl], cv_scr[:, sv]), sink=sink_ref[hq], mask=mask))
    o_ref[...] = jnp.concatenate(_attend_heads(outs), axis=-1).astype(o_ref.dtype)


def latent_attention_a(z, row0, nb, L, sink, cache_k, cache_v, rope_c, rope_s):
    nq = L // BLOCK
    return pl.pallas_call(
        _lat_a_kernel,
        grid=(nb, nq),
        in_specs=[pl.BlockSpec(memory_space=pltpu.SMEM),
                  pl.BlockSpec((BLOCK, 256), lambda b, i: (row0 // BLOCK + b * nq + i, 0)),
                  pl.BlockSpec((L, 256), lambda b, i: (row0 // L + b, 1)),
                  pl.BlockSpec((None, PAST_LEN, 128), lambda b, i: (b, 0, 0)),
                  pl.BlockSpec((None, PAST_LEN, 128), lambda b, i: (b, 0, 0)),
                  pl.BlockSpec((BLOCK, 256), lambda b, i: (i, 0)),
                  pl.BlockSpec((BLOCK, 256), lambda b, i: (i, 0)),
                  pl.BlockSpec((L, 128), lambda b, i: (0, 0)),
                  pl.BlockSpec((L, 128), lambda b, i: (0, 0))],
        out_specs=pl.BlockSpec((BLOCK, 256), lambda b, i: (b * nq + i, 0)),
        out_shape=jax.ShapeDtypeStruct((nb * L, 256), BF16),
        scratch_shapes=[pltpu.VMEM((L + 2 * BLOCK, 128), BF16),
                        pltpu.VMEM((L + 2 * BLOCK, 256), BF16),
                        pltpu.VMEM((PAST_LEN, 128), BF16),
                        pltpu.VMEM((PAST_LEN, 256), BF16)],
        compiler_params=_params(("arbitrary", "arbitrary")),
        name="latent_attention_a",
    )(sink, z, z, cache_k, cache_v, rope_c, rope_s, rope_c, rope_s)


_TN = (((0,), (0,)), ((), ()))
HG_GROUP = 4


def _hgrn_kernel(zq_ref, zf_ref, zi_ref, zg_ref, lb_ref, gn_ref, s0_ref, o_ref, sT_ref,
                 of_scr, ob_scr, g_scr, k_scr, qin_scr, kin_scr, v_scr, S_scr, *, tt):
    d = pl.program_id(1)
    j = pl.program_id(2)
    n_t = pl.num_programs(2)
    C = HGRN_CHUNK
    n_c = tt // C

    @pl.when(j == 0)
    def _():
        S_scr[...] = s0_ref[...]

    lb = lb_ref[...]
    sg = jax.nn.sigmoid(zf_ref[...])
    logf = jnp.log(lb + (1.0 - lb) * sg)
    k = (1.0 - lb) * (1.0 - sg)
    k_scr[...] = k
    v_scr[...] = zi_ref[...].astype(BF16)
    in_chunk = lax.broadcasted_iota(jnp.int32, (tt, C_HEADS * HEAD_DIM), 0) % C
    row = lax.broadcasted_iota(jnp.int32, (C, C), 0)
    col = lax.broadcasted_iota(jnp.int32, (C, C), 1)
    heads = [slice(h * HEAD_DIM, (h + 1) * HEAD_DIM) for h in range(C_HEADS)]

    def run(reverse, tile):
        G = logf
        step = 1
        while step < C:
            if reverse:
                G = G + jnp.where(in_chunk < C - step, pltpu.roll(G, tt - step, 0), 0.0)
            else:
                G = G + jnp.where(in_chunk >= step, pltpu.roll(G, step, 0), 0.0)
            step *= 2
        g_scr[...] = G
        qin_scr[...] = (zq_ref[...] * jnp.exp(G)).astype(BF16)
        kin_scr[...] = (k * jnp.exp(-G)).astype(BF16)
        tri = (row <= col) if reverse else (row >= col)

        def body(gi, carry):
            chunks = []
            for g in range(HG_GROUP):
                ci = gi * HG_GROUP + g
                c = (n_c - 1 - ci) if reverse else ci
                r0 = pl.multiple_of(c * C, C)
                rows = pl.ds(r0, C)
                G_c = g_scr[rows, :]
                G_end = G_c[0:1, :] if reverse else G_c[C - 1:C, :]
                vc = v_scr[rows, :]
                q_in = qin_scr[rows, :]
                k_in = kin_scr[rows, :]
                k_out = (k_scr[rows, :] * jnp.exp(G_end - G_c)).astype(BF16)
                chunks.append(dict(
                    r0=r0, rows=rows, vc=vc, q_in=q_in, decay=jnp.exp(G_end),
                    a=[lax.dot_general(q_in[:, sl], k_in[:, sl], _NT, preferred_element_type=F32) for sl in heads],
                    kv=[lax.dot_general(vc[:, sl], k_out[:, sl], _TN, preferred_element_type=F32) for sl in heads]))
            s_cur = [S_scr[h] for h in range(C_HEADS)]
            for ch in chunks:
                ch['qs'] = [lax.dot_general(ch['q_in'][:, sl], s_cur[h].astype(BF16), _NT,
                                            preferred_element_type=F32) for h, sl in enumerate(heads)]
                s_cur = [s_cur[h] * ch['decay'][:, sl] + ch['kv'][h] for h, sl in enumerate(heads)]
            for h in range(C_HEADS):
                S_scr[h] = s_cur[h]
            for ch in chunks:
                o_c = jnp.concatenate(
                    [jnp.dot(jnp.where(tri, ch['a'][h], 0.0).astype(BF16), ch['vc'][:, sl],
                             preferred_element_type=F32) + ch['qs'][h] for h, sl in enumerate(heads)], axis=-1)
                if reverse:
                    ob_scr[ch['rows'], :] = o_c
                else:
                    of_scr[pl.ds(pl.multiple_of(tile * tt, tt) + ch['r0'], C), :] = o_c
            return carry

        lax.fori_loop(0, n_c // HG_GROUP, body, 0)

    @pl.when(d == 0)
    def _():
        run(False, j)

    @pl.when(d == 1)
    def _():
        tile = n_t - 1 - j
        run(True, tile)
        o = of_scr[pl.ds(pl.multiple_of(tile * tt, tt), tt), :] + ob_scr[...]
        g = zg_ref[...]
        o_ref[...] = (_head_rms(o, gn_ref[...]) * (g * jax.nn.sigmoid(g))).astype(o_ref.dtype)

    @pl.when(j == n_t - 1)
    def _():
        sT_ref[...] = S_scr[...]


def hgrn_mixer(z, row0, nb, L, lb, gn, s0_t):
    tt = min(L, 512)
    n_t = L // tt
    rb = row0 // tt

    def tile(d, j):
        return jnp.where(d == 0, j, n_t - 1 - j)

    def late(d, j):
        return jnp.where(d == 0, n_t - 1, n_t - 1 - j)

    st_spec = pl.BlockSpec((None, None, C_HEADS, HEAD_DIM, HEAD_DIM), lambda b, d, j: (b, d, 0, 0, 0))
    return pl.pallas_call(
        functools.partial(_hgrn_kernel, tt=tt),
        grid=(nb, 2, n_t),
        in_specs=[pl.BlockSpec((tt, 256), lambda b, d, j: (rb + b * n_t + tile(d, j), 4)),
                  pl.BlockSpec((tt, 256), lambda b, d, j: (rb + b * n_t + tile(d, j), 5 + d)),
                  pl.BlockSpec((tt, 256), lambda b, d, j: (rb + b * n_t + tile(d, j), 7)),
                  pl.BlockSpec((tt, 256), lambda b, d, j: (rb + b * n_t + late(d, j), 8)),
                  pl.BlockSpec((1, 256), lambda b, d, j: (0, 0)),
                  pl.BlockSpec((1, HEAD_DIM), lambda b, d, j: (0, 0)),
                  st_spec],
        out_specs=[pl.BlockSpec((tt, 256), lambda b, d, j: (b * n_t + late(d, j), 0)),
                   st_spec],
        out_shape=[jax.ShapeDtypeStruct((nb * L, 256), BF16),
                   jax.ShapeDtypeStruct((nb, 2, C_HEADS, HEAD_DIM, HEAD_DIM), F32)],
        scratch_shapes=[pltpu.VMEM((L, 256), F32),
                        pltpu.VMEM((tt, 256), F32),
                        pltpu.VMEM((tt, 256), F32),
                        pltpu.VMEM((tt, 256), F32),
                        pltpu.VMEM((tt, 256), BF16),
                        pltpu.VMEM((tt, 256), BF16),
                        pltpu.VMEM((tt, 256), BF16),
                        pltpu.VMEM((C_HEADS, HEAD_DIM, HEAD_DIM), F32)],
        compiler_params=_params(("arbitrary", "arbitrary", "arbitrary")),
        name="hgrn_mixer",
    )(z, z, z, z, lb.reshape(1, 256), gn.reshape(1, HEAD_DIM), s0_t)


DL_C = DELTA_CHUNK
DL_PREP_TT = 256
DL_HALO = 8
DL_CHUNK_TT = 512
N_QKV_HEADS = 3 * D_HEADS
DL_HL_LEVELS = 3
DL_GROUP = 2


def _delta_prep_kernel(x_ref, xp_ref, xn_ref, zab_ref, cw_ref, na_ref, dtb_ref, qkv_ref, gate_ref, xs_scr):
    tt = x_ref.shape[0]
    row = pl.program_id(0) * tt
    lat = row - CTX_TOK
    first = jnp.where(row < CTX_TOK, True, lat % DEC_SEQ == 0)
    last = jnp.where(row < CTX_TOK, True, (lat + tt) % DEC_SEQ == 0)
    xs_scr[DL_HALO:DL_HALO + tt, :] = x_ref[...]
    xs_scr[0:DL_HALO, :] = jnp.where(first, 0.0, xp_ref[...])
    xs_scr[DL_HALO + tt:2 * DL_HALO + tt, :] = jnp.where(last, 0.0, xn_ref[...])
    pad = (CONV_K - 1) // 2
    y = None
    for t in range(CONV_K):
        term = xs_scr[pl.ds(DL_HALO - pad + t, tt), :] * cw_ref[t:t + 1, :]
        y = term if y is None else y + term
    y = y * jax.nn.sigmoid(y)
    for idx in range(N_QKV_HEADS):
        xh = y[:, idx * HEAD_DIM:(idx + 1) * HEAD_DIM]
        if idx < 2 * D_HEADS:
            xh = xh * lax.rsqrt(jnp.sum(xh * xh, axis=-1, keepdims=True) + EPS)
        if idx < D_HEADS:
            xh = xh * ATT_SCALE
        qkv_ref[idx] = xh
    zab = zab_ref[...]
    lane = lax.broadcasted_iota(jnp.int32, zab.shape, 1)
    t_ = zab + dtb_ref[...]
    softplus = jnp.maximum(t_, 0.0) + jnp.log(1.0 + jnp.exp(-jnp.abs(t_)))
    gate_ref[...] = jnp.where(lane < 2 * D_HEADS, na_ref[...] * softplus, jax.nn.sigmoid(zab))


def delta_prep(z, conv_w, a_log, dt_bias):
    tt = DL_PREP_TT
    hb = tt // DL_HALO
    n_hb = N_TOK // DL_HALO
    pad8 = lambda v: jnp.concatenate([v.reshape(1, 2 * D_HEADS), jnp.zeros((1, 128 - 2 * D_HEADS), F32)], axis=1)
    return pl.pallas_call(
        _delta_prep_kernel,
        grid=(N_TOK // tt,),
        in_specs=[pl.BlockSpec((tt, 768), lambda i: (i, 3)),
                  pl.BlockSpec((DL_HALO, 768), lambda i: (jnp.maximum(i * hb - 1, 0), 3)),
                  pl.BlockSpec((DL_HALO, 768), lambda i: (jnp.minimum((i + 1) * hb, n_hb - 1), 3)),
                  pl.BlockSpec((tt, 128), lambda i: (i, Z_DAB // 128)),
                  pl.BlockSpec((CONV_K, 768), lambda i: (0, 0)),
                  pl.BlockSpec((1, 128), lambda i: (0, 0)),
                  pl.BlockSpec((1, 128), lambda i: (0, 0))],
        out_specs=[pl.BlockSpec((N_QKV_HEADS, tt, HEAD_DIM), lambda i: (0, i, 0)),
                   pl.BlockSpec((tt, 128), lambda i: (i, 0))],
        out_shape=[jax.ShapeDtypeStruct((N_QKV_HEADS, N_TOK, HEAD_DIM), F32),
                   jax.ShapeDtypeStruct((N_TOK, 128), F32)],
        scratch_shapes=[pltpu.VMEM((tt + 2 * DL_HALO, 768), F32)],
        compiler_params=_params(("arbitrary",)),
        name="delta_prep",
    )(z, z, z, z, conv_w, pad8(-jnp.exp(a_log)), pad8(dt_bias))


def _split_bf16(a):
    hi = a.astype(BF16)
    return hi, (a - hi.astype(F32)).astype(BF16)


def _dot_hl(a_parts, b_parts):
    (a_hi, a_lo), (b_hi, b_lo) = a_parts, b_parts
    m = a_hi.shape[0]
    r = jnp.dot(jnp.concatenate([a_hi, a_lo], axis=0), b_hi, preferred_element_type=F32)
    return r[:m] + r[m:] + jnp.dot(a_hi, b_lo, preferred_element_type=F32)


def _delta_chunk_kernel(qkv_ref, gate_ref, u2_ref, wq_ref, ak_ref):
    C = DL_C
    n_c = gate_ref.shape[0] // C
    row = lax.broadcasted_iota(jnp.int32, (C, C), 0)
    col = lax.broadcasted_iota(jnp.int32, (C, C), 1)
    eye = (row == col).astype(F32)
    t_idx = lax.broadcasted_iota(jnp.int32, (C, 128), 0)

    def chunk_chains(c):
        r0 = pl.multiple_of(c * C, C)
        ga = gate_ref[pl.ds(r0, C), :]
        chains = []
        for d in range(2):
            incl = (row >= col) if d == 0 else (row <= col)
            strict = (row > col) if d == 0 else (row < col)
            g_all = ga
            step = 1
            while step < C:
                if d == 0:
                    g_all = g_all + jnp.where(t_idx >= step, pltpu.roll(g_all, step, 0), 0.0)
                else:
                    g_all = g_all + jnp.where(t_idx < C - step, pltpu.roll(g_all, C - step, 0), 0.0)
                step *= 2
            g_all_t = g_all.T
            for h in range(D_HEADS):
                ci = d * D_HEADS + h
                q = qkv_ref[h, pl.ds(r0, C), :]
                k = qkv_ref[D_HEADS + h, pl.ds(r0, C), :]
                v = qkv_ref[2 * D_HEADS + h, pl.ds(r0, C), :]
                g_col = g_all[:, ci:ci + 1]
                g_row = g_all_t[ci:ci + 1, :]
                beta = ga[:, 2 * D_HEADS + ci:2 * D_HEADS + ci + 1]
                g_end = g_col[C - 1:C, :] if d == 0 else g_col[0:1, :]
                kb = k * beta
                eg = jnp.exp(g_col)
                decay = jnp.where(incl, jnp.exp(jnp.where(incl, g_col - g_row, 0.0)), 0.0)
                kq = jnp.concatenate([kb, q], axis=0).astype(BF16)
                chains.append(dict(
                    strict=strict, decay=decay, qg=q * eg, g_end=g_end,
                    r=lax.dot_general(kq, k.astype(BF16), _NT, preferred_element_type=F32),
                    rhs=jnp.concatenate([v * beta, kb * eg], axis=1).astype(BF16),
                    ke_t=(k * jnp.exp(g_end - g_col)).T))
        return r0, chains

    def body(gi, carry):
        groups = [(gi * DL_GROUP + cc,) + chunk_chains(gi * DL_GROUP + cc) for cc in range(DL_GROUP)]
        chains = [ch for _, _, chs in groups for ch in chs]
        for ch in chains:
            ch['p'] = -jnp.where(ch['strict'], ch['r'][:C] * ch['decay'], 0.0)
            ch['t'] = eye + ch['p']
        for level in range(5):
            if level < DL_HL_LEVELS:
                for ch in chains:
                    parts = _split_bf16(ch['p'])
                    ch['p'] = _dot_hl(parts, parts)
                for ch in chains:
                    ch['t'] = ch['t'] + _dot_hl(_split_bf16(ch['t']), _split_bf16(ch['p']))
            else:
                for ch in chains:
                    pb = ch['p'].astype(BF16)
                    ch['p'] = jnp.dot(pb, pb, preferred_element_type=F32)
                for ch in chains:
                    ch['t'] = ch['t'] + jnp.dot(ch['t'].astype(BF16), ch['p'].astype(BF16),
                                                preferred_element_type=F32)
        for ch in chains:
            ch['uw'] = jnp.dot(ch['t'].astype(BF16), ch['rhs'], preferred_element_type=F32)
        pack = lambda xs: jnp.stack(xs).reshape((2, D_HEADS) + xs[0].shape)
        for c, r0, chs in groups:
            u2 = [jnp.concatenate([ch['uw'][:, :C], jnp.broadcast_to(jnp.exp(ch['g_end']), (C, C))], axis=1)
                  for ch in chs]
            wq = [jnp.concatenate([ch['uw'][:, C:], ch['qg']], axis=0).astype(BF16) for ch in chs]
            ak = [jnp.concatenate([ch['r'][C:] * ch['decay'], ch['ke_t']], axis=0).astype(BF16) for ch in chs]
            u2_ref[:, :, pl.ds(r0, C), :] = pack(u2)
            wq_ref[:, :, c] = pack(wq)
            ak_ref[:, :, c] = pack(ak)
        return carry

    lax.fori_loop(0, n_c // DL_GROUP, body, 0)


def delta_chunks(qkv, gates):
    tt = DL_CHUNK_TT
    n_c = tt // DL_C
    return pl.pallas_call(
        _delta_chunk_kernel,
        grid=(N_TOK // tt,),
        in_specs=[pl.BlockSpec((N_QKV_HEADS, tt, HEAD_DIM), lambda i: (0, i, 0)),
                  pl.BlockSpec((tt, 128), lambda i: (i, 0))],
        out_specs=[pl.BlockSpec((2, D_HEADS, tt, 128), lambda i: (0, 0, i, 0)),
                   pl.BlockSpec((2, D_HEADS, n_c, 2 * DL_C, HEAD_DIM), lambda i: (0, 0, i, 0, 0)),
                   pl.BlockSpec((2, D_HEADS, n_c, 2 * DL_C, HEAD_DIM), lambda i: (0, 0, i, 0, 0))],
        out_shape=[jax.ShapeDtypeStruct((2, D_HEADS, N_TOK, 128), F32),
                   jax.ShapeDtypeStruct((2, D_HEADS, N_TOK // DL_C, 2 * DL_C, HEAD_DIM), BF16),
                   jax.ShapeDtypeStruct((2, D_HEADS, N_TOK // DL_C, 2 * DL_C, HEAD_DIM), BF16)],
        compiler_params=_params(("arbitrary",)),
        name="delta_chunks",
    )(qkv, gates)


def _delta_scan_kernel(u2f_ref, wqf_ref, akf_ref, u2b_ref, wqb_ref, akb_ref, s0_ref,
                       of_ref, ob_ref, s_ref, s_scr):
    j = pl.program_id(1)
    C = DL_C
    n_c = wqf_ref.shape[1]

    @pl.when(j == 0)
    def _():
        s_scr[...] = s0_ref[...]

    def body(ci, carry):
        chains = []
        for d, (u2_ref, wq_ref, ak_ref, o_ref) in enumerate(((u2f_ref, wqf_ref, akf_ref, of_ref),
                                                            (u2b_ref, wqb_ref, akb_ref, ob_ref))):
            c = ci if d == 0 else n_c - 1 - ci
            r0 = pl.multiple_of(c * C, C)
            for h in range(D_HEADS):
                s = s_scr[d, h]
                chains.append(dict(d=d, h=h, c=c, r0=r0, s=s, ak_ref=ak_ref, o_ref=o_ref,
                                   u2=u2_ref[h, pl.ds(r0, C), :],
                                   r1=jnp.dot(wq_ref[h, c], s.astype(BF16),
                                              preferred_element_type=F32)))
        for ch in chains:
            v_new = ch['u2'][:, :C] - ch['r1'][:C]
            ch['r2'] = jnp.dot(ch['ak_ref'][ch['h'], ch['c']], v_new.astype(BF16),
                               preferred_element_type=F32)
        for ch in chains:
            ch['o_ref'][ch['h'], pl.ds(ch['r0'], C), :] = ch['r1'][C:] + ch['r2'][:C]
            s_scr[ch['d'], ch['h']] = ch['s'] * ch['u2'][0:1, C:] + ch['r2'][C:]
        return carry

    lax.fori_loop(0, n_c, body, 0)

    @pl.when(j == pl.num_programs(1) - 1)
    def _():
        s_ref[...] = s_scr[...]


def delta_scan(u2, wq, ak, row0, nb, L, s0):
    tt = min(L, 512)
    n_t = L // tt
    n_c = tt // DL_C
    rb = row0 // tt
    fwd = lambda b, j: rb + b * n_t + j
    bwd = lambda b, j: rb + b * n_t + (n_t - 1 - j)
    u_spec = lambda d, f: pl.BlockSpec((None, D_HEADS, tt, 128), lambda b, j: (d, 0, f(b, j), 0))
    c_spec = lambda d, f: pl.BlockSpec((None, D_HEADS, n_c, 2 * DL_C, HEAD_DIM), lambda b, j: (d, 0, f(b, j), 0, 0))
    st_spec = pl.BlockSpec((None, 2, D_HEADS, HEAD_DIM, HEAD_DIM), lambda b, j: (b, 0, 0, 0, 0))
    return pl.pallas_call(
        _delta_scan_kernel,
        grid=(nb, n_t),
        in_specs=[u_spec(0, fwd), c_spec(0, fwd), c_spec(0, fwd),
                  u_spec(1, bwd), c_spec(1, bwd), c_spec(1, bwd), st_spec],
        out_specs=[pl.BlockSpec((D_HEADS, tt, HEAD_DIM), lambda b, j: (0, b * n_t + j, 0)),
                   pl.BlockSpec((D_HEADS, tt, HEAD_DIM), lambda b, j: (0, b * n_t + (n_t - 1 - j), 0)),
                   st_spec],
        out_shape=[jax.ShapeDtypeStruct((D_HEADS, nb * L, HEAD_DIM), F32),
                   jax.ShapeDtypeStruct((D_HEADS, nb * L, HEAD_DIM), F32),
                   jax.ShapeDtypeStruct((nb, 2, D_HEADS, HEAD_DIM, HEAD_DIM), F32)],
        scratch_shapes=[pltpu.VMEM((2, D_HEADS, HEAD_DIM, HEAD_DIM), F32)],
        compiler_params=_params(("arbitrary", "arbitrary")),
        name="delta_scan",
    )(u2, wq, ak, u2, wq, ak, s0)


def _delta_out_kernel(of_ref, ob_ref, zg_ref, gn_ref, o_ref):
    outs = []
    for h in range(D_HEADS):
        o = of_ref[h] + ob_ref[h]
        outs.append(o * lax.rsqrt(jnp.mean(o * o, axis=-1, keepdims=True) + EPS) * gn_ref[...])
    g = zg_ref[...]
    o_ref[...] = (jnp.concatenate(outs, axis=-1) * (g * jax.nn.sigmoid(g))).astype(o_ref.dtype)


def delta_output(o_f, o_b, z, row0, gn):
    n = o_f.shape[1]
    tt = 256
    return pl.pallas_call(
        _delta_out_kernel,
        grid=(n // tt,),
        in_specs=[pl.BlockSpec((D_HEADS, tt, HEAD_DIM), lambda i: (0, i, 0)),
                  pl.BlockSpec((D_HEADS, tt, HEAD_DIM), lambda i: (0, i, 0)),
                  pl.BlockSpec((tt, 256), lambda i: (row0 // tt + i, Z_DG // 256)),
                  pl.BlockSpec((1, HEAD_DIM), lambda i: (0, 0))],
        out_specs=pl.BlockSpec((tt, 256), lambda i: (i, 0)),
        out_shape=jax.ShapeDtypeStruct((n, 256), BF16),
        compiler_params=_params(("arbitrary",)),
        name="delta_output",
    )(o_f, o_b, z, gn.reshape(1, HEAD_DIM))


def kernel(x_prompt, x_sample, cache_attn_a_k, cache_attn_a_v, cache_attn_b_k, cache_attn_b_v,
           state_hgrn, state_delta, c, c_ctx, norm1_g, norm2_g, w_ada, b_ada, w_in, a_sink,
           b_qnorm_g, b_knorm_g, c_lb, c_onorm_g, d_conv, d_a_log, d_dt_bias, d_onorm_g,
           w_branch, w_out, ffn_w1, ffn_w3, ffn_w2, router_w, router_b, moe_w1, moe_w3, moe_w2,
           final_norm_g):
    cum = jnp.cumsum(jax.nn.softmax(c_lb, axis=0), axis=0)
    lower_bounds = cum - cum[:1]

    x = jnp.concatenate([x_prompt.reshape(CTX_TOK, D_MODEL), x_sample.reshape(LAT_TOK, D_MODEL)], axis=0)
    cond = jnp.concatenate([c_ctx[None, :], c, jnp.zeros((16 - N_COND, D_MODEL), F32)], axis=0)

    rope_c, rope_s = rope_lane_tables(DEC_SEQ)
    caches = []
    for l in range(DEPTH):
        mod = ada_modulation(cond, w_ada[l], b_ada[l])[:N_COND].reshape(N_COND, 6, D_MODEL)
        w_mix = jnp.concatenate([w_in[l][:, :Z_MAIN], w_in[l][:, Z_MAIN + 16:W_IN_MIX],
                                 w_in[l][:, Z_MAIN:Z_MAIN + 16], jnp.zeros((D_MODEL, 128 - 16), F32)],
                                axis=1).astype(BF16)
        w_gl = w_in[l][:, W_IN_MIX:].reshape(D_MODEL, N_BRANCH, D_MODEL).transpose(1, 0, 2).astype(BF16)
        z = input_projection(x, mod, norm1_g[l], w_mix)
        kv2 = lambda t: t.reshape(DEC_BATCH, PAST_LEN, 128)
        o_ab_ctx, bk_ctx = ctx_attention(z, a_sink[l], b_qnorm_g[l], b_knorm_g[l])
        o_a_lat = latent_attention_a(z, CTX_TOK, DEC_BATCH, DEC_SEQ, a_sink[l], kv2(cache_attn_a_k[:, l]),
                                     kv2(cache_attn_a_v[:, l]), rope_c, rope_s)
        o_b_lat = latent_attention_b(z, CTX_TOK, DEC_BATCH, DEC_SEQ, kv2(cache_attn_b_k[:, l]),
                                     kv2(cache_attn_b_v[:, l]), rope_c, rope_s, b_qnorm_g[l], b_knorm_g[l])
        o_c_ctx, sc_t = hgrn_mixer(z, 0, BATCH, SEQ, lower_bounds[l], c_onorm_g[l],
                                   jnp.zeros((BATCH, 2, C_HEADS, HEAD_DIM, HEAD_DIM), F32))
        o_c_lat, _ = hgrn_mixer(z, CTX_TOK, DEC_BATCH, DEC_SEQ, lower_bounds[l], c_onorm_g[l],
                                jnp.swapaxes(state_hgrn[:, l], -1, -2))

        qkv, gates = delta_prep(z, d_conv[l], d_a_log[l], d_dt_bias[l])
        u2, wq, ak = delta_chunks(qkv, gates)
        of_ctx, ob_ctx, sd = delta_scan(u2, wq, ak, 0, BATCH, SEQ,
                                        jnp.zeros((BATCH, 2, D_HEADS, HEAD_DIM, HEAD_DIM), F32))
        of_lat, ob_lat, _ = delta_scan(u2, wq, ak, CTX_TOK, DEC_BATCH, DEC_SEQ, state_delta[:, l])
        o_d_ctx = delta_output(of_ctx, ob_ctx, z, 0, d_onorm_g[l])
        o_d_lat = delta_output(of_lat, ob_lat, z, CTX_TOK, d_onorm_g[l])
        kvh = lambda t: t.reshape(BATCH, SEQ, 2, HEAD_DIM)
        caches.append((kvh(z[:CTX_TOK, 256:384]), kvh(z[:CTX_TOK, 384:512]), kvh(bk_ctx), kvh(z[:CTX_TOK, 896:1024]),
                       jnp.swapaxes(sc_t, -1, -2), sd))
        x = merge_projection(x, mod, norm1_g[l], (o_ab_ctx, o_c_ctx, o_d_ctx), (o_a_lat, o_b_lat, o_c_lat, o_d_lat),
                             w_gl, w_branch[l].astype(BF16), w_out[l].astype(BF16))
        j = l // 2
        if l % 2 == 0:
            x = dense_ffn(x, mod, norm2_g[l], ffn_w1[j].astype(BF16), ffn_w3[j].astype(BF16),
                          ffn_w2[j].astype(BF16))
        else:
            assert l == DEPTH - 1, "the expert layer's residual is fused with the final norm"
            rw = jnp.concatenate([router_w[j], jnp.zeros((D_MODEL, 128 - N_EXPERTS), F32)], axis=1)
            rb = jnp.concatenate([router_b[j], jnp.zeros((128 - N_EXPERTS,), F32)])[None, :]
            h2, pos, wgt = moe_router(x, mod, norm2_g[l], rw, rb)
            f = moe_experts(h2, pos, wgt, moe_w1[j].astype(BF16), moe_w3[j].astype(BF16), moe_w2[j].astype(BF16))
            y_prompt = residual_final_norm(x, f, mod, final_norm_g, 0, CTX_TOK).reshape(BATCH, SEQ, D_MODEL)
            y_sample = residual_final_norm(x, f, mod, final_norm_g, CTX_TOK, LAT_TOK).reshape(DEC_BATCH, DEC_SEQ, D_MODEL)

    stack = lambda idx: jnp.stack([caches[l][idx] for l in range(DEPTH)], axis=1)
    return (y_prompt, y_sample, stack(0), stack(1), stack(2), stack(3), stack(4), stack(5))
```

```python
import functools

import jax
import jax.numpy as jnp
import numpy as np
from jax import lax
from jax.experimental import pallas as pl
from jax.experimental.pallas import tpu as pltpu

F32 = jnp.float32
BF16 = jnp.bfloat16

D_MODEL = 1024
BATCH = 32
SEQ = 256
DEPTH = 2
DEC_BATCH = 8
DEC_SEQ = 4096
PAST_LEN = 256
GRID_W = 64
HEAD_DIM = 64
A_HEADS = 4
A_KV_HEADS = 2
B_HEADS = 4
B_KV_HEADS = 2
C_HEADS = 4
D_HEADS = 4
BRANCH_W = 256
N_BRANCH = 4
WINDOW = 128
BLOCK = 128
ROPE_BASE = 10000.0
HGRN_CHUNK = 32
DELTA_CHUNK = 64
CONV_K = 5
D_FF = 2816
N_EXPERTS = 8
D_FF_EXPERT = 3584
EPS = 1e-6
NEG_INF = -1e30
F32_MIN = float(np.finfo(np.float32).min)

CTX_TOK = BATCH * SEQ
LAT_TOK = DEC_BATCH * DEC_SEQ
N_TOK = CTX_TOK + LAT_TOK
N_COND = 1 + DEC_BATCH

Z_MAIN = 3072
Z_DG = Z_MAIN
Z_DAB = Z_DG + BRANCH_W
Z_COLS = Z_DAB + 128
W_IN_MIX = 3344

TM = 512
VMEM_LIMIT = 56 * 1024 * 1024


def _tile_cond(i, tm):
    ctx_tiles = CTX_TOK // tm
    per_b = DEC_SEQ // tm
    return jnp.where(i < ctx_tiles, 0, 1 + (i - ctx_tiles) // per_b)


def _rms(x, g):
    return x * lax.rsqrt(jnp.mean(x * x, axis=-1, keepdims=True) + EPS) * g


def _params(sem):
    return pltpu.CompilerParams(dimension_semantics=sem, vmem_limit_bytes=VMEM_LIMIT)


def _ada_kernel(c_ref, w_ref, b_ref, o_ref):
    c = c_ref[...]
    s = c * jax.nn.sigmoid(c)
    o_ref[...] = jnp.dot(s.astype(BF16), w_ref[...].astype(BF16), preferred_element_type=F32) + b_ref[...]


def ada_modulation(cond_pad, w, b):
    n = 6 * D_MODEL
    tn = 1536
    return pl.pallas_call(
        _ada_kernel,
        grid=(n // tn,),
        in_specs=[pl.BlockSpec((16, D_MODEL), lambda j: (0, 0)),
                  pl.BlockSpec((D_MODEL, tn), lambda j: (0, j)),
                  pl.BlockSpec((1, tn), lambda j: (0, j))],
        out_specs=pl.BlockSpec((16, tn), lambda j: (0, j)),
        out_shape=jax.ShapeDtypeStruct((16, n), F32),
        compiler_params=_params(("arbitrary",)),
        name="ada_modulation",
    )(cond_pad, w, b.reshape(1, n))


def _in_kernel(x_ref, mod_ref, g_ref, w_ref, z_ref):
    h = _rms(x_ref[...], g_ref[...]) * (1.0 + mod_ref[1:2, :]) + mod_ref[0:1, :]
    z_ref[...] = jnp.dot(h.astype(BF16), w_ref[...], preferred_element_type=F32)


def input_projection(x, mod, g, w):
    nt = N_TOK // TM
    return pl.pallas_call(
        _in_kernel,
        grid=(nt,),
        in_specs=[pl.BlockSpec((TM, D_MODEL), lambda i: (i, 0)),
                  pl.BlockSpec((None, 6, D_MODEL), lambda i: (_tile_cond(i, TM), 0, 0)),
                  pl.BlockSpec((1, D_MODEL), lambda i: (0, 0)),
                  pl.BlockSpec((D_MODEL, Z_COLS), lambda i: (0, 0))],
        out_specs=pl.BlockSpec((TM, Z_COLS), lambda i: (i, 0)),
        out_shape=jax.ShapeDtypeStruct((N_TOK, Z_COLS), F32),
        compiler_params=_params(("arbitrary",)),
        name="input_projection",
    )(x, mod, g.reshape(1, D_MODEL), w)


def _merge_kernel(x_ref, mod_ref, g_ref, ab_c_ref, c_c_ref, d_c_ref, a_l_ref, b_l_ref, c_l_ref, d_l_ref,
                  wgl_ref, wbr_ref, wout_ref, xo_ref):
    x = x_ref[...]
    h = (_rms(x, g_ref[...]) * (1.0 + mod_ref[1:2, :]) + mod_ref[0:1, :]).astype(BF16)
    is_ctx = pl.program_id(0) < CTX_TOK // TM
    branches = (jnp.where(is_ctx, ab_c_ref[:, :BRANCH_W], a_l_ref[...]),
                jnp.where(is_ctx, ab_c_ref[:, BRANCH_W:], b_l_ref[...]),
                jnp.where(is_ctx, c_c_ref[...], c_l_ref[...]),
                jnp.where(is_ctx, d_c_ref[...], d_l_ref[...]))
    merged = None
    for j in range(N_BRANCH):
        gate = jax.nn.sigmoid(jnp.dot(h, wgl_ref[j], preferred_element_type=F32))
        br = jnp.dot(branches[j], wbr_ref[j], preferred_element_type=F32)
        merged = gate * br if merged is None else merged + gate * br
    mix = jnp.dot(merged.astype(BF16), wout_ref[...], preferred_element_type=F32)
    xo_ref[...] = x + mod_ref[2:3, :] * mix


def merge_projection(x, mod, g, o_ctx, o_lat, wgl, wbr, wout):
    nt = N_TOK // TM
    ctx_tiles = CTX_TOK // TM
    ctx_spec = lambda w: pl.BlockSpec((TM, w), lambda i: (jnp.minimum(i, ctx_tiles - 1), 0))
    lat_spec = pl.BlockSpec((TM, BRANCH_W), lambda i: (jnp.maximum(i - ctx_tiles, 0), 0))
    return pl.pallas_call(
        _merge_kernel,
        grid=(nt,),
        in_specs=[pl.BlockSpec((TM, D_MODEL), lambda i: (i, 0)),
                  pl.BlockSpec((None, 6, D_MODEL), lambda i: (_tile_cond(i, TM), 0, 0)),
                  pl.BlockSpec((1, D_MODEL), lambda i: (0, 0)),
                  ctx_spec(2 * BRANCH_W), ctx_spec(BRANCH_W), ctx_spec(BRANCH_W),
                  lat_spec, lat_spec, lat_spec, lat_spec,
                  pl.BlockSpec((N_BRANCH, D_MODEL, D_MODEL), lambda i: (0, 0, 0)),
                  pl.BlockSpec((N_BRANCH, BRANCH_W, D_MODEL), lambda i: (0, 0, 0)),
                  pl.BlockSpec((D_MODEL, D_MODEL), lambda i: (0, 0))],
        out_specs=pl.BlockSpec((TM, D_MODEL), lambda i: (i, 0)),
        out_shape=jax.ShapeDtypeStruct((N_TOK, D_MODEL), F32),
        compiler_params=_params(("arbitrary",)),
        name="merge_projection",
    )(x, mod, g.reshape(1, D_MODEL), *o_ctx, *o_lat, wgl, wbr, wout)


def _ffn_kernel(x_ref, mod_ref, g_ref, w1_ref, w3_ref, w2_ref, xo_ref):
    x = x_ref[...]
    h = (_rms(x, g_ref[...]) * (1.0 + mod_ref[4:5, :]) + mod_ref[3:4, :]).astype(BF16)
    a = jnp.dot(h, w1_ref[...], preferred_element_type=F32)
    b = jnp.dot(h, w3_ref[...], preferred_element_type=F32)
    hid = (a * jax.nn.sigmoid(a) * b).astype(BF16)
    f = jnp.dot(hid, w2_ref[...], preferred_element_type=F32)
    xo_ref[...] = x + mod_ref[5:6, :] * f


def dense_ffn(x, mod, g, w1, w3, w2):
    nt = N_TOK // TM
    const = lambda i: (0, 0)
    return pl.pallas_call(
        _ffn_kernel,
        grid=(nt,),
        in_specs=[pl.BlockSpec((TM, D_MODEL), lambda i: (i, 0)),
                  pl.BlockSpec((None, 6, D_MODEL), lambda i: (_tile_cond(i, TM), 0, 0)),
                  pl.BlockSpec((1, D_MODEL), const),
                  pl.BlockSpec((D_MODEL, D_FF), const, pipeline_mode=pl.Buffered(1)),
                  pl.BlockSpec((D_MODEL, D_FF), const, pipeline_mode=pl.Buffered(1)),
                  pl.BlockSpec((D_FF, D_MODEL), const, pipeline_mode=pl.Buffered(1))],
        out_specs=pl.BlockSpec((TM, D_MODEL), lambda i: (i, 0)),
        out_shape=jax.ShapeDtypeStruct((N_TOK, D_MODEL), F32),
        compiler_params=_params(("arbitrary",)),
        name="dense_ffn",
    )(x, mod, g.reshape(1, D_MODEL), w1, w3, w2)


MOE_T = 1024
MOE_R = 144
MOE_SR = 128
MOE_F = 1792
MOE_CAP = -(-MOE_T // MOE_R) * MOE_R


def _router_kernel(x_ref, mod_ref, g_ref, rw_ref, rb_ref, h_ref, pos_ref, wgt_ref):
    t = x_ref.shape[0]
    h = _rms(x_ref[...], g_ref[...]) * (1.0 + mod_ref[4:5, :]) + mod_ref[3:4, :]
    h_ref[...] = h.astype(BF16)
    logits = jnp.dot(h, rw_ref[...], preferred_element_type=F32, precision=lax.Precision.HIGHEST) + rb_ref[...]
    lt = logits.T[:N_EXPERTS, :]
    eidx = lax.broadcasted_iota(jnp.int32, lt.shape, 0)
    m1 = jnp.max(lt, axis=0, keepdims=True)
    i1 = jnp.min(jnp.where(lt == m1, eidx, N_EXPERTS), axis=0, keepdims=True)
    rest = jnp.where(eidx == i1, F32_MIN, lt)
    m2 = jnp.max(rest, axis=0, keepdims=True)
    i2 = jnp.min(jnp.where(rest == m2, eidx, N_EXPERTS), axis=0, keepdims=True)
    e2 = jnp.exp(m2 - m1)
    p1 = 1.0 / (1.0 + e2)
    p2 = e2 / (1.0 + e2)
    wgt_ref[...] = jnp.where(eidx == i1, p1, 0.0) + jnp.where(eidx == i2, p2, 0.0)
    routed = jnp.where(eidx == i1, 1.0, jnp.where(eidx == i2, 1.0, 0.0))
    s_id = lax.broadcasted_iota(jnp.int32, (t, t), 0)
    t_id = lax.broadcasted_iota(jnp.int32, (t, t), 1)
    before = jnp.where(s_id < t_id, 1.0, 0.0).astype(BF16)
    rank = jnp.dot(routed.astype(BF16), before, preferred_element_type=F32)
    pos_ref[...] = jnp.where(routed > 0.0, rank.astype(jnp.int32), -1)


def moe_router(x, mod, g, rw, rb):
    nt = N_TOK // MOE_T
    return pl.pallas_call(
        _router_kernel,
        grid=(nt,),
        in_specs=[pl.BlockSpec((MOE_T, D_MODEL), lambda i: (i, 0)),
                  pl.BlockSpec((None, 6, D_MODEL), lambda i: (_tile_cond(i, MOE_T), 0, 0)),
                  pl.BlockSpec((1, D_MODEL), lambda i: (0, 0)),
                  pl.BlockSpec((D_MODEL, 128), lambda i: (0, 0)),
                  pl.BlockSpec((1, 128), lambda i: (0, 0))],
        out_specs=[pl.BlockSpec((MOE_T, D_MODEL), lambda i: (i, 0)),
                   pl.BlockSpec((N_EXPERTS, MOE_T), lambda i: (0, i)),
                   pl.BlockSpec((N_EXPERTS, MOE_T), lambda i: (0, i))],
        out_shape=[jax.ShapeDtypeStruct((N_TOK, D_MODEL), BF16),
                   jax.ShapeDtypeStruct((N_EXPERTS, N_TOK), jnp.int32),
                   jax.ShapeDtypeStruct((N_EXPERTS, N_TOK), F32)],
        compiler_params=_params(("arbitrary",)),
        name="moe_router",
    )(x, mod, g.reshape(1, D_MODEL), rw, rb)


def _moe_sparse_kernel(h_ref, pos_ref, wgt_ref, w1_ref, w3_ref, w2_ref, y_ref, xg_scr, acc_scr, wr_scr):
    e = pl.program_id(1)
    f = pl.program_id(2)
    t = h_ref.shape[0]
    pos_e = pos_ref[pl.ds(e, 1), :]
    n_rows = jnp.max(pos_e) + 1
    n_blocks = (n_rows + MOE_R - 1) // MOE_R
    n_sc_blocks = (n_rows + MOE_SR - 1) // MOE_SR
    n_init_blocks = jnp.maximum(n_blocks, (n_sc_blocks * MOE_SR + MOE_R - 1) // MOE_R)
    row_id = lax.broadcasted_iota(jnp.int32, (MOE_R, t), 0)

    def block_rows(r):
        return pl.ds(pl.multiple_of(r * MOE_R, 16), MOE_R)

    def selects(r):
        return pos_e == row_id + r * MOE_R

    @pl.when((e == 0) & (f == 0))
    def _():
        y_ref[...] = jnp.zeros_like(y_ref)

    @pl.when(f == 0)
    def _():
        wgt_e = wgt_ref[pl.ds(e, 1), :]

        def gather(r, carry):
            sel = selects(r)
            xg = jnp.dot(jnp.where(sel, 1.0, 0.0).astype(BF16), h_ref[...], preferred_element_type=F32)
            xg_scr[block_rows(r), :] = xg.astype(BF16)
            w_rows = jnp.sum(jnp.where(sel, wgt_e, 0.0), axis=1, keepdims=True)
            wr_scr[block_rows(r), :] = jnp.broadcast_to(w_rows, (MOE_R, 128))
            acc_scr[block_rows(r), :] = jnp.zeros((MOE_R, D_MODEL), F32)
            return carry

        lax.fori_loop(0, n_init_blocks, gather, 0)

    def expert(r, carry):
        xg = xg_scr[block_rows(r), :]
        a = jnp.dot(xg, w1_ref[...], preferred_element_type=F32)
        b = jnp.dot(xg, w3_ref[...], preferred_element_type=F32)
        hid = (a * jax.nn.sigmoid(a) * b * wr_scr[block_rows(r), 0:1]).astype(BF16)
        acc_scr[block_rows(r), :] += jnp.dot(hid, w2_ref[...], preferred_element_type=F32)
        return carry

    lax.fori_loop(0, n_blocks, expert, 0)

    @pl.when(f == pl.num_programs(2) - 1)
    def _():
        sc_row_id = lax.broadcasted_iota(jnp.int32, (MOE_SR, t), 0)

        def scatter(r, carry):
            rows = pl.ds(pl.multiple_of(r * MOE_SR, MOE_SR), MOE_SR)
            onehot = jnp.where(pos_e == sc_row_id + r * MOE_SR, 1.0, 0.0).astype(BF16)
            hi, lo = _split_bf16(acc_scr[rows, :])
            y_ref[...] += lax.dot_general(jnp.concatenate([onehot, onehot], axis=0),
                                          jnp.concatenate([hi, lo], axis=0), _TN, preferred_element_type=F32)
            return carry

        lax.fori_loop(0, n_sc_blocks, scatter, 0)


def moe_experts(h2, pos, wgt, w1, w3, w2):
    nt = N_TOK // MOE_T
    nf = D_FF_EXPERT // MOE_F
    return pl.pallas_call(
        _moe_sparse_kernel,
        grid=(nt, N_EXPERTS, nf),
        in_specs=[pl.BlockSpec((MOE_T, D_MODEL), lambda i, e, f: (i, 0)),
                  pl.BlockSpec((N_EXPERTS, MOE_T), lambda i, e, f: (0, i)),
                  pl.BlockSpec((N_EXPERTS, MOE_T), lambda i, e, f: (0, i)),
                  pl.BlockSpec((None, D_MODEL, MOE_F), lambda i, e, f: (e, 0, f)),
                  pl.BlockSpec((None, D_MODEL, MOE_F), lambda i, e, f: (e, 0, f)),
                  pl.BlockSpec((None, MOE_F, D_MODEL), lambda i, e, f: (e, f, 0))],
        out_specs=pl.BlockSpec((MOE_T, D_MODEL), lambda i, e, f: (i, 0)),
        out_shape=jax.ShapeDtypeStruct((N_TOK, D_MODEL), F32),
        scratch_shapes=[pltpu.VMEM((MOE_CAP, D_MODEL), BF16),
                        pltpu.VMEM((MOE_CAP, D_MODEL), F32),
                        pltpu.VMEM((MOE_CAP, 128), F32)],
        compiler_params=_params(("arbitrary", "arbitrary", "arbitrary")),
        name="moe_experts",
    )(h2, pos, wgt, w1, w3, w2)


def _residual_norm_kernel(x_ref, y_ref, mod_ref, g_ref, o_ref):
    o_ref[...] = _rms(x_ref[...] + mod_ref[5:6, :] * y_ref[...], g_ref[...])


def residual_final_norm(x, y, mod, g, row0, n_rows):
    tm = 1024
    t0 = row0 // tm
    return pl.pallas_call(
        _residual_norm_kernel,
        grid=(n_rows // tm,),
        in_specs=[pl.BlockSpec((tm, D_MODEL), lambda i: (t0 + i, 0)),
                  pl.BlockSpec((tm, D_MODEL), lambda i: (t0 + i, 0)),
                  pl.BlockSpec((None, 6, D_MODEL), lambda i: (_tile_cond(t0 + i, tm), 0, 0)),
                  pl.BlockSpec((1, D_MODEL), lambda i: (0, 0))],
        out_specs=pl.BlockSpec((tm, D_MODEL), lambda i: (i, 0)),
        out_shape=jax.ShapeDtypeStruct((n_rows, D_MODEL), F32),
        compiler_params=_params(("arbitrary",)),
        name="residual_final_norm",
    )(x, y, mod, g.reshape(1, D_MODEL))


ATT_SCALE = HEAD_DIM ** -0.5
LOG2_E = 1.4426950408889634
_NT = (((1,), (1,)), ((), ()))


def _head_rms(x, g_row):
    outs = []
    for h in range(x.shape[1] // HEAD_DIM):
        xh = x[:, h * HEAD_DIM:(h + 1) * HEAD_DIM]
        outs.append(xh * lax.rsqrt(jnp.mean(xh * xh, axis=-1, keepdims=True) + EPS) * g_row)
    return jnp.concatenate(outs, axis=-1)


def _rope_apply(x, c, s):
    w = x.shape[-1]
    lane = lax.broadcasted_iota(jnp.int32, x.shape, 1)
    first_half = ((lane // (HEAD_DIM // 4)) % 2) == 0
    partner = jnp.where(first_half, pltpu.roll(x, w - HEAD_DIM // 4, 1), pltpu.roll(x, HEAD_DIM // 4, 1))
    return x * c + partner * s


def rope_lane_tables(L):
    rows = L // GRID_W
    row = jnp.repeat(jnp.arange(rows, dtype=F32), GRID_W)
    col = jnp.tile(jnp.arange(GRID_W, dtype=F32), rows)
    n_freq = HEAD_DIM // 4
    inv = ROPE_BASE ** (-jnp.arange(n_freq, dtype=F32) / n_freq)
    ang = jnp.stack([row, col], 0)[:, :, None] * inv
    cos, sin = jnp.cos(ang), jnp.sin(ang)
    c = jnp.concatenate([cos[0], cos[0], cos[1], cos[1]], axis=-1)
    s = jnp.concatenate([-sin[0], sin[0], -sin[1], sin[1]], axis=-1)
    return jnp.tile(c, (1, 4)), jnp.tile(s, (1, 4))


def _with_ones(v):
    ones = jnp.ones((v.shape[0], HEAD_DIM), BF16)
    parts = []
    for h in range(v.shape[1] // HEAD_DIM):
        parts += [v[:, h * HEAD_DIM:(h + 1) * HEAD_DIM].astype(BF16), ones]
    return jnp.concatenate(parts, axis=-1)


def _attend_heads(jobs):
    for job in jobs:
        q = (job['q'] * (ATT_SCALE * LOG2_E)).astype(BF16)
        job['s'] = lax.dot_general(q, job['k'], _NT, preferred_element_type=F32)
        if job.get('extra') is not None:
            job['s2'] = lax.dot_general(q, job['extra'][0], _NT, preferred_element_type=F32)
    outs = []
    for job in jobs:
        s, sink = job['s'], job.get('sink')
        if sink is not None:
            sink = sink * LOG2_E
        if job.get('mask') is not None:
            s = jnp.where(job['mask'], s, NEG_INF)
        m = jnp.max(s, axis=-1, keepdims=True)
        if 's2' in job:
            m = jnp.maximum(m, jnp.max(job['s2'], axis=-1, keepdims=True))
        if sink is not None:
            m = jnp.maximum(m, sink)
        o = jnp.dot(jnp.exp2(s - m).astype(BF16), job['v'], preferred_element_type=F32)
        if 's2' in job:
            o = o + jnp.dot(jnp.exp2(job['s2'] - m).astype(BF16), job['extra'][1], preferred_element_type=F32)
        den = o[:, HEAD_DIM:HEAD_DIM + 1]
        if sink is not None:
            den = den + jnp.exp2(sink - m)
        outs.append(o[:, :HEAD_DIM] / den)
    return outs


def _ctx_attn_kernel(sink_ref, z_ref, gq_ref, gk_ref, o_ref, bk_ref):
    z = z_ref[...]
    bq = _head_rms(z[:, 512:768], gq_ref[...])
    bk = _head_rms(z[:, 768:896], gk_ref[...])
    bk_ref[...] = bk
    groups = ((z[:, 0:256], z[:, 256:384], z[:, 384:512], True),
              (bq, bk, z[:, 896:1024], False))
    outs = []
    for q_all, k_all, v_all, use_sink in groups:
        k_all = k_all.astype(BF16)
        v_all = _with_ones(v_all)
        for hq in range(A_HEADS):
            kv = hq // (A_HEADS // A_KV_HEADS)
            outs.append(dict(q=q_all[:, hq * HEAD_DIM:(hq + 1) * HEAD_DIM],
                             k=k_all[:, kv * HEAD_DIM:(kv + 1) * HEAD_DIM],
                             v=v_all[:, kv * 2 * HEAD_DIM:(kv + 1) * 2 * HEAD_DIM],
                             sink=sink_ref[hq] if use_sink else None))
    o_ref[...] = jnp.concatenate(_attend_heads(outs), axis=-1).astype(o_ref.dtype)


def ctx_attention(z, sink, gq, gk):
    return pl.pallas_call(
        _ctx_attn_kernel,
        grid=(BATCH,),
        in_specs=[pl.BlockSpec(memory_space=pltpu.SMEM),
                  pl.BlockSpec((SEQ, 1024), lambda b: (b, 0)),
                  pl.BlockSpec((1, HEAD_DIM), lambda b: (0, 0)),
                  pl.BlockSpec((1, HEAD_DIM), lambda b: (0, 0))],
        out_specs=[pl.BlockSpec((SEQ, 512), lambda b: (b, 0)),
                   pl.BlockSpec((SEQ, 128), lambda b: (b, 0))],
        out_shape=[jax.ShapeDtypeStruct((CTX_TOK, 512), BF16),
                   jax.ShapeDtypeStruct((CTX_TOK, 128), F32)],
        compiler_params=_params(("arbitrary",)),
        name="ctx_attention",
    )(sink, z, gq.reshape(1, HEAD_DIM), gk.reshape(1, HEAD_DIM))


LB_TQ = 256


def _lat_b_kernel(zq_ref, zkv_ref, ck_ref, cv_ref, cq_ref, sq_ref, ckk_ref, skk_ref, gq_ref, gk_ref,
                  o_ref, k_scr, v_scr):
    L = zkv_ref.shape[0]

    @pl.when(pl.program_id(1) == 0)
    def _():
        kv = zkv_ref[...]
        bk = _rope_apply(_head_rms(kv[:, :128], gk_ref[...]), ckk_ref[...], skk_ref[...])
        k_scr[0:L, :] = bk.astype(BF16)
        k_scr[L:L + PAST_LEN, :] = ck_ref[...].astype(BF16)
        v_scr[0:L, :] = _with_ones(kv[:, 128:])
        v_scr[L:L + PAST_LEN, :] = _with_ones(cv_ref[...])

    q = _rope_apply(_head_rms(zq_ref[...], gq_ref[...]), cq_ref[...], sq_ref[...])
    outs = []
    for hq in range(B_HEADS):
        kv = hq // (B_HEADS // B_KV_HEADS)
        sl = slice(kv * HEAD_DIM, (kv + 1) * HEAD_DIM)
        outs.append(dict(q=q[:, hq * HEAD_DIM:(hq + 1) * HEAD_DIM], k=k_scr[:, sl],
                         v=v_scr[:, kv * 2 * HEAD_DIM:(kv + 1) * 2 * HEAD_DIM]))
    o_ref[...] = jnp.concatenate(_attend_heads(outs), axis=-1).astype(o_ref.dtype)


def latent_attention_b(z, row0, nb, L, cache_k, cache_v, rope_c, rope_s, gq, gk):
    nq = L // LB_TQ
    return pl.pallas_call(
        _lat_b_kernel,
        grid=(nb, nq),
        in_specs=[pl.BlockSpec((LB_TQ, 256), lambda b, i: (row0 // LB_TQ + b * nq + i, 2)),
                  pl.BlockSpec((L, 256), lambda b, i: (row0 // L + b, 3)),
                  pl.BlockSpec((None, PAST_LEN, 128), lambda b, i: (b, 0, 0)),
                  pl.BlockSpec((None, PAST_LEN, 128), lambda b, i: (b, 0, 0)),
                  pl.BlockSpec((LB_TQ, 256), lambda b, i: (i, 0)),
                  pl.BlockSpec((LB_TQ, 256), lambda b, i: (i, 0)),
                  pl.BlockSpec((L, 128), lambda b, i: (0, 0)),
                  pl.BlockSpec((L, 128), lambda b, i: (0, 0)),
                  pl.BlockSpec((1, HEAD_DIM), lambda b, i: (0, 0)),
                  pl.BlockSpec((1, HEAD_DIM), lambda b, i: (0, 0))],
        out_specs=pl.BlockSpec((LB_TQ, 256), lambda b, i: (b * nq + i, 0)),
        out_shape=jax.ShapeDtypeStruct((nb * L, 256), BF16),
        scratch_shapes=[pltpu.VMEM((L + PAST_LEN, 128), BF16),
                        pltpu.VMEM((L + PAST_LEN, 256), BF16)],
        compiler_params=_params(("arbitrary", "arbitrary")),
        name="latent_attention_b",
    )(z, z, cache_k, cache_v, rope_c, rope_s, rope_c, rope_s, gq.reshape(1, HEAD_DIM), gk.reshape(1, HEAD_DIM))


def _lat_a_kernel(sink_ref, zq_ref, zkv_ref, ck_ref, cv_ref, cq_ref, sq_ref, ckk_ref, skk_ref,
                  o_ref, k_scr, v_scr, ck_scr, cv_scr):
    L = zkv_ref.shape[0]
    i = pl.program_id(1)

    @pl.when(i == 0)
    def _():
        kv = zkv_ref[...]
        k_scr[0:BLOCK, :] = jnp.zeros((BLOCK, 128), BF16)
        v_scr[0:BLOCK, :] = jnp.zeros((BLOCK, 256), BF16)
        k_scr[BLOCK:BLOCK + L, :] = _rope_apply(kv[:, :128], ckk_ref[...], skk_ref[...]).astype(BF16)
        v_scr[BLOCK:BLOCK + L, :] = _with_ones(kv[:, 128:])
        k_scr[BLOCK + L:2 * BLOCK + L, :] = jnp.zeros((BLOCK, 128), BF16)
        v_scr[BLOCK + L:2 * BLOCK + L, :] = jnp.zeros((BLOCK, 256), BF16)
        ck_scr[...] = ck_ref[...].astype(BF16)
        cv_scr[...] = _with_ones(cv_ref[...])

    q = _rope_apply(zq_ref[...], cq_ref[...], sq_ref[...])
    start = pl.multiple_of(i * BLOCK, BLOCK)
    kband = k_scr[pl.ds(start, 3 * BLOCK), :]
    vband = v_scr[pl.ds(start, 3 * BLOCK), :]
    r = lax.broadcasted_iota(jnp.int32, (BLOCK, 3 * BLOCK), 0)
    cidx = lax.broadcasted_iota(jnp.int32, (BLOCK, 3 * BLOCK), 1)
    kpos = i * BLOCK - BLOCK + cidx
    mask = (jnp.abs(cidx - BLOCK - r) <= WINDOW) & (kpos >= 0) & (kpos < L)
    outs = []
    for hq in range(A_HEADS):
        kv = hq // (A_HEADS // A_KV_HEADS)
        sl = slice(kv * HEAD_DIM, (kv + 1) * HEAD_DIM)
        sv = slice(kv * 2 * HEAD_DIM, (kv + 1) * 2 * HEAD_DIM)
        outs.append(dict(q=q[:, hq * HEAD_DIM:(hq + 1) * HEAD_DIM], k=kband[:, sl], v=vband[:, sv],
                         extra=(ck_scr[:, sl], cv_scr[:, sv]), sink=sink_ref[hq], mask=mask))
    o_ref[...] = jnp.concatenate(_attend_heads(outs), axis=-1).astype(o_ref.dtype)


def latent_attention_a(z, row0, nb, L, sink, cache_k, cache_v, rope_c, rope_s):
    nq = L // BLOCK
    return pl.pallas_call(
        _lat_a_kernel,
        grid=(nb, nq),
        in_specs=[pl.BlockSpec(memory_space=pltpu.SMEM),
                  pl.BlockSpec((BLOCK, 256), lambda b, i: (row0 // BLOCK + b * nq + i, 0)),
                  pl.BlockSpec((L, 256), lambda b, i: (row0 // L + b, 1)),
                  pl.BlockSpec((None, PAST_LEN, 128), lambda b, i: (b, 0, 0)),
                  pl.BlockSpec((None, PAST_LEN, 128), lambda b, i: (b, 0, 0)),
                  pl.BlockSpec((BLOCK, 256), lambda b, i: (i, 0)),
                  pl.BlockSpec((BLOCK, 256), lambda b, i: (i, 0)),
                  pl.BlockSpec((L, 128), lambda b, i: (0, 0)),
                  pl.BlockSpec((L, 128), lambda b, i: (0, 0))],
        out_specs=pl.BlockSpec((BLOCK, 256), lambda b, i: (b * nq + i, 0)),
        out_shape=jax.ShapeDtypeStruct((nb * L, 256), BF16),
        scratch_shapes=[pltpu.VMEM((L + 2 * BLOCK, 128), BF16),
                        pltpu.VMEM((L + 2 * BLOCK, 256), BF16),
                        pltpu.VMEM((PAST_LEN, 128), BF16),
                        pltpu.VMEM((PAST_LEN, 256), BF16)],
        compiler_params=_params(("arbitrary", "arbitrary")),
        name="latent_attention_a",
    )(sink, z, z, cache_k, cache_v, rope_c, rope_s, rope_c, rope_s)


_TN = (((0,), (0,)), ((), ()))
HG_GROUP = 8


def _hgrn_kernel(zq_ref, zf_ref, zi_ref, zg_ref, lb_ref, gn_ref, s0_ref, o_ref, sT_ref,
                 of_scr, ob_scr, g_scr, k_scr, qin_scr, kin_scr, v_scr, S_scr, *, tt):
    d = pl.program_id(1)
    j = pl.program_id(2)
    n_t = pl.num_programs(2)
    C = HGRN_CHUNK
    n_c = tt // C

    @pl.when(j == 0)
    def _():
        S_scr[...] = s0_ref[...]

    lb = lb_ref[...]
    sg = jax.nn.sigmoid(zf_ref[...])
    logf = jnp.log(lb + (1.0 - lb) * sg)
    k = (1.0 - lb) * (1.0 - sg)
    k_scr[...] = k
    v_scr[...] = zi_ref[...].astype(BF16)
    in_chunk = lax.broadcasted_iota(jnp.int32, (tt, C_HEADS * HEAD_DIM), 0) % C
    row = lax.broadcasted_iota(jnp.int32, (C, C), 0)
    col = lax.broadcasted_iota(jnp.int32, (C, C), 1)
    heads = [slice(h * HEAD_DIM, (h + 1) * HEAD_DIM) for h in range(C_HEADS)]

    def run(reverse, tile):
        G = logf
        step = 1
        while step < C:
            if reverse:
                G = G + jnp.where(in_chunk < C - step, pltpu.roll(G, tt - step, 0), 0.0)
            else:
                G = G + jnp.where(in_chunk >= step, pltpu.roll(G, step, 0), 0.0)
            step *= 2
        g_scr[...] = G
        qin_scr[...] = (zq_ref[...] * jnp.exp(G)).astype(BF16)
        kin_scr[...] = (k * jnp.exp(-G)).astype(BF16)
        tri = (row <= col) if reverse else (row >= col)

        def body(gi, carry):
            chunks = []
            for g in range(HG_GROUP):
                ci = gi * HG_GROUP + g
                c = (n_c - 1 - ci) if reverse else ci
                r0 = pl.multiple_of(c * C, C)
                rows = pl.ds(r0, C)
                G_c = g_scr[rows, :]
                G_end = G_c[0:1, :] if reverse else G_c[C - 1:C, :]
                vc = v_scr[rows, :]
                q_in = qin_scr[rows, :]
                k_in = kin_scr[rows, :]
                k_out = (k_scr[rows, :] * jnp.exp(G_end - G_c)).astype(BF16)
                chunks.append(dict(
                    r0=r0, rows=rows, vc=vc, q_in=q_in, decay=jnp.exp(G_end),
                    a=[lax.dot_general(q_in[:, sl], k_in[:, sl], _NT, preferred_element_type=F32) for sl in heads],
                    kv=[lax.dot_general(vc[:, sl], k_out[:, sl], _TN, preferred_element_type=F32) for sl in heads]))
            s_cur = [S_scr[h] for h in range(C_HEADS)]
            for ch in chunks:
                ch['qs'] = [lax.dot_general(ch['q_in'][:, sl], s_cur[h].astype(BF16), _NT,
                                            preferred_element_type=F32) for h, sl in enumerate(heads)]
                s_cur = [s_cur[h] * ch['decay'][:, sl] + ch['kv'][h] for h, sl in enumerate(heads)]
            for h in range(C_HEADS):
                S_scr[h] = s_cur[h]
            for ch in chunks:
                o_c = jnp.concatenate(
                    [jnp.dot(jnp.where(tri, ch['a'][h], 0.0).astype(BF16), ch['vc'][:, sl],
                             preferred_element_type=F32) + ch['qs'][h] for h, sl in enumerate(heads)], axis=-1)
                if reverse:
                    ob_scr[ch['rows'], :] = o_c
                else:
                    of_scr[pl.ds(pl.multiple_of(tile * tt, tt) + ch['r0'], C), :] = o_c
            return carry

        lax.fori_loop(0, n_c // HG_GROUP, body, 0)

    @pl.when(d == 0)
    def _():
        run(False, j)

    @pl.when(d == 1)
    def _():
        tile = n_t - 1 - j
        run(True, tile)
        o = of_scr[pl.ds(pl.multiple_of(tile * tt, tt), tt), :] + ob_scr[...]
        g = zg_ref[...]
        o_ref[...] = (_head_rms(o, gn_ref[...]) * (g * jax.nn.sigmoid(g))).astype(o_ref.dtype)

    @pl.when(j == n_t - 1)
    def _():
        sT_ref[...] = S_scr[...]


def hgrn_mixer(z, row0, nb, L, lb, gn, s0_t):
    tt = min(L, 512)
    n_t = L // tt
    rb = row0 // tt

    def tile(d, j):
        return jnp.where(d == 0, j, n_t - 1 - j)

    def late(d, j):
        return jnp.where(d == 0, n_t - 1, n_t - 1 - j)

    st_spec = pl.BlockSpec((None, None, C_HEADS, HEAD_DIM, HEAD_DIM), lambda b, d, j: (b, d, 0, 0, 0))
    return pl.pallas_call(
        functools.partial(_hgrn_kernel, tt=tt),
        grid=(nb, 2, n_t),
        in_specs=[pl.BlockSpec((tt, 256), lambda b, d, j: (rb + b * n_t + tile(d, j), 4)),
                  pl.BlockSpec((tt, 256), lambda b, d, j: (rb + b * n_t + tile(d, j), 5 + d)),
                  pl.BlockSpec((tt, 256), lambda b, d, j: (rb + b * n_t + tile(d, j), 7)),
                  pl.BlockSpec((tt, 256), lambda b, d, j: (rb + b * n_t + late(d, j), 8)),
                  pl.BlockSpec((1, 256), lambda b, d, j: (0, 0)),
                  pl.BlockSpec((1, HEAD_DIM), lambda b, d, j: (0, 0)),
                  st_spec],
        out_specs=[pl.BlockSpec((tt, 256), lambda b, d, j: (b * n_t + late(d, j), 0)),
                   st_spec],
        out_shape=[jax.ShapeDtypeStruct((nb * L, 256), BF16),
                   jax.ShapeDtypeStruct((nb, 2, C_HEADS, HEAD_DIM, HEAD_DIM), F32)],
        scratch_shapes=[pltpu.VMEM((L, 256), F32),
                        pltpu.VMEM((tt, 256), F32),
                        pltpu.VMEM((tt, 256), F32),
                        pltpu.VMEM((tt, 256), F32),
                        pltpu.VMEM((tt, 256), BF16),
                        pltpu.VMEM((tt, 256), BF16),
                        pltpu.VMEM((tt, 256), BF16),
                        pltpu.VMEM((C_HEADS, HEAD_DIM, HEAD_DIM), F32)],
        compiler_params=_params(("arbitrary", "arbitrary", "arbitrary")),
        name="hgrn_mixer",
    )(z, z, z, z, lb.reshape(1, 256), gn.reshape(1, HEAD_DIM), s0_t)


DL_C = DELTA_CHUNK
DL_PREP_TT = 256
DL_HALO = 8
DL_CHUNK_TT = 512
N_QKV_HEADS = 3 * D_HEADS
DL_HL_LEVELS = 3
DL_GROUP = 2


def _delta_prep_kernel(x_ref, xp_ref, xn_ref, zab_ref, cw_ref, na_ref, dtb_ref, qkv_ref, gate_ref, xs_scr):
    tt = x_ref.shape[0]
    row = pl.program_id(0) * tt
    lat = row - CTX_TOK
    first = jnp.where(row < CTX_TOK, True, lat % DEC_SEQ == 0)
    last = jnp.where(row < CTX_TOK, True, (lat + tt) % DEC_SEQ == 0)
    xs_scr[DL_HALO:DL_HALO + tt, :] = x_ref[...]
    xs_scr[0:DL_HALO, :] = jnp.where(first, 0.0, xp_ref[...])
    xs_scr[DL_HALO + tt:2 * DL_HALO + tt, :] = jnp.where(last, 0.0, xn_ref[...])
    pad = (CONV_K - 1) // 2
    y = None
    for t in range(CONV_K):
        term = xs_scr[pl.ds(DL_HALO - pad + t, tt), :] * cw_ref[t:t + 1, :]
        y = term if y is None else y + term
    y = y * jax.nn.sigmoid(y)
    for idx in range(N_QKV_HEADS):
        xh = y[:, idx * HEAD_DIM:(idx + 1) * HEAD_DIM]
        if idx < 2 * D_HEADS:
            xh = xh * lax.rsqrt(jnp.sum(xh * xh, axis=-1, keepdims=True) + EPS)
        if idx < D_HEADS:
            xh = xh * ATT_SCALE
        qkv_ref[idx] = xh
    zab = zab_ref[...]
    lane = lax.broadcasted_iota(jnp.int32, zab.shape, 1)
    t_ = zab + dtb_ref[...]
    softplus = jnp.maximum(t_, 0.0) + jnp.log(1.0 + jnp.exp(-jnp.abs(t_)))
    gate_ref[...] = jnp.where(lane < 2 * D_HEADS, na_ref[...] * softplus, jax.nn.sigmoid(zab))


def delta_prep(z, conv_w, a_log, dt_bias):
    tt = DL_PREP_TT
    hb = tt // DL_HALO
    n_hb = N_TOK // DL_HALO
    pad8 = lambda v: jnp.concatenate([v.reshape(1, 2 * D_HEADS), jnp.zeros((1, 128 - 2 * D_HEADS), F32)], axis=1)
    return pl.pallas_call(
        _delta_prep_kernel,
        grid=(N_TOK // tt,),
        in_specs=[pl.BlockSpec((tt, 768), lambda i: (i, 3)),
                  pl.BlockSpec((DL_HALO, 768), lambda i: (jnp.maximum(i * hb - 1, 0), 3)),
                  pl.BlockSpec((DL_HALO, 768), lambda i: (jnp.minimum((i + 1) * hb, n_hb - 1), 3)),
                  pl.BlockSpec((tt, 128), lambda i: (i, Z_DAB // 128)),
                  pl.BlockSpec((CONV_K, 768), lambda i: (0, 0)),
                  pl.BlockSpec((1, 128), lambda i: (0, 0)),
                  pl.BlockSpec((1, 128), lambda i: (0, 0))],
        out_specs=[pl.BlockSpec((N_QKV_HEADS, tt, HEAD_DIM), lambda i: (0, i, 0)),
                   pl.BlockSpec((tt, 128), lambda i: (i, 0))],
        out_shape=[jax.ShapeDtypeStruct((N_QKV_HEADS, N_TOK, HEAD_DIM), F32),
                   jax.ShapeDtypeStruct((N_TOK, 128), F32)],
        scratch_shapes=[pltpu.VMEM((tt + 2 * DL_HALO, 768), F32)],
        compiler_params=_params(("arbitrary",)),
        name="delta_prep",
    )(z, z, z, z, conv_w, pad8(-jnp.exp(a_log)), pad8(dt_bias))


def _split_bf16(a):
    hi = a.astype(BF16)
    return hi, (a - hi.astype(F32)).astype(BF16)


def _dot_hl(a_parts, b_parts):
    (a_hi, a_lo), (b_hi, b_lo) = a_parts, b_parts
    m = a_hi.shape[0]
    r = jnp.dot(jnp.concatenate([a_hi, a_lo], axis=0), b_hi, preferred_element_type=F32)
    return r[:m] + r[m:] + jnp.dot(a_hi, b_lo, preferred_element_type=F32)


def _delta_chunk_kernel(qkv_ref, gate_ref, u2_ref, wq_ref, ak_ref):
    C = DL_C
    n_c = gate_ref.shape[0] // C
    row = lax.broadcasted_iota(jnp.int32, (C, C), 0)
    col = lax.broadcasted_iota(jnp.int32, (C, C), 1)
    eye = (row == col).astype(F32)
    t_idx = lax.broadcasted_iota(jnp.int32, (C, 128), 0)

    def chunk_chains(c):
        r0 = pl.multiple_of(c * C, C)
        ga = gate_ref[pl.ds(r0, C), :]
        chains = []
        for d in range(2):
            incl = (row >= col) if d == 0 else (row <= col)
            strict = (row > col) if d == 0 else (row < col)
            g_all = ga
            step = 1
            while step < C:
                if d == 0:
                    g_all = g_all + jnp.where(t_idx >= step, pltpu.roll(g_all, step, 0), 0.0)
                else:
                    g_all = g_all + jnp.where(t_idx < C - step, pltpu.roll(g_all, C - step, 0), 0.0)
                step *= 2
            g_all_t = g_all.T
            for h in range(D_HEADS):
                ci = d * D_HEADS + h
                q = qkv_ref[h, pl.ds(r0, C), :]
                k = qkv_ref[D_HEADS + h, pl.ds(r0, C), :]
                v = qkv_ref[2 * D_HEADS + h, pl.ds(r0, C), :]
                g_col = g_all[:, ci:ci + 1]
                g_row = g_all_t[ci:ci + 1, :]
                beta = ga[:, 2 * D_HEADS + ci:2 * D_HEADS + ci + 1]
                g_end = g_col[C - 1:C, :] if d == 0 else g_col[0:1, :]
                kb = k * beta
                eg = jnp.exp(g_col)
                decay = jnp.where(incl, jnp.exp(jnp.where(incl, g_col - g_row, 0.0)), 0.0)
                kq = jnp.concatenate([kb, q], axis=0).astype(BF16)
                chains.append(dict(
                    strict=strict, decay=decay, qg=q * eg, g_end=g_end,
                    r=lax.dot_general(kq, k.astype(BF16), _NT, preferred_element_type=F32),
                    rhs=jnp.concatenate([v * beta, kb * eg], axis=1).astype(BF16),
                    ke_t=(k * jnp.exp(g_end - g_col)).T))
        return r0, chains

    def body(gi, carry):
        groups = [(gi * DL_GROUP + cc,) + chunk_chains(gi * DL_GROUP + cc) for cc in range(DL_GROUP)]
        chains = [ch for _, _, chs in groups for ch in chs]
        for ch in chains:
            ch['p'] = -jnp.where(ch['strict'], ch['r'][:C] * ch['decay'], 0.0)
            ch['t'] = eye + ch['p']
        for level in range(5):
            if level < DL_HL_LEVELS:
                for ch in chains:
                    parts = _split_bf16(ch['p'])
                    ch['p'] = _dot_hl(parts, parts)
                for ch in chains:
                    ch['t'] = ch['t'] + _dot_hl(_split_bf16(ch['t']), _split_bf16(ch['p']))
            else:
                for ch in chains:
                    pb = ch['p'].astype(BF16)
                    ch['p'] = jnp.dot(pb, pb, preferred_element_type=F32)
                for ch in chains:
                    ch['t'] = ch['t'] + jnp.dot(ch['t'].astype(BF16), ch['p'].astype(BF16),
                                                preferred_element_type=F32)
        for ch in chains:
            ch['uw'] = jnp.dot(ch['t'].astype(BF16), ch['rhs'], preferred_element_type=F32)
        pack = lambda xs: jnp.stack(xs).reshape((2, D_HEADS) + xs[0].shape)
        for c, r0, chs in groups:
            u2 = [jnp.concatenate([ch['uw'][:, :C], jnp.broadcast_to(jnp.exp(ch['g_end']), (C, C))], axis=1)
                  for ch in chs]
            wq = [jnp.concatenate([ch['uw'][:, C:], ch['qg']], axis=0).astype(BF16) for ch in chs]
            ak = [jnp.concatenate([ch['r'][C:] * ch['decay'], ch['ke_t']], axis=0).astype(BF16) for ch in chs]
            u2_ref[:, :, pl.ds(r0, C), :] = pack(u2)
            wq_ref[:, :, c] = pack(wq)
            ak_ref[:, :, c] = pack(ak)
        return carry

    lax.fori_loop(0, n_c // DL_GROUP, body, 0)


def delta_chunks(qkv, gates):
    tt = DL_CHUNK_TT
    n_c = tt // DL_C
    return pl.pallas_call(
        _delta_chunk_kernel,
        grid=(N_TOK // tt,),
        in_specs=[pl.BlockSpec((N_QKV_HEADS, tt, HEAD_DIM), lambda i: (0, i, 0)),
                  pl.BlockSpec((tt, 128), lambda i: (i, 0))],
        out_specs=[pl.BlockSpec((2, D_HEADS, tt, 128), lambda i: (0, 0, i, 0)),
                   pl.BlockSpec((2, D_HEADS, n_c, 2 * DL_C, HEAD_DIM), lambda i: (0, 0, i, 0, 0)),
                   pl.BlockSpec((2, D_HEADS, n_c, 2 * DL_C, HEAD_DIM), lambda i: (0, 0, i, 0, 0))],
        out_shape=[jax.ShapeDtypeStruct((2, D_HEADS, N_TOK, 128), F32),
                   jax.ShapeDtypeStruct((2, D_HEADS, N_TOK // DL_C, 2 * DL_C, HEAD_DIM), BF16),
                   jax.ShapeDtypeStruct((2, D_HEADS, N_TOK // DL_C, 2 * DL_C, HEAD_DIM), BF16)],
        compiler_params=_params(("arbitrary",)),
        name="delta_chunks",
    )(qkv, gates)


def _delta_scan_kernel(u2f_ref, wqf_ref, akf_ref, u2b_ref, wqb_ref, akb_ref, s0_ref,
                       of_ref, ob_ref, s_ref, s_scr):
    j = pl.program_id(1)
    C = DL_C
    n_c = wqf_ref.shape[1]

    @pl.when(j == 0)
    def _():
        s_scr[...] = s0_ref[...]

    def body(ci, carry):
        chains = []
        for d, (u2_ref, wq_ref, ak_ref, o_ref) in enumerate(((u2f_ref, wqf_ref, akf_ref, of_ref),
                                                            (u2b_ref, wqb_ref, akb_ref, ob_ref))):
            c = ci if d == 0 else n_c - 1 - ci
            r0 = pl.multiple_of(c * C, C)
            for h in range(D_HEADS):
                s = s_scr[d, h]
                chains.append(dict(d=d, h=h, c=c, r0=r0, s=s, ak_ref=ak_ref, o_ref=o_ref,
                                   u2=u2_ref[h, pl.ds(r0, C), :],
                                   r1=jnp.dot(wq_ref[h, c], s.astype(BF16),
                                              preferred_element_type=F32)))
        for ch in chains:
            v_new = ch['u2'][:, :C] - ch['r1'][:C]
            ch['r2'] = jnp.dot(ch['ak_ref'][ch['h'], ch['c']], v_new.astype(BF16),
                               preferred_element_type=F32)
        for ch in chains:
            ch['o_ref'][ch['h'], pl.ds(ch['r0'], C), :] = ch['r1'][C:] + ch['r2'][:C]
            s_scr[ch['d'], ch['h']] = ch['s'] * ch['u2'][0:1, C:] + ch['r2'][C:]
        return carry

    lax.fori_loop(0, n_c, body, 0)

    @pl.when(j == pl.num_programs(1) - 1)
    def _():
        s_ref[...] = s_scr[...]


def delta_scan(u2, wq, ak, row0, nb, L, s0):
    tt = min(L, 512)
    n_t = L // tt
    n_c = tt // DL_C
    rb = row0 // tt
    fwd = lambda b, j: rb + b * n_t + j
    bwd = lambda b, j: rb + b * n_t + (n_t - 1 - j)
    u_spec = lambda d, f: pl.BlockSpec((None, D_HEADS, tt, 128), lambda b, j: (d, 0, f(b, j), 0))
    c_spec = lambda d, f: pl.BlockSpec((None, D_HEADS, n_c, 2 * DL_C, HEAD_DIM), lambda b, j: (d, 0, f(b, j), 0, 0))
    st_spec = pl.BlockSpec((None, 2, D_HEADS, HEAD_DIM, HEAD_DIM), lambda b, j: (b, 0, 0, 0, 0))
    return pl.pallas_call(
        _delta_scan_kernel,
        grid=(nb, n_t),
        in_specs=[u_spec(0, fwd), c_spec(0, fwd), c_spec(0, fwd),
                  u_spec(1, bwd), c_spec(1, bwd), c_spec(1, bwd), st_spec],
        out_specs=[pl.BlockSpec((D_HEADS, tt, HEAD_DIM), lambda b, j: (0, b * n_t + j, 0)),
                   pl.BlockSpec((D_HEADS, tt, HEAD_DIM), lambda b, j: (0, b * n_t + (n_t - 1 - j), 0)),
                   st_spec],
        out_shape=[jax.ShapeDtypeStruct((D_HEADS, nb * L, HEAD_DIM), F32),
                   jax.ShapeDtypeStruct((D_HEADS, nb * L, HEAD_DIM), F32),
                   jax.ShapeDtypeStruct((nb, 2, D_HEADS, HEAD_DIM, HEAD_DIM), F32)],
        scratch_shapes=[pltpu.VMEM((2, D_HEADS, HEAD_DIM, HEAD_DIM), F32)],
        compiler_params=_params(("arbitrary", "arbitrary")),
        name="delta_scan",
    )(u2, wq, ak, u2, wq, ak, s0)


def _delta_out_kernel(of_ref, ob_ref, zg_ref, gn_ref, o_ref):
    outs = []
    for h in range(D_HEADS):
        o = of_ref[h] + ob_ref[h]
        outs.append(o * lax.rsqrt(jnp.mean(o * o, axis=-1, keepdims=True) + EPS) * gn_ref[...])
    g = zg_ref[...]
    o_ref[...] = (jnp.concatenate(outs, axis=-1) * (g * jax.nn.sigmoid(g))).astype(o_ref.dtype)


def delta_output(o_f, o_b, z, row0, gn):
    n = o_f.shape[1]
    tt = 256
    return pl.pallas_call(
        _delta_out_kernel,
        grid=(n // tt,),
        in_specs=[pl.BlockSpec((D_HEADS, tt, HEAD_DIM), lambda i: (0, i, 0)),
                  pl.BlockSpec((D_HEADS, tt, HEAD_DIM), lambda i: (0, i, 0)),
                  pl.BlockSpec((tt, 256), lambda i: (row0 // tt + i, Z_DG // 256)),
                  pl.BlockSpec((1, HEAD_DIM), lambda i: (0, 0))],
        out_specs=pl.BlockSpec((tt, 256), lambda i: (i, 0)),
        out_shape=jax.ShapeDtypeStruct((n, 256), BF16),
        compiler_params=_params(("arbitrary",)),
        name="delta_output",
    )(o_f, o_b, z, gn.reshape(1, HEAD_DIM))


def kernel(x_prompt, x_sample, cache_attn_a_k, cache_attn_a_v, cache_attn_b_k, cache_attn_b_v,
           state_hgrn, state_delta, c, c_ctx, norm1_g, norm2_g, w_ada, b_ada, w_in, a_sink,
           b_qnorm_g, b_knorm_g, c_lb, c_onorm_g, d_conv, d_a_log, d_dt_bias, d_onorm_g,
           w_branch, w_out, ffn_w1, ffn_w3, ffn_w2, router_w, router_b, moe_w1, moe_w3, moe_w2,
           final_norm_g):
    cum = jnp.cumsum(jax.nn.softmax(c_lb, axis=0), axis=0)
    lower_bounds = cum - cum[:1]

    x = jnp.concatenate([x_prompt.reshape(CTX_TOK, D_MODEL), x_sample.reshape(LAT_TOK, D_MODEL)], axis=0)
    cond = jnp.concatenate([c_ctx[None, :], c, jnp.zeros((16 - N_COND, D_MODEL), F32)], axis=0)

    rope_c, rope_s = rope_lane_tables(DEC_SEQ)
    caches = []
    for l in range(DEPTH):
        mod = ada_modulation(cond, w_ada[l], b_ada[l])[:N_COND].reshape(N_COND, 6, D_MODEL)
        w_mix = jnp.concatenate([w_in[l][:, :Z_MAIN], w_in[l][:, Z_MAIN + 16:W_IN_MIX],
                                 w_in[l][:, Z_MAIN:Z_MAIN + 16], jnp.zeros((D_MODEL, 128 - 16), F32)],
                                axis=1).astype(BF16)
        w_gl = w_in[l][:, W_IN_MIX:].reshape(D_MODEL, N_BRANCH, D_MODEL).transpose(1, 0, 2).astype(BF16)
        z = input_projection(x, mod, norm1_g[l], w_mix)
        kv2 = lambda t: t.reshape(DEC_BATCH, PAST_LEN, 128)
        o_ab_ctx, bk_ctx = ctx_attention(z, a_sink[l], b_qnorm_g[l], b_knorm_g[l])
        o_a_lat = latent_attention_a(z, CTX_TOK, DEC_BATCH, DEC_SEQ, a_sink[l], kv2(cache_attn_a_k[:, l]),
                                     kv2(cache_attn_a_v[:, l]), rope_c, rope_s)
        o_b_lat = latent_attention_b(z, CTX_TOK, DEC_BATCH, DEC_SEQ, kv2(cache_attn_b_k[:, l]),
                                     kv2(cache_attn_b_v[:, l]), rope_c, rope_s, b_qnorm_g[l], b_knorm_g[l])
        o_c_ctx, sc_t = hgrn_mixer(z, 0, BATCH, SEQ, lower_bounds[l], c_onorm_g[l],
                                   jnp.zeros((BATCH, 2, C_HEADS, HEAD_DIM, HEAD_DIM), F32))
        o_c_lat, _ = hgrn_mixer(z, CTX_TOK, DEC_BATCH, DEC_SEQ, lower_bounds[l], c_onorm_g[l],
                                jnp.swapaxes(state_hgrn[:, l], -1, -2))

        qkv, gates = delta_prep(z, d_conv[l], d_a_log[l], d_dt_bias[l])
        u2, wq, ak = delta_chunks(qkv, gates)
        of_ctx, ob_ctx, sd = delta_scan(u2, wq, ak, 0, BATCH, SEQ,
                                        jnp.zeros((BATCH, 2, D_HEADS, HEAD_DIM, HEAD_DIM), F32))
        of_lat, ob_lat, _ = delta_scan(u2, wq, ak, CTX_TOK, DEC_BATCH, DEC_SEQ, state_delta[:, l])
        o_d_ctx = delta_output(of_ctx, ob_ctx, z, 0, d_onorm_g[l])
        o_d_lat = delta_output(of_lat, ob_lat, z, CTX_TOK, d_onorm_g[l])
        kvh = lambda t: t.reshape(BATCH, SEQ, 2, HEAD_DIM)
        caches.append((kvh(z[:CTX_TOK, 256:384]), kvh(z[:CTX_TOK, 384:512]), kvh(bk_ctx), kvh(z[:CTX_TOK, 896:1024]),
                       jnp.swapaxes(sc_t, -1, -2), sd))
        x = merge_projection(x, mod, norm1_g[l], (o_ab_ctx, o_c_ctx, o_d_ctx), (o_a_lat, o_b_lat, o_c_lat, o_d_lat),
                             w_gl, w_branch[l].astype(BF16), w_out[l].astype(BF16))
        j = l // 2
        if l % 2 == 0:
            x = dense_ffn(x, mod, norm2_g[l], ffn_w1[j].astype(BF16), ffn_w3[j].astype(BF16),
                          ffn_w2[j].astype(BF16))
        else:
            assert l == DEPTH - 1, "the expert layer's residual is fused with the final norm"
            rw = jnp.concatenate([router_w[j], jnp.zeros((D_MODEL, 128 - N_EXPERTS), F32)], axis=1)
            rb = jnp.concatenate([router_b[j], jnp.zeros((128 - N_EXPERTS,), F32)])[None, :]
            h2, pos, wgt = moe_router(x, mod, norm2_g[l], rw, rb)
            f = moe_experts(h2, pos, wgt, moe_w1[j].astype(BF16), moe_w3[j].astype(BF16), moe_w2[j].astype(BF16))
            y_prompt = residual_final_norm(x, f, mod, final_norm_g, 0, CTX_TOK).reshape(BATCH, SEQ, D_MODEL)
            y_sample = residual_final_norm(x, f, mod, final_norm_g, CTX_TOK, LAT_TOK).reshape(DEC_BATCH, DEC_SEQ, D_MODEL)

    stack = lambda idx: jnp.stack([caches[l][idx] for l in range(DEPTH)], axis=1)
    return (y_prompt, y_sample, stack(0), stack(1), stack(2), stack(3), stack(4), stack(5))
```

```python
import functools

import jax
import jax.numpy as jnp
import numpy as np
from jax import lax
from jax.experimental import pallas as pl
from jax.experimental.pallas import tpu as pltpu

F32 = jnp.float32
BF16 = jnp.bfloat16

D_MODEL = 1024
BATCH = 32
SEQ = 256
DEPTH = 2
DEC_BATCH = 8
DEC_SEQ = 4096
PAST_LEN = 256
GRID_W = 64
HEAD_DIM = 64
A_HEADS = 4
A_KV_HEADS = 2
B_HEADS = 4
B_KV_HEADS = 2
C_HEADS = 4
D_HEADS = 4
BRANCH_W = 256
N_BRANCH = 4
WINDOW = 128
BLOCK = 128
ROPE_BASE = 10000.0
HGRN_CHUNK = 32
DELTA_CHUNK = 64
CONV_K = 5
D_FF = 2816
N_EXPERTS = 8
D_FF_EXPERT = 3584
EPS = 1e-6
NEG_INF = -1e30
F32_MIN = float(np.finfo(np.float32).min)

CTX_TOK = BATCH * SEQ
LAT_TOK = DEC_BATCH * DEC_SEQ
N_TOK = CTX_TOK + LAT_TOK
N_COND = 1 + DEC_BATCH

Z_MAIN = 3072
Z_DG = Z_MAIN
Z_DAB = Z_DG + BRANCH_W
Z_COLS = Z_DAB + 128
W_IN_MIX = 3344

TM = 512
VMEM_LIMIT = 56 * 1024 * 1024


def _tile_cond(i, tm):
    ctx_tiles = CTX_TOK // tm
    per_b = DEC_SEQ // tm
    return jnp.where(i < ctx_tiles, 0, 1 + (i - ctx_tiles) // per_b)


def _rms(x, g):
    return x * lax.rsqrt(jnp.mean(x * x, axis=-1, keepdims=True) + EPS) * g


def _params(sem):
    return pltpu.CompilerParams(dimension_semantics=sem, vmem_limit_bytes=VMEM_LIMIT)


def _ada_kernel(c_ref, w_ref, b_ref, o_ref):
    c = c_ref[...]
    s = c * jax.nn.sigmoid(c)
    o_ref[...] = jnp.dot(s.astype(BF16), w_ref[...].astype(BF16), preferred_element_type=F32) + b_ref[...]


def ada_modulation(cond_pad, w, b):
    n = 6 * D_MODEL
    tn = 1536
    return pl.pallas_call(
        _ada_kernel,
        grid=(n // tn,),
        in_specs=[pl.BlockSpec((16, D_MODEL), lambda j: (0, 0)),
                  pl.BlockSpec((D_MODEL, tn), lambda j: (0, j)),
                  pl.BlockSpec((1, tn), lambda j: (0, j))],
        out_specs=pl.BlockSpec((16, tn), lambda j: (0, j)),
        out_shape=jax.ShapeDtypeStruct((16, n), F32),
        compiler_params=_params(("arbitrary",)),
        name="ada_modulation",
    )(cond_pad, w, b.reshape(1, n))


def _in_kernel(x_ref, mod_ref, g_ref, w_ref, z_ref):
    h = _rms(x_ref[...], g_ref[...]) * (1.0 + mod_ref[1:2, :]) + mod_ref[0:1, :]
    z_ref[...] = jnp.dot(h.astype(BF16), w_ref[...], preferred_element_type=F32)


def input_projection(x, mod, g, w):
    nt = N_TOK // TM
    return pl.pallas_call(
        _in_kernel,
        grid=(nt,),
        in_specs=[pl.BlockSpec((TM, D_MODEL), lambda i: (i, 0)),
                  pl.BlockSpec((None, 6, D_MODEL), lambda i: (_tile_cond(i, TM), 0, 0)),
                  pl.BlockSpec((1, D_MODEL), lambda i: (0, 0)),
                  pl.BlockSpec((D_MODEL, Z_COLS), lambda i: (0, 0))],
        out_specs=pl.BlockSpec((TM, Z_COLS), lambda i: (i, 0)),
        out_shape=jax.ShapeDtypeStruct((N_TOK, Z_COLS), F32),
        compiler_params=_params(("arbitrary",)),
        name="input_projection",
    )(x, mod, g.reshape(1, D_MODEL), w)


def _merge_kernel(x_ref, mod_ref, g_ref, ab_c_ref, c_c_ref, d_c_ref, a_l_ref, b_l_ref, c_l_ref, d_l_ref,
                  wgl_ref, wbr_ref, wout_ref, xo_ref):
    x = x_ref[...]
    h = (_rms(x, g_ref[...]) * (1.0 + mod_ref[1:2, :]) + mod_ref[0:1, :]).astype(BF16)
    is_ctx = pl.program_id(0) < CTX_TOK // TM
    branches = (jnp.where(is_ctx, ab_c_ref[:, :BRANCH_W], a_l_ref[...]),
                jnp.where(is_ctx, ab_c_ref[:, BRANCH_W:], b_l_ref[...]),
                jnp.where(is_ctx, c_c_ref[...], c_l_ref[...]),
                jnp.where(is_ctx, d_c_ref[...], d_l_ref[...]))
    merged = None
    for j in range(N_BRANCH):
        gate = jax.nn.sigmoid(jnp.dot(h, wgl_ref[j], preferred_element_type=F32))
        br = jnp.dot(branches[j], wbr_ref[j], preferred_element_type=F32)
        merged = gate * br if merged is None else merged + gate * br
    mix = jnp.dot(merged.astype(BF16), wout_ref[...], preferred_element_type=F32)
    xo_ref[...] = x + mod_ref[2:3, :] * mix


def merge_projection(x, mod, g, o_ctx, o_lat, wgl, wbr, wout):
    nt = N_TOK // TM
    ctx_tiles = CTX_TOK // TM
    ctx_spec = lambda w: pl.BlockSpec((TM, w), lambda i: (jnp.minimum(i, ctx_tiles - 1), 0))
    lat_spec = pl.BlockSpec((TM, BRANCH_W), lambda i: (jnp.maximum(i - ctx_tiles, 0), 0))
    return pl.pallas_call(
        _merge_kernel,
        grid=(nt,),
        in_specs=[pl.BlockSpec((TM, D_MODEL), lambda i: (i, 0)),
                  pl.BlockSpec((None, 6, D_MODEL), lambda i: (_tile_cond(i, TM), 0, 0)),
                  pl.BlockSpec((1, D_MODEL), lambda i: (0, 0)),
                  ctx_spec(2 * BRANCH_W), ctx_spec(BRANCH_W), ctx_spec(BRANCH_W),
                  lat_spec, lat_spec, lat_spec, lat_spec,
                  pl.BlockSpec((N_BRANCH, D_MODEL, D_MODEL), lambda i: (0, 0, 0)),
                  pl.BlockSpec((N_BRANCH, BRANCH_W, D_MODEL), lambda i: (0, 0, 0)),
                  pl.BlockSpec((D_MODEL, D_MODEL), lambda i: (0, 0))],
        out_specs=pl.BlockSpec((TM, D_MODEL), lambda i: (i, 0)),
        out_shape=jax.ShapeDtypeStruct((N_TOK, D_MODEL), F32),
        compiler_params=_params(("arbitrary",)),
        name="merge_projection",
    )(x, mod, g.reshape(1, D_MODEL), *o_ctx, *o_lat, wgl, wbr, wout)


def _ffn_kernel(x_ref, mod_ref, g_ref, w1_ref, w3_ref, w2_ref, xo_ref):
    x = x_ref[...]
    h = (_rms(x, g_ref[...]) * (1.0 + mod_ref[4:5, :]) + mod_ref[3:4, :]).astype(BF16)
    a = jnp.dot(h, w1_ref[...], preferred_element_type=F32)
    b = jnp.dot(h, w3_ref[...], preferred_element_type=F32)
    hid = (a * jax.nn.sigmoid(a) * b).astype(BF16)
    f = jnp.dot(hid, w2_ref[...], preferred_element_type=F32)
    xo_ref[...] = x + mod_ref[5:6, :] * f


def dense_ffn(x, mod, g, w1, w3, w2):
    nt = N_TOK // TM
    const = lambda i: (0, 0)
    return pl.pallas_call(
        _ffn_kernel,
        grid=(nt,),
        in_specs=[pl.BlockSpec((TM, D_MODEL), lambda i: (i, 0)),
                  pl.BlockSpec((None, 6, D_MODEL), lambda i: (_tile_cond(i, TM), 0, 0)),
                  pl.BlockSpec((1, D_MODEL), const),
                  pl.BlockSpec((D_MODEL, D_FF), const, pipeline_mode=pl.Buffered(1)),
                  pl.BlockSpec((D_MODEL, D_FF), const, pipeline_mode=pl.Buffered(1)),
                  pl.BlockSpec((D_FF, D_MODEL), const, pipeline_mode=pl.Buffered(1))],
        out_specs=pl.BlockSpec((TM, D_MODEL), lambda i: (i, 0)),
        out_shape=jax.ShapeDtypeStruct((N_TOK, D_MODEL), F32),
        compiler_params=_params(("arbitrary",)),
        name="dense_ffn",
    )(x, mod, g.reshape(1, D_MODEL), w1, w3, w2)


MOE_T = 1024
MOE_R = 144
MOE_SR = 128
MOE_F = 1792
MOE_CAP = -(-MOE_T // MOE_R) * MOE_R


def _router_kernel(x_ref, mod_ref, g_ref, rw_ref, rb_ref, h_ref, pos_ref, wgt_ref):
    t = x_ref.shape[0]
    h = _rms(x_ref[...], g_ref[...]) * (1.0 + mod_ref[4:5, :]) + mod_ref[3:4, :]
    h_ref[...] = h.astype(BF16)
    logits = jnp.dot(h, rw_ref[...], preferred_element_type=F32, precision=lax.Precision.HIGHEST) + rb_ref[...]
    lt = logits.T[:N_EXPERTS, :]
    eidx = lax.broadcasted_iota(jnp.int32, lt.shape, 0)
    m1 = jnp.max(lt, axis=0, keepdims=True)
    i1 = jnp.min(jnp.where(lt == m1, eidx, N_EXPERTS), axis=0, keepdims=True)
    rest = jnp.where(eidx == i1, F32_MIN, lt)
    m2 = jnp.max(rest, axis=0, keepdims=True)
    i2 = jnp.min(jnp.where(rest == m2, eidx, N_EXPERTS), axis=0, keepdims=True)
    e2 = jnp.exp(m2 - m1)
    p1 = 1.0 / (1.0 + e2)
    p2 = e2 / (1.0 + e2)
    wgt_ref[...] = jnp.where(eidx == i1, p1, 0.0) + jnp.where(eidx == i2, p2, 0.0)
    routed = jnp.where(eidx == i1, 1.0, jnp.where(eidx == i2, 1.0, 0.0))
    s_id = lax.broadcasted_iota(jnp.int32, (t, t), 0)
    t_id = lax.broadcasted_iota(jnp.int32, (t, t), 1)
    before = jnp.where(s_id < t_id, 1.0, 0.0).astype(BF16)
    rank = jnp.dot(routed.astype(BF16), before, preferred_element_type=F32)
    pos_ref[...] = jnp.where(routed > 0.0, rank.astype(jnp.int32), -1)


def moe_router(x, mod, g, rw, rb):
    nt = N_TOK // MOE_T
    return pl.pallas_call(
        _router_kernel,
        grid=(nt,),
        in_specs=[pl.BlockSpec((MOE_T, D_MODEL), lambda i: (i, 0)),
                  pl.BlockSpec((None, 6, D_MODEL), lambda i: (_tile_cond(i, MOE_T), 0, 0)),
                  pl.BlockSpec((1, D_MODEL), lambda i: (0, 0)),
                  pl.BlockSpec((D_MODEL, 128), lambda i: (0, 0)),
                  pl.BlockSpec((1, 128), lambda i: (0, 0))],
        out_specs=[pl.BlockSpec((MOE_T, D_MODEL), lambda i: (i, 0)),
                   pl.BlockSpec((N_EXPERTS, MOE_T), lambda i: (0, i)),
                   pl.BlockSpec((N_EXPERTS, MOE_T), lambda i: (0, i))],
        out_shape=[jax.ShapeDtypeStruct((N_TOK, D_MODEL), BF16),
                   jax.ShapeDtypeStruct((N_EXPERTS, N_TOK), jnp.int32),
                   jax.ShapeDtypeStruct((N_EXPERTS, N_TOK), F32)],
        compiler_params=_params(("arbitrary",)),
        name="moe_router",
    )(x, mod, g.reshape(1, D_MODEL), rw, rb)


def _moe_sparse_kernel(h_ref, pos_ref, wgt_ref, w1_ref, w3_ref, w2_ref, y_ref, xg_scr, acc_scr, wr_scr):
    e = pl.program_id(1)
    f = pl.program_id(2)
    t = h_ref.shape[0]
    pos_e = pos_ref[pl.ds(e, 1), :]
    n_rows = jnp.max(pos_e) + 1
    n_blocks = (n_rows + MOE_R - 1) // MOE_R
    n_sc_blocks = (n_rows + MOE_SR - 1) // MOE_SR
    n_init_blocks = jnp.maximum(n_blocks, (n_sc_blocks * MOE_SR + MOE_R - 1) // MOE_R)
    row_id = lax.broadcasted_iota(jnp.int32, (MOE_R, t), 0)

    def block_rows(r):
        return pl.ds(pl.multiple_of(r * MOE_R, 16), MOE_R)

    def selects(r):
        return pos_e == row_id + r * MOE_R

    @pl.when((e == 0) & (f == 0))
    def _():
        y_ref[...] = jnp.zeros_like(y_ref)

    @pl.when(f == 0)
    def _():
        wgt_e = wgt_ref[pl.ds(e, 1), :]

        def gather(r, carry):
            sel = selects(r)
            xg = jnp.dot(jnp.where(sel, 1.0, 0.0).astype(BF16), h_ref[...], preferred_element_type=F32)
            xg_scr[block_rows(r), :] = xg.astype(BF16)
            w_rows = jnp.sum(jnp.where(sel, wgt_e, 0.0), axis=1, keepdims=True)
            wr_scr[block_rows(r), :] = jnp.broadcast_to(w_rows, (MOE_R, 128))
            acc_scr[block_rows(r), :] = jnp.zeros((MOE_R, D_MODEL), F32)
            return carry

        lax.fori_loop(0, n_init_blocks, gather, 0)

    def expert(r, carry):
        xg = xg_scr[block_rows(r), :]
        a = jnp.dot(xg, w1_ref[...], preferred_element_type=F32)
        b = jnp.dot(xg, w3_ref[...], preferred_element_type=F32)
        hid = (a * jax.nn.sigmoid(a) * b * wr_scr[block_rows(r), 0:1]).astype(BF16)
        acc_scr[block_rows(r), :] += jnp.dot(hid, w2_ref[...], preferred_element_type=F32)
        return carry

    lax.fori_loop(0, n_blocks, expert, 0)

    @pl.when(f == pl.num_programs(2) - 1)
    def _():
        sc_row_id = lax.broadcasted_iota(jnp.int32, (MOE_SR, t), 0)

        def scatter(r, carry):
            rows = pl.ds(pl.multiple_of(r * MOE_SR, MOE_SR), MOE_SR)
            onehot = jnp.where(pos_e == sc_row_id + r * MOE_SR, 1.0, 0.0).astype(BF16)
            hi, lo = _split_bf16(acc_scr[rows, :])
            y_ref[...] += lax.dot_general(jnp.concatenate([onehot, onehot], axis=0),
                                          jnp.concatenate([hi, lo], axis=0), _TN, preferred_element_type=F32)
            return carry

        lax.fori_loop(0, n_sc_blocks, scatter, 0)


def moe_experts(h2, pos, wgt, w1, w3, w2):
    nt = N_TOK // MOE_T
    nf = D_FF_EXPERT // MOE_F
    return pl.pallas_call(
        _moe_sparse_kernel,
        grid=(nt, N_EXPERTS, nf),
        in_specs=[pl.BlockSpec((MOE_T, D_MODEL), lambda i, e, f: (i, 0)),
                  pl.BlockSpec((N_EXPERTS, MOE_T), lambda i, e, f: (0, i)),
                  pl.BlockSpec((N_EXPERTS, MOE_T), lambda i, e, f: (0, i)),
                  pl.BlockSpec((None, D_MODEL, MOE_F), lambda i, e, f: (e, 0, f)),
                  pl.BlockSpec((None, D_MODEL, MOE_F), lambda i, e, f: (e, 0, f)),
                  pl.BlockSpec((None, MOE_F, D_MODEL), lambda i, e, f: (e, f, 0))],
        out_specs=pl.BlockSpec((MOE_T, D_MODEL), lambda i, e, f: (i, 0)),
        out_shape=jax.ShapeDtypeStruct((N_TOK, D_MODEL), F32),
        scratch_shapes=[pltpu.VMEM((MOE_CAP, D_MODEL), BF16),
                        pltpu.VMEM((MOE_CAP, D_MODEL), F32),
                        pltpu.VMEM((MOE_CAP, 128), F32)],
        compiler_params=_params(("arbitrary", "arbitrary", "arbitrary")),
        name="moe_experts",
    )(h2, pos, wgt, w1, w3, w2)


def _residual_norm_kernel(x_ref, y_ref, mod_ref, g_ref, o_ref):
    o_ref[...] = _rms(x_ref[...] + mod_ref[5:6, :] * y_ref[...], g_ref[...])


def residual_final_norm(x, y, mod, g, row0, n_rows):
    tm = 1024
    t0 = row0 // tm
    return pl.pallas_call(
        _residual_norm_kernel,
        grid=(n_rows // tm,),
        in_specs=[pl.BlockSpec((tm, D_MODEL), lambda i: (t0 + i, 0)),
                  pl.BlockSpec((tm, D_MODEL), lambda i: (t0 + i, 0)),
                  pl.BlockSpec((None, 6, D_MODEL), lambda i: (_tile_cond(t0 + i, tm), 0, 0)),
                  pl.BlockSpec((1, D_MODEL), lambda i: (0, 0))],
        out_specs=pl.BlockSpec((tm, D_MODEL), lambda i: (i, 0)),
        out_shape=jax.ShapeDtypeStruct((n_rows, D_MODEL), F32),
        compiler_params=_params(("arbitrary",)),
        name="residual_final_norm",
    )(x, y, mod, g.reshape(1, D_MODEL))


ATT_SCALE = HEAD_DIM ** -0.5
LOG2_E = 1.4426950408889634
_NT = (((1,), (1,)), ((), ()))


def _head_rms(x, g_row):
    outs = []
    for h in range(x.shape[1] // HEAD_DIM):
        xh = x[:, h * HEAD_DIM:(h + 1) * HEAD_DIM]
        outs.append(xh * lax.rsqrt(jnp.mean(xh * xh, axis=-1, keepdims=True) + EPS) * g_row)
    return jnp.concatenate(outs, axis=-1)


def _rope_apply(x, c, s):
    w = x.shape[-1]
    lane = lax.broadcasted_iota(jnp.int32, x.shape, 1)
    first_half = ((lane // (HEAD_DIM // 4)) % 2) == 0
    partner = jnp.where(first_half, pltpu.roll(x, w - HEAD_DIM // 4, 1), pltpu.roll(x, HEAD_DIM // 4, 1))
    return x * c + partner * s


def rope_lane_tables(L):
    rows = L // GRID_W
    row = jnp.repeat(jnp.arange(rows, dtype=F32), GRID_W)
    col = jnp.tile(jnp.arange(GRID_W, dtype=F32), rows)
    n_freq = HEAD_DIM // 4
    inv = ROPE_BASE ** (-jnp.arange(n_freq, dtype=F32) / n_freq)
    ang = jnp.stack([row, col], 0)[:, :, None] * inv
    cos, sin = jnp.cos(ang), jnp.sin(ang)
    c = jnp.concatenate([cos[0], cos[0], cos[1], cos[1]], axis=-1)
    s = jnp.concatenate([-sin[0], sin[0], -sin[1], sin[1]], axis=-1)
    return jnp.tile(c, (1, 4)), jnp.tile(s, (1, 4))


def _with_ones(v):
    ones = jnp.ones((v.shape[0], HEAD_DIM), BF16)
    parts = []
    for h in range(v.shape[1] // HEAD_DIM):
        parts += [v[:, h * HEAD_DIM:(h + 1) * HEAD_DIM].astype(BF16), ones]
    return jnp.concatenate(parts, axis=-1)


def _attend_heads(jobs):
    for job in jobs:
        q = (job['q'] * (ATT_SCALE * LOG2_E)).astype(BF16)
        job['s'] = lax.dot_general(q, job['k'], _NT, preferred_element_type=F32)
        if job.get('extra') is not None:
            job['s2'] = lax.dot_general(q, job['extra'][0], _NT, preferred_element_type=F32)
    outs = []
    for job in jobs:
        s, sink = job['s'], job.get('sink')
        if sink is not None:
            sink = sink * LOG2_E
        if job.get('mask') is not None:
            s = jnp.where(job['mask'], s, NEG_INF)
        m = jnp.max(s, axis=-1, keepdims=True)
        if 's2' in job:
            m = jnp.maximum(m, jnp.max(job['s2'], axis=-1, keepdims=True))
        if sink is not None:
            m = jnp.maximum(m, sink)
        o = jnp.dot(jnp.exp2(s - m).astype(BF16), job['v'], preferred_element_type=F32)
        if 's2' in job:
            o = o + jnp.dot(jnp.exp2(job['s2'] - m).astype(BF16), job['extra'][1], preferred_element_type=F32)
        den = o[:, HEAD_DIM:HEAD_DIM + 1]
        if sink is not None:
            den = den + jnp.exp2(sink - m)
        outs.append(o[:, :HEAD_DIM] / den)
    return outs


def _ctx_attn_kernel(sink_ref, z_ref, gq_ref, gk_ref, o_ref, bk_ref):
    z = z_ref[...]
    bq = _head_rms(z[:, 512:768], gq_ref[...])
    bk = _head_rms(z[:, 768:896], gk_ref[...])
    bk_ref[...] = bk
    groups = ((z[:, 0:256], z[:, 256:384], z[:, 384:512], True),
              (bq, bk, z[:, 896:1024], False))
    outs = []
    for q_all, k_all, v_all, use_sink in groups:
        k_all = k_all.astype(BF16)
        v_all = _with_ones(v_all)
        for hq in range(A_HEADS):
            kv = hq // (A_HEADS // A_KV_HEADS)
            outs.append(dict(q=q_all[:, hq * HEAD_DIM:(hq + 1) * HEAD_DIM],
                             k=k_all[:, kv * HEAD_DIM:(kv + 1) * HEAD_DIM],
                             v=v_all[:, kv * 2 * HEAD_DIM:(kv + 1) * 2 * HEAD_DIM],
                             sink=sink_ref[hq] if use_sink else None))
    o_ref[...] = jnp.concatenate(_attend_heads(outs), axis=-1).astype(o_ref.dtype)


def ctx_attention(z, sink, gq, gk):
    return pl.pallas_call(
        _ctx_attn_kernel,
        grid=(BATCH,),
        in_specs=[pl.BlockSpec(memory_space=pltpu.SMEM),
                  pl.BlockSpec((SEQ, 1024), lambda b: (b, 0)),
                  pl.BlockSpec((1, HEAD_DIM), lambda b: (0, 0)),
                  pl.BlockSpec((1, HEAD_DIM), lambda b: (0, 0))],
        out_specs=[pl.BlockSpec((SEQ, 512), lambda b: (b, 0)),
                   pl.BlockSpec((SEQ, 128), lambda b: (b, 0))],
        out_shape=[jax.ShapeDtypeStruct((CTX_TOK, 512), BF16),
                   jax.ShapeDtypeStruct((CTX_TOK, 128), F32)],
        compiler_params=_params(("arbitrary",)),
        name="ctx_attention",
    )(sink, z, gq.reshape(1, HEAD_DIM), gk.reshape(1, HEAD_DIM))


LB_TQ = 256


def _lat_b_kernel(zq_ref, zkv_ref, ck_ref, cv_ref, cq_ref, sq_ref, ckk_ref, skk_ref, gq_ref, gk_ref,
                  o_ref, k_scr, v_scr):
    L = zkv_ref.shape[0]

    @pl.when(pl.program_id(1) == 0)
    def _():
        kv = zkv_ref[...]
        bk = _rope_apply(_head_rms(kv[:, :128], gk_ref[...]), ckk_ref[...], skk_ref[...])
        k_scr[0:L, :] = bk.astype(BF16)
        k_scr[L:L + PAST_LEN, :] = ck_ref[...].astype(BF16)
        v_scr[0:L, :] = _with_ones(kv[:, 128:])
        v_scr[L:L + PAST_LEN, :] = _with_ones(cv_ref[...])

    q = _rope_apply(_head_rms(zq_ref[...], gq_ref[...]), cq_ref[...], sq_ref[...])
    outs = []
    for hq in range(B_HEADS):
        kv = hq // (B_HEADS // B_KV_HEADS)
        sl = slice(kv * HEAD_DIM, (kv + 1) * HEAD_DIM)
        outs.append(dict(q=q[:, hq * HEAD_DIM:(hq + 1) * HEAD_DIM], k=k_scr[:, sl],
                         v=v_scr[:, kv * 2 * HEAD_DIM:(kv + 1) * 2 * HEAD_DIM]))
    o_ref[...] = jnp.concatenate(_attend_heads(outs), axis=-1).astype(o_ref.dtype)


def latent_attention_b(z, row0, nb, L, cache_k, cache_v, rope_c, rope_s, gq, gk):
    nq = L // LB_TQ
    return pl.pallas_call(
        _lat_b_kernel,
        grid=(nb, nq),
        in_specs=[pl.BlockSpec((LB_TQ, 256), lambda b, i: (row0 // LB_TQ + b * nq + i, 2)),
                  pl.BlockSpec((L, 256), lambda b, i: (row0 // L + b, 3)),
                  pl.BlockSpec((None, PAST_LEN, 128), lambda b, i: (b, 0, 0)),
                  pl.BlockSpec((None, PAST_LEN, 128), lambda b, i: (b, 0, 0)),
                  pl.BlockSpec((LB_TQ, 256), lambda b, i: (i, 0)),
                  pl.BlockSpec((LB_TQ, 256), lambda b, i: (i, 0)),
                  pl.BlockSpec((L, 128), lambda b, i: (0, 0)),
                  pl.BlockSpec((L, 128), lambda b, i: (0, 0)),
                  pl.BlockSpec((1, HEAD_DIM), lambda b, i: (0, 0)),
                  pl.BlockSpec((1, HEAD_DIM), lambda b, i: (0, 0))],
        out_specs=pl.BlockSpec((LB_TQ, 256), lambda b, i: (b * nq + i, 0)),
        out_shape=jax.ShapeDtypeStruct((nb * L, 256), BF16),
        scratch_shapes=[pltpu.VMEM((L + PAST_LEN, 128), BF16),
                        pltpu.VMEM((L + PAST_LEN, 256), BF16)],
        compiler_params=_params(("arbitrary", "arbitrary")),
        name="latent_attention_b",
    )(z, z, cache_k, cache_v, rope_c, rope_s, rope_c, rope_s, gq.reshape(1, HEAD_DIM), gk.reshape(1, HEAD_DIM))


def _lat_a_kernel(sink_ref, zq_ref, zkv_ref, ck_ref, cv_ref, cq_ref, sq_ref, ckk_ref, skk_ref,
                  o_ref, k_scr, v_scr, ck_scr, cv_scr):
    L = zkv_ref.shape[0]
    i = pl.program_id(1)

    @pl.when(i == 0)
    def _():
        kv = zkv_ref[...]
        k_scr[0:BLOCK, :] = jnp.zeros((BLOCK, 128), BF16)
        v_scr[0:BLOCK, :] = jnp.zeros((BLOCK, 256), BF16)
        k_scr[BLOCK:BLOCK + L, :] = _rope_apply(kv[:, :128], ckk_ref[...], skk_ref[...]).astype(BF16)
        v_scr[BLOCK:BLOCK + L, :] = _with_ones(kv[:, 128:])
        k_scr[BLOCK + L:2 * BLOCK + L, :] = jnp.zeros((BLOCK, 128), BF16)
        v_scr[BLOCK + L:2 * BLOCK + L, :] = jnp.zeros((BLOCK, 256), BF16)
        ck_scr[...] = ck_ref[...].astype(BF16)
        cv_scr[...] = _with_ones(cv_ref[...])

    q = _rope_apply(zq_ref[...], cq_ref[...], sq_ref[...])
    start = pl.multiple_of(i * BLOCK, BLOCK)
    kband = k_scr[pl.ds(start, 3 * BLOCK), :]
    vband = v_scr[pl.ds(start, 3 * BLOCK), :]
    r = lax.broadcasted_iota(jnp.int32, (BLOCK, 3 * BLOCK), 0)
    cidx = lax.broadcasted_iota(jnp.int32, (BLOCK, 3 * BLOCK), 1)
    kpos = i * BLOCK - BLOCK + cidx
    mask = (jnp.abs(cidx - BLOCK - r) <= WINDOW) & (kpos >= 0) & (kpos < L)
    outs = []
    for hq in range(A_HEADS):
        kv = hq // (A_HEADS // A_KV_HEADS)
        sl = slice(kv * HEAD_DIM, (kv + 1) * HEAD_DIM)
        sv = slice(kv * 2 * HEAD_DIM, (kv + 1) * 2 * HEAD_DIM)
        outs.append(dict(q=q[:, hq * HEAD_DIM:(hq + 1) * HEAD_DIM], k=kband[:, sl], v=vband[:, sv],
                         extra=(ck_scr[:, sl], cv_scr[:, sv]), sink=sink_ref[hq], mask=mask))
    o_ref[...] = jnp.concatenate(_attend_heads(outs), axis=-1).astype(o_ref.dtype)


def latent_attention_a(z, row0, nb, L, sink, cache_k, cache_v, rope_c, rope_s):
    nq = L // BLOCK
    return pl.pallas_call(
        _lat_a_kernel,
        grid=(nb, nq),
        in_specs=[pl.BlockSpec(memory_space=pltpu.SMEM),
                  pl.BlockSpec((BLOCK, 256), lambda b, i: (row0 // BLOCK + b * nq + i, 0)),
                  pl.BlockSpec((L, 256), lambda b, i: (row0 // L + b, 1)),
                  pl.BlockSpec((None, PAST_LEN, 128), lambda b, i: (b, 0, 0)),
                  pl.BlockSpec((None, PAST_LEN, 128), lambda b, i: (b, 0, 0)),
                  pl.BlockSpec((BLOCK, 256), lambda b, i: (i, 0)),
                  pl.BlockSpec((BLOCK, 256), lambda b, i: (i, 0)),
                  pl.BlockSpec((L, 128), lambda b, i: (0, 0)),
                  pl.BlockSpec((L, 128), lambda b, i: (0, 0))],
        out_specs=pl.BlockSpec((BLOCK, 256), lambda b, i: (b * nq + i, 0)),
        out_shape=jax.ShapeDtypeStruct((nb * L, 256), BF16),
        scratch_shapes=[pltpu.VMEM((L + 2 * BLOCK, 128), BF16),
                        pltpu.VMEM((L + 2 * BLOCK, 256), BF16),
                        pltpu.VMEM((PAST_LEN, 128), BF16),
                        pltpu.VMEM((PAST_LEN, 256), BF16)],
        compiler_params=_params(("arbitrary", "arbitrary")),
        name="latent_attention_a",
    )(sink, z, z, cache_k, cache_v, rope_c, rope_s, rope_c, rope_s)


_TN = (((0,), (0,)), ((), ()))
HG_GROUP = 8


def _hgrn_kernel(zq_ref, zf_ref, zi_ref, zg_ref, lb_ref, gn_ref, s0_ref, o_ref, sT_ref,
                 of_scr, ob_scr, g_scr, k_scr, qin_scr, kin_scr, v_scr, S_scr, *, tt):
    d = pl.program_id(1)
    j = pl.program_id(2)
    n_t = pl.num_programs(2)
    C = HGRN_CHUNK
    n_c = tt // C

    @pl.when(j == 0)
    def _():
        S_scr[...] = s0_ref[...]

    lb = lb_ref[...]
    sg = jax.nn.sigmoid(zf_ref[...])
    logf = jnp.log(lb + (1.0 - lb) * sg)
    k = (1.0 - lb) * (1.0 - sg)
    k_scr[...] = k
    v_scr[...] = zi_ref[...].astype(BF16)
    in_chunk = lax.broadcasted_iota(jnp.int32, (tt, C_HEADS * HEAD_DIM), 0) % C
    row = lax.broadcasted_iota(jnp.int32, (C, C), 0)
    col = lax.broadcasted_iota(jnp.int32, (C, C), 1)
    heads = [slice(h * HEAD_DIM, (h + 1) * HEAD_DIM) for h in range(C_HEADS)]

    def run(reverse, tile):
        G = logf
        step = 1
        while step < C:
            if reverse:
                G = G + jnp.where(in_chunk < C - step, pltpu.roll(G, tt - step, 0), 0.0)
            else:
                G = G + jnp.where(in_chunk >= step, pltpu.roll(G, step, 0), 0.0)
            step *= 2
        g_scr[...] = G
        qin_scr[...] = (zq_ref[...] * jnp.exp(G)).astype(BF16)
        kin_scr[...] = (k * jnp.exp(-G)).astype(BF16)
        tri = (row <= col) if reverse else (row >= col)

        def body(gi, carry):
            chunks = []
            for g in range(HG_GROUP):
                ci = gi * HG_GROUP + g
                c = (n_c - 1 - ci) if reverse else ci
                r0 = pl.multiple_of(c * C, C)
                rows = pl.ds(r0, C)
                G_c = g_scr[rows, :]
                G_end = G_c[0:1, :] if reverse else G_c[C - 1:C, :]
                vc = v_scr[rows, :]
                q_in = qin_scr[rows, :]
                k_in = kin_scr[rows, :]
                k_out = (k_scr[rows, :] * jnp.exp(G_end - G_c)).astype(BF16)
                chunks.append(dict(
                    r0=r0, rows=rows, vc=vc, q_in=q_in, decay=jnp.exp(G_end),
                    a=[lax.dot_general(q_in[:, sl], k_in[:, sl], _NT, preferred_element_type=F32) for sl in heads],
                    kv=[lax.dot_general(vc[:, sl], k_out[:, sl], _TN, preferred_element_type=F32) for sl in heads]))
            s_cur = [S_scr[h] for h in range(C_HEADS)]
            for ch in chunks:
                ch['qs'] = [lax.dot_general(ch['q_in'][:, sl], s_cur[h].astype(BF16), _NT,
                                            preferred_element_type=F32) for h, sl in enumerate(heads)]
                s_cur = [s_cur[h] * ch['decay'][:, sl] + ch['kv'][h] for h, sl in enumerate(heads)]
            for h in range(C_HEADS):
                S_scr[h] = s_cur[h]
            for ch in chunks:
                o_c = jnp.concatenate(
                    [jnp.dot(jnp.where(tri, ch['a'][h], 0.0).astype(BF16), ch['vc'][:, sl],
                             preferred_element_type=F32) + ch['qs'][h] for h, sl in enumerate(heads)], axis=-1)
                if reverse:
                    ob_scr[ch['rows'], :] = o_c
                else:
                    of_scr[pl.ds(pl.multiple_of(tile * tt, tt) + ch['r0'], C), :] = o_c
            return carry

        lax.fori_loop(0, n_c // HG_GROUP, body, 0)

    @pl.when(d == 0)
    def _():
        run(False, j)

    @pl.when(d == 1)
    def _():
        tile = n_t - 1 - j
        run(True, tile)
        o = of_scr[pl.ds(pl.multiple_of(tile * tt, tt), tt), :] + ob_scr[...]
        g = zg_ref[...]
        o_ref[...] = (_head_rms(o, gn_ref[...]) * (g * jax.nn.sigmoid(g))).astype(o_ref.dtype)

    @pl.when(j == n_t - 1)
    def _():
        sT_ref[...] = S_scr[...]


def hgrn_mixer(z, row0, nb, L, lb, gn, s0_t):
    tt = min(L, 512)
    n_t = L // tt
    rb = row0 // tt

    def tile(d, j):
        return jnp.where(d == 0, j, n_t - 1 - j)

    def late(d, j):
        return jnp.where(d == 0, n_t - 1, n_t - 1 - j)

    st_spec = pl.BlockSpec((None, None, C_HEADS, HEAD_DIM, HEAD_DIM), lambda b, d, j: (b, d, 0, 0, 0))
    return pl.pallas_call(
        functools.partial(_hgrn_kernel, tt=tt),
        grid=(nb, 2, n_t),
        in_specs=[pl.BlockSpec((tt, 256), lambda b, d, j: (rb + b * n_t + tile(d, j), 4)),
                  pl.BlockSpec((tt, 256), lambda b, d, j: (rb + b * n_t + tile(d, j), 5 + d)),
                  pl.BlockSpec((tt, 256), lambda b, d, j: (rb + b * n_t + tile(d, j), 7)),
                  pl.BlockSpec((tt, 256), lambda b, d, j: (rb + b * n_t + late(d, j), 8)),
                  pl.BlockSpec((1, 256), lambda b, d, j: (0, 0)),
                  pl.BlockSpec((1, HEAD_DIM), lambda b, d, j: (0, 0)),
                  st_spec],
        out_specs=[pl.BlockSpec((tt, 256), lambda b, d, j: (b * n_t + late(d, j), 0)),
                   st_spec],
        out_shape=[jax.ShapeDtypeStruct((nb * L, 256), BF16),
                   jax.ShapeDtypeStruct((nb, 2, C_HEADS, HEAD_DIM, HEAD_DIM), F32)],
        scratch_shapes=[pltpu.VMEM((L, 256), F32),
                        pltpu.VMEM((tt, 256), F32),
                        pltpu.VMEM((tt, 256), F32),
                        pltpu.VMEM((tt, 256), F32),
                        pltpu.VMEM((tt, 256), BF16),
                        pltpu.VMEM((tt, 256), BF16),
                        pltpu.VMEM((tt, 256), BF16),
                        pltpu.VMEM((C_HEADS, HEAD_DIM, HEAD_DIM), F32)],
        compiler_params=_params(("arbitrary", "arbitrary", "arbitrary")),
        name="hgrn_mixer",
    )(z, z, z, z, lb.reshape(1, 256), gn.reshape(1, HEAD_DIM), s0_t)


DL_C = DELTA_CHUNK
DL_PREP_TT = 256
DL_HALO = 8
DL_CHUNK_TT = 512
N_QKV_HEADS = 3 * D_HEADS
DL_GROUP = 2


def _delta_prep_kernel(x_ref, xp_ref, xn_ref, zab_ref, cw_ref, na_ref, dtb_ref, qkv_ref, gate_ref, xs_scr):
    tt = x_ref.shape[0]
    row = pl.program_id(0) * tt
    lat = row - CTX_TOK
    first = jnp.where(row < CTX_TOK, True, lat % DEC_SEQ == 0)
    last = jnp.where(row < CTX_TOK, True, (lat + tt) % DEC_SEQ == 0)
    xs_scr[DL_HALO:DL_HALO + tt, :] = x_ref[...]
    xs_scr[0:DL_HALO, :] = jnp.where(first, 0.0, xp_ref[...])
    xs_scr[DL_HALO + tt:2 * DL_HALO + tt, :] = jnp.where(last, 0.0, xn_ref[...])
    pad = (CONV_K - 1) // 2
    y = None
    for t in range(CONV_K):
        term = xs_scr[pl.ds(DL_HALO - pad + t, tt), :] * cw_ref[t:t + 1, :]
        y = term if y is None else y + term
    y = y * jax.nn.sigmoid(y)
    for idx in range(N_QKV_HEADS):
        xh = y[:, idx * HEAD_DIM:(idx + 1) * HEAD_DIM]
        if idx < 2 * D_HEADS:
            xh = xh * lax.rsqrt(jnp.sum(xh * xh, axis=-1, keepdims=True) + EPS)
        if idx < D_HEADS:
            xh = xh * ATT_SCALE
        qkv_ref[idx] = xh
    zab = zab_ref[...]
    lane = lax.broadcasted_iota(jnp.int32, zab.shape, 1)
    t_ = zab + dtb_ref[...]
    softplus = jnp.maximum(t_, 0.0) + jnp.log(1.0 + jnp.exp(-jnp.abs(t_)))
    gate_ref[...] = jnp.where(lane < 2 * D_HEADS, na_ref[...] * softplus, jax.nn.sigmoid(zab))


def delta_prep(z, conv_w, a_log, dt_bias):
    tt = DL_PREP_TT
    hb = tt // DL_HALO
    n_hb = N_TOK // DL_HALO
    pad8 = lambda v: jnp.concatenate([v.reshape(1, 2 * D_HEADS), jnp.zeros((1, 128 - 2 * D_HEADS), F32)], axis=1)
    return pl.pallas_call(
        _delta_prep_kernel,
        grid=(N_TOK // tt,),
        in_specs=[pl.BlockSpec((tt, 768), lambda i: (i, 3)),
                  pl.BlockSpec((DL_HALO, 768), lambda i: (jnp.maximum(i * hb - 1, 0), 3)),
                  pl.BlockSpec((DL_HALO, 768), lambda i: (jnp.minimum((i + 1) * hb, n_hb - 1), 3)),
                  pl.BlockSpec((tt, 128), lambda i: (i, Z_DAB // 128)),
                  pl.BlockSpec((CONV_K, 768), lambda i: (0, 0)),
                  pl.BlockSpec((1, 128), lambda i: (0, 0)),
                  pl.BlockSpec((1, 128), lambda i: (0, 0))],
        out_specs=[pl.BlockSpec((N_QKV_HEADS, tt, HEAD_DIM), lambda i: (0, i, 0)),
                   pl.BlockSpec((tt, 128), lambda i: (i, 0))],
        out_shape=[jax.ShapeDtypeStruct((N_QKV_HEADS, N_TOK, HEAD_DIM), F32),
                   jax.ShapeDtypeStruct((N_TOK, 128), F32)],
        scratch_shapes=[pltpu.VMEM((tt + 2 * DL_HALO, 768), F32)],
        compiler_params=_params(("arbitrary",)),
        name="delta_prep",
    )(z, z, z, z, conv_w, pad8(-jnp.exp(a_log)), pad8(dt_bias))


def _split_bf16(a):
    hi = a.astype(BF16)
    return hi, (a - hi.astype(F32)).astype(BF16)


def _dot_hl(a_parts, b_parts):
    (a_hi, a_lo), (b_hi, b_lo) = a_parts, b_parts
    m = a_hi.shape[0]
    r = jnp.dot(jnp.concatenate([a_hi, a_lo], axis=0), b_hi, preferred_element_type=F32)
    return r[:m] + r[m:] + jnp.dot(a_hi, b_lo, preferred_element_type=F32)


def _delta_chunk_kernel(qkv_ref, gate_ref, u2_ref, wq_ref, ak_ref):
    C = DL_C
    n_c = gate_ref.shape[0] // C
    row = lax.broadcasted_iota(jnp.int32, (C, C), 0)
    col = lax.broadcasted_iota(jnp.int32, (C, C), 1)
    eye = (row == col).astype(F32)
    t_idx = lax.broadcasted_iota(jnp.int32, (C, 128), 0)

    def chunk_chains(c):
        r0 = pl.multiple_of(c * C, C)
        ga = gate_ref[pl.ds(r0, C), :]
        chains = []
        for d in range(2):
            incl = (row >= col) if d == 0 else (row <= col)
            strict = (row > col) if d == 0 else (row < col)
            g_all = ga
            step = 1
            while step < C:
                if d == 0:
                    g_all = g_all + jnp.where(t_idx >= step, pltpu.roll(g_all, step, 0), 0.0)
                else:
                    g_all = g_all + jnp.where(t_idx < C - step, pltpu.roll(g_all, C - step, 0), 0.0)
                step *= 2
            g_all_t = g_all.T
            for h in range(D_HEADS):
                ci = d * D_HEADS + h
                q = qkv_ref[h, pl.ds(r0, C), :]
                k = qkv_ref[D_HEADS + h, pl.ds(r0, C), :]
                v = qkv_ref[2 * D_HEADS + h, pl.ds(r0, C), :]
                g_col = g_all[:, ci:ci + 1]
                g_row = g_all_t[ci:ci + 1, :]
                beta = ga[:, 2 * D_HEADS + ci:2 * D_HEADS + ci + 1]
                g_end = g_col[C - 1:C, :] if d == 0 else g_col[0:1, :]
                kb = k * beta
                eg = jnp.exp(g_col)
                decay = jnp.where(incl, jnp.exp(jnp.where(incl, g_col - g_row, 0.0)), 0.0)
                kq = jnp.concatenate([kb, q], axis=0).astype(BF16)
                chains.append(dict(
                    strict=strict, decay=decay, qg=q * eg, g_end=g_end,
                    r=lax.dot_general(kq, k.astype(BF16), _NT, preferred_element_type=F32),
                    rhs=jnp.concatenate([v * beta, kb * eg], axis=1).astype(BF16),
                    ke_t=(k * jnp.exp(g_end - g_col)).T))
        return r0, chains

    def body(gi, carry):
        groups = [(gi * DL_GROUP + cc,) + chunk_chains(gi * DL_GROUP + cc) for cc in range(DL_GROUP)]
        chains = [ch for _, _, chs in groups for ch in chs]
        for ch in chains:
            ch['p'] = -jnp.where(ch['strict'], ch['r'][:C] * ch['decay'], 0.0)
            ch['t'] = eye + ch['p']
        for _ in range(5):
            for ch in chains:
                parts = _split_bf16(ch['p'])
                ch['p'] = _dot_hl(parts, parts)
            for ch in chains:
                ch['t'] = ch['t'] + _dot_hl(_split_bf16(ch['t']), _split_bf16(ch['p']))
        for ch in chains:
            ch['uw'] = jnp.dot(ch['t'].astype(BF16), ch['rhs'], preferred_element_type=F32)
        pack = lambda xs: jnp.stack(xs).reshape((2, D_HEADS) + xs[0].shape)
        for c, r0, chs in groups:
            u2 = [jnp.concatenate([ch['uw'][:, :C], jnp.broadcast_to(jnp.exp(ch['g_end']), (C, C))], axis=1)
                  for ch in chs]
            wq = [jnp.concatenate([ch['uw'][:, C:], ch['qg']], axis=0).astype(BF16) for ch in chs]
            ak = [jnp.concatenate([ch['r'][C:] * ch['decay'], ch['ke_t']], axis=0).astype(BF16) for ch in chs]
            u2_ref[:, :, pl.ds(r0, C), :] = pack(u2)
            wq_ref[:, :, c] = pack(wq)
            ak_ref[:, :, c] = pack(ak)
        return carry

    lax.fori_loop(0, n_c // DL_GROUP, body, 0)


def delta_chunks(qkv, gates):
    tt = DL_CHUNK_TT
    n_c = tt // DL_C
    return pl.pallas_call(
        _delta_chunk_kernel,
        grid=(N_TOK // tt,),
        in_specs=[pl.BlockSpec((N_QKV_HEADS, tt, HEAD_DIM), lambda i: (0, i, 0)),
                  pl.BlockSpec((tt, 128), lambda i: (i, 0))],
        out_specs=[pl.BlockSpec((2, D_HEADS, tt, 128), lambda i: (0, 0, i, 0)),
                   pl.BlockSpec((2, D_HEADS, n_c, 2 * DL_C, HEAD_DIM), lambda i: (0, 0, i, 0, 0)),
                   pl.BlockSpec((2, D_HEADS, n_c, 2 * DL_C, HEAD_DIM), lambda i: (0, 0, i, 0, 0))],
        out_shape=[jax.ShapeDtypeStruct((2, D_HEADS, N_TOK, 128), F32),
                   jax.ShapeDtypeStruct((2, D_HEADS, N_TOK // DL_C, 2 * DL_C, HEAD_DIM), BF16),
                   jax.ShapeDtypeStruct((2, D_HEADS, N_TOK // DL_C, 2 * DL_C, HEAD_DIM), BF16)],
        compiler_params=_params(("arbitrary",)),
        name="delta_chunks",
    )(qkv, gates)


def _delta_scan_kernel(u2f_ref, wqf_ref, akf_ref, u2b_ref, wqb_ref, akb_ref, s0_ref,
                       of_ref, ob_ref, s_ref, s_scr):
    j = pl.program_id(1)
    C = DL_C
    n_c = wqf_ref.shape[1]

    @pl.when(j == 0)
    def _():
        s_scr[...] = s0_ref[...]

    def body(ci, carry):
        chains = []
        for d, (u2_ref, wq_ref, ak_ref, o_ref) in enumerate(((u2f_ref, wqf_ref, akf_ref, of_ref),
                                                            (u2b_ref, wqb_ref, akb_ref, ob_ref))):
            c = ci if d == 0 else n_c - 1 - ci
            r0 = pl.multiple_of(c * C, C)
            for h in range(D_HEADS):
                s = s_scr[d, h]
                chains.append(dict(d=d, h=h, c=c, r0=r0, s=s, ak_ref=ak_ref, o_ref=o_ref,
                                   u2=u2_ref[h, pl.ds(r0, C), :],
                                   r1=jnp.dot(wq_ref[h, c], s.astype(BF16),
                                              preferred_element_type=F32)))
        for ch in chains:
            v_new = ch['u2'][:, :C] - ch['r1'][:C]
            ch['r2'] = jnp.dot(ch['ak_ref'][ch['h'], ch['c']], v_new.astype(BF16),
                               preferred_element_type=F32)
        for ch in chains:
            ch['o_ref'][ch['h'], pl.ds(ch['r0'], C), :] = ch['r1'][C:] + ch['r2'][:C]
            s_scr[ch['d'], ch['h']] = ch['s'] * ch['u2'][0:1, C:] + ch['r2'][C:]
        return carry

    lax.fori_loop(0, n_c, body, 0)

    @pl.when(j == pl.num_programs(1) - 1)
    def _():
        s_ref[...] = s_scr[...]


def delta_scan(u2, wq, ak, row0, nb, L, s0):
    tt = min(L, 512)
    n_t = L // tt
    n_c = tt // DL_C
    rb = row0 // tt
    fwd = lambda b, j: rb + b * n_t + j
    bwd = lambda b, j: rb + b * n_t + (n_t - 1 - j)
    u_spec = lambda d, f: pl.BlockSpec((None, D_HEADS, tt, 128), lambda b, j: (d, 0, f(b, j), 0))
    c_spec = lambda d, f: pl.BlockSpec((None, D_HEADS, n_c, 2 * DL_C, HEAD_DIM), lambda b, j: (d, 0, f(b, j), 0, 0))
    st_spec = pl.BlockSpec((None, 2, D_HEADS, HEAD_DIM, HEAD_DIM), lambda b, j: (b, 0, 0, 0, 0))
    return pl.pallas_call(
        _delta_scan_kernel,
        grid=(nb, n_t),
        in_specs=[u_spec(0, fwd), c_spec(0, fwd), c_spec(0, fwd),
                  u_spec(1, bwd), c_spec(1, bwd), c_spec(1, bwd), st_spec],
        out_specs=[pl.BlockSpec((D_HEADS, tt, HEAD_DIM), lambda b, j: (0, b * n_t + j, 0)),
                   pl.BlockSpec((D_HEADS, tt, HEAD_DIM), lambda b, j: (0, b * n_t + (n_t - 1 - j), 0)),
                   st_spec],
        out_shape=[jax.ShapeDtypeStruct((D_HEADS, nb * L, HEAD_DIM), F32),
                   jax.ShapeDtypeStruct((D_HEADS, nb * L, HEAD_DIM), F32),
                   jax.ShapeDtypeStruct((nb, 2, D_HEADS, HEAD_DIM, HEAD_DIM), F32)],
        scratch_shapes=[pltpu.VMEM((2, D_HEADS, HEAD_DIM, HEAD_DIM), F32)],
        compiler_params=_params(("arbitrary", "arbitrary")),
        name="delta_scan",
    )(u2, wq, ak, u2, wq, ak, s0)


def _delta_out_kernel(of_ref, ob_ref, zg_ref, gn_ref, o_ref):
    outs = []
    for h in range(D_HEADS):
        o = of_ref[h] + ob_ref[h]
        outs.append(o * lax.rsqrt(jnp.mean(o * o, axis=-1, keepdims=True) + EPS) * gn_ref[...])
    g = zg_ref[...]
    o_ref[...] = (jnp.concatenate(outs, axis=-1) * (g * jax.nn.sigmoid(g))).astype(o_ref.dtype)


def delta_output(o_f, o_b, z, row0, gn):
    n = o_f.shape[1]
    tt = 256
    return pl.pallas_call(
        _delta_out_kernel,
        grid=(n // tt,),
        in_specs=[pl.BlockSpec((D_HEADS, tt, HEAD_DIM), lambda i: (0, i, 0)),
                  pl.BlockSpec((D_HEADS, tt, HEAD_DIM), lambda i: (0, i, 0)),
                  pl.BlockSpec((tt, 256), lambda i: (row0 // tt + i, Z_DG // 256)),
                  pl.BlockSpec((1, HEAD_DIM), lambda i: (0, 0))],
        out_specs=pl.BlockSpec((tt, 256), lambda i: (i, 0)),
        out_shape=jax.ShapeDtypeStruct((n, 256), BF16),
        compiler_params=_params(("arbitrary",)),
        name="delta_output",
    )(o_f, o_b, z, gn.reshape(1, HEAD_DIM))


def kernel(x_prompt, x_sample, cache_attn_a_k, cache_attn_a_v, cache_attn_b_k, cache_attn_b_v,
           state_hgrn, state_delta, c, c_ctx, norm1_g, norm2_g, w_ada, b_ada, w_in, a_sink,
           b_qnorm_g, b_knorm_g, c_lb, c_onorm_g, d_conv, d_a_log, d_dt_bias, d_onorm_g,
           w_branch, w_out, ffn_w1, ffn_w3, ffn_w2, router_w, router_b, moe_w1, moe_w3, moe_w2,
           final_norm_g):
    cum = jnp.cumsum(jax.nn.softmax(c_lb, axis=0), axis=0)
    lower_bounds = cum - cum[:1]

    x = jnp.concatenate([x_prompt.reshape(CTX_TOK, D_MODEL), x_sample.reshape(LAT_TOK, D_MODEL)], axis=0)
    cond = jnp.concatenate([c_ctx[None, :], c, jnp.zeros((16 - N_COND, D_MODEL), F32)], axis=0)

    rope_c, rope_s = rope_lane_tables(DEC_SEQ)
    caches = []
    for l in range(DEPTH):
        mod = ada_modulation(cond, w_ada[l], b_ada[l])[:N_COND].reshape(N_COND, 6, D_MODEL)
        w_mix = jnp.concatenate([w_in[l][:, :Z_MAIN], w_in[l][:, Z_MAIN + 16:W_IN_MIX],
                                 w_in[l][:, Z_MAIN:Z_MAIN + 16], jnp.zeros((D_MODEL, 128 - 16), F32)],
                                axis=1).astype(BF16)
        w_gl = w_in[l][:, W_IN_MIX:].reshape(D_MODEL, N_BRANCH, D_MODEL).transpose(1, 0, 2).astype(BF16)
        z = input_projection(x, mod, norm1_g[l], w_mix)
        kv2 = lambda t: t.reshape(DEC_BATCH, PAST_LEN, 128)
        o_ab_ctx, bk_ctx = ctx_attention(z, a_sink[l], b_qnorm_g[l], b_knorm_g[l])
        o_a_lat = latent_attention_a(z, CTX_TOK, DEC_BATCH, DEC_SEQ, a_sink[l], kv2(cache_attn_a_k[:, l]),
                                     kv2(cache_attn_a_v[:, l]), rope_c, rope_s)
        o_b_lat = latent_attention_b(z, CTX_TOK, DEC_BATCH, DEC_SEQ, kv2(cache_attn_b_k[:, l]),
                                     kv2(cache_attn_b_v[:, l]), rope_c, rope_s, b_qnorm_g[l], b_knorm_g[l])
        o_c_ctx, sc_t = hgrn_mixer(z, 0, BATCH, SEQ, lower_bounds[l], c_onorm_g[l],
                                   jnp.zeros((BATCH, 2, C_HEADS, HEAD_DIM, HEAD_DIM), F32))
        o_c_lat, _ = hgrn_mixer(z, CTX_TOK, DEC_BATCH, DEC_SEQ, lower_bounds[l], c_onorm_g[l],
                                jnp.swapaxes(state_hgrn[:, l], -1, -2))

        qkv, gates = delta_prep(z, d_conv[l], d_a_log[l], d_dt_bias[l])
        u2, wq, ak = delta_chunks(qkv, gates)
        of_ctx, ob_ctx, sd = delta_scan(u2, wq, ak, 0, BATCH, SEQ,
                                        jnp.zeros((BATCH, 2, D_HEADS, HEAD_DIM, HEAD_DIM), F32))
        of_lat, ob_lat, _ = delta_scan(u2, wq, ak, CTX_TOK, DEC_BATCH, DEC_SEQ, state_delta[:, l])
        o_d_ctx = delta_output(of_ctx, ob_ctx, z, 0, d_onorm_g[l])
        o_d_lat = delta_output(of_lat, ob_lat, z, CTX_TOK, d_onorm_g[l])
        kvh = lambda t: t.reshape(BATCH, SEQ, 2, HEAD_DIM)
        caches.append((kvh(z[:CTX_TOK, 256:384]), kvh(z[:CTX_TOK, 384:512]), kvh(bk_ctx), kvh(z[:CTX_TOK, 896:1024]),
                       jnp.swapaxes(sc_t, -1, -2), sd))
        x = merge_projection(x, mod, norm1_g[l], (o_ab_ctx, o_c_ctx, o_d_ctx), (o_a_lat, o_b_lat, o_c_lat, o_d_lat),
                             w_gl, w_branch[l].astype(BF16), w_out[l].astype(BF16))
        j = l // 2
        if l % 2 == 0:
            x = dense_ffn(x, mod, norm2_g[l], ffn_w1[j].astype(BF16), ffn_w3[j].astype(BF16),
                          ffn_w2[j].astype(BF16))
        else:
            assert l == DEPTH - 1, "the expert layer's residual is fused with the final norm"
            rw = jnp.concatenate([router_w[j], jnp.zeros((D_MODEL, 128 - N_EXPERTS), F32)], axis=1)
            rb = jnp.concatenate([router_b[j], jnp.zeros((128 - N_EXPERTS,), F32)])[None, :]
            h2, pos, wgt = moe_router(x, mod, norm2_g[l], rw, rb)
            f = moe_experts(h2, pos, wgt, moe_w1[j].astype(BF16), moe_w3[j].astype(BF16), moe_w2[j].astype(BF16))
            y_prompt = residual_final_norm(x, f, mod, final_norm_g, 0, CTX_TOK).reshape(BATCH, SEQ, D_MODEL)
            y_sample = residual_final_norm(x, f, mod, final_norm_g, CTX_TOK, LAT_TOK).reshape(DEC_BATCH, DEC_SEQ, D_MODEL)

    stack = lambda idx: jnp.stack([caches[l][idx] for l in range(DEPTH)], axis=1)
    return (y_prompt, y_sample, stack(0), stack(1), stack(2), stack(3), stack(4), stack(5))
```

```python
import functools

import jax
import jax.numpy as jnp
import numpy as np
from jax import lax
from jax.experimental import pallas as pl
from jax.experimental.pallas import tpu as pltpu

F32 = jnp.float32
BF16 = jnp.bfloat16

D_MODEL = 1024
BATCH = 32
SEQ = 256
DEPTH = 2
DEC_BATCH = 8
DEC_SEQ = 4096
PAST_LEN = 256
GRID_W = 64
HEAD_DIM = 64
A_HEADS = 4
A_KV_HEADS = 2
B_HEADS = 4
B_KV_HEADS = 2
C_HEADS = 4
D_HEADS = 4
BRANCH_W = 256
N_BRANCH = 4
WINDOW = 128
BLOCK = 128
ROPE_BASE = 10000.0
HGRN_CHUNK = 32
DELTA_CHUNK = 64
CONV_K = 5
D_FF = 2816
N_EXPERTS = 8
D_FF_EXPERT = 3584
EPS = 1e-6
NEG_INF = -1e30
F32_MIN = float(np.finfo(np.float32).min)

CTX_TOK = BATCH * SEQ
LAT_TOK = DEC_BATCH * DEC_SEQ
N_TOK = CTX_TOK + LAT_TOK
N_COND = 1 + DEC_BATCH

Z_MAIN = 3072
Z_DG = Z_MAIN
Z_DAB = Z_DG + BRANCH_W
Z_COLS = Z_DAB + 128
W_IN_MIX = 3344

TM = 512
VMEM_LIMIT = 56 * 1024 * 1024


def _tile_cond(i, tm):
    ctx_tiles = CTX_TOK // tm
    per_b = DEC_SEQ // tm
    return jnp.where(i < ctx_tiles, 0, 1 + (i - ctx_tiles) // per_b)


def _rms(x, g):
    return x * lax.rsqrt(jnp.mean(x * x, axis=-1, keepdims=True) + EPS) * g


def _params(sem):
    return pltpu.CompilerParams(dimension_semantics=sem, vmem_limit_bytes=VMEM_LIMIT)


def _ada_kernel(c_ref, w_ref, b_ref, o_ref):
    c = c_ref[...]
    s = c * jax.nn.sigmoid(c)
    o_ref[...] = jnp.dot(s.astype(BF16), w_ref[...].astype(BF16), preferred_element_type=F32) + b_ref[...]


def ada_modulation(cond_pad, w, b):
    n = 6 * D_MODEL
    tn = 1536
    return pl.pallas_call(
        _ada_kernel,
        grid=(n // tn,),
        in_specs=[pl.BlockSpec((16, D_MODEL), lambda j: (0, 0)),
                  pl.BlockSpec((D_MODEL, tn), lambda j: (0, j)),
                  pl.BlockSpec((1, tn), lambda j: (0, j))],
        out_specs=pl.BlockSpec((16, tn), lambda j: (0, j)),
        out_shape=jax.ShapeDtypeStruct((16, n), F32),
        compiler_params=_params(("arbitrary",)),
        name="ada_modulation",
    )(cond_pad, w, b.reshape(1, n))


def _in_kernel(x_ref, mod_ref, g_ref, w_ref, z_ref):
    h = _rms(x_ref[...], g_ref[...]) * (1.0 + mod_ref[1:2, :]) + mod_ref[0:1, :]
    z_ref[...] = jnp.dot(h.astype(BF16), w_ref[...], preferred_element_type=F32)


def input_projection(x, mod, g, w):
    nt = N_TOK // TM
    return pl.pallas_call(
        _in_kernel,
        grid=(nt,),
        in_specs=[pl.BlockSpec((TM, D_MODEL), lambda i: (i, 0)),
                  pl.BlockSpec((None, 6, D_MODEL), lambda i: (_tile_cond(i, TM), 0, 0)),
                  pl.BlockSpec((1, D_MODEL), lambda i: (0, 0)),
                  pl.BlockSpec((D_MODEL, Z_COLS), lambda i: (0, 0))],
        out_specs=pl.BlockSpec((TM, Z_COLS), lambda i: (i, 0)),
        out_shape=jax.ShapeDtypeStruct((N_TOK, Z_COLS), F32),
        compiler_params=_params(("arbitrary",)),
        name="input_projection",
    )(x, mod, g.reshape(1, D_MODEL), w)


def _merge_kernel(x_ref, mod_ref, g_ref, ab_c_ref, c_c_ref, d_c_ref, a_l_ref, b_l_ref, c_l_ref, d_l_ref,
                  wgl_ref, wbr_ref, wout_ref, xo_ref):
    x = x_ref[...]
    h = (_rms(x, g_ref[...]) * (1.0 + mod_ref[1:2, :]) + mod_ref[0:1, :]).astype(BF16)
    is_ctx = pl.program_id(0) < CTX_TOK // TM
    branches = (jnp.where(is_ctx, ab_c_ref[:, :BRANCH_W], a_l_ref[...]),
                jnp.where(is_ctx, ab_c_ref[:, BRANCH_W:], b_l_ref[...]),
                jnp.where(is_ctx, c_c_ref[...], c_l_ref[...]),
                jnp.where(is_ctx, d_c_ref[...], d_l_ref[...]))
    merged = None
    for j in range(N_BRANCH):
        gate = jax.nn.sigmoid(jnp.dot(h, wgl_ref[j], preferred_element_type=F32))
        br = jnp.dot(branches[j], wbr_ref[j], preferred_element_type=F32)
        merged = gate * br if merged is None else merged + gate * br
    mix = jnp.dot(merged.astype(BF16), wout_ref[...], preferred_element_type=F32)
    xo_ref[...] = x + mod_ref[2:3, :] * mix


def merge_projection(x, mod, g, o_ctx, o_lat, wgl, wbr, wout):
    nt = N_TOK // TM
    ctx_tiles = CTX_TOK // TM
    ctx_spec = lambda w: pl.BlockSpec((TM, w), lambda i: (jnp.minimum(i, ctx_tiles - 1), 0))
    lat_spec = pl.BlockSpec((TM, BRANCH_W), lambda i: (jnp.maximum(i - ctx_tiles, 0), 0))
    return pl.pallas_call(
        _merge_kernel,
        grid=(nt,),
        in_specs=[pl.BlockSpec((TM, D_MODEL), lambda i: (i, 0)),
                  pl.BlockSpec((None, 6, D_MODEL), lambda i: (_tile_cond(i, TM), 0, 0)),
                  pl.BlockSpec((1, D_MODEL), lambda i: (0, 0)),
                  ctx_spec(2 * BRANCH_W), ctx_spec(BRANCH_W), ctx_spec(BRANCH_W),
                  lat_spec, lat_spec, lat_spec, lat_spec,
                  pl.BlockSpec((N_BRANCH, D_MODEL, D_MODEL), lambda i: (0, 0, 0)),
                  pl.BlockSpec((N_BRANCH, BRANCH_W, D_MODEL), lambda i: (0, 0, 0)),
                  pl.BlockSpec((D_MODEL, D_MODEL), lambda i: (0, 0))],
        out_specs=pl.BlockSpec((TM, D_MODEL), lambda i: (i, 0)),
        out_shape=jax.ShapeDtypeStruct((N_TOK, D_MODEL), F32),
        compiler_params=_params(("arbitrary",)),
        name="merge_projection",
    )(x, mod, g.reshape(1, D_MODEL), *o_ctx, *o_lat, wgl, wbr, wout)


def _ffn_kernel(x_ref, mod_ref, g_ref, w1_ref, w3_ref, w2_ref, xo_ref):
    x = x_ref[...]
    h = (_rms(x, g_ref[...]) * (1.0 + mod_ref[4:5, :]) + mod_ref[3:4, :]).astype(BF16)
    a = jnp.dot(h, w1_ref[...], preferred_element_type=F32)
    b = jnp.dot(h, w3_ref[...], preferred_element_type=F32)
    hid = (a * jax.nn.sigmoid(a) * b).astype(BF16)
    f = jnp.dot(hid, w2_ref[...], preferred_element_type=F32)
    xo_ref[...] = x + mod_ref[5:6, :] * f


def dense_ffn(x, mod, g, w1, w3, w2):
    nt = N_TOK // TM
    const = lambda i: (0, 0)
    return pl.pallas_call(
        _ffn_kernel,
        grid=(nt,),
        in_specs=[pl.BlockSpec((TM, D_MODEL), lambda i: (i, 0)),
                  pl.BlockSpec((None, 6, D_MODEL), lambda i: (_tile_cond(i, TM), 0, 0)),
                  pl.BlockSpec((1, D_MODEL), const),
                  pl.BlockSpec((D_MODEL, D_FF), const, pipeline_mode=pl.Buffered(1)),
                  pl.BlockSpec((D_MODEL, D_FF), const, pipeline_mode=pl.Buffered(1)),
                  pl.BlockSpec((D_FF, D_MODEL), const, pipeline_mode=pl.Buffered(1))],
        out_specs=pl.BlockSpec((TM, D_MODEL), lambda i: (i, 0)),
        out_shape=jax.ShapeDtypeStruct((N_TOK, D_MODEL), F32),
        compiler_params=_params(("arbitrary",)),
        name="dense_ffn",
    )(x, mod, g.reshape(1, D_MODEL), w1, w3, w2)


MOE_T = 1024
MOE_R = 144
MOE_SR = 128
MOE_F = 1792
MOE_CAP = -(-MOE_T // MOE_R) * MOE_R


def _router_kernel(x_ref, mod_ref, g_ref, rw_ref, rb_ref, h_ref, pos_ref, wgt_ref):
    t = x_ref.shape[0]
    h = _rms(x_ref[...], g_ref[...]) * (1.0 + mod_ref[4:5, :]) + mod_ref[3:4, :]
    h_ref[...] = h.astype(BF16)
    logits = jnp.dot(h, rw_ref[...], preferred_element_type=F32, precision=lax.Precision.HIGHEST) + rb_ref[...]
    lt = logits.T[:N_EXPERTS, :]
    eidx = lax.broadcasted_iota(jnp.int32, lt.shape, 0)
    m1 = jnp.max(lt, axis=0, keepdims=True)
    i1 = jnp.min(jnp.where(lt == m1, eidx, N_EXPERTS), axis=0, keepdims=True)
    rest = jnp.where(eidx == i1, F32_MIN, lt)
    m2 = jnp.max(rest, axis=0, keepdims=True)
    i2 = jnp.min(jnp.where(rest == m2, eidx, N_EXPERTS), axis=0, keepdims=True)
    e2 = jnp.exp(m2 - m1)
    p1 = 1.0 / (1.0 + e2)
    p2 = e2 / (1.0 + e2)
    wgt_ref[...] = jnp.where(eidx == i1, p1, 0.0) + jnp.where(eidx == i2, p2, 0.0)
    routed = jnp.where(eidx == i1, 1.0, jnp.where(eidx == i2, 1.0, 0.0))
    s_id = lax.broadcasted_iota(jnp.int32, (t, t), 0)
    t_id = lax.broadcasted_iota(jnp.int32, (t, t), 1)
    before = jnp.where(s_id < t_id, 1.0, 0.0).astype(BF16)
    rank = jnp.dot(routed.astype(BF16), before, preferred_element_type=F32)
    pos_ref[...] = jnp.where(routed > 0.0, rank.astype(jnp.int32), -1)


def moe_router(x, mod, g, rw, rb):
    nt = N_TOK // MOE_T
    return pl.pallas_call(
        _router_kernel,
        grid=(nt,),
        in_specs=[pl.BlockSpec((MOE_T, D_MODEL), lambda i: (i, 0)),
                  pl.BlockSpec((None, 6, D_MODEL), lambda i: (_tile_cond(i, MOE_T), 0, 0)),
                  pl.BlockSpec((1, D_MODEL), lambda i: (0, 0)),
                  pl.BlockSpec((D_MODEL, 128), lambda i: (0, 0)),
                  pl.BlockSpec((1, 128), lambda i: (0, 0))],
        out_specs=[pl.BlockSpec((MOE_T, D_MODEL), lambda i: (i, 0)),
                   pl.BlockSpec((N_EXPERTS, MOE_T), lambda i: (0, i)),
                   pl.BlockSpec((N_EXPERTS, MOE_T), lambda i: (0, i))],
        out_shape=[jax.ShapeDtypeStruct((N_TOK, D_MODEL), BF16),
                   jax.ShapeDtypeStruct((N_EXPERTS, N_TOK), jnp.int32),
                   jax.ShapeDtypeStruct((N_EXPERTS, N_TOK), F32)],
        compiler_params=_params(("arbitrary",)),
        name="moe_router",
    )(x, mod, g.reshape(1, D_MODEL), rw, rb)


def _moe_sparse_kernel(h_ref, pos_ref, wgt_ref, w1_ref, w3_ref, w2_ref, y_ref, xg_scr, acc_scr, wr_scr):
    e = pl.program_id(1)
    f = pl.program_id(2)
    t = h_ref.shape[0]
    pos_e = pos_ref[pl.ds(e, 1), :]
    n_rows = jnp.max(pos_e) + 1
    n_blocks = (n_rows + MOE_R - 1) // MOE_R
    n_sc_blocks = (n_rows + MOE_SR - 1) // MOE_SR
    n_init_blocks = jnp.maximum(n_blocks, (n_sc_blocks * MOE_SR + MOE_R - 1) // MOE_R)
    row_id = lax.broadcasted_iota(jnp.int32, (MOE_R, t), 0)

    def block_rows(r):
        return pl.ds(pl.multiple_of(r * MOE_R, 16), MOE_R)

    def selects(r):
        return pos_e == row_id + r * MOE_R

    @pl.when((e == 0) & (f == 0))
    def _():
        y_ref[...] = jnp.zeros_like(y_ref)

    @pl.when(f == 0)
    def _():
        wgt_e = wgt_ref[pl.ds(e, 1), :]

        def gather(r, carry):
            sel = selects(r)
            xg = jnp.dot(jnp.where(sel, 1.0, 0.0).astype(BF16), h_ref[...], preferred_element_type=F32)
            xg_scr[block_rows(r), :] = xg.astype(BF16)
            w_rows = jnp.sum(jnp.where(sel, wgt_e, 0.0), axis=1, keepdims=True)
            wr_scr[block_rows(r), :] = jnp.broadcast_to(w_rows, (MOE_R, 128))
            acc_scr[block_rows(r), :] = jnp.zeros((MOE_R, D_MODEL), F32)
            return carry

        lax.fori_loop(0, n_init_blocks, gather, 0)

    def expert(r, carry):
        xg = xg_scr[block_rows(r), :]
        a = jnp.dot(xg, w1_ref[...], preferred_element_type=F32)
        b = jnp.dot(xg, w3_ref[...], preferred_element_type=F32)
        hid = (a * jax.nn.sigmoid(a) * b * wr_scr[block_rows(r), 0:1]).astype(BF16)
        acc_scr[block_rows(r), :] += jnp.dot(hid, w2_ref[...], preferred_element_type=F32)
        return carry

    lax.fori_loop(0, n_blocks, expert, 0)

    @pl.when(f == pl.num_programs(2) - 1)
    def _():
        sc_row_id = lax.broadcasted_iota(jnp.int32, (MOE_SR, t), 0)

        def scatter(r, carry):
            rows = pl.ds(pl.multiple_of(r * MOE_SR, MOE_SR), MOE_SR)
            onehot = jnp.where(pos_e == sc_row_id + r * MOE_SR, 1.0, 0.0).astype(BF16)
            hi, lo = _split_bf16(acc_scr[rows, :])
            y_ref[...] += lax.dot_general(jnp.concatenate([onehot, onehot], axis=0),
                                          jnp.concatenate([hi, lo], axis=0), _TN, preferred_element_type=F32)
            return carry

        lax.fori_loop(0, n_sc_blocks, scatter, 0)


def moe_experts(h2, pos, wgt, w1, w3, w2):
    nt = N_TOK // MOE_T
    nf = D_FF_EXPERT // MOE_F
    return pl.pallas_call(
        _moe_sparse_kernel,
        grid=(nt, N_EXPERTS, nf),
        in_specs=[pl.BlockSpec((MOE_T, D_MODEL), lambda i, e, f: (i, 0)),
                  pl.BlockSpec((N_EXPERTS, MOE_T), lambda i, e, f: (0, i)),
                  pl.BlockSpec((N_EXPERTS, MOE_T), lambda i, e, f: (0, i)),
                  pl.BlockSpec((None, None, D_MODEL, MOE_F), lambda i, e, f: (e, f, 0, 0)),
                  pl.BlockSpec((None, None, D_MODEL, MOE_F), lambda i, e, f: (e, f, 0, 0)),
                  pl.BlockSpec((None, MOE_F, D_MODEL), lambda i, e, f: (e, f, 0))],
        out_specs=pl.BlockSpec((MOE_T, D_MODEL), lambda i, e, f: (i, 0)),
        out_shape=jax.ShapeDtypeStruct((N_TOK, D_MODEL), F32),
        scratch_shapes=[pltpu.VMEM((MOE_CAP, D_MODEL), BF16),
                        pltpu.VMEM((MOE_CAP, D_MODEL), F32),
                        pltpu.VMEM((MOE_CAP, 128), F32)],
        compiler_params=_params(("arbitrary", "arbitrary", "arbitrary")),
        name="moe_experts",
    )(h2, pos, wgt, w1, w3, w2)


def _residual_norm_kernel(x_ref, y_ref, mod_ref, g_ref, o_ref):
    o_ref[...] = _rms(x_ref[...] + mod_ref[5:6, :] * y_ref[...], g_ref[...])


def residual_final_norm(x, y, mod, g, row0, n_rows):
    tm = 1024
    t0 = row0 // tm
    return pl.pallas_call(
        _residual_norm_kernel,
        grid=(n_rows // tm,),
        in_specs=[pl.BlockSpec((tm, D_MODEL), lambda i: (t0 + i, 0)),
                  pl.BlockSpec((tm, D_MODEL), lambda i: (t0 + i, 0)),
                  pl.BlockSpec((None, 6, D_MODEL), lambda i: (_tile_cond(t0 + i, tm), 0, 0)),
                  pl.BlockSpec((1, D_MODEL), lambda i: (0, 0))],
        out_specs=pl.BlockSpec((tm, D_MODEL), lambda i: (i, 0)),
        out_shape=jax.ShapeDtypeStruct((n_rows, D_MODEL), F32),
        compiler_params=_params(("arbitrary",)),
        name="residual_final_norm",
    )(x, y, mod, g.reshape(1, D_MODEL))


ATT_SCALE = HEAD_DIM ** -0.5
LOG2_E = 1.4426950408889634
_NT = (((1,), (1,)), ((), ()))


def _head_rms(x, g_row):
    outs = []
    for h in range(x.shape[1] // HEAD_DIM):
        xh = x[:, h * HEAD_DIM:(h + 1) * HEAD_DIM]
        outs.append(xh * lax.rsqrt(jnp.mean(xh * xh, axis=-1, keepdims=True) + EPS) * g_row)
    return jnp.concatenate(outs, axis=-1)


def _rope_apply(x, c, s):
    w = x.shape[-1]
    lane = lax.broadcasted_iota(jnp.int32, x.shape, 1)
    first_half = ((lane // (HEAD_DIM // 4)) % 2) == 0
    partner = jnp.where(first_half, pltpu.roll(x, w - HEAD_DIM // 4, 1), pltpu.roll(x, HEAD_DIM // 4, 1))
    return x * c + partner * s


def rope_lane_tables(L):
    rows = L // GRID_W
    row = jnp.repeat(jnp.arange(rows, dtype=F32), GRID_W)
    col = jnp.tile(jnp.arange(GRID_W, dtype=F32), rows)
    n_freq = HEAD_DIM // 4
    inv = ROPE_BASE ** (-jnp.arange(n_freq, dtype=F32) / n_freq)
    ang = jnp.stack([row, col], 0)[:, :, None] * inv
    cos, sin = jnp.cos(ang), jnp.sin(ang)
    c = jnp.concatenate([cos[0], cos[0], cos[1], cos[1]], axis=-1)
    s = jnp.concatenate([-sin[0], sin[0], -sin[1], sin[1]], axis=-1)
    return jnp.tile(c, (1, 4)), jnp.tile(s, (1, 4))


def _with_ones(v):
    ones = jnp.ones((v.shape[0], HEAD_DIM), BF16)
    parts = []
    for h in range(v.shape[1] // HEAD_DIM):
        parts += [v[:, h * HEAD_DIM:(h + 1) * HEAD_DIM].astype(BF16), ones]
    return jnp.concatenate(parts, axis=-1)


def _attend_heads(jobs):
    for job in jobs:
        q = (job['q'] * (ATT_SCALE * LOG2_E)).astype(BF16)
        job['s'] = lax.dot_general(q, job['k'], _NT, preferred_element_type=F32)
        if job.get('extra') is not None:
            job['s2'] = lax.dot_general(q, job['extra'][0], _NT, preferred_element_type=F32)
    outs = []
    for job in jobs:
        s, sink = job['s'], job.get('sink')
        if sink is not None:
            sink = sink * LOG2_E
        if job.get('mask') is not None:
            s = jnp.where(job['mask'], s, NEG_INF)
        m = jnp.max(s, axis=-1, keepdims=True)
        if 's2' in job:
            m = jnp.maximum(m, jnp.max(job['s2'], axis=-1, keepdims=True))
        if sink is not None:
            m = jnp.maximum(m, sink)
        o = jnp.dot(jnp.exp2(s - m).astype(BF16), job['v'], preferred_element_type=F32)
        if 's2' in job:
            o = o + jnp.dot(jnp.exp2(job['s2'] - m).astype(BF16), job['extra'][1], preferred_element_type=F32)
        den = o[:, HEAD_DIM:HEAD_DIM + 1]
        if sink is not None:
            den = den + jnp.exp2(sink - m)
        outs.append(o[:, :HEAD_DIM] / den)
    return outs


def _ctx_attn_kernel(sink_ref, z_ref, gq_ref, gk_ref, o_ref, bk_ref):
    z = z_ref[...]
    bq = _head_rms(z[:, 512:768], gq_ref[...])
    bk = _head_rms(z[:, 768:896], gk_ref[...])
    bk_ref[...] = bk
    groups = ((z[:, 0:256], z[:, 256:384], z[:, 384:512], True),
              (bq, bk, z[:, 896:1024], False))
    outs = []
    for q_all, k_all, v_all, use_sink in groups:
        k_all = k_all.astype(BF16)
        v_all = _with_ones(v_all)
        for hq in range(A_HEADS):
            kv = hq // (A_HEADS // A_KV_HEADS)
            outs.append(dict(q=q_all[:, hq * HEAD_DIM:(hq + 1) * HEAD_DIM],
                             k=k_all[:, kv * HEAD_DIM:(kv + 1) * HEAD_DIM],
                             v=v_all[:, kv * 2 * HEAD_DIM:(kv + 1) * 2 * HEAD_DIM],
                             sink=sink_ref[hq] if use_sink else None))
    o_ref[...] = jnp.concatenate(_attend_heads(outs), axis=-1).astype(o_ref.dtype)


def ctx_attention(z, sink, gq, gk):
    return pl.pallas_call(
        _ctx_attn_kernel,
        grid=(BATCH,),
        in_specs=[pl.BlockSpec(memory_space=pltpu.SMEM),
                  pl.BlockSpec((SEQ, 1024), lambda b: (b, 0)),
                  pl.BlockSpec((1, HEAD_DIM), lambda b: (0, 0)),
                  pl.BlockSpec((1, HEAD_DIM), lambda b: (0, 0))],
        out_specs=[pl.BlockSpec((SEQ, 512), lambda b: (b, 0)),
                   pl.BlockSpec((SEQ, 128), lambda b: (b, 0))],
        out_shape=[jax.ShapeDtypeStruct((CTX_TOK, 512), BF16),
                   jax.ShapeDtypeStruct((CTX_TOK, 128), F32)],
        compiler_params=_params(("arbitrary",)),
        name="ctx_attention",
    )(sink, z, gq.reshape(1, HEAD_DIM), gk.reshape(1, HEAD_DIM))


LB_TQ = 256


def _lat_b_kernel(zq_ref, zkv_ref, ck_ref, cv_ref, cq_ref, sq_ref, ckk_ref, skk_ref, gq_ref, gk_ref,
                  o_ref, k_scr, v_scr):
    L = zkv_ref.shape[0]

    @pl.when(pl.program_id(1) == 0)
    def _():
        kv = zkv_ref[...]
        bk = _rope_apply(_head_rms(kv[:, :128], gk_ref[...]), ckk_ref[...], skk_ref[...])
        k_scr[0:L, :] = bk.astype(BF16)
        k_scr[L:L + PAST_LEN, :] = ck_ref[...].astype(BF16)
        v_scr[0:L, :] = _with_ones(kv[:, 128:])
        v_scr[L:L + PAST_LEN, :] = _with_ones(cv_ref[...])

    q = _rope_apply(_head_rms(zq_ref[...], gq_ref[...]), cq_ref[...], sq_ref[...])
    outs = []
    for hq in range(B_HEADS):
        kv = hq // (B_HEADS // B_KV_HEADS)
        sl = slice(kv * HEAD_DIM, (kv + 1) * HEAD_DIM)
        outs.append(dict(q=q[:, hq * HEAD_DIM:(hq + 1) * HEAD_DIM], k=k_scr[:, sl],
                         v=v_scr[:, kv * 2 * HEAD_DIM:(kv + 1) * 2 * HEAD_DIM]))
    o_ref[...] = jnp.concatenate(_attend_heads(outs), axis=-1).astype(o_ref.dtype)


def latent_attention_b(z, row0, nb, L, cache_k, cache_v, rope_c, rope_s, gq, gk):
    nq = L // LB_TQ
    return pl.pallas_call(
        _lat_b_kernel,
        grid=(nb, nq),
        in_specs=[pl.BlockSpec((LB_TQ, 256), lambda b, i: (row0 // LB_TQ + b * nq + i, 2)),
                  pl.BlockSpec((L, 256), lambda b, i: (row0 // L + b, 3)),
                  pl.BlockSpec((None, PAST_LEN, 128), lambda b, i: (b, 0, 0)),
                  pl.BlockSpec((None, PAST_LEN, 128), lambda b, i: (b, 0, 0)),
                  pl.BlockSpec((LB_TQ, 256), lambda b, i: (i, 0)),
                  pl.BlockSpec((LB_TQ, 256), lambda b, i: (i, 0)),
                  pl.BlockSpec((L, 128), lambda b, i: (0, 0)),
                  pl.BlockSpec((L, 128), lambda b, i: (0, 0)),
                  pl.BlockSpec((1, HEAD_DIM), lambda b, i: (0, 0)),
                  pl.BlockSpec((1, HEAD_DIM), lambda b, i: (0, 0))],
        out_specs=pl.BlockSpec((LB_TQ, 256), lambda b, i: (b * nq + i, 0)),
        out_shape=jax.ShapeDtypeStruct((nb * L, 256), BF16),
        scratch_shapes=[pltpu.VMEM((L + PAST_LEN, 128), BF16),
                        pltpu.VMEM((L + PAST_LEN, 256), BF16)],
        compiler_params=_params(("arbitrary", "arbitrary")),
        name="latent_attention_b",
    )(z, z, cache_k, cache_v, rope_c, rope_s, rope_c, rope_s, gq.reshape(1, HEAD_DIM), gk.reshape(1, HEAD_DIM))


def _lat_a_kernel(sink_ref, zq_ref, zkv_ref, ck_ref, cv_ref, cq_ref, sq_ref, ckk_ref, skk_ref,
                  o_ref, k_scr, v_scr, ck_scr, cv_scr):
    L = zkv_ref.shape[0]
    i = pl.program_id(1)

    @pl.when(i == 0)
    def _():
        kv = zkv_ref[...]
        k_scr[0:BLOCK, :] = jnp.zeros((BLOCK, 128), BF16)
        v_scr[0:BLOCK, :] = jnp.zeros((BLOCK, 256), BF16)
        k_scr[BLOCK:BLOCK + L, :] = _rope_apply(kv[:, :128], ckk_ref[...], skk_ref[...]).astype(BF16)
        v_scr[BLOCK:BLOCK + L, :] = _with_ones(kv[:, 128:])
        k_scr[BLOCK + L:2 * BLOCK + L, :] = jnp.zeros((BLOCK, 128), BF16)
        v_scr[BLOCK + L:2 * BLOCK + L, :] = jnp.zeros((BLOCK, 256), BF16)
        ck_scr[...] = ck_ref[...].astype(BF16)
        cv_scr[...] = _with_ones(cv_ref[...])

    q = _rope_apply(zq_ref[...], cq_ref[...], sq_ref[...])
    start = pl.multiple_of(i * BLOCK, BLOCK)
    kband = k_scr[pl.ds(start, 3 * BLOCK), :]
    vband = v_scr[pl.ds(start, 3 * BLOCK), :]
    r = lax.broadcasted_iota(jnp.int32, (BLOCK, 3 * BLOCK), 0)
    cidx = lax.broadcasted_iota(jnp.int32, (BLOCK, 3 * BLOCK), 1)
    kpos = i * BLOCK - BLOCK + cidx
    mask = (jnp.abs(cidx - BLOCK - r) <= WINDOW) & (kpos >= 0) & (kpos < L)
    outs = []
    for hq in range(A_HEADS):
        kv = hq // (A_HEADS // A_KV_HEADS)
        sl = slice(kv * HEAD_DIM, (kv + 1) * HEAD_DIM)
        sv = slice(kv * 2 * HEAD_DIM, (kv + 1) * 2 * HEAD_DIM)
        outs.append(dict(q=q[:, hq * HEAD_DIM:(hq + 1) * HEAD_DIM], k=kband[:, sl], v=vband[:, sv],
                         extra=(ck_scr[:, sl], cv_scr[:, sv]), sink=sink_ref[hq], mask=mask))
    o_ref[...] = jnp.concatenate(_attend_heads(outs), axis=-1).astype(o_ref.dtype)


def latent_attention_a(z, row0, nb, L, sink, cache_k, cache_v, rope_c, rope_s):
    nq = L // BLOCK
    return pl.pallas_call(
        _lat_a_kernel,
        grid=(nb, nq),
        in_specs=[pl.BlockSpec(memory_space=pltpu.SMEM),
                  pl.BlockSpec((BLOCK, 256), lambda b, i: (row0 // BLOCK + b * nq + i, 0)),
                  pl.BlockSpec((L, 256), lambda b, i: (row0 // L + b, 1)),
                  pl.BlockSpec((None, PAST_LEN, 128), lambda b, i: (b, 0, 0)),
                  pl.BlockSpec((None, PAST_LEN, 128), lambda b, i: (b, 0, 0)),
                  pl.BlockSpec((BLOCK, 256), lambda b, i: (i, 0)),
                  pl.BlockSpec((BLOCK, 256), lambda b, i: (i, 0)),
                  pl.BlockSpec((L, 128), lambda b, i: (0, 0)),
                  pl.BlockSpec((L, 128), lambda b, i: (0, 0))],
        out_specs=pl.BlockSpec((BLOCK, 256), lambda b, i: (b * nq + i, 0)),
        out_shape=jax.ShapeDtypeStruct((nb * L, 256), BF16),
        scratch_shapes=[pltpu.VMEM((L + 2 * BLOCK, 128), BF16),
                        pltpu.VMEM((L + 2 * BLOCK, 256), BF16),
                        pltpu.VMEM((PAST_LEN, 128), BF16),
                        pltpu.VMEM((PAST_LEN, 256), BF16)],
        compiler_params=_params(("arbitrary", "arbitrary")),
        name="latent_attention_a",
    )(sink, z, z, cache_k, cache_v, rope_c, rope_s, rope_c, rope_s)


_TN = (((0,), (0,)), ((), ()))
HG_GROUP = 8


def _hgrn_kernel(zq_ref, zf_ref, zi_ref, zg_ref, lb_ref, gn_ref, s0_ref, o_ref, sT_ref,
                 of_scr, ob_scr, g_scr, k_scr, qin_scr, kin_scr, v_scr, S_scr, *, tt):
    d = pl.program_id(1)
    j = pl.program_id(2)
    n_t = pl.num_programs(2)
    C = HGRN_CHUNK
    n_c = tt // C

    @pl.when(j == 0)
    def _():
        S_scr[...] = s0_ref[...]

    lb = lb_ref[...]
    sg = jax.nn.sigmoid(zf_ref[...])
    logf = jnp.log(lb + (1.0 - lb) * sg)
    k = (1.0 - lb) * (1.0 - sg)
    k_scr[...] = k
    v_scr[...] = zi_ref[...].astype(BF16)
    in_chunk = lax.broadcasted_iota(jnp.int32, (tt, C_HEADS * HEAD_DIM), 0) % C
    row = lax.broadcasted_iota(jnp.int32, (C, C), 0)
    col = lax.broadcasted_iota(jnp.int32, (C, C), 1)
    heads = [slice(h * HEAD_DIM, (h + 1) * HEAD_DIM) for h in range(C_HEADS)]

    def run(reverse, tile):
        G = logf
        step = 1
        while step < C:
            if reverse:
                G = G + jnp.where(in_chunk < C - step, pltpu.roll(G, tt - step, 0), 0.0)
            else:
                G = G + jnp.where(in_chunk >= step, pltpu.roll(G, step, 0), 0.0)
            step *= 2
        g_scr[...] = G
        qin_scr[...] = (zq_ref[...] * jnp.exp(G)).astype(BF16)
        kin_scr[...] = (k * jnp.exp(-G)).astype(BF16)
        tri = (row <= col) if reverse else (row >= col)

        def body(gi, carry):
            chunks = []
            for g in range(HG_GROUP):
                ci = gi * HG_GROUP + g
                c = (n_c - 1 - ci) if reverse else ci
                r0 = pl.multiple_of(c * C, C)
                rows = pl.ds(r0, C)
                G_c = g_scr[rows, :]
                G_end = G_c[0:1, :] if reverse else G_c[C - 1:C, :]
                vc = v_scr[rows, :]
                q_in = qin_scr[rows, :]
                k_in = kin_scr[rows, :]
                k_out = (k_scr[rows, :] * jnp.exp(G_end - G_c)).astype(BF16)
                chunks.append(dict(
                    r0=r0, rows=rows, vc=vc, q_in=q_in, decay=jnp.exp(G_end),
                    a=[lax.dot_general(q_in[:, sl], k_in[:, sl], _NT, preferred_element_type=F32) for sl in heads],
                    kv=[lax.dot_general(vc[:, sl], k_out[:, sl], _TN, preferred_element_type=F32) for sl in heads]))
            s_cur = [S_scr[h] for h in range(C_HEADS)]
            for ch in chunks:
                ch['qs'] = [lax.dot_general(ch['q_in'][:, sl], s_cur[h].astype(BF16), _NT,
                                            preferred_element_type=F32) for h, sl in enumerate(heads)]
                s_cur = [s_cur[h] * ch['decay'][:, sl] + ch['kv'][h] for h, sl in enumerate(heads)]
            for h in range(C_HEADS):
                S_scr[h] = s_cur[h]
            for ch in chunks:
                o_c = jnp.concatenate(
                    [jnp.dot(jnp.where(tri, ch['a'][h], 0.0).astype(BF16), ch['vc'][:, sl],
                             preferred_element_type=F32) + ch['qs'][h] for h, sl in enumerate(heads)], axis=-1)
                if reverse:
                    ob_scr[ch['rows'], :] = o_c
                else:
                    of_scr[pl.ds(pl.multiple_of(tile * tt, tt) + ch['r0'], C), :] = o_c
            return carry

        lax.fori_loop(0, n_c // HG_GROUP, body, 0)

    @pl.when(d == 0)
    def _():
        run(False, j)

    @pl.when(d == 1)
    def _():
        tile = n_t - 1 - j
        run(True, tile)
        o = of_scr[pl.ds(pl.multiple_of(tile * tt, tt), tt), :] + ob_scr[...]
        g = zg_ref[...]
        o_ref[...] = (_head_rms(o, gn_ref[...]) * (g * jax.nn.sigmoid(g))).astype(o_ref.dtype)

    @pl.when(j == n_t - 1)
    def _():
        sT_ref[...] = S_scr[...]


def hgrn_mixer(z, row0, nb, L, lb, gn, s0_t):
    tt = min(L, 512)
    n_t = L // tt
    rb = row0 // tt

    def tile(d, j):
        return jnp.where(d == 0, j, n_t - 1 - j)

    def late(d, j):
        return jnp.where(d == 0, n_t - 1, n_t - 1 - j)

    st_spec = pl.BlockSpec((None, None, C_HEADS, HEAD_DIM, HEAD_DIM), lambda b, d, j: (b, d, 0, 0, 0))
    return pl.pallas_call(
        functools.partial(_hgrn_kernel, tt=tt),
        grid=(nb, 2, n_t),
        in_specs=[pl.BlockSpec((tt, 256), lambda b, d, j: (rb + b * n_t + tile(d, j), 4)),
                  pl.BlockSpec((tt, 256), lambda b, d, j: (rb + b * n_t + tile(d, j), 5 + d)),
                  pl.BlockSpec((tt, 256), lambda b, d, j: (rb + b * n_t + tile(d, j), 7)),
                  pl.BlockSpec((tt, 256), lambda b, d, j: (rb + b * n_t + late(d, j), 8)),
                  pl.BlockSpec((1, 256), lambda b, d, j: (0, 0)),
                  pl.BlockSpec((1, HEAD_DIM), lambda b, d, j: (0, 0)),
                  st_spec],
        out_specs=[pl.BlockSpec((tt, 256), lambda b, d, j: (b * n_t + late(d, j), 0)),
                   st_spec],
        out_shape=[jax.ShapeDtypeStruct((nb * L, 256), BF16),
                   jax.ShapeDtypeStruct((nb, 2, C_HEADS, HEAD_DIM, HEAD_DIM), F32)],
        scratch_shapes=[pltpu.VMEM((L, 256), F32),
                        pltpu.VMEM((tt, 256), F32),
                        pltpu.VMEM((tt, 256), F32),
                        pltpu.VMEM((tt, 256), F32),
                        pltpu.VMEM((tt, 256), BF16),
                        pltpu.VMEM((tt, 256), BF16),
                        pltpu.VMEM((tt, 256), BF16),
                        pltpu.VMEM((C_HEADS, HEAD_DIM, HEAD_DIM), F32)],
        compiler_params=_params(("arbitrary", "arbitrary", "arbitrary")),
        name="hgrn_mixer",
    )(z, z, z, z, lb.reshape(1, 256), gn.reshape(1, HEAD_DIM), s0_t)


DL_C = DELTA_CHUNK
DL_PREP_TT = 256
DL_HALO = 8
DL_CHUNK_TT = 512
N_QKV_HEADS = 3 * D_HEADS
DL_GROUP = 2


def _delta_prep_kernel(x_ref, xp_ref, xn_ref, zab_ref, cw_ref, na_ref, dtb_ref, qkv_ref, gate_ref, xs_scr):
    tt = x_ref.shape[0]
    row = pl.program_id(0) * tt
    lat = row - CTX_TOK
    first = jnp.where(row < CTX_TOK, True, lat % DEC_SEQ == 0)
    last = jnp.where(row < CTX_TOK, True, (lat + tt) % DEC_SEQ == 0)
    xs_scr[DL_HALO:DL_HALO + tt, :] = x_ref[...]
    xs_scr[0:DL_HALO, :] = jnp.where(first, 0.0, xp_ref[...])
    xs_scr[DL_HALO + tt:2 * DL_HALO + tt, :] = jnp.where(last, 0.0, xn_ref[...])
    pad = (CONV_K - 1) // 2
    y = None
    for t in range(CONV_K):
        term = xs_scr[pl.ds(DL_HALO - pad + t, tt), :] * cw_ref[t:t + 1, :]
        y = term if y is None else y + term
    y = y * jax.nn.sigmoid(y)
    for idx in range(N_QKV_HEADS):
        xh = y[:, idx * HEAD_DIM:(idx + 1) * HEAD_DIM]
        if idx < 2 * D_HEADS:
            xh = xh * lax.rsqrt(jnp.sum(xh * xh, axis=-1, keepdims=True) + EPS)
        if idx < D_HEADS:
            xh = xh * ATT_SCALE
        qkv_ref[idx] = xh
    zab = zab_ref[...]
    lane = lax.broadcasted_iota(jnp.int32, zab.shape, 1)
    t_ = zab + dtb_ref[...]
    softplus = jnp.maximum(t_, 0.0) + jnp.log(1.0 + jnp.exp(-jnp.abs(t_)))
    gate_ref[...] = jnp.where(lane < 2 * D_HEADS, na_ref[...] * softplus, jax.nn.sigmoid(zab))


def delta_prep(z, conv_w, a_log, dt_bias):
    tt = DL_PREP_TT
    hb = tt // DL_HALO
    n_hb = N_TOK // DL_HALO
    pad8 = lambda v: jnp.concatenate([v.reshape(1, 2 * D_HEADS), jnp.zeros((1, 128 - 2 * D_HEADS), F32)], axis=1)
    return pl.pallas_call(
        _delta_prep_kernel,
        grid=(N_TOK // tt,),
        in_specs=[pl.BlockSpec((tt, 768), lambda i: (i, 3)),
                  pl.BlockSpec((DL_HALO, 768), lambda i: (jnp.maximum(i * hb - 1, 0), 3)),
                  pl.BlockSpec((DL_HALO, 768), lambda i: (jnp.minimum((i + 1) * hb, n_hb - 1), 3)),
                  pl.BlockSpec((tt, 128), lambda i: (i, Z_DAB // 128)),
                  pl.BlockSpec((CONV_K, 768), lambda i: (0, 0)),
                  pl.BlockSpec((1, 128), lambda i: (0, 0)),
                  pl.BlockSpec((1, 128), lambda i: (0, 0))],
        out_specs=[pl.BlockSpec((N_QKV_HEADS, tt, HEAD_DIM), lambda i: (0, i, 0)),
                   pl.BlockSpec((tt, 128), lambda i: (i, 0))],
        out_shape=[jax.ShapeDtypeStruct((N_QKV_HEADS, N_TOK, HEAD_DIM), F32),
                   jax.ShapeDtypeStruct((N_TOK, 128), F32)],
        scratch_shapes=[pltpu.VMEM((tt + 2 * DL_HALO, 768), F32)],
        compiler_params=_params(("arbitrary",)),
        name="delta_prep",
    )(z, z, z, z, conv_w, pad8(-jnp.exp(a_log)), pad8(dt_bias))


def _split_bf16(a):
    hi = a.astype(BF16)
    return hi, (a - hi.astype(F32)).astype(BF16)


def _dot_hl(a_parts, b_parts):
    (a_hi, a_lo), (b_hi, b_lo) = a_parts, b_parts
    m = a_hi.shape[0]
    r = jnp.dot(jnp.concatenate([a_hi, a_lo], axis=0), b_hi, preferred_element_type=F32)
    return r[:m] + r[m:] + jnp.dot(a_hi, b_lo, preferred_element_type=F32)


def _delta_chunk_kernel(qkv_ref, gate_ref, u2_ref, wq_ref, ak_ref):
    C = DL_C
    n_c = gate_ref.shape[0] // C
    row = lax.broadcasted_iota(jnp.int32, (C, C), 0)
    col = lax.broadcasted_iota(jnp.int32, (C, C), 1)
    eye = (row == col).astype(F32)
    t_idx = lax.broadcasted_iota(jnp.int32, (C, 128), 0)

    def chunk_chains(c):
        r0 = pl.multiple_of(c * C, C)
        ga = gate_ref[pl.ds(r0, C), :]
        chains = []
        for d in range(2):
            incl = (row >= col) if d == 0 else (row <= col)
            strict = (row > col) if d == 0 else (row < col)
            g_all = ga
            step = 1
            while step < C:
                if d == 0:
                    g_all = g_all + jnp.where(t_idx >= step, pltpu.roll(g_all, step, 0), 0.0)
                else:
                    g_all = g_all + jnp.where(t_idx < C - step, pltpu.roll(g_all, C - step, 0), 0.0)
                step *= 2
            g_all_t = g_all.T
            for h in range(D_HEADS):
                ci = d * D_HEADS + h
                q = qkv_ref[h, pl.ds(r0, C), :]
                k = qkv_ref[D_HEADS + h, pl.ds(r0, C), :]
                v = qkv_ref[2 * D_HEADS + h, pl.ds(r0, C), :]
                g_col = g_all[:, ci:ci + 1]
                g_row = g_all_t[ci:ci + 1, :]
                beta = ga[:, 2 * D_HEADS + ci:2 * D_HEADS + ci + 1]
                g_end = g_col[C - 1:C, :] if d == 0 else g_col[0:1, :]
                kb = k * beta
                eg = jnp.exp(g_col)
                decay = jnp.where(incl, jnp.exp(jnp.where(incl, g_col - g_row, 0.0)), 0.0)
                kq = jnp.concatenate([kb, q], axis=0).astype(BF16)
                chains.append(dict(
                    strict=strict, decay=decay, qg=q * eg, g_end=g_end,
                    r=lax.dot_general(kq, k.astype(BF16), _NT, preferred_element_type=F32),
                    rhs=jnp.concatenate([v * beta, kb * eg], axis=1).astype(BF16),
                    ke_t=(k * jnp.exp(g_end - g_col)).T))
        return r0, chains

    def body(gi, carry):
        groups = [(gi * DL_GROUP + cc,) + chunk_chains(gi * DL_GROUP + cc) for cc in range(DL_GROUP)]
        chains = [ch for _, _, chs in groups for ch in chs]
        for ch in chains:
            ch['p'] = -jnp.where(ch['strict'], ch['r'][:C] * ch['decay'], 0.0)
            ch['t'] = eye + ch['p']
        for _ in range(5):
            for ch in chains:
                parts = _split_bf16(ch['p'])
                ch['p'] = _dot_hl(parts, parts)
            for ch in chains:
                ch['t'] = ch['t'] + _dot_hl(_split_bf16(ch['t']), _split_bf16(ch['p']))
        for ch in chains:
            ch['uw'] = jnp.dot(ch['t'].astype(BF16), ch['rhs'], preferred_element_type=F32)
        pack = lambda xs: jnp.stack(xs).reshape((2, D_HEADS) + xs[0].shape)
        for c, r0, chs in groups:
            u2 = [jnp.concatenate([ch['uw'][:, :C], jnp.broadcast_to(jnp.exp(ch['g_end']), (C, C))], axis=1)
                  for ch in chs]
            wq = [jnp.concatenate([ch['uw'][:, C:], ch['qg']], axis=0).astype(BF16) for ch in chs]
            ak = [jnp.concatenate([ch['r'][C:] * ch['decay'], ch['ke_t']], axis=0).astype(BF16) for ch in chs]
            u2_ref[:, :, pl.ds(r0, C), :] = pack(u2)
            wq_ref[:, :, c] = pack(wq)
            ak_ref[:, :, c] = pack(ak)
        return carry

    lax.fori_loop(0, n_c // DL_GROUP, body, 0)


def delta_chunks(qkv, gates):
    tt = DL_CHUNK_TT
    n_c = tt // DL_C
    return pl.pallas_call(
        _delta_chunk_kernel,
        grid=(N_TOK // tt,),
        in_specs=[pl.BlockSpec((N_QKV_HEADS, tt, HEAD_DIM), lambda i: (0, i, 0)),
                  pl.BlockSpec((tt, 128), lambda i: (i, 0))],
        out_specs=[pl.BlockSpec((2, D_HEADS, tt, 128), lambda i: (0, 0, i, 0)),
                   pl.BlockSpec((2, D_HEADS, n_c, 2 * DL_C, HEAD_DIM), lambda i: (0, 0, i, 0, 0)),
                   pl.BlockSpec((2, D_HEADS, n_c, 2 * DL_C, HEAD_DIM), lambda i: (0, 0, i, 0, 0))],
        out_shape=[jax.ShapeDtypeStruct((2, D_HEADS, N_TOK, 128), F32),
                   jax.ShapeDtypeStruct((2, D_HEADS, N_TOK // DL_C, 2 * DL_C, HEAD_DIM), BF16),
                   jax.ShapeDtypeStruct((2, D_HEADS, N_TOK // DL_C, 2 * DL_C, HEAD_DIM), BF16)],
        compiler_params=_params(("arbitrary",)),
        name="delta_chunks",
    )(qkv, gates)


def _delta_scan_kernel(u2f_ref, wqf_ref, akf_ref, u2b_ref, wqb_ref, akb_ref, s0_ref,
                       of_ref, ob_ref, s_ref, s_scr):
    j = pl.program_id(1)
    C = DL_C
    n_c = wqf_ref.shape[1]

    @pl.when(j == 0)
    def _():
        s_scr[...] = s0_ref[...]

    def body(ci, carry):
        chains = []
        for d, (u2_ref, wq_ref, ak_ref, o_ref) in enumerate(((u2f_ref, wqf_ref, akf_ref, of_ref),
                                                            (u2b_ref, wqb_ref, akb_ref, ob_ref))):
            c = ci if d == 0 else n_c - 1 - ci
            r0 = pl.multiple_of(c * C, C)
            for h in range(D_HEADS):
                s = s_scr[d, h]
                chains.append(dict(d=d, h=h, c=c, r0=r0, s=s, ak_ref=ak_ref, o_ref=o_ref,
                                   u2=u2_ref[h, pl.ds(r0, C), :],
                                   r1=jnp.dot(wq_ref[h, c], s.astype(BF16),
                                              preferred_element_type=F32)))
        for ch in chains:
            v_new = ch['u2'][:, :C] - ch['r1'][:C]
            ch['r2'] = jnp.dot(ch['ak_ref'][ch['h'], ch['c']], v_new.astype(BF16),
                               preferred_element_type=F32)
        for ch in chains:
            ch['o_ref'][ch['h'], pl.ds(ch['r0'], C), :] = ch['r1'][C:] + ch['r2'][:C]
            s_scr[ch['d'], ch['h']] = ch['s'] * ch['u2'][0:1, C:] + ch['r2'][C:]
        return carry

    lax.fori_loop(0, n_c, body, 0)

    @pl.when(j == pl.num_programs(1) - 1)
    def _():
        s_ref[...] = s_scr[...]


def delta_scan(u2, wq, ak, row0, nb, L, s0):
    tt = min(L, 512)
    n_t = L // tt
    n_c = tt // DL_C
    rb = row0 // tt
    fwd = lambda b, j: rb + b * n_t + j
    bwd = lambda b, j: rb + b * n_t + (n_t - 1 - j)
    u_spec = lambda d, f: pl.BlockSpec((None, D_HEADS, tt, 128), lambda b, j: (d, 0, f(b, j), 0))
    c_spec = lambda d, f: pl.BlockSpec((None, D_HEADS, n_c, 2 * DL_C, HEAD_DIM), lambda b, j: (d, 0, f(b, j), 0, 0))
    st_spec = pl.BlockSpec((None, 2, D_HEADS, HEAD_DIM, HEAD_DIM), lambda b, j: (b, 0, 0, 0, 0))
    return pl.pallas_call(
        _delta_scan_kernel,
        grid=(nb, n_t),
        in_specs=[u_spec(0, fwd), c_spec(0, fwd), c_spec(0, fwd),
                  u_spec(1, bwd), c_spec(1, bwd), c_spec(1, bwd), st_spec],
        out_specs=[pl.BlockSpec((D_HEADS, tt, HEAD_DIM), lambda b, j: (0, b * n_t + j, 0)),
                   pl.BlockSpec((D_HEADS, tt, HEAD_DIM), lambda b, j: (0, b * n_t + (n_t - 1 - j), 0)),
                   st_spec],
        out_shape=[jax.ShapeDtypeStruct((D_HEADS, nb * L, HEAD_DIM), F32),
                   jax.ShapeDtypeStruct((D_HEADS, nb * L, HEAD_DIM), F32),
                   jax.ShapeDtypeStruct((nb, 2, D_HEADS, HEAD_DIM, HEAD_DIM), F32)],
        scratch_shapes=[pltpu.VMEM((2, D_HEADS, HEAD_DIM, HEAD_DIM), F32)],
        compiler_params=_params(("arbitrary", "arbitrary")),
        name="delta_scan",
    )(u2, wq, ak, u2, wq, ak, s0)


def _delta_out_kernel(of_ref, ob_ref, zg_ref, gn_ref, o_ref):
    outs = []
    for h in range(D_HEADS):
        o = of_ref[h] + ob_ref[h]
        outs.append(o * lax.rsqrt(jnp.mean(o * o, axis=-1, keepdims=True) + EPS) * gn_ref[...])
    g = zg_ref[...]
    o_ref[...] = (jnp.concatenate(outs, axis=-1) * (g * jax.nn.sigmoid(g))).astype(o_ref.dtype)


def delta_output(o_f, o_b, z, row0, gn):
    n = o_f.shape[1]
    tt = 256
    return pl.pallas_call(
        _delta_out_kernel,
        grid=(n // tt,),
        in_specs=[pl.BlockSpec((D_HEADS, tt, HEAD_DIM), lambda i: (0, i, 0)),
                  pl.BlockSpec((D_HEADS, tt, HEAD_DIM), lambda i: (0, i, 0)),
                  pl.BlockSpec((tt, 256), lambda i: (row0 // tt + i, Z_DG // 256)),
                  pl.BlockSpec((1, HEAD_DIM), lambda i: (0, 0))],
        out_specs=pl.BlockSpec((tt, 256), lambda i: (i, 0)),
        out_shape=jax.ShapeDtypeStruct((n, 256), BF16),
        compiler_params=_params(("arbitrary",)),
        name="delta_output",
    )(o_f, o_b, z, gn.reshape(1, HEAD_DIM))


def kernel(x_prompt, x_sample, cache_attn_a_k, cache_attn_a_v, cache_attn_b_k, cache_attn_b_v,
           state_hgrn, state_delta, c, c_ctx, norm1_g, norm2_g, w_ada, b_ada, w_in, a_sink,
           b_qnorm_g, b_knorm_g, c_lb, c_onorm_g, d_conv, d_a_log, d_dt_bias, d_onorm_g,
           w_branch, w_out, ffn_w1, ffn_w3, ffn_w2, router_w, router_b, moe_w1, moe_w3, moe_w2,
           final_norm_g):
    cum = jnp.cumsum(jax.nn.softmax(c_lb, axis=0), axis=0)
    lower_bounds = cum - cum[:1]

    x = jnp.concatenate([x_prompt.reshape(CTX_TOK, D_MODEL), x_sample.reshape(LAT_TOK, D_MODEL)], axis=0)
    cond = jnp.concatenate([c_ctx[None, :], c, jnp.zeros((16 - N_COND, D_MODEL), F32)], axis=0)

    rope_c, rope_s = rope_lane_tables(DEC_SEQ)
    caches = []
    for l in range(DEPTH):
        mod = ada_modulation(cond, w_ada[l], b_ada[l])[:N_COND].reshape(N_COND, 6, D_MODEL)
        w_mix = jnp.concatenate([w_in[l][:, :Z_MAIN], w_in[l][:, Z_MAIN + 16:W_IN_MIX],
                                 w_in[l][:, Z_MAIN:Z_MAIN + 16], jnp.zeros((D_MODEL, 128 - 16), F32)],
                                axis=1).astype(BF16)
        w_gl = w_in[l][:, W_IN_MIX:].reshape(D_MODEL, N_BRANCH, D_MODEL).transpose(1, 0, 2).astype(BF16)
        z = input_projection(x, mod, norm1_g[l], w_mix)
        kv2 = lambda t: t.reshape(DEC_BATCH, PAST_LEN, 128)
        o_ab_ctx, bk_ctx = ctx_attention(z, a_sink[l], b_qnorm_g[l], b_knorm_g[l])
        o_a_lat = latent_attention_a(z, CTX_TOK, DEC_BATCH, DEC_SEQ, a_sink[l], kv2(cache_attn_a_k[:, l]),
                                     kv2(cache_attn_a_v[:, l]), rope_c, rope_s)
        o_b_lat = latent_attention_b(z, CTX_TOK, DEC_BATCH, DEC_SEQ, kv2(cache_attn_b_k[:, l]),
                                     kv2(cache_attn_b_v[:, l]), rope_c, rope_s, b_qnorm_g[l], b_knorm_g[l])
        o_c_ctx, sc_t = hgrn_mixer(z, 0, BATCH, SEQ, lower_bounds[l], c_onorm_g[l],
                                   jnp.zeros((BATCH, 2, C_HEADS, HEAD_DIM, HEAD_DIM), F32))
        o_c_lat, _ = hgrn_mixer(z, CTX_TOK, DEC_BATCH, DEC_SEQ, lower_bounds[l], c_onorm_g[l],
                                jnp.swapaxes(state_hgrn[:, l], -1, -2))

        qkv, gates = delta_prep(z, d_conv[l], d_a_log[l], d_dt_bias[l])
        u2, wq, ak = delta_chunks(qkv, gates)
        of_ctx, ob_ctx, sd = delta_scan(u2, wq, ak, 0, BATCH, SEQ,
                                        jnp.zeros((BATCH, 2, D_HEADS, HEAD_DIM, HEAD_DIM), F32))
        of_lat, ob_lat, _ = delta_scan(u2, wq, ak, CTX_TOK, DEC_BATCH, DEC_SEQ, state_delta[:, l])
        o_d_ctx = delta_output(of_ctx, ob_ctx, z, 0, d_onorm_g[l])
        o_d_lat = delta_output(of_lat, ob_lat, z, CTX_TOK, d_onorm_g[l])
        kvh = lambda t: t.reshape(BATCH, SEQ, 2, HEAD_DIM)
        caches.append((kvh(z[:CTX_TOK, 256:384]), kvh(z[:CTX_TOK, 384:512]), kvh(bk_ctx), kvh(z[:CTX_TOK, 896:1024]),
                       jnp.swapaxes(sc_t, -1, -2), sd))
        x = merge_projection(x, mod, norm1_g[l], (o_ab_ctx, o_c_ctx, o_d_ctx), (o_a_lat, o_b_lat, o_c_lat, o_d_lat),
                             w_gl, w_branch[l].astype(BF16), w_out[l].astype(BF16))
        j = l // 2
        if l % 2 == 0:
            x = dense_ffn(x, mod, norm2_g[l], ffn_w1[j].astype(BF16), ffn_w3[j].astype(BF16),
                          ffn_w2[j].astype(BF16))
        else:
            assert l == DEPTH - 1, "the expert layer's residual is fused with the final norm"
            rw = jnp.concatenate([router_w[j], jnp.zeros((D_MODEL, 128 - N_EXPERTS), F32)], axis=1)
            rb = jnp.concatenate([router_b[j], jnp.zeros((128 - N_EXPERTS,), F32)])[None, :]
            h2, pos, wgt = moe_router(x, mod, norm2_g[l], rw, rb)
            f_tiles = lambda w: w.astype(BF16).reshape(N_EXPERTS, D_MODEL, D_FF_EXPERT // MOE_F, MOE_F).transpose(0, 2, 1, 3)
            f = moe_experts(h2, pos, wgt, f_tiles(moe_w1[j]), f_tiles(moe_w3[j]), moe_w2[j].astype(BF16))
            y_prompt = residual_final_norm(x, f, mod, final_norm_g, 0, CTX_TOK).reshape(BATCH, SEQ, D_MODEL)
            y_sample = residual_final_norm(x, f, mod, final_norm_g, CTX_TOK, LAT_TOK).reshape(DEC_BATCH, DEC_SEQ, D_MODEL)

    stack = lambda idx: jnp.stack([caches[l][idx] for l in range(DEPTH)], axis=1)
    return (y_prompt, y_sample, stack(0), stack(1), stack(2), stack(3), stack(4), stack(5))
```

```python
import functools

import jax
import jax.numpy as jnp
import numpy as np
from jax import lax
from jax.experimental import pallas as pl
from jax.experimental.pallas import tpu as pltpu

F32 = jnp.float32
BF16 = jnp.bfloat16

D_MODEL = 1024
BATCH = 32
SEQ = 256
DEPTH = 2
DEC_BATCH = 8
DEC_SEQ = 4096
PAST_LEN = 256
GRID_W = 64
HEAD_DIM = 64
A_HEADS = 4
A_KV_HEADS = 2
B_HEADS = 4
B_KV_HEADS = 2
C_HEADS = 4
D_HEADS = 4
BRANCH_W = 256
N_BRANCH = 4
WINDOW = 128
BLOCK = 128
ROPE_BASE = 10000.0
HGRN_CHUNK = 32
DELTA_CHUNK = 64
CONV_K = 5
D_FF = 2816
N_EXPERTS = 8
D_FF_EXPERT = 3584
EPS = 1e-6
NEG_INF = -1e30
F32_MIN = float(np.finfo(np.float32).min)

CTX_TOK = BATCH * SEQ
LAT_TOK = DEC_BATCH * DEC_SEQ
N_TOK = CTX_TOK + LAT_TOK
N_COND = 1 + DEC_BATCH

Z_MAIN = 3072
Z_DG = Z_MAIN
Z_DAB = Z_DG + BRANCH_W
Z_COLS = Z_DAB + 128
W_IN_MIX = 3344

TM = 512
VMEM_LIMIT = 56 * 1024 * 1024


def _tile_cond(i, tm):
    ctx_tiles = CTX_TOK // tm
    per_b = DEC_SEQ // tm
    return jnp.where(i < ctx_tiles, 0, 1 + (i - ctx_tiles) // per_b)


def _rms(x, g):
    return x * lax.rsqrt(jnp.mean(x * x, axis=-1, keepdims=True) + EPS) * g


def _params(sem):
    return pltpu.CompilerParams(dimension_semantics=sem, vmem_limit_bytes=VMEM_LIMIT)


def _ada_kernel(c_ref, w_ref, b_ref, o_ref):
    c = c_ref[...]
    s = c * jax.nn.sigmoid(c)
    o_ref[...] = jnp.dot(s.astype(BF16), w_ref[...].astype(BF16), preferred_element_type=F32) + b_ref[...]


def ada_modulation(cond_pad, w, b):
    n = 6 * D_MODEL
    tn = 1536
    return pl.pallas_call(
        _ada_kernel,
        grid=(n // tn,),
        in_specs=[pl.BlockSpec((16, D_MODEL), lambda j: (0, 0)),
                  pl.BlockSpec((D_MODEL, tn), lambda j: (0, j)),
                  pl.BlockSpec((1, tn), lambda j: (0, j))],
        out_specs=pl.BlockSpec((16, tn), lambda j: (0, j)),
        out_shape=jax.ShapeDtypeStruct((16, n), F32),
        compiler_params=_params(("arbitrary",)),
        name="ada_modulation",
    )(cond_pad, w, b.reshape(1, n))


def _in_kernel(x_ref, mod_ref, g_ref, w_ref, z_ref):
    h = _rms(x_ref[...], g_ref[...]) * (1.0 + mod_ref[1:2, :]) + mod_ref[0:1, :]
    z_ref[...] = jnp.dot(h.astype(BF16), w_ref[...], preferred_element_type=F32)


def input_projection(x, mod, g, w):
    nt = N_TOK // TM
    return pl.pallas_call(
        _in_kernel,
        grid=(nt,),
        in_specs=[pl.BlockSpec((TM, D_MODEL), lambda i: (i, 0)),
                  pl.BlockSpec((None, 6, D_MODEL), lambda i: (_tile_cond(i, TM), 0, 0)),
                  pl.BlockSpec((1, D_MODEL), lambda i: (0, 0)),
                  pl.BlockSpec((D_MODEL, Z_COLS), lambda i: (0, 0))],
        out_specs=pl.BlockSpec((TM, Z_COLS), lambda i: (i, 0)),
        out_shape=jax.ShapeDtypeStruct((N_TOK, Z_COLS), F32),
        compiler_params=_params(("arbitrary",)),
        name="input_projection",
    )(x, mod, g.reshape(1, D_MODEL), w)


def _merge_kernel(x_ref, mod_ref, g_ref, ab_c_ref, c_c_ref, df_c_ref, db_c_ref, a_l_ref, b_l_ref, c_l_ref,
                  df_l_ref, db_l_ref, zg_ref, gd_ref, wgl_ref, wbr_ref, wout_ref, xo_ref):
    x = x_ref[...]
    h = (_rms(x, g_ref[...]) * (1.0 + mod_ref[1:2, :]) + mod_ref[0:1, :]).astype(BF16)
    is_ctx = pl.program_id(0) < CTX_TOK // TM
    d_heads = []
    for hd in range(D_HEADS):
        o = jnp.where(is_ctx, df_c_ref[hd] + db_c_ref[hd], df_l_ref[hd] + db_l_ref[hd])
        d_heads.append(o * lax.rsqrt(jnp.mean(o * o, axis=-1, keepdims=True) + EPS) * gd_ref[...])
    dg = zg_ref[...]
    branches = (jnp.where(is_ctx, ab_c_ref[:, :BRANCH_W], a_l_ref[...]),
                jnp.where(is_ctx, ab_c_ref[:, BRANCH_W:], b_l_ref[...]),
                jnp.where(is_ctx, c_c_ref[...], c_l_ref[...]),
                (jnp.concatenate(d_heads, axis=-1) * (dg * jax.nn.sigmoid(dg))).astype(BF16))
    merged = None
    for j in range(N_BRANCH):
        gate = jax.nn.sigmoid(jnp.dot(h, wgl_ref[j], preferred_element_type=F32))
        br = jnp.dot(branches[j], wbr_ref[j], preferred_element_type=F32)
        merged = gate * br if merged is None else merged + gate * br
    mix = jnp.dot(merged.astype(BF16), wout_ref[...], preferred_element_type=F32)
    xo_ref[...] = x + mod_ref[2:3, :] * mix


def merge_projection(x, mod, g, o_ctx, o_lat, z, gd, wgl, wbr, wout):
    nt = N_TOK // TM
    ctx_tiles = CTX_TOK // TM
    ctx_i = lambda i: jnp.minimum(i, ctx_tiles - 1)
    lat_i = lambda i: jnp.maximum(i - ctx_tiles, 0)
    ctx_spec = lambda w: pl.BlockSpec((TM, w), lambda i: (ctx_i(i), 0))
    lat_spec = pl.BlockSpec((TM, BRANCH_W), lambda i: (lat_i(i), 0))
    ctx_d_spec = pl.BlockSpec((D_HEADS, TM, HEAD_DIM), lambda i: (0, ctx_i(i), 0))
    lat_d_spec = pl.BlockSpec((D_HEADS, TM, HEAD_DIM), lambda i: (0, lat_i(i), 0))
    return pl.pallas_call(
        _merge_kernel,
        grid=(nt,),
        in_specs=[pl.BlockSpec((TM, D_MODEL), lambda i: (i, 0)),
                  pl.BlockSpec((None, 6, D_MODEL), lambda i: (_tile_cond(i, TM), 0, 0)),
                  pl.BlockSpec((1, D_MODEL), lambda i: (0, 0)),
                  ctx_spec(2 * BRANCH_W), ctx_spec(BRANCH_W), ctx_d_spec, ctx_d_spec,
                  lat_spec, lat_spec, lat_spec, lat_d_spec, lat_d_spec,
                  pl.BlockSpec((TM, BRANCH_W), lambda i: (i, Z_DG // BRANCH_W)),
                  pl.BlockSpec((1, HEAD_DIM), lambda i: (0, 0)),
                  pl.BlockSpec((N_BRANCH, D_MODEL, D_MODEL), lambda i: (0, 0, 0)),
                  pl.BlockSpec((N_BRANCH, BRANCH_W, D_MODEL), lambda i: (0, 0, 0)),
                  pl.BlockSpec((D_MODEL, D_MODEL), lambda i: (0, 0))],
        out_specs=pl.BlockSpec((TM, D_MODEL), lambda i: (i, 0)),
        out_shape=jax.ShapeDtypeStruct((N_TOK, D_MODEL), F32),
        compiler_params=_params(("arbitrary",)),
        name="merge_projection",
    )(x, mod, g.reshape(1, D_MODEL), *o_ctx, *o_lat, z, gd.reshape(1, HEAD_DIM), wgl, wbr, wout)


def _ffn_kernel(x_ref, mod_ref, g_ref, w1_ref, w3_ref, w2_ref, xo_ref):
    x = x_ref[...]
    h = (_rms(x, g_ref[...]) * (1.0 + mod_ref[4:5, :]) + mod_ref[3:4, :]).astype(BF16)
    a = jnp.dot(h, w1_ref[...], preferred_element_type=F32)
    b = jnp.dot(h, w3_ref[...], preferred_element_type=F32)
    hid = (a * jax.nn.sigmoid(a) * b).astype(BF16)
    f = jnp.dot(hid, w2_ref[...], preferred_element_type=F32)
    xo_ref[...] = x + mod_ref[5:6, :] * f


def dense_ffn(x, mod, g, w1, w3, w2):
    nt = N_TOK // TM
    const = lambda i: (0, 0)
    return pl.pallas_call(
        _ffn_kernel,
        grid=(nt,),
        in_specs=[pl.BlockSpec((TM, D_MODEL), lambda i: (i, 0)),
                  pl.BlockSpec((None, 6, D_MODEL), lambda i: (_tile_cond(i, TM), 0, 0)),
                  pl.BlockSpec((1, D_MODEL), const),
                  pl.BlockSpec((D_MODEL, D_FF), const, pipeline_mode=pl.Buffered(1)),
                  pl.BlockSpec((D_MODEL, D_FF), const, pipeline_mode=pl.Buffered(1)),
                  pl.BlockSpec((D_FF, D_MODEL), const, pipeline_mode=pl.Buffered(1))],
        out_specs=pl.BlockSpec((TM, D_MODEL), lambda i: (i, 0)),
        out_shape=jax.ShapeDtypeStruct((N_TOK, D_MODEL), F32),
        compiler_params=_params(("arbitrary",)),
        name="dense_ffn",
    )(x, mod, g.reshape(1, D_MODEL), w1, w3, w2)


MOE_T = 1024
MOE_R = 144
MOE_SR = 128
MOE_F = 1792
MOE_CAP = -(-MOE_T // MOE_R) * MOE_R


def _router_kernel(x_ref, mod_ref, g_ref, rw_ref, rb_ref, h_ref, pos_ref, wgt_ref):
    t = x_ref.shape[0]
    h = _rms(x_ref[...], g_ref[...]) * (1.0 + mod_ref[4:5, :]) + mod_ref[3:4, :]
    h_ref[...] = h.astype(BF16)
    logits = jnp.dot(h, rw_ref[...], preferred_element_type=F32, precision=lax.Precision.HIGHEST) + rb_ref[...]
    lt = logits.T[:N_EXPERTS, :]
    eidx = lax.broadcasted_iota(jnp.int32, lt.shape, 0)
    m1 = jnp.max(lt, axis=0, keepdims=True)
    i1 = jnp.min(jnp.where(lt == m1, eidx, N_EXPERTS), axis=0, keepdims=True)
    rest = jnp.where(eidx == i1, F32_MIN, lt)
    m2 = jnp.max(rest, axis=0, keepdims=True)
    i2 = jnp.min(jnp.where(rest == m2, eidx, N_EXPERTS), axis=0, keepdims=True)
    e2 = jnp.exp(m2 - m1)
    p1 = 1.0 / (1.0 + e2)
    p2 = e2 / (1.0 + e2)
    wgt_ref[...] = jnp.where(eidx == i1, p1, 0.0) + jnp.where(eidx == i2, p2, 0.0)
    routed = jnp.where(eidx == i1, 1.0, jnp.where(eidx == i2, 1.0, 0.0))
    s_id = lax.broadcasted_iota(jnp.int32, (t, t), 0)
    t_id = lax.broadcasted_iota(jnp.int32, (t, t), 1)
    before = jnp.where(s_id < t_id, 1.0, 0.0).astype(BF16)
    rank = jnp.dot(routed.astype(BF16), before, preferred_element_type=F32)
    pos_ref[...] = jnp.where(routed > 0.0, rank.astype(jnp.int32), -1)


def moe_router(x, mod, g, rw, rb):
    nt = N_TOK // MOE_T
    return pl.pallas_call(
        _router_kernel,
        grid=(nt,),
        in_specs=[pl.BlockSpec((MOE_T, D_MODEL), lambda i: (i, 0)),
                  pl.BlockSpec((None, 6, D_MODEL), lambda i: (_tile_cond(i, MOE_T), 0, 0)),
                  pl.BlockSpec((1, D_MODEL), lambda i: (0, 0)),
                  pl.BlockSpec((D_MODEL, 128), lambda i: (0, 0)),
                  pl.BlockSpec((1, 128), lambda i: (0, 0))],
        out_specs=[pl.BlockSpec((MOE_T, D_MODEL), lambda i: (i, 0)),
                   pl.BlockSpec((N_EXPERTS, MOE_T), lambda i: (0, i)),
                   pl.BlockSpec((N_EXPERTS, MOE_T), lambda i: (0, i))],
        out_shape=[jax.ShapeDtypeStruct((N_TOK, D_MODEL), BF16),
                   jax.ShapeDtypeStruct((N_EXPERTS, N_TOK), jnp.int32),
                   jax.ShapeDtypeStruct((N_EXPERTS, N_TOK), F32)],
        compiler_params=_params(("arbitrary",)),
        name="moe_router",
    )(x, mod, g.reshape(1, D_MODEL), rw, rb)


def _moe_sparse_kernel(h_ref, pos_ref, wgt_ref, w1_ref, w3_ref, w2_ref, y_ref, xg_scr, acc_scr, wr_scr):
    e = pl.program_id(1)
    f = pl.program_id(2)
    t = h_ref.shape[0]
    pos_e = pos_ref[pl.ds(e, 1), :]
    n_rows = jnp.max(pos_e) + 1
    n_blocks = (n_rows + MOE_R - 1) // MOE_R
    n_sc_blocks = (n_rows + MOE_SR - 1) // MOE_SR
    n_init_blocks = jnp.maximum(n_blocks, (n_sc_blocks * MOE_SR + MOE_R - 1) // MOE_R)
    row_id = lax.broadcasted_iota(jnp.int32, (MOE_R, t), 0)

    def block_rows(r):
        return pl.ds(pl.multiple_of(r * MOE_R, 16), MOE_R)

    def selects(r):
        return pos_e == row_id + r * MOE_R

    @pl.when((e == 0) & (f == 0))
    def _():
        y_ref[...] = jnp.zeros_like(y_ref)

    @pl.when(f == 0)
    def _():
        wgt_e = wgt_ref[pl.ds(e, 1), :]

        def gather(r, carry):
            sel = selects(r)
            xg = jnp.dot(jnp.where(sel, 1.0, 0.0).astype(BF16), h_ref[...], preferred_element_type=F32)
            xg_scr[block_rows(r), :] = xg.astype(BF16)
            w_rows = jnp.sum(jnp.where(sel, wgt_e, 0.0), axis=1, keepdims=True)
            wr_scr[block_rows(r), :] = jnp.broadcast_to(w_rows, (MOE_R, 128))
            acc_scr[block_rows(r), :] = jnp.zeros((MOE_R, D_MODEL), F32)
            return carry

        lax.fori_loop(0, n_init_blocks, gather, 0)

    def expert(r, carry):
        xg = xg_scr[block_rows(r), :]
        a = jnp.dot(xg, w1_ref[...], preferred_element_type=F32)
        b = jnp.dot(xg, w3_ref[...], preferred_element_type=F32)
        hid = (a * jax.nn.sigmoid(a) * b * wr_scr[block_rows(r), 0:1]).astype(BF16)
        acc_scr[block_rows(r), :] += jnp.dot(hid, w2_ref[...], preferred_element_type=F32)
        return carry

    lax.fori_loop(0, n_blocks, expert, 0)

    @pl.when(f == pl.num_programs(2) - 1)
    def _():
        sc_row_id = lax.broadcasted_iota(jnp.int32, (MOE_SR, t), 0)

        def scatter(r, carry):
            rows = pl.ds(pl.multiple_of(r * MOE_SR, MOE_SR), MOE_SR)
            onehot = jnp.where(pos_e == sc_row_id + r * MOE_SR, 1.0, 0.0).astype(BF16)
            hi, lo = _split_bf16(acc_scr[rows, :])
            y_ref[...] += lax.dot_general(jnp.concatenate([onehot, onehot], axis=0),
                                          jnp.concatenate([hi, lo], axis=0), _TN, preferred_element_type=F32)
            return carry

        lax.fori_loop(0, n_sc_blocks, scatter, 0)


def moe_experts(h2, pos, wgt, w1, w3, w2):
    nt = N_TOK // MOE_T
    nf = D_FF_EXPERT // MOE_F
    return pl.pallas_call(
        _moe_sparse_kernel,
        grid=(nt, N_EXPERTS, nf),
        in_specs=[pl.BlockSpec((MOE_T, D_MODEL), lambda i, e, f: (i, 0)),
                  pl.BlockSpec((N_EXPERTS, MOE_T), lambda i, e, f: (0, i)),
                  pl.BlockSpec((N_EXPERTS, MOE_T), lambda i, e, f: (0, i)),
                  pl.BlockSpec((None, D_MODEL, MOE_F), lambda i, e, f: (e, 0, f)),
                  pl.BlockSpec((None, D_MODEL, MOE_F), lambda i, e, f: (e, 0, f)),
                  pl.BlockSpec((None, MOE_F, D_MODEL), lambda i, e, f: (e, f, 0))],
        out_specs=pl.BlockSpec((MOE_T, D_MODEL), lambda i, e, f: (i, 0)),
        out_shape=jax.ShapeDtypeStruct((N_TOK, D_MODEL), F32),
        scratch_shapes=[pltpu.VMEM((MOE_CAP, D_MODEL), BF16),
                        pltpu.VMEM((MOE_CAP, D_MODEL), F32),
                        pltpu.VMEM((MOE_CAP, 128), F32)],
        compiler_params=_params(("arbitrary", "arbitrary", "arbitrary")),
        name="moe_experts",
    )(h2, pos, wgt, w1, w3, w2)


def _residual_norm_kernel(x_ref, y_ref, mod_ref, g_ref, o_ref):
    o_ref[...] = _rms(x_ref[...] + mod_ref[5:6, :] * y_ref[...], g_ref[...])


def residual_final_norm(x, y, mod, g, row0, n_rows):
    tm = 1024
    t0 = row0 // tm
    return pl.pallas_call(
        _residual_norm_kernel,
        grid=(n_rows // tm,),
        in_specs=[pl.BlockSpec((tm, D_MODEL), lambda i: (t0 + i, 0)),
                  pl.BlockSpec((tm, D_MODEL), lambda i: (t0 + i, 0)),
                  pl.BlockSpec((None, 6, D_MODEL), lambda i: (_tile_cond(t0 + i, tm), 0, 0)),
                  pl.BlockSpec((1, D_MODEL), lambda i: (0, 0))],
        out_specs=pl.BlockSpec((tm, D_MODEL), lambda i: (i, 0)),
        out_shape=jax.ShapeDtypeStruct((n_rows, D_MODEL), F32),
        compiler_params=_params(("arbitrary",)),
        name="residual_final_norm",
    )(x, y, mod, g.reshape(1, D_MODEL))


ATT_SCALE = HEAD_DIM ** -0.5
LOG2_E = 1.4426950408889634
_NT = (((1,), (1,)), ((), ()))


def _head_rms(x, g_row):
    outs = []
    for h in range(x.shape[1] // HEAD_DIM):
        xh = x[:, h * HEAD_DIM:(h + 1) * HEAD_DIM]
        outs.append(xh * lax.rsqrt(jnp.mean(xh * xh, axis=-1, keepdims=True) + EPS) * g_row)
    return jnp.concatenate(outs, axis=-1)


def _rope_apply(x, c, s):
    w = x.shape[-1]
    lane = lax.broadcasted_iota(jnp.int32, x.shape, 1)
    first_half = ((lane // (HEAD_DIM // 4)) % 2) == 0
    partner = jnp.where(first_half, pltpu.roll(x, w - HEAD_DIM // 4, 1), pltpu.roll(x, HEAD_DIM // 4, 1))
    return x * c + partner * s


def rope_lane_tables(L):
    rows = L // GRID_W
    row = jnp.repeat(jnp.arange(rows, dtype=F32), GRID_W)
    col = jnp.tile(jnp.arange(GRID_W, dtype=F32), rows)
    n_freq = HEAD_DIM // 4
    inv = ROPE_BASE ** (-jnp.arange(n_freq, dtype=F32) / n_freq)
    ang = jnp.stack([row, col], 0)[:, :, None] * inv
    cos, sin = jnp.cos(ang), jnp.sin(ang)
    c = jnp.concatenate([cos[0], cos[0], cos[1], cos[1]], axis=-1)
    s = jnp.concatenate([-sin[0], sin[0], -sin[1], sin[1]], axis=-1)
    return jnp.tile(c, (1, 4)), jnp.tile(s, (1, 4))


def _with_ones(v):
    ones = jnp.ones((v.shape[0], HEAD_DIM), BF16)
    parts = []
    for h in range(v.shape[1] // HEAD_DIM):
        parts += [v[:, h * HEAD_DIM:(h + 1) * HEAD_DIM].astype(BF16), ones]
    return jnp.concatenate(parts, axis=-1)


def _attend_heads(jobs):
    for job in jobs:
        q = (job['q'] * (ATT_SCALE * LOG2_E)).astype(BF16)
        job['s'] = lax.dot_general(q, job['k'], _NT, preferred_element_type=F32)
        if job.get('extra') is not None:
            job['s2'] = lax.dot_general(q, job['extra'][0], _NT, preferred_element_type=F32)
    outs = []
    for job in jobs:
        s, sink = job['s'], job.get('sink')
        if sink is not None:
            sink = sink * LOG2_E
        if job.get('mask') is not None:
            s = jnp.where(job['mask'], s, NEG_INF)
        m = jnp.max(s, axis=-1, keepdims=True)
        if 's2' in job:
            m = jnp.maximum(m, jnp.max(job['s2'], axis=-1, keepdims=True))
        if sink is not None:
            m = jnp.maximum(m, sink)
        o = jnp.dot(jnp.exp2(s - m).astype(BF16), job['v'], preferred_element_type=F32)
        if 's2' in job:
            o = o + jnp.dot(jnp.exp2(job['s2'] - m).astype(BF16), job['extra'][1], preferred_element_type=F32)
        den = o[:, HEAD_DIM:HEAD_DIM + 1]
        if sink is not None:
            den = den + jnp.exp2(sink - m)
        outs.append(o[:, :HEAD_DIM] / den)
    return outs


def _ctx_attn_kernel(sink_ref, z_ref, gq_ref, gk_ref, o_ref, bk_ref):
    z = z_ref[...]
    bq = _head_rms(z[:, 512:768], gq_ref[...])
    bk = _head_rms(z[:, 768:896], gk_ref[...])
    bk_ref[...] = bk
    groups = ((z[:, 0:256], z[:, 256:384], z[:, 384:512], True),
              (bq, bk, z[:, 896:1024], False))
    outs = []
    for q_all, k_all, v_all, use_sink in groups:
        k_all = k_all.astype(BF16)
        v_all = _with_ones(v_all)
        for hq in range(A_HEADS):
            kv = hq // (A_HEADS // A_KV_HEADS)
            outs.append(dict(q=q_all[:, hq * HEAD_DIM:(hq + 1) * HEAD_DIM],
                             k=k_all[:, kv * HEAD_DIM:(kv + 1) * HEAD_DIM],
                             v=v_all[:, kv * 2 * HEAD_DIM:(kv + 1) * 2 * HEAD_DIM],
                             sink=sink_ref[hq] if use_sink else None))
    o_ref[...] = jnp.concatenate(_attend_heads(outs), axis=-1).astype(o_ref.dtype)


def ctx_attention(z, sink, gq, gk):
    return pl.pallas_call(
        _ctx_attn_kernel,
        grid=(BATCH,),
        in_specs=[pl.BlockSpec(memory_space=pltpu.SMEM),
                  pl.BlockSpec((SEQ, 1024), lambda b: (b, 0)),
                  pl.BlockSpec((1, HEAD_DIM), lambda b: (0, 0)),
                  pl.BlockSpec((1, HEAD_DIM), lambda b: (0, 0))],
        out_specs=[pl.BlockSpec((SEQ, 512), lambda b: (b, 0)),
                   pl.BlockSpec((SEQ, 128), lambda b: (b, 0))],
        out_shape=[jax.ShapeDtypeStruct((CTX_TOK, 512), BF16),
                   jax.ShapeDtypeStruct((CTX_TOK, 128), F32)],
        compiler_params=_params(("arbitrary",)),
        name="ctx_attention",
    )(sink, z, gq.reshape(1, HEAD_DIM), gk.reshape(1, HEAD_DIM))


LB_TQ = 256


def _lat_b_kernel(zq_ref, zkv_ref, ck_ref, cv_ref, cq_ref, sq_ref, ckk_ref, skk_ref, gq_ref, gk_ref,
                  o_ref, k_scr, v_scr):
    L = zkv_ref.shape[0]

    @pl.when(pl.program_id(1) == 0)
    def _():
        kv = zkv_ref[...]
        bk = _rope_apply(_head_rms(kv[:, :128], gk_ref[...]), ckk_ref[...], skk_ref[...])
        k_scr[0:L, :] = bk.astype(BF16)
        k_scr[L:L + PAST_LEN, :] = ck_ref[...].astype(BF16)
        v_scr[0:L, :] = _with_ones(kv[:, 128:])
        v_scr[L:L + PAST_LEN, :] = _with_ones(cv_ref[...])

    q = _rope_apply(_head_rms(zq_ref[...], gq_ref[...]), cq_ref[...], sq_ref[...])
    outs = []
    for hq in range(B_HEADS):
        kv = hq // (B_HEADS // B_KV_HEADS)
        sl = slice(kv * HEAD_DIM, (kv + 1) * HEAD_DIM)
        outs.append(dict(q=q[:, hq * HEAD_DIM:(hq + 1) * HEAD_DIM], k=k_scr[:, sl],
                         v=v_scr[:, kv * 2 * HEAD_DIM:(kv + 1) * 2 * HEAD_DIM]))
    o_ref[...] = jnp.concatenate(_attend_heads(outs), axis=-1).astype(o_ref.dtype)


def latent_attention_b(z, row0, nb, L, cache_k, cache_v, rope_c, rope_s, gq, gk):
    nq = L // LB_TQ
    return pl.pallas_call(
        _lat_b_kernel,
        grid=(nb, nq),
        in_specs=[pl.BlockSpec((LB_TQ, 256), lambda b, i: (row0 // LB_TQ + b * nq + i, 2)),
                  pl.BlockSpec((L, 256), lambda b, i: (row0 // L + b, 3)),
                  pl.BlockSpec((None, PAST_LEN, 128), lambda b, i: (b, 0, 0)),
                  pl.BlockSpec((None, PAST_LEN, 128), lambda b, i: (b, 0, 0)),
                  pl.BlockSpec((LB_TQ, 256), lambda b, i: (i, 0)),
                  pl.BlockSpec((LB_TQ, 256), lambda b, i: (i, 0)),
                  pl.BlockSpec((L, 128), lambda b, i: (0, 0)),
                  pl.BlockSpec((L, 128), lambda b, i: (0, 0)),
                  pl.BlockSpec((1, HEAD_DIM), lambda b, i: (0, 0)),
                  pl.BlockSpec((1, HEAD_DIM), lambda b, i: (0, 0))],
        out_specs=pl.BlockSpec((LB_TQ, 256), lambda b, i: (b * nq + i, 0)),
        out_shape=jax.ShapeDtypeStruct((nb * L, 256), BF16),
        scratch_shapes=[pltpu.VMEM((L + PAST_LEN, 128), BF16),
                        pltpu.VMEM((L + PAST_LEN, 256), BF16)],
        compiler_params=_params(("arbitrary", "arbitrary")),
        name="latent_attention_b",
    )(z, z, cache_k, cache_v, rope_c, rope_s, rope_c, rope_s, gq.reshape(1, HEAD_DIM), gk.reshape(1, HEAD_DIM))


def _lat_a_kernel(sink_ref, zq_ref, zkv_ref, ck_ref, cv_ref, cq_ref, sq_ref, ckk_ref, skk_ref,
                  o_ref, k_scr, v_scr, ck_scr, cv_scr):
    L = zkv_ref.shape[0]
    i = pl.program_id(1)

    @pl.when(i == 0)
    def _():
        kv = zkv_ref[...]
        k_scr[0:BLOCK, :] = jnp.zeros((BLOCK, 128), BF16)
        v_scr[0:BLOCK, :] = jnp.zeros((BLOCK, 256), BF16)
        k_scr[BLOCK:BLOCK + L, :] = _rope_apply(kv[:, :128], ckk_ref[...], skk_ref[...]).astype(BF16)
        v_scr[BLOCK:BLOCK + L, :] = _with_ones(kv[:, 128:])
        k_scr[BLOCK + L:2 * BLOCK + L, :] = jnp.zeros((BLOCK, 128), BF16)
        v_scr[BLOCK + L:2 * BLOCK + L, :] = jnp.zeros((BLOCK, 256), BF16)
        ck_scr[...] = ck_ref[...].astype(BF16)
        cv_scr[...] = _with_ones(cv_ref[...])

    q = _rope_apply(zq_ref[...], cq_ref[...], sq_ref[...])
    start = pl.multiple_of(i * BLOCK, BLOCK)
    kband = k_scr[pl.ds(start, 3 * BLOCK), :]
    vband = v_scr[pl.ds(start, 3 * BLOCK), :]
    r = lax.broadcasted_iota(jnp.int32, (BLOCK, 3 * BLOCK), 0)
    cidx = lax.broadcasted_iota(jnp.int32, (BLOCK, 3 * BLOCK), 1)
    kpos = i * BLOCK - BLOCK + cidx
    mask = (jnp.abs(cidx - BLOCK - r) <= WINDOW) & (kpos >= 0) & (kpos < L)
    outs = []
    for hq in range(A_HEADS):
        kv = hq // (A_HEADS // A_KV_HEADS)
        sl = slice(kv * HEAD_DIM, (kv + 1) * HEAD_DIM)
        sv = slice(kv * 2 * HEAD_DIM, (kv + 1) * 2 * HEAD_DIM)
        outs.append(dict(q=q[:, hq * HEAD_DIM:(hq + 1) * HEAD_DIM], k=kband[:, sl], v=vband[:, sv],
                         extra=(ck_scr[:, sl], cv_scr[:, sv]), sink=sink_ref[hq], mask=mask))
    o_ref[...] = jnp.concatenate(_attend_heads(outs), axis=-1).astype(o_ref.dtype)


def latent_attention_a(z, row0, nb, L, sink, cache_k, cache_v, rope_c, rope_s):
    nq = L // BLOCK
    return pl.pallas_call(
        _lat_a_kernel,
        grid=(nb, nq),
        in_specs=[pl.BlockSpec(memory_space=pltpu.SMEM),
                  pl.BlockSpec((BLOCK, 256), lambda b, i: (row0 // BLOCK + b * nq + i, 0)),
                  pl.BlockSpec((L, 256), lambda b, i: (row0 // L + b, 1)),
                  pl.BlockSpec((None, PAST_LEN, 128), lambda b, i: (b, 0, 0)),
                  pl.BlockSpec((None, PAST_LEN, 128), lambda b, i: (b, 0, 0)),
                  pl.BlockSpec((BLOCK, 256), lambda b, i: (i, 0)),
                  pl.BlockSpec((BLOCK, 256), lambda b, i: (i, 0)),
                  pl.BlockSpec((L, 128), lambda b, i: (0, 0)),
                  pl.BlockSpec((L, 128), lambda b, i: (0, 0))],
        out_specs=pl.BlockSpec((BLOCK, 256), lambda b, i: (b * nq + i, 0)),
        out_shape=jax.ShapeDtypeStruct((nb * L, 256), BF16),
        scratch_shapes=[pltpu.VMEM((L + 2 * BLOCK, 128), BF16),
                        pltpu.VMEM((L + 2 * BLOCK, 256), BF16),
                        pltpu.VMEM((PAST_LEN, 128), BF16),
                        pltpu.VMEM((PAST_LEN, 256), BF16)],
        compiler_params=_params(("arbitrary", "arbitrary")),
        name="latent_attention_a",
    )(sink, z, z, cache_k, cache_v, rope_c, rope_s, rope_c, rope_s)


_TN = (((0,), (0,)), ((), ()))
HG_GROUP = 8


def _hgrn_kernel(zq_ref, zf_ref, zi_ref, zg_ref, lb_ref, gn_ref, s0_ref, o_ref, sT_ref,
                 of_scr, ob_scr, g_scr, k_scr, qin_scr, kin_scr, v_scr, S_scr, *, tt):
    d = pl.program_id(1)
    j = pl.program_id(2)
    n_t = pl.num_programs(2)
    C = HGRN_CHUNK
    n_c = tt // C

    @pl.when(j == 0)
    def _():
        S_scr[...] = s0_ref[...]

    lb = lb_ref[...]
    sg = jax.nn.sigmoid(zf_ref[...])
    logf = jnp.log(lb + (1.0 - lb) * sg)
    k = (1.0 - lb) * (1.0 - sg)
    k_scr[...] = k
    v_scr[...] = zi_ref[...].astype(BF16)
    in_chunk = lax.broadcasted_iota(jnp.int32, (tt, C_HEADS * HEAD_DIM), 0) % C
    row = lax.broadcasted_iota(jnp.int32, (C, C), 0)
    col = lax.broadcasted_iota(jnp.int32, (C, C), 1)
    heads = [slice(h * HEAD_DIM, (h + 1) * HEAD_DIM) for h in range(C_HEADS)]

    def run(reverse, tile):
        G = logf
        step = 1
        while step < C:
            if reverse:
                G = G + jnp.where(in_chunk < C - step, pltpu.roll(G, tt - step, 0), 0.0)
            else:
                G = G + jnp.where(in_chunk >= step, pltpu.roll(G, step, 0), 0.0)
            step *= 2
        g_scr[...] = G
        qin_scr[...] = (zq_ref[...] * jnp.exp(G)).astype(BF16)
        kin_scr[...] = (k * jnp.exp(-G)).astype(BF16)
        tri = (row <= col) if reverse else (row >= col)

        def body(gi, carry):
            chunks = []
            for g in range(HG_GROUP):
                ci = gi * HG_GROUP + g
                c = (n_c - 1 - ci) if reverse else ci
                r0 = pl.multiple_of(c * C, C)
                rows = pl.ds(r0, C)
                G_c = g_scr[rows, :]
                G_end = G_c[0:1, :] if reverse else G_c[C - 1:C, :]
                vc = v_scr[rows, :]
                q_in = qin_scr[rows, :]
                k_in = kin_scr[rows, :]
                k_out = (k_scr[rows, :] * jnp.exp(G_end - G_c)).astype(BF16)
                chunks.append(dict(
                    r0=r0, rows=rows, vc=vc, q_in=q_in, decay=jnp.exp(G_end),
                    a=[lax.dot_general(q_in[:, sl], k_in[:, sl], _NT, preferred_element_type=F32) for sl in heads],
                    kv=[lax.dot_general(vc[:, sl], k_out[:, sl], _TN, preferred_element_type=F32) for sl in heads]))
            s_cur = [S_scr[h] for h in range(C_HEADS)]
            for ch in chunks:
                ch['qs'] = [lax.dot_general(ch['q_in'][:, sl], s_cur[h].astype(BF16), _NT,
                                            preferred_element_type=F32) for h, sl in enumerate(heads)]
                s_cur = [s_cur[h] * ch['decay'][:, sl] + ch['kv'][h] for h, sl in enumerate(heads)]
            for h in range(C_HEADS):
                S_scr[h] = s_cur[h]
            for ch in chunks:
                o_c = jnp.concatenate(
                    [jnp.dot(jnp.where(tri, ch['a'][h], 0.0).astype(BF16), ch['vc'][:, sl],
                             preferred_element_type=F32) + ch['qs'][h] for h, sl in enumerate(heads)], axis=-1)
                if reverse:
                    ob_scr[ch['rows'], :] = o_c
                else:
                    of_scr[pl.ds(pl.multiple_of(tile * tt, tt) + ch['r0'], C), :] = o_c
            return carry

        lax.fori_loop(0, n_c // HG_GROUP, body, 0)

    @pl.when(d == 0)
    def _():
        run(False, j)

    @pl.when(d == 1)
    def _():
        tile = n_t - 1 - j
        run(True, tile)
        o = of_scr[pl.ds(pl.multiple_of(tile * tt, tt), tt), :] + ob_scr[...]
        g = zg_ref[...]
        o_ref[...] = (_head_rms(o, gn_ref[...]) * (g * jax.nn.sigmoid(g))).astype(o_ref.dtype)

    @pl.when(j == n_t - 1)
    def _():
        sT_ref[...] = S_scr[...]


def hgrn_mixer(z, row0, nb, L, lb, gn, s0_t):
    tt = min(L, 512)
    n_t = L // tt
    rb = row0 // tt

    def tile(d, j):
        return jnp.where(d == 0, j, n_t - 1 - j)

    def late(d, j):
        return jnp.where(d == 0, n_t - 1, n_t - 1 - j)

    st_spec = pl.BlockSpec((None, None, C_HEADS, HEAD_DIM, HEAD_DIM), lambda b, d, j: (b, d, 0, 0, 0))
    return pl.pallas_call(
        functools.partial(_hgrn_kernel, tt=tt),
        grid=(nb, 2, n_t),
        in_specs=[pl.BlockSpec((tt, 256), lambda b, d, j: (rb + b * n_t + tile(d, j), 4)),
                  pl.BlockSpec((tt, 256), lambda b, d, j: (rb + b * n_t + tile(d, j), 5 + d)),
                  pl.BlockSpec((tt, 256), lambda b, d, j: (rb + b * n_t + tile(d, j), 7)),
                  pl.BlockSpec((tt, 256), lambda b, d, j: (rb + b * n_t + late(d, j), 8)),
                  pl.BlockSpec((1, 256), lambda b, d, j: (0, 0)),
                  pl.BlockSpec((1, HEAD_DIM), lambda b, d, j: (0, 0)),
                  st_spec],
        out_specs=[pl.BlockSpec((tt, 256), lambda b, d, j: (b * n_t + late(d, j), 0)),
                   st_spec],
        out_shape=[jax.ShapeDtypeStruct((nb * L, 256), BF16),
                   jax.ShapeDtypeStruct((nb, 2, C_HEADS, HEAD_DIM, HEAD_DIM), F32)],
        scratch_shapes=[pltpu.VMEM((L, 256), F32),
                        pltpu.VMEM((tt, 256), F32),
                        pltpu.VMEM((tt, 256), F32),
                        pltpu.VMEM((tt, 256), F32),
                        pltpu.VMEM((tt, 256), BF16),
                        pltpu.VMEM((tt, 256), BF16),
                        pltpu.VMEM((tt, 256), BF16),
                        pltpu.VMEM((C_HEADS, HEAD_DIM, HEAD_DIM), F32)],
        compiler_params=_params(("arbitrary", "arbitrary", "arbitrary")),
        name="hgrn_mixer",
    )(z, z, z, z, lb.reshape(1, 256), gn.reshape(1, HEAD_DIM), s0_t)


DL_C = DELTA_CHUNK
DL_PREP_TT = 256
DL_HALO = 8
DL_CHUNK_TT = 512
N_QKV_HEADS = 3 * D_HEADS
DL_GROUP = 2


def _delta_prep_kernel(x_ref, xp_ref, xn_ref, zab_ref, cw_ref, na_ref, dtb_ref, qkv_ref, gate_ref, xs_scr):
    tt = x_ref.shape[0]
    row = pl.program_id(0) * tt
    lat = row - CTX_TOK
    first = jnp.where(row < CTX_TOK, True, lat % DEC_SEQ == 0)
    last = jnp.where(row < CTX_TOK, True, (lat + tt) % DEC_SEQ == 0)
    xs_scr[DL_HALO:DL_HALO + tt, :] = x_ref[...]
    xs_scr[0:DL_HALO, :] = jnp.where(first, 0.0, xp_ref[...])
    xs_scr[DL_HALO + tt:2 * DL_HALO + tt, :] = jnp.where(last, 0.0, xn_ref[...])
    pad = (CONV_K - 1) // 2
    y = None
    for t in range(CONV_K):
        term = xs_scr[pl.ds(DL_HALO - pad + t, tt), :] * cw_ref[t:t + 1, :]
        y = term if y is None else y + term
    y = y * jax.nn.sigmoid(y)
    for idx in range(N_QKV_HEADS):
        xh = y[:, idx * HEAD_DIM:(idx + 1) * HEAD_DIM]
        if idx < 2 * D_HEADS:
            xh = xh * lax.rsqrt(jnp.sum(xh * xh, axis=-1, keepdims=True) + EPS)
        if idx < D_HEADS:
            xh = xh * ATT_SCALE
        qkv_ref[idx] = xh
    zab = zab_ref[...]
    lane = lax.broadcasted_iota(jnp.int32, zab.shape, 1)
    t_ = zab + dtb_ref[...]
    softplus = jnp.maximum(t_, 0.0) + jnp.log(1.0 + jnp.exp(-jnp.abs(t_)))
    gate_ref[...] = jnp.where(lane < 2 * D_HEADS, na_ref[...] * softplus, jax.nn.sigmoid(zab))


def delta_prep(z, conv_w, a_log, dt_bias):
    tt = DL_PREP_TT
    hb = tt // DL_HALO
    n_hb = N_TOK // DL_HALO
    pad8 = lambda v: jnp.concatenate([v.reshape(1, 2 * D_HEADS), jnp.zeros((1, 128 - 2 * D_HEADS), F32)], axis=1)
    return pl.pallas_call(
        _delta_prep_kernel,
        grid=(N_TOK // tt,),
        in_specs=[pl.BlockSpec((tt, 768), lambda i: (i, 3)),
                  pl.BlockSpec((DL_HALO, 768), lambda i: (jnp.maximum(i * hb - 1, 0), 3)),
                  pl.BlockSpec((DL_HALO, 768), lambda i: (jnp.minimum((i + 1) * hb, n_hb - 1), 3)),
                  pl.BlockSpec((tt, 128), lambda i: (i, Z_DAB // 128)),
                  pl.BlockSpec((CONV_K, 768), lambda i: (0, 0)),
                  pl.BlockSpec((1, 128), lambda i: (0, 0)),
                  pl.BlockSpec((1, 128), lambda i: (0, 0))],
        out_specs=[pl.BlockSpec((N_QKV_HEADS, tt, HEAD_DIM), lambda i: (0, i, 0)),
                   pl.BlockSpec((tt, 128), lambda i: (i, 0))],
        out_shape=[jax.ShapeDtypeStruct((N_QKV_HEADS, N_TOK, HEAD_DIM), F32),
                   jax.ShapeDtypeStruct((N_TOK, 128), F32)],
        scratch_shapes=[pltpu.VMEM((tt + 2 * DL_HALO, 768), F32)],
        compiler_params=_params(("arbitrary",)),
        name="delta_prep",
    )(z, z, z, z, conv_w, pad8(-jnp.exp(a_log)), pad8(dt_bias))


def _split_bf16(a):
    hi = a.astype(BF16)
    return hi, (a - hi.astype(F32)).astype(BF16)


def _dot_hl(a_parts, b_parts):
    (a_hi, a_lo), (b_hi, b_lo) = a_parts, b_parts
    m = a_hi.shape[0]
    r = jnp.dot(jnp.concatenate([a_hi, a_lo], axis=0), b_hi, preferred_element_type=F32)
    return r[:m] + r[m:] + jnp.dot(a_hi, b_lo, preferred_element_type=F32)


def _delta_chunk_kernel(qkv_ref, gate_ref, u2_ref, wq_ref, ak_ref):
    C = DL_C
    n_c = gate_ref.shape[0] // C
    row = lax.broadcasted_iota(jnp.int32, (C, C), 0)
    col = lax.broadcasted_iota(jnp.int32, (C, C), 1)
    eye = (row == col).astype(F32)
    t_idx = lax.broadcasted_iota(jnp.int32, (C, 128), 0)

    def chunk_chains(c):
        r0 = pl.multiple_of(c * C, C)
        ga = gate_ref[pl.ds(r0, C), :]
        chains = []
        for d in range(2):
            incl = (row >= col) if d == 0 else (row <= col)
            strict = (row > col) if d == 0 else (row < col)
            g_all = ga
            step = 1
            while step < C:
                if d == 0:
                    g_all = g_all + jnp.where(t_idx >= step, pltpu.roll(g_all, step, 0), 0.0)
                else:
                    g_all = g_all + jnp.where(t_idx < C - step, pltpu.roll(g_all, C - step, 0), 0.0)
                step *= 2
            g_all_t = g_all.T
            for h in range(D_HEADS):
                ci = d * D_HEADS + h
                q = qkv_ref[h, pl.ds(r0, C), :]
                k = qkv_ref[D_HEADS + h, pl.ds(r0, C), :]
                v = qkv_ref[2 * D_HEADS + h, pl.ds(r0, C), :]
                g_col = g_all[:, ci:ci + 1]
                g_row = g_all_t[ci:ci + 1, :]
                beta = ga[:, 2 * D_HEADS + ci:2 * D_HEADS + ci + 1]
                g_end = g_col[C - 1:C, :] if d == 0 else g_col[0:1, :]
                kb = k * beta
                eg = jnp.exp(g_col)
                decay = jnp.where(incl, jnp.exp(jnp.where(incl, g_col - g_row, 0.0)), 0.0)
                kq = jnp.concatenate([kb, q], axis=0).astype(BF16)
                chains.append(dict(
                    strict=strict, decay=decay, qg=q * eg, g_end=g_end,
                    r=lax.dot_general(kq, k.astype(BF16), _NT, preferred_element_type=F32),
                    rhs=jnp.concatenate([v * beta, kb * eg], axis=1).astype(BF16),
                    ke_t=(k * jnp.exp(g_end - g_col)).T))
        return r0, chains

    def body(gi, carry):
        groups = [(gi * DL_GROUP + cc,) + chunk_chains(gi * DL_GROUP + cc) for cc in range(DL_GROUP)]
        chains = [ch for _, _, chs in groups for ch in chs]
        for ch in chains:
            ch['p'] = -jnp.where(ch['strict'], ch['r'][:C] * ch['decay'], 0.0)
            ch['t'] = eye + ch['p']
        for _ in range(5):
            for ch in chains:
                parts = _split_bf16(ch['p'])
                ch['p'] = _dot_hl(parts, parts)
            for ch in chains:
                ch['t'] = ch['t'] + _dot_hl(_split_bf16(ch['t']), _split_bf16(ch['p']))
        for ch in chains:
            ch['uw'] = jnp.dot(ch['t'].astype(BF16), ch['rhs'], preferred_element_type=F32)
        pack = lambda xs: jnp.stack(xs).reshape((2, D_HEADS) + xs[0].shape)
        for c, r0, chs in groups:
            u2 = [jnp.concatenate([ch['uw'][:, :C], jnp.broadcast_to(jnp.exp(ch['g_end']), (C, C))], axis=1)
                  for ch in chs]
            wq = [jnp.concatenate([ch['uw'][:, C:], ch['qg']], axis=0).astype(BF16) for ch in chs]
            ak = [jnp.concatenate([ch['r'][C:] * ch['decay'], ch['ke_t']], axis=0).astype(BF16) for ch in chs]
            u2_ref[:, :, pl.ds(r0, C), :] = pack(u2)
            wq_ref[:, :, c] = pack(wq)
            ak_ref[:, :, c] = pack(ak)
        return carry

    lax.fori_loop(0, n_c // DL_GROUP, body, 0)


def delta_chunks(qkv, gates):
    tt = DL_CHUNK_TT
    n_c = tt // DL_C
    return pl.pallas_call(
        _delta_chunk_kernel,
        grid=(N_TOK // tt,),
        in_specs=[pl.BlockSpec((N_QKV_HEADS, tt, HEAD_DIM), lambda i: (0, i, 0)),
                  pl.BlockSpec((tt, 128), lambda i: (i, 0))],
        out_specs=[pl.BlockSpec((2, D_HEADS, tt, 128), lambda i: (0, 0, i, 0)),
                   pl.BlockSpec((2, D_HEADS, n_c, 2 * DL_C, HEAD_DIM), lambda i: (0, 0, i, 0, 0)),
                   pl.BlockSpec((2, D_HEADS, n_c, 2 * DL_C, HEAD_DIM), lambda i: (0, 0, i, 0, 0))],
        out_shape=[jax.ShapeDtypeStruct((2, D_HEADS, N_TOK, 128), F32),
                   jax.ShapeDtypeStruct((2, D_HEADS, N_TOK // DL_C, 2 * DL_C, HEAD_DIM), BF16),
                   jax.ShapeDtypeStruct((2, D_HEADS, N_TOK // DL_C, 2 * DL_C, HEAD_DIM), BF16)],
        compiler_params=_params(("arbitrary",)),
        name="delta_chunks",
    )(qkv, gates)


def _delta_scan_kernel(u2f_ref, wqf_ref, akf_ref, u2b_ref, wqb_ref, akb_ref, s0_ref,
                       of_ref, ob_ref, s_ref, s_scr):
    j = pl.program_id(1)
    C = DL_C
    n_c = wqf_ref.shape[1]

    @pl.when(j == 0)
    def _():
        s_scr[...] = s0_ref[...]

    def body(ci, carry):
        chains = []
        for d, (u2_ref, wq_ref, ak_ref, o_ref) in enumerate(((u2f_ref, wqf_ref, akf_ref, of_ref),
                                                            (u2b_ref, wqb_ref, akb_ref, ob_ref))):
            c = ci if d == 0 else n_c - 1 - ci
            r0 = pl.multiple_of(c * C, C)
            for h in range(D_HEADS):
                s = s_scr[d, h]
                chains.append(dict(d=d, h=h, c=c, r0=r0, s=s, ak_ref=ak_ref, o_ref=o_ref,
                                   u2=u2_ref[h, pl.ds(r0, C), :],
                                   r1=jnp.dot(wq_ref[h, c], s.astype(BF16),
                                              preferred_element_type=F32)))
        for ch in chains:
            v_new = ch['u2'][:, :C] - ch['r1'][:C]
            ch['r2'] = jnp.dot(ch['ak_ref'][ch['h'], ch['c']], v_new.astype(BF16),
                               preferred_element_type=F32)
        for ch in chains:
            ch['o_ref'][ch['h'], pl.ds(ch['r0'], C), :] = ch['r1'][C:] + ch['r2'][:C]
            s_scr[ch['d'], ch['h']] = ch['s'] * ch['u2'][0:1, C:] + ch['r2'][C:]
        return carry

    lax.fori_loop(0, n_c, body, 0)

    @pl.when(j == pl.num_programs(1) - 1)
    def _():
        s_ref[...] = s_scr[...]


def delta_scan(u2, wq, ak, row0, nb, L, s0):
    tt = min(L, 512)
    n_t = L // tt
    n_c = tt // DL_C
    rb = row0 // tt
    fwd = lambda b, j: rb + b * n_t + j
    bwd = lambda b, j: rb + b * n_t + (n_t - 1 - j)
    u_spec = lambda d, f: pl.BlockSpec((None, D_HEADS, tt, 128), lambda b, j: (d, 0, f(b, j), 0))
    c_spec = lambda d, f: pl.BlockSpec((None, D_HEADS, n_c, 2 * DL_C, HEAD_DIM), lambda b, j: (d, 0, f(b, j), 0, 0))
    st_spec = pl.BlockSpec((None, 2, D_HEADS, HEAD_DIM, HEAD_DIM), lambda b, j: (b, 0, 0, 0, 0))
    return pl.pallas_call(
        _delta_scan_kernel,
        grid=(nb, n_t),
        in_specs=[u_spec(0, fwd), c_spec(0, fwd), c_spec(0, fwd),
                  u_spec(1, bwd), c_spec(1, bwd), c_spec(1, bwd), st_spec],
        out_specs=[pl.BlockSpec((D_HEADS, tt, HEAD_DIM), lambda b, j: (0, b * n_t + j, 0)),
                   pl.BlockSpec((D_HEADS, tt, HEAD_DIM), lambda b, j: (0, b * n_t + (n_t - 1 - j), 0)),
                   st_spec],
        out_shape=[jax.ShapeDtypeStruct((D_HEADS, nb * L, HEAD_DIM), F32),
                   jax.ShapeDtypeStruct((D_HEADS, nb * L, HEAD_DIM), F32),
                   jax.ShapeDtypeStruct((nb, 2, D_HEADS, HEAD_DIM, HEAD_DIM), F32)],
        scratch_shapes=[pltpu.VMEM((2, D_HEADS, HEAD_DIM, HEAD_DIM), F32)],
        compiler_params=_params(("arbitrary", "arbitrary")),
        name="delta_scan",
    )(u2, wq, ak, u2, wq, ak, s0)


def kernel(x_prompt, x_sample, cache_attn_a_k, cache_attn_a_v, cache_attn_b_k, cache_attn_b_v,
           state_hgrn, state_delta, c, c_ctx, norm1_g, norm2_g, w_ada, b_ada, w_in, a_sink,
           b_qnorm_g, b_knorm_g, c_lb, c_onorm_g, d_conv, d_a_log, d_dt_bias, d_onorm_g,
           w_branch, w_out, ffn_w1, ffn_w3, ffn_w2, router_w, router_b, moe_w1, moe_w3, moe_w2,
           final_norm_g):
    cum = jnp.cumsum(jax.nn.softmax(c_lb, axis=0), axis=0)
    lower_bounds = cum - cum[:1]

    x = jnp.concatenate([x_prompt.reshape(CTX_TOK, D_MODEL), x_sample.reshape(LAT_TOK, D_MODEL)], axis=0)
    cond = jnp.concatenate([c_ctx[None, :], c, jnp.zeros((16 - N_COND, D_MODEL), F32)], axis=0)

    rope_c, rope_s = rope_lane_tables(DEC_SEQ)
    caches = []
    for l in range(DEPTH):
        mod = ada_modulation(cond, w_ada[l], b_ada[l])[:N_COND].reshape(N_COND, 6, D_MODEL)
        w_mix = jnp.concatenate([w_in[l][:, :Z_MAIN], w_in[l][:, Z_MAIN + 16:W_IN_MIX],
                                 w_in[l][:, Z_MAIN:Z_MAIN + 16], jnp.zeros((D_MODEL, 128 - 16), F32)],
                                axis=1).astype(BF16)
        w_gl = w_in[l][:, W_IN_MIX:].reshape(D_MODEL, N_BRANCH, D_MODEL).transpose(1, 0, 2).astype(BF16)
        z = input_projection(x, mod, norm1_g[l], w_mix)
        kv2 = lambda t: t.reshape(DEC_BATCH, PAST_LEN, 128)
        o_ab_ctx, bk_ctx = ctx_attention(z, a_sink[l], b_qnorm_g[l], b_knorm_g[l])
        o_a_lat = latent_attention_a(z, CTX_TOK, DEC_BATCH, DEC_SEQ, a_sink[l], kv2(cache_attn_a_k[:, l]),
                                     kv2(cache_attn_a_v[:, l]), rope_c, rope_s)
        o_b_lat = latent_attention_b(z, CTX_TOK, DEC_BATCH, DEC_SEQ, kv2(cache_attn_b_k[:, l]),
                                     kv2(cache_attn_b_v[:, l]), rope_c, rope_s, b_qnorm_g[l], b_knorm_g[l])
        o_c_ctx, sc_t = hgrn_mixer(z, 0, BATCH, SEQ, lower_bounds[l], c_onorm_g[l],
                                   jnp.zeros((BATCH, 2, C_HEADS, HEAD_DIM, HEAD_DIM), F32))
        o_c_lat, _ = hgrn_mixer(z, CTX_TOK, DEC_BATCH, DEC_SEQ, lower_bounds[l], c_onorm_g[l],
                                jnp.swapaxes(state_hgrn[:, l], -1, -2))

        qkv, gates = delta_prep(z, d_conv[l], d_a_log[l], d_dt_bias[l])
        u2, wq, ak = delta_chunks(qkv, gates)
        of_ctx, ob_ctx, sd = delta_scan(u2, wq, ak, 0, BATCH, SEQ,
                                        jnp.zeros((BATCH, 2, D_HEADS, HEAD_DIM, HEAD_DIM), F32))
        of_lat, ob_lat, _ = delta_scan(u2, wq, ak, CTX_TOK, DEC_BATCH, DEC_SEQ, state_delta[:, l])
        kvh = lambda t: t.reshape(BATCH, SEQ, 2, HEAD_DIM)
        caches.append((kvh(z[:CTX_TOK, 256:384]), kvh(z[:CTX_TOK, 384:512]), kvh(bk_ctx), kvh(z[:CTX_TOK, 896:1024]),
                       jnp.swapaxes(sc_t, -1, -2), sd))
        x = merge_projection(x, mod, norm1_g[l], (o_ab_ctx, o_c_ctx, of_ctx, ob_ctx),
                             (o_a_lat, o_b_lat, o_c_lat, of_lat, ob_lat), z, d_onorm_g[l],
                             w_gl, w_branch[l].astype(BF16), w_out[l].astype(BF16))
        j = l // 2
        if l % 2 == 0:
            x = dense_ffn(x, mod, norm2_g[l], ffn_w1[j].astype(BF16), ffn_w3[j].astype(BF16),
                          ffn_w2[j].astype(BF16))
        else:
            assert l == DEPTH - 1, "the expert layer's residual is fused with the final norm"
            rw = jnp.concatenate([router_w[j], jnp.zeros((D_MODEL, 128 - N_EXPERTS), F32)], axis=1)
            rb = jnp.concatenate([router_b[j], jnp.zeros((128 - N_EXPERTS,), F32)])[None, :]
            h2, pos, wgt = moe_router(x, mod, norm2_g[l], rw, rb)
            f = moe_experts(h2, pos, wgt, moe_w1[j].astype(BF16), moe_w3[j].astype(BF16), moe_w2[j].astype(BF16))
            y_prompt = residual_final_norm(x, f, mod, final_norm_g, 0, CTX_TOK).reshape(BATCH, SEQ, D_MODEL)
            y_sample = residual_final_norm(x, f, mod, final_norm_g, CTX_TOK, LAT_TOK).reshape(DEC_BATCH, DEC_SEQ, D_MODEL)

    stack = lambda idx: jnp.stack([caches[l][idx] for l in range(DEPTH)], axis=1)
    return (y_prompt, y_sample, stack(0), stack(1), stack(2), stack(3), stack(4), stack(5))
```

```python
import functools

import jax
import jax.numpy as jnp
import numpy as np
from jax import lax
from jax.experimental import pallas as pl
from jax.experimental.pallas import tpu as pltpu

F32 = jnp.float32
BF16 = jnp.bfloat16

D_MODEL = 1024
BATCH = 32
SEQ = 256
DEPTH = 2
DEC_BATCH = 8
DEC_SEQ = 4096
PAST_LEN = 256
GRID_W = 64
HEAD_DIM = 64
A_HEADS = 4
A_KV_HEADS = 2
B_HEADS = 4
B_KV_HEADS = 2
C_HEADS = 4
D_HEADS = 4
BRANCH_W = 256
N_BRANCH = 4
WINDOW = 128
BLOCK = 128
ROPE_BASE = 10000.0
HGRN_CHUNK = 32
DELTA_CHUNK = 64
CONV_K = 5
D_FF = 2816
N_EXPERTS = 8
D_FF_EXPERT = 3584
EPS = 1e-6
NEG_INF = -1e30
F32_MIN = float(np.finfo(np.float32).min)

CTX_TOK = BATCH * SEQ
LAT_TOK = DEC_BATCH * DEC_SEQ
N_TOK = CTX_TOK + LAT_TOK
N_COND = 1 + DEC_BATCH

Z_MAIN = 3072
Z_DG = Z_MAIN
Z_DAB = Z_DG + BRANCH_W
Z_COLS = Z_DAB + 128
W_IN_MIX = 3344

TM = 512
VMEM_LIMIT = 56 * 1024 * 1024


def _tile_cond(i, tm):
    ctx_tiles = CTX_TOK // tm
    per_b = DEC_SEQ // tm
    return jnp.where(i < ctx_tiles, 0, 1 + (i - ctx_tiles) // per_b)


def _rms(x, g):
    return x * lax.rsqrt(jnp.mean(x * x, axis=-1, keepdims=True) + EPS) * g


def _params(sem):
    return pltpu.CompilerParams(dimension_semantics=sem, vmem_limit_bytes=VMEM_LIMIT)


def _ada_kernel(c_ref, w_ref, b_ref, o_ref):
    c = c_ref[...]
    s = c * jax.nn.sigmoid(c)
    o_ref[...] = jnp.dot(s.astype(BF16), w_ref[...].astype(BF16), preferred_element_type=F32) + b_ref[...]


def ada_modulation(cond_pad, w, b):
    n = 6 * D_MODEL
    tn = 1536
    return pl.pallas_call(
        _ada_kernel,
        grid=(n // tn,),
        in_specs=[pl.BlockSpec((16, D_MODEL), lambda j: (0, 0)),
                  pl.BlockSpec((D_MODEL, tn), lambda j: (0, j)),
                  pl.BlockSpec((1, tn), lambda j: (0, j))],
        out_specs=pl.BlockSpec((16, tn), lambda j: (0, j)),
        out_shape=jax.ShapeDtypeStruct((16, n), F32),
        compiler_params=_params(("arbitrary",)),
        name="ada_modulation",
    )(cond_pad, w, b.reshape(1, n))


def _x_specs(xs):
    ctx_tiles = CTX_TOK // TM
    lat_tile0 = xs[2]
    return [pl.BlockSpec((TM, D_MODEL), lambda i: (jnp.minimum(i, ctx_tiles - 1), 0)),
            pl.BlockSpec((TM, D_MODEL), lambda i: (jnp.maximum(i - ctx_tiles, 0) + lat_tile0, 0))]


def _x_tile(xc_ref, xl_ref):
    return jnp.where(pl.program_id(0) < CTX_TOK // TM, xc_ref[...], xl_ref[...])


def _in_kernel(xc_ref, xl_ref, mod_ref, g_ref, w_ref, z_ref):
    h = _rms(_x_tile(xc_ref, xl_ref), g_ref[...]) * (1.0 + mod_ref[1:2, :]) + mod_ref[0:1, :]
    z_ref[...] = jnp.dot(h.astype(BF16), w_ref[...], preferred_element_type=F32)


def input_projection(xs, mod, g, w):
    nt = N_TOK // TM
    return pl.pallas_call(
        _in_kernel,
        grid=(nt,),
        in_specs=_x_specs(xs) + [
                  pl.BlockSpec((None, 6, D_MODEL), lambda i: (_tile_cond(i, TM), 0, 0)),
                  pl.BlockSpec((1, D_MODEL), lambda i: (0, 0)),
                  pl.BlockSpec((D_MODEL, Z_COLS), lambda i: (0, 0))],
        out_specs=pl.BlockSpec((TM, Z_COLS), lambda i: (i, 0)),
        out_shape=jax.ShapeDtypeStruct((N_TOK, Z_COLS), F32),
        compiler_params=_params(("arbitrary",)),
        name="input_projection",
    )(xs[0], xs[1], mod, g.reshape(1, D_MODEL), w)


def _merge_kernel(xc_ref, xl_ref, mod_ref, g_ref, ab_c_ref, c_c_ref, df_c_ref, db_c_ref, a_l_ref, b_l_ref, c_l_ref,
                  df_l_ref, db_l_ref, zg_ref, gd_ref, wgl_ref, wbr_ref, wout_ref, xo_ref):
    x = _x_tile(xc_ref, xl_ref)
    h = (_rms(x, g_ref[...]) * (1.0 + mod_ref[1:2, :]) + mod_ref[0:1, :]).astype(BF16)
    is_ctx = pl.program_id(0) < CTX_TOK // TM
    d_heads = []
    for hd in range(D_HEADS):
        o = jnp.where(is_ctx, df_c_ref[hd] + db_c_ref[hd], df_l_ref[hd] + db_l_ref[hd])
        d_heads.append(o * lax.rsqrt(jnp.mean(o * o, axis=-1, keepdims=True) + EPS) * gd_ref[...])
    dg = zg_ref[...]
    branches = (jnp.where(is_ctx, ab_c_ref[:, :BRANCH_W], a_l_ref[...]),
                jnp.where(is_ctx, ab_c_ref[:, BRANCH_W:], b_l_ref[...]),
                jnp.where(is_ctx, c_c_ref[...], c_l_ref[...]),
                (jnp.concatenate(d_heads, axis=-1) * (dg * jax.nn.sigmoid(dg))).astype(BF16))
    merged = None
    for j in range(N_BRANCH):
        gate = jax.nn.sigmoid(jnp.dot(h, wgl_ref[j], preferred_element_type=F32))
        br = jnp.dot(branches[j], wbr_ref[j], preferred_element_type=F32)
        merged = gate * br if merged is None else merged + gate * br
    mix = jnp.dot(merged.astype(BF16), wout_ref[...], preferred_element_type=F32)
    xo_ref[...] = x + mod_ref[2:3, :] * mix


def merge_projection(xs, mod, g, o_ctx, o_lat, z, gd, wgl, wbr, wout):
    nt = N_TOK // TM
    ctx_tiles = CTX_TOK // TM
    ctx_i = lambda i: jnp.minimum(i, ctx_tiles - 1)
    lat_i = lambda i: jnp.maximum(i - ctx_tiles, 0)
    ctx_spec = lambda w: pl.BlockSpec((TM, w), lambda i: (ctx_i(i), 0))
    lat_spec = pl.BlockSpec((TM, BRANCH_W), lambda i: (lat_i(i), 0))
    ctx_d_spec = pl.BlockSpec((D_HEADS, TM, HEAD_DIM), lambda i: (0, ctx_i(i), 0))
    lat_d_spec = pl.BlockSpec((D_HEADS, TM, HEAD_DIM), lambda i: (0, lat_i(i), 0))
    return pl.pallas_call(
        _merge_kernel,
        grid=(nt,),
        in_specs=_x_specs(xs) + [
                  pl.BlockSpec((None, 6, D_MODEL), lambda i: (_tile_cond(i, TM), 0, 0)),
                  pl.BlockSpec((1, D_MODEL), lambda i: (0, 0)),
                  ctx_spec(2 * BRANCH_W), ctx_spec(BRANCH_W), ctx_d_spec, ctx_d_spec,
                  lat_spec, lat_spec, lat_spec, lat_d_spec, lat_d_spec,
                  pl.BlockSpec((TM, BRANCH_W), lambda i: (i, Z_DG // BRANCH_W)),
                  pl.BlockSpec((1, HEAD_DIM), lambda i: (0, 0)),
                  pl.BlockSpec((N_BRANCH, D_MODEL, D_MODEL), lambda i: (0, 0, 0)),
                  pl.BlockSpec((N_BRANCH, BRANCH_W, D_MODEL), lambda i: (0, 0, 0)),
                  pl.BlockSpec((D_MODEL, D_MODEL), lambda i: (0, 0))],
        out_specs=pl.BlockSpec((TM, D_MODEL), lambda i: (i, 0)),
        out_shape=jax.ShapeDtypeStruct((N_TOK, D_MODEL), F32),
        compiler_params=_params(("arbitrary",)),
        name="merge_projection",
    )(xs[0], xs[1], mod, g.reshape(1, D_MODEL), *o_ctx, *o_lat, z, gd.reshape(1, HEAD_DIM), wgl, wbr, wout)


def _ffn_kernel(x_ref, mod_ref, g_ref, w1_ref, w3_ref, w2_ref, xo_ref):
    x = x_ref[...]
    h = (_rms(x, g_ref[...]) * (1.0 + mod_ref[4:5, :]) + mod_ref[3:4, :]).astype(BF16)
    a = jnp.dot(h, w1_ref[...], preferred_element_type=F32)
    b = jnp.dot(h, w3_ref[...], preferred_element_type=F32)
    hid = (a * jax.nn.sigmoid(a) * b).astype(BF16)
    f = jnp.dot(hid, w2_ref[...], preferred_element_type=F32)
    xo_ref[...] = x + mod_ref[5:6, :] * f


def dense_ffn(x, mod, g, w1, w3, w2):
    nt = N_TOK // TM
    const = lambda i: (0, 0)
    return pl.pallas_call(
        _ffn_kernel,
        grid=(nt,),
        in_specs=[pl.BlockSpec((TM, D_MODEL), lambda i: (i, 0)),
                  pl.BlockSpec((None, 6, D_MODEL), lambda i: (_tile_cond(i, TM), 0, 0)),
                  pl.BlockSpec((1, D_MODEL), const),
                  pl.BlockSpec((D_MODEL, D_FF), const, pipeline_mode=pl.Buffered(1)),
                  pl.BlockSpec((D_MODEL, D_FF), const, pipeline_mode=pl.Buffered(1)),
                  pl.BlockSpec((D_FF, D_MODEL), const, pipeline_mode=pl.Buffered(1))],
        out_specs=pl.BlockSpec((TM, D_MODEL), lambda i: (i, 0)),
        out_shape=jax.ShapeDtypeStruct((N_TOK, D_MODEL), F32),
        compiler_params=_params(("arbitrary",)),
        name="dense_ffn",
    )(x, mod, g.reshape(1, D_MODEL), w1, w3, w2)


MOE_T = 1024
MOE_R = 144
MOE_SR = 128
MOE_F = 1792
MOE_CAP = -(-MOE_T // MOE_R) * MOE_R


def _router_kernel(x_ref, mod_ref, g_ref, rw_ref, rb_ref, before_ref, h_ref, pos_ref, wgt_ref):
    h = _rms(x_ref[...], g_ref[...]) * (1.0 + mod_ref[4:5, :]) + mod_ref[3:4, :]
    h_ref[...] = h.astype(BF16)
    logits = jnp.dot(h, rw_ref[...], preferred_element_type=F32, precision=lax.Precision.HIGHEST) + rb_ref[...]
    lt = logits.T[:N_EXPERTS, :]
    eidx = lax.broadcasted_iota(jnp.int32, lt.shape, 0)
    m1 = jnp.max(lt, axis=0, keepdims=True)
    i1 = jnp.min(jnp.where(lt == m1, eidx, N_EXPERTS), axis=0, keepdims=True)
    rest = jnp.where(eidx == i1, F32_MIN, lt)
    m2 = jnp.max(rest, axis=0, keepdims=True)
    i2 = jnp.min(jnp.where(rest == m2, eidx, N_EXPERTS), axis=0, keepdims=True)
    e2 = jnp.exp(m2 - m1)
    p1 = 1.0 / (1.0 + e2)
    p2 = e2 / (1.0 + e2)
    wgt_ref[...] = jnp.where(eidx == i1, p1, 0.0) + jnp.where(eidx == i2, p2, 0.0)
    routed = jnp.where(eidx == i1, 1.0, jnp.where(eidx == i2, 1.0, 0.0))
    rank = jnp.dot(routed.astype(BF16), before_ref[...], preferred_element_type=F32)
    pos_ref[...] = jnp.where(routed > 0.0, rank.astype(jnp.int32), -1)


def moe_router(x, mod, g, rw, rb):
    nt = N_TOK // MOE_T
    tok = jnp.arange(MOE_T, dtype=jnp.int32)
    before = (tok[:, None] < tok[None, :]).astype(BF16)
    return pl.pallas_call(
        _router_kernel,
        grid=(nt,),
        in_specs=[pl.BlockSpec((MOE_T, D_MODEL), lambda i: (i, 0)),
                  pl.BlockSpec((None, 6, D_MODEL), lambda i: (_tile_cond(i, MOE_T), 0, 0)),
                  pl.BlockSpec((1, D_MODEL), lambda i: (0, 0)),
                  pl.BlockSpec((D_MODEL, 128), lambda i: (0, 0)),
                  pl.BlockSpec((1, 128), lambda i: (0, 0)),
                  pl.BlockSpec((MOE_T, MOE_T), lambda i: (0, 0))],
        out_specs=[pl.BlockSpec((MOE_T, D_MODEL), lambda i: (i, 0)),
                   pl.BlockSpec((N_EXPERTS, MOE_T), lambda i: (0, i)),
                   pl.BlockSpec((N_EXPERTS, MOE_T), lambda i: (0, i))],
        out_shape=[jax.ShapeDtypeStruct((N_TOK, D_MODEL), BF16),
                   jax.ShapeDtypeStruct((N_EXPERTS, N_TOK), jnp.int32),
                   jax.ShapeDtypeStruct((N_EXPERTS, N_TOK), F32)],
        compiler_params=_params(("arbitrary",)),
        name="moe_router",
    )(x, mod, g.reshape(1, D_MODEL), rw, rb, before)


def _moe_sparse_kernel(h_ref, pos_ref, wgt_ref, w1_ref, w3_ref, w2_ref, y_ref, xg_scr, acc_scr, wr_scr):
    e = pl.program_id(1)
    f = pl.program_id(2)
    t = h_ref.shape[0]
    pos_e = pos_ref[pl.ds(e, 1), :]
    n_rows = jnp.max(pos_e) + 1
    n_blocks = (n_rows + MOE_R - 1) // MOE_R
    n_sc_blocks = (n_rows + MOE_SR - 1) // MOE_SR
    n_init_blocks = jnp.maximum(n_blocks, (n_sc_blocks * MOE_SR + MOE_R - 1) // MOE_R)
    row_id = lax.broadcasted_iota(jnp.int32, (MOE_R, t), 0)

    def block_rows(r):
        return pl.ds(pl.multiple_of(r * MOE_R, 16), MOE_R)

    def selects(r):
        return pos_e == row_id + r * MOE_R

    @pl.when((e == 0) & (f == 0))
    def _():
        y_ref[...] = jnp.zeros_like(y_ref)

    @pl.when(f == 0)
    def _():
        wgt_e = wgt_ref[pl.ds(e, 1), :]

        def gather(r, carry):
            sel = selects(r)
            xg = jnp.dot(jnp.where(sel, 1.0, 0.0).astype(BF16), h_ref[...], preferred_element_type=F32)
            xg_scr[block_rows(r), :] = xg.astype(BF16)
            w_rows = jnp.sum(jnp.where(sel, wgt_e, 0.0), axis=1, keepdims=True)
            wr_scr[block_rows(r), :] = jnp.broadcast_to(w_rows, (MOE_R, 128))
            acc_scr[block_rows(r), :] = jnp.zeros((MOE_R, D_MODEL), F32)
            return carry

        lax.fori_loop(0, n_init_blocks, gather, 0)

    def expert(r, carry):
        xg = xg_scr[block_rows(r), :]
        a = jnp.dot(xg, w1_ref[...], preferred_element_type=F32)
        b = jnp.dot(xg, w3_ref[...], preferred_element_type=F32)
        hid = (a * jax.nn.sigmoid(a) * b * wr_scr[block_rows(r), 0:1]).astype(BF16)
        acc_scr[block_rows(r), :] += jnp.dot(hid, w2_ref[...], preferred_element_type=F32)
        return carry

    lax.fori_loop(0, n_blocks, expert, 0)

    @pl.when(f == pl.num_programs(2) - 1)
    def _():
        sc_row_id = lax.broadcasted_iota(jnp.int32, (MOE_SR, t), 0)

        def scatter(r, carry):
            rows = pl.ds(pl.multiple_of(r * MOE_SR, MOE_SR), MOE_SR)
            onehot = jnp.where(pos_e == sc_row_id + r * MOE_SR, 1.0, 0.0).astype(BF16)
            hi, lo = _split_bf16(acc_scr[rows, :])
            y_ref[...] += lax.dot_general(jnp.concatenate([onehot, onehot], axis=0),
                                          jnp.concatenate([hi, lo], axis=0), _TN, preferred_element_type=F32)
            return carry

        lax.fori_loop(0, n_sc_blocks, scatter, 0)


def moe_experts(h2, pos, wgt, w1, w3, w2):
    nt = N_TOK // MOE_T
    nf = D_FF_EXPERT // MOE_F
    return pl.pallas_call(
        _moe_sparse_kernel,
        grid=(nt, N_EXPERTS, nf),
        in_specs=[pl.BlockSpec((MOE_T, D_MODEL), lambda i, e, f: (i, 0)),
                  pl.BlockSpec((N_EXPERTS, MOE_T), lambda i, e, f: (0, i)),
                  pl.BlockSpec((N_EXPERTS, MOE_T), lambda i, e, f: (0, i)),
                  pl.BlockSpec((None, D_MODEL, MOE_F), lambda i, e, f: (e, 0, f)),
                  pl.BlockSpec((None, D_MODEL, MOE_F), lambda i, e, f: (e, 0, f)),
                  pl.BlockSpec((None, MOE_F, D_MODEL), lambda i, e, f: (e, f, 0))],
        out_specs=pl.BlockSpec((MOE_T, D_MODEL), lambda i, e, f: (i, 0)),
        out_shape=jax.ShapeDtypeStruct((N_TOK, D_MODEL), F32),
        scratch_shapes=[pltpu.VMEM((MOE_CAP, D_MODEL), BF16),
                        pltpu.VMEM((MOE_CAP, D_MODEL), F32),
                        pltpu.VMEM((MOE_CAP, 128), F32)],
        compiler_params=_params(("arbitrary", "arbitrary", "arbitrary")),
        name="moe_experts",
    )(h2, pos, wgt, w1, w3, w2)


def _residual_norm_kernel(x_ref, y_ref, mod_ref, g_ref, o_ref):
    o_ref[...] = _rms(x_ref[...] + mod_ref[5:6, :] * y_ref[...], g_ref[...])


def residual_final_norm(x, y, mod, g, row0, n_rows):
    tm = 1024
    t0 = row0 // tm
    return pl.pallas_call(
        _residual_norm_kernel,
        grid=(n_rows // tm,),
        in_specs=[pl.BlockSpec((tm, D_MODEL), lambda i: (t0 + i, 0)),
                  pl.BlockSpec((tm, D_MODEL), lambda i: (t0 + i, 0)),
                  pl.BlockSpec((None, 6, D_MODEL), lambda i: (_tile_cond(t0 + i, tm), 0, 0)),
                  pl.BlockSpec((1, D_MODEL), lambda i: (0, 0))],
        out_specs=pl.BlockSpec((tm, D_MODEL), lambda i: (i, 0)),
        out_shape=jax.ShapeDtypeStruct((n_rows, D_MODEL), F32),
        compiler_params=_params(("arbitrary",)),
        name="residual_final_norm",
    )(x, y, mod, g.reshape(1, D_MODEL))


ATT_SCALE = HEAD_DIM ** -0.5
LOG2_E = 1.4426950408889634
_NT = (((1,), (1,)), ((), ()))


def _head_rms(x, g_row):
    outs = []
    for h in range(x.shape[1] // HEAD_DIM):
        xh = x[:, h * HEAD_DIM:(h + 1) * HEAD_DIM]
        outs.append(xh * lax.rsqrt(jnp.mean(xh * xh, axis=-1, keepdims=True) + EPS) * g_row)
    return jnp.concatenate(outs, axis=-1)


def _rope_apply(x, c, s):
    w = x.shape[-1]
    lane = lax.broadcasted_iota(jnp.int32, x.shape, 1)
    first_half = ((lane // (HEAD_DIM // 4)) % 2) == 0
    partner = jnp.where(first_half, pltpu.roll(x, w - HEAD_DIM // 4, 1), pltpu.roll(x, HEAD_DIM // 4, 1))
    return x * c + partner * s


def rope_lane_tables(L):
    rows = L // GRID_W
    row = jnp.repeat(jnp.arange(rows, dtype=F32), GRID_W)
    col = jnp.tile(jnp.arange(GRID_W, dtype=F32), rows)
    n_freq = HEAD_DIM // 4
    inv = ROPE_BASE ** (-jnp.arange(n_freq, dtype=F32) / n_freq)
    ang = jnp.stack([row, col], 0)[:, :, None] * inv
    cos, sin = jnp.cos(ang), jnp.sin(ang)
    c = jnp.concatenate([cos[0], cos[0], cos[1], cos[1]], axis=-1)
    s = jnp.concatenate([-sin[0], sin[0], -sin[1], sin[1]], axis=-1)
    return jnp.tile(c, (1, 4)), jnp.tile(s, (1, 4))


def _with_ones(v):
    ones = jnp.ones((v.shape[0], HEAD_DIM), BF16)
    parts = []
    for h in range(v.shape[1] // HEAD_DIM):
        parts += [v[:, h * HEAD_DIM:(h + 1) * HEAD_DIM].astype(BF16), ones]
    return jnp.concatenate(parts, axis=-1)


def _attend_heads(jobs):
    for job in jobs:
        q = (job['q'] * (ATT_SCALE * LOG2_E)).astype(BF16)
        job['s'] = lax.dot_general(q, job['k'], _NT, preferred_element_type=F32)
        if job.get('extra') is not None:
            job['s2'] = lax.dot_general(q, job['extra'][0], _NT, preferred_element_type=F32)
    outs = []
    for job in jobs:
        s, sink = job['s'], job.get('sink')
        if sink is not None:
            sink = sink * LOG2_E
        if job.get('mask') is not None:
            s = jnp.where(job['mask'], s, NEG_INF)
        m = jnp.max(s, axis=-1, keepdims=True)
        if 's2' in job:
            m = jnp.maximum(m, jnp.max(job['s2'], axis=-1, keepdims=True))
        if sink is not None:
            m = jnp.maximum(m, sink)
        o = jnp.dot(jnp.exp2(s - m).astype(BF16), job['v'], preferred_element_type=F32)
        if 's2' in job:
            o = o + jnp.dot(jnp.exp2(job['s2'] - m).astype(BF16), job['extra'][1], preferred_element_type=F32)
        den = o[:, HEAD_DIM:HEAD_DIM + 1]
        if sink is not None:
            den = den + jnp.exp2(sink - m)
        outs.append(o[:, :HEAD_DIM] / den)
    return outs


def _ctx_attn_kernel(sink_ref, z_ref, gq_ref, gk_ref, o_ref, bk_ref):
    z = z_ref[...]
    bq = _head_rms(z[:, 512:768], gq_ref[...])
    bk = _head_rms(z[:, 768:896], gk_ref[...])
    bk_ref[...] = bk
    groups = ((z[:, 0:256], z[:, 256:384], z[:, 384:512], True),
              (bq, bk, z[:, 896:1024], False))
    outs = []
    for q_all, k_all, v_all, use_sink in groups:
        k_all = k_all.astype(BF16)
        v_all = _with_ones(v_all)
        for hq in range(A_HEADS):
            kv = hq // (A_HEADS // A_KV_HEADS)
            outs.append(dict(q=q_all[:, hq * HEAD_DIM:(hq + 1) * HEAD_DIM],
                             k=k_all[:, kv * HEAD_DIM:(kv + 1) * HEAD_DIM],
                             v=v_all[:, kv * 2 * HEAD_DIM:(kv + 1) * 2 * HEAD_DIM],
                             sink=sink_ref[hq] if use_sink else None))
    o_ref[...] = jnp.concatenate(_attend_heads(outs), axis=-1).astype(o_ref.dtype)


def ctx_attention(z, sink, gq, gk):
    return pl.pallas_call(
        _ctx_attn_kernel,
        grid=(BATCH,),
        in_specs=[pl.BlockSpec(memory_space=pltpu.SMEM),
                  pl.BlockSpec((SEQ, 1024), lambda b: (b, 0)),
                  pl.BlockSpec((1, HEAD_DIM), lambda b: (0, 0)),
                  pl.BlockSpec((1, HEAD_DIM), lambda b: (0, 0))],
        out_specs=[pl.BlockSpec((SEQ, 512), lambda b: (b, 0)),
                   pl.BlockSpec((SEQ, 128), lambda b: (b, 0))],
        out_shape=[jax.ShapeDtypeStruct((CTX_TOK, 512), BF16),
                   jax.ShapeDtypeStruct((CTX_TOK, 128), F32)],
        compiler_params=_params(("arbitrary",)),
        name="ctx_attention",
    )(sink, z, gq.reshape(1, HEAD_DIM), gk.reshape(1, HEAD_DIM))


LB_TQ = 256


def _lat_b_kernel(zq_ref, zkv_ref, ck_ref, cv_ref, cq_ref, sq_ref, ckk_ref, skk_ref, gq_ref, gk_ref,
                  o_ref, k_scr, v_scr):
    L = zkv_ref.shape[0]

    @pl.when(pl.program_id(1) == 0)
    def _():
        kv = zkv_ref[...]
        bk = _rope_apply(_head_rms(kv[:, :128], gk_ref[...]), ckk_ref[...], skk_ref[...])
        k_scr[0:L, :] = bk.astype(BF16)
        k_scr[L:L + PAST_LEN, :] = ck_ref[...].astype(BF16)
        v_scr[0:L, :] = _with_ones(kv[:, 128:])
        v_scr[L:L + PAST_LEN, :] = _with_ones(cv_ref[...])

    q = _rope_apply(_head_rms(zq_ref[...], gq_ref[...]), cq_ref[...], sq_ref[...])
    outs = []
    for hq in range(B_HEADS):
        kv = hq // (B_HEADS // B_KV_HEADS)
        sl = slice(kv * HEAD_DIM, (kv + 1) * HEAD_DIM)
        outs.append(dict(q=q[:, hq * HEAD_DIM:(hq + 1) * HEAD_DIM], k=k_scr[:, sl],
                         v=v_scr[:, kv * 2 * HEAD_DIM:(kv + 1) * 2 * HEAD_DIM]))
    o_ref[...] = jnp.concatenate(_attend_heads(outs), axis=-1).astype(o_ref.dtype)


def latent_attention_b(z, row0, nb, L, cache_k, cache_v, rope_c, rope_s, gq, gk):
    nq = L // LB_TQ
    return pl.pallas_call(
        _lat_b_kernel,
        grid=(nb, nq),
        in_specs=[pl.BlockSpec((LB_TQ, 256), lambda b, i: (row0 // LB_TQ + b * nq + i, 2)),
                  pl.BlockSpec((L, 256), lambda b, i: (row0 // L + b, 3)),
                  pl.BlockSpec((None, PAST_LEN, 128), lambda b, i: (b, 0, 0)),
                  pl.BlockSpec((None, PAST_LEN, 128), lambda b, i: (b, 0, 0)),
                  pl.BlockSpec((LB_TQ, 256), lambda b, i: (i, 0)),
                  pl.BlockSpec((LB_TQ, 256), lambda b, i: (i, 0)),
                  pl.BlockSpec((L, 128), lambda b, i: (0, 0)),
                  pl.BlockSpec((L, 128), lambda b, i: (0, 0)),
                  pl.BlockSpec((1, HEAD_DIM), lambda b, i: (0, 0)),
                  pl.BlockSpec((1, HEAD_DIM), lambda b, i: (0, 0))],
        out_specs=pl.BlockSpec((LB_TQ, 256), lambda b, i: (b * nq + i, 0)),
        out_shape=jax.ShapeDtypeStruct((nb * L, 256), BF16),
        scratch_shapes=[pltpu.VMEM((L + PAST_LEN, 128), BF16),
                        pltpu.VMEM((L + PAST_LEN, 256), BF16)],
        compiler_params=_params(("arbitrary", "arbitrary")),
        name="latent_attention_b",
    )(z, z, cache_k, cache_v, rope_c, rope_s, rope_c, rope_s, gq.reshape(1, HEAD_DIM), gk.reshape(1, HEAD_DIM))


def _lat_a_kernel(sink_ref, zq_ref, zkv_ref, ck_ref, cv_ref, cq_ref, sq_ref, ckk_ref, skk_ref,
                  o_ref, k_scr, v_scr, ck_scr, cv_scr):
    L = zkv_ref.shape[0]
    i = pl.program_id(1)

    @pl.when(i == 0)
    def _():
        kv = zkv_ref[...]
        k_scr[0:BLOCK, :] = jnp.zeros((BLOCK, 128), BF16)
        v_scr[0:BLOCK, :] = jnp.zeros((BLOCK, 256), BF16)
        k_scr[BLOCK:BLOCK + L, :] = _rope_apply(kv[:, :128], ckk_ref[...], skk_ref[...]).astype(BF16)
        v_scr[BLOCK:BLOCK + L, :] = _with_ones(kv[:, 128:])
        k_scr[BLOCK + L:2 * BLOCK + L, :] = jnp.zeros((BLOCK, 128), BF16)
        v_scr[BLOCK + L:2 * BLOCK + L, :] = jnp.zeros((BLOCK, 256), BF16)
        ck_scr[...] = ck_ref[...].astype(BF16)
        cv_scr[...] = _with_ones(cv_ref[...])

    q = _rope_apply(zq_ref[...], cq_ref[...], sq_ref[...])
    start = pl.multiple_of(i * BLOCK, BLOCK)
    kband = k_scr[pl.ds(start, 3 * BLOCK), :]
    vband = v_scr[pl.ds(start, 3 * BLOCK), :]
    r = lax.broadcasted_iota(jnp.int32, (BLOCK, 3 * BLOCK), 0)
    cidx = lax.broadcasted_iota(jnp.int32, (BLOCK, 3 * BLOCK), 1)
    kpos = i * BLOCK - BLOCK + cidx
    mask = (jnp.abs(cidx - BLOCK - r) <= WINDOW) & (kpos >= 0) & (kpos < L)
    outs = []
    for hq in range(A_HEADS):
        kv = hq // (A_HEADS // A_KV_HEADS)
        sl = slice(kv * HEAD_DIM, (kv + 1) * HEAD_DIM)
        sv = slice(kv * 2 * HEAD_DIM, (kv + 1) * 2 * HEAD_DIM)
        outs.append(dict(q=q[:, hq * HEAD_DIM:(hq + 1) * HEAD_DIM], k=kband[:, sl], v=vband[:, sv],
                         extra=(ck_scr[:, sl], cv_scr[:, sv]), sink=sink_ref[hq], mask=mask))
    o_ref[...] = jnp.concatenate(_attend_heads(outs), axis=-1).astype(o_ref.dtype)


def latent_attention_a(z, row0, nb, L, sink, cache_k, cache_v, rope_c, rope_s):
    nq = L // BLOCK
    return pl.pallas_call(
        _lat_a_kernel,
        grid=(nb, nq),
        in_specs=[pl.BlockSpec(memory_space=pltpu.SMEM),
                  pl.BlockSpec((BLOCK, 256), lambda b, i: (row0 // BLOCK + b * nq + i, 0)),
                  pl.BlockSpec((L, 256), lambda b, i: (row0 // L + b, 1)),
                  pl.BlockSpec((None, PAST_LEN, 128), lambda b, i: (b, 0, 0)),
                  pl.BlockSpec((None, PAST_LEN, 128), lambda b, i: (b, 0, 0)),
                  pl.BlockSpec((BLOCK, 256), lambda b, i: (i, 0)),
                  pl.BlockSpec((BLOCK, 256), lambda b, i: (i, 0)),
                  pl.BlockSpec((L, 128), lambda b, i: (0, 0)),
                  pl.BlockSpec((L, 128), lambda b, i: (0, 0))],
        out_specs=pl.BlockSpec((BLOCK, 256), lambda b, i: (b * nq + i, 0)),
        out_shape=jax.ShapeDtypeStruct((nb * L, 256), BF16),
        scratch_shapes=[pltpu.VMEM((L + 2 * BLOCK, 128), BF16),
                        pltpu.VMEM((L + 2 * BLOCK, 256), BF16),
                        pltpu.VMEM((PAST_LEN, 128), BF16),
                        pltpu.VMEM((PAST_LEN, 256), BF16)],
        compiler_params=_params(("arbitrary", "arbitrary")),
        name="latent_attention_a",
    )(sink, z, z, cache_k, cache_v, rope_c, rope_s, rope_c, rope_s)


_TN = (((0,), (0,)), ((), ()))
HG_GROUP = 8


def _hgrn_kernel(zq_ref, zf_ref, zi_ref, zg_ref, lb_ref, gn_ref, s0_ref, o_ref, sT_ref,
                 of_scr, ob_scr, g_scr, k_scr, qin_scr, kin_scr, v_scr, S_scr, *, tt):
    d = pl.program_id(1)
    j = pl.program_id(2)
    n_t = pl.num_programs(2)
    C = HGRN_CHUNK
    n_c = tt // C

    @pl.when(j == 0)
    def _():
        S_scr[...] = s0_ref[...]

    lb = lb_ref[...]
    sg = jax.nn.sigmoid(zf_ref[...])
    logf = jnp.log(lb + (1.0 - lb) * sg)
    k = (1.0 - lb) * (1.0 - sg)
    k_scr[...] = k
    v_scr[...] = zi_ref[...].astype(BF16)
    in_chunk = lax.broadcasted_iota(jnp.int32, (tt, C_HEADS * HEAD_DIM), 0) % C
    row = lax.broadcasted_iota(jnp.int32, (C, C), 0)
    col = lax.broadcasted_iota(jnp.int32, (C, C), 1)
    heads = [slice(h * HEAD_DIM, (h + 1) * HEAD_DIM) for h in range(C_HEADS)]

    def run(reverse, tile):
        G = logf
        step = 1
        while step < C:
            if reverse:
                G = G + jnp.where(in_chunk < C - step, pltpu.roll(G, tt - step, 0), 0.0)
            else:
                G = G + jnp.where(in_chunk >= step, pltpu.roll(G, step, 0), 0.0)
            step *= 2
        g_scr[...] = G
        qin_scr[...] = (zq_ref[...] * jnp.exp(G)).astype(BF16)
        kin_scr[...] = (k * jnp.exp(-G)).astype(BF16)
        tri = (row <= col) if reverse else (row >= col)

        def body(gi, carry):
            chunks = []
            for g in range(HG_GROUP):
                ci = gi * HG_GROUP + g
                c = (n_c - 1 - ci) if reverse else ci
                r0 = pl.multiple_of(c * C, C)
                rows = pl.ds(r0, C)
                G_c = g_scr[rows, :]
                G_end = G_c[0:1, :] if reverse else G_c[C - 1:C, :]
                vc = v_scr[rows, :]
                q_in = qin_scr[rows, :]
                k_in = kin_scr[rows, :]
                k_out = (k_scr[rows, :] * jnp.exp(G_end - G_c)).astype(BF16)
                chunks.append(dict(
                    r0=r0, rows=rows, vc=vc, q_in=q_in, decay=jnp.exp(G_end),
                    a=[lax.dot_general(q_in[:, sl], k_in[:, sl], _NT, preferred_element_type=F32) for sl in heads],
                    kv=[lax.dot_general(vc[:, sl], k_out[:, sl], _TN, preferred_element_type=F32) for sl in heads]))
            s_cur = [S_scr[h] for h in range(C_HEADS)]
            for ch in chunks:
                ch['qs'] = [lax.dot_general(ch['q_in'][:, sl], s_cur[h].astype(BF16), _NT,
                                            preferred_element_type=F32) for h, sl in enumerate(heads)]
                s_cur = [s_cur[h] * ch['decay'][:, sl] + ch['kv'][h] for h, sl in enumerate(heads)]
            for h in range(C_HEADS):
                S_scr[h] = s_cur[h]
            for ch in chunks:
                o_c = jnp.concatenate(
                    [jnp.dot(jnp.where(tri, ch['a'][h], 0.0).astype(BF16), ch['vc'][:, sl],
                             preferred_element_type=F32) + ch['qs'][h] for h, sl in enumerate(heads)], axis=-1)
                if reverse:
                    ob_scr[ch['rows'], :] = o_c
                else:
                    of_scr[pl.ds(pl.multiple_of(tile * tt, tt) + ch['r0'], C), :] = o_c
            return carry

        lax.fori_loop(0, n_c // HG_GROUP, body, 0)

    @pl.when(d == 0)
    def _():
        run(False, j)

    @pl.when(d == 1)
    def _():
        tile = n_t - 1 - j
        run(True, tile)
        o = of_scr[pl.ds(pl.multiple_of(tile * tt, tt), tt), :] + ob_scr[...]
        g = zg_ref[...]
        o_ref[...] = (_head_rms(o, gn_ref[...]) * (g * jax.nn.sigmoid(g))).astype(o_ref.dtype)

    @pl.when(j == n_t - 1)
    def _():
        sT_ref[...] = S_scr[...]


def hgrn_mixer(z, row0, nb, L, lb, gn, s0_t):
    tt = min(L, 512)
    n_t = L // tt
    rb = row0 // tt

    def tile(d, j):
        return jnp.where(d == 0, j, n_t - 1 - j)

    def late(d, j):
        return jnp.where(d == 0, n_t - 1, n_t - 1 - j)

    st_spec = pl.BlockSpec((None, None, C_HEADS, HEAD_DIM, HEAD_DIM), lambda b, d, j: (b, d, 0, 0, 0))
    return pl.pallas_call(
        functools.partial(_hgrn_kernel, tt=tt),
        grid=(nb, 2, n_t),
        in_specs=[pl.BlockSpec((tt, 256), lambda b, d, j: (rb + b * n_t + tile(d, j), 4)),
                  pl.BlockSpec((tt, 256), lambda b, d, j: (rb + b * n_t + tile(d, j), 5 + d)),
                  pl.BlockSpec((tt, 256), lambda b, d, j: (rb + b * n_t + tile(d, j), 7)),
                  pl.BlockSpec((tt, 256), lambda b, d, j: (rb + b * n_t + late(d, j), 8)),
                  pl.BlockSpec((1, 256), lambda b, d, j: (0, 0)),
                  pl.BlockSpec((1, HEAD_DIM), lambda b, d, j: (0, 0)),
                  st_spec],
        out_specs=[pl.BlockSpec((tt, 256), lambda b, d, j: (b * n_t + late(d, j), 0)),
                   st_spec],
        out_shape=[jax.ShapeDtypeStruct((nb * L, 256), BF16),
                   jax.ShapeDtypeStruct((nb, 2, C_HEADS, HEAD_DIM, HEAD_DIM), F32)],
        scratch_shapes=[pltpu.VMEM((L, 256), F32),
                        pltpu.VMEM((tt, 256), F32),
                        pltpu.VMEM((tt, 256), F32),
                        pltpu.VMEM((tt, 256), F32),
                        pltpu.VMEM((tt, 256), BF16),
                        pltpu.VMEM((tt, 256), BF16),
                        pltpu.VMEM((tt, 256), BF16),
                        pltpu.VMEM((C_HEADS, HEAD_DIM, HEAD_DIM), F32)],
        compiler_params=_params(("arbitrary", "arbitrary", "arbitrary")),
        name="hgrn_mixer",
    )(z, z, z, z, lb.reshape(1, 256), gn.reshape(1, HEAD_DIM), s0_t)


DL_C = DELTA_CHUNK
DL_PREP_TT = 256
DL_HALO = 8
DL_CHUNK_TT = 512
N_QKV_HEADS = 3 * D_HEADS
DL_GROUP = 2


def _delta_prep_kernel(x_ref, xp_ref, xn_ref, zab_ref, cw_ref, na_ref, dtb_ref, qkv_ref, gate_ref, xs_scr):
    tt = x_ref.shape[0]
    row = pl.program_id(0) * tt
    lat = row - CTX_TOK
    first = jnp.where(row < CTX_TOK, True, lat % DEC_SEQ == 0)
    last = jnp.where(row < CTX_TOK, True, (lat + tt) % DEC_SEQ == 0)
    xs_scr[DL_HALO:DL_HALO + tt, :] = x_ref[...]
    xs_scr[0:DL_HALO, :] = jnp.where(first, 0.0, xp_ref[...])
    xs_scr[DL_HALO + tt:2 * DL_HALO + tt, :] = jnp.where(last, 0.0, xn_ref[...])
    pad = (CONV_K - 1) // 2
    y = None
    for t in range(CONV_K):
        term = xs_scr[pl.ds(DL_HALO - pad + t, tt), :] * cw_ref[t:t + 1, :]
        y = term if y is None else y + term
    y = y * jax.nn.sigmoid(y)
    for idx in range(N_QKV_HEADS):
        xh = y[:, idx * HEAD_DIM:(idx + 1) * HEAD_DIM]
        if idx < 2 * D_HEADS:
            xh = xh * lax.rsqrt(jnp.sum(xh * xh, axis=-1, keepdims=True) + EPS)
        if idx < D_HEADS:
            xh = xh * ATT_SCALE
        qkv_ref[idx] = xh
    zab = zab_ref[...]
    lane = lax.broadcasted_iota(jnp.int32, zab.shape, 1)
    t_ = zab + dtb_ref[...]
    softplus = jnp.maximum(t_, 0.0) + jnp.log(1.0 + jnp.exp(-jnp.abs(t_)))
    gate_ref[...] = jnp.where(lane < 2 * D_HEADS, na_ref[...] * softplus, jax.nn.sigmoid(zab))


def delta_prep(z, conv_w, a_log, dt_bias):
    tt = DL_PREP_TT
    hb = tt // DL_HALO
    n_hb = N_TOK // DL_HALO
    pad8 = lambda v: jnp.concatenate([v.reshape(1, 2 * D_HEADS), jnp.zeros((1, 128 - 2 * D_HEADS), F32)], axis=1)
    return pl.pallas_call(
        _delta_prep_kernel,
        grid=(N_TOK // tt,),
        in_specs=[pl.BlockSpec((tt, 768), lambda i: (i, 3)),
                  pl.BlockSpec((DL_HALO, 768), lambda i: (jnp.maximum(i * hb - 1, 0), 3)),
                  pl.BlockSpec((DL_HALO, 768), lambda i: (jnp.minimum((i + 1) * hb, n_hb - 1), 3)),
                  pl.BlockSpec((tt, 128), lambda i: (i, Z_DAB // 128)),
                  pl.BlockSpec((CONV_K, 768), lambda i: (0, 0)),
                  pl.BlockSpec((1, 128), lambda i: (0, 0)),
                  pl.BlockSpec((1, 128), lambda i: (0, 0))],
        out_specs=[pl.BlockSpec((N_QKV_HEADS, tt, HEAD_DIM), lambda i: (0, i, 0)),
                   pl.BlockSpec((tt, 128), lambda i: (i, 0))],
        out_shape=[jax.ShapeDtypeStruct((N_QKV_HEADS, N_TOK, HEAD_DIM), F32),
                   jax.ShapeDtypeStruct((N_TOK, 128), F32)],
        scratch_shapes=[pltpu.VMEM((tt + 2 * DL_HALO, 768), F32)],
        compiler_params=_params(("arbitrary",)),
        name="delta_prep",
    )(z, z, z, z, conv_w, pad8(-jnp.exp(a_log)), pad8(dt_bias))


def _split_bf16(a):
    hi = a.astype(BF16)
    return hi, (a - hi.astype(F32)).astype(BF16)


def _dot_hl(a_parts, b_parts):
    (a_hi, a_lo), (b_hi, b_lo) = a_parts, b_parts
    m = a_hi.shape[0]
    r = jnp.dot(jnp.concatenate([a_hi, a_lo], axis=0), b_hi, preferred_element_type=F32)
    return r[:m] + r[m:] + jnp.dot(a_hi, b_lo, preferred_element_type=F32)


def _delta_chunk_kernel(qkv_ref, gate_ref, u2_ref, wq_ref, ak_ref):
    C = DL_C
    n_c = gate_ref.shape[0] // C
    row = lax.broadcasted_iota(jnp.int32, (C, C), 0)
    col = lax.broadcasted_iota(jnp.int32, (C, C), 1)
    eye = (row == col).astype(F32)
    t_idx = lax.broadcasted_iota(jnp.int32, (C, 128), 0)

    def chunk_chains(c):
        r0 = pl.multiple_of(c * C, C)
        ga = gate_ref[pl.ds(r0, C), :]
        chains = []
        for d in range(2):
            incl = (row >= col) if d == 0 else (row <= col)
            strict = (row > col) if d == 0 else (row < col)
            g_all = ga
            step = 1
            while step < C:
                if d == 0:
                    g_all = g_all + jnp.where(t_idx >= step, pltpu.roll(g_all, step, 0), 0.0)
                else:
                    g_all = g_all + jnp.where(t_idx < C - step, pltpu.roll(g_all, C - step, 0), 0.0)
                step *= 2
            g_all_t = g_all.T
            for h in range(D_HEADS):
                ci = d * D_HEADS + h
                q = qkv_ref[h, pl.ds(r0, C), :]
                k = qkv_ref[D_HEADS + h, pl.ds(r0, C), :]
                v = qkv_ref[2 * D_HEADS + h, pl.ds(r0, C), :]
                g_col = g_all[:, ci:ci + 1]
                g_row = g_all_t[ci:ci + 1, :]
                beta = ga[:, 2 * D_HEADS + ci:2 * D_HEADS + ci + 1]
                g_end = g_col[C - 1:C, :] if d == 0 else g_col[0:1, :]
                kb = k * beta
                eg = jnp.exp(g_col)
                decay = jnp.where(incl, jnp.exp(jnp.where(incl, g_col - g_row, 0.0)), 0.0)
                kq = jnp.concatenate([kb, q], axis=0).astype(BF16)
                chains.append(dict(
                    strict=strict, decay=decay, qg=q * eg, g_end=g_end,
                    r=lax.dot_general(kq, k.astype(BF16), _NT, preferred_element_type=F32),
                    rhs=jnp.concatenate([v * beta, kb * eg], axis=1).astype(BF16),
                    ke_t=(k * jnp.exp(g_end - g_col)).T))
        return r0, chains

    def body(gi, carry):
        groups = [(gi * DL_GROUP + cc,) + chunk_chains(gi * DL_GROUP + cc) for cc in range(DL_GROUP)]
        chains = [ch for _, _, chs in groups for ch in chs]
        for ch in chains:
            ch['p'] = -jnp.where(ch['strict'], ch['r'][:C] * ch['decay'], 0.0)
            ch['t'] = eye + ch['p']
        for _ in range(5):
            for ch in chains:
                parts = _split_bf16(ch['p'])
                ch['p'] = _dot_hl(parts, parts)
            for ch in chains:
                ch['t'] = ch['t'] + _dot_hl(_split_bf16(ch['t']), _split_bf16(ch['p']))
        for ch in chains:
            ch['uw'] = jnp.dot(ch['t'].astype(BF16), ch['rhs'], preferred_element_type=F32)
        pack = lambda xs: jnp.stack(xs).reshape((2, D_HEADS) + xs[0].shape)
        for c, r0, chs in groups:
            u2 = [jnp.concatenate([ch['uw'][:, :C], jnp.broadcast_to(jnp.exp(ch['g_end']), (C, C))], axis=1)
                  for ch in chs]
            wq = [jnp.concatenate([ch['uw'][:, C:], ch['qg']], axis=0).astype(BF16) for ch in chs]
            ak = [jnp.concatenate([ch['r'][C:] * ch['decay'], ch['ke_t']], axis=0).astype(BF16) for ch in chs]
            u2_ref[:, :, pl.ds(r0, C), :] = pack(u2)
            wq_ref[:, :, c] = pack(wq)
            ak_ref[:, :, c] = pack(ak)
        return carry

    lax.fori_loop(0, n_c // DL_GROUP, body, 0)


def delta_chunks(qkv, gates):
    tt = DL_CHUNK_TT
    n_c = tt // DL_C
    return pl.pallas_call(
        _delta_chunk_kernel,
        grid=(N_TOK // tt,),
        in_specs=[pl.BlockSpec((N_QKV_HEADS, tt, HEAD_DIM), lambda i: (0, i, 0)),
                  pl.BlockSpec((tt, 128), lambda i: (i, 0))],
        out_specs=[pl.BlockSpec((2, D_HEADS, tt, 128), lambda i: (0, 0, i, 0)),
                   pl.BlockSpec((2, D_HEADS, n_c, 2 * DL_C, HEAD_DIM), lambda i: (0, 0, i, 0, 0)),
                   pl.BlockSpec((2, D_HEADS, n_c, 2 * DL_C, HEAD_DIM), lambda i: (0, 0, i, 0, 0))],
        out_shape=[jax.ShapeDtypeStruct((2, D_HEADS, N_TOK, 128), F32),
                   jax.ShapeDtypeStruct((2, D_HEADS, N_TOK // DL_C, 2 * DL_C, HEAD_DIM), BF16),
                   jax.ShapeDtypeStruct((2, D_HEADS, N_TOK // DL_C, 2 * DL_C, HEAD_DIM), BF16)],
        compiler_params=_params(("arbitrary",)),
        name="delta_chunks",
    )(qkv, gates)


def _delta_scan_kernel(u2f_ref, wqf_ref, akf_ref, u2b_ref, wqb_ref, akb_ref, s0_ref,
                       of_ref, ob_ref, s_ref, s_scr):
    j = pl.program_id(1)
    C = DL_C
    n_c = wqf_ref.shape[1]

    @pl.when(j == 0)
    def _():
        s_scr[...] = s0_ref[...]

    def body(ci, carry):
        chains = []
        for d, (u2_ref, wq_ref, ak_ref, o_ref) in enumerate(((u2f_ref, wqf_ref, akf_ref, of_ref),
                                                            (u2b_ref, wqb_ref, akb_ref, ob_ref))):
            c = ci if d == 0 else n_c - 1 - ci
            r0 = pl.multiple_of(c * C, C)
            for h in range(D_HEADS):
                s = s_scr[d, h]
                chains.append(dict(d=d, h=h, c=c, r0=r0, s=s, ak_ref=ak_ref, o_ref=o_ref,
                                   u2=u2_ref[h, pl.ds(r0, C), :],
                                   r1=jnp.dot(wq_ref[h, c], s.astype(BF16),
                                              preferred_element_type=F32)))
        for ch in chains:
            v_new = ch['u2'][:, :C] - ch['r1'][:C]
            ch['r2'] = jnp.dot(ch['ak_ref'][ch['h'], ch['c']], v_new.astype(BF16),
                               preferred_element_type=F32)
        for ch in chains:
            ch['o_ref'][ch['h'], pl.ds(ch['r0'], C), :] = ch['r1'][C:] + ch['r2'][:C]
            s_scr[ch['d'], ch['h']] = ch['s'] * ch['u2'][0:1, C:] + ch['r2'][C:]
        return carry

    lax.fori_loop(0, n_c, body, 0)

    @pl.when(j == pl.num_programs(1) - 1)
    def _():
        s_ref[...] = s_scr[...]


def delta_scan(u2, wq, ak, row0, nb, L, s0):
    tt = min(L, 512)
    n_t = L // tt
    n_c = tt // DL_C
    rb = row0 // tt
    fwd = lambda b, j: rb + b * n_t + j
    bwd = lambda b, j: rb + b * n_t + (n_t - 1 - j)
    u_spec = lambda d, f: pl.BlockSpec((None, D_HEADS, tt, 128), lambda b, j: (d, 0, f(b, j), 0))
    c_spec = lambda d, f: pl.BlockSpec((None, D_HEADS, n_c, 2 * DL_C, HEAD_DIM), lambda b, j: (d, 0, f(b, j), 0, 0))
    st_spec = pl.BlockSpec((None, 2, D_HEADS, HEAD_DIM, HEAD_DIM), lambda b, j: (b, 0, 0, 0, 0))
    return pl.pallas_call(
        _delta_scan_kernel,
        grid=(nb, n_t),
        in_specs=[u_spec(0, fwd), c_spec(0, fwd), c_spec(0, fwd),
                  u_spec(1, bwd), c_spec(1, bwd), c_spec(1, bwd), st_spec],
        out_specs=[pl.BlockSpec((D_HEADS, tt, HEAD_DIM), lambda b, j: (0, b * n_t + j, 0)),
                   pl.BlockSpec((D_HEADS, tt, HEAD_DIM), lambda b, j: (0, b * n_t + (n_t - 1 - j), 0)),
                   st_spec],
        out_shape=[jax.ShapeDtypeStruct((D_HEADS, nb * L, HEAD_DIM), F32),
                   jax.ShapeDtypeStruct((D_HEADS, nb * L, HEAD_DIM), F32),
                   jax.ShapeDtypeStruct((nb, 2, D_HEADS, HEAD_DIM, HEAD_DIM), F32)],
        scratch_shapes=[pltpu.VMEM((2, D_HEADS, HEAD_DIM, HEAD_DIM), F32)],
        compiler_params=_params(("arbitrary", "arbitrary")),
        name="delta_scan",
    )(u2, wq, ak, u2, wq, ak, s0)


def kernel(x_prompt, x_sample, cache_attn_a_k, cache_attn_a_v, cache_attn_b_k, cache_attn_b_v,
           state_hgrn, state_delta, c, c_ctx, norm1_g, norm2_g, w_ada, b_ada, w_in, a_sink,
           b_qnorm_g, b_knorm_g, c_lb, c_onorm_g, d_conv, d_a_log, d_dt_bias, d_onorm_g,
           w_branch, w_out, ffn_w1, ffn_w3, ffn_w2, router_w, router_b, moe_w1, moe_w3, moe_w2,
           final_norm_g):
    cum = jnp.cumsum(jax.nn.softmax(c_lb, axis=0), axis=0)
    lower_bounds = cum - cum[:1]

    xs = (x_prompt.reshape(CTX_TOK, D_MODEL), x_sample.reshape(LAT_TOK, D_MODEL), 0)
    cond = jnp.concatenate([c_ctx[None, :], c, jnp.zeros((16 - N_COND, D_MODEL), F32)], axis=0)

    rope_c, rope_s = rope_lane_tables(DEC_SEQ)
    caches = []
    for l in range(DEPTH):
        mod = ada_modulation(cond, w_ada[l], b_ada[l])[:N_COND].reshape(N_COND, 6, D_MODEL)
        w_mix = jnp.concatenate([w_in[l][:, :Z_MAIN], w_in[l][:, Z_MAIN + 16:W_IN_MIX],
                                 w_in[l][:, Z_MAIN:Z_MAIN + 16], jnp.zeros((D_MODEL, 128 - 16), F32)],
                                axis=1).astype(BF16)
        w_gl = w_in[l][:, W_IN_MIX:].reshape(D_MODEL, N_BRANCH, D_MODEL).transpose(1, 0, 2).astype(BF16)
        z = input_projection(xs, mod, norm1_g[l], w_mix)
        kv2 = lambda t: t.reshape(DEC_BATCH, PAST_LEN, 128)
        o_ab_ctx, bk_ctx = ctx_attention(z, a_sink[l], b_qnorm_g[l], b_knorm_g[l])
        o_a_lat = latent_attention_a(z, CTX_TOK, DEC_BATCH, DEC_SEQ, a_sink[l], kv2(cache_attn_a_k[:, l]),
                                     kv2(cache_attn_a_v[:, l]), rope_c, rope_s)
        o_b_lat = latent_attention_b(z, CTX_TOK, DEC_BATCH, DEC_SEQ, kv2(cache_attn_b_k[:, l]),
                                     kv2(cache_attn_b_v[:, l]), rope_c, rope_s, b_qnorm_g[l], b_knorm_g[l])
        o_c_ctx, sc_t = hgrn_mixer(z, 0, BATCH, SEQ, lower_bounds[l], c_onorm_g[l],
                                   jnp.zeros((BATCH, 2, C_HEADS, HEAD_DIM, HEAD_DIM), F32))
        o_c_lat, _ = hgrn_mixer(z, CTX_TOK, DEC_BATCH, DEC_SEQ, lower_bounds[l], c_onorm_g[l],
                                jnp.swapaxes(state_hgrn[:, l], -1, -2))

        qkv, gates = delta_prep(z, d_conv[l], d_a_log[l], d_dt_bias[l])
        u2, wq, ak = delta_chunks(qkv, gates)
        of_ctx, ob_ctx, sd = delta_scan(u2, wq, ak, 0, BATCH, SEQ,
                                        jnp.zeros((BATCH, 2, D_HEADS, HEAD_DIM, HEAD_DIM), F32))
        of_lat, ob_lat, _ = delta_scan(u2, wq, ak, CTX_TOK, DEC_BATCH, DEC_SEQ, state_delta[:, l])
        kvh = lambda t: t.reshape(BATCH, SEQ, 2, HEAD_DIM)
        caches.append((kvh(z[:CTX_TOK, 256:384]), kvh(z[:CTX_TOK, 384:512]), kvh(bk_ctx), kvh(z[:CTX_TOK, 896:1024]),
                       jnp.swapaxes(sc_t, -1, -2), sd))
        x = merge_projection(xs, mod, norm1_g[l], (o_ab_ctx, o_c_ctx, of_ctx, ob_ctx),
                             (o_a_lat, o_b_lat, o_c_lat, of_lat, ob_lat), z, d_onorm_g[l],
                             w_gl, w_branch[l].astype(BF16), w_out[l].astype(BF16))
        j = l // 2
        if l % 2 == 0:
            x = dense_ffn(x, mod, norm2_g[l], ffn_w1[j].astype(BF16), ffn_w3[j].astype(BF16),
                          ffn_w2[j].astype(BF16))
            xs = (x, x, CTX_TOK // TM)
        else:
            assert l == DEPTH - 1, "the expert layer's residual is fused with the final norm"
            rw = jnp.concatenate([router_w[j], jnp.zeros((D_MODEL, 128 - N_EXPERTS), F32)], axis=1)
            rb = jnp.concatenate([router_b[j], jnp.zeros((128 - N_EXPERTS,), F32)])[None, :]
            h2, pos, wgt = moe_router(x, mod, norm2_g[l], rw, rb)
            f = moe_experts(h2, pos, wgt, moe_w1[j].astype(BF16), moe_w3[j].astype(BF16), moe_w2[j].astype(BF16))
            y_prompt = residual_final_norm(x, f, mod, final_norm_g, 0, CTX_TOK).reshape(BATCH, SEQ, D_MODEL)
            y_sample = residual_final_norm(x, f, mod, final_norm_g, CTX_TOK, LAT_TOK).reshape(DEC_BATCH, DEC_SEQ, D_MODEL)

    stack = lambda idx: jnp.stack([caches[l][idx] for l in range(DEPTH)], axis=1)
    return (y_prompt, y_sample, stack(0), stack(1), stack(2), stack(3), stack(4), stack(5))
```

```python
import functools

import jax
import jax.numpy as jnp
import numpy as np
from jax import lax
from jax.experimental import pallas as pl
from jax.experimental.pallas import tpu as pltpu

F32 = jnp.float32
BF16 = jnp.bfloat16

D_MODEL = 1024
BATCH = 32
SEQ = 256
DEPTH = 2
DEC_BATCH = 8
DEC_SEQ = 4096
PAST_LEN = 256
GRID_W = 64
HEAD_DIM = 64
A_HEADS = 4
A_KV_HEADS = 2
B_HEADS = 4
B_KV_HEADS = 2
C_HEADS = 4
D_HEADS = 4
BRANCH_W = 256
N_BRANCH = 4
WINDOW = 128
BLOCK = 128
ROPE_BASE = 10000.0
HGRN_CHUNK = 32
DELTA_CHUNK = 64
CONV_K = 5
D_FF = 2816
N_EXPERTS = 8
D_FF_EXPERT = 3584
EPS = 1e-6
NEG_INF = -1e30
F32_MIN = float(np.finfo(np.float32).min)

CTX_TOK = BATCH * SEQ
LAT_TOK = DEC_BATCH * DEC_SEQ
N_TOK = CTX_TOK + LAT_TOK
N_COND = 1 + DEC_BATCH

Z_MAIN = 3072
Z_DG = Z_MAIN
Z_DAB = Z_DG + BRANCH_W
Z_COLS = Z_DAB + 128
W_IN_MIX = 3344

TM = 512
VMEM_LIMIT = 56 * 1024 * 1024


def _tile_cond(i, tm):
    ctx_tiles = CTX_TOK // tm
    per_b = DEC_SEQ // tm
    return jnp.where(i < ctx_tiles, 0, 1 + (i - ctx_tiles) // per_b)


def _rms(x, g):
    return x * lax.rsqrt(jnp.mean(x * x, axis=-1, keepdims=True) + EPS) * g


def _params(sem):
    return pltpu.CompilerParams(dimension_semantics=sem, vmem_limit_bytes=VMEM_LIMIT)


def _ada_kernel(c_ref, w_ref, b_ref, o_ref):
    c = c_ref[...]
    s = c * jax.nn.sigmoid(c)
    o_ref[...] = jnp.dot(s.astype(BF16), w_ref[...].astype(BF16), preferred_element_type=F32) + b_ref[...]


def ada_modulation(cond_pad, w, b):
    n = 6 * D_MODEL
    tn = 1536
    return pl.pallas_call(
        _ada_kernel,
        grid=(n // tn,),
        in_specs=[pl.BlockSpec((16, D_MODEL), lambda j: (0, 0)),
                  pl.BlockSpec((D_MODEL, tn), lambda j: (0, j)),
                  pl.BlockSpec((1, tn), lambda j: (0, j))],
        out_specs=pl.BlockSpec((16, tn), lambda j: (0, j)),
        out_shape=jax.ShapeDtypeStruct((16, n), F32),
        compiler_params=_params(("arbitrary",)),
        name="ada_modulation",
    )(cond_pad, w, b.reshape(1, n))


def _x_specs(xs):
    ctx_tiles = CTX_TOK // TM
    lat_tile0 = xs[2]
    return [pl.BlockSpec((TM, D_MODEL), lambda i: (jnp.minimum(i, ctx_tiles - 1), 0)),
            pl.BlockSpec((TM, D_MODEL), lambda i: (jnp.maximum(i - ctx_tiles, 0) + lat_tile0, 0))]


def _x_tile(xc_ref, xl_ref):
    return jnp.where(pl.program_id(0) < CTX_TOK // TM, xc_ref[...], xl_ref[...])


def _in_kernel(xc_ref, xl_ref, mod_ref, g_ref, w_ref, z_ref):
    h = _rms(_x_tile(xc_ref, xl_ref), g_ref[...]) * (1.0 + mod_ref[1:2, :]) + mod_ref[0:1, :]
    z_ref[...] = jnp.dot(h.astype(BF16), w_ref[...], preferred_element_type=F32)


def input_projection(xs, mod, g, w):
    nt = N_TOK // TM
    return pl.pallas_call(
        _in_kernel,
        grid=(nt,),
        in_specs=_x_specs(xs) + [
                  pl.BlockSpec((None, 6, D_MODEL), lambda i: (_tile_cond(i, TM), 0, 0)),
                  pl.BlockSpec((1, D_MODEL), lambda i: (0, 0)),
                  pl.BlockSpec((D_MODEL, Z_COLS), lambda i: (0, 0))],
        out_specs=pl.BlockSpec((TM, Z_COLS), lambda i: (i, 0)),
        out_shape=jax.ShapeDtypeStruct((N_TOK, Z_COLS), F32),
        compiler_params=_params(("arbitrary",)),
        name="input_projection",
    )(xs[0], xs[1], mod, g.reshape(1, D_MODEL), w)


def _merge_kernel(xc_ref, xl_ref, mod_ref, g_ref, ab_c_ref, c_c_ref, df_c_ref, db_c_ref, a_l_ref, b_l_ref, c_l_ref,
                  df_l_ref, db_l_ref, zg_ref, gd_ref, wgl_ref, wbr_ref, wout_ref, xo_ref):
    x = _x_tile(xc_ref, xl_ref)
    h = (_rms(x, g_ref[...]) * (1.0 + mod_ref[1:2, :]) + mod_ref[0:1, :]).astype(BF16)
    is_ctx = pl.program_id(0) < CTX_TOK // TM
    d_heads = []
    for hd in range(D_HEADS):
        o = jnp.where(is_ctx, df_c_ref[hd] + db_c_ref[hd], df_l_ref[hd] + db_l_ref[hd])
        d_heads.append(o * lax.rsqrt(jnp.mean(o * o, axis=-1, keepdims=True) + EPS) * gd_ref[...])
    dg = zg_ref[...]
    branches = (jnp.where(is_ctx, ab_c_ref[:, :BRANCH_W], a_l_ref[...]),
                jnp.where(is_ctx, ab_c_ref[:, BRANCH_W:], b_l_ref[...]),
                jnp.where(is_ctx, c_c_ref[...], c_l_ref[...]),
                (jnp.concatenate(d_heads, axis=-1) * (dg * jax.nn.sigmoid(dg))).astype(BF16))
    merged = None
    for j in range(N_BRANCH):
        gate = jax.nn.sigmoid(jnp.dot(h, wgl_ref[j], preferred_element_type=F32))
        br = jnp.dot(branches[j], wbr_ref[j], preferred_element_type=F32)
        merged = gate * br if merged is None else merged + gate * br
    mix = jnp.dot(merged.astype(BF16), wout_ref[...], preferred_element_type=F32)
    xo_ref[...] = x + mod_ref[2:3, :] * mix


def merge_projection(xs, mod, g, o_ctx, o_lat, z, gd, wgl, wbr, wout):
    nt = N_TOK // TM
    ctx_tiles = CTX_TOK // TM
    ctx_i = lambda i: jnp.minimum(i, ctx_tiles - 1)
    lat_i = lambda i: jnp.maximum(i - ctx_tiles, 0)
    ctx_spec = lambda w: pl.BlockSpec((TM, w), lambda i: (ctx_i(i), 0))
    lat_spec = pl.BlockSpec((TM, BRANCH_W), lambda i: (lat_i(i), 0))
    ctx_d_spec = pl.BlockSpec((D_HEADS, TM, HEAD_DIM), lambda i: (0, ctx_i(i), 0))
    lat_d_spec = pl.BlockSpec((D_HEADS, TM, HEAD_DIM), lambda i: (0, lat_i(i), 0))
    return pl.pallas_call(
        _merge_kernel,
        grid=(nt,),
        in_specs=_x_specs(xs) + [
                  pl.BlockSpec((None, 6, D_MODEL), lambda i: (_tile_cond(i, TM), 0, 0)),
                  pl.BlockSpec((1, D_MODEL), lambda i: (0, 0)),
                  ctx_spec(2 * BRANCH_W), ctx_spec(BRANCH_W), ctx_d_spec, ctx_d_spec,
                  lat_spec, lat_spec, lat_spec, lat_d_spec, lat_d_spec,
                  pl.BlockSpec((TM, BRANCH_W), lambda i: (i, Z_DG // BRANCH_W)),
                  pl.BlockSpec((1, HEAD_DIM), lambda i: (0, 0)),
                  pl.BlockSpec((N_BRANCH, D_MODEL, D_MODEL), lambda i: (0, 0, 0)),
                  pl.BlockSpec((N_BRANCH, BRANCH_W, D_MODEL), lambda i: (0, 0, 0)),
                  pl.BlockSpec((D_MODEL, D_MODEL), lambda i: (0, 0))],
        out_specs=pl.BlockSpec((TM, D_MODEL), lambda i: (i, 0)),
        out_shape=jax.ShapeDtypeStruct((N_TOK, D_MODEL), F32),
        compiler_params=_params(("arbitrary",)),
        name="merge_projection",
    )(xs[0], xs[1], mod, g.reshape(1, D_MODEL), *o_ctx, *o_lat, z, gd.reshape(1, HEAD_DIM), wgl, wbr, wout)


def _ffn_kernel(x_ref, mod_ref, g_ref, w1_ref, w3_ref, w2_ref, xo_ref):
    x = x_ref[...]
    h = (_rms(x, g_ref[...]) * (1.0 + mod_ref[4:5, :]) + mod_ref[3:4, :]).astype(BF16)
    a = jnp.dot(h, w1_ref[...], preferred_element_type=F32)
    b = jnp.dot(h, w3_ref[...], preferred_element_type=F32)
    hid = (a * jax.nn.sigmoid(a) * b).astype(BF16)
    f = jnp.dot(hid, w2_ref[...], preferred_element_type=F32)
    xo_ref[...] = x + mod_ref[5:6, :] * f


def dense_ffn(x, mod, g, w1, w3, w2):
    nt = N_TOK // TM
    const = lambda i: (0, 0)
    return pl.pallas_call(
        _ffn_kernel,
        grid=(nt,),
        in_specs=[pl.BlockSpec((TM, D_MODEL), lambda i: (i, 0)),
                  pl.BlockSpec((None, 6, D_MODEL), lambda i: (_tile_cond(i, TM), 0, 0)),
                  pl.BlockSpec((1, D_MODEL), const),
                  pl.BlockSpec((D_MODEL, D_FF), const, pipeline_mode=pl.Buffered(1)),
                  pl.BlockSpec((D_MODEL, D_FF), const, pipeline_mode=pl.Buffered(1)),
                  pl.BlockSpec((D_FF, D_MODEL), const, pipeline_mode=pl.Buffered(1))],
        out_specs=pl.BlockSpec((TM, D_MODEL), lambda i: (i, 0)),
        out_shape=jax.ShapeDtypeStruct((N_TOK, D_MODEL), F32),
        compiler_params=_params(("arbitrary",)),
        name="dense_ffn",
    )(x, mod, g.reshape(1, D_MODEL), w1, w3, w2)


MOE_T = 1024
MOE_R = 144
MOE_SR = 128
MOE_F = 1792
MOE_CAP = -(-MOE_T // MOE_R) * MOE_R


def _router_kernel(x_ref, mod_ref, g_ref, rw_ref, rb_ref, before_ref, h_ref, pos_ref, wgt_ref):
    h = _rms(x_ref[...], g_ref[...]) * (1.0 + mod_ref[4:5, :]) + mod_ref[3:4, :]
    h_ref[...] = h.astype(BF16)
    logits = jnp.dot(h, rw_ref[...], preferred_element_type=F32, precision=lax.Precision.HIGHEST) + rb_ref[...]
    lt = logits.T[:N_EXPERTS, :]
    eidx = lax.broadcasted_iota(jnp.int32, lt.shape, 0)
    m1 = jnp.max(lt, axis=0, keepdims=True)
    i1 = jnp.min(jnp.where(lt == m1, eidx, N_EXPERTS), axis=0, keepdims=True)
    rest = jnp.where(eidx == i1, F32_MIN, lt)
    m2 = jnp.max(rest, axis=0, keepdims=True)
    i2 = jnp.min(jnp.where(rest == m2, eidx, N_EXPERTS), axis=0, keepdims=True)
    e2 = jnp.exp(m2 - m1)
    p1 = 1.0 / (1.0 + e2)
    p2 = e2 / (1.0 + e2)
    wgt_ref[...] = jnp.where(eidx == i1, p1, 0.0) + jnp.where(eidx == i2, p2, 0.0)
    routed = jnp.where(eidx == i1, 1.0, jnp.where(eidx == i2, 1.0, 0.0))
    rank = jnp.dot(routed.astype(BF16), before_ref[...], preferred_element_type=F32)
    pos_ref[...] = jnp.where(routed > 0.0, rank.astype(jnp.int32), -1)


def moe_router(x, mod, g, rw, rb):
    nt = N_TOK // MOE_T
    tok = jnp.arange(MOE_T, dtype=jnp.int32)
    before = (tok[:, None] < tok[None, :]).astype(BF16)
    return pl.pallas_call(
        _router_kernel,
        grid=(nt,),
        in_specs=[pl.BlockSpec((MOE_T, D_MODEL), lambda i: (i, 0)),
                  pl.BlockSpec((None, 6, D_MODEL), lambda i: (_tile_cond(i, MOE_T), 0, 0)),
                  pl.BlockSpec((1, D_MODEL), lambda i: (0, 0)),
                  pl.BlockSpec((D_MODEL, 128), lambda i: (0, 0)),
                  pl.BlockSpec((1, 128), lambda i: (0, 0)),
                  pl.BlockSpec((MOE_T, MOE_T), lambda i: (0, 0))],
        out_specs=[pl.BlockSpec((MOE_T, D_MODEL), lambda i: (i, 0)),
                   pl.BlockSpec((N_EXPERTS, MOE_T), lambda i: (0, i)),
                   pl.BlockSpec((N_EXPERTS, MOE_T), lambda i: (0, i))],
        out_shape=[jax.ShapeDtypeStruct((N_TOK, D_MODEL), BF16),
                   jax.ShapeDtypeStruct((N_EXPERTS, N_TOK), jnp.int32),
                   jax.ShapeDtypeStruct((N_EXPERTS, N_TOK), F32)],
        compiler_params=_params(("arbitrary",)),
        name="moe_router",
    )(x, mod, g.reshape(1, D_MODEL), rw, rb, before)


def _moe_sparse_kernel(h_ref, pos_ref, wgt_ref, w1_ref, w3_ref, w2_ref, y_ref, xg_scr, acc_scr, wr_scr):
    e = pl.program_id(1)
    f = pl.program_id(2)
    t = h_ref.shape[0]
    pos_e = pos_ref[pl.ds(e, 1), :]
    n_rows = jnp.max(pos_e) + 1
    n_blocks = (n_rows + MOE_R - 1) // MOE_R
    n_sc_blocks = (n_rows + MOE_SR - 1) // MOE_SR
    n_init_blocks = jnp.maximum(n_blocks, (n_sc_blocks * MOE_SR + MOE_R - 1) // MOE_R)
    row_id = lax.broadcasted_iota(jnp.int32, (MOE_R, t), 0)

    def block_rows(r):
        return pl.ds(pl.multiple_of(r * MOE_R, 16), MOE_R)

    def selects(r):
        return pos_e == row_id + r * MOE_R

    @pl.when((e == 0) & (f == 0))
    def _():
        y_ref[...] = jnp.zeros_like(y_ref)

    @pl.when(f == 0)
    def _():
        wgt_e = wgt_ref[pl.ds(e, 1), :]

        def gather(r, carry):
            sel = selects(r)
            xg = jnp.dot(jnp.where(sel, 1.0, 0.0).astype(BF16), h_ref[...], preferred_element_type=F32)
            xg_scr[block_rows(r), :] = xg.astype(BF16)
            w_rows = jnp.sum(jnp.where(sel, wgt_e, 0.0), axis=1, keepdims=True)
            wr_scr[block_rows(r), :] = jnp.broadcast_to(w_rows, (MOE_R, 128))
            acc_scr[block_rows(r), :] = jnp.zeros((MOE_R, D_MODEL), F32)
            return carry

        lax.fori_loop(0, n_init_blocks, gather, 0)

    def expert(r, carry):
        xg = xg_scr[block_rows(r), :]
        a = jnp.dot(xg, w1_ref[...], preferred_element_type=F32)
        b = jnp.dot(xg, w3_ref[...], preferred_element_type=F32)
        hid = (a * jax.nn.sigmoid(a) * b * wr_scr[block_rows(r), 0:1]).astype(BF16)
        acc_scr[block_rows(r), :] += jnp.dot(hid, w2_ref[...], preferred_element_type=F32)
        return carry

    lax.fori_loop(0, n_blocks, expert, 0)

    @pl.when(f == pl.num_programs(2) - 1)
    def _():
        sc_row_id = lax.broadcasted_iota(jnp.int32, (MOE_SR, t), 0)

        def scatter(r, carry):
            rows = pl.ds(pl.multiple_of(r * MOE_SR, MOE_SR), MOE_SR)
            onehot = jnp.where(pos_e == sc_row_id + r * MOE_SR, 1.0, 0.0).astype(BF16)
            hi, lo = _split_bf16(acc_scr[rows, :])
            y_ref[...] += lax.dot_general(jnp.concatenate([onehot, onehot], axis=0),
                                          jnp.concatenate([hi, lo], axis=0), _TN, preferred_element_type=F32)
            return carry

        lax.fori_loop(0, n_sc_blocks, scatter, 0)


def moe_experts(h2, pos, wgt, w1, w3, w2):
    nt = N_TOK // MOE_T
    nf = D_FF_EXPERT // MOE_F
    return pl.pallas_call(
        _moe_sparse_kernel,
        grid=(nt, N_EXPERTS, nf),
        in_specs=[pl.BlockSpec((MOE_T, D_MODEL), lambda i, e, f: (i, 0)),
                  pl.BlockSpec((N_EXPERTS, MOE_T), lambda i, e, f: (0, i)),
                  pl.BlockSpec((N_EXPERTS, MOE_T), lambda i, e, f: (0, i)),
                  pl.BlockSpec((None, D_MODEL, MOE_F), lambda i, e, f: (e, 0, f)),
                  pl.BlockSpec((None, D_MODEL, MOE_F), lambda i, e, f: (e, 0, f)),
                  pl.BlockSpec((None, MOE_F, D_MODEL), lambda i, e, f: (e, f, 0))],
        out_specs=pl.BlockSpec((MOE_T, D_MODEL), lambda i, e, f: (i, 0)),
        out_shape=jax.ShapeDtypeStruct((N_TOK, D_MODEL), F32),
        scratch_shapes=[pltpu.VMEM((MOE_CAP, D_MODEL), BF16),
                        pltpu.VMEM((MOE_CAP, D_MODEL), F32),
                        pltpu.VMEM((MOE_CAP, 128), F32)],
        compiler_params=_params(("arbitrary", "arbitrary", "arbitrary")),
        name="moe_experts",
    )(h2, pos, wgt, w1, w3, w2)


def _residual_norm_kernel(x_ref, y_ref, mod_ref, g_ref, o_ref):
    o_ref[...] = _rms(x_ref[...] + mod_ref[5:6, :] * y_ref[...], g_ref[...])


def residual_final_norm(x, y, mod, g, row0, n_rows):
    tm = 1024
    t0 = row0 // tm
    return pl.pallas_call(
        _residual_norm_kernel,
        grid=(n_rows // tm,),
        in_specs=[pl.BlockSpec((tm, D_MODEL), lambda i: (t0 + i, 0)),
                  pl.BlockSpec((tm, D_MODEL), lambda i: (t0 + i, 0)),
                  pl.BlockSpec((None, 6, D_MODEL), lambda i: (_tile_cond(t0 + i, tm), 0, 0)),
                  pl.BlockSpec((1, D_MODEL), lambda i: (0, 0))],
        out_specs=pl.BlockSpec((tm, D_MODEL), lambda i: (i, 0)),
        out_shape=jax.ShapeDtypeStruct((n_rows, D_MODEL), F32),
        compiler_params=_params(("arbitrary",)),
        name="residual_final_norm",
    )(x, y, mod, g.reshape(1, D_MODEL))


ATT_SCALE = HEAD_DIM ** -0.5
LOG2_E = 1.4426950408889634
_NT = (((1,), (1,)), ((), ()))


def _head_rms(x, g_row):
    outs = []
    for h in range(x.shape[1] // HEAD_DIM):
        xh = x[:, h * HEAD_DIM:(h + 1) * HEAD_DIM]
        outs.append(xh * lax.rsqrt(jnp.mean(xh * xh, axis=-1, keepdims=True) + EPS) * g_row)
    return jnp.concatenate(outs, axis=-1)


def _rope_apply(x, c, s):
    w = x.shape[-1]
    lane = lax.broadcasted_iota(jnp.int32, x.shape, 1)
    first_half = ((lane // (HEAD_DIM // 4)) % 2) == 0
    partner = jnp.where(first_half, pltpu.roll(x, w - HEAD_DIM // 4, 1), pltpu.roll(x, HEAD_DIM // 4, 1))
    return x * c + partner * s


def rope_lane_tables(L):
    rows = L // GRID_W
    row = jnp.repeat(jnp.arange(rows, dtype=F32), GRID_W)
    col = jnp.tile(jnp.arange(GRID_W, dtype=F32), rows)
    n_freq = HEAD_DIM // 4
    inv = ROPE_BASE ** (-jnp.arange(n_freq, dtype=F32) / n_freq)
    ang = jnp.stack([row, col], 0)[:, :, None] * inv
    cos, sin = jnp.cos(ang), jnp.sin(ang)
    c = jnp.concatenate([cos[0], cos[0], cos[1], cos[1]], axis=-1)
    s = jnp.concatenate([-sin[0], sin[0], -sin[1], sin[1]], axis=-1)
    return jnp.tile(c, (1, 4)), jnp.tile(s, (1, 4))


def _with_ones(v):
    ones = jnp.ones((v.shape[0], HEAD_DIM), BF16)
    parts = []
    for h in range(v.shape[1] // HEAD_DIM):
        parts += [v[:, h * HEAD_DIM:(h + 1) * HEAD_DIM].astype(BF16), ones]
    return jnp.concatenate(parts, axis=-1)


def _attend_heads(jobs):
    for job in jobs:
        q = (job['q'] * (ATT_SCALE * LOG2_E)).astype(BF16)
        job['s'] = lax.dot_general(q, job['k'], _NT, preferred_element_type=F32)
        if job.get('extra') is not None:
            job['s2'] = lax.dot_general(q, job['extra'][0], _NT, preferred_element_type=F32)
    outs = []
    for job in jobs:
        s, sink = job['s'], job.get('sink')
        if sink is not None:
            sink = sink * LOG2_E
        if job.get('mask') is not None:
            s = jnp.where(job['mask'], s, NEG_INF)
        m = jnp.max(s, axis=-1, keepdims=True)
        if 's2' in job:
            m = jnp.maximum(m, jnp.max(job['s2'], axis=-1, keepdims=True))
        if sink is not None:
            m = jnp.maximum(m, sink)
        o = jnp.dot(jnp.exp2(s - m).astype(BF16), job['v'], preferred_element_type=F32)
        if 's2' in job:
            o = o + jnp.dot(jnp.exp2(job['s2'] - m).astype(BF16), job['extra'][1], preferred_element_type=F32)
        den = o[:, HEAD_DIM:HEAD_DIM + 1]
        if sink is not None:
            den = den + jnp.exp2(sink - m)
        outs.append(o[:, :HEAD_DIM] / den)
    return outs


def _ctx_attn_kernel(sink_ref, z_ref, gq_ref, gk_ref, o_ref, bk_ref):
    z = z_ref[...]
    bq = _head_rms(z[:, 512:768], gq_ref[...])
    bk = _head_rms(z[:, 768:896], gk_ref[...])
    bk_ref[...] = bk
    groups = ((z[:, 0:256], z[:, 256:384], z[:, 384:512], True),
              (bq, bk, z[:, 896:1024], False))
    outs = []
    for q_all, k_all, v_all, use_sink in groups:
        k_all = k_all.astype(BF16)
        v_all = _with_ones(v_all)
        for hq in range(A_HEADS):
            kv = hq // (A_HEADS // A_KV_HEADS)
            outs.append(dict(q=q_all[:, hq * HEAD_DIM:(hq + 1) * HEAD_DIM],
                             k=k_all[:, kv * HEAD_DIM:(kv + 1) * HEAD_DIM],
                             v=v_all[:, kv * 2 * HEAD_DIM:(kv + 1) * 2 * HEAD_DIM],
                             sink=sink_ref[hq] if use_sink else None))
    o_ref[...] = jnp.concatenate(_attend_heads(outs), axis=-1).astype(o_ref.dtype)


def ctx_attention(z, sink, gq, gk):
    return pl.pallas_call(
        _ctx_attn_kernel,
        grid=(BATCH,),
        in_specs=[pl.BlockSpec(memory_space=pltpu.SMEM),
                  pl.BlockSpec((SEQ, 1024), lambda b: (b, 0)),
                  pl.BlockSpec((1, HEAD_DIM), lambda b: (0, 0)),
                  pl.BlockSpec((1, HEAD_DIM), lambda b: (0, 0))],
        out_specs=[pl.BlockSpec((SEQ, 512), lambda b: (b, 0)),
                   pl.BlockSpec((SEQ, 128), lambda b: (b, 0))],
        out_shape=[jax.ShapeDtypeStruct((CTX_TOK, 512), BF16),
                   jax.ShapeDtypeStruct((CTX_TOK, 128), F32)],
        compiler_params=_params(("arbitrary",)),
        name="ctx_attention",
    )(sink, z, gq.reshape(1, HEAD_DIM), gk.reshape(1, HEAD_DIM))


LB_TQ = 256


def _lat_b_kernel(zq_ref, zkv_ref, ck_ref, cv_ref, cq_ref, sq_ref, ckk_ref, skk_ref, gq_ref, gk_ref,
                  o_ref, k_scr, v_scr):
    L = zkv_ref.shape[0]

    @pl.when(pl.program_id(1) == 0)
    def _():
        kv = zkv_ref[...]
        bk = _rope_apply(_head_rms(kv[:, :128], gk_ref[...]), ckk_ref[...], skk_ref[...])
        k_scr[0:L, :] = bk.astype(BF16)
        k_scr[L:L + PAST_LEN, :] = ck_ref[...].astype(BF16)
        v_scr[0:L, :] = _with_ones(kv[:, 128:])
        v_scr[L:L + PAST_LEN, :] = _with_ones(cv_ref[...])

    q = _rope_apply(_head_rms(zq_ref[...], gq_ref[...]), cq_ref[...], sq_ref[...])
    outs = []
    for hq in range(B_HEADS):
        kv = hq // (B_HEADS // B_KV_HEADS)
        sl = slice(kv * HEAD_DIM, (kv + 1) * HEAD_DIM)
        outs.append(dict(q=q[:, hq * HEAD_DIM:(hq + 1) * HEAD_DIM], k=k_scr[:, sl],
                         v=v_scr[:, kv * 2 * HEAD_DIM:(kv + 1) * 2 * HEAD_DIM]))
    o_ref[...] = jnp.concatenate(_attend_heads(outs), axis=-1).astype(o_ref.dtype)


def latent_attention_b(z, row0, nb, L, cache_k, cache_v, rope_c, rope_s, gq, gk):
    nq = L // LB_TQ
    return pl.pallas_call(
        _lat_b_kernel,
        grid=(nb, nq),
        in_specs=[pl.BlockSpec((LB_TQ, 256), lambda b, i: (row0 // LB_TQ + b * nq + i, 2)),
                  pl.BlockSpec((L, 256), lambda b, i: (row0 // L + b, 3)),
                  pl.BlockSpec((None, PAST_LEN, 128), lambda b, i: (b, 0, 0)),
                  pl.BlockSpec((None, PAST_LEN, 128), lambda b, i: (b, 0, 0)),
                  pl.BlockSpec((LB_TQ, 256), lambda b, i: (i, 0)),
                  pl.BlockSpec((LB_TQ, 256), lambda b, i: (i, 0)),
                  pl.BlockSpec((L, 128), lambda b, i: (0, 0)),
                  pl.BlockSpec((L, 128), lambda b, i: (0, 0)),
                  pl.BlockSpec((1, HEAD_DIM), lambda b, i: (0, 0)),
                  pl.BlockSpec((1, HEAD_DIM), lambda b, i: (0, 0))],
        out_specs=pl.BlockSpec((LB_TQ, 256), lambda b, i: (b * nq + i, 0)),
        out_shape=jax.ShapeDtypeStruct((nb * L, 256), BF16),
        scratch_shapes=[pltpu.VMEM((L + PAST_LEN, 128), BF16),
                        pltpu.VMEM((L + PAST_LEN, 256), BF16)],
        compiler_params=_params(("arbitrary", "arbitrary")),
        name="latent_attention_b",
    )(z, z, cache_k, cache_v, rope_c, rope_s, rope_c, rope_s, gq.reshape(1, HEAD_DIM), gk.reshape(1, HEAD_DIM))


def _lat_a_kernel(sink_ref, zq_ref, zkv_ref, ck_ref, cv_ref, cq_ref, sq_ref, ckk_ref, skk_ref,
                  o_ref, k_scr, v_scr, ck_scr, cv_scr):
    L = zkv_ref.shape[0]
    i = pl.program_id(1)

    @pl.when(i == 0)
    def _():
        kv = zkv_ref[...]
        k_scr[0:BLOCK, :] = jnp.zeros((BLOCK, 128), BF16)
        v_scr[0:BLOCK, :] = jnp.zeros((BLOCK, 256), BF16)
        k_scr[BLOCK:BLOCK + L, :] = _rope_apply(kv[:, :128], ckk_ref[...], skk_ref[...]).astype(BF16)
        v_scr[BLOCK:BLOCK + L, :] = _with_ones(kv[:, 128:])
        k_scr[BLOCK + L:2 * BLOCK + L, :] = jnp.zeros((BLOCK, 128), BF16)
        v_scr[BLOCK + L:2 * BLOCK + L, :] = jnp.zeros((BLOCK, 256), BF16)
        ck_scr[...] = ck_ref[...].astype(BF16)
        cv_scr[...] = _with_ones(cv_ref[...])

    q = _rope_apply(zq_ref[...], cq_ref[...], sq_ref[...])
    start = pl.multiple_of(i * BLOCK, BLOCK)
    kband = k_scr[pl.ds(start, 3 * BLOCK), :]
    vband = v_scr[pl.ds(start, 3 * BLOCK), :]
    r = lax.broadcasted_iota(jnp.int32, (BLOCK, 3 * BLOCK), 0)
    cidx = lax.broadcasted_iota(jnp.int32, (BLOCK, 3 * BLOCK), 1)
    kpos = i * BLOCK - BLOCK + cidx
    mask = (jnp.abs(cidx - BLOCK - r) <= WINDOW) & (kpos >= 0) & (kpos < L)
    outs = []
    for hq in range(A_HEADS):
        kv = hq // (A_HEADS // A_KV_HEADS)
        sl = slice(kv * HEAD_DIM, (kv + 1) * HEAD_DIM)
        sv = slice(kv * 2 * HEAD_DIM, (kv + 1) * 2 * HEAD_DIM)
        outs.append(dict(q=q[:, hq * HEAD_DIM:(hq + 1) * HEAD_DIM], k=kband[:, sl], v=vband[:, sv],
                         extra=(ck_scr[:, sl], cv_scr[:, sv]), sink=sink_ref[hq], mask=mask))
    o_ref[...] = jnp.concatenate(_attend_heads(outs), axis=-1).astype(o_ref.dtype)


def latent_attention_a(z, row0, nb, L, sink, cache_k, cache_v, rope_c, rope_s):
    nq = L // BLOCK
    return pl.pallas_call(
        _lat_a_kernel,
        grid=(nb, nq),
        in_specs=[pl.BlockSpec(memory_space=pltpu.SMEM),
                  pl.BlockSpec((BLOCK, 256), lambda b, i: (row0 // BLOCK + b * nq + i, 0)),
                  pl.BlockSpec((L, 256), lambda b, i: (row0 // L + b, 1)),
                  pl.BlockSpec((None, PAST_LEN, 128), lambda b, i: (b, 0, 0)),
                  pl.BlockSpec((None, PAST_LEN, 128), lambda b, i: (b, 0, 0)),
                  pl.BlockSpec((BLOCK, 256), lambda b, i: (i, 0)),
                  pl.BlockSpec((BLOCK, 256), lambda b, i: (i, 0)),
                  pl.BlockSpec((L, 128), lambda b, i: (0, 0)),
                  pl.BlockSpec((L, 128), lambda b, i: (0, 0))],
        out_specs=pl.BlockSpec((BLOCK, 256), lambda b, i: (b * nq + i, 0)),
        out_shape=jax.ShapeDtypeStruct((nb * L, 256), BF16),
        scratch_shapes=[pltpu.VMEM((L + 2 * BLOCK, 128), BF16),
                        pltpu.VMEM((L + 2 * BLOCK, 256), BF16),
                        pltpu.VMEM((PAST_LEN, 128), BF16),
                        pltpu.VMEM((PAST_LEN, 256), BF16)],
        compiler_params=_params(("arbitrary", "arbitrary")),
        name="latent_attention_a",
    )(sink, z, z, cache_k, cache_v, rope_c, rope_s, rope_c, rope_s)


_TN = (((0,), (0,)), ((), ()))
HG_GROUP = 8


def _hgrn_kernel(zq_ref, zf_ref, zi_ref, zg_ref, lb_ref, gn_ref, s0_ref, o_ref, sT_ref,
                 of_scr, ob_scr, g_scr, k_scr, qin_scr, kin_scr, v_scr, S_scr, *, tt):
    d = pl.program_id(1)
    j = pl.program_id(2)
    n_t = pl.num_programs(2)
    C = HGRN_CHUNK
    n_c = tt // C

    @pl.when(j == 0)
    def _():
        S_scr[...] = s0_ref[...]

    lb = lb_ref[...]
    sg = jax.nn.sigmoid(zf_ref[...])
    logf = jnp.log(lb + (1.0 - lb) * sg)
    k = (1.0 - lb) * (1.0 - sg)
    k_scr[...] = k
    v_scr[...] = zi_ref[...].astype(BF16)
    in_chunk = lax.broadcasted_iota(jnp.int32, (tt, C_HEADS * HEAD_DIM), 0) % C
    row = lax.broadcasted_iota(jnp.int32, (C, C), 0)
    col = lax.broadcasted_iota(jnp.int32, (C, C), 1)
    heads = [slice(h * HEAD_DIM, (h + 1) * HEAD_DIM) for h in range(C_HEADS)]

    def run(reverse, tile):
        G = logf
        step = 1
        while step < C:
            if reverse:
                G = G + jnp.where(in_chunk < C - step, pltpu.roll(G, tt - step, 0), 0.0)
            else:
                G = G + jnp.where(in_chunk >= step, pltpu.roll(G, step, 0), 0.0)
            step *= 2
        g_scr[...] = G
        qin_scr[...] = (zq_ref[...] * jnp.exp(G)).astype(BF16)
        kin_scr[...] = (k * jnp.exp(-G)).astype(BF16)
        tri = (row <= col) if reverse else (row >= col)

        def body(gi, carry):
            chunks = []
            for g in range(HG_GROUP):
                ci = gi * HG_GROUP + g
                c = (n_c - 1 - ci) if reverse else ci
                r0 = pl.multiple_of(c * C, C)
                rows = pl.ds(r0, C)
                G_c = g_scr[rows, :]
                G_end = G_c[0:1, :] if reverse else G_c[C - 1:C, :]
                vc = v_scr[rows, :]
                q_in = qin_scr[rows, :]
                k_in = kin_scr[rows, :]
                k_out = (k_scr[rows, :] * jnp.exp(G_end - G_c)).astype(BF16)
                chunks.append(dict(
                    r0=r0, rows=rows, vc=vc, q_in=q_in, decay=jnp.exp(G_end),
                    a=[lax.dot_general(q_in[:, sl], k_in[:, sl], _NT, preferred_element_type=F32) for sl in heads],
                    kv=[lax.dot_general(vc[:, sl], k_out[:, sl], _TN, preferred_element_type=F32) for sl in heads]))
            s_cur = [S_scr[h] for h in range(C_HEADS)]
            for ch in chunks:
                ch['qs'] = [lax.dot_general(ch['q_in'][:, sl], s_cur[h].astype(BF16), _NT,
                                            preferred_element_type=F32) for h, sl in enumerate(heads)]
                s_cur = [s_cur[h] * ch['decay'][:, sl] + ch['kv'][h] for h, sl in enumerate(heads)]
            for h in range(C_HEADS):
                S_scr[h] = s_cur[h]
            for ch in chunks:
                o_c = jnp.concatenate(
                    [jnp.dot(jnp.where(tri, ch['a'][h], 0.0).astype(BF16), ch['vc'][:, sl],
                             preferred_element_type=F32) + ch['qs'][h] for h, sl in enumerate(heads)], axis=-1)
                if reverse:
                    ob_scr[ch['rows'], :] = o_c
                else:
                    of_scr[pl.ds(pl.multiple_of(tile * tt, tt) + ch['r0'], C), :] = o_c
            return carry

        lax.fori_loop(0, n_c // HG_GROUP, body, 0)

    @pl.when(d == 0)
    def _():
        run(False, j)

    @pl.when(d == 1)
    def _():
        tile = n_t - 1 - j
        run(True, tile)
        o = of_scr[pl.ds(pl.multiple_of(tile * tt, tt), tt), :] + ob_scr[...]
        g = zg_ref[...]
        o_ref[...] = (_head_rms(o, gn_ref[...]) * (g * jax.nn.sigmoid(g))).astype(o_ref.dtype)

    @pl.when(j == n_t - 1)
    def _():
        sT_ref[...] = S_scr[...]


def hgrn_mixer(z, row0, nb, L, lb, gn, s0_t):
    tt = min(L, 512)
    n_t = L // tt
    rb = row0 // tt

    def tile(d, j):
        return jnp.where(d == 0, j, n_t - 1 - j)

    def late(d, j):
        return jnp.where(d == 0, n_t - 1, n_t - 1 - j)

    st_spec = pl.BlockSpec((None, None, C_HEADS, HEAD_DIM, HEAD_DIM), lambda b, d, j: (b, d, 0, 0, 0))
    return pl.pallas_call(
        functools.partial(_hgrn_kernel, tt=tt),
        grid=(nb, 2, n_t),
        in_specs=[pl.BlockSpec((tt, 256), lambda b, d, j: (rb + b * n_t + tile(d, j), 4)),
                  pl.BlockSpec((tt, 256), lambda b, d, j: (rb + b * n_t + tile(d, j), 5 + d)),
                  pl.BlockSpec((tt, 256), lambda b, d, j: (rb + b * n_t + tile(d, j), 7)),
                  pl.BlockSpec((tt, 256), lambda b, d, j: (rb + b * n_t + late(d, j), 8)),
                  pl.BlockSpec((1, 256), lambda b, d, j: (0, 0)),
                  pl.BlockSpec((1, HEAD_DIM), lambda b, d, j: (0, 0)),
                  st_spec],
        out_specs=[pl.BlockSpec((tt, 256), lambda b, d, j: (b * n_t + late(d, j), 0)),
                   st_spec],
        out_shape=[jax.ShapeDtypeStruct((nb * L, 256), BF16),
                   jax.ShapeDtypeStruct((nb, 2, C_HEADS, HEAD_DIM, HEAD_DIM), F32)],
        scratch_shapes=[pltpu.VMEM((L, 256), F32),
                        pltpu.VMEM((tt, 256), F32),
                        pltpu.VMEM((tt, 256), F32),
                        pltpu.VMEM((tt, 256), F32),
                        pltpu.VMEM((tt, 256), BF16),
                        pltpu.VMEM((tt, 256), BF16),
                        pltpu.VMEM((tt, 256), BF16),
                        pltpu.VMEM((C_HEADS, HEAD_DIM, HEAD_DIM), F32)],
        compiler_params=_params(("arbitrary", "arbitrary", "arbitrary")),
        name="hgrn_mixer",
    )(z, z, z, z, lb.reshape(1, 256), gn.reshape(1, HEAD_DIM), s0_t)


DL_C = DELTA_CHUNK
DL_PREP_TT = 256
DL_HALO = 8
DL_CHUNK_TT = 512
N_QKV_HEADS = 3 * D_HEADS
DL_GROUP = 2


def _delta_prep_kernel(x_ref, xp_ref, xn_ref, zab_ref, cw_ref, na_ref, dtb_ref, qkv_ref, gate_ref, xs_scr):
    tt = x_ref.shape[0]
    row = pl.program_id(0) * tt
    lat = row - CTX_TOK
    first = jnp.where(row < CTX_TOK, True, lat % DEC_SEQ == 0)
    last = jnp.where(row < CTX_TOK, True, (lat + tt) % DEC_SEQ == 0)
    xs_scr[DL_HALO:DL_HALO + tt, :] = x_ref[...]
    xs_scr[0:DL_HALO, :] = jnp.where(first, 0.0, xp_ref[...])
    xs_scr[DL_HALO + tt:2 * DL_HALO + tt, :] = jnp.where(last, 0.0, xn_ref[...])
    pad = (CONV_K - 1) // 2
    y = None
    for t in range(CONV_K):
        term = xs_scr[pl.ds(DL_HALO - pad + t, tt), :] * cw_ref[t:t + 1, :]
        y = term if y is None else y + term
    y = y * jax.nn.sigmoid(y)
    for idx in range(N_QKV_HEADS):
        xh = y[:, idx * HEAD_DIM:(idx + 1) * HEAD_DIM]
        if idx < 2 * D_HEADS:
            xh = xh * lax.rsqrt(jnp.sum(xh * xh, axis=-1, keepdims=True) + EPS)
        if idx < D_HEADS:
            xh = xh * ATT_SCALE
        qkv_ref[idx] = xh
    zab = zab_ref[...]
    lane = lax.broadcasted_iota(jnp.int32, zab.shape, 1)
    t_ = zab + dtb_ref[...]
    softplus = jnp.maximum(t_, 0.0) + jnp.log(1.0 + jnp.exp(-jnp.abs(t_)))
    gate_ref[...] = jnp.where(lane < 2 * D_HEADS, na_ref[...] * softplus, jax.nn.sigmoid(zab))


def delta_prep(z, conv_w, a_log, dt_bias):
    tt = DL_PREP_TT
    hb = tt // DL_HALO
    n_hb = N_TOK // DL_HALO
    pad8 = lambda v: jnp.concatenate([v.reshape(1, 2 * D_HEADS), jnp.zeros((1, 128 - 2 * D_HEADS), F32)], axis=1)
    return pl.pallas_call(
        _delta_prep_kernel,
        grid=(N_TOK // tt,),
        in_specs=[pl.BlockSpec((tt, 768), lambda i: (i, 3)),
                  pl.BlockSpec((DL_HALO, 768), lambda i: (jnp.maximum(i * hb - 1, 0), 3)),
                  pl.BlockSpec((DL_HALO, 768), lambda i: (jnp.minimum((i + 1) * hb, n_hb - 1), 3)),
                  pl.BlockSpec((tt, 128), lambda i: (i, Z_DAB // 128)),
                  pl.BlockSpec((CONV_K, 768), lambda i: (0, 0)),
                  pl.BlockSpec((1, 128), lambda i: (0, 0)),
                  pl.BlockSpec((1, 128), lambda i: (0, 0))],
        out_specs=[pl.BlockSpec((N_QKV_HEADS, tt, HEAD_DIM), lambda i: (0, i, 0)),
                   pl.BlockSpec((tt, 128), lambda i: (i, 0))],
        out_shape=[jax.ShapeDtypeStruct((N_QKV_HEADS, N_TOK, HEAD_DIM), F32),
                   jax.ShapeDtypeStruct((N_TOK, 128), F32)],
        scratch_shapes=[pltpu.VMEM((tt + 2 * DL_HALO, 768), F32)],
        compiler_params=_params(("arbitrary",)),
        name="delta_prep",
    )(z, z, z, z, conv_w, pad8(-jnp.exp(a_log)), pad8(dt_bias))


def _split_bf16(a):
    hi = a.astype(BF16)
    return hi, (a - hi.astype(F32)).astype(BF16)


def _dot_hl(a_parts, b_parts):
    (a_hi, a_lo), (b_hi, b_lo) = a_parts, b_parts
    m = a_hi.shape[0]
    r = jnp.dot(jnp.concatenate([a_hi, a_lo], axis=0), b_hi, preferred_element_type=F32)
    return r[:m] + r[m:] + jnp.dot(a_hi, b_lo, preferred_element_type=F32)


def _delta_chunk_kernel(qkv_ref, gate_ref, u2_ref, wq_ref, ak_ref):
    C = DL_C
    n_c = gate_ref.shape[0] // C
    row = lax.broadcasted_iota(jnp.int32, (C, C), 0)
    col = lax.broadcasted_iota(jnp.int32, (C, C), 1)
    eye = (row == col).astype(F32)
    t_idx = lax.broadcasted_iota(jnp.int32, (C, 128), 0)

    def chunk_chains(c):
        r0 = pl.multiple_of(c * C, C)
        ga = gate_ref[pl.ds(r0, C), :]
        chains = []
        for d in range(2):
            incl = (row >= col) if d == 0 else (row <= col)
            strict = (row > col) if d == 0 else (row < col)
            g_all = ga
            step = 1
            while step < C:
                if d == 0:
                    g_all = g_all + jnp.where(t_idx >= step, pltpu.roll(g_all, step, 0), 0.0)
                else:
                    g_all = g_all + jnp.where(t_idx < C - step, pltpu.roll(g_all, C - step, 0), 0.0)
                step *= 2
            g_all_t = g_all.T
            for h in range(D_HEADS):
                ci = d * D_HEADS + h
                q = qkv_ref[h, pl.ds(r0, C), :]
                k = qkv_ref[D_HEADS + h, pl.ds(r0, C), :]
                v = qkv_ref[2 * D_HEADS + h, pl.ds(r0, C), :]
                g_col = g_all[:, ci:ci + 1]
                g_row = g_all_t[ci:ci + 1, :]
                beta = ga[:, 2 * D_HEADS + ci:2 * D_HEADS + ci + 1]
                g_end = g_col[C - 1:C, :] if d == 0 else g_col[0:1, :]
                kb = k * beta
                eg = jnp.exp(g_col)
                decay = jnp.where(incl, jnp.exp(jnp.where(incl, g_col - g_row, 0.0)), 0.0)
                kq = jnp.concatenate([kb, q], axis=0).astype(BF16)
                chains.append(dict(
                    strict=strict, decay=decay, qg=q * eg, g_end=g_end,
                    r=lax.dot_general(kq, k.astype(BF16), _NT, preferred_element_type=F32),
                    rhs=jnp.concatenate([v * beta, kb * eg], axis=1).astype(BF16),
                    ke_t=(k * jnp.exp(g_end - g_col)).T))
        return r0, chains

    def body(gi, carry):
        groups = [(gi * DL_GROUP + cc,) + chunk_chains(gi * DL_GROUP + cc) for cc in range(DL_GROUP)]
        chains = [ch for _, _, chs in groups for ch in chs]
        for ch in chains:
            ch['p'] = -jnp.where(ch['strict'], ch['r'][:C] * ch['decay'], 0.0)
            ch['t'] = eye + ch['p']
        for _ in range(5):
            for ch in chains:
                parts = _split_bf16(ch['p'])
                ch['p'] = _dot_hl(parts, parts)
            for ch in chains:
                ch['t'] = ch['t'] + _dot_hl(_split_bf16(ch['t']), _split_bf16(ch['p']))
        for ch in chains:
            ch['uw'] = jnp.dot(ch['t'].astype(BF16), ch['rhs'], preferred_element_type=F32)
        pack = lambda xs: jnp.stack(xs).reshape((2, D_HEADS) + xs[0].shape)
        for c, r0, chs in groups:
            u2 = [jnp.concatenate([ch['uw'][:, :C], jnp.broadcast_to(jnp.exp(ch['g_end']), (C, C))], axis=1)
                  for ch in chs]
            wq = [jnp.concatenate([ch['uw'][:, C:], ch['qg']], axis=0).astype(BF16) for ch in chs]
            ak = [jnp.concatenate([ch['r'][C:] * ch['decay'], ch['ke_t']], axis=0).astype(BF16) for ch in chs]
            u2_ref[:, :, pl.ds(r0, C), :] = pack(u2)
            wq_ref[:, :, c] = pack(wq)
            ak_ref[:, :, c] = pack(ak)
        return carry

    lax.fori_loop(0, n_c // DL_GROUP, body, 0)


def delta_chunks(qkv, gates):
    tt = DL_CHUNK_TT
    n_c = tt // DL_C
    return pl.pallas_call(
        _delta_chunk_kernel,
        grid=(N_TOK // tt,),
        in_specs=[pl.BlockSpec((N_QKV_HEADS, tt, HEAD_DIM), lambda i: (0, i, 0)),
                  pl.BlockSpec((tt, 128), lambda i: (i, 0))],
        out_specs=[pl.BlockSpec((2, D_HEADS, tt, 128), lambda i: (0, 0, i, 0)),
                   pl.BlockSpec((2, D_HEADS, n_c, 2 * DL_C, HEAD_DIM), lambda i: (0, 0, i, 0, 0)),
                   pl.BlockSpec((2, D_HEADS, n_c, 2 * DL_C, HEAD_DIM), lambda i: (0, 0, i, 0, 0))],
        out_shape=[jax.ShapeDtypeStruct((2, D_HEADS, N_TOK, 128), F32),
                   jax.ShapeDtypeStruct((2, D_HEADS, N_TOK // DL_C, 2 * DL_C, HEAD_DIM), BF16),
                   jax.ShapeDtypeStruct((2, D_HEADS, N_TOK // DL_C, 2 * DL_C, HEAD_DIM), BF16)],
        compiler_params=_params(("arbitrary",)),
        name="delta_chunks",
    )(qkv, gates)


DS_SEQ = 2


def _delta_scan_kernel(*refs):
    in_refs, (s0_ref, of_ref, ob_ref, s_ref, s_scr) = refs[:6 * DS_SEQ], refs[6 * DS_SEQ:]
    j = pl.program_id(1)
    C = DL_C
    n_c = in_refs[1].shape[1]

    @pl.when(j == 0)
    def _():
        s_scr[...] = s0_ref[...]

    def body(ci, carry):
        chains = []
        for q in range(DS_SEQ):
            for d, o_ref in enumerate((of_ref, ob_ref)):
                u2_ref, wq_ref, ak_ref = in_refs[6 * q + 3 * d:6 * q + 3 * d + 3]
                c = ci if d == 0 else n_c - 1 - ci
                r0 = pl.multiple_of(c * C, C)
                for h in range(D_HEADS):
                    s = s_scr[q, d, h]
                    chains.append(dict(q=q, d=d, h=h, c=c, r0=r0, s=s, ak_ref=ak_ref, o_ref=o_ref,
                                       u2=u2_ref[h, pl.ds(r0, C), :],
                                       r1=jnp.dot(wq_ref[h, c], s.astype(BF16),
                                                  preferred_element_type=F32)))
        for ch in chains:
            v_new = ch['u2'][:, :C] - ch['r1'][:C]
            ch['r2'] = jnp.dot(ch['ak_ref'][ch['h'], ch['c']], v_new.astype(BF16),
                               preferred_element_type=F32)
        for ch in chains:
            ch['o_ref'][ch['h'], ch['q'], pl.ds(ch['r0'], C), :] = ch['r1'][C:] + ch['r2'][:C]
            s_scr[ch['q'], ch['d'], ch['h']] = ch['s'] * ch['u2'][0:1, C:] + ch['r2'][C:]
        return carry

    lax.fori_loop(0, n_c, body, 0)

    @pl.when(j == pl.num_programs(1) - 1)
    def _():
        s_ref[...] = s_scr[...]


def delta_scan(u2, wq, ak, row0, nb, L, s0):
    tt = min(L, 512)
    n_t = L // tt
    n_c = tt // DL_C
    rb = row0 // tt
    in_specs = []
    for q in range(DS_SEQ):
        fwd = lambda p, j, q=q: rb + (DS_SEQ * p + q) * n_t + j
        bwd = lambda p, j, q=q: rb + (DS_SEQ * p + q) * n_t + (n_t - 1 - j)
        for d, f in ((0, fwd), (1, bwd)):
            in_specs += [
                pl.BlockSpec((None, D_HEADS, tt, 128), lambda p, j, d=d, f=f: (d, 0, f(p, j), 0)),
                pl.BlockSpec((None, D_HEADS, n_c, 2 * DL_C, HEAD_DIM), lambda p, j, d=d, f=f: (d, 0, f(p, j), 0, 0)),
                pl.BlockSpec((None, D_HEADS, n_c, 2 * DL_C, HEAD_DIM), lambda p, j, d=d, f=f: (d, 0, f(p, j), 0, 0))]
    st_spec = pl.BlockSpec((DS_SEQ, 2, D_HEADS, HEAD_DIM, HEAD_DIM), lambda p, j: (p, 0, 0, 0, 0))
    o_shape = jax.ShapeDtypeStruct((D_HEADS, nb // DS_SEQ, DS_SEQ, L, HEAD_DIM), F32)
    o_f, o_b, s = pl.pallas_call(
        _delta_scan_kernel,
        grid=(nb // DS_SEQ, n_t),
        in_specs=in_specs + [st_spec],
        out_specs=[pl.BlockSpec((D_HEADS, None, DS_SEQ, tt, HEAD_DIM), lambda p, j: (0, p, 0, j, 0)),
                   pl.BlockSpec((D_HEADS, None, DS_SEQ, tt, HEAD_DIM), lambda p, j: (0, p, 0, n_t - 1 - j, 0)),
                   st_spec],
        out_shape=[o_shape, o_shape, jax.ShapeDtypeStruct((nb, 2, D_HEADS, HEAD_DIM, HEAD_DIM), F32)],
        scratch_shapes=[pltpu.VMEM((DS_SEQ, 2, D_HEADS, HEAD_DIM, HEAD_DIM), F32)],
        compiler_params=_params(("arbitrary", "arbitrary")),
        name="delta_scan",
    )(*([u2, wq, ak] * (2 * DS_SEQ)), s0)
    return o_f.reshape(D_HEADS, nb * L, HEAD_DIM), o_b.reshape(D_HEADS, nb * L, HEAD_DIM), s


def kernel(x_prompt, x_sample, cache_attn_a_k, cache_attn_a_v, cache_attn_b_k, cache_attn_b_v,
           state_hgrn, state_delta, c, c_ctx, norm1_g, norm2_g, w_ada, b_ada, w_in, a_sink,
           b_qnorm_g, b_knorm_g, c_lb, c_onorm_g, d_conv, d_a_log, d_dt_bias, d_onorm_g,
           w_branch, w_out, ffn_w1, ffn_w3, ffn_w2, router_w, router_b, moe_w1, moe_w3, moe_w2,
           final_norm_g):
    cum = jnp.cumsum(jax.nn.softmax(c_lb, axis=0), axis=0)
    lower_bounds = cum - cum[:1]

    xs = (x_prompt.reshape(CTX_TOK, D_MODEL), x_sample.reshape(LAT_TOK, D_MODEL), 0)
    cond = jnp.concatenate([c_ctx[None, :], c, jnp.zeros((16 - N_COND, D_MODEL), F32)], axis=0)

    rope_c, rope_s = rope_lane_tables(DEC_SEQ)
    caches = []
    for l in range(DEPTH):
        mod = ada_modulation(cond, w_ada[l], b_ada[l])[:N_COND].reshape(N_COND, 6, D_MODEL)
        w_mix = jnp.concatenate([w_in[l][:, :Z_MAIN], w_in[l][:, Z_MAIN + 16:W_IN_MIX],
                                 w_in[l][:, Z_MAIN:Z_MAIN + 16], jnp.zeros((D_MODEL, 128 - 16), F32)],
                                axis=1).astype(BF16)
        w_gl = w_in[l][:, W_IN_MIX:].reshape(D_MODEL, N_BRANCH, D_MODEL).transpose(1, 0, 2).astype(BF16)
        z = input_projection(xs, mod, norm1_g[l], w_mix)
        kv2 = lambda t: t.reshape(DEC_BATCH, PAST_LEN, 128)
        o_ab_ctx, bk_ctx = ctx_attention(z, a_sink[l], b_qnorm_g[l], b_knorm_g[l])
        o_a_lat = latent_attention_a(z, CTX_TOK, DEC_BATCH, DEC_SEQ, a_sink[l], kv2(cache_attn_a_k[:, l]),
                                     kv2(cache_attn_a_v[:, l]), rope_c, rope_s)
        o_b_lat = latent_attention_b(z, CTX_TOK, DEC_BATCH, DEC_SEQ, kv2(cache_attn_b_k[:, l]),
                                     kv2(cache_attn_b_v[:, l]), rope_c, rope_s, b_qnorm_g[l], b_knorm_g[l])
        o_c_ctx, sc_t = hgrn_mixer(z, 0, BATCH, SEQ, lower_bounds[l], c_onorm_g[l],
                                   jnp.zeros((BATCH, 2, C_HEADS, HEAD_DIM, HEAD_DIM), F32))
        o_c_lat, _ = hgrn_mixer(z, CTX_TOK, DEC_BATCH, DEC_SEQ, lower_bounds[l], c_onorm_g[l],
                                jnp.swapaxes(state_hgrn[:, l], -1, -2))

        qkv, gates = delta_prep(z, d_conv[l], d_a_log[l], d_dt_bias[l])
        u2, wq, ak = delta_chunks(qkv, gates)
        of_ctx, ob_ctx, sd = delta_scan(u2, wq, ak, 0, BATCH, SEQ,
                                        jnp.zeros((BATCH, 2, D_HEADS, HEAD_DIM, HEAD_DIM), F32))
        of_lat, ob_lat, _ = delta_scan(u2, wq, ak, CTX_TOK, DEC_BATCH, DEC_SEQ, state_delta[:, l])
        kvh = lambda t: t.reshape(BATCH, SEQ, 2, HEAD_DIM)
        caches.append((kvh(z[:CTX_TOK, 256:384]), kvh(z[:CTX_TOK, 384:512]), kvh(bk_ctx), kvh(z[:CTX_TOK, 896:1024]),
                       jnp.swapaxes(sc_t, -1, -2), sd))
        x = merge_projection(xs, mod, norm1_g[l], (o_ab_ctx, o_c_ctx, of_ctx, ob_ctx),
                             (o_a_lat, o_b_lat, o_c_lat, of_lat, ob_lat), z, d_onorm_g[l],
                             w_gl, w_branch[l].astype(BF16), w_out[l].astype(BF16))
        j = l // 2
        if l % 2 == 0:
            x = dense_ffn(x, mod, norm2_g[l], ffn_w1[j].astype(BF16), ffn_w3[j].astype(BF16),
                          ffn_w2[j].astype(BF16))
            xs = (x, x, CTX_TOK // TM)
        else:
            assert l == DEPTH - 1, "the expert layer's residual is fused with the final norm"
            rw = jnp.concatenate([router_w[j], jnp.zeros((D_MODEL, 128 - N_EXPERTS), F32)], axis=1)
            rb = jnp.concatenate([router_b[j], jnp.zeros((128 - N_EXPERTS,), F32)])[None, :]
            h2, pos, wgt = moe_router(x, mod, norm2_g[l], rw, rb)
            f = moe_experts(h2, pos, wgt, moe_w1[j].astype(BF16), moe_w3[j].astype(BF16), moe_w2[j].astype(BF16))
            y_prompt = residual_final_norm(x, f, mod, final_norm_g, 0, CTX_TOK).reshape(BATCH, SEQ, D_MODEL)
            y_sample = residual_final_norm(x, f, mod, final_norm_g, CTX_TOK, LAT_TOK).reshape(DEC_BATCH, DEC_SEQ, D_MODEL)

    stack = lambda idx: jnp.stack([caches[l][idx] for l in range(DEPTH)], axis=1)
    return (y_prompt, y_sample, stack(0), stack(1), stack(2), stack(3), stack(4), stack(5))
```

```python
import functools

import jax
import jax.numpy as jnp
import numpy as np
from jax import lax
from jax.experimental import pallas as pl
from jax.experimental.pallas import tpu as pltpu

F32 = jnp.float32
BF16 = jnp.bfloat16

D_MODEL = 1024
BATCH = 32
SEQ = 256
DEPTH = 2
DEC_BATCH = 8
DEC_SEQ = 4096
PAST_LEN = 256
GRID_W = 64
HEAD_DIM = 64
A_HEADS = 4
A_KV_HEADS = 2
B_HEADS = 4
B_KV_HEADS = 2
C_HEADS = 4
D_HEADS = 4
BRANCH_W = 256
N_BRANCH = 4
WINDOW = 128
BLOCK = 128
ROPE_BASE = 10000.0
HGRN_CHUNK = 32
DELTA_CHUNK = 64
CONV_K = 5
D_FF = 2816
N_EXPERTS = 8
D_FF_EXPERT = 3584
EPS = 1e-6
NEG_INF = -1e30
F32_MIN = float(np.finfo(np.float32).min)

CTX_TOK = BATCH * SEQ
LAT_TOK = DEC_BATCH * DEC_SEQ
N_TOK = CTX_TOK + LAT_TOK
N_COND = 1 + DEC_BATCH

Z_MAIN = 3072
Z_DG = Z_MAIN
Z_DAB = Z_DG + BRANCH_W
Z_COLS = Z_DAB + 128
W_IN_MIX = 3344

TM = 512
VMEM_LIMIT = 56 * 1024 * 1024


def _tile_cond(i, tm):
    ctx_tiles = CTX_TOK // tm
    per_b = DEC_SEQ // tm
    return jnp.where(i < ctx_tiles, 0, 1 + (i - ctx_tiles) // per_b)


def _rms(x, g):
    return x * lax.rsqrt(jnp.mean(x * x, axis=-1, keepdims=True) + EPS) * g


def _params(sem):
    return pltpu.CompilerParams(dimension_semantics=sem, vmem_limit_bytes=VMEM_LIMIT)


def _ada_kernel(c_ref, w_ref, b_ref, o_ref):
    c = c_ref[...]
    s = c * jax.nn.sigmoid(c)
    o_ref[...] = jnp.dot(s.astype(BF16), w_ref[...].astype(BF16), preferred_element_type=F32) + b_ref[...]


def ada_modulation(cond_pad, w, b):
    n = 6 * D_MODEL
    tn = 1536
    return pl.pallas_call(
        _ada_kernel,
        grid=(n // tn,),
        in_specs=[pl.BlockSpec((16, D_MODEL), lambda j: (0, 0)),
                  pl.BlockSpec((D_MODEL, tn), lambda j: (0, j)),
                  pl.BlockSpec((1, tn), lambda j: (0, j))],
        out_specs=pl.BlockSpec((16, tn), lambda j: (0, j)),
        out_shape=jax.ShapeDtypeStruct((16, n), F32),
        compiler_params=_params(("arbitrary",)),
        name="ada_modulation",
    )(cond_pad, w, b.reshape(1, n))


def _x_specs(xs):
    ctx_tiles = CTX_TOK // TM
    lat_tile0 = xs[2]
    return [pl.BlockSpec((TM, D_MODEL), lambda i: (jnp.minimum(i, ctx_tiles - 1), 0)),
            pl.BlockSpec((TM, D_MODEL), lambda i: (jnp.maximum(i - ctx_tiles, 0) + lat_tile0, 0))]


def _x_tile(xc_ref, xl_ref):
    return jnp.where(pl.program_id(0) < CTX_TOK // TM, xc_ref[...], xl_ref[...])


def _in_kernel(xc_ref, xl_ref, mod_ref, g_ref, w_ref, z_ref):
    h = _rms(_x_tile(xc_ref, xl_ref), g_ref[...]) * (1.0 + mod_ref[1:2, :]) + mod_ref[0:1, :]
    z_ref[...] = jnp.dot(h.astype(BF16), w_ref[...], preferred_element_type=F32)


def input_projection(xs, mod, g, w):
    nt = N_TOK // TM
    return pl.pallas_call(
        _in_kernel,
        grid=(nt,),
        in_specs=_x_specs(xs) + [
                  pl.BlockSpec((None, 6, D_MODEL), lambda i: (_tile_cond(i, TM), 0, 0)),
                  pl.BlockSpec((1, D_MODEL), lambda i: (0, 0)),
                  pl.BlockSpec((D_MODEL, Z_COLS), lambda i: (0, 0))],
        out_specs=pl.BlockSpec((TM, Z_COLS), lambda i: (i, 0)),
        out_shape=jax.ShapeDtypeStruct((N_TOK, Z_COLS), F32),
        compiler_params=_params(("arbitrary",)),
        name="input_projection",
    )(xs[0], xs[1], mod, g.reshape(1, D_MODEL), w)


def _merge_kernel(xc_ref, xl_ref, mod_ref, g_ref, ab_c_ref, c_c_ref, df_c_ref, db_c_ref, a_l_ref, b_l_ref, c_l_ref,
                  df_l_ref, db_l_ref, zg_ref, gd_ref, wgl_ref, wbr_ref, wout_ref, xo_ref):
    x = _x_tile(xc_ref, xl_ref)
    h = (_rms(x, g_ref[...]) * (1.0 + mod_ref[1:2, :]) + mod_ref[0:1, :]).astype(BF16)
    is_ctx = pl.program_id(0) < CTX_TOK // TM
    d_heads = []
    for hd in range(D_HEADS):
        o = jnp.where(is_ctx, df_c_ref[hd] + db_c_ref[hd], df_l_ref[hd] + db_l_ref[hd])
        d_heads.append(o * lax.rsqrt(jnp.mean(o * o, axis=-1, keepdims=True) + EPS) * gd_ref[...])
    dg = zg_ref[...]
    branches = (jnp.where(is_ctx, ab_c_ref[:, :BRANCH_W], a_l_ref[...]),
                jnp.where(is_ctx, ab_c_ref[:, BRANCH_W:], b_l_ref[...]),
                jnp.where(is_ctx, c_c_ref[...], c_l_ref[...]),
                (jnp.concatenate(d_heads, axis=-1) * (dg * jax.nn.sigmoid(dg))).astype(BF16))
    merged = None
    for j in range(N_BRANCH):
        gate = jax.nn.sigmoid(jnp.dot(h, wgl_ref[j], preferred_element_type=F32))
        br = jnp.dot(branches[j], wbr_ref[j], preferred_element_type=F32)
        merged = gate * br if merged is None else merged + gate * br
    mix = jnp.dot(merged.astype(BF16), wout_ref[...], preferred_element_type=F32)
    xo_ref[...] = x + mod_ref[2:3, :] * mix


def merge_projection(xs, mod, g, o_ctx, o_lat, z, gd, wgl, wbr, wout):
    nt = N_TOK // TM
    ctx_tiles = CTX_TOK // TM
    ctx_i = lambda i: jnp.minimum(i, ctx_tiles - 1)
    lat_i = lambda i: jnp.maximum(i - ctx_tiles, 0)
    ctx_spec = lambda w: pl.BlockSpec((TM, w), lambda i: (ctx_i(i), 0))
    lat_spec = pl.BlockSpec((TM, BRANCH_W), lambda i: (lat_i(i), 0))
    ctx_d_spec = pl.BlockSpec((D_HEADS, TM, HEAD_DIM), lambda i: (0, ctx_i(i), 0))
    lat_d_spec = pl.BlockSpec((D_HEADS, TM, HEAD_DIM), lambda i: (0, lat_i(i), 0))
    return pl.pallas_call(
        _merge_kernel,
        grid=(nt,),
        in_specs=_x_specs(xs) + [
                  pl.BlockSpec((None, 6, D_MODEL), lambda i: (_tile_cond(i, TM), 0, 0)),
                  pl.BlockSpec((1, D_MODEL), lambda i: (0, 0)),
                  ctx_spec(2 * BRANCH_W), ctx_spec(BRANCH_W), ctx_d_spec, ctx_d_spec,
                  lat_spec, lat_spec, lat_spec, lat_d_spec, lat_d_spec,
                  pl.BlockSpec((TM, BRANCH_W), lambda i: (i, Z_DG // BRANCH_W)),
                  pl.BlockSpec((1, HEAD_DIM), lambda i: (0, 0)),
                  pl.BlockSpec((N_BRANCH, D_MODEL, D_MODEL), lambda i: (0, 0, 0)),
                  pl.BlockSpec((N_BRANCH, BRANCH_W, D_MODEL), lambda i: (0, 0, 0)),
                  pl.BlockSpec((D_MODEL, D_MODEL), lambda i: (0, 0))],
        out_specs=pl.BlockSpec((TM, D_MODEL), lambda i: (i, 0)),
        out_shape=jax.ShapeDtypeStruct((N_TOK, D_MODEL), F32),
        compiler_params=_params(("arbitrary",)),
        name="merge_projection",
    )(xs[0], xs[1], mod, g.reshape(1, D_MODEL), *o_ctx, *o_lat, z, gd.reshape(1, HEAD_DIM), wgl, wbr, wout)


def _ffn_kernel(x_ref, mod_ref, g_ref, w1_ref, w3_ref, w2_ref, xo_ref):
    x = x_ref[...]
    h = (_rms(x, g_ref[...]) * (1.0 + mod_ref[4:5, :]) + mod_ref[3:4, :]).astype(BF16)
    a = jnp.dot(h, w1_ref[...], preferred_element_type=F32)
    b = jnp.dot(h, w3_ref[...], preferred_element_type=F32)
    hid = (a * jax.nn.sigmoid(a) * b).astype(BF16)
    f = jnp.dot(hid, w2_ref[...], preferred_element_type=F32)
    xo_ref[...] = x + mod_ref[5:6, :] * f


def dense_ffn(x, mod, g, w1, w3, w2):
    nt = N_TOK // TM
    const = lambda i: (0, 0)
    return pl.pallas_call(
        _ffn_kernel,
        grid=(nt,),
        in_specs=[pl.BlockSpec((TM, D_MODEL), lambda i: (i, 0)),
                  pl.BlockSpec((None, 6, D_MODEL), lambda i: (_tile_cond(i, TM), 0, 0)),
                  pl.BlockSpec((1, D_MODEL), const),
                  pl.BlockSpec((D_MODEL, D_FF), const, pipeline_mode=pl.Buffered(1)),
                  pl.BlockSpec((D_MODEL, D_FF), const, pipeline_mode=pl.Buffered(1)),
                  pl.BlockSpec((D_FF, D_MODEL), const, pipeline_mode=pl.Buffered(1))],
        out_specs=pl.BlockSpec((TM, D_MODEL), lambda i: (i, 0)),
        out_shape=jax.ShapeDtypeStruct((N_TOK, D_MODEL), F32),
        compiler_params=_params(("arbitrary",)),
        name="dense_ffn",
    )(x, mod, g.reshape(1, D_MODEL), w1, w3, w2)


MOE_T = 1024
MOE_R = 144
MOE_SR = 128
MOE_F = 1792
MOE_CAP = -(-MOE_T // MOE_R) * MOE_R


def _router_kernel(x_ref, mod_ref, g_ref, rw_ref, rb_ref, before_ref, h_ref, pos_ref, wgt_ref):
    h = _rms(x_ref[...], g_ref[...]) * (1.0 + mod_ref[4:5, :]) + mod_ref[3:4, :]
    h_ref[...] = h.astype(BF16)
    logits = _dot_hl(_split_bf16(h), _split_bf16(rw_ref[...])) + rb_ref[...]
    lt = logits.T[:N_EXPERTS, :]
    eidx = lax.broadcasted_iota(jnp.int32, lt.shape, 0)
    m1 = jnp.max(lt, axis=0, keepdims=True)
    i1 = jnp.min(jnp.where(lt == m1, eidx, N_EXPERTS), axis=0, keepdims=True)
    rest = jnp.where(eidx == i1, F32_MIN, lt)
    m2 = jnp.max(rest, axis=0, keepdims=True)
    i2 = jnp.min(jnp.where(rest == m2, eidx, N_EXPERTS), axis=0, keepdims=True)
    e2 = jnp.exp(m2 - m1)
    p1 = 1.0 / (1.0 + e2)
    p2 = e2 / (1.0 + e2)
    wgt_ref[...] = jnp.where(eidx == i1, p1, 0.0) + jnp.where(eidx == i2, p2, 0.0)
    routed = jnp.where(eidx == i1, 1.0, jnp.where(eidx == i2, 1.0, 0.0))
    rank = jnp.dot(routed.astype(BF16), before_ref[...], preferred_element_type=F32)
    pos_ref[...] = jnp.where(routed > 0.0, rank.astype(jnp.int32), -1)


def moe_router(x, mod, g, rw, rb):
    nt = N_TOK // MOE_T
    tok = jnp.arange(MOE_T, dtype=jnp.int32)
    before = (tok[:, None] < tok[None, :]).astype(BF16)
    return pl.pallas_call(
        _router_kernel,
        grid=(nt,),
        in_specs=[pl.BlockSpec((MOE_T, D_MODEL), lambda i: (i, 0)),
                  pl.BlockSpec((None, 6, D_MODEL), lambda i: (_tile_cond(i, MOE_T), 0, 0)),
                  pl.BlockSpec((1, D_MODEL), lambda i: (0, 0)),
                  pl.BlockSpec((D_MODEL, 128), lambda i: (0, 0)),
                  pl.BlockSpec((1, 128), lambda i: (0, 0)),
                  pl.BlockSpec((MOE_T, MOE_T), lambda i: (0, 0))],
        out_specs=[pl.BlockSpec((MOE_T, D_MODEL), lambda i: (i, 0)),
                   pl.BlockSpec((N_EXPERTS, MOE_T), lambda i: (0, i)),
                   pl.BlockSpec((N_EXPERTS, MOE_T), lambda i: (0, i))],
        out_shape=[jax.ShapeDtypeStruct((N_TOK, D_MODEL), BF16),
                   jax.ShapeDtypeStruct((N_EXPERTS, N_TOK), jnp.int32),
                   jax.ShapeDtypeStruct((N_EXPERTS, N_TOK), F32)],
        compiler_params=_params(("arbitrary",)),
        name="moe_router",
    )(x, mod, g.reshape(1, D_MODEL), rw, rb, before)


def _moe_sparse_kernel(h_ref, pos_ref, wgt_ref, w1_ref, w3_ref, w2_ref, y_ref, xg_scr, acc_scr, wr_scr):
    e = pl.program_id(1)
    f = pl.program_id(2)
    t = h_ref.shape[0]
    pos_e = pos_ref[pl.ds(e, 1), :]
    n_rows = jnp.max(pos_e) + 1
    n_blocks = (n_rows + MOE_R - 1) // MOE_R
    n_sc_blocks = (n_rows + MOE_SR - 1) // MOE_SR
    n_init_blocks = jnp.maximum(n_blocks, (n_sc_blocks * MOE_SR + MOE_R - 1) // MOE_R)
    row_id = lax.broadcasted_iota(jnp.int32, (MOE_R, t), 0)

    def block_rows(r):
        return pl.ds(pl.multiple_of(r * MOE_R, 16), MOE_R)

    def selects(r):
        return pos_e == row_id + r * MOE_R

    @pl.when((e == 0) & (f == 0))
    def _():
        y_ref[...] = jnp.zeros_like(y_ref)

    @pl.when(f == 0)
    def _():
        wgt_e = wgt_ref[pl.ds(e, 1), :]

        def gather(r, carry):
            sel = selects(r)
            xg = jnp.dot(jnp.where(sel, 1.0, 0.0).astype(BF16), h_ref[...], preferred_element_type=F32)
            xg_scr[block_rows(r), :] = xg.astype(BF16)
            w_rows = jnp.sum(jnp.where(sel, wgt_e, 0.0), axis=1, keepdims=True)
            wr_scr[block_rows(r), :] = jnp.broadcast_to(w_rows, (MOE_R, 128))
            acc_scr[block_rows(r), :] = jnp.zeros((MOE_R, D_MODEL), F32)
            return carry

        lax.fori_loop(0, n_init_blocks, gather, 0)

    def expert(r, carry):
        xg = xg_scr[block_rows(r), :]
        a = jnp.dot(xg, w1_ref[...], preferred_element_type=F32)
        b = jnp.dot(xg, w3_ref[...], preferred_element_type=F32)
        hid = (a * jax.nn.sigmoid(a) * b * wr_scr[block_rows(r), 0:1]).astype(BF16)
        acc_scr[block_rows(r), :] += jnp.dot(hid, w2_ref[...], preferred_element_type=F32)
        return carry

    lax.fori_loop(0, n_blocks, expert, 0)

    @pl.when(f == pl.num_programs(2) - 1)
    def _():
        sc_row_id = lax.broadcasted_iota(jnp.int32, (MOE_SR, t), 0)

        def scatter(r, carry):
            rows = pl.ds(pl.multiple_of(r * MOE_SR, MOE_SR), MOE_SR)
            onehot = jnp.where(pos_e == sc_row_id + r * MOE_SR, 1.0, 0.0).astype(BF16)
            hi, lo = _split_bf16(acc_scr[rows, :])
            y_ref[...] += lax.dot_general(jnp.concatenate([onehot, onehot], axis=0),
                                          jnp.concatenate([hi, lo], axis=0), _TN, preferred_element_type=F32)
            return carry

        lax.fori_loop(0, n_sc_blocks, scatter, 0)


def moe_experts(h2, pos, wgt, w1, w3, w2):
    nt = N_TOK // MOE_T
    nf = D_FF_EXPERT // MOE_F
    return pl.pallas_call(
        _moe_sparse_kernel,
        grid=(nt, N_EXPERTS, nf),
        in_specs=[pl.BlockSpec((MOE_T, D_MODEL), lambda i, e, f: (i, 0)),
                  pl.BlockSpec((N_EXPERTS, MOE_T), lambda i, e, f: (0, i)),
                  pl.BlockSpec((N_EXPERTS, MOE_T), lambda i, e, f: (0, i)),
                  pl.BlockSpec((None, D_MODEL, MOE_F), lambda i, e, f: (e, 0, f)),
                  pl.BlockSpec((None, D_MODEL, MOE_F), lambda i, e, f: (e, 0, f)),
                  pl.BlockSpec((None, MOE_F, D_MODEL), lambda i, e, f: (e, f, 0))],
        out_specs=pl.BlockSpec((MOE_T, D_MODEL), lambda i, e, f: (i, 0)),
        out_shape=jax.ShapeDtypeStruct((N_TOK, D_MODEL), F32),
        scratch_shapes=[pltpu.VMEM((MOE_CAP, D_MODEL), BF16),
                        pltpu.VMEM((MOE_CAP, D_MODEL), F32),
                        pltpu.VMEM((MOE_CAP, 128), F32)],
        compiler_params=_params(("arbitrary", "arbitrary", "arbitrary")),
        name="moe_experts",
    )(h2, pos, wgt, w1, w3, w2)


def _residual_norm_kernel(x_ref, y_ref, mod_ref, g_ref, o_ref):
    o_ref[...] = _rms(x_ref[...] + mod_ref[5:6, :] * y_ref[...], g_ref[...])


def residual_final_norm(x, y, mod, g, row0, n_rows):
    tm = 1024
    t0 = row0 // tm
    return pl.pallas_call(
        _residual_norm_kernel,
        grid=(n_rows // tm,),
        in_specs=[pl.BlockSpec((tm, D_MODEL), lambda i: (t0 + i, 0)),
                  pl.BlockSpec((tm, D_MODEL), lambda i: (t0 + i, 0)),
                  pl.BlockSpec((None, 6, D_MODEL), lambda i: (_tile_cond(t0 + i, tm), 0, 0)),
                  pl.BlockSpec((1, D_MODEL), lambda i: (0, 0))],
        out_specs=pl.BlockSpec((tm, D_MODEL), lambda i: (i, 0)),
        out_shape=jax.ShapeDtypeStruct((n_rows, D_MODEL), F32),
        compiler_params=_params(("arbitrary",)),
        name="residual_final_norm",
    )(x, y, mod, g.reshape(1, D_MODEL))


ATT_SCALE = HEAD_DIM ** -0.5
LOG2_E = 1.4426950408889634
_NT = (((1,), (1,)), ((), ()))


def _head_rms(x, g_row):
    outs = []
    for h in range(x.shape[1] // HEAD_DIM):
        xh = x[:, h * HEAD_DIM:(h + 1) * HEAD_DIM]
        outs.append(xh * lax.rsqrt(jnp.mean(xh * xh, axis=-1, keepdims=True) + EPS) * g_row)
    return jnp.concatenate(outs, axis=-1)


def _rope_apply(x, c, s):
    w = x.shape[-1]
    lane = lax.broadcasted_iota(jnp.int32, x.shape, 1)
    first_half = ((lane // (HEAD_DIM // 4)) % 2) == 0
    partner = jnp.where(first_half, pltpu.roll(x, w - HEAD_DIM // 4, 1), pltpu.roll(x, HEAD_DIM // 4, 1))
    return x * c + partner * s


def rope_lane_tables(L):
    rows = L // GRID_W
    row = jnp.repeat(jnp.arange(rows, dtype=F32), GRID_W)
    col = jnp.tile(jnp.arange(GRID_W, dtype=F32), rows)
    n_freq = HEAD_DIM // 4
    inv = ROPE_BASE ** (-jnp.arange(n_freq, dtype=F32) / n_freq)
    ang = jnp.stack([row, col], 0)[:, :, None] * inv
    cos, sin = jnp.cos(ang), jnp.sin(ang)
    c = jnp.concatenate([cos[0], cos[0], cos[1], cos[1]], axis=-1)
    s = jnp.concatenate([-sin[0], sin[0], -sin[1], sin[1]], axis=-1)
    return jnp.tile(c, (1, 4)), jnp.tile(s, (1, 4))


def _with_ones(v):
    ones = jnp.ones((v.shape[0], HEAD_DIM), BF16)
    parts = []
    for h in range(v.shape[1] // HEAD_DIM):
        parts += [v[:, h * HEAD_DIM:(h + 1) * HEAD_DIM].astype(BF16), ones]
    return jnp.concatenate(parts, axis=-1)


def _attend_heads(jobs):
    for job in jobs:
        q = (job['q'] * (ATT_SCALE * LOG2_E)).astype(BF16)
        job['s'] = lax.dot_general(q, job['k'], _NT, preferred_element_type=F32)
        if job.get('extra') is not None:
            job['s2'] = lax.dot_general(q, job['extra'][0], _NT, preferred_element_type=F32)
    outs = []
    for job in jobs:
        s, sink = job['s'], job.get('sink')
        if sink is not None:
            sink = sink * LOG2_E
        if job.get('mask') is not None:
            s = jnp.where(job['mask'], s, NEG_INF)
        m = jnp.max(s, axis=-1, keepdims=True)
        if 's2' in job:
            m = jnp.maximum(m, jnp.max(job['s2'], axis=-1, keepdims=True))
        if sink is not None:
            m = jnp.maximum(m, sink)
        o = jnp.dot(jnp.exp2(s - m).astype(BF16), job['v'], preferred_element_type=F32)
        if 's2' in job:
            o = o + jnp.dot(jnp.exp2(job['s2'] - m).astype(BF16), job['extra'][1], preferred_element_type=F32)
        den = o[:, HEAD_DIM:HEAD_DIM + 1]
        if sink is not None:
            den = den + jnp.exp2(sink - m)
        outs.append(o[:, :HEAD_DIM] / den)
    return outs


def _ctx_attn_kernel(sink_ref, z_ref, gq_ref, gk_ref, o_ref, bk_ref):
    z = z_ref[...]
    bq = _head_rms(z[:, 512:768], gq_ref[...])
    bk = _head_rms(z[:, 768:896], gk_ref[...])
    bk_ref[...] = bk
    groups = ((z[:, 0:256], z[:, 256:384], z[:, 384:512], True),
              (bq, bk, z[:, 896:1024], False))
    outs = []
    for q_all, k_all, v_all, use_sink in groups:
        k_all = k_all.astype(BF16)
        v_all = _with_ones(v_all)
        for hq in range(A_HEADS):
            kv = hq // (A_HEADS // A_KV_HEADS)
            outs.append(dict(q=q_all[:, hq * HEAD_DIM:(hq + 1) * HEAD_DIM],
                             k=k_all[:, kv * HEAD_DIM:(kv + 1) * HEAD_DIM],
                             v=v_all[:, kv * 2 * HEAD_DIM:(kv + 1) * 2 * HEAD_DIM],
                             sink=sink_ref[hq] if use_sink else None))
    o_ref[...] = jnp.concatenate(_attend_heads(outs), axis=-1).astype(o_ref.dtype)


def ctx_attention(z, sink, gq, gk):
    return pl.pallas_call(
        _ctx_attn_kernel,
        grid=(BATCH,),
        in_specs=[pl.BlockSpec(memory_space=pltpu.SMEM),
                  pl.BlockSpec((SEQ, 1024), lambda b: (b, 0)),
                  pl.BlockSpec((1, HEAD_DIM), lambda b: (0, 0)),
                  pl.BlockSpec((1, HEAD_DIM), lambda b: (0, 0))],
        out_specs=[pl.BlockSpec((SEQ, 512), lambda b: (b, 0)),
                   pl.BlockSpec((SEQ, 128), lambda b: (b, 0))],
        out_shape=[jax.ShapeDtypeStruct((CTX_TOK, 512), BF16),
                   jax.ShapeDtypeStruct((CTX_TOK, 128), F32)],
        compiler_params=_params(("arbitrary",)),
        name="ctx_attention",
    )(sink, z, gq.reshape(1, HEAD_DIM), gk.reshape(1, HEAD_DIM))


LB_TQ = 256


def _lat_b_kernel(zq_ref, zkv_ref, ck_ref, cv_ref, cq_ref, sq_ref, ckk_ref, skk_ref, gq_ref, gk_ref,
                  o_ref, k_scr, v_scr):
    L = zkv_ref.shape[0]

    @pl.when(pl.program_id(1) == 0)
    def _():
        kv = zkv_ref[...]
        bk = _rope_apply(_head_rms(kv[:, :128], gk_ref[...]), ckk_ref[...], skk_ref[...])
        k_scr[0:L, :] = bk.astype(BF16)
        k_scr[L:L + PAST_LEN, :] = ck_ref[...].astype(BF16)
        v_scr[0:L, :] = _with_ones(kv[:, 128:])
        v_scr[L:L + PAST_LEN, :] = _with_ones(cv_ref[...])

    q = _rope_apply(_head_rms(zq_ref[...], gq_ref[...]), cq_ref[...], sq_ref[...])
    outs = []
    for hq in range(B_HEADS):
        kv = hq // (B_HEADS // B_KV_HEADS)
        sl = slice(kv * HEAD_DIM, (kv + 1) * HEAD_DIM)
        outs.append(dict(q=q[:, hq * HEAD_DIM:(hq + 1) * HEAD_DIM], k=k_scr[:, sl],
                         v=v_scr[:, kv * 2 * HEAD_DIM:(kv + 1) * 2 * HEAD_DIM]))
    o_ref[...] = jnp.concatenate(_attend_heads(outs), axis=-1).astype(o_ref.dtype)


def latent_attention_b(z, row0, nb, L, cache_k, cache_v, rope_c, rope_s, gq, gk):
    nq = L // LB_TQ
    return pl.pallas_call(
        _lat_b_kernel,
        grid=(nb, nq),
        in_specs=[pl.BlockSpec((LB_TQ, 256), lambda b, i: (row0 // LB_TQ + b * nq + i, 2)),
                  pl.BlockSpec((L, 256), lambda b, i: (row0 // L + b, 3)),
                  pl.BlockSpec((None, PAST_LEN, 128), lambda b, i: (b, 0, 0)),
                  pl.BlockSpec((None, PAST_LEN, 128), lambda b, i: (b, 0, 0)),
                  pl.BlockSpec((LB_TQ, 256), lambda b, i: (i, 0)),
                  pl.BlockSpec((LB_TQ, 256), lambda b, i: (i, 0)),
                  pl.BlockSpec((L, 128), lambda b, i: (0, 0)),
                  pl.BlockSpec((L, 128), lambda b, i: (0, 0)),
                  pl.BlockSpec((1, HEAD_DIM), lambda b, i: (0, 0)),
                  pl.BlockSpec((1, HEAD_DIM), lambda b, i: (0, 0))],
        out_specs=pl.BlockSpec((LB_TQ, 256), lambda b, i: (b * nq + i, 0)),
        out_shape=jax.ShapeDtypeStruct((nb * L, 256), BF16),
        scratch_shapes=[pltpu.VMEM((L + PAST_LEN, 128), BF16),
                        pltpu.VMEM((L + PAST_LEN, 256), BF16)],
        compiler_params=_params(("arbitrary", "arbitrary")),
        name="latent_attention_b",
    )(z, z, cache_k, cache_v, rope_c, rope_s, rope_c, rope_s, gq.reshape(1, HEAD_DIM), gk.reshape(1, HEAD_DIM))


def _lat_a_kernel(sink_ref, zq_ref, zkv_ref, ck_ref, cv_ref, cq_ref, sq_ref, ckk_ref, skk_ref,
                  o_ref, k_scr, v_scr, ck_scr, cv_scr):
    L = zkv_ref.shape[0]
    i = pl.program_id(1)

    @pl.when(i == 0)
    def _():
        kv = zkv_ref[...]
        k_scr[0:BLOCK, :] = jnp.zeros((BLOCK, 128), BF16)
        v_scr[0:BLOCK, :] = jnp.zeros((BLOCK, 256), BF16)
        k_scr[BLOCK:BLOCK + L, :] = _rope_apply(kv[:, :128], ckk_ref[...], skk_ref[...]).astype(BF16)
        v_scr[BLOCK:BLOCK + L, :] = _with_ones(kv[:, 128:])
        k_scr[BLOCK + L:2 * BLOCK + L, :] = jnp.zeros((BLOCK, 128), BF16)
        v_scr[BLOCK + L:2 * BLOCK + L, :] = jnp.zeros((BLOCK, 256), BF16)
        ck_scr[...] = ck_ref[...].astype(BF16)
        cv_scr[...] = _with_ones(cv_ref[...])

    q = _rope_apply(zq_ref[...], cq_ref[...], sq_ref[...])
    start = pl.multiple_of(i * BLOCK, BLOCK)
    kband = k_scr[pl.ds(start, 3 * BLOCK), :]
    vband = v_scr[pl.ds(start, 3 * BLOCK), :]
    r = lax.broadcasted_iota(jnp.int32, (BLOCK, 3 * BLOCK), 0)
    cidx = lax.broadcasted_iota(jnp.int32, (BLOCK, 3 * BLOCK), 1)
    kpos = i * BLOCK - BLOCK + cidx
    mask = (jnp.abs(cidx - BLOCK - r) <= WINDOW) & (kpos >= 0) & (kpos < L)
    outs = []
    for hq in range(A_HEADS):
        kv = hq // (A_HEADS // A_KV_HEADS)
        sl = slice(kv * HEAD_DIM, (kv + 1) * HEAD_DIM)
        sv = slice(kv * 2 * HEAD_DIM, (kv + 1) * 2 * HEAD_DIM)
        outs.append(dict(q=q[:, hq * HEAD_DIM:(hq + 1) * HEAD_DIM], k=kband[:, sl], v=vband[:, sv],
                         extra=(ck_scr[:, sl], cv_scr[:, sv]), sink=sink_ref[hq], mask=mask))
    o_ref[...] = jnp.concatenate(_attend_heads(outs), axis=-1).astype(o_ref.dtype)


def latent_attention_a(z, row0, nb, L, sink, cache_k, cache_v, rope_c, rope_s):
    nq = L // BLOCK
    return pl.pallas_call(
        _lat_a_kernel,
        grid=(nb, nq),
        in_specs=[pl.BlockSpec(memory_space=pltpu.SMEM),
                  pl.BlockSpec((BLOCK, 256), lambda b, i: (row0 // BLOCK + b * nq + i, 0)),
                  pl.BlockSpec((L, 256), lambda b, i: (row0 // L + b, 1)),
                  pl.BlockSpec((None, PAST_LEN, 128), lambda b, i: (b, 0, 0)),
                  pl.BlockSpec((None, PAST_LEN, 128), lambda b, i: (b, 0, 0)),
                  pl.BlockSpec((BLOCK, 256), lambda b, i: (i, 0)),
                  pl.BlockSpec((BLOCK, 256), lambda b, i: (i, 0)),
                  pl.BlockSpec((L, 128), lambda b, i: (0, 0)),
                  pl.BlockSpec((L, 128), lambda b, i: (0, 0))],
        out_specs=pl.BlockSpec((BLOCK, 256), lambda b, i: (b * nq + i, 0)),
        out_shape=jax.ShapeDtypeStruct((nb * L, 256), BF16),
        scratch_shapes=[pltpu.VMEM((L + 2 * BLOCK, 128), BF16),
                        pltpu.VMEM((L + 2 * BLOCK, 256), BF16),
                        pltpu.VMEM((PAST_LEN, 128), BF16),
                        pltpu.VMEM((PAST_LEN, 256), BF16)],
        compiler_params=_params(("arbitrary", "arbitrary")),
        name="latent_attention_a",
    )(sink, z, z, cache_k, cache_v, rope_c, rope_s, rope_c, rope_s)


_TN = (((0,), (0,)), ((), ()))
HG_GROUP = 8


def _hgrn_kernel(zq_ref, zf_ref, zi_ref, zg_ref, lb_ref, gn_ref, s0_ref, o_ref, sT_ref,
                 of_scr, ob_scr, g_scr, k_scr, qin_scr, kin_scr, v_scr, S_scr, *, tt):
    d = pl.program_id(1)
    j = pl.program_id(2)
    n_t = pl.num_programs(2)
    C = HGRN_CHUNK
    n_c = tt // C

    @pl.when(j == 0)
    def _():
        S_scr[...] = s0_ref[...]

    lb = lb_ref[...]
    sg = jax.nn.sigmoid(zf_ref[...])
    logf = jnp.log(lb + (1.0 - lb) * sg)
    k = (1.0 - lb) * (1.0 - sg)
    k_scr[...] = k
    v_scr[...] = zi_ref[...].astype(BF16)
    in_chunk = lax.broadcasted_iota(jnp.int32, (tt, C_HEADS * HEAD_DIM), 0) % C
    row = lax.broadcasted_iota(jnp.int32, (C, C), 0)
    col = lax.broadcasted_iota(jnp.int32, (C, C), 1)
    heads = [slice(h * HEAD_DIM, (h + 1) * HEAD_DIM) for h in range(C_HEADS)]

    def run(reverse, tile):
        G = logf
        step = 1
        while step < C:
            if reverse:
                G = G + jnp.where(in_chunk < C - step, pltpu.roll(G, tt - step, 0), 0.0)
            else:
                G = G + jnp.where(in_chunk >= step, pltpu.roll(G, step, 0), 0.0)
            step *= 2
        g_scr[...] = G
        qin_scr[...] = (zq_ref[...] * jnp.exp(G)).astype(BF16)
        kin_scr[...] = (k * jnp.exp(-G)).astype(BF16)
        tri = (row <= col) if reverse else (row >= col)

        def body(gi, carry):
            chunks = []
            for g in range(HG_GROUP):
                ci = gi * HG_GROUP + g
                c = (n_c - 1 - ci) if reverse else ci
                r0 = pl.multiple_of(c * C, C)
                rows = pl.ds(r0, C)
                G_c = g_scr[rows, :]
                G_end = G_c[0:1, :] if reverse else G_c[C - 1:C, :]
                vc = v_scr[rows, :]
                q_in = qin_scr[rows, :]
                k_in = kin_scr[rows, :]
                k_out = (k_scr[rows, :] * jnp.exp(G_end - G_c)).astype(BF16)
                chunks.append(dict(
                    r0=r0, rows=rows, vc=vc, q_in=q_in, decay=jnp.exp(G_end),
                    a=[lax.dot_general(q_in[:, sl], k_in[:, sl], _NT, preferred_element_type=F32) for sl in heads],
                    kv=[lax.dot_general(vc[:, sl], k_out[:, sl], _TN, preferred_element_type=F32) for sl in heads]))
            s_cur = [S_scr[h] for h in range(C_HEADS)]
            for ch in chunks:
                ch['qs'] = [lax.dot_general(ch['q_in'][:, sl], s_cur[h].astype(BF16), _NT,
                                            preferred_element_type=F32) for h, sl in enumerate(heads)]
                s_cur = [s_cur[h] * ch['decay'][:, sl] + ch['kv'][h] for h, sl in enumerate(heads)]
            for h in range(C_HEADS):
                S_scr[h] = s_cur[h]
            for ch in chunks:
                o_c = jnp.concatenate(
                    [jnp.dot(jnp.where(tri, ch['a'][h], 0.0).astype(BF16), ch['vc'][:, sl],
                             preferred_element_type=F32) + ch['qs'][h] for h, sl in enumerate(heads)], axis=-1)
                if reverse:
                    ob_scr[ch['rows'], :] = o_c
                else:
                    of_scr[pl.ds(pl.multiple_of(tile * tt, tt) + ch['r0'], C), :] = o_c
            return carry

        lax.fori_loop(0, n_c // HG_GROUP, body, 0)

    @pl.when(d == 0)
    def _():
        run(False, j)

    @pl.when(d == 1)
    def _():
        tile = n_t - 1 - j
        run(True, tile)
        o = of_scr[pl.ds(pl.multiple_of(tile * tt, tt), tt), :] + ob_scr[...]
        g = zg_ref[...]
        o_ref[...] = (_head_rms(o, gn_ref[...]) * (g * jax.nn.sigmoid(g))).astype(o_ref.dtype)

    @pl.when(j == n_t - 1)
    def _():
        sT_ref[...] = S_scr[...]


def hgrn_mixer(z, row0, nb, L, lb, gn, s0_t):
    tt = min(L, 512)
    n_t = L // tt
    rb = row0 // tt

    def tile(d, j):
        return jnp.where(d == 0, j, n_t - 1 - j)

    def late(d, j):
        return jnp.where(d == 0, n_t - 1, n_t - 1 - j)

    st_spec = pl.BlockSpec((None, None, C_HEADS, HEAD_DIM, HEAD_DIM), lambda b, d, j: (b, d, 0, 0, 0))
    return pl.pallas_call(
        functools.partial(_hgrn_kernel, tt=tt),
        grid=(nb, 2, n_t),
        in_specs=[pl.BlockSpec((tt, 256), lambda b, d, j: (rb + b * n_t + tile(d, j), 4)),
                  pl.BlockSpec((tt, 256), lambda b, d, j: (rb + b * n_t + tile(d, j), 5 + d)),
                  pl.BlockSpec((tt, 256), lambda b, d, j: (rb + b * n_t + tile(d, j), 7)),
                  pl.BlockSpec((tt, 256), lambda b, d, j: (rb + b * n_t + late(d, j), 8)),
                  pl.BlockSpec((1, 256), lambda b, d, j: (0, 0)),
                  pl.BlockSpec((1, HEAD_DIM), lambda b, d, j: (0, 0)),
                  st_spec],
        out_specs=[pl.BlockSpec((tt, 256), lambda b, d, j: (b * n_t + late(d, j), 0)),
                   st_spec],
        out_shape=[jax.ShapeDtypeStruct((nb * L, 256), BF16),
                   jax.ShapeDtypeStruct((nb, 2, C_HEADS, HEAD_DIM, HEAD_DIM), F32)],
        scratch_shapes=[pltpu.VMEM((L, 256), F32),
                        pltpu.VMEM((tt, 256), F32),
                        pltpu.VMEM((tt, 256), F32),
                        pltpu.VMEM((tt, 256), F32),
                        pltpu.VMEM((tt, 256), BF16),
                        pltpu.VMEM((tt, 256), BF16),
                        pltpu.VMEM((tt, 256), BF16),
                        pltpu.VMEM((C_HEADS, HEAD_DIM, HEAD_DIM), F32)],
        compiler_params=_params(("arbitrary", "arbitrary", "arbitrary")),
        name="hgrn_mixer",
    )(z, z, z, z, lb.reshape(1, 256), gn.reshape(1, HEAD_DIM), s0_t)


DL_C = DELTA_CHUNK
DL_PREP_TT = 256
DL_HALO = 8
DL_CHUNK_TT = 512
N_QKV_HEADS = 3 * D_HEADS
DL_GROUP = 2


def _delta_prep_kernel(x_ref, xp_ref, xn_ref, zab_ref, cw_ref, na_ref, dtb_ref, qkv_ref, gate_ref, xs_scr):
    tt = x_ref.shape[0]
    row = pl.program_id(0) * tt
    lat = row - CTX_TOK
    first = jnp.where(row < CTX_TOK, True, lat % DEC_SEQ == 0)
    last = jnp.where(row < CTX_TOK, True, (lat + tt) % DEC_SEQ == 0)
    xs_scr[DL_HALO:DL_HALO + tt, :] = x_ref[...]
    xs_scr[0:DL_HALO, :] = jnp.where(first, 0.0, xp_ref[...])
    xs_scr[DL_HALO + tt:2 * DL_HALO + tt, :] = jnp.where(last, 0.0, xn_ref[...])
    pad = (CONV_K - 1) // 2
    y = None
    for t in range(CONV_K):
        term = xs_scr[pl.ds(DL_HALO - pad + t, tt), :] * cw_ref[t:t + 1, :]
        y = term if y is None else y + term
    y = y * jax.nn.sigmoid(y)
    for idx in range(N_QKV_HEADS):
        xh = y[:, idx * HEAD_DIM:(idx + 1) * HEAD_DIM]
        if idx < 2 * D_HEADS:
            xh = xh * lax.rsqrt(jnp.sum(xh * xh, axis=-1, keepdims=True) + EPS)
        if idx < D_HEADS:
            xh = xh * ATT_SCALE
        qkv_ref[idx] = xh
    zab = zab_ref[...]
    lane = lax.broadcasted_iota(jnp.int32, zab.shape, 1)
    t_ = zab + dtb_ref[...]
    softplus = jnp.maximum(t_, 0.0) + jnp.log(1.0 + jnp.exp(-jnp.abs(t_)))
    gate_ref[...] = jnp.where(lane < 2 * D_HEADS, na_ref[...] * softplus, jax.nn.sigmoid(zab))


def delta_prep(z, conv_w, a_log, dt_bias):
    tt = DL_PREP_TT
    hb = tt // DL_HALO
    n_hb = N_TOK // DL_HALO
    pad8 = lambda v: jnp.concatenate([v.reshape(1, 2 * D_HEADS), jnp.zeros((1, 128 - 2 * D_HEADS), F32)], axis=1)
    return pl.pallas_call(
        _delta_prep_kernel,
        grid=(N_TOK // tt,),
        in_specs=[pl.BlockSpec((tt, 768), lambda i: (i, 3)),
                  pl.BlockSpec((DL_HALO, 768), lambda i: (jnp.maximum(i * hb - 1, 0), 3)),
                  pl.BlockSpec((DL_HALO, 768), lambda i: (jnp.minimum((i + 1) * hb, n_hb - 1), 3)),
                  pl.BlockSpec((tt, 128), lambda i: (i, Z_DAB // 128)),
                  pl.BlockSpec((CONV_K, 768), lambda i: (0, 0)),
                  pl.BlockSpec((1, 128), lambda i: (0, 0)),
                  pl.BlockSpec((1, 128), lambda i: (0, 0))],
        out_specs=[pl.BlockSpec((N_QKV_HEADS, tt, HEAD_DIM), lambda i: (0, i, 0)),
                   pl.BlockSpec((tt, 128), lambda i: (i, 0))],
        out_shape=[jax.ShapeDtypeStruct((N_QKV_HEADS, N_TOK, HEAD_DIM), F32),
                   jax.ShapeDtypeStruct((N_TOK, 128), F32)],
        scratch_shapes=[pltpu.VMEM((tt + 2 * DL_HALO, 768), F32)],
        compiler_params=_params(("arbitrary",)),
        name="delta_prep",
    )(z, z, z, z, conv_w, pad8(-jnp.exp(a_log)), pad8(dt_bias))


def _split_bf16(a):
    hi = a.astype(BF16)
    return hi, (a - hi.astype(F32)).astype(BF16)


def _dot_hl(a_parts, b_parts):
    (a_hi, a_lo), (b_hi, b_lo) = a_parts, b_parts
    m = a_hi.shape[0]
    r = jnp.dot(jnp.concatenate([a_hi, a_lo], axis=0), b_hi, preferred_element_type=F32)
    return r[:m] + r[m:] + jnp.dot(a_hi, b_lo, preferred_element_type=F32)


def _delta_chunk_kernel(qkv_ref, gate_ref, u2_ref, wq_ref, ak_ref):
    C = DL_C
    n_c = gate_ref.shape[0] // C
    row = lax.broadcasted_iota(jnp.int32, (C, C), 0)
    col = lax.broadcasted_iota(jnp.int32, (C, C), 1)
    eye = (row == col).astype(F32)
    t_idx = lax.broadcasted_iota(jnp.int32, (C, 128), 0)

    def chunk_chains(c):
        r0 = pl.multiple_of(c * C, C)
        ga = gate_ref[pl.ds(r0, C), :]
        chains = []
        for d in range(2):
            incl = (row >= col) if d == 0 else (row <= col)
            strict = (row > col) if d == 0 else (row < col)
            g_all = ga
            step = 1
            while step < C:
                if d == 0:
                    g_all = g_all + jnp.where(t_idx >= step, pltpu.roll(g_all, step, 0), 0.0)
                else:
                    g_all = g_all + jnp.where(t_idx < C - step, pltpu.roll(g_all, C - step, 0), 0.0)
                step *= 2
            g_all_t = g_all.T
            for h in range(D_HEADS):
                ci = d * D_HEADS + h
                q = qkv_ref[h, pl.ds(r0, C), :]
                k = qkv_ref[D_HEADS + h, pl.ds(r0, C), :]
                v = qkv_ref[2 * D_HEADS + h, pl.ds(r0, C), :]
                g_col = g_all[:, ci:ci + 1]
                g_row = g_all_t[ci:ci + 1, :]
                beta = ga[:, 2 * D_HEADS + ci:2 * D_HEADS + ci + 1]
                g_end = g_col[C - 1:C, :] if d == 0 else g_col[0:1, :]
                kb = k * beta
                eg = jnp.exp(g_col)
                decay = jnp.where(incl, jnp.exp(jnp.where(incl, g_col - g_row, 0.0)), 0.0)
                kq = jnp.concatenate([kb, q], axis=0).astype(BF16)
                chains.append(dict(
                    strict=strict, decay=decay, qg=q * eg, g_end=g_end,
                    r=lax.dot_general(kq, k.astype(BF16), _NT, preferred_element_type=F32),
                    rhs=jnp.concatenate([v * beta, kb * eg], axis=1).astype(BF16),
                    ke_t=(k * jnp.exp(g_end - g_col)).T))
        return r0, chains

    def body(gi, carry):
        groups = [(gi * DL_GROUP + cc,) + chunk_chains(gi * DL_GROUP + cc) for cc in range(DL_GROUP)]
        chains = [ch for _, _, chs in groups for ch in chs]
        for ch in chains:
            ch['p'] = -jnp.where(ch['strict'], ch['r'][:C] * ch['decay'], 0.0)
            ch['t'] = eye + ch['p']
        for _ in range(5):
            for ch in chains:
                parts = _split_bf16(ch['p'])
                ch['p'] = _dot_hl(parts, parts)
            for ch in chains:
                ch['t'] = ch['t'] + _dot_hl(_split_bf16(ch['t']), _split_bf16(ch['p']))
        for ch in chains:
            ch['uw'] = jnp.dot(ch['t'].astype(BF16), ch['rhs'], preferred_element_type=F32)
        pack = lambda xs: jnp.stack(xs).reshape((2, D_HEADS) + xs[0].shape)
        for c, r0, chs in groups:
            u2 = [jnp.concatenate([ch['uw'][:, :C], jnp.broadcast_to(jnp.exp(ch['g_end']), (C, C))], axis=1)
                  for ch in chs]
            wq = [jnp.concatenate([ch['uw'][:, C:], ch['qg']], axis=0).astype(BF16) for ch in chs]
            ak = [jnp.concatenate([ch['r'][C:] * ch['decay'], ch['ke_t']], axis=0).astype(BF16) for ch in chs]
            u2_ref[:, :, pl.ds(r0, C), :] = pack(u2)
            wq_ref[:, :, c] = pack(wq)
            ak_ref[:, :, c] = pack(ak)
        return carry

    lax.fori_loop(0, n_c // DL_GROUP, body, 0)


def delta_chunks(qkv, gates):
    tt = DL_CHUNK_TT
    n_c = tt // DL_C
    return pl.pallas_call(
        _delta_chunk_kernel,
        grid=(N_TOK // tt,),
        in_specs=[pl.BlockSpec((N_QKV_HEADS, tt, HEAD_DIM), lambda i: (0, i, 0)),
                  pl.BlockSpec((tt, 128), lambda i: (i, 0))],
        out_specs=[pl.BlockSpec((2, D_HEADS, tt, 128), lambda i: (0, 0, i, 0)),
                   pl.BlockSpec((2, D_HEADS, n_c, 2 * DL_C, HEAD_DIM), lambda i: (0, 0, i, 0, 0)),
                   pl.BlockSpec((2, D_HEADS, n_c, 2 * DL_C, HEAD_DIM), lambda i: (0, 0, i, 0, 0))],
        out_shape=[jax.ShapeDtypeStruct((2, D_HEADS, N_TOK, 128), F32),
                   jax.ShapeDtypeStruct((2, D_HEADS, N_TOK // DL_C, 2 * DL_C, HEAD_DIM), BF16),
                   jax.ShapeDtypeStruct((2, D_HEADS, N_TOK // DL_C, 2 * DL_C, HEAD_DIM), BF16)],
        compiler_params=_params(("arbitrary",)),
        name="delta_chunks",
    )(qkv, gates)


DS_SEQ = 2


def _delta_scan_kernel(*refs):
    in_refs, (s0_ref, of_ref, ob_ref, s_ref, s_scr) = refs[:6 * DS_SEQ], refs[6 * DS_SEQ:]
    j = pl.program_id(1)
    C = DL_C
    n_c = in_refs[1].shape[1]

    @pl.when(j == 0)
    def _():
        s_scr[...] = s0_ref[...]

    def body(ci, carry):
        chains = []
        for q in range(DS_SEQ):
            for d, o_ref in enumerate((of_ref, ob_ref)):
                u2_ref, wq_ref, ak_ref = in_refs[6 * q + 3 * d:6 * q + 3 * d + 3]
                c = ci if d == 0 else n_c - 1 - ci
                r0 = pl.multiple_of(c * C, C)
                for h in range(D_HEADS):
                    s = s_scr[q, d, h]
                    chains.append(dict(q=q, d=d, h=h, c=c, r0=r0, s=s, ak_ref=ak_ref, o_ref=o_ref,
                                       u2=u2_ref[h, pl.ds(r0, C), :],
                                       r1=jnp.dot(wq_ref[h, c], s.astype(BF16),
                                                  preferred_element_type=F32)))
        for ch in chains:
            v_new = ch['u2'][:, :C] - ch['r1'][:C]
            ch['r2'] = jnp.dot(ch['ak_ref'][ch['h'], ch['c']], v_new.astype(BF16),
                               preferred_element_type=F32)
        for ch in chains:
            ch['o_ref'][ch['h'], ch['q'], pl.ds(ch['r0'], C), :] = ch['r1'][C:] + ch['r2'][:C]
            s_scr[ch['q'], ch['d'], ch['h']] = ch['s'] * ch['u2'][0:1, C:] + ch['r2'][C:]
        return carry

    lax.fori_loop(0, n_c, body, 0)

    @pl.when(j == pl.num_programs(1) - 1)
    def _():
        s_ref[...] = s_scr[...]


def delta_scan(u2, wq, ak, row0, nb, L, s0):
    tt = min(L, 512)
    n_t = L // tt
    n_c = tt // DL_C
    rb = row0 // tt
    in_specs = []
    for q in range(DS_SEQ):
        fwd = lambda p, j, q=q: rb + (DS_SEQ * p + q) * n_t + j
        bwd = lambda p, j, q=q: rb + (DS_SEQ * p + q) * n_t + (n_t - 1 - j)
        for d, f in ((0, fwd), (1, bwd)):
            in_specs += [
                pl.BlockSpec((None, D_HEADS, tt, 128), lambda p, j, d=d, f=f: (d, 0, f(p, j), 0)),
                pl.BlockSpec((None, D_HEADS, n_c, 2 * DL_C, HEAD_DIM), lambda p, j, d=d, f=f: (d, 0, f(p, j), 0, 0)),
                pl.BlockSpec((None, D_HEADS, n_c, 2 * DL_C, HEAD_DIM), lambda p, j, d=d, f=f: (d, 0, f(p, j), 0, 0))]
    st_spec = pl.BlockSpec((DS_SEQ, 2, D_HEADS, HEAD_DIM, HEAD_DIM), lambda p, j: (p, 0, 0, 0, 0))
    o_shape = jax.ShapeDtypeStruct((D_HEADS, nb // DS_SEQ, DS_SEQ, L, HEAD_DIM), F32)
    o_f, o_b, s = pl.pallas_call(
        _delta_scan_kernel,
        grid=(nb // DS_SEQ, n_t),
        in_specs=in_specs + [st_spec],
        out_specs=[pl.BlockSpec((D_HEADS, None, DS_SEQ, tt, HEAD_DIM), lambda p, j: (0, p, 0, j, 0)),
                   pl.BlockSpec((D_HEADS, None, DS_SEQ, tt, HEAD_DIM), lambda p, j: (0, p, 0, n_t - 1 - j, 0)),
                   st_spec],
        out_shape=[o_shape, o_shape, jax.ShapeDtypeStruct((nb, 2, D_HEADS, HEAD_DIM, HEAD_DIM), F32)],
        scratch_shapes=[pltpu.VMEM((DS_SEQ, 2, D_HEADS, HEAD_DIM, HEAD_DIM), F32)],
        compiler_params=_params(("arbitrary", "arbitrary")),
        name="delta_scan",
    )(*([u2, wq, ak] * (2 * DS_SEQ)), s0)
    return o_f.reshape(D_HEADS, nb * L, HEAD_DIM), o_b.reshape(D_HEADS, nb * L, HEAD_DIM), s


def kernel(x_prompt, x_sample, cache_attn_a_k, cache_attn_a_v, cache_attn_b_k, cache_attn_b_v,
           state_hgrn, state_delta, c, c_ctx, norm1_g, norm2_g, w_ada, b_ada, w_in, a_sink,
           b_qnorm_g, b_knorm_g, c_lb, c_onorm_g, d_conv, d_a_log, d_dt_bias, d_onorm_g,
           w_branch, w_out, ffn_w1, ffn_w3, ffn_w2, router_w, router_b, moe_w1, moe_w3, moe_w2,
           final_norm_g):
    cum = jnp.cumsum(jax.nn.softmax(c_lb, axis=0), axis=0)
    lower_bounds = cum - cum[:1]

    xs = (x_prompt.reshape(CTX_TOK, D_MODEL), x_sample.reshape(LAT_TOK, D_MODEL), 0)
    cond = jnp.concatenate([c_ctx[None, :], c, jnp.zeros((16 - N_COND, D_MODEL), F32)], axis=0)

    rope_c, rope_s = rope_lane_tables(DEC_SEQ)
    caches = []
    for l in range(DEPTH):
        mod = ada_modulation(cond, w_ada[l], b_ada[l])[:N_COND].reshape(N_COND, 6, D_MODEL)
        w_mix = jnp.concatenate([w_in[l][:, :Z_MAIN], w_in[l][:, Z_MAIN + 16:W_IN_MIX],
                                 w_in[l][:, Z_MAIN:Z_MAIN + 16], jnp.zeros((D_MODEL, 128 - 16), F32)],
                                axis=1).astype(BF16)
        w_gl = w_in[l][:, W_IN_MIX:].reshape(D_MODEL, N_BRANCH, D_MODEL).transpose(1, 0, 2).astype(BF16)
        z = input_projection(xs, mod, norm1_g[l], w_mix)
        kv2 = lambda t: t.reshape(DEC_BATCH, PAST_LEN, 128)
        o_ab_ctx, bk_ctx = ctx_attention(z, a_sink[l], b_qnorm_g[l], b_knorm_g[l])
        o_a_lat = latent_attention_a(z, CTX_TOK, DEC_BATCH, DEC_SEQ, a_sink[l], kv2(cache_attn_a_k[:, l]),
                                     kv2(cache_attn_a_v[:, l]), rope_c, rope_s)
        o_b_lat = latent_attention_b(z, CTX_TOK, DEC_BATCH, DEC_SEQ, kv2(cache_attn_b_k[:, l]),
                                     kv2(cache_attn_b_v[:, l]), rope_c, rope_s, b_qnorm_g[l], b_knorm_g[l])
        o_c_ctx, sc_t = hgrn_mixer(z, 0, BATCH, SEQ, lower_bounds[l], c_onorm_g[l],
                                   jnp.zeros((BATCH, 2, C_HEADS, HEAD_DIM, HEAD_DIM), F32))
        o_c_lat, _ = hgrn_mixer(z, CTX_TOK, DEC_BATCH, DEC_SEQ, lower_bounds[l], c_onorm_g[l],
                                jnp.swapaxes(state_hgrn[:, l], -1, -2))

        qkv, gates = delta_prep(z, d_conv[l], d_a_log[l], d_dt_bias[l])
        u2, wq, ak = delta_chunks(qkv, gates)
        of_ctx, ob_ctx, sd = delta_scan(u2, wq, ak, 0, BATCH, SEQ,
                                        jnp.zeros((BATCH, 2, D_HEADS, HEAD_DIM, HEAD_DIM), F32))
        of_lat, ob_lat, _ = delta_scan(u2, wq, ak, CTX_TOK, DEC_BATCH, DEC_SEQ, state_delta[:, l])
        kvh = lambda t: t.reshape(BATCH, SEQ, 2, HEAD_DIM)
        caches.append((kvh(z[:CTX_TOK, 256:384]), kvh(z[:CTX_TOK, 384:512]), kvh(bk_ctx), kvh(z[:CTX_TOK, 896:1024]),
                       jnp.swapaxes(sc_t, -1, -2), sd))
        x = merge_projection(xs, mod, norm1_g[l], (o_ab_ctx, o_c_ctx, of_ctx, ob_ctx),
                             (o_a_lat, o_b_lat, o_c_lat, of_lat, ob_lat), z, d_onorm_g[l],
                             w_gl, w_branch[l].astype(BF16), w_out[l].astype(BF16))
        j = l // 2
        if l % 2 == 0:
            x = dense_ffn(x, mod, norm2_g[l], ffn_w1[j].astype(BF16), ffn_w3[j].astype(BF16),
                          ffn_w2[j].astype(BF16))
            xs = (x, x, CTX_TOK // TM)
        else:
            assert l == DEPTH - 1, "the expert layer's residual is fused with the final norm"
            rw = jnp.concatenate([router_w[j], jnp.zeros((D_MODEL, 128 - N_EXPERTS), F32)], axis=1)
            rb = jnp.concatenate([router_b[j], jnp.zeros((128 - N_EXPERTS,), F32)])[None, :]
            h2, pos, wgt = moe_router(x, mod, norm2_g[l], rw, rb)
            f = moe_experts(h2, pos, wgt, moe_w1[j].astype(BF16), moe_w3[j].astype(BF16), moe_w2[j].astype(BF16))
            y_prompt = residual_final_norm(x, f, mod, final_norm_g, 0, CTX_TOK).reshape(BATCH, SEQ, D_MODEL)
            y_sample = residual_final_norm(x, f, mod, final_norm_g, CTX_TOK, LAT_TOK).reshape(DEC_BATCH, DEC_SEQ, D_MODEL)

    stack = lambda idx: jnp.stack([caches[l][idx] for l in range(DEPTH)], axis=1)
    return (y_prompt, y_sample, stack(0), stack(1), stack(2), stack(3), stack(4), stack(5))
```
